```python
import math
import jax, jax.numpy as jnp
from jax import lax
import numpy as np

D_MODEL = 1024
BATCH = 16
SEQ = 2048
DEPTH = 4

N_MIXERS = 3
BLOCK = 128
ROPE_THETA = 10000.0
NORM_EPS = 1e-5
D_FF = 4 * D_MODEL

A_HEAD_DIM = 64
A_N_HEADS = D_MODEL // A_HEAD_DIM
A_N_KV_HEADS = A_N_HEADS // 8
A_WINDOW = 128
A_Q_DIM = A_N_HEADS * A_HEAD_DIM
A_KV_DIM = A_N_KV_HEADS * A_HEAD_DIM

SSM_D_INNER = 2 * D_MODEL
SSM_HEAD_DIM = 64
SSM_N_HEADS = SSM_D_INNER // SSM_HEAD_DIM
SSM_N_GROUPS = 8
SSM_HEADS_PER_GROUP = SSM_N_HEADS // SSM_N_GROUPS
SSM_D_STATE = 128
SSM_CONV = 4
SSM_CHUNK = 128
SSM_BC_DIM = SSM_N_GROUPS * SSM_D_STATE
SSM_CONV_DIM = SSM_D_INNER + 2 * SSM_BC_DIM
SSM_IN_DIM = SSM_D_INNER + SSM_CONV_DIM + SSM_N_HEADS

C_HEAD_DIM = 64
C_HEADS_PER_GROUP = D_MODEL // C_HEAD_DIM
C_PATTERNS = ((128, 1), (512, 4), (2048, 16))
C_N_GROUPS = len(C_PATTERNS)
C_QKV_DIM = 3 * C_N_GROUPS * C_HEADS_PER_GROUP * C_HEAD_DIM

N_A = (DEPTH + 2) // 3
N_B = (DEPTH + 1) // 3
N_C = DEPTH // 3

kernel_name = "hybrid_swa_mamba2_dilated_trunk"


def rmsnorm(x, w):
    xf = x.astype(jnp.float32)
    y = xf * lax.rsqrt(jnp.mean(xf * xf, axis=-1, keepdims=True) + NORM_EPS)
    return (y * w.astype(jnp.float32)).astype(x.dtype)


def rope(t, positions):
    half = t.shape[-1] // 2
    inv = ROPE_THETA ** (-jnp.arange(half, dtype=jnp.float32) / half)
    ang = positions.astype(jnp.float32)[..., None] * inv
    cos = jnp.cos(ang)[:, :, None, :]
    sin = jnp.sin(ang)[:, :, None, :]
    tf = t.astype(jnp.float32)
    t1, t2 = tf[..., :half], tf[..., half:]
    return jnp.concatenate([t1 * cos - t2 * sin, t2 * cos + t1 * sin], axis=-1).astype(t.dtype)


def banded_attention(q, k, v, max_dist, sinks=None):
    n, L, H, Dh = q.shape
    Hkv = k.shape[2]
    R = H // Hkv
    nb = -(-L // BLOCK)
    pad = nb * BLOCK - L
    padw = ((0, 0), (0, pad), (0, 0), (0, 0))
    qb = jnp.pad(q, padw).reshape(n, nb, BLOCK, Hkv, R, Dh)
    kb = jnp.pad(k, padw).reshape(n, nb, BLOCK, Hkv, Dh)
    vb = jnp.pad(v, padw).reshape(n, nb, BLOCK, Hkv, Dh)

    def with_prev(t):
        prev = jnp.pad(t, ((0, 0), (1, 0), (0, 0), (0, 0), (0, 0)))[:, :-1]
        return jnp.concatenate([prev, t], axis=2)

    kk, vv = with_prev(kb), with_prev(vb)
    s = jnp.einsum('nbqgrd,nbkgd->nbgrqk', qb, kk,
                   preferred_element_type=jnp.float32) * (Dh ** -0.5)
    blk = jnp.arange(nb)[:, None, None] * BLOCK
    qpos = blk + jnp.arange(BLOCK)[None, :, None]
    kpos = blk - BLOCK + jnp.arange(2 * BLOCK)[None, None, :]
    dist = qpos - kpos
    mask = (dist >= 0) & (dist <= max_dist) & (kpos >= 0)
    s = jnp.where(mask[None, :, None, None], s, -jnp.inf)
    m = jnp.max(s, axis=-1)
    if sinks is not None:
        sk = sinks.astype(jnp.float32).reshape(Hkv, R)[None, None, :, :, None]
        m = jnp.maximum(m, sk)
    p = jnp.exp(s - m[..., None])
    denom = jnp.sum(p, axis=-1)
    if sinks is not None:
        denom = denom + jnp.exp(sk - m)
    lse = m + jnp.log(denom)
    o = jnp.einsum('nbgrqk,nbkgd->nbqgrd', (p / denom[..., None]).astype(v.dtype), vv)
    o = o.reshape(n, nb * BLOCK, H, Dh)[:, :L]
    lse = jnp.transpose(lse, (0, 1, 4, 2, 3)).reshape(n, nb * BLOCK, H)[:, :L]
    return o, lse


def swa_sink_mixer(h, positions, w_qkv, b_qkv, sinks, w_o, b_o):
    B, S, _ = h.shape
    qkv = h @ w_qkv + b_qkv
    q = qkv[..., :A_Q_DIM].reshape(B, S, A_N_HEADS, A_HEAD_DIM)
    k = qkv[..., A_Q_DIM:A_Q_DIM + A_KV_DIM].reshape(B, S, A_N_KV_HEADS, A_HEAD_DIM)
    v = qkv[..., A_Q_DIM + A_KV_DIM:].reshape(B, S, A_N_KV_HEADS, A_HEAD_DIM)
    q, k = rope(q, positions), rope(k, positions)
    o, _ = banded_attention(q, k, v, A_WINDOW - 1, sinks)
    return o.reshape(B, S, A_Q_DIM) @ w_o + b_o


def dilated_attention(q, k, v, window, dilation):
    B, S, H, Dh = q.shape
    Ls = S // dilation

    def to_sub(t):
        return t.reshape(B, Ls, dilation, H, Dh).transpose(0, 2, 1, 3, 4).reshape(B * dilation, Ls, H, Dh)

    o, lse = banded_attention(to_sub(q), to_sub(k), to_sub(v), window // dilation)
    o = o.reshape(B, dilation, Ls, H, Dh).transpose(0, 2, 1, 3, 4).reshape(B, S, H, Dh)
    lse = lse.reshape(B, dilation, Ls, H).transpose(0, 2, 1, 3).reshape(B, S, H)
    return o, lse


def dilated_mixer(h, positions, w_qkv, w_o):
    B, S, _ = h.shape
    GH = C_N_GROUPS * C_HEADS_PER_GROUP
    qkv = (h @ w_qkv).reshape(B, S, 3, GH, C_HEAD_DIM)
    q = rope(qkv[:, :, 0], positions).reshape(B, S, C_N_GROUPS, C_HEADS_PER_GROUP, C_HEAD_DIM)
    k = rope(qkv[:, :, 1], positions).reshape(B, S, C_N_GROUPS, C_HEADS_PER_GROUP, C_HEAD_DIM)
    v = qkv[:, :, 2].reshape(B, S, C_N_GROUPS, C_HEADS_PER_GROUP, C_HEAD_DIM)
    outs, lses = [], []
    for g, (window, dilation) in enumerate(C_PATTERNS):
        o, lse = dilated_attention(q[:, :, g], k[:, :, g], v[:, :, g], window, dilation)
        outs.append(o.astype(jnp.float32))
        lses.append(lse)
    wts = jax.nn.softmax(jnp.stack(lses, axis=0), axis=0)
    o = jnp.sum(wts[..., None] * jnp.stack(outs, axis=0), axis=0).astype(h.dtype)
    return o.reshape(B, S, C_HEADS_PER_GROUP * C_HEAD_DIM) @ w_o


def causal_depthwise_conv(x, w, b):
    C = x.shape[-1]
    y = lax.conv_general_dilated(x, w[:, None, :].astype(x.dtype), window_strides=(1,),
                                 padding=((SSM_CONV - 1, 0),),
                                 dimension_numbers=('NWC', 'WIO', 'NWC'),
                                 feature_group_count=C)
    return y + b


def ssd_chunked(xh, dt, A, Bm, Cm):
    b, S, H, P = xh.shape
    G, N, HG, Q = SSM_N_GROUPS, SSM_D_STATE, SSM_HEADS_PER_GROUP, SSM_CHUNK
    nc = S // Q
    x = (xh * dt[..., None]).reshape(b, nc, Q, G, HG, P)
    dA = (dt * A).reshape(b, nc, Q, G, HG).transpose(0, 1, 3, 4, 2)
    cs = jnp.cumsum(dA, axis=-1)
    Bc = Bm.reshape(b, nc, Q, G, N)
    Cc = Cm.reshape(b, nc, Q, G, N)
    tril = jnp.arange(Q)[:, None] >= jnp.arange(Q)[None, :]
    Lmat = jnp.exp(jnp.where(tril, cs[..., :, None] - cs[..., None, :], -jnp.inf))
    CB = jnp.einsum('bclgn,bcsgn->bcgls', Cc, Bc)
    y_diag = jnp.einsum('bcghls,bcsghp->bclghp', CB[:, :, :, None] * Lmat, x)
    decay_s = jnp.exp(cs[..., -1:] - cs)
    states = jnp.einsum('bclgn,bcghl,bclghp->bcghpn', Bc, decay_s, x)
    chunk_decay = jnp.exp(cs[..., -1])

    def step(state, inp):
        st, dec = inp
        return state * dec[..., None, None] + st, state

    init = jnp.zeros((b, G, HG, P, N), jnp.float32)
    _, prev = lax.scan(step, init, (jnp.moveaxis(states, 1, 0), jnp.moveaxis(chunk_decay, 1, 0)))
    prev = jnp.moveaxis(prev, 0, 1)
    y_off = jnp.einsum('bclgn,bcghpn,bcghl->bclghp', Cc, prev, jnp.exp(cs))
    return (y_diag + y_off).reshape(b, S, H, P)


def mamba2_mixer(h, in_w, conv_w, conv_b, dt_bias, a_log, d_skip, norm_w, out_w):
    B, S, _ = h.shape
    zxbcdt = h @ in_w
    z = zxbcdt[..., :SSM_D_INNER]
    xbc = zxbcdt[..., SSM_D_INNER:SSM_D_INNER + SSM_CONV_DIM]
    dt = zxbcdt[..., SSM_D_INNER + SSM_CONV_DIM:]
    xbc = jax.nn.silu(causal_depthwise_conv(xbc, conv_w, conv_b)).astype(jnp.float32)
    xs = xbc[..., :SSM_D_INNER].reshape(B, S, SSM_N_HEADS, SSM_HEAD_DIM)
    Bm = xbc[..., SSM_D_INNER:SSM_D_INNER + SSM_BC_DIM].reshape(B, S, SSM_N_GROUPS, SSM_D_STATE)
    Cm = xbc[..., SSM_D_INNER + SSM_BC_DIM:].reshape(B, S, SSM_N_GROUPS, SSM_D_STATE)
    dt = jax.nn.softplus(dt.astype(jnp.float32) + dt_bias.astype(jnp.float32))
    A = -jnp.exp(a_log.astype(jnp.float32))
    y = ssd_chunked(xs, dt, A, Bm, Cm) + d_skip.astype(jnp.float32)[:, None] * xs
    g = (y.reshape(B, S, SSM_D_INNER) * jax.nn.silu(z.astype(jnp.float32)))
    g = g.reshape(B, S, SSM_N_GROUPS, SSM_D_INNER // SSM_N_GROUPS)
    g = g * lax.rsqrt(jnp.mean(g * g, axis=-1, keepdims=True) + NORM_EPS)
    g = g.reshape(B, S, SSM_D_INNER) * norm_w.astype(jnp.float32)
    return g.astype(h.dtype) @ out_w


def sqrelu_mlp(h, w_up, w_down):
    u = jax.nn.relu(h @ w_up)
    return (u * u) @ w_down


def _fwd_setup_inputs(seed: int = 0) -> dict:
    key = jax.random.key(seed)
    ks = jax.random.split(key, 24)
    f32 = jnp.float32
    res_scale = (2.0 * DEPTH) ** -0.5

    def nrm(k, shape, scale):
        return jax.random.normal(k, shape, f32) * scale

    x = jax.random.normal(ks[0], (BATCH, SEQ, D_MODEL), f32)
    offs = jax.random.randint(ks[1], (BATCH, 1), 0, 4096, dtype=jnp.int32)
    positions = offs + jnp.arange(SEQ, dtype=jnp.int32)[None, :]
    norm_mix_w = 1.0 + nrm(ks[2], (DEPTH, D_MODEL), 0.02)
    norm_mlp_w = 1.0 + nrm(ks[3], (DEPTH, D_MODEL), 0.02)
    a_w_qkv = nrm(ks[4], (N_A, D_MODEL, A_Q_DIM + 2 * A_KV_DIM), D_MODEL ** -0.5)
    a_b_qkv = nrm(ks[5], (N_A, A_Q_DIM + 2 * A_KV_DIM), 0.02)
    a_sinks = nrm(ks[6], (N_A, A_N_HEADS), 0.5)
    a_w_o = nrm(ks[7], (N_A, A_Q_DIM, D_MODEL), A_Q_DIM ** -0.5 * res_scale)
    a_b_o = nrm(ks[8], (N_A, D_MODEL), 0.02)
    b_in_w = nrm(ks[9], (N_B, D_MODEL, SSM_IN_DIM), D_MODEL ** -0.5)
    b_conv_w = nrm(ks[10], (N_B, SSM_CONV, SSM_CONV_DIM), SSM_CONV ** -0.5)
    b_conv_b = nrm(ks[11], (N_B, SSM_CONV_DIM), 0.02)
    dt0 = jnp.exp(jax.random.uniform(ks[12], (N_B, SSM_N_HEADS), f32, math.log(1e-3), math.log(1e-1)))
    b_dt_bias = dt0 + jnp.log(-jnp.expm1(-dt0))
    b_a_log = jnp.log(jax.random.uniform(ks[13], (N_B, SSM_N_HEADS), f32, 1.0, 16.0))
    b_d = 1.0 + nrm(ks[14], (N_B, SSM_N_HEADS), 0.02)
    b_norm_w = 1.0 + nrm(ks[15], (N_B, SSM_D_INNER), 0.02)
    b_out_w = nrm(ks[16], (N_B, SSM_D_INNER, D_MODEL), SSM_D_INNER ** -0.5 * res_scale)
    c_w_qkv = nrm(ks[17], (N_C, D_MODEL, C_QKV_DIM), D_MODEL ** -0.5)
    c_w_o = nrm(ks[18], (N_C, C_HEADS_PER_GROUP * C_HEAD_DIM, D_MODEL),
                (C_HEADS_PER_GROUP * C_HEAD_DIM) ** -0.5 * res_scale)
    mlp_w_up = nrm(ks[19], (DEPTH, D_MODEL, D_FF), D_MODEL ** -0.5)
    mlp_w_down = nrm(ks[20], (DEPTH, D_FF, D_MODEL), D_FF ** -0.5 * res_scale)
    final_norm_w = 1.0 + nrm(ks[21], (D_MODEL,), 0.02)
    return {"x": x, "positions": positions, "norm_mix_w": norm_mix_w, "norm_mlp_w": norm_mlp_w,
            "a_w_qkv": a_w_qkv, "a_b_qkv": a_b_qkv, "a_sinks": a_sinks, "a_w_o": a_w_o, "a_b_o": a_b_o,
            "b_in_w": b_in_w, "b_conv_w": b_conv_w, "b_conv_b": b_conv_b, "b_dt_bias": b_dt_bias,
            "b_a_log": b_a_log, "b_d": b_d, "b_norm_w": b_norm_w, "b_out_w": b_out_w,
            "c_w_qkv": c_w_qkv, "c_w_o": c_w_o, "mlp_w_up": mlp_w_up, "mlp_w_down": mlp_w_down,
            "final_norm_w": final_norm_w}


def _fwd_reference(x, positions, norm_mix_w, norm_mlp_w, a_w_qkv, a_b_qkv, a_sinks, a_w_o, a_b_o,
              b_in_w, b_conv_w, b_conv_b, b_dt_bias, b_a_log, b_d, b_norm_w, b_out_w,
              c_w_qkv, c_w_o, mlp_w_up, mlp_w_down, final_norm_w):
    h = x
    for i in range(DEPTH):
        kind, j = i % N_MIXERS, i // N_MIXERS
        u = rmsnorm(h, norm_mix_w[i])
        if kind == 0:
            mix = swa_sink_mixer(u, positions, a_w_qkv[j], a_b_qkv[j], a_sinks[j], a_w_o[j], a_b_o[j])
        elif kind == 1:
            mix = mamba2_mixer(u, b_in_w[j], b_conv_w[j], b_conv_b[j], b_dt_bias[j], b_a_log[j],
                               b_d[j], b_norm_w[j], b_out_w[j])
        else:
            mix = dilated_mixer(u, positions, c_w_qkv[j], c_w_o[j])
        h = h + mix
        u = rmsnorm(h, norm_mlp_w[i])
        h = h + sqrelu_mlp(u, mlp_w_up[i], mlp_w_down[i])
    return rmsnorm(h, final_norm_w)


import jax as _jax
import jax.numpy as _jnp

TWIN_FORMAT = 'train_step'
FWD_PARAMS = ['x', 'positions', 'norm_mix_w', 'norm_mlp_w', 'a_w_qkv', 'a_b_qkv', 'a_sinks', 'a_w_o', 'a_b_o', 'b_in_w', 'b_conv_w', 'b_conv_b', 'b_dt_bias', 'b_a_log', 'b_d', 'b_norm_w', 'b_out_w', 'c_w_qkv', 'c_w_o', 'mlp_w_up', 'mlp_w_down', 'final_norm_w']
TWIN_WEIGHTS = ['norm_mix_w', 'norm_mlp_w', 'a_w_qkv', 'a_b_qkv', 'a_sinks', 'a_w_o', 'a_b_o', 'b_in_w', 'b_conv_w', 'b_conv_b', 'b_dt_bias', 'b_a_log', 'b_d', 'b_norm_w', 'b_out_w', 'c_w_qkv', 'c_w_o', 'mlp_w_up', 'mlp_w_down', 'final_norm_w']
TWIN_DIFF_INPUT = 'x'
TWIN_INPUTS = ['x', 'positions', 'norm_mix_w', 'norm_mlp_w', 'a_w_qkv', 'a_b_qkv', 'a_sinks', 'a_w_o', 'a_b_o', 'b_in_w', 'b_conv_w', 'b_conv_b', 'b_dt_bias', 'b_a_log', 'b_d', 'b_norm_w', 'b_out_w', 'c_w_qkv', 'c_w_o', 'mlp_w_up', 'mlp_w_down', 'final_norm_w', 'loss_target', 'm_norm_mix_w', 'm_norm_mlp_w', 'm_a_w_qkv', 'm_a_b_qkv', 'm_a_sinks', 'm_a_w_o', 'm_a_b_o', 'm_b_in_w', 'm_b_conv_w', 'm_b_conv_b', 'm_b_dt_bias', 'm_b_a_log', 'm_b_d', 'm_b_norm_w', 'm_b_out_w', 'm_c_w_qkv', 'm_c_w_o', 'm_mlp_w_up', 'm_mlp_w_down', 'm_final_norm_w', 'v_norm_mix_w', 'v_norm_mlp_w', 'v_a_w_qkv', 'v_a_b_qkv', 'v_a_sinks', 'v_a_w_o', 'v_a_b_o', 'v_b_in_w', 'v_b_conv_w', 'v_b_conv_b', 'v_b_dt_bias', 'v_b_a_log', 'v_b_d', 'v_b_norm_w', 'v_b_out_w', 'v_c_w_qkv', 'v_c_w_o', 'v_mlp_w_up', 'v_mlp_w_down', 'v_final_norm_w']
TWIN_OUTPUTS = ['loss', 'grad_x', 'grad_norm_mix_w', 'grad_norm_mlp_w', 'grad_a_w_qkv', 'grad_a_b_qkv', 'grad_a_sinks', 'grad_a_w_o', 'grad_a_b_o', 'grad_b_in_w', 'grad_b_conv_w', 'grad_b_conv_b', 'grad_b_dt_bias', 'grad_b_a_log', 'grad_b_d', 'grad_b_norm_w', 'grad_b_out_w', 'grad_c_w_qkv', 'grad_c_w_o', 'grad_mlp_w_up', 'grad_mlp_w_down', 'grad_final_norm_w', 'delta_norm_mix_w', 'delta_norm_mlp_w', 'delta_a_w_qkv', 'delta_a_b_qkv', 'delta_a_sinks', 'delta_a_w_o', 'delta_a_b_o', 'delta_b_in_w', 'delta_b_conv_w', 'delta_b_conv_b', 'delta_b_dt_bias', 'delta_b_a_log', 'delta_b_d', 'delta_b_norm_w', 'delta_b_out_w', 'delta_c_w_qkv', 'delta_c_w_o', 'delta_mlp_w_up', 'delta_mlp_w_down', 'delta_final_norm_w', 'new_m_norm_mix_w', 'new_m_norm_mlp_w', 'new_m_a_w_qkv', 'new_m_a_b_qkv', 'new_m_a_sinks', 'new_m_a_w_o', 'new_m_a_b_o', 'new_m_b_in_w', 'new_m_b_conv_w', 'new_m_b_conv_b', 'new_m_b_dt_bias', 'new_m_b_a_log', 'new_m_b_d', 'new_m_b_norm_w', 'new_m_b_out_w', 'new_m_c_w_qkv', 'new_m_c_w_o', 'new_m_mlp_w_up', 'new_m_mlp_w_down', 'new_m_final_norm_w', 'new_v_norm_mix_w', 'new_v_norm_mlp_w', 'new_v_a_w_qkv', 'new_v_a_b_qkv', 'new_v_a_sinks', 'new_v_a_w_o', 'new_v_a_b_o', 'new_v_b_in_w', 'new_v_b_conv_w', 'new_v_b_conv_b', 'new_v_b_dt_bias', 'new_v_b_a_log', 'new_v_b_d', 'new_v_b_norm_w', 'new_v_b_out_w', 'new_v_c_w_qkv', 'new_v_c_w_o', 'new_v_mlp_w_up', 'new_v_mlp_w_down', 'new_v_final_norm_w']
TWIN_LEAF_KINDS = {'loss': 'loss', 'grad_x': 'grad_x', 'grad_norm_mix_w': 'grad_w', 'grad_norm_mlp_w': 'grad_w', 'grad_a_w_qkv': 'grad_w', 'grad_a_b_qkv': 'grad_w', 'grad_a_sinks': 'grad_w', 'grad_a_w_o': 'grad_w', 'grad_a_b_o': 'grad_w', 'grad_b_in_w': 'grad_w', 'grad_b_conv_w': 'grad_w', 'grad_b_conv_b': 'grad_w', 'grad_b_dt_bias': 'grad_w', 'grad_b_a_log': 'grad_w', 'grad_b_d': 'grad_w', 'grad_b_norm_w': 'grad_w', 'grad_b_out_w': 'grad_w', 'grad_c_w_qkv': 'grad_w', 'grad_c_w_o': 'grad_w', 'grad_mlp_w_up': 'grad_w', 'grad_mlp_w_down': 'grad_w', 'grad_final_norm_w': 'grad_w', 'delta_norm_mix_w': 'delta_w', 'delta_norm_mlp_w': 'delta_w', 'delta_a_w_qkv': 'delta_w', 'delta_a_b_qkv': 'delta_w', 'delta_a_sinks': 'delta_w', 'delta_a_w_o': 'delta_w', 'delta_a_b_o': 'delta_w', 'delta_b_in_w': 'delta_w', 'delta_b_conv_w': 'delta_w', 'delta_b_conv_b': 'delta_w', 'delta_b_dt_bias': 'delta_w', 'delta_b_a_log': 'delta_w', 'delta_b_d': 'delta_w', 'delta_b_norm_w': 'delta_w', 'delta_b_out_w': 'delta_w', 'delta_c_w_qkv': 'delta_w', 'delta_c_w_o': 'delta_w', 'delta_mlp_w_up': 'delta_w', 'delta_mlp_w_down': 'delta_w', 'delta_final_norm_w': 'delta_w', 'new_m_norm_mix_w': 'new_m', 'new_m_norm_mlp_w': 'new_m', 'new_m_a_w_qkv': 'new_m', 'new_m_a_b_qkv': 'new_m', 'new_m_a_sinks': 'new_m', 'new_m_a_w_o': 'new_m', 'new_m_a_b_o': 'new_m', 'new_m_b_in_w': 'new_m', 'new_m_b_conv_w': 'new_m', 'new_m_b_conv_b': 'new_m', 'new_m_b_dt_bias': 'new_m', 'new_m_b_a_log': 'new_m', 'new_m_b_d': 'new_m', 'new_m_b_norm_w': 'new_m', 'new_m_b_out_w': 'new_m', 'new_m_c_w_qkv': 'new_m', 'new_m_c_w_o': 'new_m', 'new_m_mlp_w_up': 'new_m', 'new_m_mlp_w_down': 'new_m', 'new_m_final_norm_w': 'new_m', 'new_v_norm_mix_w': 'new_v', 'new_v_norm_mlp_w': 'new_v', 'new_v_a_w_qkv': 'new_v', 'new_v_a_b_qkv': 'new_v', 'new_v_a_sinks': 'new_v', 'new_v_a_w_o': 'new_v', 'new_v_a_b_o': 'new_v', 'new_v_b_in_w': 'new_v', 'new_v_b_conv_w': 'new_v', 'new_v_b_conv_b': 'new_v', 'new_v_b_dt_bias': 'new_v', 'new_v_b_a_log': 'new_v', 'new_v_b_d': 'new_v', 'new_v_b_norm_w': 'new_v', 'new_v_b_out_w': 'new_v', 'new_v_c_w_qkv': 'new_v', 'new_v_c_w_o': 'new_v', 'new_v_mlp_w_up': 'new_v', 'new_v_mlp_w_down': 'new_v', 'new_v_final_norm_w': 'new_v'}


def _forward(args):
    return _fwd_reference(*[args[k] for k in FWD_PARAMS])


def _output_shape():
    out = _jax.eval_shape(lambda: _forward(_fwd_setup_inputs(0)))
    return out.shape, out.dtype

N_MICROBATCH = 1
ADAM_LR = 0.001
ADAM_B1 = 0.9
ADAM_B2 = 0.999
ADAM_EPS = 1e-08
ADAM_WD = 0.01
ADAM_STEP = 10
PER_EXAMPLE_BATCH_AXIS = {'x': 0, 'positions': 0, 'loss_target': 0}
SHARED_INPUTS = []
_WEIGHT_DTYPES = {'norm_mix_w': _jnp.float32, 'norm_mlp_w': _jnp.float32, 'a_w_qkv': _jnp.float32, 'a_b_qkv': _jnp.float32, 'a_sinks': _jnp.float32, 'a_w_o': _jnp.float32, 'a_b_o': _jnp.float32, 'b_in_w': _jnp.float32, 'b_conv_w': _jnp.float32, 'b_conv_b': _jnp.float32, 'b_dt_bias': _jnp.float32, 'b_a_log': _jnp.float32, 'b_d': _jnp.float32, 'b_norm_w': _jnp.float32, 'b_out_w': _jnp.float32, 'c_w_qkv': _jnp.float32, 'c_w_o': _jnp.float32, 'mlp_w_up': _jnp.float32, 'mlp_w_down': _jnp.float32, 'final_norm_w': _jnp.float32}
MOMENT_SCALE = {'norm_mix_w': 4.828623e-02, 'norm_mlp_w': 7.642851e-02, 'a_w_qkv': 1.922248e-02, 'a_b_qkv': 7.523013e-02, 'a_sinks': 1.095655e-02, 'a_w_o': 4.571759e-02, 'a_b_o': 2.543721e-01, 'b_in_w': 3.532183e-02, 'b_conv_w': 3.054702e-02, 'b_conv_b': 4.585383e-02, 'b_dt_bias': 6.063620e-02, 'b_a_log': 1.367834e-01, 'b_d': 2.362659e-01, 'b_norm_w': 4.169174e-02, 'b_out_w': 1.668597e-01, 'c_w_qkv': 6.098242e-03, 'c_w_o': 3.793388e-02, 'mlp_w_up': 3.792490e-02, 'mlp_w_down': 2.029610e-01, 'final_norm_w': 3.227345e+01}


def _to_microbatches(a, axis):
    t = _jnp.moveaxis(a, axis, 0)
    t = t.reshape((N_MICROBATCH, t.shape[0] // N_MICROBATCH) + t.shape[1:])
    return _jnp.moveaxis(t, 1, axis + 1)


def setup_inputs(seed: int = 0) -> dict:
    inp = _fwd_setup_inputs(seed)
    key = _jax.random.fold_in(_jax.random.key(seed), 7919)
    shape, _ = _output_shape()
    out = dict(inp)
    out["loss_target"] = _jax.random.normal(_jax.random.fold_in(key, 0), shape, _jnp.float32)
    for i, name in enumerate(TWIN_WEIGHTS):
        w = inp[name].astype(_jnp.float32)
        if MOMENT_SCALE is None:
            s = _jnp.sqrt(_jnp.mean(_jnp.square(w)) + 1e-30)
        else:
            s = MOMENT_SCALE[name]
        km, kv = _jax.random.split(_jax.random.fold_in(key, i + 1))
        out[name] = w
        out["m_" + name] = s * _jax.random.normal(km, w.shape, _jnp.float32)
        out["v_" + name] = (s * s) * _jax.random.uniform(kv, w.shape, _jnp.float32, 0.5, 1.5)
    if N_MICROBATCH > 1:
        for name, axis in PER_EXAMPLE_BATCH_AXIS.items():
            out[name] = _to_microbatches(out[name], axis)
    return {'x': out['x'], 'positions': out['positions'], 'norm_mix_w': out['norm_mix_w'], 'norm_mlp_w': out['norm_mlp_w'], 'a_w_qkv': out['a_w_qkv'], 'a_b_qkv': out['a_b_qkv'], 'a_sinks': out['a_sinks'], 'a_w_o': out['a_w_o'], 'a_b_o': out['a_b_o'], 'b_in_w': out['b_in_w'], 'b_conv_w': out['b_conv_w'], 'b_conv_b': out['b_conv_b'], 'b_dt_bias': out['b_dt_bias'], 'b_a_log': out['b_a_log'], 'b_d': out['b_d'], 'b_norm_w': out['b_norm_w'], 'b_out_w': out['b_out_w'], 'c_w_qkv': out['c_w_qkv'], 'c_w_o': out['c_w_o'], 'mlp_w_up': out['mlp_w_up'], 'mlp_w_down': out['mlp_w_down'], 'final_norm_w': out['final_norm_w'], 'loss_target': out['loss_target'], 'm_norm_mix_w': out['m_norm_mix_w'], 'm_norm_mlp_w': out['m_norm_mlp_w'], 'm_a_w_qkv': out['m_a_w_qkv'], 'm_a_b_qkv': out['m_a_b_qkv'], 'm_a_sinks': out['m_a_sinks'], 'm_a_w_o': out['m_a_w_o'], 'm_a_b_o': out['m_a_b_o'], 'm_b_in_w': out['m_b_in_w'], 'm_b_conv_w': out['m_b_conv_w'], 'm_b_conv_b': out['m_b_conv_b'], 'm_b_dt_bias': out['m_b_dt_bias'], 'm_b_a_log': out['m_b_a_log'], 'm_b_d': out['m_b_d'], 'm_b_norm_w': out['m_b_norm_w'], 'm_b_out_w': out['m_b_out_w'], 'm_c_w_qkv': out['m_c_w_qkv'], 'm_c_w_o': out['m_c_w_o'], 'm_mlp_w_up': out['m_mlp_w_up'], 'm_mlp_w_down': out['m_mlp_w_down'], 'm_final_norm_w': out['m_final_norm_w'], 'v_norm_mix_w': out['v_norm_mix_w'], 'v_norm_mlp_w': out['v_norm_mlp_w'], 'v_a_w_qkv': out['v_a_w_qkv'], 'v_a_b_qkv': out['v_a_b_qkv'], 'v_a_sinks': out['v_a_sinks'], 'v_a_w_o': out['v_a_w_o'], 'v_a_b_o': out['v_a_b_o'], 'v_b_in_w': out['v_b_in_w'], 'v_b_conv_w': out['v_b_conv_w'], 'v_b_conv_b': out['v_b_conv_b'], 'v_b_dt_bias': out['v_b_dt_bias'], 'v_b_a_log': out['v_b_a_log'], 'v_b_d': out['v_b_d'], 'v_b_norm_w': out['v_b_norm_w'], 'v_b_out_w': out['v_b_out_w'], 'v_c_w_qkv': out['v_c_w_qkv'], 'v_c_w_o': out['v_c_w_o'], 'v_mlp_w_up': out['v_mlp_w_up'], 'v_mlp_w_down': out['v_mlp_w_down'], 'v_final_norm_w': out['v_final_norm_w']}


def _loss(weights, diff, rest, loss_target):
    with _jax.named_scope("forward"):
        args = {**rest, TWIN_DIFF_INPUT: diff, **{k: w.astype(_WEIGHT_DTYPES[k]) for k, w in weights.items()}}
        y = _forward(args)
    with _jax.named_scope("loss_head"):
        err = _jnp.square(y.astype(_jnp.float32) - loss_target)
        return 0.5 * _jnp.sum(_jnp.mean(err, axis=-1)) if err.ndim else 0.5 * err


def _adamw(w, g, m, v):
    m = ADAM_B1 * m + (1.0 - ADAM_B1) * g
    v = ADAM_B2 * v + (1.0 - ADAM_B2) * _jnp.square(g)
    m_hat = m / (1.0 - ADAM_B1 ** ADAM_STEP)
    v_hat = v / (1.0 - ADAM_B2 ** ADAM_STEP)
    delta = -ADAM_LR * (m_hat / (_jnp.sqrt(v_hat) + ADAM_EPS) + ADAM_WD * w)
    return delta, m, v


def reference(x, positions, norm_mix_w, norm_mlp_w, a_w_qkv, a_b_qkv, a_sinks, a_w_o, a_b_o, b_in_w, b_conv_w, b_conv_b, b_dt_bias, b_a_log, b_d, b_norm_w, b_out_w, c_w_qkv, c_w_o, mlp_w_up, mlp_w_down, final_norm_w, loss_target, m_norm_mix_w, m_norm_mlp_w, m_a_w_qkv, m_a_b_qkv, m_a_sinks, m_a_w_o, m_a_b_o, m_b_in_w, m_b_conv_w, m_b_conv_b, m_b_dt_bias, m_b_a_log, m_b_d, m_b_norm_w, m_b_out_w, m_c_w_qkv, m_c_w_o, m_mlp_w_up, m_mlp_w_down, m_final_norm_w, v_norm_mix_w, v_norm_mlp_w, v_a_w_qkv, v_a_b_qkv, v_a_sinks, v_a_w_o, v_a_b_o, v_b_in_w, v_b_conv_w, v_b_conv_b, v_b_dt_bias, v_b_a_log, v_b_d, v_b_norm_w, v_b_out_w, v_c_w_qkv, v_c_w_o, v_mlp_w_up, v_mlp_w_down, v_final_norm_w):
    given = dict(x=x, positions=positions, norm_mix_w=norm_mix_w, norm_mlp_w=norm_mlp_w, a_w_qkv=a_w_qkv, a_b_qkv=a_b_qkv, a_sinks=a_sinks, a_w_o=a_w_o, a_b_o=a_b_o, b_in_w=b_in_w, b_conv_w=b_conv_w, b_conv_b=b_conv_b, b_dt_bias=b_dt_bias, b_a_log=b_a_log, b_d=b_d, b_norm_w=b_norm_w, b_out_w=b_out_w, c_w_qkv=c_w_qkv, c_w_o=c_w_o, mlp_w_up=mlp_w_up, mlp_w_down=mlp_w_down, final_norm_w=final_norm_w, loss_target=loss_target, m_norm_mix_w=m_norm_mix_w, m_norm_mlp_w=m_norm_mlp_w, m_a_w_qkv=m_a_w_qkv, m_a_b_qkv=m_a_b_qkv, m_a_sinks=m_a_sinks, m_a_w_o=m_a_w_o, m_a_b_o=m_a_b_o, m_b_in_w=m_b_in_w, m_b_conv_w=m_b_conv_w, m_b_conv_b=m_b_conv_b, m_b_dt_bias=m_b_dt_bias, m_b_a_log=m_b_a_log, m_b_d=m_b_d, m_b_norm_w=m_b_norm_w, m_b_out_w=m_b_out_w, m_c_w_qkv=m_c_w_qkv, m_c_w_o=m_c_w_o, m_mlp_w_up=m_mlp_w_up, m_mlp_w_down=m_mlp_w_down, m_final_norm_w=m_final_norm_w, v_norm_mix_w=v_norm_mix_w, v_norm_mlp_w=v_norm_mlp_w, v_a_w_qkv=v_a_w_qkv, v_a_b_qkv=v_a_b_qkv, v_a_sinks=v_a_sinks, v_a_w_o=v_a_w_o, v_a_b_o=v_a_b_o, v_b_in_w=v_b_in_w, v_b_conv_w=v_b_conv_w, v_b_conv_b=v_b_conv_b, v_b_dt_bias=v_b_dt_bias, v_b_a_log=v_b_a_log, v_b_d=v_b_d, v_b_norm_w=v_b_norm_w, v_b_out_w=v_b_out_w, v_c_w_qkv=v_c_w_qkv, v_c_w_o=v_c_w_o, v_mlp_w_up=v_mlp_w_up, v_mlp_w_down=v_mlp_w_down, v_final_norm_w=v_final_norm_w)
    weights = {n: given[n] for n in TWIN_WEIGHTS}
    shared = {n: given[n] for n in SHARED_INPUTS}
    per_example = {n: given[n] for n in ['x', 'positions']}
    grad_fn = _jax.value_and_grad(_loss, argnums=(0, 1))

    def one_microbatch(ex, loss_target):
        ex = dict(ex)
        diff = ex.pop(TWIN_DIFF_INPUT)
        return grad_fn(weights, diff, {**shared, **ex}, loss_target)

    if N_MICROBATCH == 1:
        loss, (grad_w, grad_x) = one_microbatch(per_example, given["loss_target"])
    else:
        def body(carry, xs):
            loss_sum, grad_sum = carry
            l_k, (gw_k, gx_k) = one_microbatch(xs[0], xs[1])
            with _jax.named_scope("update"):
                return (loss_sum + l_k, _jax.tree.map(_jnp.add, grad_sum, gw_k)), gx_k

        init = (_jnp.zeros((), _jnp.float32), _jax.tree.map(_jnp.zeros_like, weights))
        (loss, grad_w), grad_x = _jax.lax.scan(body, init, (per_example, given["loss_target"]))
    with _jax.named_scope("update"):
        delta_w, new_m, new_v = {}, {}, {}
        for n in TWIN_WEIGHTS:
            delta_w[n], new_m[n], new_v[n] = _adamw(weights[n], grad_w[n], given["m_" + n], given["v_" + n])
    return (loss, grad_x, *[grad_w[n] for n in TWIN_WEIGHTS], *[delta_w[n] for n in TWIN_WEIGHTS],
            *[new_m[n] for n in TWIN_WEIGHTS], *[new_v[n] for n in TWIN_WEIGHTS])
```

```python
import functools
import math

import jax
import jax.numpy as jnp
import numpy as np
from jax import lax
from jax.experimental import pallas as pl
from jax.experimental.pallas import tpu as pltpu

F32 = jnp.float32
BF16 = jnp.bfloat16
SDS = jax.ShapeDtypeStruct

D_MODEL = 1024
DEPTH = 4
BLOCK = 128
ROPE_THETA = 10000.0
NORM_EPS = 1e-5
HEAD_DIM = 64
A_N_HEADS = 16
A_N_KV = 2
A_WINDOW = 128
A_Q_DIM = 1024
A_KV_DIM = 128
SSM_D_INNER = 2048
SSM_N_HEADS = 32
SSM_N_GROUPS = 8
SSM_HG = 4
SSM_D_STATE = 128
SSM_CONV = 4
SSM_CHUNK = 128
SSM_BC_DIM = 1024
SSM_CONV_DIM = 4096
C_PATTERNS = ((128, 1), (512, 4), (2048, 16))
C_HEADS = 16
ADAM_LR, ADAM_B1, ADAM_B2, ADAM_EPS, ADAM_WD, ADAM_STEP = 0.001, 0.9, 0.999, 1e-08, 0.01, 10

N_DEV = 8
AXES = ("x", "y", "c")
LANES = 128
VMEM_LIMIT = 56 * 1024 * 1024
NEG = -1e30

NN = (((1,), (0,)), ((), ()))
NT = (((1,), (1,)), ((), ()))
TN = (((0,), (0,)), ((), ()))
HI = lax.Precision.HIGHEST


def _pick(n, cap, mult=LANES):
    best = None
    for t in range(mult, min(n, cap) + 1, mult):
        if n % t == 0:
            best = t
    return best if best is not None else n


def _params(sem):
    return pltpu.CompilerParams(dimension_semantics=sem, vmem_limit_bytes=VMEM_LIMIT)


def _bf(x):
    return x if x.dtype == BF16 else x.astype(BF16)


def _rot_half(y):
    n = y.shape[-1]
    lane = lax.broadcasted_iota(jnp.int32, y.shape, y.ndim - 1)
    return jnp.where((lane % HEAD_DIM) < HEAD_DIM // 2, -pltpu.roll(y, n - 32, y.ndim - 1), pltpu.roll(y, 32, y.ndim - 1))


def _rope(y, cos, sin, sign):
    reps = y.shape[-1] // LANES
    c = jnp.tile(cos, (1, reps)) if reps > 1 else cos
    s = jnp.tile(sin, (1, reps)) if reps > 1 else sin
    return y * c + sign * (_rot_half(y) * s)


def _matmul(a, b, *, ta=False, tb=False, out_dtype=F32, bias=None, resid=None, mul=None, mul_scale=1.0,
            relu2=False, rope=None, rope_cols=0, tm=512, tn=1024, tk=1024, name="mm"):
    M = a.shape[1] if ta else a.shape[0]
    K = a.shape[0] if ta else a.shape[1]
    N = b.shape[0] if tb else b.shape[1]
    assert (b.shape[1] if tb else b.shape[0]) == K
    tm, tn, tk = _pick(M, tm), _pick(N, tn), _pick(K, tk)
    nk = K // tk
    dims = (((0 if ta else 1,), (1 if tb else 0,)), ((), ()))

    def body(*refs):
        it = iter(refs)
        a_ref, b_ref = next(it), next(it)
        bias_ref = next(it) if bias is not None else None
        resid_ref = next(it) if resid is not None else None
        mul_ref = next(it) if mul is not None else None
        cos_ref, sin_ref = (next(it), next(it)) if rope is not None else (None, None)
        o_ref = next(it)
        o2_ref = next(it) if relu2 else None
        acc_ref = next(it)
        k = pl.program_id(2)
        part = lax.dot_general(_bf(a_ref[...]), _bf(b_ref[...]), dims, preferred_element_type=F32)

        @pl.when(k == 0)
        def _():
            acc_ref[...] = part

        @pl.when(k > 0)
        def _():
            acc_ref[...] += part

        @pl.when(k == nk - 1)
        def _():
            y = acc_ref[...]
            if bias_ref is not None:
                y = y + bias_ref[...]
            if rope is not None:
                col = pl.program_id(1) * tn + lax.broadcasted_iota(jnp.int32, y.shape, 1)
                y = jnp.where(col < rope_cols, _rope(y, cos_ref[...], sin_ref[...], 1.0), y)
            if mul_ref is not None:
                y = y * (mul_ref[...].astype(F32) * mul_scale)
            if resid_ref is not None:
                y = y + resid_ref[...]
            if relu2:
                r = jnp.maximum(y, 0.0)
                o_ref[...] = r.astype(o_ref.dtype)
                o2_ref[...] = (r * r).astype(o2_ref.dtype)
            else:
                o_ref[...] = y.astype(o_ref.dtype)

    a_spec = pl.BlockSpec((tk, tm), lambda i, j, k: (k, i)) if ta else pl.BlockSpec((tm, tk), lambda i, j, k: (i, k))
    b_spec = pl.BlockSpec((tn, tk), lambda i, j, k: (j, k)) if tb else pl.BlockSpec((tk, tn), lambda i, j, k: (k, j))
    mn_spec = pl.BlockSpec((tm, tn), lambda i, j, k: (i, j))
    in_specs, args = [a_spec, b_spec], [a, b]
    if bias is not None:
        in_specs.append(pl.BlockSpec((1, tn), lambda i, j, k: (0, j)))
        args.append(bias)
    if resid is not None:
        in_specs.append(mn_spec)
        args.append(resid)
    if mul is not None:
        in_specs.append(mn_spec)
        args.append(mul)
    if rope is not None:
        in_specs += [pl.BlockSpec((tm, LANES), lambda i, j, k: (i, 0))] * 2
        args += [rope[0], rope[1]]
    out_shape = SDS((M, N), out_dtype)
    out_specs = mn_spec
    if relu2:
        out_shape, out_specs = (out_shape, out_shape), (mn_spec, mn_spec)
    return pl.pallas_call(
        body, out_shape=out_shape, grid=(M // tm, N // tn, nk), in_specs=in_specs, out_specs=out_specs,
        scratch_shapes=[pltpu.VMEM((tm, tn), F32)], name=name,
        compiler_params=_params(("parallel", "parallel", "arbitrary")),
    )(*args)


def _colsum(x, name):
    T, N = x.shape
    tm = _pick(T, 1024, 8)

    def body(x_ref, o_ref):
        s = jnp.sum(x_ref[...].astype(F32), axis=0, keepdims=True)

        @pl.when(pl.program_id(0) == 0)
        def _():
            o_ref[...] = s

        @pl.when(pl.program_id(0) > 0)
        def _():
            o_ref[...] += s

    return pl.pallas_call(
        body, out_shape=SDS((1, N), F32), grid=(T // tm,),
        in_specs=[pl.BlockSpec((tm, N), lambda i: (i, 0))], out_specs=pl.BlockSpec((1, N), lambda i: (0, 0)),
        name=name, compiler_params=_params(("arbitrary",)),
    )(x)


def _rmsnorm_fwd(h, w, name):
    T, D = h.shape
    tm = _pick(T, 512, 8)

    def body(h_ref, w_ref, o_ref):
        x = h_ref[...]
        rstd = lax.rsqrt(jnp.mean(x * x, axis=-1, keepdims=True) + NORM_EPS)
        o_ref[...] = (x * rstd * w_ref[...]).astype(BF16)

    return pl.pallas_call(
        body, out_shape=SDS((T, D), BF16), grid=(T // tm,),
        in_specs=[pl.BlockSpec((tm, D), lambda i: (i, 0)), pl.BlockSpec((1, D), lambda i: (0, 0))],
        out_specs=pl.BlockSpec((tm, D), lambda i: (i, 0)), name=name, compiler_params=_params(("parallel",)),
    )(h, w.reshape(1, D))


def _rmsnorm_bwd(h, du, w, dres, name):
    T, D = h.shape
    tm = _pick(T, 512, 8)

    def body(h_ref, du_ref, w_ref, dres_ref, dh_ref, dw_ref):
        x = h_ref[...]
        du_ = du_ref[...].astype(F32)
        rstd = lax.rsqrt(jnp.mean(x * x, axis=-1, keepdims=True) + NORM_EPS)
        g = du_ * w_ref[...]
        dh_ref[...] = dres_ref[...] + rstd * g - x * (rstd * rstd * rstd) * jnp.mean(g * x, axis=-1, keepdims=True)
        dw = jnp.sum(du_ * x * rstd, axis=0, keepdims=True)

        @pl.when(pl.program_id(0) == 0)
        def _():
            dw_ref[...] = dw

        @pl.when(pl.program_id(0) > 0)
        def _():
            dw_ref[...] += dw

    row = pl.BlockSpec((tm, D), lambda i: (i, 0))
    vec = pl.BlockSpec((1, D), lambda i: (0, 0))
    return pl.pallas_call(
        body, out_shape=(SDS((T, D), F32), SDS((1, D), F32)), grid=(T // tm,),
        in_specs=[row, row, vec, row], out_specs=(row, vec), name=name, compiler_params=_params(("arbitrary",)),
    )(h, du, w.reshape(1, D), dres)


def _final_loss(h, target, w):
    T, D = h.shape
    tm = _pick(T, 512, 8)

    def body(h_ref, t_ref, w_ref, dh_ref, dw_ref, loss_ref):
        x = h_ref[...]
        rstd = lax.rsqrt(jnp.mean(x * x, axis=-1, keepdims=True) + NORM_EPS)
        xn = x * rstd
        err = xn * w_ref[...] - t_ref[...]
        part = 0.5 * jnp.sum(jnp.mean(err * err, axis=-1, keepdims=True), axis=0, keepdims=True)
        dy = err * (1.0 / D)
        g = dy * w_ref[...]
        dh_ref[...] = rstd * g - x * (rstd * rstd * rstd) * jnp.mean(g * x, axis=-1, keepdims=True)
        dw = jnp.sum(dy * xn, axis=0, keepdims=True)
        lp = jnp.broadcast_to(part, (1, LANES))

        @pl.when(pl.program_id(0) == 0)
        def _():
            dw_ref[...] = dw
            loss_ref[...] = lp

        @pl.when(pl.program_id(0) > 0)
        def _():
            dw_ref[...] += dw
            loss_ref[...] += lp

    row = pl.BlockSpec((tm, D), lambda i: (i, 0))
    vec = pl.BlockSpec((1, D), lambda i: (0, 0))
    return pl.pallas_call(
        body, out_shape=(SDS((T, D), F32), SDS((1, D), F32), SDS((1, LANES), F32)), grid=(T // tm,),
        in_specs=[row, row, vec], out_specs=(row, vec, pl.BlockSpec((1, LANES), lambda i: (0, 0))),
        name="final_loss", compiler_params=_params(("arbitrary",)),
    )(h, target, w.reshape(1, D))


def _band_mask(i_blk, max_dist, first_ok):
    qi = lax.broadcasted_iota(jnp.int32, (BLOCK, 2 * BLOCK), 0)
    kj = lax.broadcasted_iota(jnp.int32, (BLOCK, 2 * BLOCK), 1)
    dist = qi + BLOCK - kj
    ok = (dist >= 0) & (dist <= max_dist)
    return ok & ((kj >= BLOCK) | first_ok)


def _head_cols(t, h):
    return t[:, HEAD_DIM * h:HEAD_DIM * (h + 1)]


def _lane_place(cols):
    m = cols[0].shape[0]
    lane = lax.broadcasted_iota(jnp.int32, (m, LANES), 1)
    out = jnp.zeros((m, LANES), F32)
    for h, c in enumerate(cols):
        out = jnp.where(lane == h, c, out)
    return out


def _attn_specs(B, S, d, C, n_heads, n_kv, q_col, k_col, v_col):
    kvw = n_kv * HEAD_DIM
    qw = n_heads * HEAD_DIM
    cq, ck = (C // qw if d > 1 else 0), (C // kvw if d > 1 else 0)
    q_spec = pl.BlockSpec((1, BLOCK, qw), lambda b, r, i: (b, i, r * cq + q_col // qw))
    kc = pl.BlockSpec((1, BLOCK, kvw), lambda b, r, i: (b, i, r * ck + k_col // kvw))
    kp = pl.BlockSpec((1, BLOCK, kvw), lambda b, r, i: (b, jnp.maximum(i - 1, 0), r * ck + k_col // kvw))
    vc = pl.BlockSpec((1, BLOCK, kvw), lambda b, r, i: (b, i, r * ck + v_col // kvw))
    vp = pl.BlockSpec((1, BLOCK, kvw), lambda b, r, i: (b, jnp.maximum(i - 1, 0), r * ck + v_col // kvw))
    return q_spec, kp, kc, vp, vc


def _attn_fwd(qkv, B, S, d, *, n_heads, n_kv, q_col, k_col, v_col, max_dist, sinks, name):
    C = qkv.shape[1]
    Ls = S // d
    nb = Ls // BLOCK
    qw = n_heads * HEAD_DIM
    R = n_heads // n_kv
    qkv3 = qkv.reshape(B, Ls, d * C)
    scale = HEAD_DIM ** -0.5

    def body(*refs):
        if sinks is not None:
            sink_ref, q_ref, kp_ref, kc_ref, vp_ref, vc_ref, o_ref, lse_ref = refs
        else:
            q_ref, kp_ref, kc_ref, vp_ref, vc_ref, o_ref, lse_ref = refs
        i = pl.program_id(2)
        mask = _band_mask(i, max_dist, i > 0)
        q = q_ref[0]
        kk = jnp.concatenate([kp_ref[0], kc_ref[0]], axis=0)
        vv = jnp.concatenate([vp_ref[0], vc_ref[0]], axis=0)
        lses, tiles = [], []
        for pair in range(n_heads // 2):
            outs = []
            for h in (2 * pair, 2 * pair + 1):
                g = h // R
                s = lax.dot_general(_head_cols(q, h), _head_cols(kk, g), NT, preferred_element_type=F32) * scale
                s = jnp.where(mask, s, NEG)
                m = jnp.max(s, axis=-1, keepdims=True)
                if sinks is not None:
                    sk = sink_ref[h]
                    m = jnp.maximum(m, sk)
                p = jnp.exp(s - m)
                den = jnp.sum(p, axis=-1, keepdims=True)
                if sinks is not None:
                    den = den + jnp.exp(sk - m)
                lses.append(m + jnp.log(den))
                pn = (p / den).astype(BF16)
                outs.append(jnp.dot(pn, _head_cols(vv, g), preferred_element_type=F32))
            tiles.append(jnp.concatenate(outs, axis=-1))
        o_ref[0] = jnp.concatenate(tiles, axis=-1)
        lse_ref[0] = _lane_place(lses)

    specs = list(_attn_specs(B, S, d, C, n_heads, n_kv, q_col, k_col, v_col))
    args = [qkv3] * 5
    if sinks is not None:
        specs = [pl.BlockSpec(memory_space=pltpu.SMEM)] + specs
        args = [sinks] + args
    o3, lse3 = pl.pallas_call(
        body, out_shape=(SDS((B, Ls, d * qw), F32), SDS((B, Ls, d * LANES), F32)), grid=(B, d, nb), in_specs=specs,
        out_specs=(pl.BlockSpec((1, BLOCK, qw), lambda b, r, i: (b, i, r)), pl.BlockSpec((1, BLOCK, LANES), lambda b, r, i: (b, i, r))),
        name=name, compiler_params=_params(("parallel", "parallel", "parallel")),
    )(*args)
    return o3.reshape(B * S, qw), lse3.reshape(B * S, LANES)


def _attn_dq(qkv, do, lse, delta, cos, sin, B, S, d, *, n_heads, n_kv, q_col, k_col, v_col, max_dist, name):
    C = qkv.shape[1]
    Ls = S // d
    nb = Ls // BLOCK
    qw = n_heads * HEAD_DIM
    R = n_heads // n_kv
    scale = HEAD_DIM ** -0.5

    def body(q_ref, kp_ref, kc_ref, vp_ref, vc_ref, do_ref, lse_ref, dl_ref, cos_ref, sin_ref, dq_ref):
        i = pl.program_id(2)
        mask = _band_mask(i, max_dist, i > 0)
        q = q_ref[0]
        do_ = do_ref[0]
        kk = jnp.concatenate([kp_ref[0], kc_ref[0]], axis=0)
        vv = jnp.concatenate([vp_ref[0], vc_ref[0]], axis=0)
        lse_t, dl_t = lse_ref[0], dl_ref[0]
        tiles = []
        for pair in range(n_heads // 2):
            outs = []
            for h in (2 * pair, 2 * pair + 1):
                g = h // R
                kh = _head_cols(kk, g)
                s = lax.dot_general(_head_cols(q, h), kh, NT, preferred_element_type=F32) * scale
                p = jnp.where(mask, jnp.exp(s - lse_t[:, h:h + 1]), 0.0)
                dp = lax.dot_general(_head_cols(do_, h), _head_cols(vv, g), NT, preferred_element_type=F32)
                ds = p * (dp - dl_t[:, h:h + 1])
                outs.append(jnp.dot(ds.astype(BF16), kh, preferred_element_type=F32) * scale)
            tiles.append(jnp.concatenate(outs, axis=-1))
        dq = jnp.concatenate(tiles, axis=-1)
        dq_ref[0] = _rope(dq, cos_ref[0], sin_ref[0], -1.0).astype(BF16)

    qs, kp, kc, vp, vc = _attn_specs(B, S, d, C, n_heads, n_kv, q_col, k_col, v_col)
    row_q = pl.BlockSpec((1, BLOCK, qw), lambda b, r, i: (b, i, r))
    row_l = pl.BlockSpec((1, BLOCK, LANES), lambda b, r, i: (b, i, r))
    qkv3 = qkv.reshape(B, Ls, d * C)
    v3 = lambda t, w: t.reshape(B, Ls, d * w)
    dq3 = pl.pallas_call(
        body, out_shape=SDS((B, Ls, d * qw), BF16), grid=(B, d, nb),
        in_specs=[qs, kp, kc, vp, vc, row_q, row_l, row_l, row_l, row_l], out_specs=row_q,
        name=name, compiler_params=_params(("parallel", "parallel", "parallel")),
    )(qkv3, qkv3, qkv3, qkv3, qkv3, v3(do, qw), v3(lse, LANES), v3(delta, LANES), v3(cos, LANES), v3(sin, LANES))
    return dq3.reshape(B * S, qw)


def _attn_dkv(qkv, do, lse, delta, cos, sin, B, S, d, *, n_heads, n_kv, q_col, k_col, v_col, max_dist, name):
    C = qkv.shape[1]
    Ls = S // d
    nb = Ls // BLOCK
    qw = n_heads * HEAD_DIM
    kvw = n_kv * HEAD_DIM
    R = n_heads // n_kv
    scale = HEAD_DIM ** -0.5
    cq, ck = (C // qw if d > 1 else 0), (C // kvw if d > 1 else 0)

    def body(k_ref, v_ref, q0_ref, q1_ref, do0_ref, do1_ref, lse0_ref, lse1_ref, dl0_ref, dl1_ref, cos_ref, sin_ref,
             dk_ref, dv_ref):
        j = pl.program_id(2)
        qi = lax.broadcasted_iota(jnp.int32, (BLOCK, BLOCK), 0)
        kj = lax.broadcasted_iota(jnp.int32, (BLOCK, BLOCK), 1)
        dist0 = qi - kj
        dist1 = qi + BLOCK - kj
        mask0 = (dist0 >= 0) & (dist0 <= max_dist)
        mask1 = (dist1 <= max_dist) & (j + 1 < nb)
        kb, vb = k_ref[0], v_ref[0]
        sides = ((q0_ref[0], do0_ref[0], lse0_ref[0], dl0_ref[0], mask0), (q1_ref[0], do1_ref[0], lse1_ref[0], dl1_ref[0], mask1))
        dks, dvs = [], []
        for g in range(n_kv):
            kh, vh = _head_cols(kb, g), _head_cols(vb, g)
            dk = jnp.zeros((BLOCK, HEAD_DIM), F32)
            dv = jnp.zeros((BLOCK, HEAD_DIM), F32)
            for h in range(g * R, (g + 1) * R):
                for (q, do_, lse_t, dl_t, mask) in sides:
                    qh, doh = _head_cols(q, h), _head_cols(do_, h)
                    s = lax.dot_general(qh, kh, NT, preferred_element_type=F32) * scale
                    p = jnp.where(mask, jnp.exp(s - lse_t[:, h:h + 1]), 0.0)
                    dp = lax.dot_general(doh, vh, NT, preferred_element_type=F32)
                    ds = p * (dp - dl_t[:, h:h + 1])
                    dv = dv + lax.dot_general(p.astype(BF16), doh, TN, preferred_element_type=F32)
                    dk = dk + lax.dot_general(ds.astype(BF16), qh, TN, preferred_element_type=F32) * scale
            dks.append(dk)
            dvs.append(dv)
        dk_t = jnp.concatenate([jnp.concatenate(dks[2 * t:2 * t + 2], axis=-1) for t in range(n_kv // 2)], axis=-1)
        dv_t = jnp.concatenate([jnp.concatenate(dvs[2 * t:2 * t + 2], axis=-1) for t in range(n_kv // 2)], axis=-1)
        dk_ref[0] = _rope(dk_t, cos_ref[0], sin_ref[0], -1.0).astype(BF16)
        dv_ref[0] = dv_t.astype(BF16)

    nxt = lambda j: jnp.minimum(j + 1, nb - 1)
    k_spec = pl.BlockSpec((1, BLOCK, kvw), lambda b, r, j: (b, j, r * ck + k_col // kvw))
    v_spec = pl.BlockSpec((1, BLOCK, kvw), lambda b, r, j: (b, j, r * ck + v_col // kvw))
    q0 = pl.BlockSpec((1, BLOCK, qw), lambda b, r, j: (b, j, r * cq + q_col // qw))
    q1 = pl.BlockSpec((1, BLOCK, qw), lambda b, r, j: (b, nxt(j), r * cq + q_col // qw))
    w0 = lambda w: pl.BlockSpec((1, BLOCK, w), lambda b, r, j: (b, j, r))
    w1 = lambda w: pl.BlockSpec((1, BLOCK, w), lambda b, r, j: (b, nxt(j), r))
    qkv3 = qkv.reshape(B, Ls, d * C)
    v3 = lambda t, w: t.reshape(B, Ls, d * w)
    do3, lse3, dl3 = v3(do, qw), v3(lse, LANES), v3(delta, LANES)
    dk3, dv3 = pl.pallas_call(
        body, out_shape=(SDS((B, Ls, d * kvw), BF16), SDS((B, Ls, d * kvw), BF16)), grid=(B, d, nb),
        in_specs=[k_spec, v_spec, q0, q1, w0(qw), w1(qw), w0(LANES), w1(LANES), w0(LANES), w1(LANES), w0(LANES), w0(LANES)],
        out_specs=(w0(kvw), w0(kvw)), name=name, compiler_params=_params(("parallel", "parallel", "parallel")),
    )(qkv3, qkv3, qkv3, qkv3, do3, do3, lse3, lse3, dl3, dl3, v3(cos, LANES), v3(sin, LANES))
    return dk3.reshape(B * S, kvw), dv3.reshape(B * S, kvw)


def _head_expand():
    r = lax.broadcasted_iota(jnp.int32, (LANES, C_HEADS * HEAD_DIM), 0)
    c = lax.broadcasted_iota(jnp.int32, (LANES, C_HEADS * HEAD_DIM), 1)
    return jnp.where(c // HEAD_DIM == r, 1.0, 0.0).astype(F32)


def _delta(do, o, lse=None, sinks_row=None, name="delta"):
    T, W = do.shape
    tm = _pick(T, 512, 8)
    with_sink = sinks_row is not None

    def body(*refs):
        if with_sink:
            do_ref, o_ref, lse_ref, sk_ref, dl_ref, dob_ref, ds_ref = refs
        else:
            do_ref, o_ref, dl_ref, dob_ref = refs
        do_ = do_ref[...]
        dl = lax.dot_general(do_ * o_ref[...], _head_expand(), NT, preferred_element_type=F32, precision=HI)
        dl_ref[...] = dl
        dob_ref[...] = do_.astype(BF16)
        if with_sink:
            lane = lax.broadcasted_iota(jnp.int32, dl.shape, 1)
            contrib = jnp.where(lane < A_N_HEADS, -jnp.exp(sk_ref[...] - lse_ref[...]) * dl, 0.0)
            part = jnp.sum(contrib, axis=0, keepdims=True)

            @pl.when(pl.program_id(0) == 0)
            def _():
                ds_ref[...] = part

            @pl.when(pl.program_id(0) > 0)
            def _():
                ds_ref[...] += part

    row_w = pl.BlockSpec((tm, W), lambda i: (i, 0))
    row_l = pl.BlockSpec((tm, LANES), lambda i: (i, 0))
    vec_l = pl.BlockSpec((1, LANES), lambda i: (0, 0))
    if with_sink:
        return pl.pallas_call(
            body, out_shape=(SDS((T, LANES), F32), SDS((T, W), BF16), SDS((1, LANES), F32)), grid=(T // tm,),
            in_specs=[row_w, row_w, row_l, vec_l], out_specs=(row_l, row_w, vec_l), name=name,
            compiler_params=_params(("arbitrary",)),
        )(do, o, lse, sinks_row)
    return pl.pallas_call(
        body, out_shape=(SDS((T, LANES), F32), SDS((T, W), BF16)), grid=(T // tm,),
        in_specs=[row_w, row_w], out_specs=(row_l, row_w), name=name, compiler_params=_params(("parallel",)),
    )(do, o)


def _merge(os_, lses):
    T, W = os_[0].shape
    tm = _pick(T, 512, 8)

    def body(o0, o1, o2, l0, l1, l2, o_ref, lse_ref):
        ls = [l0[...], l1[...], l2[...]]
        m = jnp.maximum(jnp.maximum(ls[0], ls[1]), ls[2])
        ws = [jnp.exp(l - m) for l in ls]
        tot = ws[0] + ws[1] + ws[2]
        lse_ref[...] = m + jnp.log(tot)
        e = _head_expand()
        acc = jnp.zeros((tm, W), F32)
        for w, o in zip(ws, (o0, o1, o2)):
            acc = acc + jnp.dot(w / tot, e, preferred_element_type=F32, precision=HI) * o[...]
        o_ref[...] = acc

    row_w = pl.BlockSpec((tm, W), lambda i: (i, 0))
    row_l = pl.BlockSpec((tm, LANES), lambda i: (i, 0))
    return pl.pallas_call(
        body, out_shape=(SDS((T, W), F32), SDS((T, LANES), F32)), grid=(T // tm,),
        in_specs=[row_w] * 3 + [row_l] * 3, out_specs=(row_w, row_l), name="c_merge", compiler_params=_params(("parallel",)),
    )(*os_, *lses)


CONV_TC = 256


def _conv_pre(x, w, bias):
    row = lax.broadcasted_iota(jnp.int32, x.shape, 0)
    acc = x * w[SSM_CONV - 1:SSM_CONV, :] + bias
    for k in range(1, SSM_CONV):
        acc = acc + jnp.where(row >= k, pltpu.roll(x, k, 0), 0.0) * w[SSM_CONV - 1 - k:SSM_CONV - k, :]
    return acc


def _conv_fwd(zx3, w, bias):
    B, S, _ = zx3.shape
    off = SSM_D_INNER // CONV_TC

    def body(x_ref, w_ref, b_ref, o_ref):
        v = _conv_pre(x_ref[0], w_ref[...], b_ref[...])
        o_ref[0] = v * jax.nn.sigmoid(v)

    return pl.pallas_call(
        body, out_shape=SDS((B, S, SSM_CONV_DIM), F32), grid=(B, SSM_CONV_DIM // CONV_TC),
        in_specs=[pl.BlockSpec((1, S, CONV_TC), lambda b, j: (b, 0, j + off)),
                  pl.BlockSpec((SSM_CONV, CONV_TC), lambda b, j: (0, j)), pl.BlockSpec((1, CONV_TC), lambda b, j: (0, j))],
        out_specs=pl.BlockSpec((1, S, CONV_TC), lambda b, j: (b, 0, j)), name="b_conv_fwd",
        compiler_params=_params(("parallel", "parallel")),
    )(zx3, w, bias)


def _conv_bwd(zx3, dxc, w, bias, col0, name):
    B, S, n = dxc.shape
    tc = _pick(n, CONV_TC)
    off_x = (SSM_D_INNER + col0) // tc
    off_w = col0 // tc

    def body(x_ref, d_ref, w_ref, b_ref, dx_ref, dw_ref, db_ref):
        x = x_ref[0]
        wv = w_ref[...]
        v = _conv_pre(x, wv, b_ref[...])
        sg = jax.nn.sigmoid(v)
        dc = d_ref[0] * (sg * (1.0 + v * (1.0 - sg)))
        row = lax.broadcasted_iota(jnp.int32, x.shape, 0)
        dx = dc * wv[SSM_CONV - 1:SSM_CONV, :]
        dws = [jnp.sum(dc * x, axis=0, keepdims=True)]
        for k in range(1, SSM_CONV):
            dx = dx + jnp.where(row < S - k, pltpu.roll(dc, S - k, 0), 0.0) * wv[SSM_CONV - 1 - k:SSM_CONV - k, :]
            dws.append(jnp.sum(dc * jnp.where(row >= k, pltpu.roll(x, k, 0), 0.0), axis=0, keepdims=True))
        dx_ref[0] = dx.astype(BF16)
        ridx = lax.broadcasted_iota(jnp.int32, (SSM_CONV, tc), 0)
        dw = jnp.zeros((SSM_CONV, tc), F32)
        for k in range(SSM_CONV):
            dw = jnp.where(ridx == SSM_CONV - 1 - k, dws[k], dw)
        db = jnp.sum(dc, axis=0, keepdims=True)

        @pl.when(pl.program_id(1) == 0)
        def _():
            dw_ref[...] = dw
            db_ref[...] = db

        @pl.when(pl.program_id(1) > 0)
        def _():
            dw_ref[...] += dw
            db_ref[...] += db

    return pl.pallas_call(
        body, out_shape=(SDS((B, S, n), BF16), SDS((SSM_CONV, n), F32), SDS((1, n), F32)), grid=(n // tc, B),
        in_specs=[pl.BlockSpec((1, S, tc), lambda j, b: (b, 0, j + off_x)), pl.BlockSpec((1, S, tc), lambda j, b: (b, 0, j)),
                  pl.BlockSpec((SSM_CONV, tc), lambda j, b: (0, j + off_w)), pl.BlockSpec((1, tc), lambda j, b: (0, j + off_w))],
        out_specs=(pl.BlockSpec((1, S, tc), lambda j, b: (b, 0, j)), pl.BlockSpec((SSM_CONV, tc), lambda j, b: (0, j)),
                   pl.BlockSpec((1, tc), lambda j, b: (0, j))),
        name=name, compiler_params=_params(("parallel", "arbitrary")),
    )(zx3, dxc, w, bias)


def _ssd_common(x, Bm, Cm, dtc_raw, dtr_raw, pr, pc):
    Q = SSM_CHUNK
    zc = dtc_raw + pr[0:1, :]
    dt_c = jax.nn.softplus(zc)
    dt_r = jax.nn.softplus(dtr_raw + pc[:, 0:1])
    A_r = -jnp.exp(pr[1:2, :])
    A_c = -jnp.exp(pc[:, 1:2])
    row = lax.broadcasted_iota(jnp.int32, (Q, Q), 0)
    col = lax.broadcasted_iota(jnp.int32, (Q, Q), 1)
    tril = jnp.where(row >= col, 1.0, 0.0).astype(F32)
    cs_c = jnp.dot(tril, dt_c * A_r, preferred_element_type=F32, precision=HI)
    cs_r = lax.dot_general(dt_r * A_c, tril, NT, preferred_element_type=F32, precision=HI)
    return zc, dt_c, A_r, cs_c, cs_r, row, col, tril


def _ssd_fwd(xc3, dtc, dtr, prow, pcol):
    B, S, _ = xc3.shape
    Q, G, HG, P, N = SSM_CHUNK, SSM_N_GROUPS, SSM_HG, HEAD_DIM, SSM_D_STATE
    nc = S // Q
    xw = HG * P

    def body(x_ref, b_ref, c_ref, dtc_ref, dtr_ref, pr_ref, pc_ref, y_ref, st_ref, state):
        c = pl.program_id(2)

        @pl.when(c == 0)
        def _():
            state[...] = jnp.zeros_like(state)

        x, Bm, Cm = x_ref[0], b_ref[0], c_ref[0]
        pr = pr_ref[0]
        _, dt_c, _, cs_c, cs_r, row, col, _ = _ssd_common(x, Bm, Cm, dtc_ref[0, 0], dtr_ref[0, 0], pr, pc_ref[0])
        Bb, Cb = Bm.astype(BF16), Cm.astype(BF16)
        CB = lax.dot_general(Cb, Bb, NT, preferred_element_type=F32)
        ys = []
        for hg in range(HG):
            xh = x[:, P * hg:P * (hg + 1)]
            xt = xh * dt_c[:, hg:hg + 1]
            csc, csr = cs_c[:, hg:hg + 1], cs_r[hg:hg + 1, :]
            L = jnp.where(row >= col, jnp.exp(jnp.minimum(csc - csr, 0.0)), 0.0)
            ydiag = jnp.dot((CB * L).astype(BF16), xt.astype(BF16), preferred_element_type=F32)
            Sh = state[hg]
            yoff = lax.dot_general(Cb, Sh.astype(BF16), NT, preferred_element_type=F32) * jnp.exp(csc)
            ys.append(ydiag + yoff + pr[2:3, hg:hg + 1] * xh)
            st_ref[0, 0, 0, P * hg:P * (hg + 1), :] = Sh
            csq = csc[Q - 1:Q, :]
            upd = lax.dot_general((xt * jnp.exp(csq - csc)).astype(BF16), Bb, TN, preferred_element_type=F32)
            state[hg] = Sh * jnp.exp(csq) + upd
        y_ref[0] = jnp.concatenate([jnp.concatenate(ys[0:2], axis=-1), jnp.concatenate(ys[2:4], axis=-1)], axis=-1)

    xo, bo, co = 0, SSM_D_INNER // N, (SSM_D_INNER + SSM_BC_DIM) // N
    return pl.pallas_call(
        body, out_shape=(SDS((B, S, SSM_D_INNER), F32), SDS((B, G, nc, xw, N), F32)), grid=(G, B, nc),
        in_specs=[pl.BlockSpec((1, Q, xw), lambda g, b, c: (b, c, g)), pl.BlockSpec((1, Q, N), lambda g, b, c: (b, c, bo + g)),
                  pl.BlockSpec((1, Q, N), lambda g, b, c: (b, c, co + g)), pl.BlockSpec((1, 1, Q, HG), lambda g, b, c: (b, g, c, 0)),
                  pl.BlockSpec((1, 1, HG, Q), lambda g, b, c: (b, g, 0, c)), pl.BlockSpec((1, 3, HG), lambda g, b, c: (g, 0, 0)),
                  pl.BlockSpec((1, HG, 3), lambda g, b, c: (g, 0, 0))],
        out_specs=(pl.BlockSpec((1, Q, xw), lambda g, b, c: (b, c, g)), pl.BlockSpec((1, 1, 1, xw, N), lambda g, b, c: (b, g, c, 0, 0))),
        scratch_shapes=[pltpu.VMEM((HG, P, N), F32)], name="b_ssd_fwd",
        compiler_params=_params(("parallel", "arbitrary", "arbitrary")),
    )(xc3, xc3, xc3, dtc, dtr, prow, pcol)


def _ssd_bwd(xc3, dtc, dtr, prow, pcol, states, dy3):
    B, S, _ = xc3.shape
    Q, G, HG, P, N = SSM_CHUNK, SSM_N_GROUPS, SSM_HG, HEAD_DIM, SSM_D_STATE
    nc = S // Q
    xw = HG * P

    def body(x_ref, b_ref, c_ref, dtc_ref, dtr_ref, pr_ref, pc_ref, st_ref, dy_ref,
             dx_ref, db_ref, dc_ref, ddt_ref, dpar_ref, dstate):
        bi, ci = pl.program_id(1), pl.program_id(2)

        @pl.when(ci == 0)
        def _():
            dstate[...] = jnp.zeros_like(dstate)

        x, Bm, Cm, dy = x_ref[0], b_ref[0], c_ref[0], dy_ref[0]
        pr = pr_ref[0]
        zc, dt_c, A_r, cs_c, cs_r, row, col, tril = _ssd_common(x, Bm, Cm, dtc_ref[0, 0], dtr_ref[0, 0], pr, pc_ref[0])
        Bb, Cb = Bm.astype(BF16), Cm.astype(BF16)
        CB = lax.dot_general(Cb, Bb, NT, preferred_element_type=F32)
        CBt = lax.dot_general(Bb, Cb, NT, preferred_element_type=F32)
        lane4 = lax.broadcasted_iota(jnp.int32, (Q, HG), 1)
        lane4r = lax.broadcasted_iota(jnp.int32, (1, HG), 1)
        rowq = lax.broadcasted_iota(jnp.int32, (Q, 1), 0)
        dB = jnp.zeros((Q, N), F32)
        dC = jnp.zeros((Q, N), F32)
        dcs4 = jnp.zeros((Q, HG), F32)
        dtx4 = jnp.zeros((Q, HG), F32)
        dD4 = jnp.zeros((1, HG), F32)
        dxts, xhs, dyhs = [], [], []
        for hg in range(HG):
            xh = x[:, P * hg:P * (hg + 1)]
            dyh = dy[:, P * hg:P * (hg + 1)]
            xt = xh * dt_c[:, hg:hg + 1]
            xtb, dyb = xt.astype(BF16), dyh.astype(BF16)
            csc, csr = cs_c[:, hg:hg + 1], cs_r[hg:hg + 1, :]
            L = jnp.where(row >= col, jnp.exp(jnp.minimum(csc - csr, 0.0)), 0.0)
            Lt = jnp.where(col >= row, jnp.exp(jnp.minimum(csr - csc, 0.0)), 0.0)
            M, Mt = CB * L, CBt * Lt
            Sh = st_ref[0, 0, 0, P * hg:P * (hg + 1), :]
            dSh = dstate[hg]
            Shb, dShb = Sh.astype(BF16), dSh.astype(BF16)
            ecs = jnp.exp(csc)
            csq = csc[Q - 1:Q, :]
            dec = jnp.exp(csq - csc)
            dxt = jnp.dot(Mt.astype(BF16), dyb, preferred_element_type=F32)
            dxt = dxt + lax.dot_general(Bb, dShb, NT, preferred_element_type=F32) * dec
            Gm = lax.dot_general(dyb, xtb, NT, preferred_element_type=F32)
            Gt = lax.dot_general(xtb, dyb, NT, preferred_element_type=F32)
            dC = dC + jnp.dot((Gm * L).astype(BF16), Bb, preferred_element_type=F32)
            dB = dB + jnp.dot((Gt * Lt).astype(BF16), Cb, preferred_element_type=F32)
            dC = dC + jnp.dot(dyb, Shb, preferred_element_type=F32) * ecs
            dBst = jnp.dot(xtb, dShb, preferred_element_type=F32) * dec
            dB = dB + dBst
            dcs = jnp.sum(Gm * M, axis=1, keepdims=True) - jnp.sum(Gt * Mt, axis=1, keepdims=True)
            yoff = lax.dot_general(Cb, Shb, NT, preferred_element_type=F32) * ecs
            dcs = dcs + jnp.sum(yoff * dyh, axis=1, keepdims=True)
            r = jnp.sum(dBst * Bm, axis=1, keepdims=True)
            dcs = dcs - r
            extra = jnp.sum(r, axis=0, keepdims=True) + jnp.exp(csq) * jnp.sum(
                jnp.sum(dSh * Sh, axis=1, keepdims=True), axis=0, keepdims=True)
            dcs = dcs + jnp.where(rowq == Q - 1, extra, 0.0)
            dcs4 = jnp.where(lane4 == hg, dcs, dcs4)
            dtx4 = jnp.where(lane4 == hg, jnp.sum(dxt * xh, axis=1, keepdims=True), dtx4)
            dD4 = jnp.where(lane4r == hg, jnp.sum(jnp.sum(dyh * xh, axis=1, keepdims=True), axis=0, keepdims=True), dD4)
            dstate[hg] = dSh * jnp.exp(csq) + lax.dot_general((dyh * ecs).astype(BF16), Cb, TN, preferred_element_type=F32)
            dxts.append(dxt)
            xhs.append(xh)
            dyhs.append(dyh)
        da4 = lax.dot_general(tril, dcs4, TN, preferred_element_type=F32, precision=HI)
        ddt4 = da4 * A_r + dtx4
        ddtraw = ddt4 * jax.nn.sigmoid(zc)
        ddt_ref[0, 0] = ddtraw
        dxs = [dxts[hg] * dt_c[:, hg:hg + 1] + pr[2:3, hg:hg + 1] * dyhs[hg] for hg in range(HG)]
        dx_ref[0] = jnp.concatenate([jnp.concatenate(dxs[0:2], axis=-1), jnp.concatenate(dxs[2:4], axis=-1)], axis=-1)
        db_ref[0] = dB
        dc_ref[0] = dC
        d_bias = jnp.sum(ddtraw, axis=0, keepdims=True)
        d_alog = jnp.sum(da4 * dt_c, axis=0, keepdims=True) * A_r
        r3 = lax.broadcasted_iota(jnp.int32, (3, HG), 0)
        dpar = jnp.where(r3 == 0, d_bias, jnp.where(r3 == 1, d_alog, dD4))
        first = (bi == 0) & (ci == 0)

        @pl.when(first)
        def _():
            dpar_ref[0] = dpar

        @pl.when(jnp.logical_not(first))
        def _():
            dpar_ref[0] += dpar

    rc = lambda c: nc - 1 - c
    bo, co = SSM_D_INNER // N, (SSM_D_INNER + SSM_BC_DIM) // N
    return pl.pallas_call(
        body,
        out_shape=(SDS((B, S, SSM_D_INNER), F32), SDS((B, S, SSM_BC_DIM), F32), SDS((B, S, SSM_BC_DIM), F32),
                   SDS((B, G, S, HG), F32), SDS((G, 3, HG), F32)),
        grid=(G, B, nc),
        in_specs=[pl.BlockSpec((1, Q, xw), lambda g, b, c: (b, rc(c), g)), pl.BlockSpec((1, Q, N), lambda g, b, c: (b, rc(c), bo + g)),
                  pl.BlockSpec((1, Q, N), lambda g, b, c: (b, rc(c), co + g)), pl.BlockSpec((1, 1, Q, HG), lambda g, b, c: (b, g, rc(c), 0)),
                  pl.BlockSpec((1, 1, HG, Q), lambda g, b, c: (b, g, 0, rc(c))), pl.BlockSpec((1, 3, HG), lambda g, b, c: (g, 0, 0)),
                  pl.BlockSpec((1, HG, 3), lambda g, b, c: (g, 0, 0)),
                  pl.BlockSpec((1, 1, 1, xw, N), lambda g, b, c: (b, g, rc(c), 0, 0)), pl.BlockSpec((1, Q, xw), lambda g, b, c: (b, rc(c), g))],
        out_specs=(pl.BlockSpec((1, Q, xw), lambda g, b, c: (b, rc(c), g)), pl.BlockSpec((1, Q, N), lambda g, b, c: (b, rc(c), g)),
                   pl.BlockSpec((1, Q, N), lambda g, b, c: (b, rc(c), g)), pl.BlockSpec((1, 1, Q, HG), lambda g, b, c: (b, g, rc(c), 0)),
                   pl.BlockSpec((1, 3, HG), lambda g, b, c: (g, 0, 0))),
        scratch_shapes=[pltpu.VMEM((HG, P, N), F32)], name="b_ssd_bwd",
        compiler_params=_params(("parallel", "arbitrary", "arbitrary")),
    )(xc3, xc3, xc3, dtc, dtr, prow, pcol, states, dy3)


GN_W = SSM_D_INNER // SSM_N_GROUPS


def _gate_fwd(y, zx, nw):
    T = y.shape[0]
    tm = _pick(T, 256, 8)

    def body(y_ref, z_ref, w_ref, o_ref):
        z = z_ref[...]
        gt = y_ref[...] * (z * jax.nn.sigmoid(z))
        outs = []
        for k in range(SSM_N_GROUPS):
            gk = gt[:, GN_W * k:GN_W * (k + 1)]
            outs.append(gk * lax.rsqrt(jnp.mean(gk * gk, axis=-1, keepdims=True) + NORM_EPS))
        o_ref[...] = (jnp.concatenate(outs, axis=-1) * w_ref[...]).astype(BF16)

    row = pl.BlockSpec((tm, SSM_D_INNER), lambda i: (i, 0))
    return pl.pallas_call(
        body, out_shape=SDS((T, SSM_D_INNER), BF16), grid=(T // tm,),
        in_specs=[row, row, pl.BlockSpec((1, SSM_D_INNER), lambda i: (0, 0))], out_specs=row, name="b_gate_fwd",
        compiler_params=_params(("parallel",)),
    )(y, zx, nw)


def _gate_bwd(dgn, y, zx, nw):
    T = y.shape[0]
    tm = _pick(T, 256, 8)

    def body(d_ref, y_ref, z_ref, w_ref, dy_ref, dz_ref, dw_ref):
        z, yv, w = z_ref[...], y_ref[...], w_ref[...]
        sg = jax.nn.sigmoid(z)
        sz = z * sg
        gt = yv * sz
        gw = d_ref[...] * w
        dgts, dws = [], []
        for k in range(SSM_N_GROUPS):
            sl = slice(GN_W * k, GN_W * (k + 1))
            gk, gwk = gt[:, sl], gw[:, sl]
            rstd = lax.rsqrt(jnp.mean(gk * gk, axis=-1, keepdims=True) + NORM_EPS)
            dgts.append(rstd * gwk - gk * (rstd * rstd * rstd) * jnp.mean(gwk * gk, axis=-1, keepdims=True))
            dws.append(jnp.sum(d_ref[:, sl] * gk * rstd, axis=0, keepdims=True))
        dgt = jnp.concatenate(dgts, axis=-1)
        dy_ref[...] = dgt * sz
        dz_ref[...] = (dgt * yv * (sg * (1.0 + z * (1.0 - sg)))).astype(BF16)
        dw = jnp.concatenate(dws, axis=-1)

        @pl.when(pl.program_id(0) == 0)
        def _():
            dw_ref[...] = dw

        @pl.when(pl.program_id(0) > 0)
        def _():
            dw_ref[...] += dw

    row = pl.BlockSpec((tm, SSM_D_INNER), lambda i: (i, 0))
    vec = pl.BlockSpec((1, SSM_D_INNER), lambda i: (0, 0))
    return pl.pallas_call(
        body, out_shape=(SDS((T, SSM_D_INNER), F32), SDS((T, SSM_D_INNER), BF16), SDS((1, SSM_D_INNER), F32)), grid=(T // tm,),
        in_specs=[row, row, row, vec], out_specs=(row, row, vec), name="b_gate_bwd", compiler_params=_params(("arbitrary",)),
    )(dgn, y, zx, nw)


MESH = pl.DeviceIdType.MESH
ANY = pl.BlockSpec(memory_space=pl.ANY)


def _all_gather(x, name):
    rows, w = x.shape

    def body(x_ref, out_ref, send_sems, recv_sems, local_sem):
        px, py, pc = lax.axis_index("x"), lax.axis_index("y"), lax.axis_index("c")
        me, sibling = (px, py, pc), (px, py, 1 - pc)
        chips = [(1 - px, py), (px, 1 - py), (1 - px, 1 - py)]

        def slot(qx, qy, qc):
            return out_ref.at[4 * qx + 2 * qy + qc]

        def copy(k, block, to, src=None):
            return pltpu.make_async_remote_copy(
                src_ref=slot(*block) if src is None else src, dst_ref=slot(*block),
                send_sem=send_sems.at[k], recv_sem=recv_sems.at[k], device_id=to, device_id_type=MESH)

        mine = pltpu.make_async_copy(x_ref, slot(*me), local_sem)
        mine.start()
        first = [copy(0, me, sibling, src=x_ref)]
        first += [copy(1 + j, me, (*chip, pc), src=x_ref) for j, chip in enumerate(chips)]
        for cp in first:
            cp.start()
        passed = [copy(4 + j, (*chip, pc), sibling) for j, chip in enumerate(chips)]
        for j, chip in enumerate(chips):
            copy(1 + j, (*chip, pc), me).wait_recv()
            passed[j].start()
        copy(0, sibling, me).wait_recv()
        for j, chip in enumerate(chips):
            copy(4 + j, (*chip, 1 - pc), me).wait_recv()
        for cp in first + passed:
            cp.wait_send()
        mine.wait()

    return pl.pallas_call(
        body, out_shape=SDS((N_DEV, rows, w), x.dtype), in_specs=[ANY], out_specs=ANY,
        scratch_shapes=[pltpu.SemaphoreType.DMA((7,)), pltpu.SemaphoreType.DMA((7,)), pltpu.SemaphoreType.DMA(())],
        name=name,
    )(x)


def _all_to_all(p, name):
    _, rows, w = p.shape

    def body(p_ref, r_ref, send_sems, recv_sems, local_sem):
        px, py, pc = lax.axis_index("x"), lax.axis_index("y"), lax.axis_index("c")
        me_id = 4 * px + 2 * py + pc
        mine = pltpu.make_async_copy(p_ref.at[me_id], r_ref.at[me_id], local_sem)
        mine.start()
        copies = []
        for k in range(1, N_DEV):
            qx, qy, qc = px ^ ((k >> 2) & 1), py ^ ((k >> 1) & 1), pc ^ (k & 1)
            peer_id = 4 * qx + 2 * qy + qc
            send = pltpu.make_async_remote_copy(
                src_ref=p_ref.at[peer_id], dst_ref=r_ref.at[me_id], send_sem=send_sems.at[k - 1],
                recv_sem=recv_sems.at[k - 1], device_id=(qx, qy, qc), device_id_type=MESH)
            recv = pltpu.make_async_remote_copy(
                src_ref=p_ref.at[peer_id], dst_ref=r_ref.at[peer_id], send_sem=send_sems.at[k - 1],
                recv_sem=recv_sems.at[k - 1], device_id=(qx, qy, qc), device_id_type=MESH)
            copies.append((send, recv))
        for send, _ in copies:
            send.start()
        for _, recv in copies:
            recv.wait_recv()
        for send, _ in copies:
            send.wait_send()
        mine.wait()

    return pl.pallas_call(
        body, out_shape=SDS(p.shape, p.dtype), in_specs=[ANY], out_specs=ANY,
        scratch_shapes=[pltpu.SemaphoreType.DMA((7,)), pltpu.SemaphoreType.DMA((7,)), pltpu.SemaphoreType.DMA(())],
        name=name,
    )(p)


def _adamw(recv, w, m, v):
    rows = w.shape[0]
    tr = _pick(rows, 1024, 8)
    c1 = 1.0 - ADAM_B1 ** ADAM_STEP
    c2 = 1.0 - ADAM_B2 ** ADAM_STEP

    def body(r_ref, w_ref, m_ref, v_ref, g_ref, d_ref, nm_ref, nv_ref):
        g = r_ref[0]
        for i in range(1, N_DEV):
            g = g + r_ref[i]
        nm = ADAM_B1 * m_ref[...] + (1.0 - ADAM_B1) * g
        nv = ADAM_B2 * v_ref[...] + (1.0 - ADAM_B2) * (g * g)
        g_ref[...] = g
        nm_ref[...] = nm
        nv_ref[...] = nv
        d_ref[...] = -ADAM_LR * ((nm / c1) / (jnp.sqrt(nv / c2) + ADAM_EPS) + ADAM_WD * w_ref[...])

    row = pl.BlockSpec((tr, LANES), lambda i: (i, 0))
    out = SDS((rows, LANES), F32)
    return pl.pallas_call(
        body, out_shape=(out, out, out, out), grid=(rows // tr,),
        in_specs=[pl.BlockSpec((N_DEV, tr, LANES), lambda i: (0, i, 0)), row, row, row], out_specs=(row, row, row, row),
        name="adamw", compiler_params=_params(("parallel",)),
    )(recv, w, m, v)


W_SPEC = (
    ("norm_mix_w", None), ("norm_mlp_w", None), ("a_w_qkv", 2), ("a_b_qkv", 1), ("a_sinks", None), ("a_w_o", 1),
    ("a_b_o", 1), ("b_in_w", 2), ("b_conv_w", 2), ("b_conv_b", None), ("b_dt_bias", None), ("b_a_log", None),
    ("b_d", None), ("b_norm_w", None), ("b_out_w", 1), ("c_w_qkv", 2), ("c_w_o", 1), ("mlp_w_up", 2),
    ("mlp_w_down", 1), ("final_norm_w", None))
W_NAMES = tuple(n for n, _ in W_SPEC)
W_AXIS = dict(W_SPEC)
GATHER_BF16 = ("a_w_qkv", "a_w_o", "b_in_w", "b_out_w", "c_w_qkv", "c_w_o", "mlp_w_up", "mlp_w_down")
GATHER_F32 = ("a_b_qkv", "a_b_o", "b_conv_w")
PACK_ROWS = 1024


def _pack_rows(parts, dtype, axis_len=None):
    ax = 0 if axis_len is None else 1
    flat = jnp.concatenate([p.astype(dtype) for p in parts], axis=ax)
    n = flat.shape[ax]
    rows = -(-n // (LANES * PACK_ROWS)) * PACK_ROWS
    pad = rows * LANES - n
    if axis_len is None:
        return jnp.pad(flat, (0, pad)).reshape(rows, LANES)
    return jnp.pad(flat, ((0, 0), (0, pad))).reshape(axis_len, rows, LANES)


def _full_shape(shard, ax):
    if ax is None:
        return shard.shape
    return shard.shape[:ax] + (N_DEV * shard.shape[ax],) + shard.shape[ax + 1:]


def _gather_weights(shards, names, dtype, name):
    packed = _pack_rows([shards[n].reshape(-1) for n in names], dtype)
    got = _all_gather(packed, name).reshape(N_DEV, -1)
    full, off = {}, 0
    for n in names:
        s = shards[n]
        blk = got[:, off:off + s.size].reshape((N_DEV,) + s.shape)
        off += s.size
        ax = W_AXIS[n]
        full[n] = jnp.moveaxis(blk, 0, ax).reshape(_full_shape(s, ax))
    return full


def _to_slots(g, ax):
    if ax is None:
        return jnp.broadcast_to(g.reshape(1, -1), (N_DEV, g.size))
    shp = g.shape
    t = g.reshape(shp[:ax] + (N_DEV, shp[ax] // N_DEV) + shp[ax + 1:])
    return jnp.moveaxis(t, ax, 0).reshape(N_DEV, -1)


def _rope_tables(positions):
    half = HEAD_DIM // 2
    inv = ROPE_THETA ** (-jnp.arange(half, dtype=F32) / half)
    ang = positions.astype(F32)[..., None] * inv
    rep = LANES // half
    cos = jnp.tile(jnp.cos(ang), (1, 1, rep))
    sin = jnp.tile(jnp.sin(ang), (1, 1, rep))
    T = positions.shape[0] * positions.shape[1]
    return cos.reshape(T, LANES), sin.reshape(T, LANES)


def _swa_fwd(u, h, p, j, B, S, cos, sin, tag):
    qkv = _matmul(u, p["a_w_qkv"][j], out_dtype=BF16, bias=p["a_b_qkv"][j][None], rope=(cos, sin),
                  rope_cols=A_Q_DIM + A_KV_DIM, tn=640, name=f"{tag}_qkv")
    o, lse = _attn_fwd(qkv, B, S, 1, n_heads=A_N_HEADS, n_kv=A_N_KV, q_col=0, k_col=A_Q_DIM, v_col=A_Q_DIM + A_KV_DIM,
                       max_dist=A_WINDOW - 1, sinks=p["a_sinks"][j], name=f"{tag}_attn")
    h1 = _matmul(o, p["a_w_o"][j], bias=p["a_b_o"][j][None], resid=h, name=f"{tag}_o")
    return h1, (qkv, o, lse)


def _swa_bwd(dh1, u, saved, p, j, B, S, cos, sin, tag):
    qkv, o, lse = saved
    kw = dict(n_heads=A_N_HEADS, n_kv=A_N_KV, q_col=0, k_col=A_Q_DIM, v_col=A_Q_DIM + A_KV_DIM, max_dist=A_WINDOW - 1)
    g = {}
    do = _matmul(dh1, p["a_w_o"][j], tb=True, name=f"{tag}_do")
    g["a_w_o"] = _matmul(o, dh1, ta=True, name=f"{tag}_dwo")
    g["a_b_o"] = _colsum(dh1, f"{tag}_dbo")[0]
    sk = jnp.pad(p["a_sinks"][j], (0, LANES - A_N_HEADS))[None]
    delta, dob, dsink = _delta(do, o, lse, sk, name=f"{tag}_delta")
    g["a_sinks"] = dsink[0, :A_N_HEADS]
    dq = _attn_dq(qkv, dob, lse, delta, cos, sin, B, S, 1, name=f"{tag}_dq", **kw)
    dk, dv = _attn_dkv(qkv, dob, lse, delta, cos, sin, B, S, 1, name=f"{tag}_dkv", **kw)
    dqkv = jnp.concatenate([dq, dk, dv], axis=1)
    g["a_w_qkv"] = _matmul(u, dqkv, ta=True, tn=640, name=f"{tag}_dwqkv")
    g["a_b_qkv"] = _colsum(dqkv, f"{tag}_dbqkv")[0]
    du = _matmul(dqkv, p["a_w_qkv"][j], tb=True, tk=640, name=f"{tag}_du")
    return du, g


def _dil_fwd(u, h, p, B, S, cos, sin):
    W = C_HEADS * HEAD_DIM
    qkv = _matmul(u, p["c_w_qkv"][0], out_dtype=BF16, rope=(cos, sin), rope_cols=6 * W, name="c_qkv")
    os_, lses = [], []
    for gi, (window, dil) in enumerate(C_PATTERNS):
        o, lse = _attn_fwd(qkv, B, S, dil, n_heads=C_HEADS, n_kv=C_HEADS, q_col=gi * W, k_col=(3 + gi) * W,
                           v_col=(6 + gi) * W, max_dist=window // dil, sinks=None, name=f"c_attn{gi}")
        os_.append(o)
        lses.append(lse)
    o, lse = _merge(os_, lses)
    h1 = _matmul(o, p["c_w_o"][0], resid=h, name="c_o")
    return h1, (qkv, o, lse)


def _dil_bwd(dh1, u, saved, p, B, S, cos, sin):
    W = C_HEADS * HEAD_DIM
    qkv, o, lse = saved
    g = {}
    do = _matmul(dh1, p["c_w_o"][0], tb=True, name="c_do")
    g["c_w_o"] = _matmul(o, dh1, ta=True, name="c_dwo")[None]
    delta, dob = _delta(do, o, name="c_delta")
    dqs, dks, dvs = [], [], []
    for gi, (window, dil) in enumerate(C_PATTERNS):
        kw = dict(n_heads=C_HEADS, n_kv=C_HEADS, q_col=gi * W, k_col=(3 + gi) * W, v_col=(6 + gi) * W, max_dist=window // dil)
        dqs.append(_attn_dq(qkv, dob, lse, delta, cos, sin, B, S, dil, name=f"c_dq{gi}", **kw))
        dk, dv = _attn_dkv(qkv, dob, lse, delta, cos, sin, B, S, dil, name=f"c_dkv{gi}", **kw)
        dks.append(dk)
        dvs.append(dv)
    dqkv = jnp.concatenate(dqs + dks + dvs, axis=1)
    g["c_w_qkv"] = _matmul(u, dqkv, ta=True, name="c_dwqkv")[None]
    du = _matmul(dqkv, p["c_w_qkv"][0], tb=True, name="c_du")
    return du, g


def _ssm_params(p):
    par = jnp.stack([p["b_dt_bias"][0], p["b_a_log"][0], p["b_d"][0]], axis=0)
    prow = par.reshape(3, SSM_N_GROUPS, SSM_HG).transpose(1, 0, 2)
    return prow, prow.transpose(0, 2, 1)


def _mamba_fwd(u, h, p, B, S):
    T = B * S
    G, HG = SSM_N_GROUPS, SSM_HG
    w_in = p["b_in_w"][0]
    nzx = SSM_D_INNER + SSM_CONV_DIM
    w_dt = jnp.pad(w_in[:, nzx:], ((0, 0), (0, LANES - SSM_N_HEADS)))
    zx = _matmul(u, w_in[:, :nzx], name="b_zx")
    dtraw = _matmul(u, w_dt, name="b_dt")[:, :SSM_N_HEADS]
    dtc = dtraw.reshape(B, S, G, HG).transpose(0, 2, 1, 3)
    dtr = dtraw.reshape(B, S, G, HG).transpose(0, 2, 3, 1)
    prow, pcol = _ssm_params(p)
    zx3 = zx.reshape(B, S, nzx)
    xc3 = _conv_fwd(zx3, p["b_conv_w"][0], p["b_conv_b"])
    y3, states = _ssd_fwd(xc3, dtc, dtr, prow, pcol)
    y = y3.reshape(T, SSM_D_INNER)
    gn = _gate_fwd(y, zx, p["b_norm_w"])
    h1 = _matmul(gn, p["b_out_w"][0], resid=h, name="b_out")
    return h1, (zx, dtc, dtr, xc3, y, states, gn, w_dt)


def _mamba_bwd(dh1, u, saved, p, B, S):
    T = B * S
    zx, dtc, dtr, xc3, y, states, gn, w_dt = saved
    nzx = SSM_D_INNER + SSM_CONV_DIM
    w_in = p["b_in_w"][0]
    prow, pcol = _ssm_params(p)
    g = {}
    dgn = _matmul(dh1, p["b_out_w"][0], tb=True, name="b_dgn")
    g["b_out_w"] = _matmul(gn, dh1, ta=True, name="b_dwout")[None]
    dy, dz, dnw = _gate_bwd(dgn, y, zx, p["b_norm_w"])
    g["b_norm_w"] = dnw
    dx3, dB3, dC3, ddt, dpar = _ssd_bwd(xc3, dtc, dtr, prow, pcol, states, dy.reshape(B, S, SSM_D_INNER))
    dpar = dpar.transpose(1, 0, 2).reshape(3, SSM_N_HEADS)
    g["b_dt_bias"], g["b_a_log"], g["b_d"] = dpar[0:1], dpar[1:2], dpar[2:3]
    zx3 = zx.reshape(B, S, nzx)
    cw, cb = p["b_conv_w"][0], p["b_conv_b"]
    parts, dws, dbs = [], [], []
    for col0, dpart, nm in ((0, dx3, "b_conv_bwd_x"), (SSM_D_INNER, dB3, "b_conv_bwd_b"),
                            (SSM_D_INNER + SSM_BC_DIM, dC3, "b_conv_bwd_c")):
        dxp, dw, db = _conv_bwd(zx3, dpart, cw, cb, col0, nm)
        parts.append(dxp.reshape(T, -1))
        dws.append(dw)
        dbs.append(db)
    g["b_conv_w"] = jnp.concatenate(dws, axis=1)[None]
    g["b_conv_b"] = jnp.concatenate(dbs, axis=1)
    dzx = jnp.concatenate([dz] + parts, axis=1)
    ddtraw = ddt.transpose(0, 2, 1, 3).reshape(T, SSM_N_HEADS)
    ddtp = jnp.pad(ddtraw, ((0, 0), (0, LANES - SSM_N_HEADS)))
    dw_zx = _matmul(u, dzx, ta=True, name="b_dwzx")
    dw_dt = _matmul(u, ddtp, ta=True, name="b_dwdt")[:, :SSM_N_HEADS]
    g["b_in_w"] = jnp.concatenate([dw_zx, dw_dt], axis=1)[None]
    du = _matmul(dzx, w_in[:, :nzx], tb=True, name="b_du_zx")
    du = _matmul(ddtp, w_dt, tb=True, resid=du, name="b_du_dt")
    return du, g


def _local_step(x, positions, p, target):
    B, S, D = x.shape
    T = B * S
    cos, sin = _rope_tables(positions)
    h = x.reshape(T, D)
    tape = []
    for i in range(DEPTH):
        kind, j = i % 3, i // 3
        u = _rmsnorm_fwd(h, p["norm_mix_w"][i], f"l{i}_norm_mix")
        if kind == 0:
            h1, saved = _swa_fwd(u, h, p, j, B, S, cos, sin, f"a{j}")
        elif kind == 1:
            h1, saved = _mamba_fwd(u, h, p, B, S)
        else:
            h1, saved = _dil_fwd(u, h, p, B, S, cos, sin)
        u2 = _rmsnorm_fwd(h1, p["norm_mlp_w"][i], f"l{i}_norm_mlp")
        r, s = _matmul(u2, p["mlp_w_up"][i], out_dtype=BF16, relu2=True, name=f"l{i}_up")
        h2 = _matmul(s, p["mlp_w_down"][i], resid=h1, name=f"l{i}_down")
        tape.append((h, u, saved, h1, u2, r, s))
        h = h2
    dh, dwf, loss = _final_loss(h, target.reshape(T, D), p["final_norm_w"])
    grads = {"final_norm_w": dwf[0]}
    per_layer = {n: [None] * DEPTH for n in ("norm_mix_w", "norm_mlp_w", "mlp_w_up", "mlp_w_down")}
    a_grads = [None, None]
    for i in reversed(range(DEPTH)):
        kind, j = i % 3, i // 3
        h0, u, saved, h1, u2, r, s = tape[i]
        da = _matmul(dh, p["mlp_w_down"][i], tb=True, out_dtype=BF16, mul=r, mul_scale=2.0, name=f"l{i}_da")
        per_layer["mlp_w_down"][i] = _matmul(s, dh, ta=True, name=f"l{i}_dwdown")
        per_layer["mlp_w_up"][i] = _matmul(u2, da, ta=True, name=f"l{i}_dwup")
        du2 = _matmul(da, p["mlp_w_up"][i], tb=True, name=f"l{i}_du2")
        dh1, dnw = _rmsnorm_bwd(h1, du2, p["norm_mlp_w"][i], dh, f"l{i}_norm_mlp_bwd")
        per_layer["norm_mlp_w"][i] = dnw[0]
        if kind == 0:
            du, g = _swa_bwd(dh1, u, saved, p, j, B, S, cos, sin, f"a{j}")
            a_grads[j] = g
        elif kind == 1:
            du, g = _mamba_bwd(dh1, u, saved, p, B, S)
            grads.update(g)
        else:
            du, g = _dil_bwd(dh1, u, saved, p, B, S, cos, sin)
            grads.update(g)
        dh, dnw = _rmsnorm_bwd(h0, du, p["norm_mix_w"][i], dh1, f"l{i}_norm_mix_bwd")
        per_layer["norm_mix_w"][i] = dnw[0]
    for n, lst in per_layer.items():
        grads[n] = jnp.stack(lst, axis=0)
    for n in ("a_w_qkv", "a_b_qkv", "a_sinks", "a_w_o", "a_b_o"):
        grads[n] = jnp.stack([a_grads[0][n], a_grads[1][n]], axis=0)
    return loss, dh.reshape(B, S, D), grads


def kernel(x, positions, norm_mix_w, norm_mlp_w, a_w_qkv, a_b_qkv, a_sinks, a_w_o, a_b_o, b_in_w, b_conv_w, b_conv_b, b_dt_bias, b_a_log, b_d, b_norm_w, b_out_w, c_w_qkv, c_w_o, mlp_w_up, mlp_w_down, final_norm_w, loss_target, m_norm_mix_w, m_norm_mlp_w, m_a_w_qkv, m_a_b_qkv, m_a_sinks, m_a_w_o, m_a_b_o, m_b_in_w, m_b_conv_w, m_b_conv_b, m_b_dt_bias, m_b_a_log, m_b_d, m_b_norm_w, m_b_out_w, m_c_w_qkv, m_c_w_o, m_mlp_w_up, m_mlp_w_down, m_final_norm_w, v_norm_mix_w, v_norm_mlp_w, v_a_w_qkv, v_a_b_qkv, v_a_sinks, v_a_w_o, v_a_b_o, v_b_in_w, v_b_conv_w, v_b_conv_b, v_b_dt_bias, v_b_a_log, v_b_d, v_b_norm_w, v_b_out_w, v_c_w_qkv, v_c_w_o, v_mlp_w_up, v_mlp_w_down, v_final_norm_w):
    w = dict(zip(W_NAMES, (norm_mix_w, norm_mlp_w, a_w_qkv, a_b_qkv, a_sinks, a_w_o, a_b_o, b_in_w, b_conv_w, b_conv_b,
                           b_dt_bias, b_a_log, b_d, b_norm_w, b_out_w, c_w_qkv, c_w_o, mlp_w_up, mlp_w_down, final_norm_w)))
    m = dict(zip(W_NAMES, (m_norm_mix_w, m_norm_mlp_w, m_a_w_qkv, m_a_b_qkv, m_a_sinks, m_a_w_o, m_a_b_o, m_b_in_w,
                           m_b_conv_w, m_b_conv_b, m_b_dt_bias, m_b_a_log, m_b_d, m_b_norm_w, m_b_out_w, m_c_w_qkv, m_c_w_o,
                           m_mlp_w_up, m_mlp_w_down, m_final_norm_w)))
    v = dict(zip(W_NAMES, (v_norm_mix_w, v_norm_mlp_w, v_a_w_qkv, v_a_b_qkv, v_a_sinks, v_a_w_o, v_a_b_o, v_b_in_w,
                           v_b_conv_w, v_b_conv_b, v_b_dt_bias, v_b_a_log, v_b_d, v_b_norm_w, v_b_out_w, v_c_w_qkv, v_c_w_o,
                           v_mlp_w_up, v_mlp_w_down, v_final_norm_w)))
    full = dict(w)
    full.update(_gather_weights(w, GATHER_BF16, BF16, "gather_bf16"))
    full.update(_gather_weights(w, GATHER_F32, F32, "gather_f32"))
    loss_part, dx, grads = _local_step(x, positions, full, loss_target)
    loss = lax.psum(loss_part[0, 0], AXES)

    send = _pack_rows([_to_slots(grads[n].reshape(_full_shape(w[n], W_AXIS[n])), W_AXIS[n]) for n in W_NAMES], F32, N_DEV)
    recv = _all_to_all(send, "grad_all_to_all")
    pk = lambda d: _pack_rows([d[n].reshape(-1) for n in W_NAMES], F32)
    outs = _adamw(recv, pk(w), pk(m), pk(v))
    flats = [o.reshape(-1) for o in outs]
    res = [[], [], [], []]
    off = 0
    for n in W_NAMES:
        for k in range(4):
            res[k].append(flats[k][off:off + w[n].size].reshape(w[n].shape))
        off += w[n].size
    return (loss, dx, *res[0], *res[1], *res[2], *res[3])
```

```python
import functools
import math

import jax
import jax.numpy as jnp
import numpy as np
from jax import lax
from jax.experimental import pallas as pl
from jax.experimental.pallas import tpu as pltpu

F32 = jnp.float32
BF16 = jnp.bfloat16
SDS = jax.ShapeDtypeStruct

D_MODEL = 1024
DEPTH = 4
BLOCK = 128
ROPE_THETA = 10000.0
NORM_EPS = 1e-5
HEAD_DIM = 64
A_N_HEADS = 16
A_N_KV = 2
A_WINDOW = 128
A_Q_DIM = 1024
A_KV_DIM = 128
SSM_D_INNER = 2048
SSM_N_HEADS = 32
SSM_N_GROUPS = 8
SSM_HG = 4
SSM_D_STATE = 128
SSM_CONV = 4
SSM_CHUNK = 128
SSM_BC_DIM = 1024
SSM_CONV_DIM = 4096
C_PATTERNS = ((128, 1), (512, 4), (2048, 16))
C_HEADS = 16
ADAM_LR, ADAM_B1, ADAM_B2, ADAM_EPS, ADAM_WD, ADAM_STEP = 0.001, 0.9, 0.999, 1e-08, 0.01, 10

N_DEV = 8
AXES = ("x", "y", "c")
LANES = 128
VMEM_LIMIT = 56 * 1024 * 1024
STREAM_VMEM = 16 * 1024 * 1024
NEG = -1e30

NN = (((1,), (0,)), ((), ()))
NT = (((1,), (1,)), ((), ()))
TN = (((0,), (0,)), ((), ()))
HI = lax.Precision.HIGHEST


def _pick(n, cap, mult=LANES):
    best = None
    for t in range(mult, min(n, cap) + 1, mult):
        if n % t == 0:
            best = t
    return best if best is not None else n


def _params(sem):
    return pltpu.CompilerParams(dimension_semantics=sem, vmem_limit_bytes=VMEM_LIMIT)


def _bf(x):
    return x if x.dtype == BF16 else x.astype(BF16)


def _rot_half(y):
    n = y.shape[-1]
    lane = lax.broadcasted_iota(jnp.int32, y.shape, y.ndim - 1)
    return jnp.where((lane % HEAD_DIM) < HEAD_DIM // 2, -pltpu.roll(y, n - 32, y.ndim - 1), pltpu.roll(y, 32, y.ndim - 1))


def _rope(y, cos, sin, sign):
    reps = y.shape[-1] // LANES
    c = jnp.tile(cos, (1, reps)) if reps > 1 else cos
    s = jnp.tile(sin, (1, reps)) if reps > 1 else sin
    return y * c + sign * (_rot_half(y) * s)


def _matmul(a, b, *, ta=False, tb=False, out_dtype=F32, bias=None, resid=None, mul=None, mul_scale=1.0,
            relu2=False, rope=None, rope_cols=0, tm=512, tn=1024, tk=1024, name="mm"):
    M = a.shape[1] if ta else a.shape[0]
    K = a.shape[0] if ta else a.shape[1]
    N = b.shape[0] if tb else b.shape[1]
    assert (b.shape[1] if tb else b.shape[0]) == K
    tm, tn, tk = _pick(M, tm), _pick(N, tn), _pick(K, tk)
    nk = K // tk
    dims = (((0 if ta else 1,), (1 if tb else 0,)), ((), ()))

    def body(*refs):
        it = iter(refs)
        a_ref, b_ref = next(it), next(it)
        bias_ref = next(it) if bias is not None else None
        resid_ref = next(it) if resid is not None else None
        mul_ref = next(it) if mul is not None else None
        cos_ref, sin_ref = (next(it), next(it)) if rope is not None else (None, None)
        o_ref = next(it)
        o2_ref = next(it) if relu2 else None
        acc_ref = next(it)
        k = pl.program_id(2)
        part = lax.dot_general(_bf(a_ref[...]), _bf(b_ref[...]), dims, preferred_element_type=F32)

        @pl.when(k == 0)
        def _():
            acc_ref[...] = part

        @pl.when(k > 0)
        def _():
            acc_ref[...] += part

        @pl.when(k == nk - 1)
        def _():
            y = acc_ref[...]
            if bias_ref is not None:
                y = y + bias_ref[...]
            if rope is not None:
                col = pl.program_id(1) * tn + lax.broadcasted_iota(jnp.int32, y.shape, 1)
                y = jnp.where(col < rope_cols, _rope(y, cos_ref[...], sin_ref[...], 1.0), y)
            if mul_ref is not None:
                y = y * (mul_ref[...].astype(F32) * mul_scale)
            if resid_ref is not None:
                y = y + resid_ref[...]
            if relu2:
                r = jnp.maximum(y, 0.0)
                o_ref[...] = r.astype(o_ref.dtype)
                o2_ref[...] = (r * r).astype(o2_ref.dtype)
            else:
                o_ref[...] = y.astype(o_ref.dtype)

    a_spec = pl.BlockSpec((tk, tm), lambda i, j, k: (k, i)) if ta else pl.BlockSpec((tm, tk), lambda i, j, k: (i, k))
    b_spec = pl.BlockSpec((tn, tk), lambda i, j, k: (j, k)) if tb else pl.BlockSpec((tk, tn), lambda i, j, k: (k, j))
    mn_spec = pl.BlockSpec((tm, tn), lambda i, j, k: (i, j))
    in_specs, args = [a_spec, b_spec], [a, b]
    if bias is not None:
        in_specs.append(pl.BlockSpec((1, tn), lambda i, j, k: (0, j)))
        args.append(bias)
    if resid is not None:
        in_specs.append(mn_spec)
        args.append(resid)
    if mul is not None:
        in_specs.append(mn_spec)
        args.append(mul)
    if rope is not None:
        in_specs += [pl.BlockSpec((tm, LANES), lambda i, j, k: (i, 0))] * 2
        args += [rope[0], rope[1]]
    out_shape = SDS((M, N), out_dtype)
    out_specs = mn_spec
    if relu2:
        out_shape, out_specs = (out_shape, out_shape), (mn_spec, mn_spec)
    return pl.pallas_call(
        body, out_shape=out_shape, grid=(M // tm, N // tn, nk), in_specs=in_specs, out_specs=out_specs,
        scratch_shapes=[pltpu.VMEM((tm, tn), F32)], name=name,
        compiler_params=_params(("parallel", "parallel", "arbitrary")),
    )(*args)


def _colsum(x, name):
    T, N = x.shape
    tm = _pick(T, 1024, 8)

    def body(x_ref, o_ref):
        s = jnp.sum(x_ref[...].astype(F32), axis=0, keepdims=True)

        @pl.when(pl.program_id(0) == 0)
        def _():
            o_ref[...] = s

        @pl.when(pl.program_id(0) > 0)
        def _():
            o_ref[...] += s

    return pl.pallas_call(
        body, out_shape=SDS((1, N), F32), grid=(T // tm,),
        in_specs=[pl.BlockSpec((tm, N), lambda i: (i, 0))], out_specs=pl.BlockSpec((1, N), lambda i: (0, 0)),
        name=name, compiler_params=_params(("arbitrary",)),
    )(x)


def _rmsnorm_fwd(h, w, name):
    T, D = h.shape
    tm = _pick(T, 512, 8)

    def body(h_ref, w_ref, o_ref):
        x = h_ref[...]
        rstd = lax.rsqrt(jnp.mean(x * x, axis=-1, keepdims=True) + NORM_EPS)
        o_ref[...] = (x * rstd * w_ref[...]).astype(BF16)

    return pl.pallas_call(
        body, out_shape=SDS((T, D), BF16), grid=(T // tm,),
        in_specs=[pl.BlockSpec((tm, D), lambda i: (i, 0)), pl.BlockSpec((1, D), lambda i: (0, 0))],
        out_specs=pl.BlockSpec((tm, D), lambda i: (i, 0)), name=name, compiler_params=_params(("parallel",)),
    )(h, w.reshape(1, D))


def _rmsnorm_bwd(h, du, w, dres, name):
    T, D = h.shape
    tm = _pick(T, 512, 8)

    def body(h_ref, du_ref, w_ref, dres_ref, dh_ref, dw_ref):
        x = h_ref[...]
        du_ = du_ref[...].astype(F32)
        rstd = lax.rsqrt(jnp.mean(x * x, axis=-1, keepdims=True) + NORM_EPS)
        g = du_ * w_ref[...]
        dh_ref[...] = dres_ref[...] + rstd * g - x * (rstd * rstd * rstd) * jnp.mean(g * x, axis=-1, keepdims=True)
        dw = jnp.sum(du_ * x * rstd, axis=0, keepdims=True)

        @pl.when(pl.program_id(0) == 0)
        def _():
            dw_ref[...] = dw

        @pl.when(pl.program_id(0) > 0)
        def _():
            dw_ref[...] += dw

    row = pl.BlockSpec((tm, D), lambda i: (i, 0))
    vec = pl.BlockSpec((1, D), lambda i: (0, 0))
    return pl.pallas_call(
        body, out_shape=(SDS((T, D), F32), SDS((1, D), F32)), grid=(T // tm,),
        in_specs=[row, row, vec, row], out_specs=(row, vec), name=name, compiler_params=_params(("arbitrary",)),
    )(h, du, w.reshape(1, D), dres)


def _final_loss(h, target, w):
    T, D = h.shape
    tm = _pick(T, 512, 8)

    def body(h_ref, t_ref, w_ref, dh_ref, dw_ref, loss_ref):
        x = h_ref[...]
        rstd = lax.rsqrt(jnp.mean(x * x, axis=-1, keepdims=True) + NORM_EPS)
        xn = x * rstd
        err = xn * w_ref[...] - t_ref[...]
        part = 0.5 * jnp.sum(jnp.mean(err * err, axis=-1, keepdims=True), axis=0, keepdims=True)
        dy = err * (1.0 / D)
        g = dy * w_ref[...]
        dh_ref[...] = rstd * g - x * (rstd * rstd * rstd) * jnp.mean(g * x, axis=-1, keepdims=True)
        dw = jnp.sum(dy * xn, axis=0, keepdims=True)
        lp = jnp.broadcast_to(part, (1, LANES))

        @pl.when(pl.program_id(0) == 0)
        def _():
            dw_ref[...] = dw
            loss_ref[...] = lp

        @pl.when(pl.program_id(0) > 0)
        def _():
            dw_ref[...] += dw
            loss_ref[...] += lp

    row = pl.BlockSpec((tm, D), lambda i: (i, 0))
    vec = pl.BlockSpec((1, D), lambda i: (0, 0))
    return pl.pallas_call(
        body, out_shape=(SDS((T, D), F32), SDS((1, D), F32), SDS((1, LANES), F32)), grid=(T // tm,),
        in_specs=[row, row, vec], out_specs=(row, vec, pl.BlockSpec((1, LANES), lambda i: (0, 0))),
        name="final_loss", compiler_params=_params(("arbitrary",)),
    )(h, target, w.reshape(1, D))


def _band_mask(i_blk, max_dist, first_ok):
    qi = lax.broadcasted_iota(jnp.int32, (BLOCK, 2 * BLOCK), 0)
    kj = lax.broadcasted_iota(jnp.int32, (BLOCK, 2 * BLOCK), 1)
    dist = qi + BLOCK - kj
    ok = (dist >= 0) & (dist <= max_dist)
    return ok & ((kj >= BLOCK) | first_ok)


def _head_cols(t, h):
    return t[:, HEAD_DIM * h:HEAD_DIM * (h + 1)]


def _lane_place(cols):
    m = cols[0].shape[0]
    lane = lax.broadcasted_iota(jnp.int32, (m, LANES), 1)
    out = jnp.zeros((m, LANES), F32)
    for h, c in enumerate(cols):
        out = jnp.where(lane == h, c, out)
    return out


def _attn_specs(B, S, d, C, n_heads, n_kv, q_col, k_col, v_col):
    kvw = n_kv * HEAD_DIM
    qw = n_heads * HEAD_DIM
    cq, ck = (C // qw if d > 1 else 0), (C // kvw if d > 1 else 0)
    q_spec = pl.BlockSpec((1, BLOCK, qw), lambda b, r, i: (b, i, r * cq + q_col // qw))
    kc = pl.BlockSpec((1, BLOCK, kvw), lambda b, r, i: (b, i, r * ck + k_col // kvw))
    kp = pl.BlockSpec((1, BLOCK, kvw), lambda b, r, i: (b, jnp.maximum(i - 1, 0), r * ck + k_col // kvw))
    vc = pl.BlockSpec((1, BLOCK, kvw), lambda b, r, i: (b, i, r * ck + v_col // kvw))
    vp = pl.BlockSpec((1, BLOCK, kvw), lambda b, r, i: (b, jnp.maximum(i - 1, 0), r * ck + v_col // kvw))
    return q_spec, kp, kc, vp, vc


def _attn_fwd(qkv, B, S, d, *, n_heads, n_kv, q_col, k_col, v_col, max_dist, sinks, name):
    C = qkv.shape[1]
    Ls = S // d
    nb = Ls // BLOCK
    qw = n_heads * HEAD_DIM
    R = n_heads // n_kv
    qkv3 = qkv.reshape(B, Ls, d * C)
    scale = HEAD_DIM ** -0.5

    def body(*refs):
        if sinks is not None:
            sink_ref, q_ref, kp_ref, kc_ref, vp_ref, vc_ref, o_ref, lse_ref = refs
        else:
            q_ref, kp_ref, kc_ref, vp_ref, vc_ref, o_ref, lse_ref = refs
        i = pl.program_id(2)
        mask = _band_mask(i, max_dist, i > 0)
        q = q_ref[0]
        kk = jnp.concatenate([kp_ref[0], kc_ref[0]], axis=0)
        vv = jnp.concatenate([vp_ref[0], vc_ref[0]], axis=0)
        lses, tiles = [], []
        for pair in range(n_heads // 2):
            outs = []
            for h in (2 * pair, 2 * pair + 1):
                g = h // R
                s = lax.dot_general(_head_cols(q, h), _head_cols(kk, g), NT, preferred_element_type=F32) * scale
                s = jnp.where(mask, s, NEG)
                m = jnp.max(s, axis=-1, keepdims=True)
                if sinks is not None:
                    sk = sink_ref[h]
                    m = jnp.maximum(m, sk)
                p = jnp.exp(s - m)
                den = jnp.sum(p, axis=-1, keepdims=True)
                if sinks is not None:
                    den = den + jnp.exp(sk - m)
                lses.append(m + jnp.log(den))
                pn = (p / den).astype(BF16)
                outs.append(jnp.dot(pn, _head_cols(vv, g), preferred_element_type=F32))
            tiles.append(jnp.concatenate(outs, axis=-1))
        o_ref[0] = jnp.concatenate(tiles, axis=-1)
        lse_ref[0] = _lane_place(lses)

    specs = list(_attn_specs(B, S, d, C, n_heads, n_kv, q_col, k_col, v_col))
    args = [qkv3] * 5
    if sinks is not None:
        specs = [pl.BlockSpec(memory_space=pltpu.SMEM)] + specs
        args = [sinks] + args
    o3, lse3 = pl.pallas_call(
        body, out_shape=(SDS((B, Ls, d * qw), F32), SDS((B, Ls, d * LANES), F32)), grid=(B, d, nb), in_specs=specs,
        out_specs=(pl.BlockSpec((1, BLOCK, qw), lambda b, r, i: (b, i, r)), pl.BlockSpec((1, BLOCK, LANES), lambda b, r, i: (b, i, r))),
        name=name, compiler_params=_params(("parallel", "parallel", "parallel")),
    )(*args)
    return o3.reshape(B * S, qw), lse3.reshape(B * S, LANES)


def _attn_dq(qkv, do, lse, delta, cos, sin, B, S, d, *, n_heads, n_kv, q_col, k_col, v_col, max_dist, name):
    C = qkv.shape[1]
    Ls = S // d
    nb = Ls // BLOCK
    qw = n_heads * HEAD_DIM
    R = n_heads // n_kv
    scale = HEAD_DIM ** -0.5

    def body(q_ref, kp_ref, kc_ref, vp_ref, vc_ref, do_ref, lse_ref, dl_ref, cos_ref, sin_ref, dq_ref):
        i = pl.program_id(2)
        mask = _band_mask(i, max_dist, i > 0)
        q = q_ref[0]
        do_ = do_ref[0]
        kk = jnp.concatenate([kp_ref[0], kc_ref[0]], axis=0)
        vv = jnp.concatenate([vp_ref[0], vc_ref[0]], axis=0)
        lse_t, dl_t = lse_ref[0], dl_ref[0]
        tiles = []
        for pair in range(n_heads // 2):
            outs = []
            for h in (2 * pair, 2 * pair + 1):
                g = h // R
                kh = _head_cols(kk, g)
                s = lax.dot_general(_head_cols(q, h), kh, NT, preferred_element_type=F32) * scale
                p = jnp.where(mask, jnp.exp(s - lse_t[:, h:h + 1]), 0.0)
                dp = lax.dot_general(_head_cols(do_, h), _head_cols(vv, g), NT, preferred_element_type=F32)
                ds = p * (dp - dl_t[:, h:h + 1])
                outs.append(jnp.dot(ds.astype(BF16), kh, preferred_element_type=F32) * scale)
            tiles.append(jnp.concatenate(outs, axis=-1))
        dq = jnp.concatenate(tiles, axis=-1)
        dq_ref[0] = _rope(dq, cos_ref[0], sin_ref[0], -1.0).astype(BF16)

    qs, kp, kc, vp, vc = _attn_specs(B, S, d, C, n_heads, n_kv, q_col, k_col, v_col)
    row_q = pl.BlockSpec((1, BLOCK, qw), lambda b, r, i: (b, i, r))
    row_l = pl.BlockSpec((1, BLOCK, LANES), lambda b, r, i: (b, i, r))
    qkv3 = qkv.reshape(B, Ls, d * C)
    v3 = lambda t, w: t.reshape(B, Ls, d * w)
    dq3 = pl.pallas_call(
        body, out_shape=SDS((B, Ls, d * qw), BF16), grid=(B, d, nb),
        in_specs=[qs, kp, kc, vp, vc, row_q, row_l, row_l, row_l, row_l], out_specs=row_q,
        name=name, compiler_params=_params(("parallel", "parallel", "parallel")),
    )(qkv3, qkv3, qkv3, qkv3, qkv3, v3(do, qw), v3(lse, LANES), v3(delta, LANES), v3(cos, LANES), v3(sin, LANES))
    return dq3.reshape(B * S, qw)


def _attn_dkv(qkv, do, lse, delta, cos, sin, B, S, d, *, n_heads, n_kv, q_col, k_col, v_col, max_dist, name):
    C = qkv.shape[1]
    Ls = S // d
    nb = Ls // BLOCK
    qw = n_heads * HEAD_DIM
    kvw = n_kv * HEAD_DIM
    R = n_heads // n_kv
    scale = HEAD_DIM ** -0.5
    cq, ck = (C // qw if d > 1 else 0), (C // kvw if d > 1 else 0)

    def body(k_ref, v_ref, q0_ref, q1_ref, do0_ref, do1_ref, lse0_ref, lse1_ref, dl0_ref, dl1_ref, cos_ref, sin_ref,
             dk_ref, dv_ref):
        j = pl.program_id(2)
        qi = lax.broadcasted_iota(jnp.int32, (BLOCK, BLOCK), 0)
        kj = lax.broadcasted_iota(jnp.int32, (BLOCK, BLOCK), 1)
        dist0 = qi - kj
        dist1 = qi + BLOCK - kj
        mask0 = (dist0 >= 0) & (dist0 <= max_dist)
        mask1 = (dist1 <= max_dist) & (j + 1 < nb)
        kb, vb = k_ref[0], v_ref[0]
        sides = ((q0_ref[0], do0_ref[0], lse0_ref[0], dl0_ref[0], mask0), (q1_ref[0], do1_ref[0], lse1_ref[0], dl1_ref[0], mask1))
        dks, dvs = [], []
        for g in range(n_kv):
            kh, vh = _head_cols(kb, g), _head_cols(vb, g)
            dk = jnp.zeros((BLOCK, HEAD_DIM), F32)
            dv = jnp.zeros((BLOCK, HEAD_DIM), F32)
            for h in range(g * R, (g + 1) * R):
                for (q, do_, lse_t, dl_t, mask) in sides:
                    qh, doh = _head_cols(q, h), _head_cols(do_, h)
                    s = lax.dot_general(qh, kh, NT, preferred_element_type=F32) * scale
                    p = jnp.where(mask, jnp.exp(s - lse_t[:, h:h + 1]), 0.0)
                    dp = lax.dot_general(doh, vh, NT, preferred_element_type=F32)
                    ds = p * (dp - dl_t[:, h:h + 1])
                    dv = dv + lax.dot_general(p.astype(BF16), doh, TN, preferred_element_type=F32)
                    dk = dk + lax.dot_general(ds.astype(BF16), qh, TN, preferred_element_type=F32) * scale
            dks.append(dk)
            dvs.append(dv)
        dk_t = jnp.concatenate([jnp.concatenate(dks[2 * t:2 * t + 2], axis=-1) for t in range(n_kv // 2)], axis=-1)
        dv_t = jnp.concatenate([jnp.concatenate(dvs[2 * t:2 * t + 2], axis=-1) for t in range(n_kv // 2)], axis=-1)
        dk_ref[0] = _rope(dk_t, cos_ref[0], sin_ref[0], -1.0).astype(BF16)
        dv_ref[0] = dv_t.astype(BF16)

    nxt = lambda j: jnp.minimum(j + 1, nb - 1)
    k_spec = pl.BlockSpec((1, BLOCK, kvw), lambda b, r, j: (b, j, r * ck + k_col // kvw))
    v_spec = pl.BlockSpec((1, BLOCK, kvw), lambda b, r, j: (b, j, r * ck + v_col // kvw))
    q0 = pl.BlockSpec((1, BLOCK, qw), lambda b, r, j: (b, j, r * cq + q_col // qw))
    q1 = pl.BlockSpec((1, BLOCK, qw), lambda b, r, j: (b, nxt(j), r * cq + q_col // qw))
    w0 = lambda w: pl.BlockSpec((1, BLOCK, w), lambda b, r, j: (b, j, r))
    w1 = lambda w: pl.BlockSpec((1, BLOCK, w), lambda b, r, j: (b, nxt(j), r))
    qkv3 = qkv.reshape(B, Ls, d * C)
    v3 = lambda t, w: t.reshape(B, Ls, d * w)
    do3, lse3, dl3 = v3(do, qw), v3(lse, LANES), v3(delta, LANES)
    dk3, dv3 = pl.pallas_call(
        body, out_shape=(SDS((B, Ls, d * kvw), BF16), SDS((B, Ls, d * kvw), BF16)), grid=(B, d, nb),
        in_specs=[k_spec, v_spec, q0, q1, w0(qw), w1(qw), w0(LANES), w1(LANES), w0(LANES), w1(LANES), w0(LANES), w0(LANES)],
        out_specs=(w0(kvw), w0(kvw)), name=name, compiler_params=_params(("parallel", "parallel", "parallel")),
    )(qkv3, qkv3, qkv3, qkv3, do3, do3, lse3, lse3, dl3, dl3, v3(cos, LANES), v3(sin, LANES))
    return dk3.reshape(B * S, kvw), dv3.reshape(B * S, kvw)


def _head_expand():
    r = lax.broadcasted_iota(jnp.int32, (LANES, C_HEADS * HEAD_DIM), 0)
    c = lax.broadcasted_iota(jnp.int32, (LANES, C_HEADS * HEAD_DIM), 1)
    return jnp.where(c // HEAD_DIM == r, 1.0, 0.0).astype(F32)


def _delta(do, o, lse=None, sinks_row=None, name="delta"):
    T, W = do.shape
    tm = _pick(T, 512, 8)
    with_sink = sinks_row is not None

    def body(*refs):
        if with_sink:
            do_ref, o_ref, lse_ref, sk_ref, dl_ref, dob_ref, ds_ref = refs
        else:
            do_ref, o_ref, dl_ref, dob_ref = refs
        do_ = do_ref[...]
        dl = lax.dot_general(do_ * o_ref[...], _head_expand(), NT, preferred_element_type=F32, precision=HI)
        dl_ref[...] = dl
        dob_ref[...] = do_.astype(BF16)
        if with_sink:
            lane = lax.broadcasted_iota(jnp.int32, dl.shape, 1)
            contrib = jnp.where(lane < A_N_HEADS, -jnp.exp(sk_ref[...] - lse_ref[...]) * dl, 0.0)
            part = jnp.sum(contrib, axis=0, keepdims=True)

            @pl.when(pl.program_id(0) == 0)
            def _():
                ds_ref[...] = part

            @pl.when(pl.program_id(0) > 0)
            def _():
                ds_ref[...] += part

    row_w = pl.BlockSpec((tm, W), lambda i: (i, 0))
    row_l = pl.BlockSpec((tm, LANES), lambda i: (i, 0))
    vec_l = pl.BlockSpec((1, LANES), lambda i: (0, 0))
    if with_sink:
        return pl.pallas_call(
            body, out_shape=(SDS((T, LANES), F32), SDS((T, W), BF16), SDS((1, LANES), F32)), grid=(T // tm,),
            in_specs=[row_w, row_w, row_l, vec_l], out_specs=(row_l, row_w, vec_l), name=name,
            compiler_params=_params(("arbitrary",)),
        )(do, o, lse, sinks_row)
    return pl.pallas_call(
        body, out_shape=(SDS((T, LANES), F32), SDS((T, W), BF16)), grid=(T // tm,),
        in_specs=[row_w, row_w], out_specs=(row_l, row_w), name=name, compiler_params=_params(("parallel",)),
    )(do, o)


def _merge(os_, lses):
    T, W = os_[0].shape
    tm = _pick(T, 512, 8)

    def body(o0, o1, o2, l0, l1, l2, o_ref, lse_ref):
        ls = [l0[...], l1[...], l2[...]]
        m = jnp.maximum(jnp.maximum(ls[0], ls[1]), ls[2])
        ws = [jnp.exp(l - m) for l in ls]
        tot = ws[0] + ws[1] + ws[2]
        lse_ref[...] = m + jnp.log(tot)
        e = _head_expand()
        acc = jnp.zeros((tm, W), F32)
        for w, o in zip(ws, (o0, o1, o2)):
            acc = acc + jnp.dot(w / tot, e, preferred_element_type=F32, precision=HI) * o[...]
        o_ref[...] = acc

    row_w = pl.BlockSpec((tm, W), lambda i: (i, 0))
    row_l = pl.BlockSpec((tm, LANES), lambda i: (i, 0))
    return pl.pallas_call(
        body, out_shape=(SDS((T, W), F32), SDS((T, LANES), F32)), grid=(T // tm,),
        in_specs=[row_w] * 3 + [row_l] * 3, out_specs=(row_w, row_l), name="c_merge", compiler_params=_params(("parallel",)),
    )(*os_, *lses)


CONV_TC = 256


def _conv_pre(x, w, bias):
    row = lax.broadcasted_iota(jnp.int32, x.shape, 0)
    acc = x * w[SSM_CONV - 1:SSM_CONV, :] + bias
    for k in range(1, SSM_CONV):
        acc = acc + jnp.where(row >= k, pltpu.roll(x, k, 0), 0.0) * w[SSM_CONV - 1 - k:SSM_CONV - k, :]
    return acc


def _conv_fwd(zx3, w, bias):
    B, S, _ = zx3.shape
    off = SSM_D_INNER // CONV_TC

    def body(x_ref, w_ref, b_ref, o_ref):
        v = _conv_pre(x_ref[0], w_ref[...], b_ref[...])
        o_ref[0] = v * jax.nn.sigmoid(v)

    return pl.pallas_call(
        body, out_shape=SDS((B, S, SSM_CONV_DIM), F32), grid=(B, SSM_CONV_DIM // CONV_TC),
        in_specs=[pl.BlockSpec((1, S, CONV_TC), lambda b, j: (b, 0, j + off)),
                  pl.BlockSpec((SSM_CONV, CONV_TC), lambda b, j: (0, j)), pl.BlockSpec((1, CONV_TC), lambda b, j: (0, j))],
        out_specs=pl.BlockSpec((1, S, CONV_TC), lambda b, j: (b, 0, j)), name="b_conv_fwd",
        compiler_params=_params(("parallel", "parallel")),
    )(zx3, w, bias)


def _conv_bwd(zx3, dxc, w, bias, col0, name):
    B, S, n = dxc.shape
    tc = _pick(n, CONV_TC)
    off_x = (SSM_D_INNER + col0) // tc
    off_w = col0 // tc

    def body(x_ref, d_ref, w_ref, b_ref, dx_ref, dw_ref, db_ref):
        x = x_ref[0]
        wv = w_ref[...]
        v = _conv_pre(x, wv, b_ref[...])
        sg = jax.nn.sigmoid(v)
        dc = d_ref[0] * (sg * (1.0 + v * (1.0 - sg)))
        row = lax.broadcasted_iota(jnp.int32, x.shape, 0)
        dx = dc * wv[SSM_CONV - 1:SSM_CONV, :]
        dws = [jnp.sum(dc * x, axis=0, keepdims=True)]
        for k in range(1, SSM_CONV):
            dx = dx + jnp.where(row < S - k, pltpu.roll(dc, S - k, 0), 0.0) * wv[SSM_CONV - 1 - k:SSM_CONV - k, :]
            dws.append(jnp.sum(dc * jnp.where(row >= k, pltpu.roll(x, k, 0), 0.0), axis=0, keepdims=True))
        dx_ref[0] = dx.astype(BF16)
        ridx = lax.broadcasted_iota(jnp.int32, (SSM_CONV, tc), 0)
        dw = jnp.zeros((SSM_CONV, tc), F32)
        for k in range(SSM_CONV):
            dw = jnp.where(ridx == SSM_CONV - 1 - k, dws[k], dw)
        db = jnp.sum(dc, axis=0, keepdims=True)

        @pl.when(pl.program_id(1) == 0)
        def _():
            dw_ref[...] = dw
            db_ref[...] = db

        @pl.when(pl.program_id(1) > 0)
        def _():
            dw_ref[...] += dw
            db_ref[...] += db

    return pl.pallas_call(
        body, out_shape=(SDS((B, S, n), BF16), SDS((SSM_CONV, n), F32), SDS((1, n), F32)), grid=(n // tc, B),
        in_specs=[pl.BlockSpec((1, S, tc), lambda j, b: (b, 0, j + off_x)), pl.BlockSpec((1, S, tc), lambda j, b: (b, 0, j)),
                  pl.BlockSpec((SSM_CONV, tc), lambda j, b: (0, j + off_w)), pl.BlockSpec((1, tc), lambda j, b: (0, j + off_w))],
        out_specs=(pl.BlockSpec((1, S, tc), lambda j, b: (b, 0, j)), pl.BlockSpec((SSM_CONV, tc), lambda j, b: (0, j)),
                   pl.BlockSpec((1, tc), lambda j, b: (0, j))),
        name=name, compiler_params=_params(("parallel", "arbitrary")),
    )(zx3, dxc, w, bias)


def _ssd_common(x, Bm, Cm, dtc_raw, dtr_raw, pr, pc):
    Q = SSM_CHUNK
    zc = dtc_raw + pr[0:1, :]
    dt_c = jax.nn.softplus(zc)
    dt_r = jax.nn.softplus(dtr_raw + pc[:, 0:1])
    A_r = -jnp.exp(pr[1:2, :])
    A_c = -jnp.exp(pc[:, 1:2])
    row = lax.broadcasted_iota(jnp.int32, (Q, Q), 0)
    col = lax.broadcasted_iota(jnp.int32, (Q, Q), 1)
    tril = jnp.where(row >= col, 1.0, 0.0).astype(F32)
    cs_c = jnp.dot(tril, dt_c * A_r, preferred_element_type=F32, precision=HI)
    cs_r = lax.dot_general(dt_r * A_c, tril, NT, preferred_element_type=F32, precision=HI)
    return zc, dt_c, A_r, cs_c, cs_r, row, col, tril


def _ssd_fwd(xc3, dtc, dtr, prow, pcol):
    B, S, _ = xc3.shape
    Q, G, HG, P, N = SSM_CHUNK, SSM_N_GROUPS, SSM_HG, HEAD_DIM, SSM_D_STATE
    nc = S // Q
    xw = HG * P

    def body(x_ref, b_ref, c_ref, dtc_ref, dtr_ref, pr_ref, pc_ref, y_ref, st_ref, state):
        c = pl.program_id(2)

        @pl.when(c == 0)
        def _():
            state[...] = jnp.zeros_like(state)

        x, Bm, Cm = x_ref[0], b_ref[0], c_ref[0]
        pr = pr_ref[0]
        _, dt_c, _, cs_c, cs_r, row, col, _ = _ssd_common(x, Bm, Cm, dtc_ref[0, 0], dtr_ref[0, 0], pr, pc_ref[0])
        Bb, Cb = Bm.astype(BF16), Cm.astype(BF16)
        CB = lax.dot_general(Cb, Bb, NT, preferred_element_type=F32)
        ys = []
        for hg in range(HG):
            xh = x[:, P * hg:P * (hg + 1)]
            xt = xh * dt_c[:, hg:hg + 1]
            csc, csr = cs_c[:, hg:hg + 1], cs_r[hg:hg + 1, :]
            L = jnp.where(row >= col, jnp.exp(jnp.minimum(csc - csr, 0.0)), 0.0)
            ydiag = jnp.dot((CB * L).astype(BF16), xt.astype(BF16), preferred_element_type=F32)
            Sh = state[hg]
            yoff = lax.dot_general(Cb, Sh.astype(BF16), NT, preferred_element_type=F32) * jnp.exp(csc)
            ys.append(ydiag + yoff + pr[2:3, hg:hg + 1] * xh)
            st_ref[0, 0, 0, P * hg:P * (hg + 1), :] = Sh
            csq = csc[Q - 1:Q, :]
            upd = lax.dot_general((xt * jnp.exp(csq - csc)).astype(BF16), Bb, TN, preferred_element_type=F32)
            state[hg] = Sh * jnp.exp(csq) + upd
        y_ref[0] = jnp.concatenate([jnp.concatenate(ys[0:2], axis=-1), jnp.concatenate(ys[2:4], axis=-1)], axis=-1)

    xo, bo, co = 0, SSM_D_INNER // N, (SSM_D_INNER + SSM_BC_DIM) // N
    return pl.pallas_call(
        body, out_shape=(SDS((B, S, SSM_D_INNER), F32), SDS((B, G, nc, xw, N), F32)), grid=(G, B, nc),
        in_specs=[pl.BlockSpec((1, Q, xw), lambda g, b, c: (b, c, g)), pl.BlockSpec((1, Q, N), lambda g, b, c: (b, c, bo + g)),
                  pl.BlockSpec((1, Q, N), lambda g, b, c: (b, c, co + g)), pl.BlockSpec((1, 1, Q, HG), lambda g, b, c: (b, g, c, 0)),
                  pl.BlockSpec((1, 1, HG, Q), lambda g, b, c: (b, g, 0, c)), pl.BlockSpec((1, 3, HG), lambda g, b, c: (g, 0, 0)),
                  pl.BlockSpec((1, HG, 3), lambda g, b, c: (g, 0, 0))],
        out_specs=(pl.BlockSpec((1, Q, xw), lambda g, b, c: (b, c, g)), pl.BlockSpec((1, 1, 1, xw, N), lambda g, b, c: (b, g, c, 0, 0))),
        scratch_shapes=[pltpu.VMEM((HG, P, N), F32)], name="b_ssd_fwd",
        compiler_params=_params(("parallel", "arbitrary", "arbitrary")),
    )(xc3, xc3, xc3, dtc, dtr, prow, pcol)


def _ssd_bwd(xc3, dtc, dtr, prow, pcol, states, dy3):
    B, S, _ = xc3.shape
    Q, G, HG, P, N = SSM_CHUNK, SSM_N_GROUPS, SSM_HG, HEAD_DIM, SSM_D_STATE
    nc = S // Q
    xw = HG * P

    def body(x_ref, b_ref, c_ref, dtc_ref, dtr_ref, pr_ref, pc_ref, st_ref, dy_ref,
             dx_ref, db_ref, dc_ref, ddt_ref, dpar_ref, dstate):
        bi, ci = pl.program_id(1), pl.program_id(2)

        @pl.when(ci == 0)
        def _():
            dstate[...] = jnp.zeros_like(dstate)

        x, Bm, Cm, dy = x_ref[0], b_ref[0], c_ref[0], dy_ref[0]
        pr = pr_ref[0]
        zc, dt_c, A_r, cs_c, cs_r, row, col, tril = _ssd_common(x, Bm, Cm, dtc_ref[0, 0], dtr_ref[0, 0], pr, pc_ref[0])
        Bb, Cb = Bm.astype(BF16), Cm.astype(BF16)
        CB = lax.dot_general(Cb, Bb, NT, preferred_element_type=F32)
        CBt = lax.dot_general(Bb, Cb, NT, preferred_element_type=F32)
        lane4 = lax.broadcasted_iota(jnp.int32, (Q, HG), 1)
        lane4r = lax.broadcasted_iota(jnp.int32, (1, HG), 1)
        rowq = lax.broadcasted_iota(jnp.int32, (Q, 1), 0)
        dB = jnp.zeros((Q, N), F32)
        dC = jnp.zeros((Q, N), F32)
        dcs4 = jnp.zeros((Q, HG), F32)
        dtx4 = jnp.zeros((Q, HG), F32)
        dD4 = jnp.zeros((1, HG), F32)
        dxts, xhs, dyhs = [], [], []
        for hg in range(HG):
            xh = x[:, P * hg:P * (hg + 1)]
            dyh = dy[:, P * hg:P * (hg + 1)]
            xt = xh * dt_c[:, hg:hg + 1]
            xtb, dyb = xt.astype(BF16), dyh.astype(BF16)
            csc, csr = cs_c[:, hg:hg + 1], cs_r[hg:hg + 1, :]
            L = jnp.where(row >= col, jnp.exp(jnp.minimum(csc - csr, 0.0)), 0.0)
            Lt = jnp.where(col >= row, jnp.exp(jnp.minimum(csr - csc, 0.0)), 0.0)
            M, Mt = CB * L, CBt * Lt
            Sh = st_ref[0, 0, 0, P * hg:P * (hg + 1), :]
            dSh = dstate[hg]
            Shb, dShb = Sh.astype(BF16), dSh.astype(BF16)
            ecs = jnp.exp(csc)
            csq = csc[Q - 1:Q, :]
            dec = jnp.exp(csq - csc)
            dxt = jnp.dot(Mt.astype(BF16), dyb, preferred_element_type=F32)
            dxt = dxt + lax.dot_general(Bb, dShb, NT, preferred_element_type=F32) * dec
            Gm = lax.dot_general(dyb, xtb, NT, preferred_element_type=F32)
            Gt = lax.dot_general(xtb, dyb, NT, preferred_element_type=F32)
            dC = dC + jnp.dot((Gm * L).astype(BF16), Bb, preferred_element_type=F32)
            dB = dB + jnp.dot((Gt * Lt).astype(BF16), Cb, preferred_element_type=F32)
            dC = dC + jnp.dot(dyb, Shb, preferred_element_type=F32) * ecs
            dBst = jnp.dot(xtb, dShb, preferred_element_type=F32) * dec
            dB = dB + dBst
            dcs = jnp.sum(Gm * M, axis=1, keepdims=True) - jnp.sum(Gt * Mt, axis=1, keepdims=True)
            yoff = lax.dot_general(Cb, Shb, NT, preferred_element_type=F32) * ecs
            dcs = dcs + jnp.sum(yoff * dyh, axis=1, keepdims=True)
            r = jnp.sum(dBst * Bm, axis=1, keepdims=True)
            dcs = dcs - r
            extra = jnp.sum(r, axis=0, keepdims=True) + jnp.exp(csq) * jnp.sum(
                jnp.sum(dSh * Sh, axis=1, keepdims=True), axis=0, keepdims=True)
            dcs = dcs + jnp.where(rowq == Q - 1, extra, 0.0)
            dcs4 = jnp.where(lane4 == hg, dcs, dcs4)
            dtx4 = jnp.where(lane4 == hg, jnp.sum(dxt * xh, axis=1, keepdims=True), dtx4)
            dD4 = jnp.where(lane4r == hg, jnp.sum(jnp.sum(dyh * xh, axis=1, keepdims=True), axis=0, keepdims=True), dD4)
            dstate[hg] = dSh * jnp.exp(csq) + lax.dot_general((dyh * ecs).astype(BF16), Cb, TN, preferred_element_type=F32)
            dxts.append(dxt)
            xhs.append(xh)
            dyhs.append(dyh)
        da4 = lax.dot_general(tril, dcs4, TN, preferred_element_type=F32, precision=HI)
        ddt4 = da4 * A_r + dtx4
        ddtraw = ddt4 * jax.nn.sigmoid(zc)
        ddt_ref[0, 0] = ddtraw
        dxs = [dxts[hg] * dt_c[:, hg:hg + 1] + pr[2:3, hg:hg + 1] * dyhs[hg] for hg in range(HG)]
        dx_ref[0] = jnp.concatenate([jnp.concatenate(dxs[0:2], axis=-1), jnp.concatenate(dxs[2:4], axis=-1)], axis=-1)
        db_ref[0] = dB
        dc_ref[0] = dC
        d_bias = jnp.sum(ddtraw, axis=0, keepdims=True)
        d_alog = jnp.sum(da4 * dt_c, axis=0, keepdims=True) * A_r
        r3 = lax.broadcasted_iota(jnp.int32, (3, HG), 0)
        dpar = jnp.where(r3 == 0, d_bias, jnp.where(r3 == 1, d_alog, dD4))
        first = (bi == 0) & (ci == 0)

        @pl.when(first)
        def _():
            dpar_ref[0] = dpar

        @pl.when(jnp.logical_not(first))
        def _():
            dpar_ref[0] += dpar

    rc = lambda c: nc - 1 - c
    bo, co = SSM_D_INNER // N, (SSM_D_INNER + SSM_BC_DIM) // N
    return pl.pallas_call(
        body,
        out_shape=(SDS((B, S, SSM_D_INNER), F32), SDS((B, S, SSM_BC_DIM), F32), SDS((B, S, SSM_BC_DIM), F32),
                   SDS((B, G, S, HG), F32), SDS((G, 3, HG), F32)),
        grid=(G, B, nc),
        in_specs=[pl.BlockSpec((1, Q, xw), lambda g, b, c: (b, rc(c), g)), pl.BlockSpec((1, Q, N), lambda g, b, c: (b, rc(c), bo + g)),
                  pl.BlockSpec((1, Q, N), lambda g, b, c: (b, rc(c), co + g)), pl.BlockSpec((1, 1, Q, HG), lambda g, b, c: (b, g, rc(c), 0)),
                  pl.BlockSpec((1, 1, HG, Q), lambda g, b, c: (b, g, 0, rc(c))), pl.BlockSpec((1, 3, HG), lambda g, b, c: (g, 0, 0)),
                  pl.BlockSpec((1, HG, 3), lambda g, b, c: (g, 0, 0)),
                  pl.BlockSpec((1, 1, 1, xw, N), lambda g, b, c: (b, g, rc(c), 0, 0)), pl.BlockSpec((1, Q, xw), lambda g, b, c: (b, rc(c), g))],
        out_specs=(pl.BlockSpec((1, Q, xw), lambda g, b, c: (b, rc(c), g)), pl.BlockSpec((1, Q, N), lambda g, b, c: (b, rc(c), g)),
                   pl.BlockSpec((1, Q, N), lambda g, b, c: (b, rc(c), g)), pl.BlockSpec((1, 1, Q, HG), lambda g, b, c: (b, g, rc(c), 0)),
                   pl.BlockSpec((1, 3, HG), lambda g, b, c: (g, 0, 0))),
        scratch_shapes=[pltpu.VMEM((HG, P, N), F32)], name="b_ssd_bwd",
        compiler_params=_params(("parallel", "arbitrary", "arbitrary")),
    )(xc3, xc3, xc3, dtc, dtr, prow, pcol, states, dy3)


GN_W = SSM_D_INNER // SSM_N_GROUPS


def _gate_fwd(y, zx, nw):
    T = y.shape[0]
    tm = _pick(T, 256, 8)

    def body(y_ref, z_ref, w_ref, o_ref):
        z = z_ref[...]
        gt = y_ref[...] * (z * jax.nn.sigmoid(z))
        outs = []
        for k in range(SSM_N_GROUPS):
            gk = gt[:, GN_W * k:GN_W * (k + 1)]
            outs.append(gk * lax.rsqrt(jnp.mean(gk * gk, axis=-1, keepdims=True) + NORM_EPS))
        o_ref[...] = (jnp.concatenate(outs, axis=-1) * w_ref[...]).astype(BF16)

    row = pl.BlockSpec((tm, SSM_D_INNER), lambda i: (i, 0))
    return pl.pallas_call(
        body, out_shape=SDS((T, SSM_D_INNER), BF16), grid=(T // tm,),
        in_specs=[row, row, pl.BlockSpec((1, SSM_D_INNER), lambda i: (0, 0))], out_specs=row, name="b_gate_fwd",
        compiler_params=_params(("parallel",)),
    )(y, zx, nw)


def _gate_bwd(dgn, y, zx, nw):
    T = y.shape[0]
    tm = _pick(T, 256, 8)

    def body(d_ref, y_ref, z_ref, w_ref, dy_ref, dz_ref, dw_ref):
        z, yv, w = z_ref[...], y_ref[...], w_ref[...]
        sg = jax.nn.sigmoid(z)
        sz = z * sg
        gt = yv * sz
        gw = d_ref[...] * w
        dgts, dws = [], []
        for k in range(SSM_N_GROUPS):
            sl = slice(GN_W * k, GN_W * (k + 1))
            gk, gwk = gt[:, sl], gw[:, sl]
            rstd = lax.rsqrt(jnp.mean(gk * gk, axis=-1, keepdims=True) + NORM_EPS)
            dgts.append(rstd * gwk - gk * (rstd * rstd * rstd) * jnp.mean(gwk * gk, axis=-1, keepdims=True))
            dws.append(jnp.sum(d_ref[:, sl] * gk * rstd, axis=0, keepdims=True))
        dgt = jnp.concatenate(dgts, axis=-1)
        dy_ref[...] = dgt * sz
        dz_ref[...] = (dgt * yv * (sg * (1.0 + z * (1.0 - sg)))).astype(BF16)
        dw = jnp.concatenate(dws, axis=-1)

        @pl.when(pl.program_id(0) == 0)
        def _():
            dw_ref[...] = dw

        @pl.when(pl.program_id(0) > 0)
        def _():
            dw_ref[...] += dw

    row = pl.BlockSpec((tm, SSM_D_INNER), lambda i: (i, 0))
    vec = pl.BlockSpec((1, SSM_D_INNER), lambda i: (0, 0))
    return pl.pallas_call(
        body, out_shape=(SDS((T, SSM_D_INNER), F32), SDS((T, SSM_D_INNER), BF16), SDS((1, SSM_D_INNER), F32)), grid=(T // tm,),
        in_specs=[row, row, row, vec], out_specs=(row, row, vec), name="b_gate_bwd", compiler_params=_params(("arbitrary",)),
    )(dgn, y, zx, nw)


MESH = pl.DeviceIdType.MESH
ANY = pl.BlockSpec(memory_space=pl.ANY)


N_CHIPS = 4


def _dev_block(ref, kind, j, size):
    if kind == "slot":
        return ref.at[j]
    start = pl.multiple_of(j * size, size)
    nd = len(ref.shape)
    if kind == "col":
        return ref.at[(slice(None),) * (nd - 1) + (pl.ds(start, size),)]
    return ref.at[(slice(None),) * (nd - 2) + (pl.ds(start, size), slice(None))]


def _dma_sems(n, k):
    return [pltpu.SemaphoreType.DMA((n, k)), pltpu.SemaphoreType.DMA((n, k)), pltpu.SemaphoreType.DMA((n, k))]


def _gather(items, name):
    n = len(items)

    def body(*refs):
        srcs, dsts = refs[:n], refs[n:2 * n]
        send_sems, recv_sems, local_sems = refs[2 * n:]
        px, py, pc = lax.axis_index("x"), lax.axis_index("y"), lax.axis_index("c")
        me, sibling = (px, py, pc), (px, py, 1 - pc)
        chips = [(1 - px, py), (px, 1 - py), (1 - px, 1 - py)]

        def src_of(a):
            return srcs[a] if items[a][1] is None else srcs[a].at[items[a][1]]

        def blk(a, dev):
            return _dev_block(dsts[a], items[a][2], 4 * dev[0] + 2 * dev[1] + dev[2], items[a][3])

        def copy(a, k, block, to, src=None):
            return pltpu.make_async_remote_copy(
                src_ref=blk(a, block) if src is None else src, dst_ref=blk(a, block),
                send_sem=send_sems.at[a, k], recv_sem=recv_sems.at[a, k], device_id=to, device_id_type=MESH)

        mine = [pltpu.make_async_copy(src_of(a), blk(a, me), local_sems.at[a, 0]) for a in range(n)]
        for cp in mine:
            cp.start()
        first = []
        for a in range(n):
            first.append(copy(a, 0, me, sibling, src=src_of(a)))
            first += [copy(a, 1 + j, me, (*chip, pc), src=src_of(a)) for j, chip in enumerate(chips)]
        for cp in first:
            cp.start()
        passed = []
        for j, chip in enumerate(chips):
            for a in range(n):
                copy(a, 1 + j, (*chip, pc), me).wait_recv()
                fwd = copy(a, 4 + j, (*chip, pc), sibling)
                fwd.start()
                passed.append(fwd)
        for a in range(n):
            copy(a, 0, sibling, me).wait_recv()
            for j, chip in enumerate(chips):
                copy(a, 4 + j, (*chip, 1 - pc), me).wait_recv()
        for cp in first + passed:
            cp.wait_send()
        for cp in mine:
            cp.wait()

    return pl.pallas_call(
        body, out_shape=[SDS(it[4], it[0].dtype) for it in items], in_specs=[ANY] * n, out_specs=[ANY] * n,
        scratch_shapes=_dma_sems(n, 7), name=name,
    )(*[it[0] for it in items])


def _reduce_d2d(items, name):
    n = len(items)

    def body(*refs):
        gs, owns, gots = refs[:n], refs[n:2 * n], refs[2 * n:3 * n]
        send_sems, recv_sems, local_sems = refs[3 * n:]
        px, py, pc = lax.axis_index("x"), lax.axis_index("y"), lax.axis_index("c")
        copies = []
        for a in range(n):
            _, kind, size, _ = items[a]
            for q in range(N_CHIPS):
                base = 4 * (q >> 1) + 2 * (q & 1)
                copies.append(pltpu.make_async_copy(_dev_block(gs[a], kind, base + pc, size), owns[a].at[q], local_sems.at[a, q]))
                copies.append(pltpu.make_async_remote_copy(
                    src_ref=_dev_block(gs[a], kind, base + 1 - pc, size), dst_ref=gots[a].at[q], send_sem=send_sems.at[a, q],
                    recv_sem=recv_sems.at[a, q], device_id=(px, py, 1 - pc), device_id_type=MESH))
        for cp in copies:
            cp.start()
        for cp in copies:
            cp.wait()

    shapes = [SDS((N_CHIPS,) + tuple(it[3]), F32) for it in items]
    outs = pl.pallas_call(
        body, out_shape=shapes + shapes, in_specs=[ANY] * n, out_specs=[ANY] * (2 * n),
        scratch_shapes=_dma_sems(n, N_CHIPS), name=name,
    )(*[it[0] for it in items])
    return outs[:n], outs[n:]


def _pair_sum(own, got, name):
    shp = own.shape
    a2, b2 = own.reshape(-1, shp[-1]), got.reshape(-1, shp[-1])
    R, C = a2.shape
    tr = _pick(R, max(16, STREAM_VMEM // (2 * C * 10)), 16)

    def body(a_ref, b_ref, o_ref):
        o_ref[...] = (a_ref[...] + b_ref[...]).astype(BF16)

    row = pl.BlockSpec((tr, C), lambda i: (i, 0))
    return pl.pallas_call(
        body, out_shape=SDS((R, C), BF16), grid=(R // tr,), in_specs=[row, row], out_specs=row, name=name,
        compiler_params=_params(("parallel",)),
    )(a2, b2).reshape(shp)


def _reduce_ici(parts, name):
    n = len(parts)

    def body(*refs):
        ps, rs = refs[:n], refs[n:2 * n]
        send_sems, recv_sems, local_sems = refs[2 * n:]
        px, py, pc = lax.axis_index("x"), lax.axis_index("y"), lax.axis_index("c")
        my_chip = 2 * px + py
        mine = [pltpu.make_async_copy(ps[a].at[my_chip], rs[a].at[my_chip], local_sems.at[a, 0]) for a in range(n)]
        for cp in mine:
            cp.start()
        sends, recvs = [], []
        for a in range(n):
            for k in range(1, N_CHIPS):
                qx, qy = px ^ (k >> 1), py ^ (k & 1)
                q = 2 * qx + qy
                kw = dict(send_sem=send_sems.at[a, k - 1], recv_sem=recv_sems.at[a, k - 1], device_id=(qx, qy, pc),
                          device_id_type=MESH)
                sends.append(pltpu.make_async_remote_copy(src_ref=ps[a].at[q], dst_ref=rs[a].at[my_chip], **kw))
                recvs.append(pltpu.make_async_remote_copy(src_ref=ps[a].at[q], dst_ref=rs[a].at[q], **kw))
        for cp in sends:
            cp.start()
        for cp in recvs:
            cp.wait_recv()
        for cp in sends:
            cp.wait_send()
        for cp in mine:
            cp.wait()

    return pl.pallas_call(
        body, out_shape=[SDS(p.shape, p.dtype) for p in parts], in_specs=[ANY] * n, out_specs=[ANY] * n,
        scratch_shapes=_dma_sems(n, N_CHIPS - 1), name=name,
    )(*parts)


def _adam_update(g, w, m, v):
    c1 = 1.0 - ADAM_B1 ** ADAM_STEP
    c2 = 1.0 - ADAM_B2 ** ADAM_STEP
    nm = ADAM_B1 * m + (1.0 - ADAM_B1) * g
    nv = ADAM_B2 * v + (1.0 - ADAM_B2) * (g * g)
    delta = -ADAM_LR * ((nm / c1) / (jnp.sqrt(nv / c2) + ADAM_EPS) + ADAM_WD * w)
    return delta, nm, nv


def _adamw(recv, w, m, v, name):
    shp = w.shape
    C = shp[-1]
    r2 = recv.reshape(N_CHIPS, -1, C)
    w2, m2, v2 = (t.reshape(-1, C) for t in (w, m, v))
    R = w2.shape[0]
    row_bytes = 2 * C * (N_CHIPS * 2 + 7 * 4)
    tr = _pick(R, max(16, STREAM_VMEM // row_bytes), 16)

    def body(r_ref, w_ref, m_ref, v_ref, g_ref, d_ref, nm_ref, nv_ref):
        g = r_ref[0].astype(F32)
        for q in range(1, N_CHIPS):
            g = g + r_ref[q].astype(F32)
        g_ref[...] = g
        d_ref[...], nm_ref[...], nv_ref[...] = _adam_update(g, w_ref[...], m_ref[...], v_ref[...])

    row = pl.BlockSpec((tr, C), lambda i: (i, 0))
    out = SDS((R, C), F32)
    outs = pl.pallas_call(
        body, out_shape=(out, out, out, out), grid=(R // tr,),
        in_specs=[pl.BlockSpec((N_CHIPS, tr, C), lambda i: (0, i, 0)), row, row, row], out_specs=(row, row, row, row),
        name=name, compiler_params=_params(("parallel",)),
    )(r2, w2, m2, v2)
    return [o.reshape(shp) for o in outs]


def _small_adamw(gathered, ws, ms, vs):
    n = len(ws)

    def body(*refs):
        g_in, w_in, m_in, v_in = refs[:n], refs[n:2 * n], refs[2 * n:3 * n], refs[3 * n:4 * n]
        outs = refs[4 * n:]
        for i in range(n):
            g = g_in[i][0]
            for dev in range(1, N_DEV):
                g = g + g_in[i][dev]
            d, nm, nv = _adam_update(g, w_in[i][...], m_in[i][...], v_in[i][...])
            outs[i][...] = g
            outs[n + i][...] = d
            outs[2 * n + i][...] = nm
            outs[3 * n + i][...] = nv

    shapes = [SDS(w.shape, F32) for w in ws]
    outs = pl.pallas_call(body, out_shape=shapes * 4, name="small_adamw")(*gathered, *ws, *ms, *vs)
    return outs[:n], outs[n:2 * n], outs[2 * n:3 * n], outs[3 * n:]


W_NAMES = ("norm_mix_w", "norm_mlp_w", "a_w_qkv", "a_b_qkv", "a_sinks", "a_w_o", "a_b_o", "b_in_w", "b_conv_w", "b_conv_b",
           "b_dt_bias", "b_a_log", "b_d", "b_norm_w", "b_out_w", "c_w_qkv", "c_w_o", "mlp_w_up", "mlp_w_down", "final_norm_w")
BIG_KIND = {"a_w_qkv": "slot", "a_w_o": "row", "b_in_w": "slot", "b_out_w": "row", "c_w_qkv": "col", "c_w_o": "row",
            "mlp_w_up": "col", "mlp_w_down": "row"}
SMALL_SHARDED = {"a_b_qkv": 1, "a_b_o": 1, "b_conv_w": 2}
SMALL_REPLICATED = ("norm_mix_w", "norm_mlp_w", "a_sinks", "b_conv_b", "b_dt_bias", "b_a_log", "b_d", "b_norm_w", "final_norm_w")


def _layer_big(i):
    kind, j = i % 3, i // 3
    mix = {0: [("a_w_qkv", j), ("a_w_o", j)], 1: [("b_in_w", 0), ("b_out_w", 0)], 2: [("c_w_qkv", 0), ("c_w_o", 0)]}[kind]
    return mix + [("mlp_w_up", i), ("mlp_w_down", i)]


def _block_size(kind, shard2d):
    return {"slot": None, "row": shard2d[0], "col": shard2d[1]}[kind]


def _full2d(kind, shard2d):
    k, n = shard2d
    return {"slot": (N_DEV, k, n), "row": (N_DEV * k, n), "col": (k, N_DEV * n)}[kind]


def _from_slots(t, ax):
    s = t.shape[1:]
    return jnp.moveaxis(t, 0, ax).reshape(s[:ax] + (N_DEV * s[ax],) + s[ax + 1:])


def _to_slots(g, ax):
    s = g.shape
    return jnp.moveaxis(g.reshape(s[:ax] + (N_DEV, s[ax] // N_DEV) + s[ax + 1:]), ax, 0)


def _rope_tables(positions):
    half = HEAD_DIM // 2
    inv = ROPE_THETA ** (-jnp.arange(half, dtype=F32) / half)
    ang = positions.astype(F32)[..., None] * inv
    rep = LANES // half
    cos = jnp.tile(jnp.cos(ang), (1, 1, rep))
    sin = jnp.tile(jnp.sin(ang), (1, 1, rep))
    T = positions.shape[0] * positions.shape[1]
    return cos.reshape(T, LANES), sin.reshape(T, LANES)


def _swa_fwd(u, h, p, j, B, S, cos, sin, tag):
    qkv = _matmul(u, p["a_w_qkv"][j], out_dtype=BF16, bias=p["a_b_qkv"][j][None], rope=(cos, sin),
                  rope_cols=A_Q_DIM + A_KV_DIM, tn=640, name=f"{tag}_qkv")
    o, lse = _attn_fwd(qkv, B, S, 1, n_heads=A_N_HEADS, n_kv=A_N_KV, q_col=0, k_col=A_Q_DIM, v_col=A_Q_DIM + A_KV_DIM,
                       max_dist=A_WINDOW - 1, sinks=p["a_sinks"][j], name=f"{tag}_attn")
    h1 = _matmul(o, p["a_w_o"][j], bias=p["a_b_o"][j][None], resid=h, name=f"{tag}_o")
    return h1, (qkv, o, lse)


def _swa_bwd(dh1, u, saved, p, j, B, S, cos, sin, tag):
    qkv, o, lse = saved
    kw = dict(n_heads=A_N_HEADS, n_kv=A_N_KV, q_col=0, k_col=A_Q_DIM, v_col=A_Q_DIM + A_KV_DIM, max_dist=A_WINDOW - 1)
    g = {}
    do = _matmul(dh1, p["a_w_o"][j], tb=True, name=f"{tag}_do")
    g["a_w_o"] = _matmul(o, dh1, ta=True, name=f"{tag}_dwo")
    g["a_b_o"] = _colsum(dh1, f"{tag}_dbo")[0]
    sk = jnp.pad(p["a_sinks"][j], (0, LANES - A_N_HEADS))[None]
    delta, dob, dsink = _delta(do, o, lse, sk, name=f"{tag}_delta")
    g["a_sinks"] = dsink[0, :A_N_HEADS]
    dq = _attn_dq(qkv, dob, lse, delta, cos, sin, B, S, 1, name=f"{tag}_dq", **kw)
    dk, dv = _attn_dkv(qkv, dob, lse, delta, cos, sin, B, S, 1, name=f"{tag}_dkv", **kw)
    dqkv = jnp.concatenate([dq, dk, dv], axis=1)
    g["a_w_qkv"] = _matmul(u, dqkv, ta=True, tn=640, name=f"{tag}_dwqkv")
    g["a_b_qkv"] = _colsum(dqkv, f"{tag}_dbqkv")[0]
    du = _matmul(dqkv, p["a_w_qkv"][j], tb=True, tk=640, name=f"{tag}_du")
    return du, g


def _dil_fwd(u, h, p, B, S, cos, sin):
    W = C_HEADS * HEAD_DIM
    qkv = _matmul(u, p["c_w_qkv"][0], out_dtype=BF16, rope=(cos, sin), rope_cols=6 * W, name="c_qkv")
    os_, lses = [], []
    for gi, (window, dil) in enumerate(C_PATTERNS):
        o, lse = _attn_fwd(qkv, B, S, dil, n_heads=C_HEADS, n_kv=C_HEADS, q_col=gi * W, k_col=(3 + gi) * W,
                           v_col=(6 + gi) * W, max_dist=window // dil, sinks=None, name=f"c_attn{gi}")
        os_.append(o)
        lses.append(lse)
    o, lse = _merge(os_, lses)
    h1 = _matmul(o, p["c_w_o"][0], resid=h, name="c_o")
    return h1, (qkv, o, lse)


def _dil_bwd(dh1, u, saved, p, B, S, cos, sin):
    W = C_HEADS * HEAD_DIM
    qkv, o, lse = saved
    g = {}
    do = _matmul(dh1, p["c_w_o"][0], tb=True, name="c_do")
    g["c_w_o"] = _matmul(o, dh1, ta=True, name="c_dwo")[None]
    delta, dob = _delta(do, o, name="c_delta")
    dqs, dks, dvs = [], [], []
    for gi, (window, dil) in enumerate(C_PATTERNS):
        kw = dict(n_heads=C_HEADS, n_kv=C_HEADS, q_col=gi * W, k_col=(3 + gi) * W, v_col=(6 + gi) * W, max_dist=window // dil)
        dqs.append(_attn_dq(qkv, dob, lse, delta, cos, sin, B, S, dil, name=f"c_dq{gi}", **kw))
        dk, dv = _attn_dkv(qkv, dob, lse, delta, cos, sin, B, S, dil, name=f"c_dkv{gi}", **kw)
        dks.append(dk)
        dvs.append(dv)
    dqkv = jnp.concatenate(dqs + dks + dvs, axis=1)
    g["c_w_qkv"] = _matmul(u, dqkv, ta=True, name="c_dwqkv")[None]
    du = _matmul(dqkv, p["c_w_qkv"][0], tb=True, name="c_du")
    return du, g


def _ssm_params(p):
    par = jnp.stack([p["b_dt_bias"][0], p["b_a_log"][0], p["b_d"][0]], axis=0)
    prow = par.reshape(3, SSM_N_GROUPS, SSM_HG).transpose(1, 0, 2)
    return prow, prow.transpose(0, 2, 1)


def _mamba_fwd(u, h, p, B, S):
    T = B * S
    G, HG = SSM_N_GROUPS, SSM_HG
    w_in = p["b_in_w"][0]
    nzx = SSM_D_INNER + SSM_CONV_DIM
    w_dt = jnp.pad(w_in[:, nzx:], ((0, 0), (0, LANES - SSM_N_HEADS)))
    zx = _matmul(u, w_in[:, :nzx], name="b_zx")
    dtraw = _matmul(u, w_dt, name="b_dt")[:, :SSM_N_HEADS]
    dtc = dtraw.reshape(B, S, G, HG).transpose(0, 2, 1, 3)
    dtr = dtraw.reshape(B, S, G, HG).transpose(0, 2, 3, 1)
    prow, pcol = _ssm_params(p)
    zx3 = zx.reshape(B, S, nzx)
    xc3 = _conv_fwd(zx3, p["b_conv_w"][0], p["b_conv_b"])
    y3, states = _ssd_fwd(xc3, dtc, dtr, prow, pcol)
    y = y3.reshape(T, SSM_D_INNER)
    gn = _gate_fwd(y, zx, p["b_norm_w"])
    h1 = _matmul(gn, p["b_out_w"][0], resid=h, name="b_out")
    return h1, (zx, dtc, dtr, xc3, y, states, gn, w_dt)


def _mamba_bwd(dh1, u, saved, p, B, S):
    T = B * S
    zx, dtc, dtr, xc3, y, states, gn, w_dt = saved
    nzx = SSM_D_INNER + SSM_CONV_DIM
    w_in = p["b_in_w"][0]
    prow, pcol = _ssm_params(p)
    g = {}
    dgn = _matmul(dh1, p["b_out_w"][0], tb=True, name="b_dgn")
    g["b_out_w"] = _matmul(gn, dh1, ta=True, name="b_dwout")[None]
    dy, dz, dnw = _gate_bwd(dgn, y, zx, p["b_norm_w"])
    g["b_norm_w"] = dnw
    dx3, dB3, dC3, ddt, dpar = _ssd_bwd(xc3, dtc, dtr, prow, pcol, states, dy.reshape(B, S, SSM_D_INNER))
    dpar = dpar.transpose(1, 0, 2).reshape(3, SSM_N_HEADS)
    g["b_dt_bias"], g["b_a_log"], g["b_d"] = dpar[0:1], dpar[1:2], dpar[2:3]
    zx3 = zx.reshape(B, S, nzx)
    cw, cb = p["b_conv_w"][0], p["b_conv_b"]
    parts, dws, dbs = [], [], []
    for col0, dpart, nm in ((0, dx3, "b_conv_bwd_x"), (SSM_D_INNER, dB3, "b_conv_bwd_b"),
                            (SSM_D_INNER + SSM_BC_DIM, dC3, "b_conv_bwd_c")):
        dxp, dw, db = _conv_bwd(zx3, dpart, cw, cb, col0, nm)
        parts.append(dxp.reshape(T, -1))
        dws.append(dw)
        dbs.append(db)
    g["b_conv_w"] = jnp.concatenate(dws, axis=1)[None]
    g["b_conv_b"] = jnp.concatenate(dbs, axis=1)
    dzx = jnp.concatenate([dz] + parts, axis=1)
    ddtraw = ddt.transpose(0, 2, 1, 3).reshape(T, SSM_N_HEADS)
    ddtp = jnp.pad(ddtraw, ((0, 0), (0, LANES - SSM_N_HEADS)))
    dw_zx = _matmul(u, dzx, ta=True, name="b_dwzx")
    dw_dt = _matmul(u, ddtp, ta=True, name="b_dwdt")[:, :SSM_N_HEADS]
    g["b_in_w"] = jnp.concatenate([dw_zx, dw_dt], axis=1)[None]
    du = _matmul(dzx, w_in[:, :nzx], tb=True, name="b_du_zx")
    du = _matmul(ddtp, w_dt, tb=True, resid=du, name="b_du_dt")
    return du, g


def _local_step(x, positions, p, target):
    B, S, D = x.shape
    T = B * S
    cos, sin = _rope_tables(positions)
    h = x.reshape(T, D)
    tape = []
    for i in range(DEPTH):
        kind, j = i % 3, i // 3
        u = _rmsnorm_fwd(h, p["norm_mix_w"][i], f"l{i}_norm_mix")
        if kind == 0:
            h1, saved = _swa_fwd(u, h, p, j, B, S, cos, sin, f"a{j}")
        elif kind == 1:
            h1, saved = _mamba_fwd(u, h, p, B, S)
        else:
            h1, saved = _dil_fwd(u, h, p, B, S, cos, sin)
        u2 = _rmsnorm_fwd(h1, p["norm_mlp_w"][i], f"l{i}_norm_mlp")
        r, s = _matmul(u2, p["mlp_w_up"][i], out_dtype=BF16, relu2=True, name=f"l{i}_up")
        h2 = _matmul(s, p["mlp_w_down"][i], resid=h1, name=f"l{i}_down")
        tape.append((h, u, saved, h1, u2, r, s))
        h = h2
    dh, dwf, loss = _final_loss(h, target.reshape(T, D), p["final_norm_w"])
    grads = {"final_norm_w": dwf[0]}
    per_layer = {n: [None] * DEPTH for n in ("norm_mix_w", "norm_mlp_w", "mlp_w_up", "mlp_w_down")}
    a_grads = [None, None]
    for i in reversed(range(DEPTH)):
        kind, j = i % 3, i // 3
        h0, u, saved, h1, u2, r, s = tape[i]
        da = _matmul(dh, p["mlp_w_down"][i], tb=True, out_dtype=BF16, mul=r, mul_scale=2.0, name=f"l{i}_da")
        per_layer["mlp_w_down"][i] = _matmul(s, dh, ta=True, name=f"l{i}_dwdown")
        per_layer["mlp_w_up"][i] = _matmul(u2, da, ta=True, name=f"l{i}_dwup")
        du2 = _matmul(da, p["mlp_w_up"][i], tb=True, name=f"l{i}_du2")
        dh1, dnw = _rmsnorm_bwd(h1, du2, p["norm_mlp_w"][i], dh, f"l{i}_norm_mlp_bwd")
        per_layer["norm_mlp_w"][i] = dnw[0]
        if kind == 0:
            du, g = _swa_bwd(dh1, u, saved, p, j, B, S, cos, sin, f"a{j}")
            a_grads[j] = g
        elif kind == 1:
            du, g = _mamba_bwd(dh1, u, saved, p, B, S)
            grads.update(g)
        else:
            du, g = _dil_bwd(dh1, u, saved, p, B, S, cos, sin)
            grads.update(g)
        dh, dnw = _rmsnorm_bwd(h0, du, p["norm_mix_w"][i], dh1, f"l{i}_norm_mix_bwd")
        per_layer["norm_mix_w"][i] = dnw[0]
    for n in ("norm_mix_w", "norm_mlp_w"):
        grads[n] = jnp.stack(per_layer[n], axis=0)
    for n in ("mlp_w_up", "mlp_w_down"):
        grads[n] = per_layer[n]
    for n in ("a_b_qkv", "a_sinks", "a_b_o"):
        grads[n] = jnp.stack([a_grads[0][n], a_grads[1][n]], axis=0)
    for n in ("a_w_qkv", "a_w_o"):
        grads[n] = [a_grads[0][n], a_grads[1][n]]
    for n in ("b_in_w", "b_out_w", "c_w_qkv", "c_w_o"):
        grads[n] = [grads[n][0]]
    return loss, dh.reshape(B, S, D), grads


def kernel(x, positions, norm_mix_w, norm_mlp_w, a_w_qkv, a_b_qkv, a_sinks, a_w_o, a_b_o, b_in_w, b_conv_w, b_conv_b, b_dt_bias, b_a_log, b_d, b_norm_w, b_out_w, c_w_qkv, c_w_o, mlp_w_up, mlp_w_down, final_norm_w, loss_target, m_norm_mix_w, m_norm_mlp_w, m_a_w_qkv, m_a_b_qkv, m_a_sinks, m_a_w_o, m_a_b_o, m_b_in_w, m_b_conv_w, m_b_conv_b, m_b_dt_bias, m_b_a_log, m_b_d, m_b_norm_w, m_b_out_w, m_c_w_qkv, m_c_w_o, m_mlp_w_up, m_mlp_w_down, m_final_norm_w, v_norm_mix_w, v_norm_mlp_w, v_a_w_qkv, v_a_b_qkv, v_a_sinks, v_a_w_o, v_a_b_o, v_b_in_w, v_b_conv_w, v_b_conv_b, v_b_dt_bias, v_b_a_log, v_b_d, v_b_norm_w, v_b_out_w, v_c_w_qkv, v_c_w_o, v_mlp_w_up, v_mlp_w_down, v_final_norm_w):
    w = dict(zip(W_NAMES, (norm_mix_w, norm_mlp_w, a_w_qkv, a_b_qkv, a_sinks, a_w_o, a_b_o, b_in_w, b_conv_w, b_conv_b,
                           b_dt_bias, b_a_log, b_d, b_norm_w, b_out_w, c_w_qkv, c_w_o, mlp_w_up, mlp_w_down, final_norm_w)))
    m = dict(zip(W_NAMES, (m_norm_mix_w, m_norm_mlp_w, m_a_w_qkv, m_a_b_qkv, m_a_sinks, m_a_w_o, m_a_b_o, m_b_in_w,
                           m_b_conv_w, m_b_conv_b, m_b_dt_bias, m_b_a_log, m_b_d, m_b_norm_w, m_b_out_w, m_c_w_qkv, m_c_w_o,
                           m_mlp_w_up, m_mlp_w_down, m_final_norm_w)))
    v = dict(zip(W_NAMES, (v_norm_mix_w, v_norm_mlp_w, v_a_w_qkv, v_a_b_qkv, v_a_sinks, v_a_w_o, v_a_b_o, v_b_in_w,
                           v_b_conv_w, v_b_conv_b, v_b_dt_bias, v_b_a_log, v_b_d, v_b_norm_w, v_b_out_w, v_c_w_qkv, v_c_w_o,
                           v_mlp_w_up, v_mlp_w_down, v_final_norm_w)))
    me = 4 * lax.axis_index("x") + 2 * lax.axis_index("y") + lax.axis_index("c")

    trio = tuple(SMALL_SHARDED)
    got = _gather([(d[n], None, "slot", None, (N_DEV,) + d[n].shape) for n in trio for d in (w, m, v)], "gather_small")
    slots = {n: got[3 * i:3 * i + 3] for i, n in enumerate(trio)}
    p = {n: w[n] for n in SMALL_REPLICATED}
    for n in trio:
        p[n] = _from_slots(slots[n][0], SMALL_SHARDED[n])
    wb = {n: w[n].astype(BF16) for n in BIG_KIND}
    for n in BIG_KIND:
        p[n] = [None] * w[n].shape[0]
    for i in range(DEPTH):
        names = _layer_big(i)
        items = []
        for n, l in names:
            kind, s2 = BIG_KIND[n], w[n].shape[1:]
            items.append((wb[n], l, kind, _block_size(kind, s2), _full2d(kind, s2)))
        for (n, l), t in zip(names, _gather(items, f"gather_l{i}")):
            p[n][l] = _from_slots(t, 1) if BIG_KIND[n] == "slot" else t

    loss_part, dx, grads = _local_step(x, positions, p, loss_target)
    loss = lax.psum(loss_part[0, 0], AXES)

    res = {n: [[None] * w[n].shape[0] for _ in range(4)] for n in BIG_KIND}
    for i in reversed(range(DEPTH)):
        names = _layer_big(i)
        items = []
        for n, l in names:
            kind, s2 = BIG_KIND[n], w[n].shape[1:]
            g = grads[n][l]
            items.append((_to_slots(g, 1) if kind == "slot" else g, kind, _block_size(kind, s2), s2))
        own, sib = _reduce_d2d(items, f"reduce_d2d_l{i}")
        parts = [_pair_sum(o, s, f"pair_sum_l{i}_{n}") for (n, _), o, s in zip(names, own, sib)]
        recv = _reduce_ici(parts, f"reduce_ici_l{i}")
        for (n, l), r in zip(names, recv):
            for k, o in enumerate(_adamw(r, w[n][l], m[n][l], v[n][l], f"adamw_l{i}_{n}")):
                res[n][k][l] = o
    out = {n: [jnp.stack(res[n][k], axis=0) for k in range(4)] for n in BIG_KIND}

    small = SMALL_REPLICATED + trio
    as2d = lambda t: t.reshape(1, -1) if t.ndim == 1 else t
    g_sm = [as2d(grads[n]) for n in SMALL_REPLICATED] + [_to_slots(grads[n].reshape(p[n].shape), SMALL_SHARDED[n]) for n in trio]
    gathered = _gather([(g, None, "slot", None, (N_DEV,) + g.shape) for g in g_sm], "gather_small_grads")
    ws = [as2d(w[n]) for n in SMALL_REPLICATED] + [slots[n][0] for n in trio]
    ms = [as2d(m[n]) for n in SMALL_REPLICATED] + [slots[n][1] for n in trio]
    vs = [as2d(v[n]) for n in SMALL_REPLICATED] + [slots[n][2] for n in trio]
    sm_out = _small_adamw(gathered, ws, ms, vs)
    for i, n in enumerate(small):
        if n in SMALL_SHARDED:
            out[n] = [lax.dynamic_index_in_dim(sm_out[k][i], me, 0, keepdims=False) for k in range(4)]
        else:
            out[n] = [sm_out[k][i].reshape(w[n].shape) for k in range(4)]
    return (loss, dx, *[out[n][0] for n in W_NAMES], *[out[n][1] for n in W_NAMES], *[out[n][2] for n in W_NAMES],
            *[out[n][3] for n in W_NAMES])
```

```python
import functools
import math

import jax
import jax.numpy as jnp
import numpy as np
from jax import lax
from jax.experimental import pallas as pl
from jax.experimental.pallas import tpu as pltpu

F32 = jnp.float32
BF16 = jnp.bfloat16
SDS = jax.ShapeDtypeStruct

D_MODEL = 1024
DEPTH = 4
BLOCK = 128
ROPE_THETA = 10000.0
NORM_EPS = 1e-5
HEAD_DIM = 64
A_N_HEADS = 16
A_N_KV = 2
A_WINDOW = 128
A_Q_DIM = 1024
A_KV_DIM = 128
SSM_D_INNER = 2048
SSM_N_HEADS = 32
SSM_N_GROUPS = 8
SSM_HG = 4
SSM_D_STATE = 128
SSM_CONV = 4
SSM_CHUNK = 128
SSM_BC_DIM = 1024
SSM_CONV_DIM = 4096
C_PATTERNS = ((128, 1), (512, 4), (2048, 16))
C_HEADS = 16
ADAM_LR, ADAM_B1, ADAM_B2, ADAM_EPS, ADAM_WD, ADAM_STEP = 0.001, 0.9, 0.999, 1e-08, 0.01, 10

N_DEV = 8
AXES = ("x", "y", "c")
LANES = 128
VMEM_LIMIT = 56 * 1024 * 1024
STREAM_VMEM = 16 * 1024 * 1024
NEG = -1e30

NN = (((1,), (0,)), ((), ()))
NT = (((1,), (1,)), ((), ()))
TN = (((0,), (0,)), ((), ()))
HI = lax.Precision.HIGHEST


def _pick(n, cap, mult=LANES):
    best = None
    for t in range(mult, min(n, cap) + 1, mult):
        if n % t == 0:
            best = t
    return best if best is not None else n


def _params(sem):
    return pltpu.CompilerParams(dimension_semantics=sem, vmem_limit_bytes=VMEM_LIMIT)


def _bf(x):
    return x if x.dtype == BF16 else x.astype(BF16)


def _rot_half(y):
    n = y.shape[-1]
    lane = lax.broadcasted_iota(jnp.int32, y.shape, y.ndim - 1)
    return jnp.where((lane % HEAD_DIM) < HEAD_DIM // 2, -pltpu.roll(y, n - 32, y.ndim - 1), pltpu.roll(y, 32, y.ndim - 1))


def _rope(y, cos, sin, sign):
    reps = y.shape[-1] // LANES
    c = jnp.tile(cos, (1, reps)) if reps > 1 else cos
    s = jnp.tile(sin, (1, reps)) if reps > 1 else sin
    return y * c + sign * (_rot_half(y) * s)


def _matmul(a, b, *, ta=False, tb=False, out_dtype=F32, bias=None, resid=None, mul=None, mul_scale=1.0,
            relu2=False, rope=None, rope_cols=0, tm=512, tn=1024, tk=1024, name="mm"):
    M = a.shape[1] if ta else a.shape[0]
    K = a.shape[0] if ta else a.shape[1]
    N = b.shape[0] if tb else b.shape[1]
    assert (b.shape[1] if tb else b.shape[0]) == K
    tm, tn, tk = _pick(M, tm), _pick(N, tn), _pick(K, tk)
    nk = K // tk
    dims = (((0 if ta else 1,), (1 if tb else 0,)), ((), ()))

    def body(*refs):
        it = iter(refs)
        a_ref, b_ref = next(it), next(it)
        bias_ref = next(it) if bias is not None else None
        resid_ref = next(it) if resid is not None else None
        mul_ref = next(it) if mul is not None else None
        cos_ref, sin_ref = (next(it), next(it)) if rope is not None else (None, None)
        o_ref = next(it)
        o2_ref = next(it) if relu2 else None
        acc_ref = next(it)
        k = pl.program_id(2)
        part = lax.dot_general(_bf(a_ref[...]), _bf(b_ref[...]), dims, preferred_element_type=F32)

        @pl.when(k == 0)
        def _():
            acc_ref[...] = part

        @pl.when(k > 0)
        def _():
            acc_ref[...] += part

        @pl.when(k == nk - 1)
        def _():
            y = acc_ref[...]
            if bias_ref is not None:
                y = y + bias_ref[...]
            if rope is not None:
                col = pl.program_id(1) * tn + lax.broadcasted_iota(jnp.int32, y.shape, 1)
                y = jnp.where(col < rope_cols, _rope(y, cos_ref[...], sin_ref[...], 1.0), y)
            if mul_ref is not None:
                y = y * (mul_ref[...].astype(F32) * mul_scale)
            if resid_ref is not None:
                y = y + resid_ref[...]
            if relu2:
                r = jnp.maximum(y, 0.0)
                o_ref[...] = r.astype(o_ref.dtype)
                o2_ref[...] = (r * r).astype(o2_ref.dtype)
            else:
                o_ref[...] = y.astype(o_ref.dtype)

    a_spec = pl.BlockSpec((tk, tm), lambda i, j, k: (k, i)) if ta else pl.BlockSpec((tm, tk), lambda i, j, k: (i, k))
    b_spec = pl.BlockSpec((tn, tk), lambda i, j, k: (j, k)) if tb else pl.BlockSpec((tk, tn), lambda i, j, k: (k, j))
    mn_spec = pl.BlockSpec((tm, tn), lambda i, j, k: (i, j))
    in_specs, args = [a_spec, b_spec], [a, b]
    if bias is not None:
        in_specs.append(pl.BlockSpec((1, tn), lambda i, j, k: (0, j)))
        args.append(bias)
    if resid is not None:
        in_specs.append(mn_spec)
        args.append(resid)
    if mul is not None:
        in_specs.append(mn_spec)
        args.append(mul)
    if rope is not None:
        in_specs += [pl.BlockSpec((tm, LANES), lambda i, j, k: (i, 0))] * 2
        args += [rope[0], rope[1]]
    out_shape = SDS((M, N), out_dtype)
    out_specs = mn_spec
    if relu2:
        out_shape, out_specs = (out_shape, out_shape), (mn_spec, mn_spec)
    return pl.pallas_call(
        body, out_shape=out_shape, grid=(M // tm, N // tn, nk), in_specs=in_specs, out_specs=out_specs,
        scratch_shapes=[pltpu.VMEM((tm, tn), F32)], name=name,
        compiler_params=_params(("parallel", "parallel", "arbitrary")),
    )(*args)


def _colsum(x, name):
    T, N = x.shape
    tm = _pick(T, 1024, 8)

    def body(x_ref, o_ref):
        s = jnp.sum(x_ref[...].astype(F32), axis=0, keepdims=True)

        @pl.when(pl.program_id(0) == 0)
        def _():
            o_ref[...] = s

        @pl.when(pl.program_id(0) > 0)
        def _():
            o_ref[...] += s

    return pl.pallas_call(
        body, out_shape=SDS((1, N), F32), grid=(T // tm,),
        in_specs=[pl.BlockSpec((tm, N), lambda i: (i, 0))], out_specs=pl.BlockSpec((1, N), lambda i: (0, 0)),
        name=name, compiler_params=_params(("arbitrary",)),
    )(x)


def _rmsnorm_fwd(h, w, name):
    T, D = h.shape
    tm = _pick(T, 512, 8)

    def body(h_ref, w_ref, o_ref):
        x = h_ref[...]
        rstd = lax.rsqrt(jnp.mean(x * x, axis=-1, keepdims=True) + NORM_EPS)
        o_ref[...] = (x * rstd * w_ref[...]).astype(BF16)

    return pl.pallas_call(
        body, out_shape=SDS((T, D), BF16), grid=(T // tm,),
        in_specs=[pl.BlockSpec((tm, D), lambda i: (i, 0)), pl.BlockSpec((1, D), lambda i: (0, 0))],
        out_specs=pl.BlockSpec((tm, D), lambda i: (i, 0)), name=name, compiler_params=_params(("parallel",)),
    )(h, w.reshape(1, D))


def _rmsnorm_bwd(h, du, w, dres, name):
    T, D = h.shape
    tm = _pick(T, 512, 8)

    def body(h_ref, du_ref, w_ref, dres_ref, dh_ref, dw_ref):
        x = h_ref[...]
        du_ = du_ref[...].astype(F32)
        rstd = lax.rsqrt(jnp.mean(x * x, axis=-1, keepdims=True) + NORM_EPS)
        g = du_ * w_ref[...]
        dh_ref[...] = dres_ref[...] + rstd * g - x * (rstd * rstd * rstd) * jnp.mean(g * x, axis=-1, keepdims=True)
        dw = jnp.sum(du_ * x * rstd, axis=0, keepdims=True)

        @pl.when(pl.program_id(0) == 0)
        def _():
            dw_ref[...] = dw

        @pl.when(pl.program_id(0) > 0)
        def _():
            dw_ref[...] += dw

    row = pl.BlockSpec((tm, D), lambda i: (i, 0))
    vec = pl.BlockSpec((1, D), lambda i: (0, 0))
    return pl.pallas_call(
        body, out_shape=(SDS((T, D), F32), SDS((1, D), F32)), grid=(T // tm,),
        in_specs=[row, row, vec, row], out_specs=(row, vec), name=name, compiler_params=_params(("arbitrary",)),
    )(h, du, w.reshape(1, D), dres)


def _final_loss(h, target, w):
    T, D = h.shape
    tm = _pick(T, 512, 8)

    def body(h_ref, t_ref, w_ref, dh_ref, dw_ref, loss_ref):
        x = h_ref[...]
        rstd = lax.rsqrt(jnp.mean(x * x, axis=-1, keepdims=True) + NORM_EPS)
        xn = x * rstd
        err = xn * w_ref[...] - t_ref[...]
        part = 0.5 * jnp.sum(jnp.mean(err * err, axis=-1, keepdims=True), axis=0, keepdims=True)
        dy = err * (1.0 / D)
        g = dy * w_ref[...]
        dh_ref[...] = rstd * g - x * (rstd * rstd * rstd) * jnp.mean(g * x, axis=-1, keepdims=True)
        dw = jnp.sum(dy * xn, axis=0, keepdims=True)
        lp = jnp.broadcast_to(part, (1, LANES))

        @pl.when(pl.program_id(0) == 0)
        def _():
            dw_ref[...] = dw
            loss_ref[...] = lp

        @pl.when(pl.program_id(0) > 0)
        def _():
            dw_ref[...] += dw
            loss_ref[...] += lp

    row = pl.BlockSpec((tm, D), lambda i: (i, 0))
    vec = pl.BlockSpec((1, D), lambda i: (0, 0))
    return pl.pallas_call(
        body, out_shape=(SDS((T, D), F32), SDS((1, D), F32), SDS((1, LANES), F32)), grid=(T // tm,),
        in_specs=[row, row, vec], out_specs=(row, vec, pl.BlockSpec((1, LANES), lambda i: (0, 0))),
        name="final_loss", compiler_params=_params(("arbitrary",)),
    )(h, target, w.reshape(1, D))


def _band_mask(i_blk, max_dist, first_ok):
    qi = lax.broadcasted_iota(jnp.int32, (BLOCK, 2 * BLOCK), 0)
    kj = lax.broadcasted_iota(jnp.int32, (BLOCK, 2 * BLOCK), 1)
    dist = qi + BLOCK - kj
    ok = (dist >= 0) & (dist <= max_dist)
    return ok & ((kj >= BLOCK) | first_ok)


def _head_cols(t, h):
    return t[:, HEAD_DIM * h:HEAD_DIM * (h + 1)]


def _lane_place(cols):
    m = cols[0].shape[0]
    lane = lax.broadcasted_iota(jnp.int32, (m, LANES), 1)
    out = jnp.zeros((m, LANES), F32)
    for h, c in enumerate(cols):
        out = jnp.where(lane == h, c, out)
    return out


def _attn_specs(B, S, d, C, n_heads, n_kv, q_col, k_col, v_col):
    kvw = n_kv * HEAD_DIM
    qw = n_heads * HEAD_DIM
    cq, ck = (C // qw if d > 1 else 0), (C // kvw if d > 1 else 0)
    q_spec = pl.BlockSpec((1, BLOCK, qw), lambda b, r, i: (b, i, r * cq + q_col // qw))
    kc = pl.BlockSpec((1, BLOCK, kvw), lambda b, r, i: (b, i, r * ck + k_col // kvw))
    kp = pl.BlockSpec((1, BLOCK, kvw), lambda b, r, i: (b, jnp.maximum(i - 1, 0), r * ck + k_col // kvw))
    vc = pl.BlockSpec((1, BLOCK, kvw), lambda b, r, i: (b, i, r * ck + v_col // kvw))
    vp = pl.BlockSpec((1, BLOCK, kvw), lambda b, r, i: (b, jnp.maximum(i - 1, 0), r * ck + v_col // kvw))
    return q_spec, kp, kc, vp, vc


def _attn_fwd(qkv, B, S, d, *, n_heads, n_kv, q_col, k_col, v_col, max_dist, sinks, name):
    C = qkv.shape[1]
    Ls = S // d
    nb = Ls // BLOCK
    qw = n_heads * HEAD_DIM
    R = n_heads // n_kv
    qkv3 = qkv.reshape(B, Ls, d * C)
    scale = HEAD_DIM ** -0.5

    def body(*refs):
        if sinks is not None:
            sink_ref, q_ref, kp_ref, kc_ref, vp_ref, vc_ref, o_ref, lse_ref = refs
        else:
            q_ref, kp_ref, kc_ref, vp_ref, vc_ref, o_ref, lse_ref = refs
        i = pl.program_id(2)
        mask = _band_mask(i, max_dist, i > 0)
        q = q_ref[0]
        kk = jnp.concatenate([kp_ref[0], kc_ref[0]], axis=0)
        vv = jnp.concatenate([vp_ref[0], vc_ref[0]], axis=0)
        lses, tiles = [], []
        for pair in range(n_heads // 2):
            outs = []
            for h in (2 * pair, 2 * pair + 1):
                g = h // R
                s = lax.dot_general(_head_cols(q, h), _head_cols(kk, g), NT, preferred_element_type=F32) * scale
                s = jnp.where(mask, s, NEG)
                m = jnp.max(s, axis=-1, keepdims=True)
                if sinks is not None:
                    sk = sink_ref[h]
                    m = jnp.maximum(m, sk)
                p = jnp.exp(s - m)
                den = jnp.sum(p, axis=-1, keepdims=True)
                if sinks is not None:
                    den = den + jnp.exp(sk - m)
                lses.append(m + jnp.log(den))
                pn = (p / den).astype(BF16)
                outs.append(jnp.dot(pn, _head_cols(vv, g), preferred_element_type=F32))
            tiles.append(jnp.concatenate(outs, axis=-1))
        o_ref[0] = jnp.concatenate(tiles, axis=-1)
        lse_ref[0] = _lane_place(lses)

    specs = list(_attn_specs(B, S, d, C, n_heads, n_kv, q_col, k_col, v_col))
    args = [qkv3] * 5
    if sinks is not None:
        specs = [pl.BlockSpec(memory_space=pltpu.SMEM)] + specs
        args = [sinks] + args
    o3, lse3 = pl.pallas_call(
        body, out_shape=(SDS((B, Ls, d * qw), F32), SDS((B, Ls, d * LANES), F32)), grid=(B, d, nb), in_specs=specs,
        out_specs=(pl.BlockSpec((1, BLOCK, qw), lambda b, r, i: (b, i, r)), pl.BlockSpec((1, BLOCK, LANES), lambda b, r, i: (b, i, r))),
        name=name, compiler_params=_params(("parallel", "parallel", "parallel")),
    )(*args)
    return o3.reshape(B * S, qw), lse3.reshape(B * S, LANES)


def _attn_dq(qkv, do, lse, delta, cos, sin, B, S, d, *, n_heads, n_kv, q_col, k_col, v_col, max_dist, name):
    C = qkv.shape[1]
    Ls = S // d
    nb = Ls // BLOCK
    qw = n_heads * HEAD_DIM
    R = n_heads // n_kv
    scale = HEAD_DIM ** -0.5

    def body(q_ref, kp_ref, kc_ref, vp_ref, vc_ref, do_ref, lse_ref, dl_ref, cos_ref, sin_ref, dq_ref):
        i = pl.program_id(2)
        mask = _band_mask(i, max_dist, i > 0)
        q = q_ref[0]
        do_ = do_ref[0]
        kk = jnp.concatenate([kp_ref[0], kc_ref[0]], axis=0)
        vv = jnp.concatenate([vp_ref[0], vc_ref[0]], axis=0)
        lse_t, dl_t = lse_ref[0], dl_ref[0]
        tiles = []
        for pair in range(n_heads // 2):
            outs = []
            for h in (2 * pair, 2 * pair + 1):
                g = h // R
                kh = _head_cols(kk, g)
                s = lax.dot_general(_head_cols(q, h), kh, NT, preferred_element_type=F32) * scale
                p = jnp.where(mask, jnp.exp(s - lse_t[:, h:h + 1]), 0.0)
                dp = lax.dot_general(_head_cols(do_, h), _head_cols(vv, g), NT, preferred_element_type=F32)
                ds = p * (dp - dl_t[:, h:h + 1])
                outs.append(jnp.dot(ds.astype(BF16), kh, preferred_element_type=F32) * scale)
            tiles.append(jnp.concatenate(outs, axis=-1))
        dq = jnp.concatenate(tiles, axis=-1)
        dq_ref[0] = _rope(dq, cos_ref[0], sin_ref[0], -1.0).astype(BF16)

    qs, kp, kc, vp, vc = _attn_specs(B, S, d, C, n_heads, n_kv, q_col, k_col, v_col)
    row_q = pl.BlockSpec((1, BLOCK, qw), lambda b, r, i: (b, i, r))
    row_l = pl.BlockSpec((1, BLOCK, LANES), lambda b, r, i: (b, i, r))
    qkv3 = qkv.reshape(B, Ls, d * C)
    v3 = lambda t, w: t.reshape(B, Ls, d * w)
    dq3 = pl.pallas_call(
        body, out_shape=SDS((B, Ls, d * qw), BF16), grid=(B, d, nb),
        in_specs=[qs, kp, kc, vp, vc, row_q, row_l, row_l, row_l, row_l], out_specs=row_q,
        name=name, compiler_params=_params(("parallel", "parallel", "parallel")),
    )(qkv3, qkv3, qkv3, qkv3, qkv3, v3(do, qw), v3(lse, LANES), v3(delta, LANES), v3(cos, LANES), v3(sin, LANES))
    return dq3.reshape(B * S, qw)


def _attn_dkv(qkv, do, lse, delta, cos, sin, B, S, d, *, n_heads, n_kv, q_col, k_col, v_col, max_dist, name):
    C = qkv.shape[1]
    Ls = S // d
    nb = Ls // BLOCK
    qw = n_heads * HEAD_DIM
    kvw = n_kv * HEAD_DIM
    R = n_heads // n_kv
    scale = HEAD_DIM ** -0.5
    cq, ck = (C // qw if d > 1 else 0), (C // kvw if d > 1 else 0)

    def body(k_ref, v_ref, q0_ref, q1_ref, do0_ref, do1_ref, lse0_ref, lse1_ref, dl0_ref, dl1_ref, cos_ref, sin_ref,
             dk_ref, dv_ref):
        j = pl.program_id(2)
        qi = lax.broadcasted_iota(jnp.int32, (BLOCK, BLOCK), 0)
        kj = lax.broadcasted_iota(jnp.int32, (BLOCK, BLOCK), 1)
        dist0 = qi - kj
        dist1 = qi + BLOCK - kj
        mask0 = (dist0 >= 0) & (dist0 <= max_dist)
        mask1 = (dist1 <= max_dist) & (j + 1 < nb)
        kb, vb = k_ref[0], v_ref[0]
        sides = ((q0_ref[0], do0_ref[0], lse0_ref[0], dl0_ref[0], mask0), (q1_ref[0], do1_ref[0], lse1_ref[0], dl1_ref[0], mask1))
        dks, dvs = [], []
        for g in range(n_kv):
            kh, vh = _head_cols(kb, g), _head_cols(vb, g)
            dk = jnp.zeros((BLOCK, HEAD_DIM), F32)
            dv = jnp.zeros((BLOCK, HEAD_DIM), F32)
            for h in range(g * R, (g + 1) * R):
                for (q, do_, lse_t, dl_t, mask) in sides:
                    qh, doh = _head_cols(q, h), _head_cols(do_, h)
                    s = lax.dot_general(qh, kh, NT, preferred_element_type=F32) * scale
                    p = jnp.where(mask, jnp.exp(s - lse_t[:, h:h + 1]), 0.0)
                    dp = lax.dot_general(doh, vh, NT, preferred_element_type=F32)
                    ds = p * (dp - dl_t[:, h:h + 1])
                    dv = dv + lax.dot_general(p.astype(BF16), doh, TN, preferred_element_type=F32)
                    dk = dk + lax.dot_general(ds.astype(BF16), qh, TN, preferred_element_type=F32) * scale
            dks.append(dk)
            dvs.append(dv)
        dk_t = jnp.concatenate([jnp.concatenate(dks[2 * t:2 * t + 2], axis=-1) for t in range(n_kv // 2)], axis=-1)
        dv_t = jnp.concatenate([jnp.concatenate(dvs[2 * t:2 * t + 2], axis=-1) for t in range(n_kv // 2)], axis=-1)
        dk_ref[0] = _rope(dk_t, cos_ref[0], sin_ref[0], -1.0).astype(BF16)
        dv_ref[0] = dv_t.astype(BF16)

    nxt = lambda j: jnp.minimum(j + 1, nb - 1)
    k_spec = pl.BlockSpec((1, BLOCK, kvw), lambda b, r, j: (b, j, r * ck + k_col // kvw))
    v_spec = pl.BlockSpec((1, BLOCK, kvw), lambda b, r, j: (b, j, r * ck + v_col // kvw))
    q0 = pl.BlockSpec((1, BLOCK, qw), lambda b, r, j: (b, j, r * cq + q_col // qw))
    q1 = pl.BlockSpec((1, BLOCK, qw), lambda b, r, j: (b, nxt(j), r * cq + q_col // qw))
    w0 = lambda w: pl.BlockSpec((1, BLOCK, w), lambda b, r, j: (b, j, r))
    w1 = lambda w: pl.BlockSpec((1, BLOCK, w), lambda b, r, j: (b, nxt(j), r))
    qkv3 = qkv.reshape(B, Ls, d * C)
    v3 = lambda t, w: t.reshape(B, Ls, d * w)
    do3, lse3, dl3 = v3(do, qw), v3(lse, LANES), v3(delta, LANES)
    dk3, dv3 = pl.pallas_call(
        body, out_shape=(SDS((B, Ls, d * kvw), BF16), SDS((B, Ls, d * kvw), BF16)), grid=(B, d, nb),
        in_specs=[k_spec, v_spec, q0, q1, w0(qw), w1(qw), w0(LANES), w1(LANES), w0(LANES), w1(LANES), w0(LANES), w0(LANES)],
        out_specs=(w0(kvw), w0(kvw)), name=name, compiler_params=_params(("parallel", "parallel", "parallel")),
    )(qkv3, qkv3, qkv3, qkv3, do3, do3, lse3, lse3, dl3, dl3, v3(cos, LANES), v3(sin, LANES))
    return dk3.reshape(B * S, kvw), dv3.reshape(B * S, kvw)


def _head_expand():
    r = lax.broadcasted_iota(jnp.int32, (LANES, C_HEADS * HEAD_DIM), 0)
    c = lax.broadcasted_iota(jnp.int32, (LANES, C_HEADS * HEAD_DIM), 1)
    return jnp.where(c // HEAD_DIM == r, 1.0, 0.0).astype(F32)


def _delta(do, o, lse=None, sinks_row=None, name="delta"):
    T, W = do.shape
    tm = _pick(T, 512, 8)
    with_sink = sinks_row is not None

    def body(*refs):
        if with_sink:
            do_ref, o_ref, lse_ref, sk_ref, dl_ref, dob_ref, ds_ref = refs
        else:
            do_ref, o_ref, dl_ref, dob_ref = refs
        do_ = do_ref[...]
        dl = lax.dot_general(do_ * o_ref[...], _head_expand(), NT, preferred_element_type=F32, precision=HI)
        dl_ref[...] = dl
        dob_ref[...] = do_.astype(BF16)
        if with_sink:
            lane = lax.broadcasted_iota(jnp.int32, dl.shape, 1)
            contrib = jnp.where(lane < A_N_HEADS, -jnp.exp(sk_ref[...] - lse_ref[...]) * dl, 0.0)
            part = jnp.sum(contrib, axis=0, keepdims=True)

            @pl.when(pl.program_id(0) == 0)
            def _():
                ds_ref[...] = part

            @pl.when(pl.program_id(0) > 0)
            def _():
                ds_ref[...] += part

    row_w = pl.BlockSpec((tm, W), lambda i: (i, 0))
    row_l = pl.BlockSpec((tm, LANES), lambda i: (i, 0))
    vec_l = pl.BlockSpec((1, LANES), lambda i: (0, 0))
    if with_sink:
        return pl.pallas_call(
            body, out_shape=(SDS((T, LANES), F32), SDS((T, W), BF16), SDS((1, LANES), F32)), grid=(T // tm,),
            in_specs=[row_w, row_w, row_l, vec_l], out_specs=(row_l, row_w, vec_l), name=name,
            compiler_params=_params(("arbitrary",)),
        )(do, o, lse, sinks_row)
    return pl.pallas_call(
        body, out_shape=(SDS((T, LANES), F32), SDS((T, W), BF16)), grid=(T // tm,),
        in_specs=[row_w, row_w], out_specs=(row_l, row_w), name=name, compiler_params=_params(("parallel",)),
    )(do, o)


def _merge(os_, lses):
    T, W = os_[0].shape
    tm = _pick(T, 512, 8)

    def body(o0, o1, o2, l0, l1, l2, o_ref, lse_ref):
        ls = [l0[...], l1[...], l2[...]]
        m = jnp.maximum(jnp.maximum(ls[0], ls[1]), ls[2])
        ws = [jnp.exp(l - m) for l in ls]
        tot = ws[0] + ws[1] + ws[2]
        lse_ref[...] = m + jnp.log(tot)
        e = _head_expand()
        acc = jnp.zeros((tm, W), F32)
        for w, o in zip(ws, (o0, o1, o2)):
            acc = acc + jnp.dot(w / tot, e, preferred_element_type=F32, precision=HI) * o[...]
        o_ref[...] = acc

    row_w = pl.BlockSpec((tm, W), lambda i: (i, 0))
    row_l = pl.BlockSpec((tm, LANES), lambda i: (i, 0))
    return pl.pallas_call(
        body, out_shape=(SDS((T, W), F32), SDS((T, LANES), F32)), grid=(T // tm,),
        in_specs=[row_w] * 3 + [row_l] * 3, out_specs=(row_w, row_l), name="c_merge", compiler_params=_params(("parallel",)),
    )(*os_, *lses)


CONV_TC = 256


def _conv_pre(x, w, bias):
    row = lax.broadcasted_iota(jnp.int32, x.shape, 0)
    acc = x * w[SSM_CONV - 1:SSM_CONV, :] + bias
    for k in range(1, SSM_CONV):
        acc = acc + jnp.where(row >= k, pltpu.roll(x, k, 0), 0.0) * w[SSM_CONV - 1 - k:SSM_CONV - k, :]
    return acc


def _conv_fwd(zx3, w, bias):
    B, S, _ = zx3.shape
    off = SSM_D_INNER // CONV_TC

    def body(x_ref, w_ref, b_ref, o_ref):
        v = _conv_pre(x_ref[0], w_ref[...], b_ref[...])
        o_ref[0] = v * jax.nn.sigmoid(v)

    return pl.pallas_call(
        body, out_shape=SDS((B, S, SSM_CONV_DIM), F32), grid=(B, SSM_CONV_DIM // CONV_TC),
        in_specs=[pl.BlockSpec((1, S, CONV_TC), lambda b, j: (b, 0, j + off)),
                  pl.BlockSpec((SSM_CONV, CONV_TC), lambda b, j: (0, j)), pl.BlockSpec((1, CONV_TC), lambda b, j: (0, j))],
        out_specs=pl.BlockSpec((1, S, CONV_TC), lambda b, j: (b, 0, j)), name="b_conv_fwd",
        compiler_params=_params(("parallel", "parallel")),
    )(zx3, w, bias)


def _conv_bwd(zx3, dxc, w, bias, col0, name):
    B, S, n = dxc.shape
    tc = _pick(n, CONV_TC)
    off_x = (SSM_D_INNER + col0) // tc
    off_w = col0 // tc

    def body(x_ref, d_ref, w_ref, b_ref, dx_ref, dw_ref, db_ref):
        x = x_ref[0]
        wv = w_ref[...]
        v = _conv_pre(x, wv, b_ref[...])
        sg = jax.nn.sigmoid(v)
        dc = d_ref[0] * (sg * (1.0 + v * (1.0 - sg)))
        row = lax.broadcasted_iota(jnp.int32, x.shape, 0)
        dx = dc * wv[SSM_CONV - 1:SSM_CONV, :]
        dws = [jnp.sum(dc * x, axis=0, keepdims=True)]
        for k in range(1, SSM_CONV):
            dx = dx + jnp.where(row < S - k, pltpu.roll(dc, S - k, 0), 0.0) * wv[SSM_CONV - 1 - k:SSM_CONV - k, :]
            dws.append(jnp.sum(dc * jnp.where(row >= k, pltpu.roll(x, k, 0), 0.0), axis=0, keepdims=True))
        dx_ref[0] = dx.astype(BF16)
        ridx = lax.broadcasted_iota(jnp.int32, (SSM_CONV, tc), 0)
        dw = jnp.zeros((SSM_CONV, tc), F32)
        for k in range(SSM_CONV):
            dw = jnp.where(ridx == SSM_CONV - 1 - k, dws[k], dw)
        db = jnp.sum(dc, axis=0, keepdims=True)

        @pl.when(pl.program_id(1) == 0)
        def _():
            dw_ref[...] = dw
            db_ref[...] = db

        @pl.when(pl.program_id(1) > 0)
        def _():
            dw_ref[...] += dw
            db_ref[...] += db

    return pl.pallas_call(
        body, out_shape=(SDS((B, S, n), BF16), SDS((SSM_CONV, n), F32), SDS((1, n), F32)), grid=(n // tc, B),
        in_specs=[pl.BlockSpec((1, S, tc), lambda j, b: (b, 0, j + off_x)), pl.BlockSpec((1, S, tc), lambda j, b: (b, 0, j)),
                  pl.BlockSpec((SSM_CONV, tc), lambda j, b: (0, j + off_w)), pl.BlockSpec((1, tc), lambda j, b: (0, j + off_w))],
        out_specs=(pl.BlockSpec((1, S, tc), lambda j, b: (b, 0, j)), pl.BlockSpec((SSM_CONV, tc), lambda j, b: (0, j)),
                   pl.BlockSpec((1, tc), lambda j, b: (0, j))),
        name=name, compiler_params=_params(("parallel", "arbitrary")),
    )(zx3, dxc, w, bias)


def _ssd_common(x, Bm, Cm, dtc_raw, dtr_raw, pr, pc):
    Q = SSM_CHUNK
    zc = dtc_raw + pr[0:1, :]
    dt_c = jax.nn.softplus(zc)
    dt_r = jax.nn.softplus(dtr_raw + pc[:, 0:1])
    A_r = -jnp.exp(pr[1:2, :])
    A_c = -jnp.exp(pc[:, 1:2])
    row = lax.broadcasted_iota(jnp.int32, (Q, Q), 0)
    col = lax.broadcasted_iota(jnp.int32, (Q, Q), 1)
    tril = jnp.where(row >= col, 1.0, 0.0).astype(F32)
    cs_c = jnp.dot(tril, dt_c * A_r, preferred_element_type=F32, precision=HI)
    cs_r = lax.dot_general(dt_r * A_c, tril, NT, preferred_element_type=F32, precision=HI)
    return zc, dt_c, A_r, cs_c, cs_r, row, col, tril


def _ssd_fwd(xc3, dtc, dtr, prow, pcol):
    B, S, _ = xc3.shape
    Q, G, HG, P, N = SSM_CHUNK, SSM_N_GROUPS, SSM_HG, HEAD_DIM, SSM_D_STATE
    nc = S // Q
    xw = HG * P

    def body(x_ref, b_ref, c_ref, dtc_ref, dtr_ref, pr_ref, pc_ref, y_ref, st_ref, state):
        c = pl.program_id(2)

        @pl.when(c == 0)
        def _():
            state[...] = jnp.zeros_like(state)

        x, Bm, Cm = x_ref[0], b_ref[0], c_ref[0]
        pr = pr_ref[0]
        _, dt_c, _, cs_c, cs_r, row, col, _ = _ssd_common(x, Bm, Cm, dtc_ref[0, 0], dtr_ref[0, 0], pr, pc_ref[0])
        Bb, Cb = Bm.astype(BF16), Cm.astype(BF16)
        CB = lax.dot_general(Cb, Bb, NT, preferred_element_type=F32)
        ys = []
        for hg in range(HG):
            xh = x[:, P * hg:P * (hg + 1)]
            xt = xh * dt_c[:, hg:hg + 1]
            csc, csr = cs_c[:, hg:hg + 1], cs_r[hg:hg + 1, :]
            L = jnp.where(row >= col, jnp.exp(jnp.minimum(csc - csr, 0.0)), 0.0)
            ydiag = jnp.dot((CB * L).astype(BF16), xt.astype(BF16), preferred_element_type=F32)
            Sh = state[hg]
            yoff = lax.dot_general(Cb, Sh.astype(BF16), NT, preferred_element_type=F32) * jnp.exp(csc)
            ys.append(ydiag + yoff + pr[2:3, hg:hg + 1] * xh)
            st_ref[0, 0, 0, P * hg:P * (hg + 1), :] = Sh
            csq = csc[Q - 1:Q, :]
            upd = lax.dot_general((xt * jnp.exp(csq - csc)).astype(BF16), Bb, TN, preferred_element_type=F32)
            state[hg] = Sh * jnp.exp(csq) + upd
        y_ref[0] = jnp.concatenate([jnp.concatenate(ys[0:2], axis=-1), jnp.concatenate(ys[2:4], axis=-1)], axis=-1)

    xo, bo, co = 0, SSM_D_INNER // N, (SSM_D_INNER + SSM_BC_DIM) // N
    return pl.pallas_call(
        body, out_shape=(SDS((B, S, SSM_D_INNER), F32), SDS((B, G, nc, xw, N), F32)), grid=(G, B, nc),
        in_specs=[pl.BlockSpec((1, Q, xw), lambda g, b, c: (b, c, g)), pl.BlockSpec((1, Q, N), lambda g, b, c: (b, c, bo + g)),
                  pl.BlockSpec((1, Q, N), lambda g, b, c: (b, c, co + g)), pl.BlockSpec((1, 1, Q, HG), lambda g, b, c: (b, g, c, 0)),
                  pl.BlockSpec((1, 1, HG, Q), lambda g, b, c: (b, g, 0, c)), pl.BlockSpec((1, 3, HG), lambda g, b, c: (g, 0, 0)),
                  pl.BlockSpec((1, HG, 3), lambda g, b, c: (g, 0, 0))],
        out_specs=(pl.BlockSpec((1, Q, xw), lambda g, b, c: (b, c, g)), pl.BlockSpec((1, 1, 1, xw, N), lambda g, b, c: (b, g, c, 0, 0))),
        scratch_shapes=[pltpu.VMEM((HG, P, N), F32)], name="b_ssd_fwd",
        compiler_params=_params(("parallel", "arbitrary", "arbitrary")),
    )(xc3, xc3, xc3, dtc, dtr, prow, pcol)


def _ssd_bwd(xc3, dtc, dtr, prow, pcol, states, dy3):
    B, S, _ = xc3.shape
    Q, G, HG, P, N = SSM_CHUNK, SSM_N_GROUPS, SSM_HG, HEAD_DIM, SSM_D_STATE
    nc = S // Q
    xw = HG * P

    def body(x_ref, b_ref, c_ref, dtc_ref, dtr_ref, pr_ref, pc_ref, st_ref, dy_ref,
             dx_ref, db_ref, dc_ref, ddt_ref, dpar_ref, dstate):
        bi, ci = pl.program_id(1), pl.program_id(2)

        @pl.when(ci == 0)
        def _():
            dstate[...] = jnp.zeros_like(dstate)

        x, Bm, Cm, dy = x_ref[0], b_ref[0], c_ref[0], dy_ref[0]
        pr = pr_ref[0]
        zc, dt_c, A_r, cs_c, cs_r, row, col, tril = _ssd_common(x, Bm, Cm, dtc_ref[0, 0], dtr_ref[0, 0], pr, pc_ref[0])
        Bb, Cb = Bm.astype(BF16), Cm.astype(BF16)
        CB = lax.dot_general(Cb, Bb, NT, preferred_element_type=F32)
        CBt = lax.dot_general(Bb, Cb, NT, preferred_element_type=F32)
        lane4 = lax.broadcasted_iota(jnp.int32, (Q, HG), 1)
        lane4r = lax.broadcasted_iota(jnp.int32, (1, HG), 1)
        rowq = lax.broadcasted_iota(jnp.int32, (Q, 1), 0)
        dB = jnp.zeros((Q, N), F32)
        dC = jnp.zeros((Q, N), F32)
        dcs4 = jnp.zeros((Q, HG), F32)
        dtx4 = jnp.zeros((Q, HG), F32)
        dD4 = jnp.zeros((1, HG), F32)
        dxts, xhs, dyhs = [], [], []
        for hg in range(HG):
            xh = x[:, P * hg:P * (hg + 1)]
            dyh = dy[:, P * hg:P * (hg + 1)]
            xt = xh * dt_c[:, hg:hg + 1]
            xtb, dyb = xt.astype(BF16), dyh.astype(BF16)
            csc, csr = cs_c[:, hg:hg + 1], cs_r[hg:hg + 1, :]
            L = jnp.where(row >= col, jnp.exp(jnp.minimum(csc - csr, 0.0)), 0.0)
            Lt = jnp.where(col >= row, jnp.exp(jnp.minimum(csr - csc, 0.0)), 0.0)
            M, Mt = CB * L, CBt * Lt
            Sh = st_ref[0, 0, 0, P * hg:P * (hg + 1), :]
            dSh = dstate[hg]
            Shb, dShb = Sh.astype(BF16), dSh.astype(BF16)
            ecs = jnp.exp(csc)
            csq = csc[Q - 1:Q, :]
            dec = jnp.exp(csq - csc)
            dxt = jnp.dot(Mt.astype(BF16), dyb, preferred_element_type=F32)
            dxt = dxt + lax.dot_general(Bb, dShb, NT, preferred_element_type=F32) * dec
            Gm = lax.dot_general(dyb, xtb, NT, preferred_element_type=F32)
            Gt = lax.dot_general(xtb, dyb, NT, preferred_element_type=F32)
            dC = dC + jnp.dot((Gm * L).astype(BF16), Bb, preferred_element_type=F32)
            dB = dB + jnp.dot((Gt * Lt).astype(BF16), Cb, preferred_element_type=F32)
            dC = dC + jnp.dot(dyb, Shb, preferred_element_type=F32) * ecs
            dBst = jnp.dot(xtb, dShb, preferred_element_type=F32) * dec
            dB = dB + dBst
            dcs = jnp.sum(Gm * M, axis=1, keepdims=True) - jnp.sum(Gt * Mt, axis=1, keepdims=True)
            yoff = lax.dot_general(Cb, Shb, NT, preferred_element_type=F32) * ecs
            dcs = dcs + jnp.sum(yoff * dyh, axis=1, keepdims=True)
            r = jnp.sum(dBst * Bm, axis=1, keepdims=True)
            dcs = dcs - r
            extra = jnp.sum(r, axis=0, keepdims=True) + jnp.exp(csq) * jnp.sum(
                jnp.sum(dSh * Sh, axis=1, keepdims=True), axis=0, keepdims=True)
            dcs = dcs + jnp.where(rowq == Q - 1, extra, 0.0)
            dcs4 = jnp.where(lane4 == hg, dcs, dcs4)
            dtx4 = jnp.where(lane4 == hg, jnp.sum(dxt * xh, axis=1, keepdims=True), dtx4)
            dD4 = jnp.where(lane4r == hg, jnp.sum(jnp.sum(dyh * xh, axis=1, keepdims=True), axis=0, keepdims=True), dD4)
            dstate[hg] = dSh * jnp.exp(csq) + lax.dot_general((dyh * ecs).astype(BF16), Cb, TN, preferred_element_type=F32)
            dxts.append(dxt)
            xhs.append(xh)
            dyhs.append(dyh)
        da4 = lax.dot_general(tril, dcs4, TN, preferred_element_type=F32, precision=HI)
        ddt4 = da4 * A_r + dtx4
        ddtraw = ddt4 * jax.nn.sigmoid(zc)
        ddt_ref[0, 0] = ddtraw
        dxs = [dxts[hg] * dt_c[:, hg:hg + 1] + pr[2:3, hg:hg + 1] * dyhs[hg] for hg in range(HG)]
        dx_ref[0] = jnp.concatenate([jnp.concatenate(dxs[0:2], axis=-1), jnp.concatenate(dxs[2:4], axis=-1)], axis=-1)
        db_ref[0] = dB
        dc_ref[0] = dC
        d_bias = jnp.sum(ddtraw, axis=0, keepdims=True)
        d_alog = jnp.sum(da4 * dt_c, axis=0, keepdims=True) * A_r
        r3 = lax.broadcasted_iota(jnp.int32, (3, HG), 0)
        dpar = jnp.where(r3 == 0, d_bias, jnp.where(r3 == 1, d_alog, dD4))
        first = (bi == 0) & (ci == 0)

        @pl.when(first)
        def _():
            dpar_ref[0] = dpar

        @pl.when(jnp.logical_not(first))
        def _():
            dpar_ref[0] += dpar

    rc = lambda c: nc - 1 - c
    bo, co = SSM_D_INNER // N, (SSM_D_INNER + SSM_BC_DIM) // N
    return pl.pallas_call(
        body,
        out_shape=(SDS((B, S, SSM_D_INNER), F32), SDS((B, S, SSM_BC_DIM), F32), SDS((B, S, SSM_BC_DIM), F32),
                   SDS((B, G, S, HG), F32), SDS((G, 3, HG), F32)),
        grid=(G, B, nc),
        in_specs=[pl.BlockSpec((1, Q, xw), lambda g, b, c: (b, rc(c), g)), pl.BlockSpec((1, Q, N), lambda g, b, c: (b, rc(c), bo + g)),
                  pl.BlockSpec((1, Q, N), lambda g, b, c: (b, rc(c), co + g)), pl.BlockSpec((1, 1, Q, HG), lambda g, b, c: (b, g, rc(c), 0)),
                  pl.BlockSpec((1, 1, HG, Q), lambda g, b, c: (b, g, 0, rc(c))), pl.BlockSpec((1, 3, HG), lambda g, b, c: (g, 0, 0)),
                  pl.BlockSpec((1, HG, 3), lambda g, b, c: (g, 0, 0)),
                  pl.BlockSpec((1, 1, 1, xw, N), lambda g, b, c: (b, g, rc(c), 0, 0)), pl.BlockSpec((1, Q, xw), lambda g, b, c: (b, rc(c), g))],
        out_specs=(pl.BlockSpec((1, Q, xw), lambda g, b, c: (b, rc(c), g)), pl.BlockSpec((1, Q, N), lambda g, b, c: (b, rc(c), g)),
                   pl.BlockSpec((1, Q, N), lambda g, b, c: (b, rc(c), g)), pl.BlockSpec((1, 1, Q, HG), lambda g, b, c: (b, g, rc(c), 0)),
                   pl.BlockSpec((1, 3, HG), lambda g, b, c: (g, 0, 0))),
        scratch_shapes=[pltpu.VMEM((HG, P, N), F32)], name="b_ssd_bwd",
        compiler_params=_params(("parallel", "arbitrary", "arbitrary")),
    )(xc3, xc3, xc3, dtc, dtr, prow, pcol, states, dy3)


GN_W = SSM_D_INNER // SSM_N_GROUPS


def _gate_fwd(y, zx, nw):
    T = y.shape[0]
    tm = _pick(T, 256, 8)

    def body(y_ref, z_ref, w_ref, o_ref):
        z = z_ref[...]
        gt = y_ref[...] * (z * jax.nn.sigmoid(z))
        outs = []
        for k in range(SSM_N_GROUPS):
            gk = gt[:, GN_W * k:GN_W * (k + 1)]
            outs.append(gk * lax.rsqrt(jnp.mean(gk * gk, axis=-1, keepdims=True) + NORM_EPS))
        o_ref[...] = (jnp.concatenate(outs, axis=-1) * w_ref[...]).astype(BF16)

    row = pl.BlockSpec((tm, SSM_D_INNER), lambda i: (i, 0))
    return pl.pallas_call(
        body, out_shape=SDS((T, SSM_D_INNER), BF16), grid=(T // tm,),
        in_specs=[row, row, pl.BlockSpec((1, SSM_D_INNER), lambda i: (0, 0))], out_specs=row, name="b_gate_fwd",
        compiler_params=_params(("parallel",)),
    )(y, zx, nw)


def _gate_bwd(dgn, y, zx, nw):
    T = y.shape[0]
    tm = _pick(T, 256, 8)

    def body(d_ref, y_ref, z_ref, w_ref, dy_ref, dz_ref, dw_ref):
        z, yv, w = z_ref[...], y_ref[...], w_ref[...]
        sg = jax.nn.sigmoid(z)
        sz = z * sg
        gt = yv * sz
        gw = d_ref[...] * w
        dgts, dws = [], []
        for k in range(SSM_N_GROUPS):
            sl = slice(GN_W * k, GN_W * (k + 1))
            gk, gwk = gt[:, sl], gw[:, sl]
            rstd = lax.rsqrt(jnp.mean(gk * gk, axis=-1, keepdims=True) + NORM_EPS)
            dgts.append(rstd * gwk - gk * (rstd * rstd * rstd) * jnp.mean(gwk * gk, axis=-1, keepdims=True))
            dws.append(jnp.sum(d_ref[:, sl] * gk * rstd, axis=0, keepdims=True))
        dgt = jnp.concatenate(dgts, axis=-1)
        dy_ref[...] = dgt * sz
        dz_ref[...] = (dgt * yv * (sg * (1.0 + z * (1.0 - sg)))).astype(BF16)
        dw = jnp.concatenate(dws, axis=-1)

        @pl.when(pl.program_id(0) == 0)
        def _():
            dw_ref[...] = dw

        @pl.when(pl.program_id(0) > 0)
        def _():
            dw_ref[...] += dw

    row = pl.BlockSpec((tm, SSM_D_INNER), lambda i: (i, 0))
    vec = pl.BlockSpec((1, SSM_D_INNER), lambda i: (0, 0))
    return pl.pallas_call(
        body, out_shape=(SDS((T, SSM_D_INNER), F32), SDS((T, SSM_D_INNER), BF16), SDS((1, SSM_D_INNER), F32)), grid=(T // tm,),
        in_specs=[row, row, row, vec], out_specs=(row, row, vec), name="b_gate_bwd", compiler_params=_params(("arbitrary",)),
    )(dgn, y, zx, nw)


MESH = pl.DeviceIdType.MESH
ANY = pl.BlockSpec(memory_space=pl.ANY)


N_CHIPS = 4


def _dev_block(ref, kind, j, size):
    if kind == "slot":
        return ref.at[j]
    start = pl.multiple_of(j * size, size)
    nd = len(ref.shape)
    if kind == "col":
        return ref.at[(slice(None),) * (nd - 1) + (pl.ds(start, size),)]
    return ref.at[(slice(None),) * (nd - 2) + (pl.ds(start, size), slice(None))]


def _dma_sems(n, k):
    return [pltpu.SemaphoreType.DMA((n, k)), pltpu.SemaphoreType.DMA((n, k)), pltpu.SemaphoreType.DMA((n, k))]


def _place(shard, layer, kind, full_shape, dev, name):
    k, n = shard.shape[1:]
    tr = _pick(k, 512, 16)
    nb = k // tr

    def body(dev_ref, s_ref, o_ref):
        if kind == "slot":
            o_ref[0] = s_ref[0].astype(BF16)
        else:
            o_ref[...] = s_ref[0].astype(BF16)

    out_spec = {"slot": pl.BlockSpec((1, tr, n), lambda i, d: (d[0], i, 0)),
                "row": pl.BlockSpec((tr, n), lambda i, d: (d[0] * nb + i, 0)),
                "col": pl.BlockSpec((tr, n), lambda i, d: (i, d[0]))}[kind]
    return pl.pallas_call(
        body, out_shape=SDS(full_shape, BF16),
        grid_spec=pltpu.PrefetchScalarGridSpec(
            num_scalar_prefetch=1, grid=(nb,), in_specs=[pl.BlockSpec((1, tr, n), lambda i, d: (layer, i, 0))], out_specs=out_spec),
        name=name, compiler_params=_params(("arbitrary",)),
    )(dev, shard)


def _gather(items, name):
    n = len(items)

    def body(*refs):
        srcs, dsts = refs[:n], refs[n:2 * n]
        send_sems, recv_sems, local_sems = refs[2 * n:]
        px, py, pc = lax.axis_index("x"), lax.axis_index("y"), lax.axis_index("c")
        me, sibling = (px, py, pc), (px, py, 1 - pc)
        chips = [(1 - px, py), (px, 1 - py), (1 - px, 1 - py)]

        def blk(a, dev):
            return _dev_block(dsts[a], items[a][1], 4 * dev[0] + 2 * dev[1] + dev[2], items[a][2])

        def src_of(a):
            return blk(a, me) if items[a][4] else srcs[a]

        def copy(a, k, block, to, src=None):
            return pltpu.make_async_remote_copy(
                src_ref=blk(a, block) if src is None else src, dst_ref=blk(a, block),
                send_sem=send_sems.at[a, k], recv_sem=recv_sems.at[a, k], device_id=to, device_id_type=MESH)

        mine = [pltpu.make_async_copy(srcs[a], blk(a, me), local_sems.at[a, 0]) for a in range(n) if not items[a][4]]
        for cp in mine:
            cp.start()
        first = []
        for a in range(n):
            first.append(copy(a, 0, me, sibling, src=src_of(a)))
            first += [copy(a, 1 + j, me, (*chip, pc), src=src_of(a)) for j, chip in enumerate(chips)]
        for cp in first:
            cp.start()
        passed = []
        for j, chip in enumerate(chips):
            for a in range(n):
                copy(a, 1 + j, (*chip, pc), me).wait_recv()
                fwd = copy(a, 4 + j, (*chip, pc), sibling)
                fwd.start()
                passed.append(fwd)
        for a in range(n):
            copy(a, 0, sibling, me).wait_recv()
            for j, chip in enumerate(chips):
                copy(a, 4 + j, (*chip, 1 - pc), me).wait_recv()
        for cp in first + passed:
            cp.wait_send()
        for cp in mine:
            cp.wait()

    return pl.pallas_call(
        body, out_shape=[SDS(it[3], it[0].dtype) for it in items], in_specs=[ANY] * n, out_specs=[ANY] * n,
        input_output_aliases={a: a for a in range(n) if items[a][4]}, scratch_shapes=_dma_sems(n, 7), name=name,
    )(*[it[0] for it in items])


def _reduce_d2d(items, name):
    n = len(items)

    def body(*refs):
        gs, gots = refs[:n], refs[n:2 * n]
        send_sems, recv_sems, _ = refs[2 * n:]
        px, py, pc = lax.axis_index("x"), lax.axis_index("y"), lax.axis_index("c")
        copies = []
        for a in range(n):
            _, kind, size, _ = items[a]
            for q in range(N_CHIPS):
                copies.append(pltpu.make_async_remote_copy(
                    src_ref=_dev_block(gs[a], kind, 2 * q + 1 - pc, size), dst_ref=gots[a].at[q], send_sem=send_sems.at[a, q],
                    recv_sem=recv_sems.at[a, q], device_id=(px, py, 1 - pc), device_id_type=MESH))
        for cp in copies:
            cp.start()
        for cp in copies:
            cp.wait()

    return pl.pallas_call(
        body, out_shape=[SDS((N_CHIPS,) + tuple(it[3]), F32) for it in items], in_specs=[ANY] * n, out_specs=[ANY] * n,
        scratch_shapes=_dma_sems(n, N_CHIPS), name=name,
    )(*[it[0] for it in items])


def _pair_sum(g, got, kind, core, name):
    _, k, n = got.shape
    tr = _pick(k, max(16, STREAM_VMEM // (2 * n * 10)), 16)
    nb = k // tr

    def body(c_ref, g_ref, s_ref, o_ref):
        mine = g_ref[0] if kind == "slot" else g_ref[...]
        o_ref[0] = (mine + s_ref[0]).astype(BF16)

    g_spec = {"slot": pl.BlockSpec((1, tr, n), lambda q, i, c: (2 * q + c[0], i, 0)),
              "row": pl.BlockSpec((tr, n), lambda q, i, c: ((2 * q + c[0]) * nb + i, 0)),
              "col": pl.BlockSpec((tr, n), lambda q, i, c: (i, 2 * q + c[0]))}[kind]
    part = pl.BlockSpec((1, tr, n), lambda q, i, c: (q, i, 0))
    return pl.pallas_call(
        body, out_shape=SDS((N_CHIPS, k, n), BF16),
        grid_spec=pltpu.PrefetchScalarGridSpec(num_scalar_prefetch=1, grid=(N_CHIPS, nb), in_specs=[g_spec, part], out_specs=part),
        name=name, compiler_params=_params(("arbitrary", "arbitrary")),
    )(core, g, got)


def _reduce_ici(parts, name):
    n = len(parts)

    def body(*refs):
        ps, rs = refs[:n], refs[n:2 * n]
        send_sems, recv_sems, _ = refs[2 * n:]
        px, py, pc = lax.axis_index("x"), lax.axis_index("y"), lax.axis_index("c")
        my_chip = 2 * px + py
        sends, recvs = [], []
        for a in range(n):
            for k in range(1, N_CHIPS):
                qx, qy = px ^ (k >> 1), py ^ (k & 1)
                q = 2 * qx + qy
                kw = dict(send_sem=send_sems.at[a, k - 1], recv_sem=recv_sems.at[a, k - 1], device_id=(qx, qy, pc),
                          device_id_type=MESH)
                sends.append(pltpu.make_async_remote_copy(src_ref=ps[a].at[q], dst_ref=rs[a].at[my_chip], **kw))
                recvs.append(pltpu.make_async_remote_copy(src_ref=ps[a].at[q], dst_ref=rs[a].at[q], **kw))
        for cp in sends:
            cp.start()
        for cp in recvs:
            cp.wait_recv()
        for cp in sends:
            cp.wait_send()

    return pl.pallas_call(
        body, out_shape=[SDS(p.shape, p.dtype) for p in parts], in_specs=[ANY] * n, out_specs=[ANY] * n,
        scratch_shapes=_dma_sems(n, N_CHIPS - 1), name=name,
    )(*parts)


def _adam_update(g, w, m, v):
    c1 = 1.0 - ADAM_B1 ** ADAM_STEP
    c2 = 1.0 - ADAM_B2 ** ADAM_STEP
    nm = ADAM_B1 * m + (1.0 - ADAM_B1) * g
    nv = ADAM_B2 * v + (1.0 - ADAM_B2) * (g * g)
    delta = -ADAM_LR * ((nm / c1) / (jnp.sqrt(nv / c2) + ADAM_EPS) + ADAM_WD * w)
    return delta, nm, nv


def _adamw(parts, recv, w, m, v, chip, name):
    R, C = w.shape
    row_bytes = 2 * C * (N_CHIPS * 2 + 7 * 4)
    tr = _pick(R, max(16, STREAM_VMEM // row_bytes), 16)

    def body(ch_ref, own_ref, r1_ref, r2_ref, r3_ref, w_ref, m_ref, v_ref, g_ref, d_ref, nm_ref, nv_ref):
        g = own_ref[0].astype(F32)
        for r_ref in (r1_ref, r2_ref, r3_ref):
            g = g + r_ref[0].astype(F32)
        g_ref[...] = g
        d_ref[...], nm_ref[...], nv_ref[...] = _adam_update(g, w_ref[...], m_ref[...], v_ref[...])

    row = pl.BlockSpec((tr, C), lambda i, ch: (i, 0))
    other = lambda k: pl.BlockSpec((1, tr, C), lambda i, ch: (ch[0] ^ k, i, 0))
    out = SDS((R, C), F32)
    return pl.pallas_call(
        body, out_shape=(out, out, out, out),
        grid_spec=pltpu.PrefetchScalarGridSpec(
            num_scalar_prefetch=1, grid=(R // tr,),
            in_specs=[pl.BlockSpec((1, tr, C), lambda i, ch: (ch[0], i, 0)), other(2), other(1), other(3), row, row, row],
            out_specs=(row, row, row, row)),
        name=name, compiler_params=_params(("arbitrary",)),
    )(chip, parts, recv, recv, recv, w, m, v)


def _small_adamw(gathered, ws, ms, vs):
    n = len(ws)

    def body(*refs):
        g_in, w_in, m_in, v_in = refs[:n], refs[n:2 * n], refs[2 * n:3 * n], refs[3 * n:4 * n]
        outs = refs[4 * n:]
        for i in range(n):
            g = g_in[i][0]
            for dev in range(1, N_DEV):
                g = g + g_in[i][dev]
            d, nm, nv = _adam_update(g, w_in[i][...], m_in[i][...], v_in[i][...])
            outs[i][...] = g
            outs[n + i][...] = d
            outs[2 * n + i][...] = nm
            outs[3 * n + i][...] = nv

    shapes = [SDS(w.shape, F32) for w in ws]
    outs = pl.pallas_call(body, out_shape=shapes * 4, name="small_adamw")(*gathered, *ws, *ms, *vs)
    return outs[:n], outs[n:2 * n], outs[2 * n:3 * n], outs[3 * n:]


W_NAMES = ("norm_mix_w", "norm_mlp_w", "a_w_qkv", "a_b_qkv", "a_sinks", "a_w_o", "a_b_o", "b_in_w", "b_conv_w", "b_conv_b",
           "b_dt_bias", "b_a_log", "b_d", "b_norm_w", "b_out_w", "c_w_qkv", "c_w_o", "mlp_w_up", "mlp_w_down", "final_norm_w")
BIG_KIND = {"a_w_qkv": "slot", "a_w_o": "row", "b_in_w": "slot", "b_out_w": "row", "c_w_qkv": "col", "c_w_o": "row",
            "mlp_w_up": "col", "mlp_w_down": "row"}
SMALL_SHARDED = {"a_b_qkv": 1, "a_b_o": 1, "b_conv_w": 2}
SMALL_REPLICATED = ("norm_mix_w", "norm_mlp_w", "a_sinks", "b_conv_b", "b_dt_bias", "b_a_log", "b_d", "b_norm_w", "final_norm_w")


def _layer_big(i):
    kind, j = i % 3, i // 3
    mix = {0: [("a_w_qkv", j), ("a_w_o", j)], 1: [("b_in_w", 0), ("b_out_w", 0)], 2: [("c_w_qkv", 0), ("c_w_o", 0)]}[kind]
    return mix + [("mlp_w_up", i), ("mlp_w_down", i)]


def _block_size(kind, shard2d):
    return {"slot": None, "row": shard2d[0], "col": shard2d[1]}[kind]


def _full2d(kind, shard2d):
    k, n = shard2d
    return {"slot": (N_DEV, k, n), "row": (N_DEV * k, n), "col": (k, N_DEV * n)}[kind]


def _from_slots(t, ax):
    s = t.shape[1:]
    return jnp.moveaxis(t, 0, ax).reshape(s[:ax] + (N_DEV * s[ax],) + s[ax + 1:])


def _to_slots(g, ax):
    s = g.shape
    return jnp.moveaxis(g.reshape(s[:ax] + (N_DEV, s[ax] // N_DEV) + s[ax + 1:]), ax, 0)


def _rope_tables(positions):
    half = HEAD_DIM // 2
    inv = ROPE_THETA ** (-jnp.arange(half, dtype=F32) / half)
    ang = positions.astype(F32)[..., None] * inv
    rep = LANES // half
    cos = jnp.tile(jnp.cos(ang), (1, 1, rep))
    sin = jnp.tile(jnp.sin(ang), (1, 1, rep))
    T = positions.shape[0] * positions.shape[1]
    return cos.reshape(T, LANES), sin.reshape(T, LANES)


def _swa_fwd(u, h, p, j, B, S, cos, sin, tag):
    qkv = _matmul(u, p["a_w_qkv"][j], out_dtype=BF16, bias=p["a_b_qkv"][j][None], rope=(cos, sin),
                  rope_cols=A_Q_DIM + A_KV_DIM, tn=640, name=f"{tag}_qkv")
    o, lse = _attn_fwd(qkv, B, S, 1, n_heads=A_N_HEADS, n_kv=A_N_KV, q_col=0, k_col=A_Q_DIM, v_col=A_Q_DIM + A_KV_DIM,
                       max_dist=A_WINDOW - 1, sinks=p["a_sinks"][j], name=f"{tag}_attn")
    h1 = _matmul(o, p["a_w_o"][j], bias=p["a_b_o"][j][None], resid=h, name=f"{tag}_o")
    return h1, (qkv, o, lse)


def _swa_bwd(dh1, u, saved, p, j, B, S, cos, sin, tag):
    qkv, o, lse = saved
    kw = dict(n_heads=A_N_HEADS, n_kv=A_N_KV, q_col=0, k_col=A_Q_DIM, v_col=A_Q_DIM + A_KV_DIM, max_dist=A_WINDOW - 1)
    g = {}
    do = _matmul(dh1, p["a_w_o"][j], tb=True, name=f"{tag}_do")
    g["a_w_o"] = _matmul(o, dh1, ta=True, name=f"{tag}_dwo")
    g["a_b_o"] = _colsum(dh1, f"{tag}_dbo")[0]
    sk = jnp.pad(p["a_sinks"][j], (0, LANES - A_N_HEADS))[None]
    delta, dob, dsink = _delta(do, o, lse, sk, name=f"{tag}_delta")
    g["a_sinks"] = dsink[0, :A_N_HEADS]
    dq = _attn_dq(qkv, dob, lse, delta, cos, sin, B, S, 1, name=f"{tag}_dq", **kw)
    dk, dv = _attn_dkv(qkv, dob, lse, delta, cos, sin, B, S, 1, name=f"{tag}_dkv", **kw)
    dqkv = jnp.concatenate([dq, dk, dv], axis=1)
    g["a_w_qkv"] = _matmul(u, dqkv, ta=True, tn=640, name=f"{tag}_dwqkv")
    g["a_b_qkv"] = _colsum(dqkv, f"{tag}_dbqkv")[0]
    du = _matmul(dqkv, p["a_w_qkv"][j], tb=True, tk=640, name=f"{tag}_du")
    return du, g


def _dil_fwd(u, h, p, B, S, cos, sin):
    W = C_HEADS * HEAD_DIM
    qkv = _matmul(u, p["c_w_qkv"][0], out_dtype=BF16, rope=(cos, sin), rope_cols=6 * W, name="c_qkv")
    os_, lses = [], []
    for gi, (window, dil) in enumerate(C_PATTERNS):
        o, lse = _attn_fwd(qkv, B, S, dil, n_heads=C_HEADS, n_kv=C_HEADS, q_col=gi * W, k_col=(3 + gi) * W,
                           v_col=(6 + gi) * W, max_dist=window // dil, sinks=None, name=f"c_attn{gi}")
        os_.append(o)
        lses.append(lse)
    o, lse = _merge(os_, lses)
    h1 = _matmul(o, p["c_w_o"][0], resid=h, name="c_o")
    return h1, (qkv, o, lse)


def _dil_bwd(dh1, u, saved, p, B, S, cos, sin):
    W = C_HEADS * HEAD_DIM
    qkv, o, lse = saved
    g = {}
    do = _matmul(dh1, p["c_w_o"][0], tb=True, name="c_do")
    g["c_w_o"] = _matmul(o, dh1, ta=True, name="c_dwo")[None]
    delta, dob = _delta(do, o, name="c_delta")
    dqs, dks, dvs = [], [], []
    for gi, (window, dil) in enumerate(C_PATTERNS):
        kw = dict(n_heads=C_HEADS, n_kv=C_HEADS, q_col=gi * W, k_col=(3 + gi) * W, v_col=(6 + gi) * W, max_dist=window // dil)
        dqs.append(_attn_dq(qkv, dob, lse, delta, cos, sin, B, S, dil, name=f"c_dq{gi}", **kw))
        dk, dv = _attn_dkv(qkv, dob, lse, delta, cos, sin, B, S, dil, name=f"c_dkv{gi}", **kw)
        dks.append(dk)
        dvs.append(dv)
    dqkv = jnp.concatenate(dqs + dks + dvs, axis=1)
    g["c_w_qkv"] = _matmul(u, dqkv, ta=True, name="c_dwqkv")[None]
    du = _matmul(dqkv, p["c_w_qkv"][0], tb=True, name="c_du")
    return du, g


def _ssm_params(p):
    par = jnp.stack([p["b_dt_bias"][0], p["b_a_log"][0], p["b_d"][0]], axis=0)
    prow = par.reshape(3, SSM_N_GROUPS, SSM_HG).transpose(1, 0, 2)
    return prow, prow.transpose(0, 2, 1)


def _mamba_fwd(u, h, p, B, S):
    T = B * S
    G, HG = SSM_N_GROUPS, SSM_HG
    w_in = p["b_in_w"][0]
    nzx = SSM_D_INNER + SSM_CONV_DIM
    w_dt = jnp.pad(w_in[:, nzx:], ((0, 0), (0, LANES - SSM_N_HEADS)))
    zx = _matmul(u, w_in[:, :nzx], name="b_zx")
    dtraw = _matmul(u, w_dt, name="b_dt")[:, :SSM_N_HEADS]
    dtc = dtraw.reshape(B, S, G, HG).transpose(0, 2, 1, 3)
    dtr = dtraw.reshape(B, S, G, HG).transpose(0, 2, 3, 1)
    prow, pcol = _ssm_params(p)
    zx3 = zx.reshape(B, S, nzx)
    xc3 = _conv_fwd(zx3, p["b_conv_w"][0], p["b_conv_b"])
    y3, states = _ssd_fwd(xc3, dtc, dtr, prow, pcol)
    y = y3.reshape(T, SSM_D_INNER)
    gn = _gate_fwd(y, zx, p["b_norm_w"])
    h1 = _matmul(gn, p["b_out_w"][0], resid=h, name="b_out")
    return h1, (zx, dtc, dtr, xc3, y, states, gn, w_dt)


def _mamba_bwd(dh1, u, saved, p, B, S):
    T = B * S
    zx, dtc, dtr, xc3, y, states, gn, w_dt = saved
    nzx = SSM_D_INNER + SSM_CONV_DIM
    w_in = p["b_in_w"][0]
    prow, pcol = _ssm_params(p)
    g = {}
    dgn = _matmul(dh1, p["b_out_w"][0], tb=True, name="b_dgn")
    g["b_out_w"] = _matmul(gn, dh1, ta=True, name="b_dwout")[None]
    dy, dz, dnw = _gate_bwd(dgn, y, zx, p["b_norm_w"])
    g["b_norm_w"] = dnw
    dx3, dB3, dC3, ddt, dpar = _ssd_bwd(xc3, dtc, dtr, prow, pcol, states, dy.reshape(B, S, SSM_D_INNER))
    dpar = dpar.transpose(1, 0, 2).reshape(3, SSM_N_HEADS)
    g["b_dt_bias"], g["b_a_log"], g["b_d"] = dpar[0:1], dpar[1:2], dpar[2:3]
    zx3 = zx.reshape(B, S, nzx)
    cw, cb = p["b_conv_w"][0], p["b_conv_b"]
    parts, dws, dbs = [], [], []
    for col0, dpart, nm in ((0, dx3, "b_conv_bwd_x"), (SSM_D_INNER, dB3, "b_conv_bwd_b"),
                            (SSM_D_INNER + SSM_BC_DIM, dC3, "b_conv_bwd_c")):
        dxp, dw, db = _conv_bwd(zx3, dpart, cw, cb, col0, nm)
        parts.append(dxp.reshape(T, -1))
        dws.append(dw)
        dbs.append(db)
    g["b_conv_w"] = jnp.concatenate(dws, axis=1)[None]
    g["b_conv_b"] = jnp.concatenate(dbs, axis=1)
    dzx = jnp.concatenate([dz] + parts, axis=1)
    ddtraw = ddt.transpose(0, 2, 1, 3).reshape(T, SSM_N_HEADS)
    ddtp = jnp.pad(ddtraw, ((0, 0), (0, LANES - SSM_N_HEADS)))
    dw_zx = _matmul(u, dzx, ta=True, name="b_dwzx")
    dw_dt = _matmul(u, ddtp, ta=True, name="b_dwdt")[:, :SSM_N_HEADS]
    g["b_in_w"] = jnp.concatenate([dw_zx, dw_dt], axis=1)[None]
    du = _matmul(dzx, w_in[:, :nzx], tb=True, name="b_du_zx")
    du = _matmul(ddtp, w_dt, tb=True, resid=du, name="b_du_dt")
    return du, g


def _local_step(x, positions, p, target):
    B, S, D = x.shape
    T = B * S
    cos, sin = _rope_tables(positions)
    h = x.reshape(T, D)
    tape = []
    for i in range(DEPTH):
        kind, j = i % 3, i // 3
        u = _rmsnorm_fwd(h, p["norm_mix_w"][i], f"l{i}_norm_mix")
        if kind == 0:
            h1, saved = _swa_fwd(u, h, p, j, B, S, cos, sin, f"a{j}")
        elif kind == 1:
            h1, saved = _mamba_fwd(u, h, p, B, S)
        else:
            h1, saved = _dil_fwd(u, h, p, B, S, cos, sin)
        u2 = _rmsnorm_fwd(h1, p["norm_mlp_w"][i], f"l{i}_norm_mlp")
        r, s = _matmul(u2, p["mlp_w_up"][i], out_dtype=BF16, relu2=True, name=f"l{i}_up")
        h2 = _matmul(s, p["mlp_w_down"][i], resid=h1, name=f"l{i}_down")
        tape.append((h, u, saved, h1, u2, r, s))
        h = h2
    dh, dwf, loss = _final_loss(h, target.reshape(T, D), p["final_norm_w"])
    grads = {"final_norm_w": dwf[0]}
    per_layer = {n: [None] * DEPTH for n in ("norm_mix_w", "norm_mlp_w", "mlp_w_up", "mlp_w_down")}
    a_grads = [None, None]
    for i in reversed(range(DEPTH)):
        kind, j = i % 3, i // 3
        h0, u, saved, h1, u2, r, s = tape[i]
        da = _matmul(dh, p["mlp_w_down"][i], tb=True, out_dtype=BF16, mul=r, mul_scale=2.0, name=f"l{i}_da")
        per_layer["mlp_w_down"][i] = _matmul(s, dh, ta=True, name=f"l{i}_dwdown")
        per_layer["mlp_w_up"][i] = _matmul(u2, da, ta=True, name=f"l{i}_dwup")
        du2 = _matmul(da, p["mlp_w_up"][i], tb=True, name=f"l{i}_du2")
        dh1, dnw = _rmsnorm_bwd(h1, du2, p["norm_mlp_w"][i], dh, f"l{i}_norm_mlp_bwd")
        per_layer["norm_mlp_w"][i] = dnw[0]
        if kind == 0:
            du, g = _swa_bwd(dh1, u, saved, p, j, B, S, cos, sin, f"a{j}")
            a_grads[j] = g
        elif kind == 1:
            du, g = _mamba_bwd(dh1, u, saved, p, B, S)
            grads.update(g)
        else:
            du, g = _dil_bwd(dh1, u, saved, p, B, S, cos, sin)
            grads.update(g)
        dh, dnw = _rmsnorm_bwd(h0, du, p["norm_mix_w"][i], dh1, f"l{i}_norm_mix_bwd")
        per_layer["norm_mix_w"][i] = dnw[0]
    for n in ("norm_mix_w", "norm_mlp_w"):
        grads[n] = jnp.stack(per_layer[n], axis=0)
    for n in ("mlp_w_up", "mlp_w_down"):
        grads[n] = per_layer[n]
    for n in ("a_b_qkv", "a_sinks", "a_b_o"):
        grads[n] = jnp.stack([a_grads[0][n], a_grads[1][n]], axis=0)
    for n in ("a_w_qkv", "a_w_o"):
        grads[n] = [a_grads[0][n], a_grads[1][n]]
    for n in ("b_in_w", "b_out_w", "c_w_qkv", "c_w_o"):
        grads[n] = [grads[n][0]]
    return loss, dh.reshape(B, S, D), grads


def kernel(x, positions, norm_mix_w, norm_mlp_w, a_w_qkv, a_b_qkv, a_sinks, a_w_o, a_b_o, b_in_w, b_conv_w, b_conv_b, b_dt_bias, b_a_log, b_d, b_norm_w, b_out_w, c_w_qkv, c_w_o, mlp_w_up, mlp_w_down, final_norm_w, loss_target, m_norm_mix_w, m_norm_mlp_w, m_a_w_qkv, m_a_b_qkv, m_a_sinks, m_a_w_o, m_a_b_o, m_b_in_w, m_b_conv_w, m_b_conv_b, m_b_dt_bias, m_b_a_log, m_b_d, m_b_norm_w, m_b_out_w, m_c_w_qkv, m_c_w_o, m_mlp_w_up, m_mlp_w_down, m_final_norm_w, v_norm_mix_w, v_norm_mlp_w, v_a_w_qkv, v_a_b_qkv, v_a_sinks, v_a_w_o, v_a_b_o, v_b_in_w, v_b_conv_w, v_b_conv_b, v_b_dt_bias, v_b_a_log, v_b_d, v_b_norm_w, v_b_out_w, v_c_w_qkv, v_c_w_o, v_mlp_w_up, v_mlp_w_down, v_final_norm_w):
    w = dict(zip(W_NAMES, (norm_mix_w, norm_mlp_w, a_w_qkv, a_b_qkv, a_sinks, a_w_o, a_b_o, b_in_w, b_conv_w, b_conv_b,
                           b_dt_bias, b_a_log, b_d, b_norm_w, b_out_w, c_w_qkv, c_w_o, mlp_w_up, mlp_w_down, final_norm_w)))
    m = dict(zip(W_NAMES, (m_norm_mix_w, m_norm_mlp_w, m_a_w_qkv, m_a_b_qkv, m_a_sinks, m_a_w_o, m_a_b_o, m_b_in_w,
                           m_b_conv_w, m_b_conv_b, m_b_dt_bias, m_b_a_log, m_b_d, m_b_norm_w, m_b_out_w, m_c_w_qkv, m_c_w_o,
                           m_mlp_w_up, m_mlp_w_down, m_final_norm_w)))
    v = dict(zip(W_NAMES, (v_norm_mix_w, v_norm_mlp_w, v_a_w_qkv, v_a_b_qkv, v_a_sinks, v_a_w_o, v_a_b_o, v_b_in_w,
                           v_b_conv_w, v_b_conv_b, v_b_dt_bias, v_b_a_log, v_b_d, v_b_norm_w, v_b_out_w, v_c_w_qkv, v_c_w_o,
                           v_mlp_w_up, v_mlp_w_down, v_final_norm_w)))
    px, py, pc = lax.axis_index("x"), lax.axis_index("y"), lax.axis_index("c")
    me = 4 * px + 2 * py + pc
    dev, chip, core = (t.astype(jnp.int32).reshape(1) for t in (me, 2 * px + py, pc))

    trio = tuple(SMALL_SHARDED)
    got = _gather([(d[n], "slot", None, (N_DEV,) + d[n].shape, False) for n in trio for d in (w, m, v)], "gather_small")
    slots = {n: got[3 * i:3 * i + 3] for i, n in enumerate(trio)}
    p = {n: w[n] for n in SMALL_REPLICATED}
    for n in trio:
        p[n] = _from_slots(slots[n][0], SMALL_SHARDED[n])
    for n in BIG_KIND:
        p[n] = [None] * w[n].shape[0]
    for i in range(DEPTH):
        names = _layer_big(i)
        items = []
        for n, l in names:
            kind, s2 = BIG_KIND[n], w[n].shape[1:]
            placed = _place(w[n], l, kind, _full2d(kind, s2), dev, f"place_l{i}_{n}")
            items.append((placed, kind, _block_size(kind, s2), _full2d(kind, s2), True))
        for (n, l), t in zip(names, _gather(items, f"gather_l{i}")):
            p[n][l] = _from_slots(t, 1) if BIG_KIND[n] == "slot" else t

    loss_part, dx, grads = _local_step(x, positions, p, loss_target)
    loss = lax.psum(loss_part[0, 0], AXES)

    res = {n: [[None] * w[n].shape[0] for _ in range(4)] for n in BIG_KIND}
    for i in reversed(range(DEPTH)):
        names = _layer_big(i)
        items = []
        for n, l in names:
            kind, s2 = BIG_KIND[n], w[n].shape[1:]
            g = grads[n][l]
            items.append((_to_slots(g, 1) if kind == "slot" else g, kind, _block_size(kind, s2), s2))
        sib = _reduce_d2d(items, f"reduce_d2d_l{i}")
        parts = [_pair_sum(it[0], s, it[1], core, f"pair_sum_l{i}_{n}") for (n, _), it, s in zip(names, items, sib)]
        recv = _reduce_ici(parts, f"reduce_ici_l{i}")
        for (n, l), pt, r in zip(names, parts, recv):
            for k, o in enumerate(_adamw(pt, r, w[n][l], m[n][l], v[n][l], chip, f"adamw_l{i}_{n}")):
                res[n][k][l] = o
    out = {n: [jnp.stack(res[n][k], axis=0) for k in range(4)] for n in BIG_KIND}

    small = SMALL_REPLICATED + trio
    as2d = lambda t: t.reshape(1, -1) if t.ndim == 1 else t
    g_sm = [as2d(grads[n]) for n in SMALL_REPLICATED] + [_to_slots(grads[n].reshape(p[n].shape), SMALL_SHARDED[n]) for n in trio]
    gathered = _gather([(g, "slot", None, (N_DEV,) + g.shape, False) for g in g_sm], "gather_small_grads")
    ws = [as2d(w[n]) for n in SMALL_REPLICATED] + [slots[n][0] for n in trio]
    ms = [as2d(m[n]) for n in SMALL_REPLICATED] + [slots[n][1] for n in trio]
    vs = [as2d(v[n]) for n in SMALL_REPLICATED] + [slots[n][2] for n in trio]
    sm_out = _small_adamw(gathered, ws, ms, vs)
    for i, n in enumerate(small):
        if n in SMALL_SHARDED:
            out[n] = [lax.dynamic_index_in_dim(sm_out[k][i], me, 0, keepdims=False) for k in range(4)]
        else:
            out[n] = [sm_out[k][i].reshape(w[n].shape) for k in range(4)]
    return (loss, dx, *[out[n][0] for n in W_NAMES], *[out[n][1] for n in W_NAMES], *[out[n][2] for n in W_NAMES],
            *[out[n][3] for n in W_NAMES])
```

```python
import functools
import math

import jax
import jax.numpy as jnp
import numpy as np
from jax import lax
from jax.experimental import pallas as pl
from jax.experimental.pallas import tpu as pltpu

F32 = jnp.float32
BF16 = jnp.bfloat16
SDS = jax.ShapeDtypeStruct

D_MODEL = 1024
DEPTH = 4
BLOCK = 128
ROPE_THETA = 10000.0
NORM_EPS = 1e-5
HEAD_DIM = 64
A_N_HEADS = 16
A_N_KV = 2
A_WINDOW = 128
A_Q_DIM = 1024
A_KV_DIM = 128
SSM_D_INNER = 2048
SSM_N_HEADS = 32
SSM_N_GROUPS = 8
SSM_HG = 4
SSM_D_STATE = 128
SSM_CONV = 4
SSM_CHUNK = 128
SSM_BC_DIM = 1024
SSM_CONV_DIM = 4096
C_PATTERNS = ((128, 1), (512, 4), (2048, 16))
C_HEADS = 16
ADAM_LR, ADAM_B1, ADAM_B2, ADAM_EPS, ADAM_WD, ADAM_STEP = 0.001, 0.9, 0.999, 1e-08, 0.01, 10

N_DEV = 8
AXES = ("x", "y", "c")
LANES = 128
VMEM_LIMIT = 56 * 1024 * 1024
STREAM_VMEM = 16 * 1024 * 1024
NEG = -1e30

NN = (((1,), (0,)), ((), ()))
NT = (((1,), (1,)), ((), ()))
TN = (((0,), (0,)), ((), ()))
HI = lax.Precision.HIGHEST


def _pick(n, cap, mult=LANES):
    best = None
    for t in range(mult, min(n, cap) + 1, mult):
        if n % t == 0:
            best = t
    return best if best is not None else n


def _params(sem):
    return pltpu.CompilerParams(dimension_semantics=sem, vmem_limit_bytes=VMEM_LIMIT)


def _bf(x):
    return x if x.dtype == BF16 else x.astype(BF16)


def _rot_half(y):
    n = y.shape[-1]
    lane = lax.broadcasted_iota(jnp.int32, y.shape, y.ndim - 1)
    return jnp.where((lane % HEAD_DIM) < HEAD_DIM // 2, -pltpu.roll(y, n - 32, y.ndim - 1), pltpu.roll(y, 32, y.ndim - 1))


def _rope(y, cos, sin, sign):
    reps = y.shape[-1] // LANES
    c = jnp.tile(cos, (1, reps)) if reps > 1 else cos
    s = jnp.tile(sin, (1, reps)) if reps > 1 else sin
    return y * c + sign * (_rot_half(y) * s)


def _matmul(a, b, *, ta=False, tb=False, out_dtype=F32, bias=None, resid=None, mul=None, mul_scale=1.0,
            relu2=False, rope=None, rope_cols=0, tm=512, tn=1024, tk=1024, name="mm"):
    M = a.shape[1] if ta else a.shape[0]
    K = a.shape[0] if ta else a.shape[1]
    N = b.shape[0] if tb else b.shape[1]
    assert (b.shape[1] if tb else b.shape[0]) == K
    tm, tn, tk = _pick(M, tm), _pick(N, tn), _pick(K, tk)
    nk = K // tk
    dims = (((0 if ta else 1,), (1 if tb else 0,)), ((), ()))

    def body(*refs):
        it = iter(refs)
        a_ref, b_ref = next(it), next(it)
        bias_ref = next(it) if bias is not None else None
        resid_ref = next(it) if resid is not None else None
        mul_ref = next(it) if mul is not None else None
        cos_ref, sin_ref = (next(it), next(it)) if rope is not None else (None, None)
        o_ref = next(it)
        o2_ref = next(it) if relu2 else None
        acc_ref = next(it)
        k = pl.program_id(2)
        part = lax.dot_general(_bf(a_ref[...]), _bf(b_ref[...]), dims, preferred_element_type=F32)

        @pl.when(k == 0)
        def _():
            acc_ref[...] = part

        @pl.when(k > 0)
        def _():
            acc_ref[...] += part

        @pl.when(k == nk - 1)
        def _():
            y = acc_ref[...]
            if bias_ref is not None:
                y = y + bias_ref[...]
            if rope is not None:
                col = pl.program_id(1) * tn + lax.broadcasted_iota(jnp.int32, y.shape, 1)
                y = jnp.where(col < rope_cols, _rope(y, cos_ref[...], sin_ref[...], 1.0), y)
            if mul_ref is not None:
                y = y * (mul_ref[...].astype(F32) * mul_scale)
            if resid_ref is not None:
                y = y + resid_ref[...]
            if relu2:
                r = jnp.maximum(y, 0.0)
                o_ref[...] = r.astype(o_ref.dtype)
                o2_ref[...] = (r * r).astype(o2_ref.dtype)
            else:
                o_ref[...] = y.astype(o_ref.dtype)

    a_spec = pl.BlockSpec((tk, tm), lambda i, j, k: (k, i)) if ta else pl.BlockSpec((tm, tk), lambda i, j, k: (i, k))
    b_spec = pl.BlockSpec((tn, tk), lambda i, j, k: (j, k)) if tb else pl.BlockSpec((tk, tn), lambda i, j, k: (k, j))
    mn_spec = pl.BlockSpec((tm, tn), lambda i, j, k: (i, j))
    in_specs, args = [a_spec, b_spec], [a, b]
    if bias is not None:
        in_specs.append(pl.BlockSpec((1, tn), lambda i, j, k: (0, j)))
        args.append(bias)
    if resid is not None:
        in_specs.append(mn_spec)
        args.append(resid)
    if mul is not None:
        in_specs.append(mn_spec)
        args.append(mul)
    if rope is not None:
        in_specs += [pl.BlockSpec((tm, LANES), lambda i, j, k: (i, 0))] * 2
        args += [rope[0], rope[1]]
    out_shape = SDS((M, N), out_dtype)
    out_specs = mn_spec
    if relu2:
        out_shape, out_specs = (out_shape, out_shape), (mn_spec, mn_spec)
    return pl.pallas_call(
        body, out_shape=out_shape, grid=(M // tm, N // tn, nk), in_specs=in_specs, out_specs=out_specs,
        scratch_shapes=[pltpu.VMEM((tm, tn), F32)], name=name,
        compiler_params=_params(("parallel", "parallel", "arbitrary")),
    )(*args)


def _colsum(x, name):
    T, N = x.shape
    tm = _pick(T, 1024, 8)

    def body(x_ref, o_ref):
        s = jnp.sum(x_ref[...].astype(F32), axis=0, keepdims=True)

        @pl.when(pl.program_id(0) == 0)
        def _():
            o_ref[...] = s

        @pl.when(pl.program_id(0) > 0)
        def _():
            o_ref[...] += s

    return pl.pallas_call(
        body, out_shape=SDS((1, N), F32), grid=(T // tm,),
        in_specs=[pl.BlockSpec((tm, N), lambda i: (i, 0))], out_specs=pl.BlockSpec((1, N), lambda i: (0, 0)),
        name=name, compiler_params=_params(("arbitrary",)),
    )(x)


def _rmsnorm_fwd(h, w, name):
    T, D = h.shape
    tm = _pick(T, 512, 8)

    def body(h_ref, w_ref, o_ref):
        x = h_ref[...]
        rstd = lax.rsqrt(jnp.mean(x * x, axis=-1, keepdims=True) + NORM_EPS)
        o_ref[...] = (x * rstd * w_ref[...]).astype(BF16)

    return pl.pallas_call(
        body, out_shape=SDS((T, D), BF16), grid=(T // tm,),
        in_specs=[pl.BlockSpec((tm, D), lambda i: (i, 0)), pl.BlockSpec((1, D), lambda i: (0, 0))],
        out_specs=pl.BlockSpec((tm, D), lambda i: (i, 0)), name=name, compiler_params=_params(("parallel",)),
    )(h, w.reshape(1, D))


def _rmsnorm_bwd(h, du, w, dres, name):
    T, D = h.shape
    tm = _pick(T, 512, 8)

    def body(h_ref, du_ref, w_ref, dres_ref, dh_ref, dw_ref):
        x = h_ref[...]
        du_ = du_ref[...].astype(F32)
        rstd = lax.rsqrt(jnp.mean(x * x, axis=-1, keepdims=True) + NORM_EPS)
        g = du_ * w_ref[...]
        dh_ref[...] = dres_ref[...] + rstd * g - x * (rstd * rstd * rstd) * jnp.mean(g * x, axis=-1, keepdims=True)
        dw = jnp.sum(du_ * x * rstd, axis=0, keepdims=True)

        @pl.when(pl.program_id(0) == 0)
        def _():
            dw_ref[...] = dw

        @pl.when(pl.program_id(0) > 0)
        def _():
            dw_ref[...] += dw

    row = pl.BlockSpec((tm, D), lambda i: (i, 0))
    vec = pl.BlockSpec((1, D), lambda i: (0, 0))
    return pl.pallas_call(
        body, out_shape=(SDS((T, D), F32), SDS((1, D), F32)), grid=(T // tm,),
        in_specs=[row, row, vec, row], out_specs=(row, vec), name=name, compiler_params=_params(("arbitrary",)),
    )(h, du, w.reshape(1, D), dres)


def _final_loss(h, target, w):
    T, D = h.shape
    tm = _pick(T, 512, 8)

    def body(h_ref, t_ref, w_ref, dh_ref, dw_ref, loss_ref):
        x = h_ref[...]
        rstd = lax.rsqrt(jnp.mean(x * x, axis=-1, keepdims=True) + NORM_EPS)
        xn = x * rstd
        err = xn * w_ref[...] - t_ref[...]
        part = 0.5 * jnp.sum(jnp.mean(err * err, axis=-1, keepdims=True), axis=0, keepdims=True)
        dy = err * (1.0 / D)
        g = dy * w_ref[...]
        dh_ref[...] = rstd * g - x * (rstd * rstd * rstd) * jnp.mean(g * x, axis=-1, keepdims=True)
        dw = jnp.sum(dy * xn, axis=0, keepdims=True)
        lp = jnp.broadcast_to(part, (1, LANES))

        @pl.when(pl.program_id(0) == 0)
        def _():
            dw_ref[...] = dw
            loss_ref[...] = lp

        @pl.when(pl.program_id(0) > 0)
        def _():
            dw_ref[...] += dw
            loss_ref[...] += lp

    row = pl.BlockSpec((tm, D), lambda i: (i, 0))
    vec = pl.BlockSpec((1, D), lambda i: (0, 0))
    return pl.pallas_call(
        body, out_shape=(SDS((T, D), F32), SDS((1, D), F32), SDS((1, LANES), F32)), grid=(T // tm,),
        in_specs=[row, row, vec], out_specs=(row, vec, pl.BlockSpec((1, LANES), lambda i: (0, 0))),
        name="final_loss", compiler_params=_params(("arbitrary",)),
    )(h, target, w.reshape(1, D))


def _band_mask(i_blk, max_dist, first_ok):
    qi = lax.broadcasted_iota(jnp.int32, (BLOCK, 2 * BLOCK), 0)
    kj = lax.broadcasted_iota(jnp.int32, (BLOCK, 2 * BLOCK), 1)
    dist = qi + BLOCK - kj
    ok = (dist >= 0) & (dist <= max_dist)
    return ok & ((kj >= BLOCK) | first_ok)


def _pair(t, i):
    return t[:, LANES * i:LANES * (i + 1)]


def _low_half(shape):
    return lax.broadcasted_iota(jnp.int32, shape, len(shape) - 1) < HEAD_DIM


def _stack_heads(t):
    lo = _low_half(t.shape)
    z = jnp.zeros_like(t)
    return jnp.concatenate([jnp.where(lo, t, z), jnp.where(lo, z, t)], axis=0)


def _swap_halves(t):
    return jnp.concatenate([t[:, HEAD_DIM:], t[:, :HEAD_DIM]], axis=1)


def _kv_operand(kv, kv_swapped, h0, n_kv, n_heads):
    R = n_heads // n_kv
    if R == 1:
        return _pair(kv, h0 // 2)
    assert kv.shape[1] == LANES and R % 2 == 0, "grouped queries: one 128-lane tile of kv heads, both heads of a pair in one group"
    g = h0 // R
    t, ts = _pair(kv, g // 2), _pair(kv_swapped, g // 2)
    lo = _low_half(t.shape)
    return jnp.where(lo, t, ts) if g % 2 == 0 else jnp.where(lo, ts, t)


def _lane_place(cols):
    m = cols[0].shape[0]
    lane = lax.broadcasted_iota(jnp.int32, (m, LANES), 1)
    out = jnp.zeros((m, LANES), F32)
    for h, c in enumerate(cols):
        out = jnp.where(lane == h, c, out)
    return out


def _attn_specs(B, S, d, C, n_heads, n_kv, q_col, k_col, v_col):
    kvw = n_kv * HEAD_DIM
    qw = n_heads * HEAD_DIM
    cq, ck = (C // qw if d > 1 else 0), (C // kvw if d > 1 else 0)
    q_spec = pl.BlockSpec((1, BLOCK, qw), lambda b, r, i: (b, i, r * cq + q_col // qw))
    kc = pl.BlockSpec((1, BLOCK, kvw), lambda b, r, i: (b, i, r * ck + k_col // kvw))
    kp = pl.BlockSpec((1, BLOCK, kvw), lambda b, r, i: (b, jnp.maximum(i - 1, 0), r * ck + k_col // kvw))
    vc = pl.BlockSpec((1, BLOCK, kvw), lambda b, r, i: (b, i, r * ck + v_col // kvw))
    vp = pl.BlockSpec((1, BLOCK, kvw), lambda b, r, i: (b, jnp.maximum(i - 1, 0), r * ck + v_col // kvw))
    return q_spec, kp, kc, vp, vc


def _attn_fwd(qkv, B, S, d, *, n_heads, n_kv, q_col, k_col, v_col, max_dist, sinks, name):
    C = qkv.shape[1]
    Ls = S // d
    nb = Ls // BLOCK
    qw = n_heads * HEAD_DIM
    R = n_heads // n_kv
    qkv3 = qkv.reshape(B, Ls, d * C)
    scale = HEAD_DIM ** -0.5

    def body(*refs):
        if sinks is not None:
            sink_ref, q_ref, kp_ref, kc_ref, vp_ref, vc_ref, o_ref, lse_ref = refs
        else:
            q_ref, kp_ref, kc_ref, vp_ref, vc_ref, o_ref, lse_ref = refs
        i = pl.program_id(2)
        mask1 = _band_mask(i, max_dist, i > 0)
        mask = jnp.concatenate([mask1, mask1], axis=0)
        q = q_ref[0]
        kk = jnp.concatenate([kp_ref[0], kc_ref[0]], axis=0)
        vv = jnp.concatenate([vp_ref[0], vc_ref[0]], axis=0)
        kks, vvs = (_swap_halves(kk), _swap_halves(vv)) if R > 1 else (None, None)
        lo = _low_half((BLOCK, LANES))
        top = lax.broadcasted_iota(jnp.int32, (2 * BLOCK, 1), 0) < BLOCK
        lses, tiles = [], []
        for t in range(n_heads // 2):
            k2 = _kv_operand(kk, kks, 2 * t, n_kv, n_heads)
            v2 = _kv_operand(vv, vvs, 2 * t, n_kv, n_heads)
            s = lax.dot_general(_stack_heads(_pair(q, t)), k2, NT, preferred_element_type=F32) * scale
            s = jnp.where(mask, s, NEG)
            m = jnp.max(s, axis=-1, keepdims=True)
            if sinks is not None:
                sk = jnp.where(top, sink_ref[2 * t], sink_ref[2 * t + 1])
                m = jnp.maximum(m, sk)
            p = jnp.exp(s - m)
            den = jnp.sum(p, axis=-1, keepdims=True)
            if sinks is not None:
                den = den + jnp.exp(sk - m)
            lse2 = m + jnp.log(den)
            o2 = jnp.dot((p / den).astype(BF16), v2, preferred_element_type=F32)
            tiles.append(jnp.where(lo, o2[:BLOCK], o2[BLOCK:]))
            lses += [lse2[:BLOCK], lse2[BLOCK:]]
        o_ref[0] = jnp.concatenate(tiles, axis=-1)
        lse_ref[0] = _lane_place(lses)

    specs = list(_attn_specs(B, S, d, C, n_heads, n_kv, q_col, k_col, v_col))
    args = [qkv3] * 5
    if sinks is not None:
        specs = [pl.BlockSpec(memory_space=pltpu.SMEM)] + specs
        args = [sinks] + args
    o3, lse3 = pl.pallas_call(
        body, out_shape=(SDS((B, Ls, d * qw), F32), SDS((B, Ls, d * LANES), F32)), grid=(B, d, nb), in_specs=specs,
        out_specs=(pl.BlockSpec((1, BLOCK, qw), lambda b, r, i: (b, i, r)), pl.BlockSpec((1, BLOCK, LANES), lambda b, r, i: (b, i, r))),
        name=name, compiler_params=_params(("parallel", "parallel", "parallel")),
    )(*args)
    return o3.reshape(B * S, qw), lse3.reshape(B * S, LANES)


def _attn_dq(qkv, do, lse, delta, cos, sin, B, S, d, *, n_heads, n_kv, q_col, k_col, v_col, max_dist, name):
    C = qkv.shape[1]
    Ls = S // d
    nb = Ls // BLOCK
    qw = n_heads * HEAD_DIM
    R = n_heads // n_kv
    scale = HEAD_DIM ** -0.5

    def body(q_ref, kp_ref, kc_ref, vp_ref, vc_ref, do_ref, lse_ref, dl_ref, cos_ref, sin_ref, dq_ref):
        i = pl.program_id(2)
        mask1 = _band_mask(i, max_dist, i > 0)
        mask = jnp.concatenate([mask1, mask1], axis=0)
        q = q_ref[0]
        do_ = do_ref[0]
        kk = jnp.concatenate([kp_ref[0], kc_ref[0]], axis=0)
        vv = jnp.concatenate([vp_ref[0], vc_ref[0]], axis=0)
        kks, vvs = (_swap_halves(kk), _swap_halves(vv)) if R > 1 else (None, None)
        lo = _low_half((BLOCK, LANES))
        lse_t, dl_t = lse_ref[0], dl_ref[0]
        tiles = []
        for t in range(n_heads // 2):
            k2 = _kv_operand(kk, kks, 2 * t, n_kv, n_heads)
            v2 = _kv_operand(vv, vvs, 2 * t, n_kv, n_heads)
            lse2 = jnp.concatenate([lse_t[:, 2 * t:2 * t + 1], lse_t[:, 2 * t + 1:2 * t + 2]], axis=0)
            dl2 = jnp.concatenate([dl_t[:, 2 * t:2 * t + 1], dl_t[:, 2 * t + 1:2 * t + 2]], axis=0)
            s = lax.dot_general(_stack_heads(_pair(q, t)), k2, NT, preferred_element_type=F32) * scale
            p = jnp.where(mask, jnp.exp(s - lse2), 0.0)
            dp = lax.dot_general(_stack_heads(_pair(do_, t)), v2, NT, preferred_element_type=F32)
            ds = p * (dp - dl2)
            dq2 = jnp.dot(ds.astype(BF16), k2, preferred_element_type=F32) * scale
            tiles.append(jnp.where(lo, dq2[:BLOCK], dq2[BLOCK:]))
        dq = jnp.concatenate(tiles, axis=-1)
        dq_ref[0] = _rope(dq, cos_ref[0], sin_ref[0], -1.0).astype(BF16)

    qs, kp, kc, vp, vc = _attn_specs(B, S, d, C, n_heads, n_kv, q_col, k_col, v_col)
    row_q = pl.BlockSpec((1, BLOCK, qw), lambda b, r, i: (b, i, r))
    row_l = pl.BlockSpec((1, BLOCK, LANES), lambda b, r, i: (b, i, r))
    qkv3 = qkv.reshape(B, Ls, d * C)
    v3 = lambda t, w: t.reshape(B, Ls, d * w)
    dq3 = pl.pallas_call(
        body, out_shape=SDS((B, Ls, d * qw), BF16), grid=(B, d, nb),
        in_specs=[qs, kp, kc, vp, vc, row_q, row_l, row_l, row_l, row_l], out_specs=row_q,
        name=name, compiler_params=_params(("parallel", "parallel", "parallel")),
    )(qkv3, qkv3, qkv3, qkv3, qkv3, v3(do, qw), v3(lse, LANES), v3(delta, LANES), v3(cos, LANES), v3(sin, LANES))
    return dq3.reshape(B * S, qw)


def _attn_dkv(qkv, do, lse, delta, cos, sin, B, S, d, *, n_heads, n_kv, q_col, k_col, v_col, max_dist, name):
    C = qkv.shape[1]
    Ls = S // d
    nb = Ls // BLOCK
    qw = n_heads * HEAD_DIM
    kvw = n_kv * HEAD_DIM
    R = n_heads // n_kv
    scale = HEAD_DIM ** -0.5
    cq, ck = (C // qw if d > 1 else 0), (C // kvw if d > 1 else 0)

    def body(k_ref, v_ref, q0_ref, q1_ref, do0_ref, do1_ref, lse0_ref, lse1_ref, dl0_ref, dl1_ref, cos_ref, sin_ref,
             dk_ref, dv_ref):
        j = pl.program_id(2)
        kj = lax.broadcasted_iota(jnp.int32, (BLOCK, BLOCK), 0)
        qi = lax.broadcasted_iota(jnp.int32, (BLOCK, BLOCK), 1)
        dist0 = qi - kj
        dist1 = qi + BLOCK - kj
        mask0 = (dist0 >= 0) & (dist0 <= max_dist)
        mask1 = (dist1 <= max_dist) & (j + 1 < nb)
        kb, vb = k_ref[0], v_ref[0]
        kbs, vbs = (_swap_halves(kb), _swap_halves(vb)) if R > 1 else (None, None)
        sides = ((q0_ref[0], do0_ref[0], lse0_ref[0].T, dl0_ref[0].T, mask0), (q1_ref[0], do1_ref[0], lse1_ref[0].T, dl1_ref[0].T, mask1))
        n_acc = n_kv if R > 1 else n_kv // 2
        dks = [jnp.zeros((BLOCK, LANES), F32) for _ in range(n_acc)]
        dvs = [jnp.zeros((BLOCK, LANES), F32) for _ in range(n_acc)]
        for t in range(n_heads // 2):
            k2 = _kv_operand(kb, kbs, 2 * t, n_kv, n_heads)
            v2 = _kv_operand(vb, vbs, 2 * t, n_kv, n_heads)
            a = (2 * t) // R if R > 1 else t
            for (q, do_, lse_r, dl_r, mask) in sides:
                q2, do2 = _stack_heads(_pair(q, t)), _stack_heads(_pair(do_, t))
                s = lax.dot_general(k2, q2, NT, preferred_element_type=F32) * scale
                dp = lax.dot_general(v2, do2, NT, preferred_element_type=F32)
                ps, dss = [], []
                for half in (0, 1):
                    h = 2 * t + half
                    sl = slice(BLOCK * half, BLOCK * (half + 1))
                    p = jnp.where(mask, jnp.exp(s[:, sl] - lse_r[h:h + 1, :]), 0.0)
                    ps.append(p)
                    dss.append(p * (dp[:, sl] - dl_r[h:h + 1, :]))
                dvs[a] = dvs[a] + jnp.dot(jnp.concatenate(ps, axis=1).astype(BF16), do2, preferred_element_type=F32)
                dks[a] = dks[a] + jnp.dot(jnp.concatenate(dss, axis=1).astype(BF16), q2, preferred_element_type=F32)
        if R > 1:
            lo = _low_half((BLOCK, LANES))
            fold = lambda x: x + pltpu.roll(x, HEAD_DIM, 1)
            dks = [jnp.where(lo, fold(dks[2 * t]), fold(dks[2 * t + 1])) for t in range(n_kv // 2)]
            dvs = [jnp.where(lo, fold(dvs[2 * t]), fold(dvs[2 * t + 1])) for t in range(n_kv // 2)]
        dk_t = jnp.concatenate(dks, axis=-1) * scale
        dk_ref[0] = _rope(dk_t, cos_ref[0], sin_ref[0], -1.0).astype(BF16)
        dv_ref[0] = jnp.concatenate(dvs, axis=-1).astype(BF16)

    nxt = lambda j: jnp.minimum(j + 1, nb - 1)
    k_spec = pl.BlockSpec((1, BLOCK, kvw), lambda b, r, j: (b, j, r * ck + k_col // kvw))
    v_spec = pl.BlockSpec((1, BLOCK, kvw), lambda b, r, j: (b, j, r * ck + v_col // kvw))
    q0 = pl.BlockSpec((1, BLOCK, qw), lambda b, r, j: (b, j, r * cq + q_col // qw))
    q1 = pl.BlockSpec((1, BLOCK, qw), lambda b, r, j: (b, nxt(j), r * cq + q_col // qw))
    w0 = lambda w: pl.BlockSpec((1, BLOCK, w), lambda b, r, j: (b, j, r))
    w1 = lambda w: pl.BlockSpec((1, BLOCK, w), lambda b, r, j: (b, nxt(j), r))
    qkv3 = qkv.reshape(B, Ls, d * C)
    v3 = lambda t, w: t.reshape(B, Ls, d * w)
    do3, lse3, dl3 = v3(do, qw), v3(lse, LANES), v3(delta, LANES)
    dk3, dv3 = pl.pallas_call(
        body, out_shape=(SDS((B, Ls, d * kvw), BF16), SDS((B, Ls, d * kvw), BF16)), grid=(B, d, nb),
        in_specs=[k_spec, v_spec, q0, q1, w0(qw), w1(qw), w0(LANES), w1(LANES), w0(LANES), w1(LANES), w0(LANES), w0(LANES)],
        out_specs=(w0(kvw), w0(kvw)), name=name, compiler_params=_params(("parallel", "parallel", "parallel")),
    )(qkv3, qkv3, qkv3, qkv3, do3, do3, lse3, lse3, dl3, dl3, v3(cos, LANES), v3(sin, LANES))
    return dk3.reshape(B * S, kvw), dv3.reshape(B * S, kvw)


def _head_expand():
    r = lax.broadcasted_iota(jnp.int32, (LANES, C_HEADS * HEAD_DIM), 0)
    c = lax.broadcasted_iota(jnp.int32, (LANES, C_HEADS * HEAD_DIM), 1)
    return jnp.where(c // HEAD_DIM == r, 1.0, 0.0).astype(F32)


def _delta(do, o, lse=None, sinks_row=None, name="delta"):
    T, W = do.shape
    tm = _pick(T, 512, 8)
    with_sink = sinks_row is not None

    def body(*refs):
        if with_sink:
            do_ref, o_ref, lse_ref, sk_ref, dl_ref, dob_ref, ds_ref = refs
        else:
            do_ref, o_ref, dl_ref, dob_ref = refs
        do_ = do_ref[...]
        dl = lax.dot_general(do_ * o_ref[...], _head_expand(), NT, preferred_element_type=F32, precision=HI)
        dl_ref[...] = dl
        dob_ref[...] = do_.astype(BF16)
        if with_sink:
            lane = lax.broadcasted_iota(jnp.int32, dl.shape, 1)
            contrib = jnp.where(lane < A_N_HEADS, -jnp.exp(sk_ref[...] - lse_ref[...]) * dl, 0.0)
            part = jnp.sum(contrib, axis=0, keepdims=True)

            @pl.when(pl.program_id(0) == 0)
            def _():
                ds_ref[...] = part

            @pl.when(pl.program_id(0) > 0)
            def _():
                ds_ref[...] += part

    row_w = pl.BlockSpec((tm, W), lambda i: (i, 0))
    row_l = pl.BlockSpec((tm, LANES), lambda i: (i, 0))
    vec_l = pl.BlockSpec((1, LANES), lambda i: (0, 0))
    if with_sink:
        return pl.pallas_call(
            body, out_shape=(SDS((T, LANES), F32), SDS((T, W), BF16), SDS((1, LANES), F32)), grid=(T // tm,),
            in_specs=[row_w, row_w, row_l, vec_l], out_specs=(row_l, row_w, vec_l), name=name,
            compiler_params=_params(("arbitrary",)),
        )(do, o, lse, sinks_row)
    return pl.pallas_call(
        body, out_shape=(SDS((T, LANES), F32), SDS((T, W), BF16)), grid=(T // tm,),
        in_specs=[row_w, row_w], out_specs=(row_l, row_w), name=name, compiler_params=_params(("parallel",)),
    )(do, o)


def _merge(os_, lses):
    T, W = os_[0].shape
    tm = _pick(T, 512, 8)

    def body(o0, o1, o2, l0, l1, l2, o_ref, lse_ref):
        ls = [l0[...], l1[...], l2[...]]
        m = jnp.maximum(jnp.maximum(ls[0], ls[1]), ls[2])
        ws = [jnp.exp(l - m) for l in ls]
        tot = ws[0] + ws[1] + ws[2]
        lse_ref[...] = m + jnp.log(tot)
        e = _head_expand()
        acc = jnp.zeros((tm, W), F32)
        for w, o in zip(ws, (o0, o1, o2)):
            acc = acc + jnp.dot(w / tot, e, preferred_element_type=F32, precision=HI) * o[...]
        o_ref[...] = acc

    row_w = pl.BlockSpec((tm, W), lambda i: (i, 0))
    row_l = pl.BlockSpec((tm, LANES), lambda i: (i, 0))
    return pl.pallas_call(
        body, out_shape=(SDS((T, W), F32), SDS((T, LANES), F32)), grid=(T // tm,),
        in_specs=[row_w] * 3 + [row_l] * 3, out_specs=(row_w, row_l), name="c_merge", compiler_params=_params(("parallel",)),
    )(*os_, *lses)


CONV_TC = 256


def _conv_pre(x, w, bias):
    row = lax.broadcasted_iota(jnp.int32, x.shape, 0)
    acc = x * w[SSM_CONV - 1:SSM_CONV, :] + bias
    for k in range(1, SSM_CONV):
        acc = acc + jnp.where(row >= k, pltpu.roll(x, k, 0), 0.0) * w[SSM_CONV - 1 - k:SSM_CONV - k, :]
    return acc


def _conv_fwd(zx3, w, bias):
    B, S, _ = zx3.shape
    off = SSM_D_INNER // CONV_TC

    def body(x_ref, w_ref, b_ref, o_ref):
        v = _conv_pre(x_ref[0], w_ref[...], b_ref[...])
        o_ref[0] = v * jax.nn.sigmoid(v)

    return pl.pallas_call(
        body, out_shape=SDS((B, S, SSM_CONV_DIM), F32), grid=(B, SSM_CONV_DIM // CONV_TC),
        in_specs=[pl.BlockSpec((1, S, CONV_TC), lambda b, j: (b, 0, j + off)),
                  pl.BlockSpec((SSM_CONV, CONV_TC), lambda b, j: (0, j)), pl.BlockSpec((1, CONV_TC), lambda b, j: (0, j))],
        out_specs=pl.BlockSpec((1, S, CONV_TC), lambda b, j: (b, 0, j)), name="b_conv_fwd",
        compiler_params=_params(("parallel", "parallel")),
    )(zx3, w, bias)


def _conv_bwd(zx3, dxc, w, bias, col0, name):
    B, S, n = dxc.shape
    tc = _pick(n, CONV_TC)
    off_x = (SSM_D_INNER + col0) // tc
    off_w = col0 // tc

    def body(x_ref, d_ref, w_ref, b_ref, dx_ref, dw_ref, db_ref):
        x = x_ref[0]
        wv = w_ref[...]
        v = _conv_pre(x, wv, b_ref[...])
        sg = jax.nn.sigmoid(v)
        dc = d_ref[0] * (sg * (1.0 + v * (1.0 - sg)))
        row = lax.broadcasted_iota(jnp.int32, x.shape, 0)
        dx = dc * wv[SSM_CONV - 1:SSM_CONV, :]
        dws = [jnp.sum(dc * x, axis=0, keepdims=True)]
        for k in range(1, SSM_CONV):
            dx = dx + jnp.where(row < S - k, pltpu.roll(dc, S - k, 0), 0.0) * wv[SSM_CONV - 1 - k:SSM_CONV - k, :]
            dws.append(jnp.sum(dc * jnp.where(row >= k, pltpu.roll(x, k, 0), 0.0), axis=0, keepdims=True))
        dx_ref[0] = dx.astype(BF16)
        ridx = lax.broadcasted_iota(jnp.int32, (SSM_CONV, tc), 0)
        dw = jnp.zeros((SSM_CONV, tc), F32)
        for k in range(SSM_CONV):
            dw = jnp.where(ridx == SSM_CONV - 1 - k, dws[k], dw)
        db = jnp.sum(dc, axis=0, keepdims=True)

        @pl.when(pl.program_id(1) == 0)
        def _():
            dw_ref[...] = dw
            db_ref[...] = db

        @pl.when(pl.program_id(1) > 0)
        def _():
            dw_ref[...] += dw
            db_ref[...] += db

    return pl.pallas_call(
        body, out_shape=(SDS((B, S, n), BF16), SDS((SSM_CONV, n), F32), SDS((1, n), F32)), grid=(n // tc, B),
        in_specs=[pl.BlockSpec((1, S, tc), lambda j, b: (b, 0, j + off_x)), pl.BlockSpec((1, S, tc), lambda j, b: (b, 0, j)),
                  pl.BlockSpec((SSM_CONV, tc), lambda j, b: (0, j + off_w)), pl.BlockSpec((1, tc), lambda j, b: (0, j + off_w))],
        out_specs=(pl.BlockSpec((1, S, tc), lambda j, b: (b, 0, j)), pl.BlockSpec((SSM_CONV, tc), lambda j, b: (0, j)),
                   pl.BlockSpec((1, tc), lambda j, b: (0, j))),
        name=name, compiler_params=_params(("parallel", "arbitrary")),
    )(zx3, dxc, w, bias)


def _ssd_common(x, Bm, Cm, dtc_raw, dtr_raw, pr, pc):
    Q = SSM_CHUNK
    zc = dtc_raw + pr[0:1, :]
    dt_c = jax.nn.softplus(zc)
    dt_r = jax.nn.softplus(dtr_raw + pc[:, 0:1])
    A_r = -jnp.exp(pr[1:2, :])
    A_c = -jnp.exp(pc[:, 1:2])
    row = lax.broadcasted_iota(jnp.int32, (Q, Q), 0)
    col = lax.broadcasted_iota(jnp.int32, (Q, Q), 1)
    tril = jnp.where(row >= col, 1.0, 0.0).astype(F32)
    cs_c = jnp.dot(tril, dt_c * A_r, preferred_element_type=F32, precision=HI)
    cs_r = lax.dot_general(dt_r * A_c, tril, NT, preferred_element_type=F32, precision=HI)
    return zc, dt_c, A_r, cs_c, cs_r, row, col, tril


def _ssd_fwd(xc3, dtc, dtr, prow, pcol):
    B, S, _ = xc3.shape
    Q, G, HG, P, N = SSM_CHUNK, SSM_N_GROUPS, SSM_HG, HEAD_DIM, SSM_D_STATE
    nc = S // Q
    xw = HG * P

    def body(x_ref, b_ref, c_ref, dtc_ref, dtr_ref, pr_ref, pc_ref, y_ref, st_ref, state):
        c = pl.program_id(2)

        @pl.when(c == 0)
        def _():
            state[...] = jnp.zeros_like(state)

        x, Bm, Cm = x_ref[0], b_ref[0], c_ref[0]
        pr = pr_ref[0]
        _, dt_c, _, cs_c, cs_r, row, col, _ = _ssd_common(x, Bm, Cm, dtc_ref[0, 0], dtr_ref[0, 0], pr, pc_ref[0])
        Bb, Cb = Bm.astype(BF16), Cm.astype(BF16)
        CB = lax.dot_general(Cb, Bb, NT, preferred_element_type=F32)
        ys = []
        for hg in range(HG):
            xh = x[:, P * hg:P * (hg + 1)]
            xt = xh * dt_c[:, hg:hg + 1]
            csc, csr = cs_c[:, hg:hg + 1], cs_r[hg:hg + 1, :]
            L = jnp.where(row >= col, jnp.exp(jnp.minimum(csc - csr, 0.0)), 0.0)
            ydiag = jnp.dot((CB * L).astype(BF16), xt.astype(BF16), preferred_element_type=F32)
            Sh = state[hg]
            yoff = lax.dot_general(Cb, Sh.astype(BF16), NT, preferred_element_type=F32) * jnp.exp(csc)
            ys.append(ydiag + yoff + pr[2:3, hg:hg + 1] * xh)
            st_ref[0, 0, 0, P * hg:P * (hg + 1), :] = Sh
            csq = csc[Q - 1:Q, :]
            upd = lax.dot_general((xt * jnp.exp(csq - csc)).astype(BF16), Bb, TN, preferred_element_type=F32)
            state[hg] = Sh * jnp.exp(csq) + upd
        y_ref[0] = jnp.concatenate([jnp.concatenate(ys[0:2], axis=-1), jnp.concatenate(ys[2:4], axis=-1)], axis=-1)

    xo, bo, co = 0, SSM_D_INNER // N, (SSM_D_INNER + SSM_BC_DIM) // N
    return pl.pallas_call(
        body, out_shape=(SDS((B, S, SSM_D_INNER), F32), SDS((B, G, nc, xw, N), F32)), grid=(G, B, nc),
        in_specs=[pl.BlockSpec((1, Q, xw), lambda g, b, c: (b, c, g)), pl.BlockSpec((1, Q, N), lambda g, b, c: (b, c, bo + g)),
                  pl.BlockSpec((1, Q, N), lambda g, b, c: (b, c, co + g)), pl.BlockSpec((1, 1, Q, HG), lambda g, b, c: (b, g, c, 0)),
                  pl.BlockSpec((1, 1, HG, Q), lambda g, b, c: (b, g, 0, c)), pl.BlockSpec((1, 3, HG), lambda g, b, c: (g, 0, 0)),
                  pl.BlockSpec((1, HG, 3), lambda g, b, c: (g, 0, 0))],
        out_specs=(pl.BlockSpec((1, Q, xw), lambda g, b, c: (b, c, g)), pl.BlockSpec((1, 1, 1, xw, N), lambda g, b, c: (b, g, c, 0, 0))),
        scratch_shapes=[pltpu.VMEM((HG, P, N), F32)], name="b_ssd_fwd",
        compiler_params=_params(("parallel", "arbitrary", "arbitrary")),
    )(xc3, xc3, xc3, dtc, dtr, prow, pcol)


def _ssd_bwd(xc3, dtc, dtr, prow, pcol, states, dy3):
    B, S, _ = xc3.shape
    Q, G, HG, P, N = SSM_CHUNK, SSM_N_GROUPS, SSM_HG, HEAD_DIM, SSM_D_STATE
    nc = S // Q
    xw = HG * P

    def body(x_ref, b_ref, c_ref, dtc_ref, dtr_ref, pr_ref, pc_ref, st_ref, dy_ref,
             dx_ref, db_ref, dc_ref, ddt_ref, dpar_ref, dstate):
        bi, ci = pl.program_id(1), pl.program_id(2)

        @pl.when(ci == 0)
        def _():
            dstate[...] = jnp.zeros_like(dstate)

        x, Bm, Cm, dy = x_ref[0], b_ref[0], c_ref[0], dy_ref[0]
        pr = pr_ref[0]
        zc, dt_c, A_r, cs_c, cs_r, row, col, tril = _ssd_common(x, Bm, Cm, dtc_ref[0, 0], dtr_ref[0, 0], pr, pc_ref[0])
        Bb, Cb = Bm.astype(BF16), Cm.astype(BF16)
        CB = lax.dot_general(Cb, Bb, NT, preferred_element_type=F32)
        CBt = lax.dot_general(Bb, Cb, NT, preferred_element_type=F32)
        lane4 = lax.broadcasted_iota(jnp.int32, (Q, HG), 1)
        lane4r = lax.broadcasted_iota(jnp.int32, (1, HG), 1)
        rowq = lax.broadcasted_iota(jnp.int32, (Q, 1), 0)
        dB = jnp.zeros((Q, N), F32)
        dC = jnp.zeros((Q, N), F32)
        dcs4 = jnp.zeros((Q, HG), F32)
        dtx4 = jnp.zeros((Q, HG), F32)
        dD4 = jnp.zeros((1, HG), F32)
        dxts, xhs, dyhs = [], [], []
        for hg in range(HG):
            xh = x[:, P * hg:P * (hg + 1)]
            dyh = dy[:, P * hg:P * (hg + 1)]
            xt = xh * dt_c[:, hg:hg + 1]
            xtb, dyb = xt.astype(BF16), dyh.astype(BF16)
            csc, csr = cs_c[:, hg:hg + 1], cs_r[hg:hg + 1, :]
            L = jnp.where(row >= col, jnp.exp(jnp.minimum(csc - csr, 0.0)), 0.0)
            Lt = jnp.where(col >= row, jnp.exp(jnp.minimum(csr - csc, 0.0)), 0.0)
            M, Mt = CB * L, CBt * Lt
            Sh = st_ref[0, 0, 0, P * hg:P * (hg + 1), :]
            dSh = dstate[hg]
            Shb, dShb = Sh.astype(BF16), dSh.astype(BF16)
            ecs = jnp.exp(csc)
            csq = csc[Q - 1:Q, :]
            dec = jnp.exp(csq - csc)
            dxt = jnp.dot(Mt.astype(BF16), dyb, preferred_element_type=F32)
            dxt = dxt + lax.dot_general(Bb, dShb, NT, preferred_element_type=F32) * dec
            Gm = lax.dot_general(dyb, xtb, NT, preferred_element_type=F32)
            Gt = lax.dot_general(xtb, dyb, NT, preferred_element_type=F32)
            dC = dC + jnp.dot((Gm * L).astype(BF16), Bb, preferred_element_type=F32)
            dB = dB + jnp.dot((Gt * Lt).astype(BF16), Cb, preferred_element_type=F32)
            dC = dC + jnp.dot(dyb, Shb, preferred_element_type=F32) * ecs
            dBst = jnp.dot(xtb, dShb, preferred_element_type=F32) * dec
            dB = dB + dBst
            dcs = jnp.sum(Gm * M, axis=1, keepdims=True) - jnp.sum(Gt * Mt, axis=1, keepdims=True)
            yoff = lax.dot_general(Cb, Shb, NT, preferred_element_type=F32) * ecs
            dcs = dcs + jnp.sum(yoff * dyh, axis=1, keepdims=True)
            r = jnp.sum(dBst * Bm, axis=1, keepdims=True)
            dcs = dcs - r
            extra = jnp.sum(r, axis=0, keepdims=True) + jnp.exp(csq) * jnp.sum(
                jnp.sum(dSh * Sh, axis=1, keepdims=True), axis=0, keepdims=True)
            dcs = dcs + jnp.where(rowq == Q - 1, extra, 0.0)
            dcs4 = jnp.where(lane4 == hg, dcs, dcs4)
            dtx4 = jnp.where(lane4 == hg, jnp.sum(dxt * xh, axis=1, keepdims=True), dtx4)
            dD4 = jnp.where(lane4r == hg, jnp.sum(jnp.sum(dyh * xh, axis=1, keepdims=True), axis=0, keepdims=True), dD4)
            dstate[hg] = dSh * jnp.exp(csq) + lax.dot_general((dyh * ecs).astype(BF16), Cb, TN, preferred_element_type=F32)
            dxts.append(dxt)
            xhs.append(xh)
            dyhs.append(dyh)
        da4 = lax.dot_general(tril, dcs4, TN, preferred_element_type=F32, precision=HI)
        ddt4 = da4 * A_r + dtx4
        ddtraw = ddt4 * jax.nn.sigmoid(zc)
        ddt_ref[0, 0] = ddtraw
        dxs = [dxts[hg] * dt_c[:, hg:hg + 1] + pr[2:3, hg:hg + 1] * dyhs[hg] for hg in range(HG)]
        dx_ref[0] = jnp.concatenate([jnp.concatenate(dxs[0:2], axis=-1), jnp.concatenate(dxs[2:4], axis=-1)], axis=-1)
        db_ref[0] = dB
        dc_ref[0] = dC
        d_bias = jnp.sum(ddtraw, axis=0, keepdims=True)
        d_alog = jnp.sum(da4 * dt_c, axis=0, keepdims=True) * A_r
        r3 = lax.broadcasted_iota(jnp.int32, (3, HG), 0)
        dpar = jnp.where(r3 == 0, d_bias, jnp.where(r3 == 1, d_alog, dD4))
        first = (bi == 0) & (ci == 0)

        @pl.when(first)
        def _():
            dpar_ref[0] = dpar

        @pl.when(jnp.logical_not(first))
        def _():
            dpar_ref[0] += dpar

    rc = lambda c: nc - 1 - c
    bo, co = SSM_D_INNER // N, (SSM_D_INNER + SSM_BC_DIM) // N
    return pl.pallas_call(
        body,
        out_shape=(SDS((B, S, SSM_D_INNER), F32), SDS((B, S, SSM_BC_DIM), F32), SDS((B, S, SSM_BC_DIM), F32),
                   SDS((B, G, S, HG), F32), SDS((G, 3, HG), F32)),
        grid=(G, B, nc),
        in_specs=[pl.BlockSpec((1, Q, xw), lambda g, b, c: (b, rc(c), g)), pl.BlockSpec((1, Q, N), lambda g, b, c: (b, rc(c), bo + g)),
                  pl.BlockSpec((1, Q, N), lambda g, b, c: (b, rc(c), co + g)), pl.BlockSpec((1, 1, Q, HG), lambda g, b, c: (b, g, rc(c), 0)),
                  pl.BlockSpec((1, 1, HG, Q), lambda g, b, c: (b, g, 0, rc(c))), pl.BlockSpec((1, 3, HG), lambda g, b, c: (g, 0, 0)),
                  pl.BlockSpec((1, HG, 3), lambda g, b, c: (g, 0, 0)),
                  pl.BlockSpec((1, 1, 1, xw, N), lambda g, b, c: (b, g, rc(c), 0, 0)), pl.BlockSpec((1, Q, xw), lambda g, b, c: (b, rc(c), g))],
        out_specs=(pl.BlockSpec((1, Q, xw), lambda g, b, c: (b, rc(c), g)), pl.BlockSpec((1, Q, N), lambda g, b, c: (b, rc(c), g)),
                   pl.BlockSpec((1, Q, N), lambda g, b, c: (b, rc(c), g)), pl.BlockSpec((1, 1, Q, HG), lambda g, b, c: (b, g, rc(c), 0)),
                   pl.BlockSpec((1, 3, HG), lambda g, b, c: (g, 0, 0))),
        scratch_shapes=[pltpu.VMEM((HG, P, N), F32)], name="b_ssd_bwd",
        compiler_params=_params(("parallel", "arbitrary", "arbitrary")),
    )(xc3, xc3, xc3, dtc, dtr, prow, pcol, states, dy3)


GN_W = SSM_D_INNER // SSM_N_GROUPS


def _gate_fwd(y, zx, nw):
    T = y.shape[0]
    tm = _pick(T, 256, 8)

    def body(y_ref, z_ref, w_ref, o_ref):
        z = z_ref[...]
        gt = y_ref[...] * (z * jax.nn.sigmoid(z))
        outs = []
        for k in range(SSM_N_GROUPS):
            gk = gt[:, GN_W * k:GN_W * (k + 1)]
            outs.append(gk * lax.rsqrt(jnp.mean(gk * gk, axis=-1, keepdims=True) + NORM_EPS))
        o_ref[...] = (jnp.concatenate(outs, axis=-1) * w_ref[...]).astype(BF16)

    row = pl.BlockSpec((tm, SSM_D_INNER), lambda i: (i, 0))
    return pl.pallas_call(
        body, out_shape=SDS((T, SSM_D_INNER), BF16), grid=(T // tm,),
        in_specs=[row, row, pl.BlockSpec((1, SSM_D_INNER), lambda i: (0, 0))], out_specs=row, name="b_gate_fwd",
        compiler_params=_params(("parallel",)),
    )(y, zx, nw)


def _gate_bwd(dgn, y, zx, nw):
    T = y.shape[0]
    tm = _pick(T, 256, 8)

    def body(d_ref, y_ref, z_ref, w_ref, dy_ref, dz_ref, dw_ref):
        z, yv, w = z_ref[...], y_ref[...], w_ref[...]
        sg = jax.nn.sigmoid(z)
        sz = z * sg
        gt = yv * sz
        gw = d_ref[...] * w
        dgts, dws = [], []
        for k in range(SSM_N_GROUPS):
            sl = slice(GN_W * k, GN_W * (k + 1))
            gk, gwk = gt[:, sl], gw[:, sl]
            rstd = lax.rsqrt(jnp.mean(gk * gk, axis=-1, keepdims=True) + NORM_EPS)
            dgts.append(rstd * gwk - gk * (rstd * rstd * rstd) * jnp.mean(gwk * gk, axis=-1, keepdims=True))
            dws.append(jnp.sum(d_ref[:, sl] * gk * rstd, axis=0, keepdims=True))
        dgt = jnp.concatenate(dgts, axis=-1)
        dy_ref[...] = dgt * sz
        dz_ref[...] = (dgt * yv * (sg * (1.0 + z * (1.0 - sg)))).astype(BF16)
        dw = jnp.concatenate(dws, axis=-1)

        @pl.when(pl.program_id(0) == 0)
        def _():
            dw_ref[...] = dw

        @pl.when(pl.program_id(0) > 0)
        def _():
            dw_ref[...] += dw

    row = pl.BlockSpec((tm, SSM_D_INNER), lambda i: (i, 0))
    vec = pl.BlockSpec((1, SSM_D_INNER), lambda i: (0, 0))
    return pl.pallas_call(
        body, out_shape=(SDS((T, SSM_D_INNER), F32), SDS((T, SSM_D_INNER), BF16), SDS((1, SSM_D_INNER), F32)), grid=(T // tm,),
        in_specs=[row, row, row, vec], out_specs=(row, row, vec), name="b_gate_bwd", compiler_params=_params(("arbitrary",)),
    )(dgn, y, zx, nw)


MESH = pl.DeviceIdType.MESH
ANY = pl.BlockSpec(memory_space=pl.ANY)


N_CHIPS = 4


def _dev_block(ref, kind, j, size):
    if kind == "slot":
        return ref.at[j]
    start = pl.multiple_of(j * size, size)
    nd = len(ref.shape)
    if kind == "col":
        return ref.at[(slice(None),) * (nd - 1) + (pl.ds(start, size),)]
    return ref.at[(slice(None),) * (nd - 2) + (pl.ds(start, size), slice(None))]


def _dma_sems(n, k):
    return [pltpu.SemaphoreType.DMA((n, k)), pltpu.SemaphoreType.DMA((n, k)), pltpu.SemaphoreType.DMA((n, k))]


def _place(shard, layer, kind, full_shape, dev, name):
    k, n = shard.shape[1:]
    tr = _pick(k, 512, 16)
    nb = k // tr

    def body(dev_ref, s_ref, o_ref):
        if kind == "slot":
            o_ref[0] = s_ref[0].astype(BF16)
        else:
            o_ref[...] = s_ref[0].astype(BF16)

    out_spec = {"slot": pl.BlockSpec((1, tr, n), lambda i, d: (d[0], i, 0)),
                "row": pl.BlockSpec((tr, n), lambda i, d: (d[0] * nb + i, 0)),
                "col": pl.BlockSpec((tr, n), lambda i, d: (i, d[0]))}[kind]
    return pl.pallas_call(
        body, out_shape=SDS(full_shape, BF16),
        grid_spec=pltpu.PrefetchScalarGridSpec(
            num_scalar_prefetch=1, grid=(nb,), in_specs=[pl.BlockSpec((1, tr, n), lambda i, d: (layer, i, 0))], out_specs=out_spec),
        name=name, compiler_params=_params(("arbitrary",)),
    )(dev, shard)


def _gather(items, name):
    n = len(items)

    def body(*refs):
        srcs, dsts = refs[:n], refs[n:2 * n]
        send_sems, recv_sems, local_sems = refs[2 * n:]
        px, py, pc = lax.axis_index("x"), lax.axis_index("y"), lax.axis_index("c")
        me, sibling = (px, py, pc), (px, py, 1 - pc)
        chips = [(1 - px, py), (px, 1 - py), (1 - px, 1 - py)]

        def blk(a, dev):
            return _dev_block(dsts[a], items[a][1], 4 * dev[0] + 2 * dev[1] + dev[2], items[a][2])

        def src_of(a):
            return blk(a, me) if items[a][4] else srcs[a]

        def copy(a, k, block, to, src=None):
            return pltpu.make_async_remote_copy(
                src_ref=blk(a, block) if src is None else src, dst_ref=blk(a, block),
                send_sem=send_sems.at[a, k], recv_sem=recv_sems.at[a, k], device_id=to, device_id_type=MESH)

        mine = [pltpu.make_async_copy(srcs[a], blk(a, me), local_sems.at[a, 0]) for a in range(n) if not items[a][4]]
        for cp in mine:
            cp.start()
        first = []
        for a in range(n):
            first.append(copy(a, 0, me, sibling, src=src_of(a)))
            first += [copy(a, 1 + j, me, (*chip, pc), src=src_of(a)) for j, chip in enumerate(chips)]
        for cp in first:
            cp.start()
        passed = []
        for j, chip in enumerate(chips):
            for a in range(n):
                copy(a, 1 + j, (*chip, pc), me).wait_recv()
                fwd = copy(a, 4 + j, (*chip, pc), sibling)
                fwd.start()
                passed.append(fwd)
        for a in range(n):
            copy(a, 0, sibling, me).wait_recv()
            for j, chip in enumerate(chips):
                copy(a, 4 + j, (*chip, 1 - pc), me).wait_recv()
        for cp in first + passed:
            cp.wait_send()
        for cp in mine:
            cp.wait()

    return pl.pallas_call(
        body, out_shape=[SDS(it[3], it[0].dtype) for it in items], in_specs=[ANY] * n, out_specs=[ANY] * n,
        input_output_aliases={a: a for a in range(n) if items[a][4]}, scratch_shapes=_dma_sems(n, 7), name=name,
    )(*[it[0] for it in items])


def _reduce_d2d(items, name):
    n = len(items)

    def body(*refs):
        gs, gots = refs[:n], refs[n:2 * n]
        send_sems, recv_sems, _ = refs[2 * n:]
        px, py, pc = lax.axis_index("x"), lax.axis_index("y"), lax.axis_index("c")
        copies = []
        for a in range(n):
            _, kind, size, _ = items[a]
            for q in range(N_CHIPS):
                copies.append(pltpu.make_async_remote_copy(
                    src_ref=_dev_block(gs[a], kind, 2 * q + 1 - pc, size), dst_ref=gots[a].at[q], send_sem=send_sems.at[a, q],
                    recv_sem=recv_sems.at[a, q], device_id=(px, py, 1 - pc), device_id_type=MESH))
        for cp in copies:
            cp.start()
        for cp in copies:
            cp.wait()

    return pl.pallas_call(
        body, out_shape=[SDS((N_CHIPS,) + tuple(it[3]), F32) for it in items], in_specs=[ANY] * n, out_specs=[ANY] * n,
        scratch_shapes=_dma_sems(n, N_CHIPS), name=name,
    )(*[it[0] for it in items])


def _pair_sum(g, got, kind, core, name):
    _, k, n = got.shape
    tr = _pick(k, max(16, STREAM_VMEM // (2 * n * 10)), 16)
    nb = k // tr

    def body(c_ref, g_ref, s_ref, o_ref):
        mine = g_ref[0] if kind == "slot" else g_ref[...]
        o_ref[0] = (mine + s_ref[0]).astype(BF16)

    g_spec = {"slot": pl.BlockSpec((1, tr, n), lambda q, i, c: (2 * q + c[0], i, 0)),
              "row": pl.BlockSpec((tr, n), lambda q, i, c: ((2 * q + c[0]) * nb + i, 0)),
              "col": pl.BlockSpec((tr, n), lambda q, i, c: (i, 2 * q + c[0]))}[kind]
    part = pl.BlockSpec((1, tr, n), lambda q, i, c: (q, i, 0))
    return pl.pallas_call(
        body, out_shape=SDS((N_CHIPS, k, n), BF16),
        grid_spec=pltpu.PrefetchScalarGridSpec(num_scalar_prefetch=1, grid=(N_CHIPS, nb), in_specs=[g_spec, part], out_specs=part),
        name=name, compiler_params=_params(("arbitrary", "arbitrary")),
    )(core, g, got)


def _reduce_ici(parts, name):
    n = len(parts)

    def body(*refs):
        ps, rs = refs[:n], refs[n:2 * n]
        send_sems, recv_sems, _ = refs[2 * n:]
        px, py, pc = lax.axis_index("x"), lax.axis_index("y"), lax.axis_index("c")
        my_chip = 2 * px + py
        sends, recvs = [], []
        for a in range(n):
            for k in range(1, N_CHIPS):
                qx, qy = px ^ (k >> 1), py ^ (k & 1)
                q = 2 * qx + qy
                kw = dict(send_sem=send_sems.at[a, k - 1], recv_sem=recv_sems.at[a, k - 1], device_id=(qx, qy, pc),
                          device_id_type=MESH)
                sends.append(pltpu.make_async_remote_copy(src_ref=ps[a].at[q], dst_ref=rs[a].at[my_chip], **kw))
                recvs.append(pltpu.make_async_remote_copy(src_ref=ps[a].at[q], dst_ref=rs[a].at[q], **kw))
        for cp in sends:
            cp.start()
        for cp in recvs:
            cp.wait_recv()
        for cp in sends:
            cp.wait_send()

    return pl.pallas_call(
        body, out_shape=[SDS(p.shape, p.dtype) for p in parts], in_specs=[ANY] * n, out_specs=[ANY] * n,
        scratch_shapes=_dma_sems(n, N_CHIPS - 1), name=name,
    )(*parts)


def _adam_update(g, w, m, v):
    c1 = 1.0 - ADAM_B1 ** ADAM_STEP
    c2 = 1.0 - ADAM_B2 ** ADAM_STEP
    nm = ADAM_B1 * m + (1.0 - ADAM_B1) * g
    nv = ADAM_B2 * v + (1.0 - ADAM_B2) * (g * g)
    delta = -ADAM_LR * ((nm / c1) / (jnp.sqrt(nv / c2) + ADAM_EPS) + ADAM_WD * w)
    return delta, nm, nv


def _adamw(parts, recv, w, m, v, chip, name):
    R, C = w.shape
    row_bytes = 2 * C * (N_CHIPS * 2 + 7 * 4)
    tr = _pick(R, max(16, STREAM_VMEM // row_bytes), 16)

    def body(ch_ref, own_ref, r1_ref, r2_ref, r3_ref, w_ref, m_ref, v_ref, g_ref, d_ref, nm_ref, nv_ref):
        g = own_ref[0].astype(F32)
        for r_ref in (r1_ref, r2_ref, r3_ref):
            g = g + r_ref[0].astype(F32)
        g_ref[...] = g
        d_ref[...], nm_ref[...], nv_ref[...] = _adam_update(g, w_ref[...], m_ref[...], v_ref[...])

    row = pl.BlockSpec((tr, C), lambda i, ch: (i, 0))
    other = lambda k: pl.BlockSpec((1, tr, C), lambda i, ch: (ch[0] ^ k, i, 0))
    out = SDS((R, C), F32)
    return pl.pallas_call(
        body, out_shape=(out, out, out, out),
        grid_spec=pltpu.PrefetchScalarGridSpec(
            num_scalar_prefetch=1, grid=(R // tr,),
            in_specs=[pl.BlockSpec((1, tr, C), lambda i, ch: (ch[0], i, 0)), other(2), other(1), other(3), row, row, row],
            out_specs=(row, row, row, row)),
        name=name, compiler_params=_params(("arbitrary",)),
    )(chip, parts, recv, recv, recv, w, m, v)


def _small_adamw(gathered, ws, ms, vs):
    n = len(ws)

    def body(*refs):
        g_in, w_in, m_in, v_in = refs[:n], refs[n:2 * n], refs[2 * n:3 * n], refs[3 * n:4 * n]
        outs = refs[4 * n:]
        for i in range(n):
            g = g_in[i][0]
            for dev in range(1, N_DEV):
                g = g + g_in[i][dev]
            d, nm, nv = _adam_update(g, w_in[i][...], m_in[i][...], v_in[i][...])
            outs[i][...] = g
            outs[n + i][...] = d
            outs[2 * n + i][...] = nm
            outs[3 * n + i][...] = nv

    shapes = [SDS(w.shape, F32) for w in ws]
    outs = pl.pallas_call(body, out_shape=shapes * 4, name="small_adamw")(*gathered, *ws, *ms, *vs)
    return outs[:n], outs[n:2 * n], outs[2 * n:3 * n], outs[3 * n:]


W_NAMES = ("norm_mix_w", "norm_mlp_w", "a_w_qkv", "a_b_qkv", "a_sinks", "a_w_o", "a_b_o", "b_in_w", "b_conv_w", "b_conv_b",
           "b_dt_bias", "b_a_log", "b_d", "b_norm_w", "b_out_w", "c_w_qkv", "c_w_o", "mlp_w_up", "mlp_w_down", "final_norm_w")
BIG_KIND = {"a_w_qkv": "slot", "a_w_o": "row", "b_in_w": "slot", "b_out_w": "row", "c_w_qkv": "col", "c_w_o": "row",
            "mlp_w_up": "col", "mlp_w_down": "row"}
SMALL_SHARDED = {"a_b_qkv": 1, "a_b_o": 1, "b_conv_w": 2}
SMALL_REPLICATED = ("norm_mix_w", "norm_mlp_w", "a_sinks", "b_conv_b", "b_dt_bias", "b_a_log", "b_d", "b_norm_w", "final_norm_w")


def _layer_big(i):
    kind, j = i % 3, i // 3
    mix = {0: [("a_w_qkv", j), ("a_w_o", j)], 1: [("b_in_w", 0), ("b_out_w", 0)], 2: [("c_w_qkv", 0), ("c_w_o", 0)]}[kind]
    return mix + [("mlp_w_up", i), ("mlp_w_down", i)]


def _block_size(kind, shard2d):
    return {"slot": None, "row": shard2d[0], "col": shard2d[1]}[kind]


def _full2d(kind, shard2d):
    k, n = shard2d
    return {"slot": (N_DEV, k, n), "row": (N_DEV * k, n), "col": (k, N_DEV * n)}[kind]


def _from_slots(t, ax):
    s = t.shape[1:]
    return jnp.moveaxis(t, 0, ax).reshape(s[:ax] + (N_DEV * s[ax],) + s[ax + 1:])


def _to_slots(g, ax):
    s = g.shape
    return jnp.moveaxis(g.reshape(s[:ax] + (N_DEV, s[ax] // N_DEV) + s[ax + 1:]), ax, 0)


def _rope_tables(positions):
    half = HEAD_DIM // 2
    inv = ROPE_THETA ** (-jnp.arange(half, dtype=F32) / half)
    ang = positions.astype(F32)[..., None] * inv
    rep = LANES // half
    cos = jnp.tile(jnp.cos(ang), (1, 1, rep))
    sin = jnp.tile(jnp.sin(ang), (1, 1, rep))
    T = positions.shape[0] * positions.shape[1]
    return cos.reshape(T, LANES), sin.reshape(T, LANES)


def _swa_fwd(u, h, p, j, B, S, cos, sin, tag):
    qkv = _matmul(u, p["a_w_qkv"][j], out_dtype=BF16, bias=p["a_b_qkv"][j][None], rope=(cos, sin),
                  rope_cols=A_Q_DIM + A_KV_DIM, tn=640, name=f"{tag}_qkv")
    o, lse = _attn_fwd(qkv, B, S, 1, n_heads=A_N_HEADS, n_kv=A_N_KV, q_col=0, k_col=A_Q_DIM, v_col=A_Q_DIM + A_KV_DIM,
                       max_dist=A_WINDOW - 1, sinks=p["a_sinks"][j], name=f"{tag}_attn")
    h1 = _matmul(o, p["a_w_o"][j], bias=p["a_b_o"][j][None], resid=h, name=f"{tag}_o")
    return h1, (qkv, o, lse)


def _swa_bwd(dh1, u, saved, p, j, B, S, cos, sin, tag):
    qkv, o, lse = saved
    kw = dict(n_heads=A_N_HEADS, n_kv=A_N_KV, q_col=0, k_col=A_Q_DIM, v_col=A_Q_DIM + A_KV_DIM, max_dist=A_WINDOW - 1)
    g = {}
    do = _matmul(dh1, p["a_w_o"][j], tb=True, name=f"{tag}_do")
    g["a_w_o"] = _matmul(o, dh1, ta=True, name=f"{tag}_dwo")
    g["a_b_o"] = _colsum(dh1, f"{tag}_dbo")[0]
    sk = jnp.pad(p["a_sinks"][j], (0, LANES - A_N_HEADS))[None]
    delta, dob, dsink = _delta(do, o, lse, sk, name=f"{tag}_delta")
    g["a_sinks"] = dsink[0, :A_N_HEADS]
    dq = _attn_dq(qkv, dob, lse, delta, cos, sin, B, S, 1, name=f"{tag}_dq", **kw)
    dk, dv = _attn_dkv(qkv, dob, lse, delta, cos, sin, B, S, 1, name=f"{tag}_dkv", **kw)
    dqkv = jnp.concatenate([dq, dk, dv], axis=1)
    g["a_w_qkv"] = _matmul(u, dqkv, ta=True, tn=640, name=f"{tag}_dwqkv")
    g["a_b_qkv"] = _colsum(dqkv, f"{tag}_dbqkv")[0]
    du = _matmul(dqkv, p["a_w_qkv"][j], tb=True, tk=640, name=f"{tag}_du")
    return du, g


def _dil_fwd(u, h, p, B, S, cos, sin):
    W = C_HEADS * HEAD_DIM
    qkv = _matmul(u, p["c_w_qkv"][0], out_dtype=BF16, rope=(cos, sin), rope_cols=6 * W, name="c_qkv")
    os_, lses = [], []
    for gi, (window, dil) in enumerate(C_PATTERNS):
        o, lse = _attn_fwd(qkv, B, S, dil, n_heads=C_HEADS, n_kv=C_HEADS, q_col=gi * W, k_col=(3 + gi) * W,
                           v_col=(6 + gi) * W, max_dist=window // dil, sinks=None, name=f"c_attn{gi}")
        os_.append(o)
        lses.append(lse)
    o, lse = _merge(os_, lses)
    h1 = _matmul(o, p["c_w_o"][0], resid=h, name="c_o")
    return h1, (qkv, o, lse)


def _dil_bwd(dh1, u, saved, p, B, S, cos, sin):
    W = C_HEADS * HEAD_DIM
    qkv, o, lse = saved
    g = {}
    do = _matmul(dh1, p["c_w_o"][0], tb=True, name="c_do")
    g["c_w_o"] = _matmul(o, dh1, ta=True, name="c_dwo")[None]
    delta, dob = _delta(do, o, name="c_delta")
    dqs, dks, dvs = [], [], []
    for gi, (window, dil) in enumerate(C_PATTERNS):
        kw = dict(n_heads=C_HEADS, n_kv=C_HEADS, q_col=gi * W, k_col=(3 + gi) * W, v_col=(6 + gi) * W, max_dist=window // dil)
        dqs.append(_attn_dq(qkv, dob, lse, delta, cos, sin, B, S, dil, name=f"c_dq{gi}", **kw))
        dk, dv = _attn_dkv(qkv, dob, lse, delta, cos, sin, B, S, dil, name=f"c_dkv{gi}", **kw)
        dks.append(dk)
        dvs.append(dv)
    dqkv = jnp.concatenate(dqs + dks + dvs, axis=1)
    g["c_w_qkv"] = _matmul(u, dqkv, ta=True, name="c_dwqkv")[None]
    du = _matmul(dqkv, p["c_w_qkv"][0], tb=True, name="c_du")
    return du, g


def _ssm_params(p):
    par = jnp.stack([p["b_dt_bias"][0], p["b_a_log"][0], p["b_d"][0]], axis=0)
    prow = par.reshape(3, SSM_N_GROUPS, SSM_HG).transpose(1, 0, 2)
    return prow, prow.transpose(0, 2, 1)


def _mamba_fwd(u, h, p, B, S):
    T = B * S
    G, HG = SSM_N_GROUPS, SSM_HG
    w_in = p["b_in_w"][0]
    nzx = SSM_D_INNER + SSM_CONV_DIM
    w_dt = jnp.pad(w_in[:, nzx:], ((0, 0), (0, LANES - SSM_N_HEADS)))
    zx = _matmul(u, w_in[:, :nzx], name="b_zx")
    dtraw = _matmul(u, w_dt, name="b_dt")[:, :SSM_N_HEADS]
    dtc = dtraw.reshape(B, S, G, HG).transpose(0, 2, 1, 3)
    dtr = dtraw.reshape(B, S, G, HG).transpose(0, 2, 3, 1)
    prow, pcol = _ssm_params(p)
    zx3 = zx.reshape(B, S, nzx)
    xc3 = _conv_fwd(zx3, p["b_conv_w"][0], p["b_conv_b"])
    y3, states = _ssd_fwd(xc3, dtc, dtr, prow, pcol)
    y = y3.reshape(T, SSM_D_INNER)
    gn = _gate_fwd(y, zx, p["b_norm_w"])
    h1 = _matmul(gn, p["b_out_w"][0], resid=h, name="b_out")
    return h1, (zx, dtc, dtr, xc3, y, states, gn, w_dt)


def _mamba_bwd(dh1, u, saved, p, B, S):
    T = B * S
    zx, dtc, dtr, xc3, y, states, gn, w_dt = saved
    nzx = SSM_D_INNER + SSM_CONV_DIM
    w_in = p["b_in_w"][0]
    prow, pcol = _ssm_params(p)
    g = {}
    dgn = _matmul(dh1, p["b_out_w"][0], tb=True, name="b_dgn")
    g["b_out_w"] = _matmul(gn, dh1, ta=True, name="b_dwout")[None]
    dy, dz, dnw = _gate_bwd(dgn, y, zx, p["b_norm_w"])
    g["b_norm_w"] = dnw
    dx3, dB3, dC3, ddt, dpar = _ssd_bwd(xc3, dtc, dtr, prow, pcol, states, dy.reshape(B, S, SSM_D_INNER))
    dpar = dpar.transpose(1, 0, 2).reshape(3, SSM_N_HEADS)
    g["b_dt_bias"], g["b_a_log"], g["b_d"] = dpar[0:1], dpar[1:2], dpar[2:3]
    zx3 = zx.reshape(B, S, nzx)
    cw, cb = p["b_conv_w"][0], p["b_conv_b"]
    parts, dws, dbs = [], [], []
    for col0, dpart, nm in ((0, dx3, "b_conv_bwd_x"), (SSM_D_INNER, dB3, "b_conv_bwd_b"),
                            (SSM_D_INNER + SSM_BC_DIM, dC3, "b_conv_bwd_c")):
        dxp, dw, db = _conv_bwd(zx3, dpart, cw, cb, col0, nm)
        parts.append(dxp.reshape(T, -1))
        dws.append(dw)
        dbs.append(db)
    g["b_conv_w"] = jnp.concatenate(dws, axis=1)[None]
    g["b_conv_b"] = jnp.concatenate(dbs, axis=1)
    dzx = jnp.concatenate([dz] + parts, axis=1)
    ddtraw = ddt.transpose(0, 2, 1, 3).reshape(T, SSM_N_HEADS)
    ddtp = jnp.pad(ddtraw, ((0, 0), (0, LANES - SSM_N_HEADS)))
    dw_zx = _matmul(u, dzx, ta=True, name="b_dwzx")
    dw_dt = _matmul(u, ddtp, ta=True, name="b_dwdt")[:, :SSM_N_HEADS]
    g["b_in_w"] = jnp.concatenate([dw_zx, dw_dt], axis=1)[None]
    du = _matmul(dzx, w_in[:, :nzx], tb=True, name="b_du_zx")
    du = _matmul(ddtp, w_dt, tb=True, resid=du, name="b_du_dt")
    return du, g


def _local_step(x, positions, p, target):
    B, S, D = x.shape
    T = B * S
    cos, sin = _rope_tables(positions)
    h = x.reshape(T, D)
    tape = []
    for i in range(DEPTH):
        kind, j = i % 3, i // 3
        u = _rmsnorm_fwd(h, p["norm_mix_w"][i], f"l{i}_norm_mix")
        if kind == 0:
            h1, saved = _swa_fwd(u, h, p, j, B, S, cos, sin, f"a{j}")
        elif kind == 1:
            h1, saved = _mamba_fwd(u, h, p, B, S)
        else:
            h1, saved = _dil_fwd(u, h, p, B, S, cos, sin)
        u2 = _rmsnorm_fwd(h1, p["norm_mlp_w"][i], f"l{i}_norm_mlp")
        r, s = _matmul(u2, p["mlp_w_up"][i], out_dtype=BF16, relu2=True, name=f"l{i}_up")
        h2 = _matmul(s, p["mlp_w_down"][i], resid=h1, name=f"l{i}_down")
        tape.append((h, u, saved, h1, u2, r, s))
        h = h2
    dh, dwf, loss = _final_loss(h, target.reshape(T, D), p["final_norm_w"])
    grads = {"final_norm_w": dwf[0]}
    per_layer = {n: [None] * DEPTH for n in ("norm_mix_w", "norm_mlp_w", "mlp_w_up", "mlp_w_down")}
    a_grads = [None, None]
    for i in reversed(range(DEPTH)):
        kind, j = i % 3, i // 3
        h0, u, saved, h1, u2, r, s = tape[i]
        da = _matmul(dh, p["mlp_w_down"][i], tb=True, out_dtype=BF16, mul=r, mul_scale=2.0, name=f"l{i}_da")
        per_layer["mlp_w_down"][i] = _matmul(s, dh, ta=True, name=f"l{i}_dwdown")
        per_layer["mlp_w_up"][i] = _matmul(u2, da, ta=True, name=f"l{i}_dwup")
        du2 = _matmul(da, p["mlp_w_up"][i], tb=True, name=f"l{i}_du2")
        dh1, dnw = _rmsnorm_bwd(h1, du2, p["norm_mlp_w"][i], dh, f"l{i}_norm_mlp_bwd")
        per_layer["norm_mlp_w"][i] = dnw[0]
        if kind == 0:
            du, g = _swa_bwd(dh1, u, saved, p, j, B, S, cos, sin, f"a{j}")
            a_grads[j] = g
        elif kind == 1:
            du, g = _mamba_bwd(dh1, u, saved, p, B, S)
            grads.update(g)
        else:
            du, g = _dil_bwd(dh1, u, saved, p, B, S, cos, sin)
            grads.update(g)
        dh, dnw = _rmsnorm_bwd(h0, du, p["norm_mix_w"][i], dh1, f"l{i}_norm_mix_bwd")
        per_layer["norm_mix_w"][i] = dnw[0]
    for n in ("norm_mix_w", "norm_mlp_w"):
        grads[n] = jnp.stack(per_layer[n], axis=0)
    for n in ("mlp_w_up", "mlp_w_down"):
        grads[n] = per_layer[n]
    for n in ("a_b_qkv", "a_sinks", "a_b_o"):
        grads[n] = jnp.stack([a_grads[0][n], a_grads[1][n]], axis=0)
    for n in ("a_w_qkv", "a_w_o"):
        grads[n] = [a_grads[0][n], a_grads[1][n]]
    for n in ("b_in_w", "b_out_w", "c_w_qkv", "c_w_o"):
        grads[n] = [grads[n][0]]
    return loss, dh.reshape(B, S, D), grads


def kernel(x, positions, norm_mix_w, norm_mlp_w, a_w_qkv, a_b_qkv, a_sinks, a_w_o, a_b_o, b_in_w, b_conv_w, b_conv_b, b_dt_bias, b_a_log, b_d, b_norm_w, b_out_w, c_w_qkv, c_w_o, mlp_w_up, mlp_w_down, final_norm_w, loss_target, m_norm_mix_w, m_norm_mlp_w, m_a_w_qkv, m_a_b_qkv, m_a_sinks, m_a_w_o, m_a_b_o, m_b_in_w, m_b_conv_w, m_b_conv_b, m_b_dt_bias, m_b_a_log, m_b_d, m_b_norm_w, m_b_out_w, m_c_w_qkv, m_c_w_o, m_mlp_w_up, m_mlp_w_down, m_final_norm_w, v_norm_mix_w, v_norm_mlp_w, v_a_w_qkv, v_a_b_qkv, v_a_sinks, v_a_w_o, v_a_b_o, v_b_in_w, v_b_conv_w, v_b_conv_b, v_b_dt_bias, v_b_a_log, v_b_d, v_b_norm_w, v_b_out_w, v_c_w_qkv, v_c_w_o, v_mlp_w_up, v_mlp_w_down, v_final_norm_w):
    w = dict(zip(W_NAMES, (norm_mix_w, norm_mlp_w, a_w_qkv, a_b_qkv, a_sinks, a_w_o, a_b_o, b_in_w, b_conv_w, b_conv_b,
                           b_dt_bias, b_a_log, b_d, b_norm_w, b_out_w, c_w_qkv, c_w_o, mlp_w_up, mlp_w_down, final_norm_w)))
    m = dict(zip(W_NAMES, (m_norm_mix_w, m_norm_mlp_w, m_a_w_qkv, m_a_b_qkv, m_a_sinks, m_a_w_o, m_a_b_o, m_b_in_w,
                           m_b_conv_w, m_b_conv_b, m_b_dt_bias, m_b_a_log, m_b_d, m_b_norm_w, m_b_out_w, m_c_w_qkv, m_c_w_o,
                           m_mlp_w_up, m_mlp_w_down, m_final_norm_w)))
    v = dict(zip(W_NAMES, (v_norm_mix_w, v_norm_mlp_w, v_a_w_qkv, v_a_b_qkv, v_a_sinks, v_a_w_o, v_a_b_o, v_b_in_w,
                           v_b_conv_w, v_b_conv_b, v_b_dt_bias, v_b_a_log, v_b_d, v_b_norm_w, v_b_out_w, v_c_w_qkv, v_c_w_o,
                           v_mlp_w_up, v_mlp_w_down, v_final_norm_w)))
    px, py, pc = lax.axis_index("x"), lax.axis_index("y"), lax.axis_index("c")
    me = 4 * px + 2 * py + pc
    dev, chip, core = (t.astype(jnp.int32).reshape(1) for t in (me, 2 * px + py, pc))

    trio = tuple(SMALL_SHARDED)
    got = _gather([(d[n], "slot", None, (N_DEV,) + d[n].shape, False) for n in trio for d in (w, m, v)], "gather_small")
    slots = {n: got[3 * i:3 * i + 3] for i, n in enumerate(trio)}
    p = {n: w[n] for n in SMALL_REPLICATED}
    for n in trio:
        p[n] = _from_slots(slots[n][0], SMALL_SHARDED[n])
    for n in BIG_KIND:
        p[n] = [None] * w[n].shape[0]
    for i in range(DEPTH):
        names = _layer_big(i)
        items = []
        for n, l in names:
            kind, s2 = BIG_KIND[n], w[n].shape[1:]
            placed = _place(w[n], l, kind, _full2d(kind, s2), dev, f"place_l{i}_{n}")
            items.append((placed, kind, _block_size(kind, s2), _full2d(kind, s2), True))
        for (n, l), t in zip(names, _gather(items, f"gather_l{i}")):
            p[n][l] = _from_slots(t, 1) if BIG_KIND[n] == "slot" else t

    loss_part, dx, grads = _local_step(x, positions, p, loss_target)
    loss = lax.psum(loss_part[0, 0], AXES)

    res = {n: [[None] * w[n].shape[0] for _ in range(4)] for n in BIG_KIND}
    for i in reversed(range(DEPTH)):
        names = _layer_big(i)
        items = []
        for n, l in names:
            kind, s2 = BIG_KIND[n], w[n].shape[1:]
            g = grads[n][l]
            items.append((_to_slots(g, 1) if kind == "slot" else g, kind, _block_size(kind, s2), s2))
        sib = _reduce_d2d(items, f"reduce_d2d_l{i}")
        parts = [_pair_sum(it[0], s, it[1], core, f"pair_sum_l{i}_{n}") for (n, _), it, s in zip(names, items, sib)]
        recv = _reduce_ici(parts, f"reduce_ici_l{i}")
        for (n, l), pt, r in zip(names, parts, recv):
            for k, o in enumerate(_adamw(pt, r, w[n][l], m[n][l], v[n][l], chip, f"adamw_l{i}_{n}")):
                res[n][k][l] = o
    out = {n: [jnp.stack(res[n][k], axis=0) for k in range(4)] for n in BIG_KIND}

    small = SMALL_REPLICATED + trio
    as2d = lambda t: t.reshape(1, -1) if t.ndim == 1 else t
    g_sm = [as2d(grads[n]) for n in SMALL_REPLICATED] + [_to_slots(grads[n].reshape(p[n].shape), SMALL_SHARDED[n]) for n in trio]
    gathered = _gather([(g, "slot", None, (N_DEV,) + g.shape, False) for g in g_sm], "gather_small_grads")
    ws = [as2d(w[n]) for n in SMALL_REPLICATED] + [slots[n][0] for n in trio]
    ms = [as2d(m[n]) for n in SMALL_REPLICATED] + [slots[n][1] for n in trio]
    vs = [as2d(v[n]) for n in SMALL_REPLICATED] + [slots[n][2] for n in trio]
    sm_out = _small_adamw(gathered, ws, ms, vs)
    for i, n in enumerate(small):
        if n in SMALL_SHARDED:
            out[n] = [lax.dynamic_index_in_dim(sm_out[k][i], me, 0, keepdims=False) for k in range(4)]
        else:
            out[n] = [sm_out[k][i].reshape(w[n].shape) for k in range(4)]
    return (loss, dx, *[out[n][0] for n in W_NAMES], *[out[n][1] for n in W_NAMES], *[out[n][2] for n in W_NAMES],
            *[out[n][3] for n in W_NAMES])
```

```python
import functools
import math

import jax
import jax.numpy as jnp
import numpy as np
from jax import lax
from jax.experimental import pallas as pl
from jax.experimental.pallas import tpu as pltpu

F32 = jnp.float32
BF16 = jnp.bfloat16
SDS = jax.ShapeDtypeStruct

D_MODEL = 1024
DEPTH = 4
BLOCK = 128
ROPE_THETA = 10000.0
NORM_EPS = 1e-5
HEAD_DIM = 64
A_N_HEADS = 16
A_N_KV = 2
A_WINDOW = 128
A_Q_DIM = 1024
A_KV_DIM = 128
SSM_D_INNER = 2048
SSM_N_HEADS = 32
SSM_N_GROUPS = 8
SSM_HG = 4
SSM_D_STATE = 128
SSM_CONV = 4
SSM_CHUNK = 128
SSM_BC_DIM = 1024
SSM_CONV_DIM = 4096
C_PATTERNS = ((128, 1), (512, 4), (2048, 16))
C_HEADS = 16
ADAM_LR, ADAM_B1, ADAM_B2, ADAM_EPS, ADAM_WD, ADAM_STEP = 0.001, 0.9, 0.999, 1e-08, 0.01, 10

N_DEV = 8
AXES = ("x", "y", "c")
LANES = 128
VMEM_LIMIT = 56 * 1024 * 1024
STREAM_VMEM = 16 * 1024 * 1024
NEG = -1e30

NN = (((1,), (0,)), ((), ()))
NT = (((1,), (1,)), ((), ()))
TN = (((0,), (0,)), ((), ()))
HI = lax.Precision.HIGHEST


def _pick(n, cap, mult=LANES):
    best = None
    for t in range(mult, min(n, cap) + 1, mult):
        if n % t == 0:
            best = t
    return best if best is not None else n


def _params(sem):
    return pltpu.CompilerParams(dimension_semantics=sem, vmem_limit_bytes=VMEM_LIMIT)


def _bf(x):
    return x if x.dtype == BF16 else x.astype(BF16)


def _rot_half(y):
    n = y.shape[-1]
    lane = lax.broadcasted_iota(jnp.int32, y.shape, y.ndim - 1)
    return jnp.where((lane % HEAD_DIM) < HEAD_DIM // 2, -pltpu.roll(y, n - 32, y.ndim - 1), pltpu.roll(y, 32, y.ndim - 1))


def _rope(y, cos, sin, sign):
    reps = y.shape[-1] // LANES
    c = jnp.tile(cos, (1, reps)) if reps > 1 else cos
    s = jnp.tile(sin, (1, reps)) if reps > 1 else sin
    return y * c + sign * (_rot_half(y) * s)


def _matmul(a, b, *, ta=False, tb=False, out_dtype=F32, bias=None, resid=None, mul=None, mul_scale=1.0,
            relu2=False, rope=None, rope_cols=0, tm=512, tn=1024, tk=1024, name="mm"):
    M = a.shape[1] if ta else a.shape[0]
    K = a.shape[0] if ta else a.shape[1]
    N = b.shape[0] if tb else b.shape[1]
    assert (b.shape[1] if tb else b.shape[0]) == K
    tm, tn, tk = _pick(M, tm), _pick(N, tn), _pick(K, tk)
    nk = K // tk
    dims = (((0 if ta else 1,), (1 if tb else 0,)), ((), ()))

    def body(*refs):
        it = iter(refs)
        a_ref, b_ref = next(it), next(it)
        bias_ref = next(it) if bias is not None else None
        resid_ref = next(it) if resid is not None else None
        mul_ref = next(it) if mul is not None else None
        cos_ref, sin_ref = (next(it), next(it)) if rope is not None else (None, None)
        o_ref = next(it)
        o2_ref = next(it) if relu2 else None
        acc_ref = next(it)
        k = pl.program_id(2)
        part = lax.dot_general(_bf(a_ref[...]), _bf(b_ref[...]), dims, preferred_element_type=F32)

        @pl.when(k == 0)
        def _():
            acc_ref[...] = part

        @pl.when(k > 0)
        def _():
            acc_ref[...] += part

        @pl.when(k == nk - 1)
        def _():
            y = acc_ref[...]
            if bias_ref is not None:
                y = y + bias_ref[...]
            if rope is not None:
                col = pl.program_id(1) * tn + lax.broadcasted_iota(jnp.int32, y.shape, 1)
                y = jnp.where(col < rope_cols, _rope(y, cos_ref[...], sin_ref[...], 1.0), y)
            if mul_ref is not None:
                y = y * (mul_ref[...].astype(F32) * mul_scale)
            if resid_ref is not None:
                y = y + resid_ref[...]
            if relu2:
                r = jnp.maximum(y, 0.0)
                o_ref[...] = r.astype(o_ref.dtype)
                o2_ref[...] = (r * r).astype(o2_ref.dtype)
            else:
                o_ref[...] = y.astype(o_ref.dtype)

    a_spec = pl.BlockSpec((tk, tm), lambda i, j, k: (k, i)) if ta else pl.BlockSpec((tm, tk), lambda i, j, k: (i, k))
    b_spec = pl.BlockSpec((tn, tk), lambda i, j, k: (j, k)) if tb else pl.BlockSpec((tk, tn), lambda i, j, k: (k, j))
    mn_spec = pl.BlockSpec((tm, tn), lambda i, j, k: (i, j))
    in_specs, args = [a_spec, b_spec], [a, b]
    if bias is not None:
        in_specs.append(pl.BlockSpec((1, tn), lambda i, j, k: (0, j)))
        args.append(bias)
    if resid is not None:
        in_specs.append(mn_spec)
        args.append(resid)
    if mul is not None:
        in_specs.append(mn_spec)
        args.append(mul)
    if rope is not None:
        in_specs += [pl.BlockSpec((tm, LANES), lambda i, j, k: (i, 0))] * 2
        args += [rope[0], rope[1]]
    out_shape = SDS((M, N), out_dtype)
    out_specs = mn_spec
    if relu2:
        out_shape, out_specs = (out_shape, out_shape), (mn_spec, mn_spec)
    return pl.pallas_call(
        body, out_shape=out_shape, grid=(M // tm, N // tn, nk), in_specs=in_specs, out_specs=out_specs,
        scratch_shapes=[pltpu.VMEM((tm, tn), F32)], name=name,
        compiler_params=_params(("parallel", "parallel", "arbitrary")),
    )(*args)


def _colsum(x, name):
    T, N = x.shape
    tm = _pick(T, 1024, 8)

    def body(x_ref, o_ref):
        s = jnp.sum(x_ref[...].astype(F32), axis=0, keepdims=True)

        @pl.when(pl.program_id(0) == 0)
        def _():
            o_ref[...] = s

        @pl.when(pl.program_id(0) > 0)
        def _():
            o_ref[...] += s

    return pl.pallas_call(
        body, out_shape=SDS((1, N), F32), grid=(T // tm,),
        in_specs=[pl.BlockSpec((tm, N), lambda i: (i, 0))], out_specs=pl.BlockSpec((1, N), lambda i: (0, 0)),
        name=name, compiler_params=_params(("arbitrary",)),
    )(x)


def _rmsnorm_fwd(h, w, name):
    T, D = h.shape
    tm = _pick(T, 512, 8)

    def body(h_ref, w_ref, o_ref):
        x = h_ref[...]
        rstd = lax.rsqrt(jnp.mean(x * x, axis=-1, keepdims=True) + NORM_EPS)
        o_ref[...] = (x * rstd * w_ref[...]).astype(BF16)

    return pl.pallas_call(
        body, out_shape=SDS((T, D), BF16), grid=(T // tm,),
        in_specs=[pl.BlockSpec((tm, D), lambda i: (i, 0)), pl.BlockSpec((1, D), lambda i: (0, 0))],
        out_specs=pl.BlockSpec((tm, D), lambda i: (i, 0)), name=name, compiler_params=_params(("parallel",)),
    )(h, w.reshape(1, D))


def _rmsnorm_bwd(h, du, w, dres, name):
    T, D = h.shape
    tm = _pick(T, 512, 8)

    def body(h_ref, du_ref, w_ref, dres_ref, dh_ref, dw_ref):
        x = h_ref[...]
        du_ = du_ref[...].astype(F32)
        rstd = lax.rsqrt(jnp.mean(x * x, axis=-1, keepdims=True) + NORM_EPS)
        g = du_ * w_ref[...]
        dh_ref[...] = dres_ref[...] + rstd * g - x * (rstd * rstd * rstd) * jnp.mean(g * x, axis=-1, keepdims=True)
        dw = jnp.sum(du_ * x * rstd, axis=0, keepdims=True)

        @pl.when(pl.program_id(0) == 0)
        def _():
            dw_ref[...] = dw

        @pl.when(pl.program_id(0) > 0)
        def _():
            dw_ref[...] += dw

    row = pl.BlockSpec((tm, D), lambda i: (i, 0))
    vec = pl.BlockSpec((1, D), lambda i: (0, 0))
    return pl.pallas_call(
        body, out_shape=(SDS((T, D), F32), SDS((1, D), F32)), grid=(T // tm,),
        in_specs=[row, row, vec, row], out_specs=(row, vec), name=name, compiler_params=_params(("arbitrary",)),
    )(h, du, w.reshape(1, D), dres)


def _final_loss(h, target, w):
    T, D = h.shape
    tm = _pick(T, 512, 8)

    def body(h_ref, t_ref, w_ref, dh_ref, dw_ref, loss_ref):
        x = h_ref[...]
        rstd = lax.rsqrt(jnp.mean(x * x, axis=-1, keepdims=True) + NORM_EPS)
        xn = x * rstd
        err = xn * w_ref[...] - t_ref[...]
        part = 0.5 * jnp.sum(jnp.mean(err * err, axis=-1, keepdims=True), axis=0, keepdims=True)
        dy = err * (1.0 / D)
        g = dy * w_ref[...]
        dh_ref[...] = rstd * g - x * (rstd * rstd * rstd) * jnp.mean(g * x, axis=-1, keepdims=True)
        dw = jnp.sum(dy * xn, axis=0, keepdims=True)
        lp = jnp.broadcast_to(part, (1, LANES))

        @pl.when(pl.program_id(0) == 0)
        def _():
            dw_ref[...] = dw
            loss_ref[...] = lp

        @pl.when(pl.program_id(0) > 0)
        def _():
            dw_ref[...] += dw
            loss_ref[...] += lp

    row = pl.BlockSpec((tm, D), lambda i: (i, 0))
    vec = pl.BlockSpec((1, D), lambda i: (0, 0))
    return pl.pallas_call(
        body, out_shape=(SDS((T, D), F32), SDS((1, D), F32), SDS((1, LANES), F32)), grid=(T // tm,),
        in_specs=[row, row, vec], out_specs=(row, vec, pl.BlockSpec((1, LANES), lambda i: (0, 0))),
        name="final_loss", compiler_params=_params(("arbitrary",)),
    )(h, target, w.reshape(1, D))


def _band_mask(i_blk, max_dist, first_ok):
    qi = lax.broadcasted_iota(jnp.int32, (BLOCK, 2 * BLOCK), 0)
    kj = lax.broadcasted_iota(jnp.int32, (BLOCK, 2 * BLOCK), 1)
    dist = qi + BLOCK - kj
    ok = (dist >= 0) & (dist <= max_dist)
    return ok & ((kj >= BLOCK) | first_ok)


def _pair(t, i):
    return t[:, LANES * i:LANES * (i + 1)]


def _low_half(shape):
    return lax.broadcasted_iota(jnp.int32, shape, len(shape) - 1) < HEAD_DIM


def _stack_heads(t):
    lo = _low_half(t.shape)
    z = jnp.zeros_like(t)
    return jnp.concatenate([jnp.where(lo, t, z), jnp.where(lo, z, t)], axis=0)


def _swap_halves(t):
    return jnp.concatenate([t[:, HEAD_DIM:], t[:, :HEAD_DIM]], axis=1)


def _kv_operand(kv, kv_swapped, h0, n_kv, n_heads):
    R = n_heads // n_kv
    if R == 1:
        return _pair(kv, h0 // 2)
    assert kv.shape[1] == LANES and R % 2 == 0, "grouped queries: one 128-lane tile of kv heads, both heads of a pair in one group"
    g = h0 // R
    t, ts = _pair(kv, g // 2), _pair(kv_swapped, g // 2)
    lo = _low_half(t.shape)
    return jnp.where(lo, t, ts) if g % 2 == 0 else jnp.where(lo, ts, t)


def _lane_place(cols):
    m = cols[0].shape[0]
    lane = lax.broadcasted_iota(jnp.int32, (m, LANES), 1)
    out = jnp.zeros((m, LANES), F32)
    for h, c in enumerate(cols):
        out = jnp.where(lane == h, c, out)
    return out


def _attn_specs(B, S, d, C, n_heads, n_kv, q_col, k_col, v_col):
    kvw = n_kv * HEAD_DIM
    qw = n_heads * HEAD_DIM
    cq, ck = (C // qw if d > 1 else 0), (C // kvw if d > 1 else 0)
    q_spec = pl.BlockSpec((1, BLOCK, qw), lambda b, r, i: (b, i, r * cq + q_col // qw))
    kc = pl.BlockSpec((1, BLOCK, kvw), lambda b, r, i: (b, i, r * ck + k_col // kvw))
    kp = pl.BlockSpec((1, BLOCK, kvw), lambda b, r, i: (b, jnp.maximum(i - 1, 0), r * ck + k_col // kvw))
    vc = pl.BlockSpec((1, BLOCK, kvw), lambda b, r, i: (b, i, r * ck + v_col // kvw))
    vp = pl.BlockSpec((1, BLOCK, kvw), lambda b, r, i: (b, jnp.maximum(i - 1, 0), r * ck + v_col // kvw))
    return q_spec, kp, kc, vp, vc


def _attn_fwd(qkv, B, S, d, *, n_heads, n_kv, q_col, k_col, v_col, max_dist, sinks, name):
    C = qkv.shape[1]
    Ls = S // d
    nb = Ls // BLOCK
    qw = n_heads * HEAD_DIM
    R = n_heads // n_kv
    qkv3 = qkv.reshape(B, Ls, d * C)
    scale = HEAD_DIM ** -0.5

    def body(*refs):
        if sinks is not None:
            sink_ref, q_ref, kp_ref, kc_ref, vp_ref, vc_ref, o_ref, lse_ref = refs
        else:
            q_ref, kp_ref, kc_ref, vp_ref, vc_ref, o_ref, lse_ref = refs
        i = pl.program_id(2)
        mask1 = _band_mask(i, max_dist, i > 0)
        mask = jnp.concatenate([mask1, mask1], axis=0)
        q = q_ref[0]
        kk = jnp.concatenate([kp_ref[0], kc_ref[0]], axis=0)
        vv = jnp.concatenate([vp_ref[0], vc_ref[0]], axis=0)
        kks, vvs = (_swap_halves(kk), _swap_halves(vv)) if R > 1 else (None, None)
        lo = _low_half((BLOCK, LANES))
        top = lax.broadcasted_iota(jnp.int32, (2 * BLOCK, 1), 0) < BLOCK
        lses, tiles = [], []
        for t in range(n_heads // 2):
            k2 = _kv_operand(kk, kks, 2 * t, n_kv, n_heads)
            v2 = _kv_operand(vv, vvs, 2 * t, n_kv, n_heads)
            s = lax.dot_general(_stack_heads(_pair(q, t)), k2, NT, preferred_element_type=F32) * scale
            s = jnp.where(mask, s, NEG)
            m = jnp.max(s, axis=-1, keepdims=True)
            if sinks is not None:
                sk = jnp.where(top, sink_ref[2 * t], sink_ref[2 * t + 1])
                m = jnp.maximum(m, sk)
            p = jnp.exp(s - m)
            den = jnp.sum(p, axis=-1, keepdims=True)
            if sinks is not None:
                den = den + jnp.exp(sk - m)
            lse2 = m + jnp.log(den)
            o2 = jnp.dot((p / den).astype(BF16), v2, preferred_element_type=F32)
            tiles.append(jnp.where(lo, o2[:BLOCK], o2[BLOCK:]))
            lses += [lse2[:BLOCK], lse2[BLOCK:]]
        o_ref[0] = jnp.concatenate(tiles, axis=-1)
        lse_ref[0] = _lane_place(lses)

    specs = list(_attn_specs(B, S, d, C, n_heads, n_kv, q_col, k_col, v_col))
    args = [qkv3] * 5
    if sinks is not None:
        specs = [pl.BlockSpec(memory_space=pltpu.SMEM)] + specs
        args = [sinks] + args
    o3, lse3 = pl.pallas_call(
        body, out_shape=(SDS((B, Ls, d * qw), F32), SDS((B, Ls, d * LANES), F32)), grid=(B, d, nb), in_specs=specs,
        out_specs=(pl.BlockSpec((1, BLOCK, qw), lambda b, r, i: (b, i, r)), pl.BlockSpec((1, BLOCK, LANES), lambda b, r, i: (b, i, r))),
        name=name, compiler_params=_params(("parallel", "parallel", "parallel")),
    )(*args)
    return o3.reshape(B * S, qw), lse3.reshape(B * S, LANES)


def _attn_dq(qkv, do, lse, delta, cos, sin, B, S, d, *, n_heads, n_kv, q_col, k_col, v_col, max_dist, name):
    C = qkv.shape[1]
    Ls = S // d
    nb = Ls // BLOCK
    qw = n_heads * HEAD_DIM
    R = n_heads // n_kv
    scale = HEAD_DIM ** -0.5

    def body(q_ref, kp_ref, kc_ref, vp_ref, vc_ref, do_ref, lse_ref, dl_ref, cos_ref, sin_ref, dq_ref):
        i = pl.program_id(2)
        mask1 = _band_mask(i, max_dist, i > 0)
        mask = jnp.concatenate([mask1, mask1], axis=0)
        q = q_ref[0]
        do_ = do_ref[0]
        kk = jnp.concatenate([kp_ref[0], kc_ref[0]], axis=0)
        vv = jnp.concatenate([vp_ref[0], vc_ref[0]], axis=0)
        kks, vvs = (_swap_halves(kk), _swap_halves(vv)) if R > 1 else (None, None)
        lo = _low_half((BLOCK, LANES))
        lse_t, dl_t = lse_ref[0], dl_ref[0]
        tiles = []
        for t in range(n_heads // 2):
            k2 = _kv_operand(kk, kks, 2 * t, n_kv, n_heads)
            v2 = _kv_operand(vv, vvs, 2 * t, n_kv, n_heads)
            lse2 = jnp.concatenate([lse_t[:, 2 * t:2 * t + 1], lse_t[:, 2 * t + 1:2 * t + 2]], axis=0)
            dl2 = jnp.concatenate([dl_t[:, 2 * t:2 * t + 1], dl_t[:, 2 * t + 1:2 * t + 2]], axis=0)
            s = lax.dot_general(_stack_heads(_pair(q, t)), k2, NT, preferred_element_type=F32) * scale
            p = jnp.where(mask, jnp.exp(s - lse2), 0.0)
            dp = lax.dot_general(_stack_heads(_pair(do_, t)), v2, NT, preferred_element_type=F32)
            ds = p * (dp - dl2)
            dq2 = jnp.dot(ds.astype(BF16), k2, preferred_element_type=F32) * scale
            tiles.append(jnp.where(lo, dq2[:BLOCK], dq2[BLOCK:]))
        dq = jnp.concatenate(tiles, axis=-1)
        dq_ref[0] = _rope(dq, cos_ref[0], sin_ref[0], -1.0).astype(BF16)

    qs, kp, kc, vp, vc = _attn_specs(B, S, d, C, n_heads, n_kv, q_col, k_col, v_col)
    row_q = pl.BlockSpec((1, BLOCK, qw), lambda b, r, i: (b, i, r))
    row_l = pl.BlockSpec((1, BLOCK, LANES), lambda b, r, i: (b, i, r))
    qkv3 = qkv.reshape(B, Ls, d * C)
    v3 = lambda t, w: t.reshape(B, Ls, d * w)
    dq3 = pl.pallas_call(
        body, out_shape=SDS((B, Ls, d * qw), BF16), grid=(B, d, nb),
        in_specs=[qs, kp, kc, vp, vc, row_q, row_l, row_l, row_l, row_l], out_specs=row_q,
        name=name, compiler_params=_params(("parallel", "parallel", "parallel")),
    )(qkv3, qkv3, qkv3, qkv3, qkv3, v3(do, qw), v3(lse, LANES), v3(delta, LANES), v3(cos, LANES), v3(sin, LANES))
    return dq3.reshape(B * S, qw)


def _attn_dkv(qkv, do, lse, delta, cos, sin, B, S, d, *, n_heads, n_kv, q_col, k_col, v_col, max_dist, name):
    C = qkv.shape[1]
    Ls = S // d
    nb = Ls // BLOCK
    qw = n_heads * HEAD_DIM
    kvw = n_kv * HEAD_DIM
    R = n_heads // n_kv
    scale = HEAD_DIM ** -0.5
    cq, ck = (C // qw if d > 1 else 0), (C // kvw if d > 1 else 0)

    def body(k_ref, v_ref, q0_ref, q1_ref, do0_ref, do1_ref, lse0_ref, lse1_ref, dl0_ref, dl1_ref, cos_ref, sin_ref,
             dk_ref, dv_ref):
        j = pl.program_id(2)
        kj = lax.broadcasted_iota(jnp.int32, (BLOCK, BLOCK), 0)
        qi = lax.broadcasted_iota(jnp.int32, (BLOCK, BLOCK), 1)
        dist0 = qi - kj
        dist1 = qi + BLOCK - kj
        mask0 = (dist0 >= 0) & (dist0 <= max_dist)
        mask1 = (dist1 <= max_dist) & (j + 1 < nb)
        kb, vb = k_ref[0], v_ref[0]
        kbs, vbs = (_swap_halves(kb), _swap_halves(vb)) if R > 1 else (None, None)
        sides = ((q0_ref[0], do0_ref[0], lse0_ref[0].T, dl0_ref[0].T, mask0), (q1_ref[0], do1_ref[0], lse1_ref[0].T, dl1_ref[0].T, mask1))
        n_acc = n_kv if R > 1 else n_kv // 2
        dks = [jnp.zeros((BLOCK, LANES), F32) for _ in range(n_acc)]
        dvs = [jnp.zeros((BLOCK, LANES), F32) for _ in range(n_acc)]
        for t in range(n_heads // 2):
            k2 = _kv_operand(kb, kbs, 2 * t, n_kv, n_heads)
            v2 = _kv_operand(vb, vbs, 2 * t, n_kv, n_heads)
            a = (2 * t) // R if R > 1 else t
            for (q, do_, lse_r, dl_r, mask) in sides:
                q2, do2 = _stack_heads(_pair(q, t)), _stack_heads(_pair(do_, t))
                s = lax.dot_general(k2, q2, NT, preferred_element_type=F32) * scale
                dp = lax.dot_general(v2, do2, NT, preferred_element_type=F32)
                ps, dss = [], []
                for half in (0, 1):
                    h = 2 * t + half
                    sl = slice(BLOCK * half, BLOCK * (half + 1))
                    p = jnp.where(mask, jnp.exp(s[:, sl] - lse_r[h:h + 1, :]), 0.0)
                    ps.append(p)
                    dss.append(p * (dp[:, sl] - dl_r[h:h + 1, :]))
                dvs[a] = dvs[a] + jnp.dot(jnp.concatenate(ps, axis=1).astype(BF16), do2, preferred_element_type=F32)
                dks[a] = dks[a] + jnp.dot(jnp.concatenate(dss, axis=1).astype(BF16), q2, preferred_element_type=F32)
        if R > 1:
            lo = _low_half((BLOCK, LANES))
            fold = lambda x: x + pltpu.roll(x, HEAD_DIM, 1)
            dks = [jnp.where(lo, fold(dks[2 * t]), fold(dks[2 * t + 1])) for t in range(n_kv // 2)]
            dvs = [jnp.where(lo, fold(dvs[2 * t]), fold(dvs[2 * t + 1])) for t in range(n_kv // 2)]
        dk_t = jnp.concatenate(dks, axis=-1) * scale
        dk_ref[0] = _rope(dk_t, cos_ref[0], sin_ref[0], -1.0).astype(BF16)
        dv_ref[0] = jnp.concatenate(dvs, axis=-1).astype(BF16)

    nxt = lambda j: jnp.minimum(j + 1, nb - 1)
    k_spec = pl.BlockSpec((1, BLOCK, kvw), lambda b, r, j: (b, j, r * ck + k_col // kvw))
    v_spec = pl.BlockSpec((1, BLOCK, kvw), lambda b, r, j: (b, j, r * ck + v_col // kvw))
    q0 = pl.BlockSpec((1, BLOCK, qw), lambda b, r, j: (b, j, r * cq + q_col // qw))
    q1 = pl.BlockSpec((1, BLOCK, qw), lambda b, r, j: (b, nxt(j), r * cq + q_col // qw))
    w0 = lambda w: pl.BlockSpec((1, BLOCK, w), lambda b, r, j: (b, j, r))
    w1 = lambda w: pl.BlockSpec((1, BLOCK, w), lambda b, r, j: (b, nxt(j), r))
    qkv3 = qkv.reshape(B, Ls, d * C)
    v3 = lambda t, w: t.reshape(B, Ls, d * w)
    do3, lse3, dl3 = v3(do, qw), v3(lse, LANES), v3(delta, LANES)
    dk3, dv3 = pl.pallas_call(
        body, out_shape=(SDS((B, Ls, d * kvw), BF16), SDS((B, Ls, d * kvw), BF16)), grid=(B, d, nb),
        in_specs=[k_spec, v_spec, q0, q1, w0(qw), w1(qw), w0(LANES), w1(LANES), w0(LANES), w1(LANES), w0(LANES), w0(LANES)],
        out_specs=(w0(kvw), w0(kvw)), name=name, compiler_params=_params(("parallel", "parallel", "parallel")),
    )(qkv3, qkv3, qkv3, qkv3, do3, do3, lse3, lse3, dl3, dl3, v3(cos, LANES), v3(sin, LANES))
    return dk3.reshape(B * S, kvw), dv3.reshape(B * S, kvw)


def _head_expand():
    r = lax.broadcasted_iota(jnp.int32, (LANES, C_HEADS * HEAD_DIM), 0)
    c = lax.broadcasted_iota(jnp.int32, (LANES, C_HEADS * HEAD_DIM), 1)
    return jnp.where(c // HEAD_DIM == r, 1.0, 0.0).astype(F32)


def _delta(do, o, lse=None, sinks_row=None, name="delta"):
    T, W = do.shape
    tm = _pick(T, 512, 8)
    with_sink = sinks_row is not None

    def body(*refs):
        if with_sink:
            do_ref, o_ref, lse_ref, sk_ref, dl_ref, dob_ref, ds_ref = refs
        else:
            do_ref, o_ref, dl_ref, dob_ref = refs
        do_ = do_ref[...]
        dl = lax.dot_general(do_ * o_ref[...], _head_expand(), NT, preferred_element_type=F32, precision=HI)
        dl_ref[...] = dl
        dob_ref[...] = do_.astype(BF16)
        if with_sink:
            lane = lax.broadcasted_iota(jnp.int32, dl.shape, 1)
            contrib = jnp.where(lane < A_N_HEADS, -jnp.exp(sk_ref[...] - lse_ref[...]) * dl, 0.0)
            part = jnp.sum(contrib, axis=0, keepdims=True)

            @pl.when(pl.program_id(0) == 0)
            def _():
                ds_ref[...] = part

            @pl.when(pl.program_id(0) > 0)
            def _():
                ds_ref[...] += part

    row_w = pl.BlockSpec((tm, W), lambda i: (i, 0))
    row_l = pl.BlockSpec((tm, LANES), lambda i: (i, 0))
    vec_l = pl.BlockSpec((1, LANES), lambda i: (0, 0))
    if with_sink:
        return pl.pallas_call(
            body, out_shape=(SDS((T, LANES), F32), SDS((T, W), BF16), SDS((1, LANES), F32)), grid=(T // tm,),
            in_specs=[row_w, row_w, row_l, vec_l], out_specs=(row_l, row_w, vec_l), name=name,
            compiler_params=_params(("arbitrary",)),
        )(do, o, lse, sinks_row)
    return pl.pallas_call(
        body, out_shape=(SDS((T, LANES), F32), SDS((T, W), BF16)), grid=(T // tm,),
        in_specs=[row_w, row_w], out_specs=(row_l, row_w), name=name, compiler_params=_params(("parallel",)),
    )(do, o)


def _merge(os_, lses):
    T, W = os_[0].shape
    tm = _pick(T, 512, 8)

    def body(o0, o1, o2, l0, l1, l2, o_ref, lse_ref):
        ls = [l0[...], l1[...], l2[...]]
        m = jnp.maximum(jnp.maximum(ls[0], ls[1]), ls[2])
        ws = [jnp.exp(l - m) for l in ls]
        tot = ws[0] + ws[1] + ws[2]
        lse_ref[...] = m + jnp.log(tot)
        e = _head_expand()
        acc = jnp.zeros((tm, W), F32)
        for w, o in zip(ws, (o0, o1, o2)):
            acc = acc + jnp.dot(w / tot, e, preferred_element_type=F32, precision=HI) * o[...]
        o_ref[...] = acc

    row_w = pl.BlockSpec((tm, W), lambda i: (i, 0))
    row_l = pl.BlockSpec((tm, LANES), lambda i: (i, 0))
    return pl.pallas_call(
        body, out_shape=(SDS((T, W), F32), SDS((T, LANES), F32)), grid=(T // tm,),
        in_specs=[row_w] * 3 + [row_l] * 3, out_specs=(row_w, row_l), name="c_merge", compiler_params=_params(("parallel",)),
    )(*os_, *lses)


CONV_TC = 256


def _conv_pre(x, w, bias):
    row = lax.broadcasted_iota(jnp.int32, x.shape, 0)
    acc = x * w[SSM_CONV - 1:SSM_CONV, :] + bias
    for k in range(1, SSM_CONV):
        acc = acc + jnp.where(row >= k, pltpu.roll(x, k, 0), 0.0) * w[SSM_CONV - 1 - k:SSM_CONV - k, :]
    return acc


def _conv_fwd(zx3, w, bias):
    B, S, _ = zx3.shape
    off = SSM_D_INNER // CONV_TC

    def body(x_ref, w_ref, b_ref, o_ref):
        v = _conv_pre(x_ref[0], w_ref[...], b_ref[...])
        o_ref[0] = v * jax.nn.sigmoid(v)

    return pl.pallas_call(
        body, out_shape=SDS((B, S, SSM_CONV_DIM), F32), grid=(B, SSM_CONV_DIM // CONV_TC),
        in_specs=[pl.BlockSpec((1, S, CONV_TC), lambda b, j: (b, 0, j + off)),
                  pl.BlockSpec((SSM_CONV, CONV_TC), lambda b, j: (0, j)), pl.BlockSpec((1, CONV_TC), lambda b, j: (0, j))],
        out_specs=pl.BlockSpec((1, S, CONV_TC), lambda b, j: (b, 0, j)), name="b_conv_fwd",
        compiler_params=_params(("parallel", "parallel")),
    )(zx3, w, bias)


def _conv_bwd(zx3, dxc, w, bias, col0, name):
    B, S, n = dxc.shape
    tc = _pick(n, CONV_TC)
    off_x = (SSM_D_INNER + col0) // tc
    off_w = col0 // tc

    def body(x_ref, d_ref, w_ref, b_ref, dx_ref, dw_ref, db_ref):
        x = x_ref[0]
        wv = w_ref[...]
        v = _conv_pre(x, wv, b_ref[...])
        sg = jax.nn.sigmoid(v)
        dc = d_ref[0] * (sg * (1.0 + v * (1.0 - sg)))
        row = lax.broadcasted_iota(jnp.int32, x.shape, 0)
        dx = dc * wv[SSM_CONV - 1:SSM_CONV, :]
        dws = [jnp.sum(dc * x, axis=0, keepdims=True)]
        for k in range(1, SSM_CONV):
            dx = dx + jnp.where(row < S - k, pltpu.roll(dc, S - k, 0), 0.0) * wv[SSM_CONV - 1 - k:SSM_CONV - k, :]
            dws.append(jnp.sum(dc * jnp.where(row >= k, pltpu.roll(x, k, 0), 0.0), axis=0, keepdims=True))
        dx_ref[0] = dx.astype(BF16)
        ridx = lax.broadcasted_iota(jnp.int32, (SSM_CONV, tc), 0)
        dw = jnp.zeros((SSM_CONV, tc), F32)
        for k in range(SSM_CONV):
            dw = jnp.where(ridx == SSM_CONV - 1 - k, dws[k], dw)
        db = jnp.sum(dc, axis=0, keepdims=True)

        @pl.when(pl.program_id(1) == 0)
        def _():
            dw_ref[...] = dw
            db_ref[...] = db

        @pl.when(pl.program_id(1) > 0)
        def _():
            dw_ref[...] += dw
            db_ref[...] += db

    return pl.pallas_call(
        body, out_shape=(SDS((B, S, n), BF16), SDS((SSM_CONV, n), F32), SDS((1, n), F32)), grid=(n // tc, B),
        in_specs=[pl.BlockSpec((1, S, tc), lambda j, b: (b, 0, j + off_x)), pl.BlockSpec((1, S, tc), lambda j, b: (b, 0, j)),
                  pl.BlockSpec((SSM_CONV, tc), lambda j, b: (0, j + off_w)), pl.BlockSpec((1, tc), lambda j, b: (0, j + off_w))],
        out_specs=(pl.BlockSpec((1, S, tc), lambda j, b: (b, 0, j)), pl.BlockSpec((SSM_CONV, tc), lambda j, b: (0, j)),
                   pl.BlockSpec((1, tc), lambda j, b: (0, j))),
        name=name, compiler_params=_params(("parallel", "arbitrary")),
    )(zx3, dxc, w, bias)


def _ssd_common(x, Bm, Cm, dtc_raw, dtr_raw, pr, pc):
    Q = SSM_CHUNK
    zc = dtc_raw + pr[0:1, :]
    dt_c = jax.nn.softplus(zc)
    dt_r = jax.nn.softplus(dtr_raw + pc[:, 0:1])
    A_r = -jnp.exp(pr[1:2, :])
    A_c = -jnp.exp(pc[:, 1:2])
    row = lax.broadcasted_iota(jnp.int32, (Q, Q), 0)
    col = lax.broadcasted_iota(jnp.int32, (Q, Q), 1)
    tril = jnp.where(row >= col, 1.0, 0.0).astype(F32)
    cs_c = jnp.dot(tril, dt_c * A_r, preferred_element_type=F32, precision=HI)
    cs_r = lax.dot_general(dt_r * A_c, tril, NT, preferred_element_type=F32, precision=HI)
    return zc, dt_c, A_r, cs_c, cs_r, row, col, tril


def _ssd_fwd(xc3, dtc, dtr, prow, pcol):
    B, S, _ = xc3.shape
    Q, G, HG, P, N = SSM_CHUNK, SSM_N_GROUPS, SSM_HG, HEAD_DIM, SSM_D_STATE
    nc = S // Q
    xw = HG * P

    def body(x_ref, b_ref, c_ref, dtc_ref, dtr_ref, pr_ref, pc_ref, y_ref, st_ref, state):
        c = pl.program_id(2)

        @pl.when(c == 0)
        def _():
            state[...] = jnp.zeros_like(state)

        x, Bm, Cm = x_ref[0], b_ref[0], c_ref[0]
        pr = pr_ref[0]
        _, dt_c, _, cs_c, cs_r, row, col, _ = _ssd_common(x, Bm, Cm, dtc_ref[0, 0], dtr_ref[0, 0], pr, pc_ref[0])
        Bb, Cb = Bm.astype(BF16), Cm.astype(BF16)
        CB = lax.dot_general(Cb, Bb, NT, preferred_element_type=F32)
        ys = []
        for hg in range(HG):
            xh = x[:, P * hg:P * (hg + 1)]
            xt = xh * dt_c[:, hg:hg + 1]
            csc, csr = cs_c[:, hg:hg + 1], cs_r[hg:hg + 1, :]
            L = jnp.where(row >= col, jnp.exp(jnp.minimum(csc - csr, 0.0)), 0.0)
            ydiag = jnp.dot((CB * L).astype(BF16), xt.astype(BF16), preferred_element_type=F32)
            Sh = state[hg]
            yoff = lax.dot_general(Cb, Sh.astype(BF16), NT, preferred_element_type=F32) * jnp.exp(csc)
            ys.append(ydiag + yoff + pr[2:3, hg:hg + 1] * xh)
            st_ref[0, 0, 0, P * hg:P * (hg + 1), :] = Sh
            csq = csc[Q - 1:Q, :]
            upd = lax.dot_general((xt * jnp.exp(csq - csc)).astype(BF16), Bb, TN, preferred_element_type=F32)
            state[hg] = Sh * jnp.exp(csq) + upd
        y_ref[0] = jnp.concatenate([jnp.concatenate(ys[0:2], axis=-1), jnp.concatenate(ys[2:4], axis=-1)], axis=-1)

    xo, bo, co = 0, SSM_D_INNER // N, (SSM_D_INNER + SSM_BC_DIM) // N
    return pl.pallas_call(
        body, out_shape=(SDS((B, S, SSM_D_INNER), F32), SDS((B, G, nc, xw, N), F32)), grid=(G, B, nc),
        in_specs=[pl.BlockSpec((1, Q, xw), lambda g, b, c: (b, c, g)), pl.BlockSpec((1, Q, N), lambda g, b, c: (b, c, bo + g)),
                  pl.BlockSpec((1, Q, N), lambda g, b, c: (b, c, co + g)), pl.BlockSpec((1, 1, Q, HG), lambda g, b, c: (b, g, c, 0)),
                  pl.BlockSpec((1, 1, HG, Q), lambda g, b, c: (b, g, 0, c)), pl.BlockSpec((1, 3, HG), lambda g, b, c: (g, 0, 0)),
                  pl.BlockSpec((1, HG, 3), lambda g, b, c: (g, 0, 0))],
        out_specs=(pl.BlockSpec((1, Q, xw), lambda g, b, c: (b, c, g)), pl.BlockSpec((1, 1, 1, xw, N), lambda g, b, c: (b, g, c, 0, 0))),
        scratch_shapes=[pltpu.VMEM((HG, P, N), F32)], name="b_ssd_fwd",
        compiler_params=_params(("parallel", "arbitrary", "arbitrary")),
    )(xc3, xc3, xc3, dtc, dtr, prow, pcol)


def _ssd_bwd(xc3, dtc, dtr, prow, pcol, states, dy3):
    B, S, _ = xc3.shape
    Q, G, HG, P, N = SSM_CHUNK, SSM_N_GROUPS, SSM_HG, HEAD_DIM, SSM_D_STATE
    nc = S // Q
    xw = HG * P

    def body(x_ref, b_ref, c_ref, dtc_ref, dtr_ref, pr_ref, pc_ref, st_ref, dy_ref,
             dx_ref, db_ref, dc_ref, ddt_ref, dpar_ref, dstate):
        bi, ci = pl.program_id(1), pl.program_id(2)

        @pl.when(ci == 0)
        def _():
            dstate[...] = jnp.zeros_like(dstate)

        x, Bm, Cm, dy = x_ref[0], b_ref[0], c_ref[0], dy_ref[0]
        pr = pr_ref[0]
        zc, dt_c, A_r, cs_c, cs_r, row, col, tril = _ssd_common(x, Bm, Cm, dtc_ref[0, 0], dtr_ref[0, 0], pr, pc_ref[0])
        Bb, Cb = Bm.astype(BF16), Cm.astype(BF16)
        CB = lax.dot_general(Cb, Bb, NT, preferred_element_type=F32)
        CBt = lax.dot_general(Bb, Cb, NT, preferred_element_type=F32)
        lane4 = lax.broadcasted_iota(jnp.int32, (Q, HG), 1)
        lane4r = lax.broadcasted_iota(jnp.int32, (1, HG), 1)
        rowq = lax.broadcasted_iota(jnp.int32, (Q, 1), 0)
        dB = jnp.zeros((Q, N), F32)
        dC = jnp.zeros((Q, N), F32)
        dcs4 = jnp.zeros((Q, HG), F32)
        dtx4 = jnp.zeros((Q, HG), F32)
        dD4 = jnp.zeros((1, HG), F32)
        dxts, xhs, dyhs = [], [], []
        for hg in range(HG):
            xh = x[:, P * hg:P * (hg + 1)]
            dyh = dy[:, P * hg:P * (hg + 1)]
            xt = xh * dt_c[:, hg:hg + 1]
            xtb, dyb = xt.astype(BF16), dyh.astype(BF16)
            csc, csr = cs_c[:, hg:hg + 1], cs_r[hg:hg + 1, :]
            L = jnp.where(row >= col, jnp.exp(jnp.minimum(csc - csr, 0.0)), 0.0)
            Lt = jnp.where(col >= row, jnp.exp(jnp.minimum(csr - csc, 0.0)), 0.0)
            M, Mt = CB * L, CBt * Lt
            Sh = st_ref[0, 0, 0, P * hg:P * (hg + 1), :]
            dSh = dstate[hg]
            Shb, dShb = Sh.astype(BF16), dSh.astype(BF16)
            ecs = jnp.exp(csc)
            csq = csc[Q - 1:Q, :]
            dec = jnp.exp(csq - csc)
            dxt = jnp.dot(Mt.astype(BF16), dyb, preferred_element_type=F32)
            dxt = dxt + lax.dot_general(Bb, dShb, NT, preferred_element_type=F32) * dec
            Gm = lax.dot_general(dyb, xtb, NT, preferred_element_type=F32)
            Gt = lax.dot_general(xtb, dyb, NT, preferred_element_type=F32)
            dC = dC + jnp.dot((Gm * L).astype(BF16), Bb, preferred_element_type=F32)
            dB = dB + jnp.dot((Gt * Lt).astype(BF16), Cb, preferred_element_type=F32)
            dC = dC + jnp.dot(dyb, Shb, preferred_element_type=F32) * ecs
            dBst = jnp.dot(xtb, dShb, preferred_element_type=F32) * dec
            dB = dB + dBst
            dcs = jnp.sum(Gm * M, axis=1, keepdims=True) - jnp.sum(Gt * Mt, axis=1, keepdims=True)
            yoff = lax.dot_general(Cb, Shb, NT, preferred_element_type=F32) * ecs
            dcs = dcs + jnp.sum(yoff * dyh, axis=1, keepdims=True)
            r = jnp.sum(dBst * Bm, axis=1, keepdims=True)
            dcs = dcs - r
            extra = jnp.sum(r, axis=0, keepdims=True) + jnp.exp(csq) * jnp.sum(
                jnp.sum(dSh * Sh, axis=1, keepdims=True), axis=0, keepdims=True)
            dcs = dcs + jnp.where(rowq == Q - 1, extra, 0.0)
            dcs4 = jnp.where(lane4 == hg, dcs, dcs4)
            dtx4 = jnp.where(lane4 == hg, jnp.sum(dxt * xh, axis=1, keepdims=True), dtx4)
            dD4 = jnp.where(lane4r == hg, jnp.sum(jnp.sum(dyh * xh, axis=1, keepdims=True), axis=0, keepdims=True), dD4)
            dstate[hg] = dSh * jnp.exp(csq) + lax.dot_general((dyh * ecs).astype(BF16), Cb, TN, preferred_element_type=F32)
            dxts.append(dxt)
            xhs.append(xh)
            dyhs.append(dyh)
        da4 = lax.dot_general(tril, dcs4, TN, preferred_element_type=F32, precision=HI)
        ddt4 = da4 * A_r + dtx4
        ddtraw = ddt4 * jax.nn.sigmoid(zc)
        ddt_ref[0, 0] = ddtraw
        dxs = [dxts[hg] * dt_c[:, hg:hg + 1] + pr[2:3, hg:hg + 1] * dyhs[hg] for hg in range(HG)]
        dx_ref[0] = jnp.concatenate([jnp.concatenate(dxs[0:2], axis=-1), jnp.concatenate(dxs[2:4], axis=-1)], axis=-1)
        db_ref[0] = dB
        dc_ref[0] = dC
        d_bias = jnp.sum(ddtraw, axis=0, keepdims=True)
        d_alog = jnp.sum(da4 * dt_c, axis=0, keepdims=True) * A_r
        r3 = lax.broadcasted_iota(jnp.int32, (3, HG), 0)
        dpar = jnp.where(r3 == 0, d_bias, jnp.where(r3 == 1, d_alog, dD4))
        first = (bi == 0) & (ci == 0)

        @pl.when(first)
        def _():
            dpar_ref[0] = dpar

        @pl.when(jnp.logical_not(first))
        def _():
            dpar_ref[0] += dpar

    rc = lambda c: nc - 1 - c
    bo, co = SSM_D_INNER // N, (SSM_D_INNER + SSM_BC_DIM) // N
    return pl.pallas_call(
        body,
        out_shape=(SDS((B, S, SSM_D_INNER), F32), SDS((B, S, SSM_BC_DIM), F32), SDS((B, S, SSM_BC_DIM), F32),
                   SDS((B, G, S, HG), F32), SDS((G, 3, HG), F32)),
        grid=(G, B, nc),
        in_specs=[pl.BlockSpec((1, Q, xw), lambda g, b, c: (b, rc(c), g)), pl.BlockSpec((1, Q, N), lambda g, b, c: (b, rc(c), bo + g)),
                  pl.BlockSpec((1, Q, N), lambda g, b, c: (b, rc(c), co + g)), pl.BlockSpec((1, 1, Q, HG), lambda g, b, c: (b, g, rc(c), 0)),
                  pl.BlockSpec((1, 1, HG, Q), lambda g, b, c: (b, g, 0, rc(c))), pl.BlockSpec((1, 3, HG), lambda g, b, c: (g, 0, 0)),
                  pl.BlockSpec((1, HG, 3), lambda g, b, c: (g, 0, 0)),
                  pl.BlockSpec((1, 1, 1, xw, N), lambda g, b, c: (b, g, rc(c), 0, 0)), pl.BlockSpec((1, Q, xw), lambda g, b, c: (b, rc(c), g))],
        out_specs=(pl.BlockSpec((1, Q, xw), lambda g, b, c: (b, rc(c), g)), pl.BlockSpec((1, Q, N), lambda g, b, c: (b, rc(c), g)),
                   pl.BlockSpec((1, Q, N), lambda g, b, c: (b, rc(c), g)), pl.BlockSpec((1, 1, Q, HG), lambda g, b, c: (b, g, rc(c), 0)),
                   pl.BlockSpec((1, 3, HG), lambda g, b, c: (g, 0, 0))),
        scratch_shapes=[pltpu.VMEM((HG, P, N), F32)], name="b_ssd_bwd",
        compiler_params=_params(("parallel", "arbitrary", "arbitrary")),
    )(xc3, xc3, xc3, dtc, dtr, prow, pcol, states, dy3)


GN_W = SSM_D_INNER // SSM_N_GROUPS


def _gate_fwd(y, zx, nw):
    T = y.shape[0]
    tm = _pick(T, 256, 8)

    def body(y_ref, z_ref, w_ref, o_ref):
        z = z_ref[...]
        gt = y_ref[...] * (z * jax.nn.sigmoid(z))
        outs = []
        for k in range(SSM_N_GROUPS):
            gk = gt[:, GN_W * k:GN_W * (k + 1)]
            outs.append(gk * lax.rsqrt(jnp.mean(gk * gk, axis=-1, keepdims=True) + NORM_EPS))
        o_ref[...] = (jnp.concatenate(outs, axis=-1) * w_ref[...]).astype(BF16)

    row = pl.BlockSpec((tm, SSM_D_INNER), lambda i: (i, 0))
    return pl.pallas_call(
        body, out_shape=SDS((T, SSM_D_INNER), BF16), grid=(T // tm,),
        in_specs=[row, row, pl.BlockSpec((1, SSM_D_INNER), lambda i: (0, 0))], out_specs=row, name="b_gate_fwd",
        compiler_params=_params(("parallel",)),
    )(y, zx, nw)


def _gate_bwd(dgn, y, zx, nw):
    T = y.shape[0]
    tm = _pick(T, 256, 8)

    def body(d_ref, y_ref, z_ref, w_ref, dy_ref, dz_ref, dw_ref):
        z, yv, w = z_ref[...], y_ref[...], w_ref[...]
        sg = jax.nn.sigmoid(z)
        sz = z * sg
        gt = yv * sz
        gw = d_ref[...] * w
        dgts, dws = [], []
        for k in range(SSM_N_GROUPS):
            sl = slice(GN_W * k, GN_W * (k + 1))
            gk, gwk = gt[:, sl], gw[:, sl]
            rstd = lax.rsqrt(jnp.mean(gk * gk, axis=-1, keepdims=True) + NORM_EPS)
            dgts.append(rstd * gwk - gk * (rstd * rstd * rstd) * jnp.mean(gwk * gk, axis=-1, keepdims=True))
            dws.append(jnp.sum(d_ref[:, sl] * gk * rstd, axis=0, keepdims=True))
        dgt = jnp.concatenate(dgts, axis=-1)
        dy_ref[...] = dgt * sz
        dz_ref[...] = (dgt * yv * (sg * (1.0 + z * (1.0 - sg)))).astype(BF16)
        dw = jnp.concatenate(dws, axis=-1)

        @pl.when(pl.program_id(0) == 0)
        def _():
            dw_ref[...] = dw

        @pl.when(pl.program_id(0) > 0)
        def _():
            dw_ref[...] += dw

    row = pl.BlockSpec((tm, SSM_D_INNER), lambda i: (i, 0))
    vec = pl.BlockSpec((1, SSM_D_INNER), lambda i: (0, 0))
    return pl.pallas_call(
        body, out_shape=(SDS((T, SSM_D_INNER), F32), SDS((T, SSM_D_INNER), BF16), SDS((1, SSM_D_INNER), F32)), grid=(T // tm,),
        in_specs=[row, row, row, vec], out_specs=(row, row, vec), name="b_gate_bwd", compiler_params=_params(("arbitrary",)),
    )(dgn, y, zx, nw)


MESH = pl.DeviceIdType.MESH
ANY = pl.BlockSpec(memory_space=pl.ANY)


N_CHIPS = 4


def _dev_block(ref, kind, j, size):
    if kind == "slot":
        return ref.at[j]
    start = pl.multiple_of(j * size, size)
    nd = len(ref.shape)
    if kind == "col":
        return ref.at[(slice(None),) * (nd - 1) + (pl.ds(start, size),)]
    return ref.at[(slice(None),) * (nd - 2) + (pl.ds(start, size), slice(None))]


def _dma_sems(n, k):
    return [pltpu.SemaphoreType.DMA((n, k)), pltpu.SemaphoreType.DMA((n, k)), pltpu.SemaphoreType.DMA((n, k))]


def _place(shard, layer, kind, full_shape, dev, name):
    k, n = shard.shape[1:]
    tr = _pick(k, 512, 16)
    nb = k // tr

    def body(dev_ref, s_ref, o_ref):
        if kind == "slot":
            o_ref[0] = s_ref[0].astype(BF16)
        else:
            o_ref[...] = s_ref[0].astype(BF16)

    out_spec = {"slot": pl.BlockSpec((1, tr, n), lambda i, d: (d[0], i, 0)),
                "row": pl.BlockSpec((tr, n), lambda i, d: (d[0] * nb + i, 0)),
                "col": pl.BlockSpec((tr, n), lambda i, d: (i, d[0]))}[kind]
    return pl.pallas_call(
        body, out_shape=SDS(full_shape, BF16),
        grid_spec=pltpu.PrefetchScalarGridSpec(
            num_scalar_prefetch=1, grid=(nb,), in_specs=[pl.BlockSpec((1, tr, n), lambda i, d: (layer, i, 0))], out_specs=out_spec),
        name=name, compiler_params=_params(("arbitrary",)),
    )(dev, shard)


def _gather(items, name):
    n = len(items)

    def body(*refs):
        srcs, dsts = refs[:n], refs[n:2 * n]
        send_sems, recv_sems, local_sems = refs[2 * n:]
        px, py, pc = lax.axis_index("x"), lax.axis_index("y"), lax.axis_index("c")
        me, sibling = (px, py, pc), (px, py, 1 - pc)
        chips = [(1 - px, py), (px, 1 - py), (1 - px, 1 - py)]

        def blk(a, dev):
            return _dev_block(dsts[a], items[a][1], 4 * dev[0] + 2 * dev[1] + dev[2], items[a][2])

        def src_of(a):
            return blk(a, me) if items[a][4] else srcs[a]

        def copy(a, k, block, to, src=None):
            return pltpu.make_async_remote_copy(
                src_ref=blk(a, block) if src is None else src, dst_ref=blk(a, block),
                send_sem=send_sems.at[a, k], recv_sem=recv_sems.at[a, k], device_id=to, device_id_type=MESH)

        mine = [pltpu.make_async_copy(srcs[a], blk(a, me), local_sems.at[a, 0]) for a in range(n) if not items[a][4]]
        for cp in mine:
            cp.start()
        first = []
        for a in range(n):
            first.append(copy(a, 0, me, sibling, src=src_of(a)))
            first += [copy(a, 1 + j, me, (*chip, pc), src=src_of(a)) for j, chip in enumerate(chips)]
        for cp in first:
            cp.start()
        passed = []
        for j, chip in enumerate(chips):
            for a in range(n):
                copy(a, 1 + j, (*chip, pc), me).wait_recv()
                fwd = copy(a, 4 + j, (*chip, pc), sibling)
                fwd.start()
                passed.append(fwd)
        for a in range(n):
            copy(a, 0, sibling, me).wait_recv()
            for j, chip in enumerate(chips):
                copy(a, 4 + j, (*chip, 1 - pc), me).wait_recv()
        for cp in first + passed:
            cp.wait_send()
        for cp in mine:
            cp.wait()

    return pl.pallas_call(
        body, out_shape=[SDS(it[3], it[0].dtype) for it in items], in_specs=[ANY] * n, out_specs=[ANY] * n,
        input_output_aliases={a: a for a in range(n) if items[a][4]}, scratch_shapes=_dma_sems(n, 7), name=name,
    )(*[it[0] for it in items])


def _reduce_d2d(items, name):
    n = len(items)

    def body(*refs):
        gs, gots = refs[:n], refs[n:2 * n]
        send_sems, recv_sems, _ = refs[2 * n:]
        px, py, pc = lax.axis_index("x"), lax.axis_index("y"), lax.axis_index("c")
        copies = []
        for a in range(n):
            _, kind, size, _ = items[a]
            for q in range(N_CHIPS):
                copies.append(pltpu.make_async_remote_copy(
                    src_ref=_dev_block(gs[a], kind, 2 * q + 1 - pc, size), dst_ref=gots[a].at[q], send_sem=send_sems.at[a, q],
                    recv_sem=recv_sems.at[a, q], device_id=(px, py, 1 - pc), device_id_type=MESH))
        for cp in copies:
            cp.start()
        for cp in copies:
            cp.wait()

    return pl.pallas_call(
        body, out_shape=[SDS((N_CHIPS,) + tuple(it[3]), F32) for it in items], in_specs=[ANY] * n, out_specs=[ANY] * n,
        scratch_shapes=_dma_sems(n, N_CHIPS), name=name,
    )(*[it[0] for it in items])


def _pair_sum(g, got, kind, core, name):
    _, k, n = got.shape
    tr = _pick(k, max(16, STREAM_VMEM // (2 * n * 10)), 16)
    nb = k // tr

    def body(c_ref, g_ref, s_ref, o_ref):
        mine = g_ref[0] if kind == "slot" else g_ref[...]
        o_ref[0] = (mine + s_ref[0]).astype(BF16)

    g_spec = {"slot": pl.BlockSpec((1, tr, n), lambda q, i, c: (2 * q + c[0], i, 0)),
              "row": pl.BlockSpec((tr, n), lambda q, i, c: ((2 * q + c[0]) * nb + i, 0)),
              "col": pl.BlockSpec((tr, n), lambda q, i, c: (i, 2 * q + c[0]))}[kind]
    part = pl.BlockSpec((1, tr, n), lambda q, i, c: (q, i, 0))
    return pl.pallas_call(
        body, out_shape=SDS((N_CHIPS, k, n), BF16),
        grid_spec=pltpu.PrefetchScalarGridSpec(num_scalar_prefetch=1, grid=(N_CHIPS, nb), in_specs=[g_spec, part], out_specs=part),
        name=name, compiler_params=_params(("arbitrary", "arbitrary")),
    )(core, g, got)


def _reduce_ici(parts, name):
    n = len(parts)

    def body(*refs):
        ps, rs = refs[:n], refs[n:2 * n]
        send_sems, recv_sems, _ = refs[2 * n:]
        px, py, pc = lax.axis_index("x"), lax.axis_index("y"), lax.axis_index("c")
        my_chip = 2 * px + py
        sends, recvs = [], []
        for a in range(n):
            for k in range(1, N_CHIPS):
                qx, qy = px ^ (k >> 1), py ^ (k & 1)
                q = 2 * qx + qy
                kw = dict(send_sem=send_sems.at[a, k - 1], recv_sem=recv_sems.at[a, k - 1], device_id=(qx, qy, pc),
                          device_id_type=MESH)
                sends.append(pltpu.make_async_remote_copy(src_ref=ps[a].at[q], dst_ref=rs[a].at[my_chip], **kw))
                recvs.append(pltpu.make_async_remote_copy(src_ref=ps[a].at[q], dst_ref=rs[a].at[q], **kw))
        for cp in sends:
            cp.start()
        for cp in recvs:
            cp.wait_recv()
        for cp in sends:
            cp.wait_send()

    return pl.pallas_call(
        body, out_shape=[SDS(p.shape, p.dtype) for p in parts], in_specs=[ANY] * n, out_specs=[ANY] * n,
        scratch_shapes=_dma_sems(n, N_CHIPS - 1), name=name,
    )(*parts)


def _adam_update(g, w, m, v):
    c1 = 1.0 - ADAM_B1 ** ADAM_STEP
    c2 = 1.0 - ADAM_B2 ** ADAM_STEP
    nm = ADAM_B1 * m + (1.0 - ADAM_B1) * g
    nv = ADAM_B2 * v + (1.0 - ADAM_B2) * (g * g)
    delta = -ADAM_LR * ((nm / c1) / (jnp.sqrt(nv / c2) + ADAM_EPS) + ADAM_WD * w)
    return delta, nm, nv


def _adamw(parts, recv, w, m, v, chip, name):
    R, C = w.shape
    row_bytes = 2 * C * (N_CHIPS * 2 + 7 * 4)
    tr = _pick(R, max(16, STREAM_VMEM // row_bytes), 16)

    def body(ch_ref, own_ref, r1_ref, r2_ref, r3_ref, w_ref, m_ref, v_ref, g_ref, d_ref, nm_ref, nv_ref):
        g = own_ref[0].astype(F32)
        for r_ref in (r1_ref, r2_ref, r3_ref):
            g = g + r_ref[0].astype(F32)
        g_ref[...] = g
        d_ref[...], nm_ref[...], nv_ref[...] = _adam_update(g, w_ref[...], m_ref[...], v_ref[...])

    row = pl.BlockSpec((tr, C), lambda i, ch: (i, 0))
    other = lambda k: pl.BlockSpec((1, tr, C), lambda i, ch: (ch[0] ^ k, i, 0))
    out = SDS((R, C), F32)
    return pl.pallas_call(
        body, out_shape=(out, out, out, out),
        grid_spec=pltpu.PrefetchScalarGridSpec(
            num_scalar_prefetch=1, grid=(R // tr,),
            in_specs=[pl.BlockSpec((1, tr, C), lambda i, ch: (ch[0], i, 0)), other(2), other(1), other(3), row, row, row],
            out_specs=(row, row, row, row)),
        name=name, compiler_params=_params(("arbitrary",)),
    )(chip, parts, recv, recv, recv, w, m, v)


def _small_adamw(gathered, ws, ms, vs):
    n = len(ws)

    def body(*refs):
        g_in, w_in, m_in, v_in = refs[:n], refs[n:2 * n], refs[2 * n:3 * n], refs[3 * n:4 * n]
        outs = refs[4 * n:]
        for i in range(n):
            g = g_in[i][0]
            for dev in range(1, N_DEV):
                g = g + g_in[i][dev]
            d, nm, nv = _adam_update(g, w_in[i][...], m_in[i][...], v_in[i][...])
            outs[i][...] = g
            outs[n + i][...] = d
            outs[2 * n + i][...] = nm
            outs[3 * n + i][...] = nv

    shapes = [SDS(w.shape, F32) for w in ws]
    outs = pl.pallas_call(body, out_shape=shapes * 4, name="small_adamw")(*gathered, *ws, *ms, *vs)
    return outs[:n], outs[n:2 * n], outs[2 * n:3 * n], outs[3 * n:]


W_NAMES = ("norm_mix_w", "norm_mlp_w", "a_w_qkv", "a_b_qkv", "a_sinks", "a_w_o", "a_b_o", "b_in_w", "b_conv_w", "b_conv_b",
           "b_dt_bias", "b_a_log", "b_d", "b_norm_w", "b_out_w", "c_w_qkv", "c_w_o", "mlp_w_up", "mlp_w_down", "final_norm_w")
BIG_KIND = {"a_w_qkv": "slot", "a_w_o": "row", "b_in_w": "slot", "b_out_w": "row", "c_w_qkv": "col", "c_w_o": "row",
            "mlp_w_up": "col", "mlp_w_down": "row"}
SMALL_SHARDED = {"a_b_qkv": 1, "a_b_o": 1, "b_conv_w": 2}
SMALL_REPLICATED = ("norm_mix_w", "norm_mlp_w", "a_sinks", "b_conv_b", "b_dt_bias", "b_a_log", "b_d", "b_norm_w", "final_norm_w")


def _layer_big(i):
    kind, j = i % 3, i // 3
    mix = {0: [("a_w_qkv", j), ("a_w_o", j)], 1: [("b_in_w", 0), ("b_out_w", 0)], 2: [("c_w_qkv", 0), ("c_w_o", 0)]}[kind]
    return mix + [("mlp_w_up", i), ("mlp_w_down", i)]


def _block_size(kind, shard2d):
    return {"slot": None, "row": shard2d[0], "col": shard2d[1]}[kind]


def _full2d(kind, shard2d):
    k, n = shard2d
    return {"slot": (N_DEV, k, n), "row": (N_DEV * k, n), "col": (k, N_DEV * n)}[kind]


def _from_slots(t, ax):
    s = t.shape[1:]
    return jnp.moveaxis(t, 0, ax).reshape(s[:ax] + (N_DEV * s[ax],) + s[ax + 1:])


def _to_slots(g, ax):
    s = g.shape
    return jnp.moveaxis(g.reshape(s[:ax] + (N_DEV, s[ax] // N_DEV) + s[ax + 1:]), ax, 0)


def _rope_tables(positions):
    half = HEAD_DIM // 2
    inv = ROPE_THETA ** (-(jnp.arange(LANES, dtype=jnp.int32) % half).astype(F32) / half)
    ang = positions.astype(F32).reshape(-1, 1) * inv
    return jnp.cos(ang), jnp.sin(ang)


def _swa_fwd(u, h, p, j, B, S, cos, sin, tag):
    qkv = _matmul(u, p["a_w_qkv"][j], out_dtype=BF16, bias=p["a_b_qkv"][j][None], rope=(cos, sin),
                  rope_cols=A_Q_DIM + A_KV_DIM, tn=640, name=f"{tag}_qkv")
    o, lse = _attn_fwd(qkv, B, S, 1, n_heads=A_N_HEADS, n_kv=A_N_KV, q_col=0, k_col=A_Q_DIM, v_col=A_Q_DIM + A_KV_DIM,
                       max_dist=A_WINDOW - 1, sinks=p["a_sinks"][j], name=f"{tag}_attn")
    h1 = _matmul(o, p["a_w_o"][j], bias=p["a_b_o"][j][None], resid=h, name=f"{tag}_o")
    return h1, (qkv, o, lse)


def _swa_bwd(dh1, u, saved, p, j, B, S, cos, sin, tag):
    qkv, o, lse = saved
    kw = dict(n_heads=A_N_HEADS, n_kv=A_N_KV, q_col=0, k_col=A_Q_DIM, v_col=A_Q_DIM + A_KV_DIM, max_dist=A_WINDOW - 1)
    g = {}
    do = _matmul(dh1, p["a_w_o"][j], tb=True, name=f"{tag}_do")
    g["a_w_o"] = _matmul(o, dh1, ta=True, name=f"{tag}_dwo")
    g["a_b_o"] = _colsum(dh1, f"{tag}_dbo")[0]
    sk = jnp.pad(p["a_sinks"][j], (0, LANES - A_N_HEADS))[None]
    delta, dob, dsink = _delta(do, o, lse, sk, name=f"{tag}_delta")
    g["a_sinks"] = dsink[0, :A_N_HEADS]
    dq = _attn_dq(qkv, dob, lse, delta, cos, sin, B, S, 1, name=f"{tag}_dq", **kw)
    dk, dv = _attn_dkv(qkv, dob, lse, delta, cos, sin, B, S, 1, name=f"{tag}_dkv", **kw)
    dqkv = jnp.concatenate([dq, dk, dv], axis=1)
    g["a_w_qkv"] = _matmul(u, dqkv, ta=True, tn=640, name=f"{tag}_dwqkv")
    g["a_b_qkv"] = _colsum(dqkv, f"{tag}_dbqkv")[0]
    du = _matmul(dqkv, p["a_w_qkv"][j], tb=True, tk=640, name=f"{tag}_du")
    return du, g


def _group_cols(gi, qkv):
    W = C_HEADS * HEAD_DIM
    if C_PATTERNS[gi][1] == 1:
        return qkv, (gi * W, (3 + gi) * W, (6 + gi) * W)
    part = jnp.concatenate([qkv[:, (3 * j + gi) * W:(3 * j + gi + 1) * W] for j in range(3)], axis=1)
    return part, (0, W, 2 * W)


def _dil_fwd(u, h, p, B, S, cos, sin):
    W = C_HEADS * HEAD_DIM
    qkv = _matmul(u, p["c_w_qkv"][0], out_dtype=BF16, rope=(cos, sin), rope_cols=6 * W, name="c_qkv")
    os_, lses, parts = [], [], []
    for gi, (window, dil) in enumerate(C_PATTERNS):
        part, (qc, kc, vc) = _group_cols(gi, qkv)
        o, lse = _attn_fwd(part, B, S, dil, n_heads=C_HEADS, n_kv=C_HEADS, q_col=qc, k_col=kc, v_col=vc,
                           max_dist=window // dil, sinks=None, name=f"c_attn{gi}")
        os_.append(o)
        lses.append(lse)
        parts.append((part, (qc, kc, vc)))
    o, lse = _merge(os_, lses)
    h1 = _matmul(o, p["c_w_o"][0], resid=h, name="c_o")
    return h1, (parts, o, lse)


def _dil_bwd(dh1, u, saved, p, B, S, cos, sin):
    parts, o, lse = saved
    g = {}
    do = _matmul(dh1, p["c_w_o"][0], tb=True, name="c_do")
    g["c_w_o"] = _matmul(o, dh1, ta=True, name="c_dwo")[None]
    delta, dob = _delta(do, o, name="c_delta")
    dqs, dks, dvs = [], [], []
    for gi, (window, dil) in enumerate(C_PATTERNS):
        part, (qc, kc, vc) = parts[gi]
        kw = dict(n_heads=C_HEADS, n_kv=C_HEADS, q_col=qc, k_col=kc, v_col=vc, max_dist=window // dil)
        dqs.append(_attn_dq(part, dob, lse, delta, cos, sin, B, S, dil, name=f"c_dq{gi}", **kw))
        dk, dv = _attn_dkv(part, dob, lse, delta, cos, sin, B, S, dil, name=f"c_dkv{gi}", **kw)
        dks.append(dk)
        dvs.append(dv)
    dqkv = jnp.concatenate(dqs + dks + dvs, axis=1)
    g["c_w_qkv"] = _matmul(u, dqkv, ta=True, name="c_dwqkv")[None]
    du = _matmul(dqkv, p["c_w_qkv"][0], tb=True, name="c_du")
    return du, g


def _ssm_params(p):
    par = jnp.stack([p["b_dt_bias"][0], p["b_a_log"][0], p["b_d"][0]], axis=0)
    prow = par.reshape(3, SSM_N_GROUPS, SSM_HG).transpose(1, 0, 2)
    return prow, prow.transpose(0, 2, 1)


def _mamba_fwd(u, h, p, B, S):
    T = B * S
    G, HG = SSM_N_GROUPS, SSM_HG
    w_in = p["b_in_w"][0]
    nzx = SSM_D_INNER + SSM_CONV_DIM
    w_dt = jnp.pad(w_in[:, nzx:], ((0, 0), (0, LANES - SSM_N_HEADS)))
    zx = _matmul(u, w_in[:, :nzx], name="b_zx")
    dtraw = _matmul(u, w_dt, name="b_dt")[:, :SSM_N_HEADS]
    dtc = dtraw.reshape(B, S, G, HG).transpose(0, 2, 1, 3)
    dtr = dtraw.reshape(B, S, G, HG).transpose(0, 2, 3, 1)
    prow, pcol = _ssm_params(p)
    zx3 = zx.reshape(B, S, nzx)
    xc3 = _conv_fwd(zx3, p["b_conv_w"][0], p["b_conv_b"])
    y3, states = _ssd_fwd(xc3, dtc, dtr, prow, pcol)
    y = y3.reshape(T, SSM_D_INNER)
    gn = _gate_fwd(y, zx, p["b_norm_w"])
    h1 = _matmul(gn, p["b_out_w"][0], resid=h, name="b_out")
    return h1, (zx, dtc, dtr, xc3, y, states, gn, w_dt)


def _mamba_bwd(dh1, u, saved, p, B, S):
    T = B * S
    zx, dtc, dtr, xc3, y, states, gn, w_dt = saved
    nzx = SSM_D_INNER + SSM_CONV_DIM
    w_in = p["b_in_w"][0]
    prow, pcol = _ssm_params(p)
    g = {}
    dgn = _matmul(dh1, p["b_out_w"][0], tb=True, name="b_dgn")
    g["b_out_w"] = _matmul(gn, dh1, ta=True, name="b_dwout")[None]
    dy, dz, dnw = _gate_bwd(dgn, y, zx, p["b_norm_w"])
    g["b_norm_w"] = dnw
    dx3, dB3, dC3, ddt, dpar = _ssd_bwd(xc3, dtc, dtr, prow, pcol, states, dy.reshape(B, S, SSM_D_INNER))
    dpar = dpar.transpose(1, 0, 2).reshape(3, SSM_N_HEADS)
    g["b_dt_bias"], g["b_a_log"], g["b_d"] = dpar[0:1], dpar[1:2], dpar[2:3]
    zx3 = zx.reshape(B, S, nzx)
    cw, cb = p["b_conv_w"][0], p["b_conv_b"]
    parts, dws, dbs = [], [], []
    for col0, dpart, nm in ((0, dx3, "b_conv_bwd_x"), (SSM_D_INNER, dB3, "b_conv_bwd_b"),
                            (SSM_D_INNER + SSM_BC_DIM, dC3, "b_conv_bwd_c")):
        dxp, dw, db = _conv_bwd(zx3, dpart, cw, cb, col0, nm)
        parts.append(dxp.reshape(T, -1))
        dws.append(dw)
        dbs.append(db)
    g["b_conv_w"] = jnp.concatenate(dws, axis=1)[None]
    g["b_conv_b"] = jnp.concatenate(dbs, axis=1)
    dzx = jnp.concatenate([dz] + parts, axis=1)
    ddtraw = ddt.transpose(0, 2, 1, 3).reshape(T, SSM_N_HEADS)
    ddtp = jnp.pad(ddtraw, ((0, 0), (0, LANES - SSM_N_HEADS)))
    dw_zx = _matmul(u, dzx, ta=True, name="b_dwzx")
    dw_dt = _matmul(u, ddtp, ta=True, name="b_dwdt")[:, :SSM_N_HEADS]
    g["b_in_w"] = jnp.concatenate([dw_zx, dw_dt], axis=1)[None]
    du = _matmul(dzx, w_in[:, :nzx], tb=True, name="b_du_zx")
    du = _matmul(ddtp, w_dt, tb=True, resid=du, name="b_du_dt")
    return du, g


def _local_step(x, positions, p, target):
    B, S, D = x.shape
    T = B * S
    cos, sin = _rope_tables(positions)
    h = x.reshape(T, D)
    tape = []
    for i in range(DEPTH):
        kind, j = i % 3, i // 3
        u = _rmsnorm_fwd(h, p["norm_mix_w"][i], f"l{i}_norm_mix")
        if kind == 0:
            h1, saved = _swa_fwd(u, h, p, j, B, S, cos, sin, f"a{j}")
        elif kind == 1:
            h1, saved = _mamba_fwd(u, h, p, B, S)
        else:
            h1, saved = _dil_fwd(u, h, p, B, S, cos, sin)
        u2 = _rmsnorm_fwd(h1, p["norm_mlp_w"][i], f"l{i}_norm_mlp")
        r, s = _matmul(u2, p["mlp_w_up"][i], out_dtype=BF16, relu2=True, name=f"l{i}_up")
        h2 = _matmul(s, p["mlp_w_down"][i], resid=h1, name=f"l{i}_down")
        tape.append((h, u, saved, h1, u2, r, s))
        h = h2
    dh, dwf, loss = _final_loss(h, target.reshape(T, D), p["final_norm_w"])
    grads = {"final_norm_w": dwf[0]}
    per_layer = {n: [None] * DEPTH for n in ("norm_mix_w", "norm_mlp_w", "mlp_w_up", "mlp_w_down")}
    a_grads = [None, None]
    for i in reversed(range(DEPTH)):
        kind, j = i % 3, i // 3
        h0, u, saved, h1, u2, r, s = tape[i]
        da = _matmul(dh, p["mlp_w_down"][i], tb=True, out_dtype=BF16, mul=r, mul_scale=2.0, name=f"l{i}_da")
        per_layer["mlp_w_down"][i] = _matmul(s, dh, ta=True, name=f"l{i}_dwdown")
        per_layer["mlp_w_up"][i] = _matmul(u2, da, ta=True, name=f"l{i}_dwup")
        du2 = _matmul(da, p["mlp_w_up"][i], tb=True, name=f"l{i}_du2")
        dh1, dnw = _rmsnorm_bwd(h1, du2, p["norm_mlp_w"][i], dh, f"l{i}_norm_mlp_bwd")
        per_layer["norm_mlp_w"][i] = dnw[0]
        if kind == 0:
            du, g = _swa_bwd(dh1, u, saved, p, j, B, S, cos, sin, f"a{j}")
            a_grads[j] = g
        elif kind == 1:
            du, g = _mamba_bwd(dh1, u, saved, p, B, S)
            grads.update(g)
        else:
            du, g = _dil_bwd(dh1, u, saved, p, B, S, cos, sin)
            grads.update(g)
        dh, dnw = _rmsnorm_bwd(h0, du, p["norm_mix_w"][i], dh1, f"l{i}_norm_mix_bwd")
        per_layer["norm_mix_w"][i] = dnw[0]
    for n in ("norm_mix_w", "norm_mlp_w"):
        grads[n] = jnp.stack(per_layer[n], axis=0)
    for n in ("mlp_w_up", "mlp_w_down"):
        grads[n] = per_layer[n]
    for n in ("a_b_qkv", "a_sinks", "a_b_o"):
        grads[n] = jnp.stack([a_grads[0][n], a_grads[1][n]], axis=0)
    for n in ("a_w_qkv", "a_w_o"):
        grads[n] = [a_grads[0][n], a_grads[1][n]]
    for n in ("b_in_w", "b_out_w", "c_w_qkv", "c_w_o"):
        grads[n] = [grads[n][0]]
    return loss, dh.reshape(B, S, D), grads


def kernel(x, positions, norm_mix_w, norm_mlp_w, a_w_qkv, a_b_qkv, a_sinks, a_w_o, a_b_o, b_in_w, b_conv_w, b_conv_b, b_dt_bias, b_a_log, b_d, b_norm_w, b_out_w, c_w_qkv, c_w_o, mlp_w_up, mlp_w_down, final_norm_w, loss_target, m_norm_mix_w, m_norm_mlp_w, m_a_w_qkv, m_a_b_qkv, m_a_sinks, m_a_w_o, m_a_b_o, m_b_in_w, m_b_conv_w, m_b_conv_b, m_b_dt_bias, m_b_a_log, m_b_d, m_b_norm_w, m_b_out_w, m_c_w_qkv, m_c_w_o, m_mlp_w_up, m_mlp_w_down, m_final_norm_w, v_norm_mix_w, v_norm_mlp_w, v_a_w_qkv, v_a_b_qkv, v_a_sinks, v_a_w_o, v_a_b_o, v_b_in_w, v_b_conv_w, v_b_conv_b, v_b_dt_bias, v_b_a_log, v_b_d, v_b_norm_w, v_b_out_w, v_c_w_qkv, v_c_w_o, v_mlp_w_up, v_mlp_w_down, v_final_norm_w):
    w = dict(zip(W_NAMES, (norm_mix_w, norm_mlp_w, a_w_qkv, a_b_qkv, a_sinks, a_w_o, a_b_o, b_in_w, b_conv_w, b_conv_b,
                           b_dt_bias, b_a_log, b_d, b_norm_w, b_out_w, c_w_qkv, c_w_o, mlp_w_up, mlp_w_down, final_norm_w)))
    m = dict(zip(W_NAMES, (m_norm_mix_w, m_norm_mlp_w, m_a_w_qkv, m_a_b_qkv, m_a_sinks, m_a_w_o, m_a_b_o, m_b_in_w,
                           m_b_conv_w, m_b_conv_b, m_b_dt_bias, m_b_a_log, m_b_d, m_b_norm_w, m_b_out_w, m_c_w_qkv, m_c_w_o,
                           m_mlp_w_up, m_mlp_w_down, m_final_norm_w)))
    v = dict(zip(W_NAMES, (v_norm_mix_w, v_norm_mlp_w, v_a_w_qkv, v_a_b_qkv, v_a_sinks, v_a_w_o, v_a_b_o, v_b_in_w,
                           v_b_conv_w, v_b_conv_b, v_b_dt_bias, v_b_a_log, v_b_d, v_b_norm_w, v_b_out_w, v_c_w_qkv, v_c_w_o,
                           v_mlp_w_up, v_mlp_w_down, v_final_norm_w)))
    px, py, pc = lax.axis_index("x"), lax.axis_index("y"), lax.axis_index("c")
    me = 4 * px + 2 * py + pc
    dev, chip, core = (t.astype(jnp.int32).reshape(1) for t in (me, 2 * px + py, pc))

    trio = tuple(SMALL_SHARDED)
    got = _gather([(d[n], "slot", None, (N_DEV,) + d[n].shape, False) for n in trio for d in (w, m, v)], "gather_small")
    slots = {n: got[3 * i:3 * i + 3] for i, n in enumerate(trio)}
    p = {n: w[n] for n in SMALL_REPLICATED}
    for n in trio:
        p[n] = _from_slots(slots[n][0], SMALL_SHARDED[n])
    for n in BIG_KIND:
        p[n] = [None] * w[n].shape[0]
    for i in range(DEPTH):
        names = _layer_big(i)
        items = []
        for n, l in names:
            kind, s2 = BIG_KIND[n], w[n].shape[1:]
            placed = _place(w[n], l, kind, _full2d(kind, s2), dev, f"place_l{i}_{n}")
            items.append((placed, kind, _block_size(kind, s2), _full2d(kind, s2), True))
        for (n, l), t in zip(names, _gather(items, f"gather_l{i}")):
            p[n][l] = _from_slots(t, 1) if BIG_KIND[n] == "slot" else t

    loss_part, dx, grads = _local_step(x, positions, p, loss_target)
    loss = lax.psum(loss_part[0, 0], AXES)

    res = {n: [[None] * w[n].shape[0] for _ in range(4)] for n in BIG_KIND}
    for i in reversed(range(DEPTH)):
        names = _layer_big(i)
        items = []
        for n, l in names:
            kind, s2 = BIG_KIND[n], w[n].shape[1:]
            g = grads[n][l]
            items.append((_to_slots(g, 1) if kind == "slot" else g, kind, _block_size(kind, s2), s2))
        sib = _reduce_d2d(items, f"reduce_d2d_l{i}")
        parts = [_pair_sum(it[0], s, it[1], core, f"pair_sum_l{i}_{n}") for (n, _), it, s in zip(names, items, sib)]
        recv = _reduce_ici(parts, f"reduce_ici_l{i}")
        for (n, l), pt, r in zip(names, parts, recv):
            for k, o in enumerate(_adamw(pt, r, w[n][l], m[n][l], v[n][l], chip, f"adamw_l{i}_{n}")):
                res[n][k][l] = o
    out = {n: [jnp.stack(res[n][k], axis=0) for k in range(4)] for n in BIG_KIND}

    small = SMALL_REPLICATED + trio
    as2d = lambda t: t.reshape(1, -1) if t.ndim == 1 else t
    g_sm = [as2d(grads[n]) for n in SMALL_REPLICATED] + [_to_slots(grads[n].reshape(p[n].shape), SMALL_SHARDED[n]) for n in trio]
    gathered = _gather([(g, "slot", None, (N_DEV,) + g.shape, False) for g in g_sm], "gather_small_grads")
    ws = [as2d(w[n]) for n in SMALL_REPLICATED] + [slots[n][0] for n in trio]
    ms = [as2d(m[n]) for n in SMALL_REPLICATED] + [slots[n][1] for n in trio]
    vs = [as2d(v[n]) for n in SMALL_REPLICATED] + [slots[n][2] for n in trio]
    sm_out = _small_adamw(gathered, ws, ms, vs)
    for i, n in enumerate(small):
        if n in SMALL_SHARDED:
            out[n] = [lax.dynamic_index_in_dim(sm_out[k][i], me, 0, keepdims=False) for k in range(4)]
        else:
            out[n] = [sm_out[k][i].reshape(w[n].shape) for k in range(4)]
    return (loss, dx, *[out[n][0] for n in W_NAMES], *[out[n][1] for n in W_NAMES], *[out[n][2] for n in W_NAMES],
            *[out[n][3] for n in W_NAMES])
```

```python
import functools
import math

import jax
import jax.numpy as jnp
import numpy as np
from jax import lax
from jax.experimental import pallas as pl
from jax.experimental.pallas import tpu as pltpu

F32 = jnp.float32
BF16 = jnp.bfloat16
SDS = jax.ShapeDtypeStruct

D_MODEL = 1024
DEPTH = 4
BLOCK = 128
ROPE_THETA = 10000.0
NORM_EPS = 1e-5
HEAD_DIM = 64
A_N_HEADS = 16
A_N_KV = 2
A_WINDOW = 128
A_Q_DIM = 1024
A_KV_DIM = 128
SSM_D_INNER = 2048
SSM_N_HEADS = 32
SSM_N_GROUPS = 8
SSM_HG = 4
SSM_D_STATE = 128
SSM_CONV = 4
SSM_CHUNK = 128
SSM_BC_DIM = 1024
SSM_CONV_DIM = 4096
C_PATTERNS = ((128, 1), (512, 4), (2048, 16))
C_HEADS = 16
ADAM_LR, ADAM_B1, ADAM_B2, ADAM_EPS, ADAM_WD, ADAM_STEP = 0.001, 0.9, 0.999, 1e-08, 0.01, 10

N_DEV = 8
AXES = ("x", "y", "c")
LANES = 128
VMEM_LIMIT = 56 * 1024 * 1024
STREAM_VMEM = 16 * 1024 * 1024
NEG = -1e30

NN = (((1,), (0,)), ((), ()))
NT = (((1,), (1,)), ((), ()))
TN = (((0,), (0,)), ((), ()))
HI = lax.Precision.HIGHEST


def _pick(n, cap, mult=LANES):
    best = None
    for t in range(mult, min(n, cap) + 1, mult):
        if n % t == 0:
            best = t
    return best if best is not None else n


def _params(sem):
    return pltpu.CompilerParams(dimension_semantics=sem, vmem_limit_bytes=VMEM_LIMIT)


def _bf(x):
    return x if x.dtype == BF16 else x.astype(BF16)


def _rot_half(y):
    n = y.shape[-1]
    lane = lax.broadcasted_iota(jnp.int32, y.shape, y.ndim - 1)
    return jnp.where((lane % HEAD_DIM) < HEAD_DIM // 2, -pltpu.roll(y, n - 32, y.ndim - 1), pltpu.roll(y, 32, y.ndim - 1))


def _rope(y, cos, sin, sign):
    reps = y.shape[-1] // LANES
    c = jnp.tile(cos, (1, reps)) if reps > 1 else cos
    s = jnp.tile(sin, (1, reps)) if reps > 1 else sin
    return y * c + sign * (_rot_half(y) * s)


MESH = pl.DeviceIdType.MESH
ANY = pl.BlockSpec(memory_space=pl.ANY)


class _Comm:
    def __init__(self, inputs, out_shapes, aliases, sems, phases):
        self.inputs, self.out_shapes, self.aliases, self.sems, self.phases = inputs, out_shapes, aliases, sems, phases


def _pc(body, args, *, out_shape, grid, in_specs, out_specs, name, sem, scratch_shapes=(), comm=None):
    single = not isinstance(out_shape, (tuple, list))
    outs, ospecs = ([out_shape], [out_specs]) if single else (list(out_shape), list(out_specs))
    unpack = (lambda r: r[0]) if single else (lambda r: tuple(r))
    if comm is None:
        res = pl.pallas_call(body, out_shape=outs, grid=grid, in_specs=list(in_specs), out_specs=ospecs,
                             scratch_shapes=list(scratch_shapes), name=name, compiler_params=_params(sem))(*args)
        return unpack(res)
    n_in, n_out, n_scr = len(in_specs), len(outs), len(scratch_shapes)
    c_in, c_out = len(comm.inputs), len(comm.out_shapes)
    total = math.prod(grid)
    steps = [min(total - 1, int(f * total)) for f, _ in comm.phases[:-1]]

    def wrapped(*refs):
        ins, cins = refs[:n_in], refs[n_in:n_in + c_in]
        o = refs[n_in + c_in:n_in + c_in + n_out]
        couts = refs[n_in + c_in + n_out:n_in + c_in + n_out + c_out]
        rest = refs[n_in + c_in + n_out + c_out:]
        scr, csems = rest[:n_scr], rest[n_scr:]
        step = pl.program_id(0)
        for ax in range(1, len(grid)):
            step = step * grid[ax] + pl.program_id(ax)
        for (_, fn), st in zip(comm.phases[:-1], steps):
            @pl.when(step == st)
            def _(fn=fn):
                fn(cins, couts, csems)
        body(*ins, *o, *scr)

        @pl.when(step == total - 1)
        def _():
            comm.phases[-1][1](cins, couts, csems)

    res = pl.pallas_call(
        wrapped, out_shape=outs + list(comm.out_shapes), grid=grid, in_specs=list(in_specs) + [ANY] * c_in,
        out_specs=ospecs + [ANY] * c_out, scratch_shapes=list(scratch_shapes) + list(comm.sems),
        input_output_aliases={n_in + i: n_out + j for i, j in comm.aliases.items()}, name=name,
        compiler_params=_params(("arbitrary",) * len(grid)),
    )(*args, *comm.inputs)
    return unpack(res[:n_out]), list(res[n_out:])


def _hosted(plan, name, run):
    comm = plan.take(name) if plan is not None else None
    if comm is None:
        return run(None)
    res, extra = run(comm)
    plan.give(name, extra)
    return res


def _matmul(a, b, *, ta=False, tb=False, out_dtype=F32, bias=None, resid=None, mul=None, mul_scale=1.0,
            relu2=False, rope=None, rope_cols=0, tm=512, tn=1024, tk=1024, name="mm", plan=None):
    M = a.shape[1] if ta else a.shape[0]
    K = a.shape[0] if ta else a.shape[1]
    N = b.shape[0] if tb else b.shape[1]
    assert (b.shape[1] if tb else b.shape[0]) == K
    tm, tn, tk = _pick(M, tm), _pick(N, tn), _pick(K, tk)
    nk = K // tk
    dims = (((0 if ta else 1,), (1 if tb else 0,)), ((), ()))

    def body(*refs):
        it = iter(refs)
        a_ref, b_ref = next(it), next(it)
        bias_ref = next(it) if bias is not None else None
        resid_ref = next(it) if resid is not None else None
        mul_ref = next(it) if mul is not None else None
        cos_ref, sin_ref = (next(it), next(it)) if rope is not None else (None, None)
        o_ref = next(it)
        o2_ref = next(it) if relu2 else None
        acc_ref = next(it)
        k = pl.program_id(2)
        part = lax.dot_general(_bf(a_ref[...]), _bf(b_ref[...]), dims, preferred_element_type=F32)

        @pl.when(k == 0)
        def _():
            acc_ref[...] = part

        @pl.when(k > 0)
        def _():
            acc_ref[...] += part

        @pl.when(k == nk - 1)
        def _():
            y = acc_ref[...]
            if bias_ref is not None:
                y = y + bias_ref[...]
            if rope is not None:
                col = pl.program_id(1) * tn + lax.broadcasted_iota(jnp.int32, y.shape, 1)
                y = jnp.where(col < rope_cols, _rope(y, cos_ref[...], sin_ref[...], 1.0), y)
            if mul_ref is not None:
                y = y * (mul_ref[...].astype(F32) * mul_scale)
            if resid_ref is not None:
                y = y + resid_ref[...]
            if relu2:
                r = jnp.maximum(y, 0.0)
                o_ref[...] = r.astype(o_ref.dtype)
                o2_ref[...] = (r * r).astype(o2_ref.dtype)
            else:
                o_ref[...] = y.astype(o_ref.dtype)

    a_spec = pl.BlockSpec((tk, tm), lambda i, j, k: (k, i)) if ta else pl.BlockSpec((tm, tk), lambda i, j, k: (i, k))
    b_spec = pl.BlockSpec((tn, tk), lambda i, j, k: (j, k)) if tb else pl.BlockSpec((tk, tn), lambda i, j, k: (k, j))
    mn_spec = pl.BlockSpec((tm, tn), lambda i, j, k: (i, j))
    in_specs, args = [a_spec, b_spec], [a, b]
    if bias is not None:
        in_specs.append(pl.BlockSpec((1, tn), lambda i, j, k: (0, j)))
        args.append(bias)
    if resid is not None:
        in_specs.append(mn_spec)
        args.append(resid)
    if mul is not None:
        in_specs.append(mn_spec)
        args.append(mul)
    if rope is not None:
        in_specs += [pl.BlockSpec((tm, LANES), lambda i, j, k: (i, 0))] * 2
        args += [rope[0], rope[1]]
    out_shape = SDS((M, N), out_dtype)
    out_specs = mn_spec
    if relu2:
        out_shape, out_specs = (out_shape, out_shape), (mn_spec, mn_spec)
    return _hosted(plan, name, lambda comm: _pc(
        body, args, out_shape=out_shape, grid=(M // tm, N // tn, nk), in_specs=in_specs, out_specs=out_specs,
        scratch_shapes=[pltpu.VMEM((tm, tn), F32)], name=name, sem=("parallel", "parallel", "arbitrary"), comm=comm))


def _colsum(x, name):
    T, N = x.shape
    tm = _pick(T, 1024, 8)

    def body(x_ref, o_ref):
        s = jnp.sum(x_ref[...].astype(F32), axis=0, keepdims=True)

        @pl.when(pl.program_id(0) == 0)
        def _():
            o_ref[...] = s

        @pl.when(pl.program_id(0) > 0)
        def _():
            o_ref[...] += s

    return pl.pallas_call(
        body, out_shape=SDS((1, N), F32), grid=(T // tm,),
        in_specs=[pl.BlockSpec((tm, N), lambda i: (i, 0))], out_specs=pl.BlockSpec((1, N), lambda i: (0, 0)),
        name=name, compiler_params=_params(("arbitrary",)),
    )(x)


def _rmsnorm_fwd(h, w, name):
    T, D = h.shape
    tm = _pick(T, 512, 8)

    def body(h_ref, w_ref, o_ref):
        x = h_ref[...]
        rstd = lax.rsqrt(jnp.mean(x * x, axis=-1, keepdims=True) + NORM_EPS)
        o_ref[...] = (x * rstd * w_ref[...]).astype(BF16)

    return pl.pallas_call(
        body, out_shape=SDS((T, D), BF16), grid=(T // tm,),
        in_specs=[pl.BlockSpec((tm, D), lambda i: (i, 0)), pl.BlockSpec((1, D), lambda i: (0, 0))],
        out_specs=pl.BlockSpec((tm, D), lambda i: (i, 0)), name=name, compiler_params=_params(("parallel",)),
    )(h, w.reshape(1, D))


def _rmsnorm_bwd(h, du, w, dres, name):
    T, D = h.shape
    tm = _pick(T, 512, 8)

    def body(h_ref, du_ref, w_ref, dres_ref, dh_ref, dw_ref):
        x = h_ref[...]
        du_ = du_ref[...].astype(F32)
        rstd = lax.rsqrt(jnp.mean(x * x, axis=-1, keepdims=True) + NORM_EPS)
        g = du_ * w_ref[...]
        dh_ref[...] = dres_ref[...] + rstd * g - x * (rstd * rstd * rstd) * jnp.mean(g * x, axis=-1, keepdims=True)
        dw = jnp.sum(du_ * x * rstd, axis=0, keepdims=True)

        @pl.when(pl.program_id(0) == 0)
        def _():
            dw_ref[...] = dw

        @pl.when(pl.program_id(0) > 0)
        def _():
            dw_ref[...] += dw

    row = pl.BlockSpec((tm, D), lambda i: (i, 0))
    vec = pl.BlockSpec((1, D), lambda i: (0, 0))
    return pl.pallas_call(
        body, out_shape=(SDS((T, D), F32), SDS((1, D), F32)), grid=(T // tm,),
        in_specs=[row, row, vec, row], out_specs=(row, vec), name=name, compiler_params=_params(("arbitrary",)),
    )(h, du, w.reshape(1, D), dres)


def _final_loss(h, target, w):
    T, D = h.shape
    tm = _pick(T, 512, 8)

    def body(h_ref, t_ref, w_ref, dh_ref, dw_ref, loss_ref):
        x = h_ref[...]
        rstd = lax.rsqrt(jnp.mean(x * x, axis=-1, keepdims=True) + NORM_EPS)
        xn = x * rstd
        err = xn * w_ref[...] - t_ref[...]
        part = 0.5 * jnp.sum(jnp.mean(err * err, axis=-1, keepdims=True), axis=0, keepdims=True)
        dy = err * (1.0 / D)
        g = dy * w_ref[...]
        dh_ref[...] = rstd * g - x * (rstd * rstd * rstd) * jnp.mean(g * x, axis=-1, keepdims=True)
        dw = jnp.sum(dy * xn, axis=0, keepdims=True)
        lp = jnp.broadcast_to(part, (1, LANES))

        @pl.when(pl.program_id(0) == 0)
        def _():
            dw_ref[...] = dw
            loss_ref[...] = lp

        @pl.when(pl.program_id(0) > 0)
        def _():
            dw_ref[...] += dw
            loss_ref[...] += lp

    row = pl.BlockSpec((tm, D), lambda i: (i, 0))
    vec = pl.BlockSpec((1, D), lambda i: (0, 0))
    return pl.pallas_call(
        body, out_shape=(SDS((T, D), F32), SDS((1, D), F32), SDS((1, LANES), F32)), grid=(T // tm,),
        in_specs=[row, row, vec], out_specs=(row, vec, pl.BlockSpec((1, LANES), lambda i: (0, 0))),
        name="final_loss", compiler_params=_params(("arbitrary",)),
    )(h, target, w.reshape(1, D))


def _band_mask(i_blk, max_dist, first_ok):
    qi = lax.broadcasted_iota(jnp.int32, (BLOCK, 2 * BLOCK), 0)
    kj = lax.broadcasted_iota(jnp.int32, (BLOCK, 2 * BLOCK), 1)
    dist = qi + BLOCK - kj
    ok = (dist >= 0) & (dist <= max_dist)
    return ok & ((kj >= BLOCK) | first_ok)


def _pair(t, i):
    return t[:, LANES * i:LANES * (i + 1)]


def _low_half(shape):
    return lax.broadcasted_iota(jnp.int32, shape, len(shape) - 1) < HEAD_DIM


def _stack_heads(t):
    lo = _low_half(t.shape)
    z = jnp.zeros_like(t)
    return jnp.concatenate([jnp.where(lo, t, z), jnp.where(lo, z, t)], axis=0)


def _swap_halves(t):
    return jnp.concatenate([t[:, HEAD_DIM:], t[:, :HEAD_DIM]], axis=1)


def _kv_operand(kv, kv_swapped, h0, n_kv, n_heads):
    R = n_heads // n_kv
    if R == 1:
        return _pair(kv, h0 // 2)
    assert kv.shape[1] == LANES and R % 2 == 0, "grouped queries: one 128-lane tile of kv heads, both heads of a pair in one group"
    g = h0 // R
    t, ts = _pair(kv, g // 2), _pair(kv_swapped, g // 2)
    lo = _low_half(t.shape)
    return jnp.where(lo, t, ts) if g % 2 == 0 else jnp.where(lo, ts, t)


def _lane_place(cols):
    m = cols[0].shape[0]
    lane = lax.broadcasted_iota(jnp.int32, (m, LANES), 1)
    out = jnp.zeros((m, LANES), F32)
    for h, c in enumerate(cols):
        out = jnp.where(lane == h, c, out)
    return out


def _attn_specs(B, S, d, C, n_heads, n_kv, q_col, k_col, v_col):
    kvw = n_kv * HEAD_DIM
    qw = n_heads * HEAD_DIM
    cq, ck = (C // qw if d > 1 else 0), (C // kvw if d > 1 else 0)
    q_spec = pl.BlockSpec((1, BLOCK, qw), lambda b, r, i: (b, i, r * cq + q_col // qw))
    kc = pl.BlockSpec((1, BLOCK, kvw), lambda b, r, i: (b, i, r * ck + k_col // kvw))
    kp = pl.BlockSpec((1, BLOCK, kvw), lambda b, r, i: (b, jnp.maximum(i - 1, 0), r * ck + k_col // kvw))
    vc = pl.BlockSpec((1, BLOCK, kvw), lambda b, r, i: (b, i, r * ck + v_col // kvw))
    vp = pl.BlockSpec((1, BLOCK, kvw), lambda b, r, i: (b, jnp.maximum(i - 1, 0), r * ck + v_col // kvw))
    return q_spec, kp, kc, vp, vc


def _attn_fwd(qkv, B, S, d, *, n_heads, n_kv, q_col, k_col, v_col, max_dist, sinks, name, plan=None):
    C = qkv.shape[1]
    Ls = S // d
    nb = Ls // BLOCK
    qw = n_heads * HEAD_DIM
    R = n_heads // n_kv
    qkv3 = qkv.reshape(B, Ls, d * C)
    scale = HEAD_DIM ** -0.5

    def body(*refs):
        if sinks is not None:
            sink_ref, q_ref, kp_ref, kc_ref, vp_ref, vc_ref, o_ref, lse_ref = refs
        else:
            q_ref, kp_ref, kc_ref, vp_ref, vc_ref, o_ref, lse_ref = refs
        i = pl.program_id(2)
        mask1 = _band_mask(i, max_dist, i > 0)
        mask = jnp.concatenate([mask1, mask1], axis=0)
        q = q_ref[0]
        kk = jnp.concatenate([kp_ref[0], kc_ref[0]], axis=0)
        vv = jnp.concatenate([vp_ref[0], vc_ref[0]], axis=0)
        kks, vvs = (_swap_halves(kk), _swap_halves(vv)) if R > 1 else (None, None)
        lo = _low_half((BLOCK, LANES))
        top = lax.broadcasted_iota(jnp.int32, (2 * BLOCK, 1), 0) < BLOCK
        lses, tiles = [], []
        for t in range(n_heads // 2):
            k2 = _kv_operand(kk, kks, 2 * t, n_kv, n_heads)
            v2 = _kv_operand(vv, vvs, 2 * t, n_kv, n_heads)
            s = lax.dot_general(_stack_heads(_pair(q, t)), k2, NT, preferred_element_type=F32) * scale
            s = jnp.where(mask, s, NEG)
            m = jnp.max(s, axis=-1, keepdims=True)
            if sinks is not None:
                sk = jnp.where(top, sink_ref[2 * t], sink_ref[2 * t + 1])
                m = jnp.maximum(m, sk)
            p = jnp.exp(s - m)
            den = jnp.sum(p, axis=-1, keepdims=True)
            if sinks is not None:
                den = den + jnp.exp(sk - m)
            lse2 = m + jnp.log(den)
            o2 = jnp.dot((p / den).astype(BF16), v2, preferred_element_type=F32)
            tiles.append(jnp.where(lo, o2[:BLOCK], o2[BLOCK:]))
            lses += [lse2[:BLOCK], lse2[BLOCK:]]
        o_ref[0] = jnp.concatenate(tiles, axis=-1)
        lse_ref[0] = _lane_place(lses)

    specs = list(_attn_specs(B, S, d, C, n_heads, n_kv, q_col, k_col, v_col))
    args = [qkv3] * 5
    if sinks is not None:
        specs = [pl.BlockSpec(memory_space=pltpu.SMEM)] + specs
        args = [sinks] + args
    o3, lse3 = _hosted(plan, name, lambda comm: _pc(
        body, args, out_shape=(SDS((B, Ls, d * qw), F32), SDS((B, Ls, d * LANES), F32)), grid=(B, d, nb), in_specs=specs,
        out_specs=(pl.BlockSpec((1, BLOCK, qw), lambda b, r, i: (b, i, r)), pl.BlockSpec((1, BLOCK, LANES), lambda b, r, i: (b, i, r))),
        name=name, sem=("parallel", "parallel", "parallel"), comm=comm))
    return o3.reshape(B * S, qw), lse3.reshape(B * S, LANES)


def _attn_dq(qkv, do, lse, delta, cos, sin, B, S, d, *, n_heads, n_kv, q_col, k_col, v_col, max_dist, name, plan=None):
    C = qkv.shape[1]
    Ls = S // d
    nb = Ls // BLOCK
    qw = n_heads * HEAD_DIM
    R = n_heads // n_kv
    scale = HEAD_DIM ** -0.5

    def body(q_ref, kp_ref, kc_ref, vp_ref, vc_ref, do_ref, lse_ref, dl_ref, cos_ref, sin_ref, dq_ref):
        i = pl.program_id(2)
        mask1 = _band_mask(i, max_dist, i > 0)
        mask = jnp.concatenate([mask1, mask1], axis=0)
        q = q_ref[0]
        do_ = do_ref[0]
        kk = jnp.concatenate([kp_ref[0], kc_ref[0]], axis=0)
        vv = jnp.concatenate([vp_ref[0], vc_ref[0]], axis=0)
        kks, vvs = (_swap_halves(kk), _swap_halves(vv)) if R > 1 else (None, None)
        lo = _low_half((BLOCK, LANES))
        lse_t, dl_t = lse_ref[0], dl_ref[0]
        tiles = []
        for t in range(n_heads // 2):
            k2 = _kv_operand(kk, kks, 2 * t, n_kv, n_heads)
            v2 = _kv_operand(vv, vvs, 2 * t, n_kv, n_heads)
            lse2 = jnp.concatenate([lse_t[:, 2 * t:2 * t + 1], lse_t[:, 2 * t + 1:2 * t + 2]], axis=0)
            dl2 = jnp.concatenate([dl_t[:, 2 * t:2 * t + 1], dl_t[:, 2 * t + 1:2 * t + 2]], axis=0)
            s = lax.dot_general(_stack_heads(_pair(q, t)), k2, NT, preferred_element_type=F32) * scale
            p = jnp.where(mask, jnp.exp(s - lse2), 0.0)
            dp = lax.dot_general(_stack_heads(_pair(do_, t)), v2, NT, preferred_element_type=F32)
            ds = p * (dp - dl2)
            dq2 = jnp.dot(ds.astype(BF16), k2, preferred_element_type=F32) * scale
            tiles.append(jnp.where(lo, dq2[:BLOCK], dq2[BLOCK:]))
        dq = jnp.concatenate(tiles, axis=-1)
        dq_ref[0] = _rope(dq, cos_ref[0], sin_ref[0], -1.0).astype(BF16)

    qs, kp, kc, vp, vc = _attn_specs(B, S, d, C, n_heads, n_kv, q_col, k_col, v_col)
    row_q = pl.BlockSpec((1, BLOCK, qw), lambda b, r, i: (b, i, r))
    row_l = pl.BlockSpec((1, BLOCK, LANES), lambda b, r, i: (b, i, r))
    qkv3 = qkv.reshape(B, Ls, d * C)
    v3 = lambda t, w: t.reshape(B, Ls, d * w)
    args = (qkv3, qkv3, qkv3, qkv3, qkv3, v3(do, qw), v3(lse, LANES), v3(delta, LANES), v3(cos, LANES), v3(sin, LANES))
    dq3 = _hosted(plan, name, lambda comm: _pc(
        body, args, out_shape=SDS((B, Ls, d * qw), BF16), grid=(B, d, nb),
        in_specs=[qs, kp, kc, vp, vc, row_q, row_l, row_l, row_l, row_l], out_specs=row_q,
        name=name, sem=("parallel", "parallel", "parallel"), comm=comm))
    return dq3.reshape(B * S, qw)


def _attn_dkv(qkv, do, lse, delta, cos, sin, B, S, d, *, n_heads, n_kv, q_col, k_col, v_col, max_dist, name, plan=None):
    C = qkv.shape[1]
    Ls = S // d
    nb = Ls // BLOCK
    qw = n_heads * HEAD_DIM
    kvw = n_kv * HEAD_DIM
    R = n_heads // n_kv
    scale = HEAD_DIM ** -0.5
    cq, ck = (C // qw if d > 1 else 0), (C // kvw if d > 1 else 0)

    def body(k_ref, v_ref, q0_ref, q1_ref, do0_ref, do1_ref, lse0_ref, lse1_ref, dl0_ref, dl1_ref, cos_ref, sin_ref,
             dk_ref, dv_ref):
        j = pl.program_id(2)
        kj = lax.broadcasted_iota(jnp.int32, (BLOCK, BLOCK), 0)
        qi = lax.broadcasted_iota(jnp.int32, (BLOCK, BLOCK), 1)
        dist0 = qi - kj
        dist1 = qi + BLOCK - kj
        mask0 = (dist0 >= 0) & (dist0 <= max_dist)
        mask1 = (dist1 <= max_dist) & (j + 1 < nb)
        kb, vb = k_ref[0], v_ref[0]
        kbs, vbs = (_swap_halves(kb), _swap_halves(vb)) if R > 1 else (None, None)
        sides = ((q0_ref[0], do0_ref[0], lse0_ref[0].T, dl0_ref[0].T, mask0), (q1_ref[0], do1_ref[0], lse1_ref[0].T, dl1_ref[0].T, mask1))
        n_acc = n_kv if R > 1 else n_kv // 2
        dks = [jnp.zeros((BLOCK, LANES), F32) for _ in range(n_acc)]
        dvs = [jnp.zeros((BLOCK, LANES), F32) for _ in range(n_acc)]
        for t in range(n_heads // 2):
            k2 = _kv_operand(kb, kbs, 2 * t, n_kv, n_heads)
            v2 = _kv_operand(vb, vbs, 2 * t, n_kv, n_heads)
            a = (2 * t) // R if R > 1 else t
            for (q, do_, lse_r, dl_r, mask) in sides:
                q2, do2 = _stack_heads(_pair(q, t)), _stack_heads(_pair(do_, t))
                s = lax.dot_general(k2, q2, NT, preferred_element_type=F32) * scale
                dp = lax.dot_general(v2, do2, NT, preferred_element_type=F32)
                ps, dss = [], []
                for half in (0, 1):
                    h = 2 * t + half
                    sl = slice(BLOCK * half, BLOCK * (half + 1))
                    p = jnp.where(mask, jnp.exp(s[:, sl] - lse_r[h:h + 1, :]), 0.0)
                    ps.append(p)
                    dss.append(p * (dp[:, sl] - dl_r[h:h + 1, :]))
                dvs[a] = dvs[a] + jnp.dot(jnp.concatenate(ps, axis=1).astype(BF16), do2, preferred_element_type=F32)
                dks[a] = dks[a] + jnp.dot(jnp.concatenate(dss, axis=1).astype(BF16), q2, preferred_element_type=F32)
        if R > 1:
            lo = _low_half((BLOCK, LANES))
            fold = lambda x: x + pltpu.roll(x, HEAD_DIM, 1)
            dks = [jnp.where(lo, fold(dks[2 * t]), fold(dks[2 * t + 1])) for t in range(n_kv // 2)]
            dvs = [jnp.where(lo, fold(dvs[2 * t]), fold(dvs[2 * t + 1])) for t in range(n_kv // 2)]
        dk_t = jnp.concatenate(dks, axis=-1) * scale
        dk_ref[0] = _rope(dk_t, cos_ref[0], sin_ref[0], -1.0).astype(BF16)
        dv_ref[0] = jnp.concatenate(dvs, axis=-1).astype(BF16)

    nxt = lambda j: jnp.minimum(j + 1, nb - 1)
    k_spec = pl.BlockSpec((1, BLOCK, kvw), lambda b, r, j: (b, j, r * ck + k_col // kvw))
    v_spec = pl.BlockSpec((1, BLOCK, kvw), lambda b, r, j: (b, j, r * ck + v_col // kvw))
    q0 = pl.BlockSpec((1, BLOCK, qw), lambda b, r, j: (b, j, r * cq + q_col // qw))
    q1 = pl.BlockSpec((1, BLOCK, qw), lambda b, r, j: (b, nxt(j), r * cq + q_col // qw))
    w0 = lambda w: pl.BlockSpec((1, BLOCK, w), lambda b, r, j: (b, j, r))
    w1 = lambda w: pl.BlockSpec((1, BLOCK, w), lambda b, r, j: (b, nxt(j), r))
    qkv3 = qkv.reshape(B, Ls, d * C)
    v3 = lambda t, w: t.reshape(B, Ls, d * w)
    do3, lse3, dl3 = v3(do, qw), v3(lse, LANES), v3(delta, LANES)
    args = (qkv3, qkv3, qkv3, qkv3, do3, do3, lse3, lse3, dl3, dl3, v3(cos, LANES), v3(sin, LANES))
    dk3, dv3 = _hosted(plan, name, lambda comm: _pc(
        body, args, out_shape=(SDS((B, Ls, d * kvw), BF16), SDS((B, Ls, d * kvw), BF16)), grid=(B, d, nb),
        in_specs=[k_spec, v_spec, q0, q1, w0(qw), w1(qw), w0(LANES), w1(LANES), w0(LANES), w1(LANES), w0(LANES), w0(LANES)],
        out_specs=(w0(kvw), w0(kvw)), name=name, sem=("parallel", "parallel", "parallel"), comm=comm))
    return dk3.reshape(B * S, kvw), dv3.reshape(B * S, kvw)


def _head_expand():
    r = lax.broadcasted_iota(jnp.int32, (LANES, C_HEADS * HEAD_DIM), 0)
    c = lax.broadcasted_iota(jnp.int32, (LANES, C_HEADS * HEAD_DIM), 1)
    return jnp.where(c // HEAD_DIM == r, 1.0, 0.0).astype(F32)


def _delta(do, o, lse=None, sinks_row=None, name="delta"):
    T, W = do.shape
    tm = _pick(T, 512, 8)
    with_sink = sinks_row is not None

    def body(*refs):
        if with_sink:
            do_ref, o_ref, lse_ref, sk_ref, dl_ref, dob_ref, ds_ref = refs
        else:
            do_ref, o_ref, dl_ref, dob_ref = refs
        do_ = do_ref[...]
        dl = lax.dot_general(do_ * o_ref[...], _head_expand(), NT, preferred_element_type=F32, precision=HI)
        dl_ref[...] = dl
        dob_ref[...] = do_.astype(BF16)
        if with_sink:
            lane = lax.broadcasted_iota(jnp.int32, dl.shape, 1)
            contrib = jnp.where(lane < A_N_HEADS, -jnp.exp(sk_ref[...] - lse_ref[...]) * dl, 0.0)
            part = jnp.sum(contrib, axis=0, keepdims=True)

            @pl.when(pl.program_id(0) == 0)
            def _():
                ds_ref[...] = part

            @pl.when(pl.program_id(0) > 0)
            def _():
                ds_ref[...] += part

    row_w = pl.BlockSpec((tm, W), lambda i: (i, 0))
    row_l = pl.BlockSpec((tm, LANES), lambda i: (i, 0))
    vec_l = pl.BlockSpec((1, LANES), lambda i: (0, 0))
    if with_sink:
        return pl.pallas_call(
            body, out_shape=(SDS((T, LANES), F32), SDS((T, W), BF16), SDS((1, LANES), F32)), grid=(T // tm,),
            in_specs=[row_w, row_w, row_l, vec_l], out_specs=(row_l, row_w, vec_l), name=name,
            compiler_params=_params(("arbitrary",)),
        )(do, o, lse, sinks_row)
    return pl.pallas_call(
        body, out_shape=(SDS((T, LANES), F32), SDS((T, W), BF16)), grid=(T // tm,),
        in_specs=[row_w, row_w], out_specs=(row_l, row_w), name=name, compiler_params=_params(("parallel",)),
    )(do, o)


def _merge(os_, lses):
    T, W = os_[0].shape
    tm = _pick(T, 512, 8)

    def body(o0, o1, o2, l0, l1, l2, o_ref, lse_ref):
        ls = [l0[...], l1[...], l2[...]]
        m = jnp.maximum(jnp.maximum(ls[0], ls[1]), ls[2])
        ws = [jnp.exp(l - m) for l in ls]
        tot = ws[0] + ws[1] + ws[2]
        lse_ref[...] = m + jnp.log(tot)
        e = _head_expand()
        acc = jnp.zeros((tm, W), F32)
        for w, o in zip(ws, (o0, o1, o2)):
            acc = acc + jnp.dot(w / tot, e, preferred_element_type=F32, precision=HI) * o[...]
        o_ref[...] = acc

    row_w = pl.BlockSpec((tm, W), lambda i: (i, 0))
    row_l = pl.BlockSpec((tm, LANES), lambda i: (i, 0))
    return pl.pallas_call(
        body, out_shape=(SDS((T, W), F32), SDS((T, LANES), F32)), grid=(T // tm,),
        in_specs=[row_w] * 3 + [row_l] * 3, out_specs=(row_w, row_l), name="c_merge", compiler_params=_params(("parallel",)),
    )(*os_, *lses)


CONV_TC = 256


def _conv_pre(x, w, bias):
    row = lax.broadcasted_iota(jnp.int32, x.shape, 0)
    acc = x * w[SSM_CONV - 1:SSM_CONV, :] + bias
    for k in range(1, SSM_CONV):
        acc = acc + jnp.where(row >= k, pltpu.roll(x, k, 0), 0.0) * w[SSM_CONV - 1 - k:SSM_CONV - k, :]
    return acc


def _conv_fwd(zx3, w, bias):
    B, S, _ = zx3.shape
    off = SSM_D_INNER // CONV_TC

    def body(x_ref, w_ref, b_ref, o_ref):
        v = _conv_pre(x_ref[0], w_ref[...], b_ref[...])
        o_ref[0] = v * jax.nn.sigmoid(v)

    return pl.pallas_call(
        body, out_shape=SDS((B, S, SSM_CONV_DIM), F32), grid=(B, SSM_CONV_DIM // CONV_TC),
        in_specs=[pl.BlockSpec((1, S, CONV_TC), lambda b, j: (b, 0, j + off)),
                  pl.BlockSpec((SSM_CONV, CONV_TC), lambda b, j: (0, j)), pl.BlockSpec((1, CONV_TC), lambda b, j: (0, j))],
        out_specs=pl.BlockSpec((1, S, CONV_TC), lambda b, j: (b, 0, j)), name="b_conv_fwd",
        compiler_params=_params(("parallel", "parallel")),
    )(zx3, w, bias)


def _conv_bwd(zx3, dxc, w, bias, col0, name):
    B, S, n = dxc.shape
    tc = _pick(n, CONV_TC)
    off_x = (SSM_D_INNER + col0) // tc
    off_w = col0 // tc

    def body(x_ref, d_ref, w_ref, b_ref, dx_ref, dw_ref, db_ref):
        x = x_ref[0]
        wv = w_ref[...]
        v = _conv_pre(x, wv, b_ref[...])
        sg = jax.nn.sigmoid(v)
        dc = d_ref[0] * (sg * (1.0 + v * (1.0 - sg)))
        row = lax.broadcasted_iota(jnp.int32, x.shape, 0)
        dx = dc * wv[SSM_CONV - 1:SSM_CONV, :]
        dws = [jnp.sum(dc * x, axis=0, keepdims=True)]
        for k in range(1, SSM_CONV):
            dx = dx + jnp.where(row < S - k, pltpu.roll(dc, S - k, 0), 0.0) * wv[SSM_CONV - 1 - k:SSM_CONV - k, :]
            dws.append(jnp.sum(dc * jnp.where(row >= k, pltpu.roll(x, k, 0), 0.0), axis=0, keepdims=True))
        dx_ref[0] = dx.astype(BF16)
        ridx = lax.broadcasted_iota(jnp.int32, (SSM_CONV, tc), 0)
        dw = jnp.zeros((SSM_CONV, tc), F32)
        for k in range(SSM_CONV):
            dw = jnp.where(ridx == SSM_CONV - 1 - k, dws[k], dw)
        db = jnp.sum(dc, axis=0, keepdims=True)

        @pl.when(pl.program_id(1) == 0)
        def _():
            dw_ref[...] = dw
            db_ref[...] = db

        @pl.when(pl.program_id(1) > 0)
        def _():
            dw_ref[...] += dw
            db_ref[...] += db

    return pl.pallas_call(
        body, out_shape=(SDS((B, S, n), BF16), SDS((SSM_CONV, n), F32), SDS((1, n), F32)), grid=(n // tc, B),
        in_specs=[pl.BlockSpec((1, S, tc), lambda j, b: (b, 0, j + off_x)), pl.BlockSpec((1, S, tc), lambda j, b: (b, 0, j)),
                  pl.BlockSpec((SSM_CONV, tc), lambda j, b: (0, j + off_w)), pl.BlockSpec((1, tc), lambda j, b: (0, j + off_w))],
        out_specs=(pl.BlockSpec((1, S, tc), lambda j, b: (b, 0, j)), pl.BlockSpec((SSM_CONV, tc), lambda j, b: (0, j)),
                   pl.BlockSpec((1, tc), lambda j, b: (0, j))),
        name=name, compiler_params=_params(("parallel", "arbitrary")),
    )(zx3, dxc, w, bias)


def _ssd_common(x, Bm, Cm, dtc_raw, dtr_raw, pr, pc):
    Q = SSM_CHUNK
    zc = dtc_raw + pr[0:1, :]
    dt_c = jax.nn.softplus(zc)
    dt_r = jax.nn.softplus(dtr_raw + pc[:, 0:1])
    A_r = -jnp.exp(pr[1:2, :])
    A_c = -jnp.exp(pc[:, 1:2])
    row = lax.broadcasted_iota(jnp.int32, (Q, Q), 0)
    col = lax.broadcasted_iota(jnp.int32, (Q, Q), 1)
    tril = jnp.where(row >= col, 1.0, 0.0).astype(F32)
    cs_c = jnp.dot(tril, dt_c * A_r, preferred_element_type=F32, precision=HI)
    cs_r = lax.dot_general(dt_r * A_c, tril, NT, preferred_element_type=F32, precision=HI)
    return zc, dt_c, A_r, cs_c, cs_r, row, col, tril


def _ssd_fwd(xc3, dtc, dtr, prow, pcol, plan=None):
    B, S, _ = xc3.shape
    Q, G, HG, P, N = SSM_CHUNK, SSM_N_GROUPS, SSM_HG, HEAD_DIM, SSM_D_STATE
    nc = S // Q
    xw = HG * P

    def body(x_ref, b_ref, c_ref, dtc_ref, dtr_ref, pr_ref, pc_ref, y_ref, st_ref, state):
        c = pl.program_id(2)

        @pl.when(c == 0)
        def _():
            state[...] = jnp.zeros_like(state)

        x, Bm, Cm = x_ref[0], b_ref[0], c_ref[0]
        pr = pr_ref[0]
        _, dt_c, _, cs_c, cs_r, row, col, _ = _ssd_common(x, Bm, Cm, dtc_ref[0, 0], dtr_ref[0, 0], pr, pc_ref[0])
        Bb, Cb = Bm.astype(BF16), Cm.astype(BF16)
        CB = lax.dot_general(Cb, Bb, NT, preferred_element_type=F32)
        ys = []
        for hg in range(HG):
            xh = x[:, P * hg:P * (hg + 1)]
            xt = xh * dt_c[:, hg:hg + 1]
            csc, csr = cs_c[:, hg:hg + 1], cs_r[hg:hg + 1, :]
            L = jnp.where(row >= col, jnp.exp(jnp.minimum(csc - csr, 0.0)), 0.0)
            ydiag = jnp.dot((CB * L).astype(BF16), xt.astype(BF16), preferred_element_type=F32)
            Sh = state[hg]
            yoff = lax.dot_general(Cb, Sh.astype(BF16), NT, preferred_element_type=F32) * jnp.exp(csc)
            ys.append(ydiag + yoff + pr[2:3, hg:hg + 1] * xh)
            st_ref[0, 0, 0, P * hg:P * (hg + 1), :] = Sh
            csq = csc[Q - 1:Q, :]
            upd = lax.dot_general((xt * jnp.exp(csq - csc)).astype(BF16), Bb, TN, preferred_element_type=F32)
            state[hg] = Sh * jnp.exp(csq) + upd
        y_ref[0] = jnp.concatenate([jnp.concatenate(ys[0:2], axis=-1), jnp.concatenate(ys[2:4], axis=-1)], axis=-1)

    bo, co = SSM_D_INNER // N, (SSM_D_INNER + SSM_BC_DIM) // N
    return _hosted(plan, "b_ssd_fwd", lambda comm: _pc(
        body, (xc3, xc3, xc3, dtc, dtr, prow, pcol),
        out_shape=(SDS((B, S, SSM_D_INNER), F32), SDS((B, G, nc, xw, N), F32)), grid=(G, B, nc),
        in_specs=[pl.BlockSpec((1, Q, xw), lambda g, b, c: (b, c, g)), pl.BlockSpec((1, Q, N), lambda g, b, c: (b, c, bo + g)),
                  pl.BlockSpec((1, Q, N), lambda g, b, c: (b, c, co + g)), pl.BlockSpec((1, 1, Q, HG), lambda g, b, c: (b, g, c, 0)),
                  pl.BlockSpec((1, 1, HG, Q), lambda g, b, c: (b, g, 0, c)), pl.BlockSpec((1, 3, HG), lambda g, b, c: (g, 0, 0)),
                  pl.BlockSpec((1, HG, 3), lambda g, b, c: (g, 0, 0))],
        out_specs=(pl.BlockSpec((1, Q, xw), lambda g, b, c: (b, c, g)), pl.BlockSpec((1, 1, 1, xw, N), lambda g, b, c: (b, g, c, 0, 0))),
        scratch_shapes=[pltpu.VMEM((HG, P, N), F32)], name="b_ssd_fwd", sem=("parallel", "arbitrary", "arbitrary"), comm=comm))


def _ssd_bwd(xc3, dtc, dtr, prow, pcol, states, dy3, plan=None):
    B, S, _ = xc3.shape
    Q, G, HG, P, N = SSM_CHUNK, SSM_N_GROUPS, SSM_HG, HEAD_DIM, SSM_D_STATE
    nc = S // Q
    xw = HG * P

    def body(x_ref, b_ref, c_ref, dtc_ref, dtr_ref, pr_ref, pc_ref, st_ref, dy_ref,
             dx_ref, db_ref, dc_ref, ddt_ref, dpar_ref, dstate):
        bi, ci = pl.program_id(1), pl.program_id(2)

        @pl.when(ci == 0)
        def _():
            dstate[...] = jnp.zeros_like(dstate)

        x, Bm, Cm, dy = x_ref[0], b_ref[0], c_ref[0], dy_ref[0]
        pr = pr_ref[0]
        zc, dt_c, A_r, cs_c, cs_r, row, col, tril = _ssd_common(x, Bm, Cm, dtc_ref[0, 0], dtr_ref[0, 0], pr, pc_ref[0])
        Bb, Cb = Bm.astype(BF16), Cm.astype(BF16)
        CB = lax.dot_general(Cb, Bb, NT, preferred_element_type=F32)
        CBt = lax.dot_general(Bb, Cb, NT, preferred_element_type=F32)
        lane4 = lax.broadcasted_iota(jnp.int32, (Q, HG), 1)
        lane4r = lax.broadcasted_iota(jnp.int32, (1, HG), 1)
        rowq = lax.broadcasted_iota(jnp.int32, (Q, 1), 0)
        dB = jnp.zeros((Q, N), F32)
        dC = jnp.zeros((Q, N), F32)
        dcs4 = jnp.zeros((Q, HG), F32)
        dtx4 = jnp.zeros((Q, HG), F32)
        dD4 = jnp.zeros((1, HG), F32)
        dxts, xhs, dyhs = [], [], []
        for hg in range(HG):
            xh = x[:, P * hg:P * (hg + 1)]
            dyh = dy[:, P * hg:P * (hg + 1)]
            xt = xh * dt_c[:, hg:hg + 1]
            xtb, dyb = xt.astype(BF16), dyh.astype(BF16)
            csc, csr = cs_c[:, hg:hg + 1], cs_r[hg:hg + 1, :]
            L = jnp.where(row >= col, jnp.exp(jnp.minimum(csc - csr, 0.0)), 0.0)
            Lt = jnp.where(col >= row, jnp.exp(jnp.minimum(csr - csc, 0.0)), 0.0)
            M, Mt = CB * L, CBt * Lt
            Sh = st_ref[0, 0, 0, P * hg:P * (hg + 1), :]
            dSh = dstate[hg]
            Shb, dShb = Sh.astype(BF16), dSh.astype(BF16)
            ecs = jnp.exp(csc)
            csq = csc[Q - 1:Q, :]
            dec = jnp.exp(csq - csc)
            dxt = jnp.dot(Mt.astype(BF16), dyb, preferred_element_type=F32)
            dxt = dxt + lax.dot_general(Bb, dShb, NT, preferred_element_type=F32) * dec
            Gm = lax.dot_general(dyb, xtb, NT, preferred_element_type=F32)
            Gt = lax.dot_general(xtb, dyb, NT, preferred_element_type=F32)
            dC = dC + jnp.dot((Gm * L).astype(BF16), Bb, preferred_element_type=F32)
            dB = dB + jnp.dot((Gt * Lt).astype(BF16), Cb, preferred_element_type=F32)
            dC = dC + jnp.dot(dyb, Shb, preferred_element_type=F32) * ecs
            dBst = jnp.dot(xtb, dShb, preferred_element_type=F32) * dec
            dB = dB + dBst
            dcs = jnp.sum(Gm * M, axis=1, keepdims=True) - jnp.sum(Gt * Mt, axis=1, keepdims=True)
            yoff = lax.dot_general(Cb, Shb, NT, preferred_element_type=F32) * ecs
            dcs = dcs + jnp.sum(yoff * dyh, axis=1, keepdims=True)
            r = jnp.sum(dBst * Bm, axis=1, keepdims=True)
            dcs = dcs - r
            extra = jnp.sum(r, axis=0, keepdims=True) + jnp.exp(csq) * jnp.sum(
                jnp.sum(dSh * Sh, axis=1, keepdims=True), axis=0, keepdims=True)
            dcs = dcs + jnp.where(rowq == Q - 1, extra, 0.0)
            dcs4 = jnp.where(lane4 == hg, dcs, dcs4)
            dtx4 = jnp.where(lane4 == hg, jnp.sum(dxt * xh, axis=1, keepdims=True), dtx4)
            dD4 = jnp.where(lane4r == hg, jnp.sum(jnp.sum(dyh * xh, axis=1, keepdims=True), axis=0, keepdims=True), dD4)
            dstate[hg] = dSh * jnp.exp(csq) + lax.dot_general((dyh * ecs).astype(BF16), Cb, TN, preferred_element_type=F32)
            dxts.append(dxt)
            xhs.append(xh)
            dyhs.append(dyh)
        da4 = lax.dot_general(tril, dcs4, TN, preferred_element_type=F32, precision=HI)
        ddt4 = da4 * A_r + dtx4
        ddtraw = ddt4 * jax.nn.sigmoid(zc)
        ddt_ref[0, 0] = ddtraw
        dxs = [dxts[hg] * dt_c[:, hg:hg + 1] + pr[2:3, hg:hg + 1] * dyhs[hg] for hg in range(HG)]
        dx_ref[0] = jnp.concatenate([jnp.concatenate(dxs[0:2], axis=-1), jnp.concatenate(dxs[2:4], axis=-1)], axis=-1)
        db_ref[0] = dB
        dc_ref[0] = dC
        d_bias = jnp.sum(ddtraw, axis=0, keepdims=True)
        d_alog = jnp.sum(da4 * dt_c, axis=0, keepdims=True) * A_r
        r3 = lax.broadcasted_iota(jnp.int32, (3, HG), 0)
        dpar = jnp.where(r3 == 0, d_bias, jnp.where(r3 == 1, d_alog, dD4))
        first = (bi == 0) & (ci == 0)

        @pl.when(first)
        def _():
            dpar_ref[0] = dpar

        @pl.when(jnp.logical_not(first))
        def _():
            dpar_ref[0] += dpar

    rc = lambda c: nc - 1 - c
    bo, co = SSM_D_INNER // N, (SSM_D_INNER + SSM_BC_DIM) // N
    return _hosted(plan, "b_ssd_bwd", lambda comm: _pc(
        body, (xc3, xc3, xc3, dtc, dtr, prow, pcol, states, dy3),
        out_shape=(SDS((B, S, SSM_D_INNER), F32), SDS((B, S, SSM_BC_DIM), F32), SDS((B, S, SSM_BC_DIM), F32),
                   SDS((B, G, S, HG), F32), SDS((G, 3, HG), F32)),
        grid=(G, B, nc),
        in_specs=[pl.BlockSpec((1, Q, xw), lambda g, b, c: (b, rc(c), g)), pl.BlockSpec((1, Q, N), lambda g, b, c: (b, rc(c), bo + g)),
                  pl.BlockSpec((1, Q, N), lambda g, b, c: (b, rc(c), co + g)), pl.BlockSpec((1, 1, Q, HG), lambda g, b, c: (b, g, rc(c), 0)),
                  pl.BlockSpec((1, 1, HG, Q), lambda g, b, c: (b, g, 0, rc(c))), pl.BlockSpec((1, 3, HG), lambda g, b, c: (g, 0, 0)),
                  pl.BlockSpec((1, HG, 3), lambda g, b, c: (g, 0, 0)),
                  pl.BlockSpec((1, 1, 1, xw, N), lambda g, b, c: (b, g, rc(c), 0, 0)), pl.BlockSpec((1, Q, xw), lambda g, b, c: (b, rc(c), g))],
        out_specs=(pl.BlockSpec((1, Q, xw), lambda g, b, c: (b, rc(c), g)), pl.BlockSpec((1, Q, N), lambda g, b, c: (b, rc(c), g)),
                   pl.BlockSpec((1, Q, N), lambda g, b, c: (b, rc(c), g)), pl.BlockSpec((1, 1, Q, HG), lambda g, b, c: (b, g, rc(c), 0)),
                   pl.BlockSpec((1, 3, HG), lambda g, b, c: (g, 0, 0))),
        scratch_shapes=[pltpu.VMEM((HG, P, N), F32)], name="b_ssd_bwd", sem=("parallel", "arbitrary", "arbitrary"), comm=comm))


GN_W = SSM_D_INNER // SSM_N_GROUPS


def _gate_fwd(y, zx, nw):
    T = y.shape[0]
    tm = _pick(T, 256, 8)

    def body(y_ref, z_ref, w_ref, o_ref):
        z = z_ref[...]
        gt = y_ref[...] * (z * jax.nn.sigmoid(z))
        outs = []
        for k in range(SSM_N_GROUPS):
            gk = gt[:, GN_W * k:GN_W * (k + 1)]
            outs.append(gk * lax.rsqrt(jnp.mean(gk * gk, axis=-1, keepdims=True) + NORM_EPS))
        o_ref[...] = (jnp.concatenate(outs, axis=-1) * w_ref[...]).astype(BF16)

    row = pl.BlockSpec((tm, SSM_D_INNER), lambda i: (i, 0))
    return pl.pallas_call(
        body, out_shape=SDS((T, SSM_D_INNER), BF16), grid=(T // tm,),
        in_specs=[row, row, pl.BlockSpec((1, SSM_D_INNER), lambda i: (0, 0))], out_specs=row, name="b_gate_fwd",
        compiler_params=_params(("parallel",)),
    )(y, zx, nw)


def _gate_bwd(dgn, y, zx, nw):
    T = y.shape[0]
    tm = _pick(T, 256, 8)

    def body(d_ref, y_ref, z_ref, w_ref, dy_ref, dz_ref, dw_ref):
        z, yv, w = z_ref[...], y_ref[...], w_ref[...]
        sg = jax.nn.sigmoid(z)
        sz = z * sg
        gt = yv * sz
        gw = d_ref[...] * w
        dgts, dws = [], []
        for k in range(SSM_N_GROUPS):
            sl = slice(GN_W * k, GN_W * (k + 1))
            gk, gwk = gt[:, sl], gw[:, sl]
            rstd = lax.rsqrt(jnp.mean(gk * gk, axis=-1, keepdims=True) + NORM_EPS)
            dgts.append(rstd * gwk - gk * (rstd * rstd * rstd) * jnp.mean(gwk * gk, axis=-1, keepdims=True))
            dws.append(jnp.sum(d_ref[:, sl] * gk * rstd, axis=0, keepdims=True))
        dgt = jnp.concatenate(dgts, axis=-1)
        dy_ref[...] = dgt * sz
        dz_ref[...] = (dgt * yv * (sg * (1.0 + z * (1.0 - sg)))).astype(BF16)
        dw = jnp.concatenate(dws, axis=-1)

        @pl.when(pl.program_id(0) == 0)
        def _():
            dw_ref[...] = dw

        @pl.when(pl.program_id(0) > 0)
        def _():
            dw_ref[...] += dw

    row = pl.BlockSpec((tm, SSM_D_INNER), lambda i: (i, 0))
    vec = pl.BlockSpec((1, SSM_D_INNER), lambda i: (0, 0))
    return pl.pallas_call(
        body, out_shape=(SDS((T, SSM_D_INNER), F32), SDS((T, SSM_D_INNER), BF16), SDS((1, SSM_D_INNER), F32)), grid=(T // tm,),
        in_specs=[row, row, row, vec], out_specs=(row, row, vec), name="b_gate_bwd", compiler_params=_params(("arbitrary",)),
    )(dgn, y, zx, nw)


N_CHIPS = 4


def _dev_block(ref, kind, j, size):
    if kind == "slot":
        return ref.at[j]
    start = pl.multiple_of(j * size, size)
    nd = len(ref.shape)
    if kind == "col":
        return ref.at[(slice(None),) * (nd - 1) + (pl.ds(start, size),)]
    return ref.at[(slice(None),) * (nd - 2) + (pl.ds(start, size), slice(None))]


def _dma_sems(n, k):
    return [pltpu.SemaphoreType.DMA((n, k)), pltpu.SemaphoreType.DMA((n, k)), pltpu.SemaphoreType.DMA((n, k))]


def _place(shard, layer, kind, full_shape, dev, name):
    k, n = shard.shape[1:]
    tr = _pick(k, 512, 16)
    nb = k // tr

    def body(dev_ref, s_ref, o_ref):
        if kind == "slot":
            o_ref[0] = s_ref[0].astype(BF16)
        else:
            o_ref[...] = s_ref[0].astype(BF16)

    out_spec = {"slot": pl.BlockSpec((1, tr, n), lambda i, d: (d[0], i, 0)),
                "row": pl.BlockSpec((tr, n), lambda i, d: (d[0] * nb + i, 0)),
                "col": pl.BlockSpec((tr, n), lambda i, d: (i, d[0]))}[kind]
    return pl.pallas_call(
        body, out_shape=SDS(full_shape, BF16),
        grid_spec=pltpu.PrefetchScalarGridSpec(
            num_scalar_prefetch=1, grid=(nb,), in_specs=[pl.BlockSpec((1, tr, n), lambda i, d: (layer, i, 0))], out_specs=out_spec),
        name=name, compiler_params=_params(("arbitrary",)),
    )(dev, shard)


def _run_comm(comm, name):
    c_in = len(comm.inputs)

    def body(*refs):
        cins, couts, sems = refs[:c_in], refs[c_in:c_in + len(comm.out_shapes)], refs[c_in + len(comm.out_shapes):]
        for _, fn in comm.phases:
            fn(cins, couts, sems)

    return pl.pallas_call(
        body, out_shape=list(comm.out_shapes), in_specs=[ANY] * c_in, out_specs=[ANY] * len(comm.out_shapes),
        input_output_aliases=dict(comm.aliases), scratch_shapes=list(comm.sems), name=name,
    )(*comm.inputs)


def _gather_comm(items, mid=0.7):
    n = len(items)

    def tools(srcs, dsts, sems):
        send_sems, recv_sems, local_sems = sems
        px, py, pc = lax.axis_index("x"), lax.axis_index("y"), lax.axis_index("c")
        me, sibling = (px, py, pc), (px, py, 1 - pc)
        chips = [(1 - px, py), (px, 1 - py), (1 - px, 1 - py)]

        def blk(a, dev):
            return _dev_block(dsts[a], items[a][1], 4 * dev[0] + 2 * dev[1] + dev[2], items[a][2])

        def copy(a, k, block, to, src=None):
            return pltpu.make_async_remote_copy(
                src_ref=blk(a, block) if src is None else src, dst_ref=blk(a, block),
                send_sem=send_sems.at[a, k], recv_sem=recv_sems.at[a, k], device_id=to, device_id_type=MESH)

        def mine():
            return [pltpu.make_async_copy(srcs[a], blk(a, me), local_sems.at[a, 0]) for a in range(n) if not items[a][4]]

        def first():
            out = []
            for a in range(n):
                src = blk(a, me) if items[a][4] else srcs[a]
                out.append(copy(a, 0, me, sibling, src=src))
                out += [copy(a, 1 + j, me, (*chip, pc), src=src) for j, chip in enumerate(chips)]
            return out

        def passed():
            return [copy(a, 4 + j, (*chip, pc), sibling) for j, chip in enumerate(chips) for a in range(n)]

        return me, sibling, chips, pc, copy, mine, first, passed

    def start(srcs, dsts, sems):
        *_, mine, first, _ = tools(srcs, dsts, sems)
        for cp in mine() + first():
            cp.start()

    def forward(srcs, dsts, sems):
        me, _, chips, pc, copy, _, _, passed = tools(srcs, dsts, sems)
        fwd = passed()
        for j, chip in enumerate(chips):
            for a in range(n):
                copy(a, 1 + j, (*chip, pc), me).wait_recv()
                fwd[j * n + a].start()

    def finish(srcs, dsts, sems):
        me, sibling, chips, pc, copy, mine, first, passed = tools(srcs, dsts, sems)
        for a in range(n):
            copy(a, 0, sibling, me).wait_recv()
            for j, chip in enumerate(chips):
                copy(a, 4 + j, (*chip, 1 - pc), me).wait_recv()
        for cp in first() + passed():
            cp.wait_send()
        for cp in mine():
            cp.wait()

    return _Comm([it[0] for it in items], [SDS(it[3], it[0].dtype) for it in items],
                 {a: a for a in range(n) if items[a][4]}, _dma_sems(n, 7), [(0.0, start), (mid, forward), (1.0, finish)])


def _gather(items, name):
    return _run_comm(_gather_comm(items), name)


def _reduce_d2d(items, name):
    n = len(items)

    def body(*refs):
        gs, gots = refs[:n], refs[n:2 * n]
        send_sems, recv_sems, _ = refs[2 * n:]
        px, py, pc = lax.axis_index("x"), lax.axis_index("y"), lax.axis_index("c")
        copies = []
        for a in range(n):
            _, kind, size, _ = items[a]
            for q in range(N_CHIPS):
                copies.append(pltpu.make_async_remote_copy(
                    src_ref=_dev_block(gs[a], kind, 2 * q + 1 - pc, size), dst_ref=gots[a].at[q], send_sem=send_sems.at[a, q],
                    recv_sem=recv_sems.at[a, q], device_id=(px, py, 1 - pc), device_id_type=MESH))
        for cp in copies:
            cp.start()
        for cp in copies:
            cp.wait()

    return pl.pallas_call(
        body, out_shape=[SDS((N_CHIPS,) + tuple(it[3]), F32) for it in items], in_specs=[ANY] * n, out_specs=[ANY] * n,
        scratch_shapes=_dma_sems(n, N_CHIPS), name=name,
    )(*[it[0] for it in items])


def _pair_sum(g, got, kind, core, name):
    _, k, n = got.shape
    tr = _pick(k, max(16, STREAM_VMEM // (2 * n * 10)), 16)
    nb = k // tr

    def body(c_ref, g_ref, s_ref, o_ref):
        mine = g_ref[0] if kind == "slot" else g_ref[...]
        o_ref[0] = (mine + s_ref[0]).astype(BF16)

    g_spec = {"slot": pl.BlockSpec((1, tr, n), lambda q, i, c: (2 * q + c[0], i, 0)),
              "row": pl.BlockSpec((tr, n), lambda q, i, c: ((2 * q + c[0]) * nb + i, 0)),
              "col": pl.BlockSpec((tr, n), lambda q, i, c: (i, 2 * q + c[0]))}[kind]
    part = pl.BlockSpec((1, tr, n), lambda q, i, c: (q, i, 0))
    return pl.pallas_call(
        body, out_shape=SDS((N_CHIPS, k, n), BF16),
        grid_spec=pltpu.PrefetchScalarGridSpec(num_scalar_prefetch=1, grid=(N_CHIPS, nb), in_specs=[g_spec, part], out_specs=part),
        name=name, compiler_params=_params(("arbitrary", "arbitrary")),
    )(core, g, got)


def _reduce_ici_comm(parts):
    n = len(parts)

    def copies(ps, rs, sems, arriving):
        send_sems, recv_sems, _ = sems
        px, py, pc = lax.axis_index("x"), lax.axis_index("y"), lax.axis_index("c")
        my_chip = 2 * px + py
        out = []
        for a in range(n):
            for k in range(1, N_CHIPS):
                qx, qy = px ^ (k >> 1), py ^ (k & 1)
                q = 2 * qx + qy
                out.append(pltpu.make_async_remote_copy(
                    src_ref=ps[a].at[q], dst_ref=rs[a].at[q] if arriving else rs[a].at[my_chip], send_sem=send_sems.at[a, k - 1],
                    recv_sem=recv_sems.at[a, k - 1], device_id=(qx, qy, pc), device_id_type=MESH))
        return out

    def start(ps, rs, sems):
        for cp in copies(ps, rs, sems, False):
            cp.start()

    def finish(ps, rs, sems):
        for cp in copies(ps, rs, sems, True):
            cp.wait_recv()
        for cp in copies(ps, rs, sems, False):
            cp.wait_send()

    return _Comm(list(parts), [SDS(p.shape, p.dtype) for p in parts], {}, _dma_sems(n, N_CHIPS - 1),
                 [(0.0, start), (1.0, finish)])


def _adam_update(g, w, m, v):
    c1 = 1.0 - ADAM_B1 ** ADAM_STEP
    c2 = 1.0 - ADAM_B2 ** ADAM_STEP
    nm = ADAM_B1 * m + (1.0 - ADAM_B1) * g
    nv = ADAM_B2 * v + (1.0 - ADAM_B2) * (g * g)
    delta = -ADAM_LR * ((nm / c1) / (jnp.sqrt(nv / c2) + ADAM_EPS) + ADAM_WD * w)
    return delta, nm, nv


def _adamw(parts, recv, w, m, v, chip, name):
    R, C = w.shape
    row_bytes = 2 * C * (N_CHIPS * 2 + 7 * 4)
    tr = _pick(R, max(16, STREAM_VMEM // row_bytes), 16)

    def body(ch_ref, own_ref, r1_ref, r2_ref, r3_ref, w_ref, m_ref, v_ref, g_ref, d_ref, nm_ref, nv_ref):
        g = own_ref[0].astype(F32)
        for r_ref in (r1_ref, r2_ref, r3_ref):
            g = g + r_ref[0].astype(F32)
        g_ref[...] = g
        d_ref[...], nm_ref[...], nv_ref[...] = _adam_update(g, w_ref[...], m_ref[...], v_ref[...])

    row = pl.BlockSpec((tr, C), lambda i, ch: (i, 0))
    other = lambda k: pl.BlockSpec((1, tr, C), lambda i, ch: (ch[0] ^ k, i, 0))
    out = SDS((R, C), F32)
    return pl.pallas_call(
        body, out_shape=(out, out, out, out),
        grid_spec=pltpu.PrefetchScalarGridSpec(
            num_scalar_prefetch=1, grid=(R // tr,),
            in_specs=[pl.BlockSpec((1, tr, C), lambda i, ch: (ch[0], i, 0)), other(2), other(1), other(3), row, row, row],
            out_specs=(row, row, row, row)),
        name=name, compiler_params=_params(("arbitrary",)),
    )(chip, parts, recv, recv, recv, w, m, v)


def _small_adamw(gathered, ws, ms, vs):
    n = len(ws)

    def body(*refs):
        g_in, w_in, m_in, v_in = refs[:n], refs[n:2 * n], refs[2 * n:3 * n], refs[3 * n:4 * n]
        outs = refs[4 * n:]
        for i in range(n):
            g = g_in[i][0]
            for dev in range(1, N_DEV):
                g = g + g_in[i][dev]
            d, nm, nv = _adam_update(g, w_in[i][...], m_in[i][...], v_in[i][...])
            outs[i][...] = g
            outs[n + i][...] = d
            outs[2 * n + i][...] = nm
            outs[3 * n + i][...] = nv

    shapes = [SDS(w.shape, F32) for w in ws]
    outs = pl.pallas_call(body, out_shape=shapes * 4, name="small_adamw")(*gathered, *ws, *ms, *vs)
    return outs[:n], outs[n:2 * n], outs[2 * n:3 * n], outs[3 * n:]


W_NAMES = ("norm_mix_w", "norm_mlp_w", "a_w_qkv", "a_b_qkv", "a_sinks", "a_w_o", "a_b_o", "b_in_w", "b_conv_w", "b_conv_b",
           "b_dt_bias", "b_a_log", "b_d", "b_norm_w", "b_out_w", "c_w_qkv", "c_w_o", "mlp_w_up", "mlp_w_down", "final_norm_w")
BIG_KIND = {"a_w_qkv": "slot", "a_w_o": "row", "b_in_w": "slot", "b_out_w": "row", "c_w_qkv": "col", "c_w_o": "row",
            "mlp_w_up": "col", "mlp_w_down": "row"}
SMALL_SHARDED = {"a_b_qkv": 1, "a_b_o": 1, "b_conv_w": 2}
SMALL_REPLICATED = ("norm_mix_w", "norm_mlp_w", "a_sinks", "b_conv_b", "b_dt_bias", "b_a_log", "b_d", "b_norm_w", "final_norm_w")


def _layer_big(i):
    kind, j = i % 3, i // 3
    mix = {0: [("a_w_qkv", j), ("a_w_o", j)], 1: [("b_in_w", 0), ("b_out_w", 0)], 2: [("c_w_qkv", 0), ("c_w_o", 0)]}[kind]
    return mix + [("mlp_w_up", i), ("mlp_w_down", i)]


def _block_size(kind, shard2d):
    return {"slot": None, "row": shard2d[0], "col": shard2d[1]}[kind]


def _full2d(kind, shard2d):
    k, n = shard2d
    return {"slot": (N_DEV, k, n), "row": (N_DEV * k, n), "col": (k, N_DEV * n)}[kind]


def _from_slots(t, ax):
    s = t.shape[1:]
    return jnp.moveaxis(t, 0, ax).reshape(s[:ax] + (N_DEV * s[ax],) + s[ax + 1:])


def _to_slots(g, ax):
    s = g.shape
    return jnp.moveaxis(g.reshape(s[:ax] + (N_DEV, s[ax] // N_DEV) + s[ax + 1:]), ax, 0)


def _rope_tables(positions):
    half = HEAD_DIM // 2
    inv = ROPE_THETA ** (-(jnp.arange(LANES, dtype=jnp.int32) % half).astype(F32) / half)
    ang = positions.astype(F32).reshape(-1, 1) * inv
    return jnp.cos(ang), jnp.sin(ang)


def _swa_fwd(u, h, p, j, B, S, cos, sin, tag, plan=None):
    qkv = _matmul(u, p["a_w_qkv"][j], out_dtype=BF16, bias=p["a_b_qkv"][j][None], rope=(cos, sin),
                  rope_cols=A_Q_DIM + A_KV_DIM, tn=640, name=f"{tag}_qkv")
    o, lse = _attn_fwd(qkv, B, S, 1, n_heads=A_N_HEADS, n_kv=A_N_KV, q_col=0, k_col=A_Q_DIM, v_col=A_Q_DIM + A_KV_DIM,
                       max_dist=A_WINDOW - 1, sinks=p["a_sinks"][j], name=f"{tag}_attn", plan=plan)
    h1 = _matmul(o, p["a_w_o"][j], bias=p["a_b_o"][j][None], resid=h, name=f"{tag}_o")
    return h1, (qkv, o, lse)


def _swa_bwd(dh1, u, saved, p, j, B, S, cos, sin, tag, plan=None):
    qkv, o, lse = saved
    kw = dict(n_heads=A_N_HEADS, n_kv=A_N_KV, q_col=0, k_col=A_Q_DIM, v_col=A_Q_DIM + A_KV_DIM, max_dist=A_WINDOW - 1)
    g = {}
    do = _matmul(dh1, p["a_w_o"][j], tb=True, name=f"{tag}_do")
    g["a_w_o"] = _matmul(o, dh1, ta=True, name=f"{tag}_dwo")
    g["a_b_o"] = _colsum(dh1, f"{tag}_dbo")[0]
    sk = jnp.pad(p["a_sinks"][j], (0, LANES - A_N_HEADS))[None]
    delta, dob, dsink = _delta(do, o, lse, sk, name=f"{tag}_delta")
    g["a_sinks"] = dsink[0, :A_N_HEADS]
    dq = _attn_dq(qkv, dob, lse, delta, cos, sin, B, S, 1, name=f"{tag}_dq", plan=plan, **kw)
    dk, dv = _attn_dkv(qkv, dob, lse, delta, cos, sin, B, S, 1, name=f"{tag}_dkv", plan=plan, **kw)
    dqkv = jnp.concatenate([dq, dk, dv], axis=1)
    g["a_w_qkv"] = _matmul(u, dqkv, ta=True, tn=640, name=f"{tag}_dwqkv")
    g["a_b_qkv"] = _colsum(dqkv, f"{tag}_dbqkv")[0]
    du = _matmul(dqkv, p["a_w_qkv"][j], tb=True, tk=640, name=f"{tag}_du")
    return du, g


def _group_cols(gi, qkv):
    W = C_HEADS * HEAD_DIM
    if C_PATTERNS[gi][1] == 1:
        return qkv, (gi * W, (3 + gi) * W, (6 + gi) * W)
    part = jnp.concatenate([qkv[:, (3 * j + gi) * W:(3 * j + gi + 1) * W] for j in range(3)], axis=1)
    return part, (0, W, 2 * W)


def _dil_fwd(u, h, p, B, S, cos, sin, plan=None):
    W = C_HEADS * HEAD_DIM
    qkv = _matmul(u, p["c_w_qkv"][0], out_dtype=BF16, rope=(cos, sin), rope_cols=6 * W, name="c_qkv", plan=plan)
    os_, lses, parts = [], [], []
    for gi, (window, dil) in enumerate(C_PATTERNS):
        part, (qc, kc, vc) = _group_cols(gi, qkv)
        o, lse = _attn_fwd(part, B, S, dil, n_heads=C_HEADS, n_kv=C_HEADS, q_col=qc, k_col=kc, v_col=vc,
                           max_dist=window // dil, sinks=None, name=f"c_attn{gi}")
        os_.append(o)
        lses.append(lse)
        parts.append((part, (qc, kc, vc)))
    o, lse = _merge(os_, lses)
    h1 = _matmul(o, p["c_w_o"][0], resid=h, name="c_o")
    return h1, (parts, o, lse)


def _dil_bwd(dh1, u, saved, p, B, S, cos, sin, plan=None):
    parts, o, lse = saved
    g = {}
    do = _matmul(dh1, p["c_w_o"][0], tb=True, name="c_do")
    g["c_w_o"] = _matmul(o, dh1, ta=True, name="c_dwo")[None]
    delta, dob = _delta(do, o, name="c_delta")
    dqs, dks, dvs = [], [], []
    for gi, (window, dil) in enumerate(C_PATTERNS):
        part, (qc, kc, vc) = parts[gi]
        kw = dict(n_heads=C_HEADS, n_kv=C_HEADS, q_col=qc, k_col=kc, v_col=vc, max_dist=window // dil)
        dqs.append(_attn_dq(part, dob, lse, delta, cos, sin, B, S, dil, name=f"c_dq{gi}", **kw))
        dk, dv = _attn_dkv(part, dob, lse, delta, cos, sin, B, S, dil, name=f"c_dkv{gi}", **kw)
        dks.append(dk)
        dvs.append(dv)
    dqkv = jnp.concatenate(dqs + dks + dvs, axis=1)
    g["c_w_qkv"] = _matmul(u, dqkv, ta=True, name="c_dwqkv", plan=plan)[None]
    du = _matmul(dqkv, p["c_w_qkv"][0], tb=True, name="c_du")
    return du, g


def _ssm_params(p):
    par = jnp.stack([p["b_dt_bias"][0], p["b_a_log"][0], p["b_d"][0]], axis=0)
    prow = par.reshape(3, SSM_N_GROUPS, SSM_HG).transpose(1, 0, 2)
    return prow, prow.transpose(0, 2, 1)


def _mamba_fwd(u, h, p, B, S, plan=None):
    T = B * S
    G, HG = SSM_N_GROUPS, SSM_HG
    w_in = p["b_in_w"][0]
    nzx = SSM_D_INNER + SSM_CONV_DIM
    w_dt = jnp.pad(w_in[:, nzx:], ((0, 0), (0, LANES - SSM_N_HEADS)))
    zx = _matmul(u, w_in[:, :nzx], name="b_zx")
    dtraw = _matmul(u, w_dt, name="b_dt")[:, :SSM_N_HEADS]
    dtc = dtraw.reshape(B, S, G, HG).transpose(0, 2, 1, 3)
    dtr = dtraw.reshape(B, S, G, HG).transpose(0, 2, 3, 1)
    prow, pcol = _ssm_params(p)
    zx3 = zx.reshape(B, S, nzx)
    xc3 = _conv_fwd(zx3, p["b_conv_w"][0], p["b_conv_b"])
    y3, states = _ssd_fwd(xc3, dtc, dtr, prow, pcol, plan=plan)
    y = y3.reshape(T, SSM_D_INNER)
    gn = _gate_fwd(y, zx, p["b_norm_w"])
    h1 = _matmul(gn, p["b_out_w"][0], resid=h, name="b_out")
    return h1, (zx, dtc, dtr, xc3, y, states, gn, w_dt)


def _mamba_bwd(dh1, u, saved, p, B, S, plan=None):
    T = B * S
    zx, dtc, dtr, xc3, y, states, gn, w_dt = saved
    nzx = SSM_D_INNER + SSM_CONV_DIM
    w_in = p["b_in_w"][0]
    prow, pcol = _ssm_params(p)
    g = {}
    dgn = _matmul(dh1, p["b_out_w"][0], tb=True, name="b_dgn")
    g["b_out_w"] = _matmul(gn, dh1, ta=True, name="b_dwout")[None]
    dy, dz, dnw = _gate_bwd(dgn, y, zx, p["b_norm_w"])
    g["b_norm_w"] = dnw
    dx3, dB3, dC3, ddt, dpar = _ssd_bwd(xc3, dtc, dtr, prow, pcol, states, dy.reshape(B, S, SSM_D_INNER), plan=plan)
    dpar = dpar.transpose(1, 0, 2).reshape(3, SSM_N_HEADS)
    g["b_dt_bias"], g["b_a_log"], g["b_d"] = dpar[0:1], dpar[1:2], dpar[2:3]
    zx3 = zx.reshape(B, S, nzx)
    cw, cb = p["b_conv_w"][0], p["b_conv_b"]
    parts, dws, dbs = [], [], []
    for col0, dpart, nm in ((0, dx3, "b_conv_bwd_x"), (SSM_D_INNER, dB3, "b_conv_bwd_b"),
                            (SSM_D_INNER + SSM_BC_DIM, dC3, "b_conv_bwd_c")):
        dxp, dw, db = _conv_bwd(zx3, dpart, cw, cb, col0, nm)
        parts.append(dxp.reshape(T, -1))
        dws.append(dw)
        dbs.append(db)
    g["b_conv_w"] = jnp.concatenate(dws, axis=1)[None]
    g["b_conv_b"] = jnp.concatenate(dbs, axis=1)
    dzx = jnp.concatenate([dz] + parts, axis=1)
    ddtraw = ddt.transpose(0, 2, 1, 3).reshape(T, SSM_N_HEADS)
    ddtp = jnp.pad(ddtraw, ((0, 0), (0, LANES - SSM_N_HEADS)))
    dw_zx = _matmul(u, dzx, ta=True, name="b_dwzx")
    dw_dt = _matmul(u, ddtp, ta=True, name="b_dwdt")[:, :SSM_N_HEADS]
    g["b_in_w"] = jnp.concatenate([dw_zx, dw_dt], axis=1)[None]
    du = _matmul(dzx, w_in[:, :nzx], tb=True, name="b_du_zx")
    du = _matmul(ddtp, w_dt, tb=True, resid=du, name="b_du_dt")
    return du, g


def _local_step(x, positions, p, target, plan=None):
    B, S, D = x.shape
    T = B * S
    cos, sin = _rope_tables(positions)
    h = x.reshape(T, D)
    tape = []
    for i in range(DEPTH):
        kind, j = i % 3, i // 3
        u = _rmsnorm_fwd(h, p["norm_mix_w"][i], f"l{i}_norm_mix")
        if kind == 0:
            h1, saved = _swa_fwd(u, h, p, j, B, S, cos, sin, f"a{j}", plan)
        elif kind == 1:
            h1, saved = _mamba_fwd(u, h, p, B, S, plan)
        else:
            h1, saved = _dil_fwd(u, h, p, B, S, cos, sin, plan)
        u2 = _rmsnorm_fwd(h1, p["norm_mlp_w"][i], f"l{i}_norm_mlp")
        r, s = _matmul(u2, p["mlp_w_up"][i], out_dtype=BF16, relu2=True, name=f"l{i}_up", plan=plan)
        h2 = _matmul(s, p["mlp_w_down"][i], resid=h1, name=f"l{i}_down", plan=plan)
        tape.append((h, u, saved, h1, u2, r, s))
        h = h2
    dh, dwf, loss = _final_loss(h, target.reshape(T, D), p["final_norm_w"])
    grads = {"final_norm_w": dwf[0]}
    per_layer = {n: [None] * DEPTH for n in ("norm_mix_w", "norm_mlp_w", "mlp_w_up", "mlp_w_down")}
    a_grads = [None, None]
    for i in reversed(range(DEPTH)):
        kind, j = i % 3, i // 3
        h0, u, saved, h1, u2, r, s = tape[i]
        da = _matmul(dh, p["mlp_w_down"][i], tb=True, out_dtype=BF16, mul=r, mul_scale=2.0, name=f"l{i}_da")
        per_layer["mlp_w_down"][i] = _matmul(s, dh, ta=True, name=f"l{i}_dwdown")
        per_layer["mlp_w_up"][i] = _matmul(u2, da, ta=True, name=f"l{i}_dwup")
        du2 = _matmul(da, p["mlp_w_up"][i], tb=True, name=f"l{i}_du2")
        dh1, dnw = _rmsnorm_bwd(h1, du2, p["norm_mlp_w"][i], dh, f"l{i}_norm_mlp_bwd")
        per_layer["norm_mlp_w"][i] = dnw[0]
        if kind == 0:
            du, g = _swa_bwd(dh1, u, saved, p, j, B, S, cos, sin, f"a{j}", plan)
            a_grads[j] = g
            big = {"a_w_qkv": g["a_w_qkv"], "a_w_o": g["a_w_o"]}
        elif kind == 1:
            du, g = _mamba_bwd(dh1, u, saved, p, B, S, plan)
            grads.update(g)
            big = {"b_in_w": g["b_in_w"][0], "b_out_w": g["b_out_w"][0]}
        else:
            du, g = _dil_bwd(dh1, u, saved, p, B, S, cos, sin, plan)
            grads.update(g)
            big = {"c_w_qkv": g["c_w_qkv"][0], "c_w_o": g["c_w_o"][0]}
        dh, dnw = _rmsnorm_bwd(h0, du, p["norm_mix_w"][i], dh1, f"l{i}_norm_mix_bwd")
        per_layer["norm_mix_w"][i] = dnw[0]
        if plan is not None:
            plan.layer_grads(i, dict(big, mlp_w_up=per_layer["mlp_w_up"][i], mlp_w_down=per_layer["mlp_w_down"][i]))
    for n in ("norm_mix_w", "norm_mlp_w"):
        grads[n] = jnp.stack(per_layer[n], axis=0)
    for n in ("mlp_w_up", "mlp_w_down"):
        grads[n] = per_layer[n]
    for n in ("a_b_qkv", "a_sinks", "a_b_o"):
        grads[n] = jnp.stack([a_grads[0][n], a_grads[1][n]], axis=0)
    for n in ("a_w_qkv", "a_w_o"):
        grads[n] = [a_grads[0][n], a_grads[1][n]]
    for n in ("b_in_w", "b_out_w", "c_w_qkv", "c_w_o"):
        grads[n] = [grads[n][0]]
    return loss, dh.reshape(B, S, D), grads


GATHER_HOSTS = {"a0_attn": (1, ("mlp_w_up",)), "l0_up": (1, ("b_in_w",)), "l0_down": (1, ("b_out_w", "mlp_w_down")),
                "b_ssd_fwd": (2, None), "c_qkv": (3, None)}
REDUCE_HOSTS = {3: (("c_dwqkv", None),), 2: (("b_ssd_bwd", None),),
                1: (("a0_dq", ("b_in_w", "b_out_w")), ("a0_dkv", ("mlp_w_up", "mlp_w_down"))), 0: ()}


class _Plan:
    def __init__(self, w, m, v, p, dev, chip, core):
        self.w, self.m, self.v, self.p, self.dev, self.chip, self.core = w, m, v, p, dev, chip, core
        self.pending = {}
        self.res = {n: [[None] * w[n].shape[0] for _ in range(4)] for n in BIG_KIND}
        self._install(0, None)(_gather(self._gather_items(0, None), "gather_l0"))
        for host, (i, only) in GATHER_HOSTS.items():
            self.pending[host] = (_gather_comm(self._gather_items(i, only)), self._install(i, only))

    def _names(self, i, only):
        return [(n, l) for n, l in _layer_big(i) if only is None or n in only]

    def _gather_items(self, i, only):
        items = []
        for n, l in self._names(i, only):
            kind, s2 = BIG_KIND[n], self.w[n].shape[1:]
            placed = _place(self.w[n], l, kind, _full2d(kind, s2), self.dev, f"place_l{i}_{n}")
            items.append((placed, kind, _block_size(kind, s2), _full2d(kind, s2), True))
        return items

    def _install(self, i, only):
        def done(fulls):
            for (n, l), t in zip(self._names(i, only), fulls):
                self.p[n][l] = _from_slots(t, 1) if BIG_KIND[n] == "slot" else t
        return done

    def take(self, host):
        return self.pending[host][0] if host in self.pending else None

    def give(self, host, results):
        self.pending.pop(host)[1](results)

    def layer_grads(self, i, grads):
        names = self._names(i, None)
        items = []
        for n, _ in names:
            kind, s2 = BIG_KIND[n], self.w[n].shape[1:]
            items.append((_to_slots(grads[n], 1) if kind == "slot" else grads[n], kind, _block_size(kind, s2), s2))
        sib = _reduce_d2d(items, f"reduce_d2d_l{i}")
        parts = {n: _pair_sum(it[0], s, it[1], self.core, f"pair_sum_l{i}_{n}") for (n, _), it, s in zip(names, items, sib)}

        def update(sel):
            def done(recv):
                for (n, l), r in zip(sel, recv):
                    outs = _adamw(parts[n], r, self.w[n][l], self.m[n][l], self.v[n][l], self.chip, f"adamw_l{i}_{n}")
                    for k, o in enumerate(outs):
                        self.res[n][k][l] = o
            return done

        if not REDUCE_HOSTS[i]:
            update(names)(_run_comm(_reduce_ici_comm([parts[n] for n, _ in names]), f"reduce_ici_l{i}"))
        for host, only in REDUCE_HOSTS[i]:
            sel = self._names(i, only)
            self.pending[host] = (_reduce_ici_comm([parts[n] for n, _ in sel]), update(sel))

    def flush(self):
        for host in list(self.pending):
            comm, done = self.pending.pop(host)
            done(_run_comm(comm, f"comm_{host}"))


def kernel(x, positions, norm_mix_w, norm_mlp_w, a_w_qkv, a_b_qkv, a_sinks, a_w_o, a_b_o, b_in_w, b_conv_w, b_conv_b, b_dt_bias, b_a_log, b_d, b_norm_w, b_out_w, c_w_qkv, c_w_o, mlp_w_up, mlp_w_down, final_norm_w, loss_target, m_norm_mix_w, m_norm_mlp_w, m_a_w_qkv, m_a_b_qkv, m_a_sinks, m_a_w_o, m_a_b_o, m_b_in_w, m_b_conv_w, m_b_conv_b, m_b_dt_bias, m_b_a_log, m_b_d, m_b_norm_w, m_b_out_w, m_c_w_qkv, m_c_w_o, m_mlp_w_up, m_mlp_w_down, m_final_norm_w, v_norm_mix_w, v_norm_mlp_w, v_a_w_qkv, v_a_b_qkv, v_a_sinks, v_a_w_o, v_a_b_o, v_b_in_w, v_b_conv_w, v_b_conv_b, v_b_dt_bias, v_b_a_log, v_b_d, v_b_norm_w, v_b_out_w, v_c_w_qkv, v_c_w_o, v_mlp_w_up, v_mlp_w_down, v_final_norm_w):
    w = dict(zip(W_NAMES, (norm_mix_w, norm_mlp_w, a_w_qkv, a_b_qkv, a_sinks, a_w_o, a_b_o, b_in_w, b_conv_w, b_conv_b,
                           b_dt_bias, b_a_log, b_d, b_norm_w, b_out_w, c_w_qkv, c_w_o, mlp_w_up, mlp_w_down, final_norm_w)))
    m = dict(zip(W_NAMES, (m_norm_mix_w, m_norm_mlp_w, m_a_w_qkv, m_a_b_qkv, m_a_sinks, m_a_w_o, m_a_b_o, m_b_in_w,
                           m_b_conv_w, m_b_conv_b, m_b_dt_bias, m_b_a_log, m_b_d, m_b_norm_w, m_b_out_w, m_c_w_qkv, m_c_w_o,
                           m_mlp_w_up, m_mlp_w_down, m_final_norm_w)))
    v = dict(zip(W_NAMES, (v_norm_mix_w, v_norm_mlp_w, v_a_w_qkv, v_a_b_qkv, v_a_sinks, v_a_w_o, v_a_b_o, v_b_in_w,
                           v_b_conv_w, v_b_conv_b, v_b_dt_bias, v_b_a_log, v_b_d, v_b_norm_w, v_b_out_w, v_c_w_qkv, v_c_w_o,
                           v_mlp_w_up, v_mlp_w_down, v_final_norm_w)))
    px, py, pc = lax.axis_index("x"), lax.axis_index("y"), lax.axis_index("c")
    me = 4 * px + 2 * py + pc
    dev, chip, core = (t.astype(jnp.int32).reshape(1) for t in (me, 2 * px + py, pc))

    trio = tuple(SMALL_SHARDED)
    got = _gather([(d[n], "slot", None, (N_DEV,) + d[n].shape, False) for n in trio for d in (w, m, v)], "gather_small")
    slots = {n: got[3 * i:3 * i + 3] for i, n in enumerate(trio)}
    p = {n: w[n] for n in SMALL_REPLICATED}
    for n in trio:
        p[n] = _from_slots(slots[n][0], SMALL_SHARDED[n])
    for n in BIG_KIND:
        p[n] = [None] * w[n].shape[0]
    plan = _Plan(w, m, v, p, dev, chip, core)
    loss_part, dx, grads = _local_step(x, positions, p, loss_target, plan)
    loss = lax.psum(loss_part[0, 0], AXES)
    plan.flush()
    out = {n: [jnp.stack(plan.res[n][k], axis=0) for k in range(4)] for n in BIG_KIND}

    small = SMALL_REPLICATED + trio
    as2d = lambda t: t.reshape(1, -1) if t.ndim == 1 else t
    g_sm = [as2d(grads[n]) for n in SMALL_REPLICATED] + [_to_slots(grads[n].reshape(p[n].shape), SMALL_SHARDED[n]) for n in trio]
    gathered = _gather([(g, "slot", None, (N_DEV,) + g.shape, False) for g in g_sm], "gather_small_grads")
    ws = [as2d(w[n]) for n in SMALL_REPLICATED] + [slots[n][0] for n in trio]
    ms = [as2d(m[n]) for n in SMALL_REPLICATED] + [slots[n][1] for n in trio]
    vs = [as2d(v[n]) for n in SMALL_REPLICATED] + [slots[n][2] for n in trio]
    sm_out = _small_adamw(gathered, ws, ms, vs)
    for i, n in enumerate(small):
        if n in SMALL_SHARDED:
            out[n] = [lax.dynamic_index_in_dim(sm_out[k][i], me, 0, keepdims=False) for k in range(4)]
        else:
            out[n] = [sm_out[k][i].reshape(w[n].shape) for k in range(4)]
    return (loss, dx, *[out[n][0] for n in W_NAMES], *[out[n][1] for n in W_NAMES], *[out[n][2] for n in W_NAMES],
            *[out[n][3] for n in W_NAMES])
```

```python
import functools
import math

import jax
import jax.numpy as jnp
import numpy as np
from jax import lax
from jax.experimental import pallas as pl
from jax.experimental.pallas import tpu as pltpu

F32 = jnp.float32
BF16 = jnp.bfloat16
SDS = jax.ShapeDtypeStruct

D_MODEL = 1024
DEPTH = 4
BLOCK = 128
ROPE_THETA = 10000.0
NORM_EPS = 1e-5
HEAD_DIM = 64
A_N_HEADS = 16
A_N_KV = 2
A_WINDOW = 128
A_Q_DIM = 1024
A_KV_DIM = 128
SSM_D_INNER = 2048
SSM_N_HEADS = 32
SSM_N_GROUPS = 8
SSM_HG = 4
SSM_D_STATE = 128
SSM_CONV = 4
SSM_CHUNK = 128
SSM_BC_DIM = 1024
SSM_CONV_DIM = 4096
C_PATTERNS = ((128, 1), (512, 4), (2048, 16))
C_HEADS = 16
ADAM_LR, ADAM_B1, ADAM_B2, ADAM_EPS, ADAM_WD, ADAM_STEP = 0.001, 0.9, 0.999, 1e-08, 0.01, 10

N_DEV = 8
AXES = ("x", "y", "c")
LANES = 128
VMEM_LIMIT = 56 * 1024 * 1024
STREAM_VMEM = 16 * 1024 * 1024
NEG = -1e30

NN = (((1,), (0,)), ((), ()))
NT = (((1,), (1,)), ((), ()))
TN = (((0,), (0,)), ((), ()))
HI = lax.Precision.HIGHEST


def _pick(n, cap, mult=LANES):
    best = None
    for t in range(mult, min(n, cap) + 1, mult):
        if n % t == 0:
            best = t
    return best if best is not None else n


def _params(sem):
    return pltpu.CompilerParams(dimension_semantics=sem, vmem_limit_bytes=VMEM_LIMIT)


def _bf(x):
    return x if x.dtype == BF16 else x.astype(BF16)


def _rot_half(y):
    n = y.shape[-1]
    lane = lax.broadcasted_iota(jnp.int32, y.shape, y.ndim - 1)
    return jnp.where((lane % HEAD_DIM) < HEAD_DIM // 2, -pltpu.roll(y, n - 32, y.ndim - 1), pltpu.roll(y, 32, y.ndim - 1))


def _rope(y, cos, sin, sign):
    reps = y.shape[-1] // LANES
    c = jnp.tile(cos, (1, reps)) if reps > 1 else cos
    s = jnp.tile(sin, (1, reps)) if reps > 1 else sin
    return y * c + sign * (_rot_half(y) * s)


MESH = pl.DeviceIdType.MESH
ANY = pl.BlockSpec(memory_space=pl.ANY)


class _Comm:
    def __init__(self, inputs, out_shapes, aliases, sems, phases):
        self.inputs, self.out_shapes, self.aliases, self.sems, self.phases = inputs, out_shapes, aliases, sems, phases


def _pc(body, args, *, out_shape, grid, in_specs, out_specs, name, sem, scratch_shapes=(), comm=None):
    single = not isinstance(out_shape, (tuple, list))
    outs, ospecs = ([out_shape], [out_specs]) if single else (list(out_shape), list(out_specs))
    unpack = (lambda r: r[0]) if single else (lambda r: tuple(r))
    if comm is None:
        res = pl.pallas_call(body, out_shape=outs, grid=grid, in_specs=list(in_specs), out_specs=ospecs,
                             scratch_shapes=list(scratch_shapes), name=name, compiler_params=_params(sem))(*args)
        return unpack(res)
    n_in, n_out, n_scr = len(in_specs), len(outs), len(scratch_shapes)
    c_in, c_out = len(comm.inputs), len(comm.out_shapes)
    total = math.prod(grid)
    steps = [min(total - 1, int(f * total)) for f, _ in comm.phases[:-1]]

    def wrapped(*refs):
        ins, cins = refs[:n_in], refs[n_in:n_in + c_in]
        o = refs[n_in + c_in:n_in + c_in + n_out]
        couts = refs[n_in + c_in + n_out:n_in + c_in + n_out + c_out]
        rest = refs[n_in + c_in + n_out + c_out:]
        scr, csems = rest[:n_scr], rest[n_scr:]
        step = pl.program_id(0)
        for ax in range(1, len(grid)):
            step = step * grid[ax] + pl.program_id(ax)
        for (_, fn), st in zip(comm.phases[:-1], steps):
            @pl.when(step == st)
            def _(fn=fn):
                fn(cins, couts, csems)
        body(*ins, *o, *scr)

        @pl.when(step == total - 1)
        def _():
            comm.phases[-1][1](cins, couts, csems)

    res = pl.pallas_call(
        wrapped, out_shape=outs + list(comm.out_shapes), grid=grid, in_specs=list(in_specs) + [ANY] * c_in,
        out_specs=ospecs + [ANY] * c_out, scratch_shapes=list(scratch_shapes) + list(comm.sems),
        input_output_aliases={n_in + i: n_out + j for i, j in comm.aliases.items()}, name=name,
        compiler_params=_params(("arbitrary",) * len(grid)),
    )(*args, *comm.inputs)
    return unpack(res[:n_out]), list(res[n_out:])


def _hosted(plan, name, run):
    comm = plan.take(name) if plan is not None else None
    if comm is None:
        return run(None)
    res, extra = run(comm)
    plan.give(name, extra)
    return res


def _matmul(a, b, *, ta=False, tb=False, out_dtype=F32, bias=None, resid=None, mul=None, mul_scale=1.0,
            relu2=False, rope=None, rope_cols=0, tm=1024, tn=1024, tk=1024, name="mm", plan=None):
    M = a.shape[1] if ta else a.shape[0]
    K = a.shape[0] if ta else a.shape[1]
    N = b.shape[0] if tb else b.shape[1]
    assert (b.shape[1] if tb else b.shape[0]) == K
    tm, tn, tk = _pick(M, tm), _pick(N, tn), _pick(K, tk)
    nk = K // tk
    dims = (((0 if ta else 1,), (1 if tb else 0,)), ((), ()))

    def body(*refs):
        it = iter(refs)
        a_ref, b_ref = next(it), next(it)
        bias_ref = next(it) if bias is not None else None
        resid_ref = next(it) if resid is not None else None
        mul_ref = next(it) if mul is not None else None
        cos_ref, sin_ref = (next(it), next(it)) if rope is not None else (None, None)
        o_ref = next(it)
        o2_ref = next(it) if relu2 else None
        acc_ref = next(it)
        k = pl.program_id(2)
        part = lax.dot_general(_bf(a_ref[...]), _bf(b_ref[...]), dims, preferred_element_type=F32)

        @pl.when(k == 0)
        def _():
            acc_ref[...] = part

        @pl.when(k > 0)
        def _():
            acc_ref[...] += part

        @pl.when(k == nk - 1)
        def _():
            y = acc_ref[...]
            if bias_ref is not None:
                y = y + bias_ref[...]
            if rope is not None:
                col = pl.program_id(1) * tn + lax.broadcasted_iota(jnp.int32, y.shape, 1)
                y = jnp.where(col < rope_cols, _rope(y, cos_ref[...], sin_ref[...], 1.0), y)
            if mul_ref is not None:
                y = y * (mul_ref[...].astype(F32) * mul_scale)
            if resid_ref is not None:
                y = y + resid_ref[...]
            if relu2:
                r = jnp.maximum(y, 0.0)
                o_ref[...] = r.astype(o_ref.dtype)
                o2_ref[...] = (r * r).astype(o2_ref.dtype)
            else:
                o_ref[...] = y.astype(o_ref.dtype)

    a_spec = pl.BlockSpec((tk, tm), lambda i, j, k: (k, i)) if ta else pl.BlockSpec((tm, tk), lambda i, j, k: (i, k))
    b_spec = pl.BlockSpec((tn, tk), lambda i, j, k: (j, k)) if tb else pl.BlockSpec((tk, tn), lambda i, j, k: (k, j))
    mn_spec = pl.BlockSpec((tm, tn), lambda i, j, k: (i, j))
    in_specs, args = [a_spec, b_spec], [a, b]
    if bias is not None:
        in_specs.append(pl.BlockSpec((1, tn), lambda i, j, k: (0, j)))
        args.append(bias)
    if resid is not None:
        in_specs.append(mn_spec)
        args.append(resid)
    if mul is not None:
        in_specs.append(mn_spec)
        args.append(mul)
    if rope is not None:
        in_specs += [pl.BlockSpec((tm, LANES), lambda i, j, k: (i, 0))] * 2
        args += [rope[0], rope[1]]
    out_shape = SDS((M, N), out_dtype)
    out_specs = mn_spec
    if relu2:
        out_shape, out_specs = (out_shape, out_shape), (mn_spec, mn_spec)
    return _hosted(plan, name, lambda comm: _pc(
        body, args, out_shape=out_shape, grid=(M // tm, N // tn, nk), in_specs=in_specs, out_specs=out_specs,
        scratch_shapes=[pltpu.VMEM((tm, tn), F32)], name=name, sem=("parallel", "parallel", "arbitrary"), comm=comm))


def _colsum(x, name):
    T, N = x.shape
    tm = _pick(T, 1024, 8)

    def body(x_ref, o_ref):
        s = jnp.sum(x_ref[...].astype(F32), axis=0, keepdims=True)

        @pl.when(pl.program_id(0) == 0)
        def _():
            o_ref[...] = s

        @pl.when(pl.program_id(0) > 0)
        def _():
            o_ref[...] += s

    return pl.pallas_call(
        body, out_shape=SDS((1, N), F32), grid=(T // tm,),
        in_specs=[pl.BlockSpec((tm, N), lambda i: (i, 0))], out_specs=pl.BlockSpec((1, N), lambda i: (0, 0)),
        name=name, compiler_params=_params(("arbitrary",)),
    )(x)


def _rmsnorm_fwd(h, w, name):
    T, D = h.shape
    tm = _pick(T, 512, 8)

    def body(h_ref, w_ref, o_ref):
        x = h_ref[...]
        rstd = lax.rsqrt(jnp.mean(x * x, axis=-1, keepdims=True) + NORM_EPS)
        o_ref[...] = (x * rstd * w_ref[...]).astype(BF16)

    return pl.pallas_call(
        body, out_shape=SDS((T, D), BF16), grid=(T // tm,),
        in_specs=[pl.BlockSpec((tm, D), lambda i: (i, 0)), pl.BlockSpec((1, D), lambda i: (0, 0))],
        out_specs=pl.BlockSpec((tm, D), lambda i: (i, 0)), name=name, compiler_params=_params(("parallel",)),
    )(h, w.reshape(1, D))


def _rmsnorm_bwd(h, du, w, dres, name):
    T, D = h.shape
    tm = _pick(T, 512, 8)

    def body(h_ref, du_ref, w_ref, dres_ref, dh_ref, dw_ref):
        x = h_ref[...]
        du_ = du_ref[...].astype(F32)
        rstd = lax.rsqrt(jnp.mean(x * x, axis=-1, keepdims=True) + NORM_EPS)
        g = du_ * w_ref[...]
        dh_ref[...] = dres_ref[...] + rstd * g - x * (rstd * rstd * rstd) * jnp.mean(g * x, axis=-1, keepdims=True)
        dw = jnp.sum(du_ * x * rstd, axis=0, keepdims=True)

        @pl.when(pl.program_id(0) == 0)
        def _():
            dw_ref[...] = dw

        @pl.when(pl.program_id(0) > 0)
        def _():
            dw_ref[...] += dw

    row = pl.BlockSpec((tm, D), lambda i: (i, 0))
    vec = pl.BlockSpec((1, D), lambda i: (0, 0))
    return pl.pallas_call(
        body, out_shape=(SDS((T, D), F32), SDS((1, D), F32)), grid=(T // tm,),
        in_specs=[row, row, vec, row], out_specs=(row, vec), name=name, compiler_params=_params(("arbitrary",)),
    )(h, du, w.reshape(1, D), dres)


def _final_loss(h, target, w):
    T, D = h.shape
    tm = _pick(T, 512, 8)

    def body(h_ref, t_ref, w_ref, dh_ref, dw_ref, loss_ref):
        x = h_ref[...]
        rstd = lax.rsqrt(jnp.mean(x * x, axis=-1, keepdims=True) + NORM_EPS)
        xn = x * rstd
        err = xn * w_ref[...] - t_ref[...]
        part = 0.5 * jnp.sum(jnp.mean(err * err, axis=-1, keepdims=True), axis=0, keepdims=True)
        dy = err * (1.0 / D)
        g = dy * w_ref[...]
        dh_ref[...] = rstd * g - x * (rstd * rstd * rstd) * jnp.mean(g * x, axis=-1, keepdims=True)
        dw = jnp.sum(dy * xn, axis=0, keepdims=True)
        lp = jnp.broadcast_to(part, (1, LANES))

        @pl.when(pl.program_id(0) == 0)
        def _():
            dw_ref[...] = dw
            loss_ref[...] = lp

        @pl.when(pl.program_id(0) > 0)
        def _():
            dw_ref[...] += dw
            loss_ref[...] += lp

    row = pl.BlockSpec((tm, D), lambda i: (i, 0))
    vec = pl.BlockSpec((1, D), lambda i: (0, 0))
    return pl.pallas_call(
        body, out_shape=(SDS((T, D), F32), SDS((1, D), F32), SDS((1, LANES), F32)), grid=(T // tm,),
        in_specs=[row, row, vec], out_specs=(row, vec, pl.BlockSpec((1, LANES), lambda i: (0, 0))),
        name="final_loss", compiler_params=_params(("arbitrary",)),
    )(h, target, w.reshape(1, D))


def _band_mask(i_blk, max_dist, first_ok):
    qi = lax.broadcasted_iota(jnp.int32, (BLOCK, 2 * BLOCK), 0)
    kj = lax.broadcasted_iota(jnp.int32, (BLOCK, 2 * BLOCK), 1)
    dist = qi + BLOCK - kj
    ok = (dist >= 0) & (dist <= max_dist)
    return ok & ((kj >= BLOCK) | first_ok)


def _pair(t, i):
    return t[:, LANES * i:LANES * (i + 1)]


def _low_half(shape):
    return lax.broadcasted_iota(jnp.int32, shape, len(shape) - 1) < HEAD_DIM


def _stack_heads(t):
    lo = _low_half(t.shape)
    z = jnp.zeros_like(t)
    return jnp.concatenate([jnp.where(lo, t, z), jnp.where(lo, z, t)], axis=0)


def _swap_halves(t):
    return jnp.concatenate([t[:, HEAD_DIM:], t[:, :HEAD_DIM]], axis=1)


def _kv_operand(kv, kv_swapped, h0, n_kv, n_heads):
    R = n_heads // n_kv
    if R == 1:
        return _pair(kv, h0 // 2)
    assert kv.shape[1] == LANES and R % 2 == 0, "grouped queries: one 128-lane tile of kv heads, both heads of a pair in one group"
    g = h0 // R
    t, ts = _pair(kv, g // 2), _pair(kv_swapped, g // 2)
    lo = _low_half(t.shape)
    return jnp.where(lo, t, ts) if g % 2 == 0 else jnp.where(lo, ts, t)


def _lane_place(cols):
    m = cols[0].shape[0]
    lane = lax.broadcasted_iota(jnp.int32, (m, LANES), 1)
    out = jnp.zeros((m, LANES), F32)
    for h, c in enumerate(cols):
        out = jnp.where(lane == h, c, out)
    return out


def _attn_specs(B, S, d, C, n_heads, n_kv, q_col, k_col, v_col):
    kvw = n_kv * HEAD_DIM
    qw = n_heads * HEAD_DIM
    cq, ck = (C // qw if d > 1 else 0), (C // kvw if d > 1 else 0)
    q_spec = pl.BlockSpec((1, BLOCK, qw), lambda b, r, i: (b, i, r * cq + q_col // qw))
    kc = pl.BlockSpec((1, BLOCK, kvw), lambda b, r, i: (b, i, r * ck + k_col // kvw))
    kp = pl.BlockSpec((1, BLOCK, kvw), lambda b, r, i: (b, jnp.maximum(i - 1, 0), r * ck + k_col // kvw))
    vc = pl.BlockSpec((1, BLOCK, kvw), lambda b, r, i: (b, i, r * ck + v_col // kvw))
    vp = pl.BlockSpec((1, BLOCK, kvw), lambda b, r, i: (b, jnp.maximum(i - 1, 0), r * ck + v_col // kvw))
    return q_spec, kp, kc, vp, vc


def _attn_fwd(qkv, B, S, d, *, n_heads, n_kv, q_col, k_col, v_col, max_dist, sinks, name, plan=None):
    C = qkv.shape[1]
    Ls = S // d
    nb = Ls // BLOCK
    qw = n_heads * HEAD_DIM
    R = n_heads // n_kv
    qkv3 = qkv.reshape(B, Ls, d * C)
    scale = HEAD_DIM ** -0.5

    def body(*refs):
        if sinks is not None:
            sink_ref, q_ref, kp_ref, kc_ref, vp_ref, vc_ref, o_ref, lse_ref = refs
        else:
            q_ref, kp_ref, kc_ref, vp_ref, vc_ref, o_ref, lse_ref = refs
        i = pl.program_id(2)
        mask1 = _band_mask(i, max_dist, i > 0)
        mask = jnp.concatenate([mask1, mask1], axis=0)
        q = q_ref[0]
        kk = jnp.concatenate([kp_ref[0], kc_ref[0]], axis=0)
        vv = jnp.concatenate([vp_ref[0], vc_ref[0]], axis=0)
        kks, vvs = (_swap_halves(kk), _swap_halves(vv)) if R > 1 else (None, None)
        lo = _low_half((BLOCK, LANES))
        top = lax.broadcasted_iota(jnp.int32, (2 * BLOCK, 1), 0) < BLOCK
        lses, tiles = [], []
        for t in range(n_heads // 2):
            k2 = _kv_operand(kk, kks, 2 * t, n_kv, n_heads)
            v2 = _kv_operand(vv, vvs, 2 * t, n_kv, n_heads)
            s = lax.dot_general(_stack_heads(_pair(q, t)), k2, NT, preferred_element_type=F32) * scale
            s = jnp.where(mask, s, NEG)
            m = jnp.max(s, axis=-1, keepdims=True)
            if sinks is not None:
                sk = jnp.where(top, sink_ref[2 * t], sink_ref[2 * t + 1])
                m = jnp.maximum(m, sk)
            p = jnp.exp(s - m)
            den = jnp.sum(p, axis=-1, keepdims=True)
            if sinks is not None:
                den = den + jnp.exp(sk - m)
            lse2 = m + jnp.log(den)
            o2 = jnp.dot((p / den).astype(BF16), v2, preferred_element_type=F32)
            tiles.append(jnp.where(lo, o2[:BLOCK], o2[BLOCK:]))
            lses += [lse2[:BLOCK], lse2[BLOCK:]]
        o_ref[0] = jnp.concatenate(tiles, axis=-1)
        lse_ref[0] = _lane_place(lses)

    specs = list(_attn_specs(B, S, d, C, n_heads, n_kv, q_col, k_col, v_col))
    args = [qkv3] * 5
    if sinks is not None:
        specs = [pl.BlockSpec(memory_space=pltpu.SMEM)] + specs
        args = [sinks] + args
    o3, lse3 = _hosted(plan, name, lambda comm: _pc(
        body, args, out_shape=(SDS((B, Ls, d * qw), F32), SDS((B, Ls, d * LANES), F32)), grid=(B, d, nb), in_specs=specs,
        out_specs=(pl.BlockSpec((1, BLOCK, qw), lambda b, r, i: (b, i, r)), pl.BlockSpec((1, BLOCK, LANES), lambda b, r, i: (b, i, r))),
        name=name, sem=("parallel", "parallel", "parallel"), comm=comm))
    return o3.reshape(B * S, qw), lse3.reshape(B * S, LANES)


def _attn_dq(qkv, do, lse, delta, cos, sin, B, S, d, *, n_heads, n_kv, q_col, k_col, v_col, max_dist, name, plan=None):
    C = qkv.shape[1]
    Ls = S // d
    nb = Ls // BLOCK
    qw = n_heads * HEAD_DIM
    R = n_heads // n_kv
    scale = HEAD_DIM ** -0.5

    def body(q_ref, kp_ref, kc_ref, vp_ref, vc_ref, do_ref, lse_ref, dl_ref, cos_ref, sin_ref, dq_ref):
        i = pl.program_id(2)
        mask1 = _band_mask(i, max_dist, i > 0)
        mask = jnp.concatenate([mask1, mask1], axis=0)
        q = q_ref[0]
        do_ = do_ref[0]
        kk = jnp.concatenate([kp_ref[0], kc_ref[0]], axis=0)
        vv = jnp.concatenate([vp_ref[0], vc_ref[0]], axis=0)
        kks, vvs = (_swap_halves(kk), _swap_halves(vv)) if R > 1 else (None, None)
        lo = _low_half((BLOCK, LANES))
        lse_t, dl_t = lse_ref[0], dl_ref[0]
        tiles = []
        for t in range(n_heads // 2):
            k2 = _kv_operand(kk, kks, 2 * t, n_kv, n_heads)
            v2 = _kv_operand(vv, vvs, 2 * t, n_kv, n_heads)
            lse2 = jnp.concatenate([lse_t[:, 2 * t:2 * t + 1], lse_t[:, 2 * t + 1:2 * t + 2]], axis=0)
            dl2 = jnp.concatenate([dl_t[:, 2 * t:2 * t + 1], dl_t[:, 2 * t + 1:2 * t + 2]], axis=0)
            s = lax.dot_general(_stack_heads(_pair(q, t)), k2, NT, preferred_element_type=F32) * scale
            p = jnp.where(mask, jnp.exp(s - lse2), 0.0)
            dp = lax.dot_general(_stack_heads(_pair(do_, t)), v2, NT, preferred_element_type=F32)
            ds = p * (dp - dl2)
            dq2 = jnp.dot(ds.astype(BF16), k2, preferred_element_type=F32) * scale
            tiles.append(jnp.where(lo, dq2[:BLOCK], dq2[BLOCK:]))
        dq = jnp.concatenate(tiles, axis=-1)
        dq_ref[0] = _rope(dq, cos_ref[0], sin_ref[0], -1.0).astype(BF16)

    qs, kp, kc, vp, vc = _attn_specs(B, S, d, C, n_heads, n_kv, q_col, k_col, v_col)
    row_q = pl.BlockSpec((1, BLOCK, qw), lambda b, r, i: (b, i, r))
    row_l = pl.BlockSpec((1, BLOCK, LANES), lambda b, r, i: (b, i, r))
    qkv3 = qkv.reshape(B, Ls, d * C)
    v3 = lambda t, w: t.reshape(B, Ls, d * w)
    args = (qkv3, qkv3, qkv3, qkv3, qkv3, v3(do, qw), v3(lse, LANES), v3(delta, LANES), v3(cos, LANES), v3(sin, LANES))
    dq3 = _hosted(plan, name, lambda comm: _pc(
        body, args, out_shape=SDS((B, Ls, d * qw), BF16), grid=(B, d, nb),
        in_specs=[qs, kp, kc, vp, vc, row_q, row_l, row_l, row_l, row_l], out_specs=row_q,
        name=name, sem=("parallel", "parallel", "parallel"), comm=comm))
    return dq3.reshape(B * S, qw)


def _attn_dkv(qkv, do, lse, delta, cos, sin, B, S, d, *, n_heads, n_kv, q_col, k_col, v_col, max_dist, name, plan=None):
    C = qkv.shape[1]
    Ls = S // d
    nb = Ls // BLOCK
    qw = n_heads * HEAD_DIM
    kvw = n_kv * HEAD_DIM
    R = n_heads // n_kv
    scale = HEAD_DIM ** -0.5
    cq, ck = (C // qw if d > 1 else 0), (C // kvw if d > 1 else 0)

    def body(k_ref, v_ref, q0_ref, q1_ref, do0_ref, do1_ref, lse0_ref, lse1_ref, dl0_ref, dl1_ref, cos_ref, sin_ref,
             dk_ref, dv_ref):
        j = pl.program_id(2)
        kj = lax.broadcasted_iota(jnp.int32, (BLOCK, BLOCK), 0)
        qi = lax.broadcasted_iota(jnp.int32, (BLOCK, BLOCK), 1)
        dist0 = qi - kj
        dist1 = qi + BLOCK - kj
        mask0 = (dist0 >= 0) & (dist0 <= max_dist)
        mask1 = (dist1 <= max_dist) & (j + 1 < nb)
        kb, vb = k_ref[0], v_ref[0]
        kbs, vbs = (_swap_halves(kb), _swap_halves(vb)) if R > 1 else (None, None)
        sides = ((q0_ref[0], do0_ref[0], lse0_ref[0].T, dl0_ref[0].T, mask0), (q1_ref[0], do1_ref[0], lse1_ref[0].T, dl1_ref[0].T, mask1))
        n_acc = n_kv if R > 1 else n_kv // 2
        dks = [jnp.zeros((BLOCK, LANES), F32) for _ in range(n_acc)]
        dvs = [jnp.zeros((BLOCK, LANES), F32) for _ in range(n_acc)]
        for t in range(n_heads // 2):
            k2 = _kv_operand(kb, kbs, 2 * t, n_kv, n_heads)
            v2 = _kv_operand(vb, vbs, 2 * t, n_kv, n_heads)
            a = (2 * t) // R if R > 1 else t
            for (q, do_, lse_r, dl_r, mask) in sides:
                q2, do2 = _stack_heads(_pair(q, t)), _stack_heads(_pair(do_, t))
                s = lax.dot_general(k2, q2, NT, preferred_element_type=F32) * scale
                dp = lax.dot_general(v2, do2, NT, preferred_element_type=F32)
                ps, dss = [], []
                for half in (0, 1):
                    h = 2 * t + half
                    sl = slice(BLOCK * half, BLOCK * (half + 1))
                    p = jnp.where(mask, jnp.exp(s[:, sl] - lse_r[h:h + 1, :]), 0.0)
                    ps.append(p)
                    dss.append(p * (dp[:, sl] - dl_r[h:h + 1, :]))
                dvs[a] = dvs[a] + jnp.dot(jnp.concatenate(ps, axis=1).astype(BF16), do2, preferred_element_type=F32)
                dks[a] = dks[a] + jnp.dot(jnp.concatenate(dss, axis=1).astype(BF16), q2, preferred_element_type=F32)
        if R > 1:
            lo = _low_half((BLOCK, LANES))
            fold = lambda x: x + pltpu.roll(x, HEAD_DIM, 1)
            dks = [jnp.where(lo, fold(dks[2 * t]), fold(dks[2 * t + 1])) for t in range(n_kv // 2)]
            dvs = [jnp.where(lo, fold(dvs[2 * t]), fold(dvs[2 * t + 1])) for t in range(n_kv // 2)]
        dk_t = jnp.concatenate(dks, axis=-1) * scale
        dk_ref[0] = _rope(dk_t, cos_ref[0], sin_ref[0], -1.0).astype(BF16)
        dv_ref[0] = jnp.concatenate(dvs, axis=-1).astype(BF16)

    nxt = lambda j: jnp.minimum(j + 1, nb - 1)
    k_spec = pl.BlockSpec((1, BLOCK, kvw), lambda b, r, j: (b, j, r * ck + k_col // kvw))
    v_spec = pl.BlockSpec((1, BLOCK, kvw), lambda b, r, j: (b, j, r * ck + v_col // kvw))
    q0 = pl.BlockSpec((1, BLOCK, qw), lambda b, r, j: (b, j, r * cq + q_col // qw))
    q1 = pl.BlockSpec((1, BLOCK, qw), lambda b, r, j: (b, nxt(j), r * cq + q_col // qw))
    w0 = lambda w: pl.BlockSpec((1, BLOCK, w), lambda b, r, j: (b, j, r))
    w1 = lambda w: pl.BlockSpec((1, BLOCK, w), lambda b, r, j: (b, nxt(j), r))
    qkv3 = qkv.reshape(B, Ls, d * C)
    v3 = lambda t, w: t.reshape(B, Ls, d * w)
    do3, lse3, dl3 = v3(do, qw), v3(lse, LANES), v3(delta, LANES)
    args = (qkv3, qkv3, qkv3, qkv3, do3, do3, lse3, lse3, dl3, dl3, v3(cos, LANES), v3(sin, LANES))
    dk3, dv3 = _hosted(plan, name, lambda comm: _pc(
        body, args, out_shape=(SDS((B, Ls, d * kvw), BF16), SDS((B, Ls, d * kvw), BF16)), grid=(B, d, nb),
        in_specs=[k_spec, v_spec, q0, q1, w0(qw), w1(qw), w0(LANES), w1(LANES), w0(LANES), w1(LANES), w0(LANES), w0(LANES)],
        out_specs=(w0(kvw), w0(kvw)), name=name, sem=("parallel", "parallel", "parallel"), comm=comm))
    return dk3.reshape(B * S, kvw), dv3.reshape(B * S, kvw)


def _head_expand():
    r = lax.broadcasted_iota(jnp.int32, (LANES, C_HEADS * HEAD_DIM), 0)
    c = lax.broadcasted_iota(jnp.int32, (LANES, C_HEADS * HEAD_DIM), 1)
    return jnp.where(c // HEAD_DIM == r, 1.0, 0.0).astype(F32)


def _delta(do, o, lse=None, sinks_row=None, name="delta"):
    T, W = do.shape
    tm = _pick(T, 512, 8)
    with_sink = sinks_row is not None

    def body(*refs):
        if with_sink:
            do_ref, o_ref, lse_ref, sk_ref, dl_ref, dob_ref, ds_ref = refs
        else:
            do_ref, o_ref, dl_ref, dob_ref = refs
        do_ = do_ref[...]
        dl = lax.dot_general(do_ * o_ref[...], _head_expand(), NT, preferred_element_type=F32, precision=HI)
        dl_ref[...] = dl
        dob_ref[...] = do_.astype(BF16)
        if with_sink:
            lane = lax.broadcasted_iota(jnp.int32, dl.shape, 1)
            contrib = jnp.where(lane < A_N_HEADS, -jnp.exp(sk_ref[...] - lse_ref[...]) * dl, 0.0)
            part = jnp.sum(contrib, axis=0, keepdims=True)

            @pl.when(pl.program_id(0) == 0)
            def _():
                ds_ref[...] = part

            @pl.when(pl.program_id(0) > 0)
            def _():
                ds_ref[...] += part

    row_w = pl.BlockSpec((tm, W), lambda i: (i, 0))
    row_l = pl.BlockSpec((tm, LANES), lambda i: (i, 0))
    vec_l = pl.BlockSpec((1, LANES), lambda i: (0, 0))
    if with_sink:
        return pl.pallas_call(
            body, out_shape=(SDS((T, LANES), F32), SDS((T, W), BF16), SDS((1, LANES), F32)), grid=(T // tm,),
            in_specs=[row_w, row_w, row_l, vec_l], out_specs=(row_l, row_w, vec_l), name=name,
            compiler_params=_params(("arbitrary",)),
        )(do, o, lse, sinks_row)
    return pl.pallas_call(
        body, out_shape=(SDS((T, LANES), F32), SDS((T, W), BF16)), grid=(T // tm,),
        in_specs=[row_w, row_w], out_specs=(row_l, row_w), name=name, compiler_params=_params(("parallel",)),
    )(do, o)


def _merge(os_, lses):
    T, W = os_[0].shape
    tm = _pick(T, 512, 8)

    def body(o0, o1, o2, l0, l1, l2, o_ref, lse_ref):
        ls = [l0[...], l1[...], l2[...]]
        m = jnp.maximum(jnp.maximum(ls[0], ls[1]), ls[2])
        ws = [jnp.exp(l - m) for l in ls]
        tot = ws[0] + ws[1] + ws[2]
        lse_ref[...] = m + jnp.log(tot)
        e = _head_expand()
        acc = jnp.zeros((tm, W), F32)
        for w, o in zip(ws, (o0, o1, o2)):
            acc = acc + jnp.dot(w / tot, e, preferred_element_type=F32, precision=HI) * o[...]
        o_ref[...] = acc

    row_w = pl.BlockSpec((tm, W), lambda i: (i, 0))
    row_l = pl.BlockSpec((tm, LANES), lambda i: (i, 0))
    return pl.pallas_call(
        body, out_shape=(SDS((T, W), F32), SDS((T, LANES), F32)), grid=(T // tm,),
        in_specs=[row_w] * 3 + [row_l] * 3, out_specs=(row_w, row_l), name="c_merge", compiler_params=_params(("parallel",)),
    )(*os_, *lses)


CONV_TC = 256


def _conv_pre(x, w, bias):
    row = lax.broadcasted_iota(jnp.int32, x.shape, 0)
    acc = x * w[SSM_CONV - 1:SSM_CONV, :] + bias
    for k in range(1, SSM_CONV):
        acc = acc + jnp.where(row >= k, pltpu.roll(x, k, 0), 0.0) * w[SSM_CONV - 1 - k:SSM_CONV - k, :]
    return acc


def _conv_fwd(zx3, w, bias):
    B, S, _ = zx3.shape
    off = SSM_D_INNER // CONV_TC

    def body(x_ref, w_ref, b_ref, o_ref):
        v = _conv_pre(x_ref[0], w_ref[...], b_ref[...])
        o_ref[0] = v * jax.nn.sigmoid(v)

    return pl.pallas_call(
        body, out_shape=SDS((B, S, SSM_CONV_DIM), F32), grid=(B, SSM_CONV_DIM // CONV_TC),
        in_specs=[pl.BlockSpec((1, S, CONV_TC), lambda b, j: (b, 0, j + off)),
                  pl.BlockSpec((SSM_CONV, CONV_TC), lambda b, j: (0, j)), pl.BlockSpec((1, CONV_TC), lambda b, j: (0, j))],
        out_specs=pl.BlockSpec((1, S, CONV_TC), lambda b, j: (b, 0, j)), name="b_conv_fwd",
        compiler_params=_params(("parallel", "parallel")),
    )(zx3, w, bias)


def _conv_bwd(zx3, dxc, w, bias, col0, name):
    B, S, n = dxc.shape
    tc = _pick(n, CONV_TC)
    off_x = (SSM_D_INNER + col0) // tc
    off_w = col0 // tc

    def body(x_ref, d_ref, w_ref, b_ref, dx_ref, dw_ref, db_ref):
        x = x_ref[0]
        wv = w_ref[...]
        v = _conv_pre(x, wv, b_ref[...])
        sg = jax.nn.sigmoid(v)
        dc = d_ref[0] * (sg * (1.0 + v * (1.0 - sg)))
        row = lax.broadcasted_iota(jnp.int32, x.shape, 0)
        dx = dc * wv[SSM_CONV - 1:SSM_CONV, :]
        dws = [jnp.sum(dc * x, axis=0, keepdims=True)]
        for k in range(1, SSM_CONV):
            dx = dx + jnp.where(row < S - k, pltpu.roll(dc, S - k, 0), 0.0) * wv[SSM_CONV - 1 - k:SSM_CONV - k, :]
            dws.append(jnp.sum(dc * jnp.where(row >= k, pltpu.roll(x, k, 0), 0.0), axis=0, keepdims=True))
        dx_ref[0] = dx.astype(BF16)
        ridx = lax.broadcasted_iota(jnp.int32, (SSM_CONV, tc), 0)
        dw = jnp.zeros((SSM_CONV, tc), F32)
        for k in range(SSM_CONV):
            dw = jnp.where(ridx == SSM_CONV - 1 - k, dws[k], dw)
        db = jnp.sum(dc, axis=0, keepdims=True)

        @pl.when(pl.program_id(1) == 0)
        def _():
            dw_ref[...] = dw
            db_ref[...] = db

        @pl.when(pl.program_id(1) > 0)
        def _():
            dw_ref[...] += dw
            db_ref[...] += db

    return pl.pallas_call(
        body, out_shape=(SDS((B, S, n), BF16), SDS((SSM_CONV, n), F32), SDS((1, n), F32)), grid=(n // tc, B),
        in_specs=[pl.BlockSpec((1, S, tc), lambda j, b: (b, 0, j + off_x)), pl.BlockSpec((1, S, tc), lambda j, b: (b, 0, j)),
                  pl.BlockSpec((SSM_CONV, tc), lambda j, b: (0, j + off_w)), pl.BlockSpec((1, tc), lambda j, b: (0, j + off_w))],
        out_specs=(pl.BlockSpec((1, S, tc), lambda j, b: (b, 0, j)), pl.BlockSpec((SSM_CONV, tc), lambda j, b: (0, j)),
                   pl.BlockSpec((1, tc), lambda j, b: (0, j))),
        name=name, compiler_params=_params(("parallel", "arbitrary")),
    )(zx3, dxc, w, bias)


def _ssd_common(x, Bm, Cm, dtc_raw, dtr_raw, pr, pc):
    Q = SSM_CHUNK
    zc = dtc_raw + pr[0:1, :]
    dt_c = jax.nn.softplus(zc)
    dt_r = jax.nn.softplus(dtr_raw + pc[:, 0:1])
    A_r = -jnp.exp(pr[1:2, :])
    A_c = -jnp.exp(pc[:, 1:2])
    row = lax.broadcasted_iota(jnp.int32, (Q, Q), 0)
    col = lax.broadcasted_iota(jnp.int32, (Q, Q), 1)
    tril = jnp.where(row >= col, 1.0, 0.0).astype(F32)
    cs_c = jnp.dot(tril, dt_c * A_r, preferred_element_type=F32, precision=HI)
    cs_r = lax.dot_general(dt_r * A_c, tril, NT, preferred_element_type=F32, precision=HI)
    return zc, dt_c, A_r, cs_c, cs_r, row, col, tril


def _ssd_fwd(xc3, dtc, dtr, prow, pcol, plan=None):
    B, S, _ = xc3.shape
    Q, G, HG, P, N = SSM_CHUNK, SSM_N_GROUPS, SSM_HG, HEAD_DIM, SSM_D_STATE
    nc = S // Q
    xw = HG * P

    def body(x_ref, b_ref, c_ref, dtc_ref, dtr_ref, pr_ref, pc_ref, y_ref, st_ref, state):
        c = pl.program_id(2)

        @pl.when(c == 0)
        def _():
            state[...] = jnp.zeros_like(state)

        x, Bm, Cm = x_ref[0], b_ref[0], c_ref[0]
        pr = pr_ref[0]
        _, dt_c, _, cs_c, cs_r, row, col, _ = _ssd_common(x, Bm, Cm, dtc_ref[0, 0], dtr_ref[0, 0], pr, pc_ref[0])
        Bb, Cb = Bm.astype(BF16), Cm.astype(BF16)
        CB = lax.dot_general(Cb, Bb, NT, preferred_element_type=F32)
        ys = []
        for hg in range(HG):
            xh = x[:, P * hg:P * (hg + 1)]
            xt = xh * dt_c[:, hg:hg + 1]
            csc, csr = cs_c[:, hg:hg + 1], cs_r[hg:hg + 1, :]
            L = jnp.where(row >= col, jnp.exp(jnp.minimum(csc - csr, 0.0)), 0.0)
            ydiag = jnp.dot((CB * L).astype(BF16), xt.astype(BF16), preferred_element_type=F32)
            Sh = state[hg]
            yoff = lax.dot_general(Cb, Sh.astype(BF16), NT, preferred_element_type=F32) * jnp.exp(csc)
            ys.append(ydiag + yoff + pr[2:3, hg:hg + 1] * xh)
            st_ref[0, 0, 0, P * hg:P * (hg + 1), :] = Sh
            csq = csc[Q - 1:Q, :]
            upd = lax.dot_general((xt * jnp.exp(csq - csc)).astype(BF16), Bb, TN, preferred_element_type=F32)
            state[hg] = Sh * jnp.exp(csq) + upd
        y_ref[0] = jnp.concatenate([jnp.concatenate(ys[0:2], axis=-1), jnp.concatenate(ys[2:4], axis=-1)], axis=-1)

    bo, co = SSM_D_INNER // N, (SSM_D_INNER + SSM_BC_DIM) // N
    return _hosted(plan, "b_ssd_fwd", lambda comm: _pc(
        body, (xc3, xc3, xc3, dtc, dtr, prow, pcol),
        out_shape=(SDS((B, S, SSM_D_INNER), F32), SDS((B, G, nc, xw, N), F32)), grid=(G, B, nc),
        in_specs=[pl.BlockSpec((1, Q, xw), lambda g, b, c: (b, c, g)), pl.BlockSpec((1, Q, N), lambda g, b, c: (b, c, bo + g)),
                  pl.BlockSpec((1, Q, N), lambda g, b, c: (b, c, co + g)), pl.BlockSpec((1, 1, Q, HG), lambda g, b, c: (b, g, c, 0)),
                  pl.BlockSpec((1, 1, HG, Q), lambda g, b, c: (b, g, 0, c)), pl.BlockSpec((1, 3, HG), lambda g, b, c: (g, 0, 0)),
                  pl.BlockSpec((1, HG, 3), lambda g, b, c: (g, 0, 0))],
        out_specs=(pl.BlockSpec((1, Q, xw), lambda g, b, c: (b, c, g)), pl.BlockSpec((1, 1, 1, xw, N), lambda g, b, c: (b, g, c, 0, 0))),
        scratch_shapes=[pltpu.VMEM((HG, P, N), F32)], name="b_ssd_fwd", sem=("parallel", "arbitrary", "arbitrary"), comm=comm))


def _ssd_bwd(xc3, dtc, dtr, prow, pcol, states, dy3, plan=None):
    B, S, _ = xc3.shape
    Q, G, HG, P, N = SSM_CHUNK, SSM_N_GROUPS, SSM_HG, HEAD_DIM, SSM_D_STATE
    nc = S // Q
    xw = HG * P

    def body(x_ref, b_ref, c_ref, dtc_ref, dtr_ref, pr_ref, pc_ref, st_ref, dy_ref,
             dx_ref, db_ref, dc_ref, ddt_ref, dpar_ref, dstate):
        bi, ci = pl.program_id(1), pl.program_id(2)

        @pl.when(ci == 0)
        def _():
            dstate[...] = jnp.zeros_like(dstate)

        x, Bm, Cm, dy = x_ref[0], b_ref[0], c_ref[0], dy_ref[0]
        pr = pr_ref[0]
        zc, dt_c, A_r, cs_c, cs_r, row, col, tril = _ssd_common(x, Bm, Cm, dtc_ref[0, 0], dtr_ref[0, 0], pr, pc_ref[0])
        Bb, Cb = Bm.astype(BF16), Cm.astype(BF16)
        CB = lax.dot_general(Cb, Bb, NT, preferred_element_type=F32)
        CBt = lax.dot_general(Bb, Cb, NT, preferred_element_type=F32)
        lane4 = lax.broadcasted_iota(jnp.int32, (Q, HG), 1)
        lane4r = lax.broadcasted_iota(jnp.int32, (1, HG), 1)
        rowq = lax.broadcasted_iota(jnp.int32, (Q, 1), 0)
        dB = jnp.zeros((Q, N), F32)
        dC = jnp.zeros((Q, N), F32)
        dcs4 = jnp.zeros((Q, HG), F32)
        dtx4 = jnp.zeros((Q, HG), F32)
        dD4 = jnp.zeros((1, HG), F32)
        dxts, xhs, dyhs = [], [], []
        for hg in range(HG):
            xh = x[:, P * hg:P * (hg + 1)]
            dyh = dy[:, P * hg:P * (hg + 1)]
            xt = xh * dt_c[:, hg:hg + 1]
            xtb, dyb = xt.astype(BF16), dyh.astype(BF16)
            csc, csr = cs_c[:, hg:hg + 1], cs_r[hg:hg + 1, :]
            L = jnp.where(row >= col, jnp.exp(jnp.minimum(csc - csr, 0.0)), 0.0)
            Lt = jnp.where(col >= row, jnp.exp(jnp.minimum(csr - csc, 0.0)), 0.0)
            M, Mt = CB * L, CBt * Lt
            Sh = st_ref[0, 0, 0, P * hg:P * (hg + 1), :]
            dSh = dstate[hg]
            Shb, dShb = Sh.astype(BF16), dSh.astype(BF16)
            ecs = jnp.exp(csc)
            csq = csc[Q - 1:Q, :]
            dec = jnp.exp(csq - csc)
            dxt = jnp.dot(Mt.astype(BF16), dyb, preferred_element_type=F32)
            dxt = dxt + lax.dot_general(Bb, dShb, NT, preferred_element_type=F32) * dec
            Gm = lax.dot_general(dyb, xtb, NT, preferred_element_type=F32)
            Gt = lax.dot_general(xtb, dyb, NT, preferred_element_type=F32)
            dC = dC + jnp.dot((Gm * L).astype(BF16), Bb, preferred_element_type=F32)
            dB = dB + jnp.dot((Gt * Lt).astype(BF16), Cb, preferred_element_type=F32)
            dC = dC + jnp.dot(dyb, Shb, preferred_element_type=F32) * ecs
            dBst = jnp.dot(xtb, dShb, preferred_element_type=F32) * dec
            dB = dB + dBst
            dcs = jnp.sum(Gm * M, axis=1, keepdims=True) - jnp.sum(Gt * Mt, axis=1, keepdims=True)
            yoff = lax.dot_general(Cb, Shb, NT, preferred_element_type=F32) * ecs
            dcs = dcs + jnp.sum(yoff * dyh, axis=1, keepdims=True)
            r = jnp.sum(dBst * Bm, axis=1, keepdims=True)
            dcs = dcs - r
            extra = jnp.sum(r, axis=0, keepdims=True) + jnp.exp(csq) * jnp.sum(
                jnp.sum(dSh * Sh, axis=1, keepdims=True), axis=0, keepdims=True)
            dcs = dcs + jnp.where(rowq == Q - 1, extra, 0.0)
            dcs4 = jnp.where(lane4 == hg, dcs, dcs4)
            dtx4 = jnp.where(lane4 == hg, jnp.sum(dxt * xh, axis=1, keepdims=True), dtx4)
            dD4 = jnp.where(lane4r == hg, jnp.sum(jnp.sum(dyh * xh, axis=1, keepdims=True), axis=0, keepdims=True), dD4)
            dstate[hg] = dSh * jnp.exp(csq) + lax.dot_general((dyh * ecs).astype(BF16), Cb, TN, preferred_element_type=F32)
            dxts.append(dxt)
            xhs.append(xh)
            dyhs.append(dyh)
        da4 = lax.dot_general(tril, dcs4, TN, preferred_element_type=F32, precision=HI)
        ddt4 = da4 * A_r + dtx4
        ddtraw = ddt4 * jax.nn.sigmoid(zc)
        ddt_ref[0, 0] = ddtraw
        dxs = [dxts[hg] * dt_c[:, hg:hg + 1] + pr[2:3, hg:hg + 1] * dyhs[hg] for hg in range(HG)]
        dx_ref[0] = jnp.concatenate([jnp.concatenate(dxs[0:2], axis=-1), jnp.concatenate(dxs[2:4], axis=-1)], axis=-1)
        db_ref[0] = dB
        dc_ref[0] = dC
        d_bias = jnp.sum(ddtraw, axis=0, keepdims=True)
        d_alog = jnp.sum(da4 * dt_c, axis=0, keepdims=True) * A_r
        r3 = lax.broadcasted_iota(jnp.int32, (3, HG), 0)
        dpar = jnp.where(r3 == 0, d_bias, jnp.where(r3 == 1, d_alog, dD4))
        first = (bi == 0) & (ci == 0)

        @pl.when(first)
        def _():
            dpar_ref[0] = dpar

        @pl.when(jnp.logical_not(first))
        def _():
            dpar_ref[0] += dpar

    rc = lambda c: nc - 1 - c
    bo, co = SSM_D_INNER // N, (SSM_D_INNER + SSM_BC_DIM) // N
    return _hosted(plan, "b_ssd_bwd", lambda comm: _pc(
        body, (xc3, xc3, xc3, dtc, dtr, prow, pcol, states, dy3),
        out_shape=(SDS((B, S, SSM_D_INNER), F32), SDS((B, S, SSM_BC_DIM), F32), SDS((B, S, SSM_BC_DIM), F32),
                   SDS((B, G, S, HG), F32), SDS((G, 3, HG), F32)),
        grid=(G, B, nc),
        in_specs=[pl.BlockSpec((1, Q, xw), lambda g, b, c: (b, rc(c), g)), pl.BlockSpec((1, Q, N), lambda g, b, c: (b, rc(c), bo + g)),
                  pl.BlockSpec((1, Q, N), lambda g, b, c: (b, rc(c), co + g)), pl.BlockSpec((1, 1, Q, HG), lambda g, b, c: (b, g, rc(c), 0)),
                  pl.BlockSpec((1, 1, HG, Q), lambda g, b, c: (b, g, 0, rc(c))), pl.BlockSpec((1, 3, HG), lambda g, b, c: (g, 0, 0)),
                  pl.BlockSpec((1, HG, 3), lambda g, b, c: (g, 0, 0)),
                  pl.BlockSpec((1, 1, 1, xw, N), lambda g, b, c: (b, g, rc(c), 0, 0)), pl.BlockSpec((1, Q, xw), lambda g, b, c: (b, rc(c), g))],
        out_specs=(pl.BlockSpec((1, Q, xw), lambda g, b, c: (b, rc(c), g)), pl.BlockSpec((1, Q, N), lambda g, b, c: (b, rc(c), g)),
                   pl.BlockSpec((1, Q, N), lambda g, b, c: (b, rc(c), g)), pl.BlockSpec((1, 1, Q, HG), lambda g, b, c: (b, g, rc(c), 0)),
                   pl.BlockSpec((1, 3, HG), lambda g, b, c: (g, 0, 0))),
        scratch_shapes=[pltpu.VMEM((HG, P, N), F32)], name="b_ssd_bwd", sem=("parallel", "arbitrary", "arbitrary"), comm=comm))


GN_W = SSM_D_INNER // SSM_N_GROUPS


def _gate_fwd(y, zx, nw):
    T = y.shape[0]
    tm = _pick(T, 256, 8)

    def body(y_ref, z_ref, w_ref, o_ref):
        z = z_ref[...]
        gt = y_ref[...] * (z * jax.nn.sigmoid(z))
        outs = []
        for k in range(SSM_N_GROUPS):
            gk = gt[:, GN_W * k:GN_W * (k + 1)]
            outs.append(gk * lax.rsqrt(jnp.mean(gk * gk, axis=-1, keepdims=True) + NORM_EPS))
        o_ref[...] = (jnp.concatenate(outs, axis=-1) * w_ref[...]).astype(BF16)

    row = pl.BlockSpec((tm, SSM_D_INNER), lambda i: (i, 0))
    return pl.pallas_call(
        body, out_shape=SDS((T, SSM_D_INNER), BF16), grid=(T // tm,),
        in_specs=[row, row, pl.BlockSpec((1, SSM_D_INNER), lambda i: (0, 0))], out_specs=row, name="b_gate_fwd",
        compiler_params=_params(("parallel",)),
    )(y, zx, nw)


def _gate_bwd(dgn, y, zx, nw):
    T = y.shape[0]
    tm = _pick(T, 256, 8)

    def body(d_ref, y_ref, z_ref, w_ref, dy_ref, dz_ref, dw_ref):
        z, yv, w = z_ref[...], y_ref[...], w_ref[...]
        sg = jax.nn.sigmoid(z)
        sz = z * sg
        gt = yv * sz
        gw = d_ref[...] * w
        dgts, dws = [], []
        for k in range(SSM_N_GROUPS):
            sl = slice(GN_W * k, GN_W * (k + 1))
            gk, gwk = gt[:, sl], gw[:, sl]
            rstd = lax.rsqrt(jnp.mean(gk * gk, axis=-1, keepdims=True) + NORM_EPS)
            dgts.append(rstd * gwk - gk * (rstd * rstd * rstd) * jnp.mean(gwk * gk, axis=-1, keepdims=True))
            dws.append(jnp.sum(d_ref[:, sl] * gk * rstd, axis=0, keepdims=True))
        dgt = jnp.concatenate(dgts, axis=-1)
        dy_ref[...] = dgt * sz
        dz_ref[...] = (dgt * yv * (sg * (1.0 + z * (1.0 - sg)))).astype(BF16)
        dw = jnp.concatenate(dws, axis=-1)

        @pl.when(pl.program_id(0) == 0)
        def _():
            dw_ref[...] = dw

        @pl.when(pl.program_id(0) > 0)
        def _():
            dw_ref[...] += dw

    row = pl.BlockSpec((tm, SSM_D_INNER), lambda i: (i, 0))
    vec = pl.BlockSpec((1, SSM_D_INNER), lambda i: (0, 0))
    return pl.pallas_call(
        body, out_shape=(SDS((T, SSM_D_INNER), F32), SDS((T, SSM_D_INNER), BF16), SDS((1, SSM_D_INNER), F32)), grid=(T // tm,),
        in_specs=[row, row, row, vec], out_specs=(row, row, vec), name="b_gate_bwd", compiler_params=_params(("arbitrary",)),
    )(dgn, y, zx, nw)


N_CHIPS = 4


def _dev_block(ref, kind, j, size):
    if kind == "slot":
        return ref.at[j]
    start = pl.multiple_of(j * size, size)
    nd = len(ref.shape)
    if kind == "col":
        return ref.at[(slice(None),) * (nd - 1) + (pl.ds(start, size),)]
    return ref.at[(slice(None),) * (nd - 2) + (pl.ds(start, size), slice(None))]


def _dma_sems(n, k):
    return [pltpu.SemaphoreType.DMA((n, k)), pltpu.SemaphoreType.DMA((n, k)), pltpu.SemaphoreType.DMA((n, k))]


def _place(shard, layer, kind, full_shape, dev, name):
    k, n = shard.shape[1:]
    tr = _pick(k, 512, 16)
    nb = k // tr

    def body(dev_ref, s_ref, o_ref):
        if kind == "slot":
            o_ref[0] = s_ref[0].astype(BF16)
        else:
            o_ref[...] = s_ref[0].astype(BF16)

    out_spec = {"slot": pl.BlockSpec((1, tr, n), lambda i, d: (d[0], i, 0)),
                "row": pl.BlockSpec((tr, n), lambda i, d: (d[0] * nb + i, 0)),
                "col": pl.BlockSpec((tr, n), lambda i, d: (i, d[0]))}[kind]
    return pl.pallas_call(
        body, out_shape=SDS(full_shape, BF16),
        grid_spec=pltpu.PrefetchScalarGridSpec(
            num_scalar_prefetch=1, grid=(nb,), in_specs=[pl.BlockSpec((1, tr, n), lambda i, d: (layer, i, 0))], out_specs=out_spec),
        name=name, compiler_params=_params(("arbitrary",)),
    )(dev, shard)


def _run_comm(comm, name):
    c_in = len(comm.inputs)

    def body(*refs):
        cins, couts, sems = refs[:c_in], refs[c_in:c_in + len(comm.out_shapes)], refs[c_in + len(comm.out_shapes):]
        for _, fn in comm.phases:
            fn(cins, couts, sems)

    return pl.pallas_call(
        body, out_shape=list(comm.out_shapes), in_specs=[ANY] * c_in, out_specs=[ANY] * len(comm.out_shapes),
        input_output_aliases=dict(comm.aliases), scratch_shapes=list(comm.sems), name=name,
    )(*comm.inputs)


def _gather_comm(items, mid=0.7):
    n = len(items)

    def tools(srcs, dsts, sems):
        send_sems, recv_sems, local_sems = sems
        px, py, pc = lax.axis_index("x"), lax.axis_index("y"), lax.axis_index("c")
        me, sibling = (px, py, pc), (px, py, 1 - pc)
        chips = [(1 - px, py), (px, 1 - py), (1 - px, 1 - py)]

        def blk(a, dev):
            return _dev_block(dsts[a], items[a][1], 4 * dev[0] + 2 * dev[1] + dev[2], items[a][2])

        def copy(a, k, block, to, src=None):
            return pltpu.make_async_remote_copy(
                src_ref=blk(a, block) if src is None else src, dst_ref=blk(a, block),
                send_sem=send_sems.at[a, k], recv_sem=recv_sems.at[a, k], device_id=to, device_id_type=MESH)

        def mine():
            return [pltpu.make_async_copy(srcs[a], blk(a, me), local_sems.at[a, 0]) for a in range(n) if not items[a][4]]

        def first():
            out = []
            for a in range(n):
                src = blk(a, me) if items[a][4] else srcs[a]
                out.append(copy(a, 0, me, sibling, src=src))
                out += [copy(a, 1 + j, me, (*chip, pc), src=src) for j, chip in enumerate(chips)]
            return out

        def passed():
            return [copy(a, 4 + j, (*chip, pc), sibling) for j, chip in enumerate(chips) for a in range(n)]

        return me, sibling, chips, pc, copy, mine, first, passed

    def start(srcs, dsts, sems):
        *_, mine, first, _ = tools(srcs, dsts, sems)
        for cp in mine() + first():
            cp.start()

    def forward(srcs, dsts, sems):
        me, _, chips, pc, copy, _, _, passed = tools(srcs, dsts, sems)
        fwd = passed()
        for j, chip in enumerate(chips):
            for a in range(n):
                copy(a, 1 + j, (*chip, pc), me).wait_recv()
                fwd[j * n + a].start()

    def finish(srcs, dsts, sems):
        me, sibling, chips, pc, copy, mine, first, passed = tools(srcs, dsts, sems)
        for a in range(n):
            copy(a, 0, sibling, me).wait_recv()
            for j, chip in enumerate(chips):
                copy(a, 4 + j, (*chip, 1 - pc), me).wait_recv()
        for cp in first() + passed():
            cp.wait_send()
        for cp in mine():
            cp.wait()

    return _Comm([it[0] for it in items], [SDS(it[3], it[0].dtype) for it in items],
                 {a: a for a in range(n) if items[a][4]}, _dma_sems(n, 7), [(0.0, start), (mid, forward), (1.0, finish)])


def _gather(items, name):
    return _run_comm(_gather_comm(items), name)


def _reduce_d2d(items, name):
    n = len(items)

    def body(*refs):
        gs, gots = refs[:n], refs[n:2 * n]
        send_sems, recv_sems, _ = refs[2 * n:]
        px, py, pc = lax.axis_index("x"), lax.axis_index("y"), lax.axis_index("c")
        copies = []
        for a in range(n):
            _, kind, size, _ = items[a]
            for q in range(N_CHIPS):
                copies.append(pltpu.make_async_remote_copy(
                    src_ref=_dev_block(gs[a], kind, 2 * q + 1 - pc, size), dst_ref=gots[a].at[q], send_sem=send_sems.at[a, q],
                    recv_sem=recv_sems.at[a, q], device_id=(px, py, 1 - pc), device_id_type=MESH))
        for cp in copies:
            cp.start()
        for cp in copies:
            cp.wait()

    return pl.pallas_call(
        body, out_shape=[SDS((N_CHIPS,) + tuple(it[3]), F32) for it in items], in_specs=[ANY] * n, out_specs=[ANY] * n,
        scratch_shapes=_dma_sems(n, N_CHIPS), name=name,
    )(*[it[0] for it in items])


def _pair_sum(g, got, kind, core, name):
    _, k, n = got.shape
    tr = _pick(k, max(16, STREAM_VMEM // (2 * n * 10)), 16)
    nb = k // tr

    def body(c_ref, g_ref, s_ref, o_ref):
        mine = g_ref[0] if kind == "slot" else g_ref[...]
        o_ref[0] = (mine + s_ref[0]).astype(BF16)

    g_spec = {"slot": pl.BlockSpec((1, tr, n), lambda q, i, c: (2 * q + c[0], i, 0)),
              "row": pl.BlockSpec((tr, n), lambda q, i, c: ((2 * q + c[0]) * nb + i, 0)),
              "col": pl.BlockSpec((tr, n), lambda q, i, c: (i, 2 * q + c[0]))}[kind]
    part = pl.BlockSpec((1, tr, n), lambda q, i, c: (q, i, 0))
    return pl.pallas_call(
        body, out_shape=SDS((N_CHIPS, k, n), BF16),
        grid_spec=pltpu.PrefetchScalarGridSpec(num_scalar_prefetch=1, grid=(N_CHIPS, nb), in_specs=[g_spec, part], out_specs=part),
        name=name, compiler_params=_params(("arbitrary", "arbitrary")),
    )(core, g, got)


def _reduce_ici_comm(parts):
    n = len(parts)

    def copies(ps, rs, sems, arriving):
        send_sems, recv_sems, _ = sems
        px, py, pc = lax.axis_index("x"), lax.axis_index("y"), lax.axis_index("c")
        my_chip = 2 * px + py
        out = []
        for a in range(n):
            for k in range(1, N_CHIPS):
                qx, qy = px ^ (k >> 1), py ^ (k & 1)
                q = 2 * qx + qy
                out.append(pltpu.make_async_remote_copy(
                    src_ref=ps[a].at[q], dst_ref=rs[a].at[q] if arriving else rs[a].at[my_chip], send_sem=send_sems.at[a, k - 1],
                    recv_sem=recv_sems.at[a, k - 1], device_id=(qx, qy, pc), device_id_type=MESH))
        return out

    def start(ps, rs, sems):
        for cp in copies(ps, rs, sems, False):
            cp.start()

    def finish(ps, rs, sems):
        for cp in copies(ps, rs, sems, True):
            cp.wait_recv()
        for cp in copies(ps, rs, sems, False):
            cp.wait_send()

    return _Comm(list(parts), [SDS(p.shape, p.dtype) for p in parts], {}, _dma_sems(n, N_CHIPS - 1),
                 [(0.0, start), (1.0, finish)])


def _adam_update(g, w, m, v):
    c1 = 1.0 - ADAM_B1 ** ADAM_STEP
    c2 = 1.0 - ADAM_B2 ** ADAM_STEP
    nm = ADAM_B1 * m + (1.0 - ADAM_B1) * g
    nv = ADAM_B2 * v + (1.0 - ADAM_B2) * (g * g)
    delta = -ADAM_LR * ((nm / c1) / (jnp.sqrt(nv / c2) + ADAM_EPS) + ADAM_WD * w)
    return delta, nm, nv


def _adamw(parts, recv, w, m, v, layer, prev, chip, name):
    _, R, C = w.shape
    row_bytes = 2 * C * (N_CHIPS * 2 + 7 * 4)
    tr = _pick(R, max(16, STREAM_VMEM // row_bytes), 16)
    n_prev = 0 if prev is None else 4

    def body(ch_ref, own_ref, r1_ref, r2_ref, r3_ref, w_ref, m_ref, v_ref, *rest):
        g_ref, d_ref, nm_ref, nv_ref = rest[n_prev:]
        g = own_ref[0].astype(F32)
        for r_ref in (r1_ref, r2_ref, r3_ref):
            g = g + r_ref[0].astype(F32)
        g_ref[0] = g
        d_ref[0], nm_ref[0], nv_ref[0] = _adam_update(g, w_ref[0], m_ref[0], v_ref[0])

    lay = pl.BlockSpec((1, tr, C), lambda i, ch: (layer, i, 0))
    other = lambda k: pl.BlockSpec((1, tr, C), lambda i, ch: (ch[0] ^ k, i, 0))
    out = SDS(w.shape, F32)
    return pl.pallas_call(
        body, out_shape=(out, out, out, out),
        grid_spec=pltpu.PrefetchScalarGridSpec(
            num_scalar_prefetch=1, grid=(R // tr,),
            in_specs=[pl.BlockSpec((1, tr, C), lambda i, ch: (ch[0], i, 0)), other(2), other(1), other(3), lay, lay, lay]
            + [ANY] * n_prev,
            out_specs=(lay, lay, lay, lay)),
        input_output_aliases={8 + k: k for k in range(n_prev)},
        name=name, compiler_params=_params(("arbitrary",)),
    )(chip, parts, recv, recv, recv, w, m, v, *(prev or ()))


def _small_adamw(gathered, ws, ms, vs):
    n = len(ws)

    def body(*refs):
        g_in, w_in, m_in, v_in = refs[:n], refs[n:2 * n], refs[2 * n:3 * n], refs[3 * n:4 * n]
        outs = refs[4 * n:]
        for i in range(n):
            g = g_in[i][0]
            for dev in range(1, N_DEV):
                g = g + g_in[i][dev]
            d, nm, nv = _adam_update(g, w_in[i][...], m_in[i][...], v_in[i][...])
            outs[i][...] = g
            outs[n + i][...] = d
            outs[2 * n + i][...] = nm
            outs[3 * n + i][...] = nv

    shapes = [SDS(w.shape, F32) for w in ws]
    outs = pl.pallas_call(body, out_shape=shapes * 4, name="small_adamw")(*gathered, *ws, *ms, *vs)
    return outs[:n], outs[n:2 * n], outs[2 * n:3 * n], outs[3 * n:]


W_NAMES = ("norm_mix_w", "norm_mlp_w", "a_w_qkv", "a_b_qkv", "a_sinks", "a_w_o", "a_b_o", "b_in_w", "b_conv_w", "b_conv_b",
           "b_dt_bias", "b_a_log", "b_d", "b_norm_w", "b_out_w", "c_w_qkv", "c_w_o", "mlp_w_up", "mlp_w_down", "final_norm_w")
BIG_KIND = {"a_w_qkv": "slot", "a_w_o": "row", "b_in_w": "slot", "b_out_w": "row", "c_w_qkv": "col", "c_w_o": "row",
            "mlp_w_up": "col", "mlp_w_down": "row"}
SMALL_SHARDED = {"a_b_qkv": 1, "a_b_o": 1, "b_conv_w": 2}
SMALL_REPLICATED = ("norm_mix_w", "norm_mlp_w", "a_sinks", "b_conv_b", "b_dt_bias", "b_a_log", "b_d", "b_norm_w", "final_norm_w")


def _layer_big(i):
    kind, j = i % 3, i // 3
    mix = {0: [("a_w_qkv", j), ("a_w_o", j)], 1: [("b_in_w", 0), ("b_out_w", 0)], 2: [("c_w_qkv", 0), ("c_w_o", 0)]}[kind]
    return mix + [("mlp_w_up", i), ("mlp_w_down", i)]


def _block_size(kind, shard2d):
    return {"slot": None, "row": shard2d[0], "col": shard2d[1]}[kind]


def _full2d(kind, shard2d):
    k, n = shard2d
    return {"slot": (N_DEV, k, n), "row": (N_DEV * k, n), "col": (k, N_DEV * n)}[kind]


def _from_slots(t, ax):
    s = t.shape[1:]
    return jnp.moveaxis(t, 0, ax).reshape(s[:ax] + (N_DEV * s[ax],) + s[ax + 1:])


def _to_slots(g, ax):
    s = g.shape
    return jnp.moveaxis(g.reshape(s[:ax] + (N_DEV, s[ax] // N_DEV) + s[ax + 1:]), ax, 0)


def _rope_tables(positions):
    half = HEAD_DIM // 2
    inv = ROPE_THETA ** (-(jnp.arange(LANES, dtype=jnp.int32) % half).astype(F32) / half)
    ang = positions.astype(F32).reshape(-1, 1) * inv
    return jnp.cos(ang), jnp.sin(ang)


def _swa_fwd(u, h, p, j, B, S, cos, sin, tag, plan=None):
    qkv = _matmul(u, p["a_w_qkv"][j], out_dtype=BF16, bias=p["a_b_qkv"][j][None], rope=(cos, sin),
                  rope_cols=A_Q_DIM + A_KV_DIM, tn=640, name=f"{tag}_qkv")
    o, lse = _attn_fwd(qkv, B, S, 1, n_heads=A_N_HEADS, n_kv=A_N_KV, q_col=0, k_col=A_Q_DIM, v_col=A_Q_DIM + A_KV_DIM,
                       max_dist=A_WINDOW - 1, sinks=p["a_sinks"][j], name=f"{tag}_attn", plan=plan)
    h1 = _matmul(o, p["a_w_o"][j], bias=p["a_b_o"][j][None], resid=h, name=f"{tag}_o")
    return h1, (qkv, o, lse)


def _swa_bwd(dh1, u, saved, p, j, B, S, cos, sin, tag, plan=None):
    qkv, o, lse = saved
    kw = dict(n_heads=A_N_HEADS, n_kv=A_N_KV, q_col=0, k_col=A_Q_DIM, v_col=A_Q_DIM + A_KV_DIM, max_dist=A_WINDOW - 1)
    g = {}
    do = _matmul(dh1, p["a_w_o"][j], tb=True, name=f"{tag}_do")
    g["a_w_o"] = _matmul(o, dh1, ta=True, name=f"{tag}_dwo")
    g["a_b_o"] = _colsum(dh1, f"{tag}_dbo")[0]
    sk = jnp.pad(p["a_sinks"][j], (0, LANES - A_N_HEADS))[None]
    delta, dob, dsink = _delta(do, o, lse, sk, name=f"{tag}_delta")
    g["a_sinks"] = dsink[0, :A_N_HEADS]
    dq = _attn_dq(qkv, dob, lse, delta, cos, sin, B, S, 1, name=f"{tag}_dq", plan=plan, **kw)
    dk, dv = _attn_dkv(qkv, dob, lse, delta, cos, sin, B, S, 1, name=f"{tag}_dkv", plan=plan, **kw)
    dqkv = jnp.concatenate([dq, dk, dv], axis=1)
    g["a_w_qkv"] = _matmul(u, dqkv, ta=True, tn=640, name=f"{tag}_dwqkv")
    g["a_b_qkv"] = _colsum(dqkv, f"{tag}_dbqkv")[0]
    du = _matmul(dqkv, p["a_w_qkv"][j], tb=True, tk=640, name=f"{tag}_du")
    return du, g


def _group_cols(gi, qkv):
    W = C_HEADS * HEAD_DIM
    if C_PATTERNS[gi][1] == 1:
        return qkv, (gi * W, (3 + gi) * W, (6 + gi) * W)
    part = jnp.concatenate([qkv[:, (3 * j + gi) * W:(3 * j + gi + 1) * W] for j in range(3)], axis=1)
    return part, (0, W, 2 * W)


def _dil_fwd(u, h, p, B, S, cos, sin, plan=None):
    W = C_HEADS * HEAD_DIM
    qkv = _matmul(u, p["c_w_qkv"][0], out_dtype=BF16, rope=(cos, sin), rope_cols=6 * W, name="c_qkv", plan=plan)
    os_, lses, parts = [], [], []
    for gi, (window, dil) in enumerate(C_PATTERNS):
        part, (qc, kc, vc) = _group_cols(gi, qkv)
        o, lse = _attn_fwd(part, B, S, dil, n_heads=C_HEADS, n_kv=C_HEADS, q_col=qc, k_col=kc, v_col=vc,
                           max_dist=window // dil, sinks=None, name=f"c_attn{gi}")
        os_.append(o)
        lses.append(lse)
        parts.append((part, (qc, kc, vc)))
    o, lse = _merge(os_, lses)
    h1 = _matmul(o, p["c_w_o"][0], resid=h, name="c_o")
    return h1, (parts, o, lse)


def _dil_bwd(dh1, u, saved, p, B, S, cos, sin, plan=None):
    parts, o, lse = saved
    g = {}
    do = _matmul(dh1, p["c_w_o"][0], tb=True, name="c_do")
    g["c_w_o"] = _matmul(o, dh1, ta=True, name="c_dwo")[None]
    delta, dob = _delta(do, o, name="c_delta")
    dqs, dks, dvs = [], [], []
    for gi, (window, dil) in enumerate(C_PATTERNS):
        part, (qc, kc, vc) = parts[gi]
        kw = dict(n_heads=C_HEADS, n_kv=C_HEADS, q_col=qc, k_col=kc, v_col=vc, max_dist=window // dil)
        dqs.append(_attn_dq(part, dob, lse, delta, cos, sin, B, S, dil, name=f"c_dq{gi}", **kw))
        dk, dv = _attn_dkv(part, dob, lse, delta, cos, sin, B, S, dil, name=f"c_dkv{gi}", **kw)
        dks.append(dk)
        dvs.append(dv)
    dqkv = jnp.concatenate(dqs + dks + dvs, axis=1)
    g["c_w_qkv"] = _matmul(u, dqkv, ta=True, name="c_dwqkv", plan=plan)[None]
    du = _matmul(dqkv, p["c_w_qkv"][0], tb=True, name="c_du")
    return du, g


def _ssm_params(p):
    par = jnp.stack([p["b_dt_bias"][0], p["b_a_log"][0], p["b_d"][0]], axis=0)
    prow = par.reshape(3, SSM_N_GROUPS, SSM_HG).transpose(1, 0, 2)
    return prow, prow.transpose(0, 2, 1)


def _mamba_fwd(u, h, p, B, S, plan=None):
    T = B * S
    G, HG = SSM_N_GROUPS, SSM_HG
    w_in = p["b_in_w"][0]
    nzx = SSM_D_INNER + SSM_CONV_DIM
    w_dt = jnp.pad(w_in[:, nzx:], ((0, 0), (0, LANES - SSM_N_HEADS)))
    zx = _matmul(u, w_in[:, :nzx], name="b_zx")
    dtraw = _matmul(u, w_dt, name="b_dt")[:, :SSM_N_HEADS]
    dtc = dtraw.reshape(B, S, G, HG).transpose(0, 2, 1, 3)
    dtr = dtraw.reshape(B, S, G, HG).transpose(0, 2, 3, 1)
    prow, pcol = _ssm_params(p)
    zx3 = zx.reshape(B, S, nzx)
    xc3 = _conv_fwd(zx3, p["b_conv_w"][0], p["b_conv_b"])
    y3, states = _ssd_fwd(xc3, dtc, dtr, prow, pcol, plan=plan)
    y = y3.reshape(T, SSM_D_INNER)
    gn = _gate_fwd(y, zx, p["b_norm_w"])
    h1 = _matmul(gn, p["b_out_w"][0], resid=h, name="b_out")
    return h1, (zx, dtc, dtr, xc3, y, states, gn, w_dt)


def _mamba_bwd(dh1, u, saved, p, B, S, plan=None):
    T = B * S
    zx, dtc, dtr, xc3, y, states, gn, w_dt = saved
    nzx = SSM_D_INNER + SSM_CONV_DIM
    w_in = p["b_in_w"][0]
    prow, pcol = _ssm_params(p)
    g = {}
    dgn = _matmul(dh1, p["b_out_w"][0], tb=True, name="b_dgn")
    g["b_out_w"] = _matmul(gn, dh1, ta=True, name="b_dwout")[None]
    dy, dz, dnw = _gate_bwd(dgn, y, zx, p["b_norm_w"])
    g["b_norm_w"] = dnw
    dx3, dB3, dC3, ddt, dpar = _ssd_bwd(xc3, dtc, dtr, prow, pcol, states, dy.reshape(B, S, SSM_D_INNER), plan=plan)
    dpar = dpar.transpose(1, 0, 2).reshape(3, SSM_N_HEADS)
    g["b_dt_bias"], g["b_a_log"], g["b_d"] = dpar[0:1], dpar[1:2], dpar[2:3]
    zx3 = zx.reshape(B, S, nzx)
    cw, cb = p["b_conv_w"][0], p["b_conv_b"]
    parts, dws, dbs = [], [], []
    for col0, dpart, nm in ((0, dx3, "b_conv_bwd_x"), (SSM_D_INNER, dB3, "b_conv_bwd_b"),
                            (SSM_D_INNER + SSM_BC_DIM, dC3, "b_conv_bwd_c")):
        dxp, dw, db = _conv_bwd(zx3, dpart, cw, cb, col0, nm)
        parts.append(dxp.reshape(T, -1))
        dws.append(dw)
        dbs.append(db)
    g["b_conv_w"] = jnp.concatenate(dws, axis=1)[None]
    g["b_conv_b"] = jnp.concatenate(dbs, axis=1)
    dzx = jnp.concatenate([dz] + parts, axis=1)
    ddtraw = ddt.transpose(0, 2, 1, 3).reshape(T, SSM_N_HEADS)
    ddtp = jnp.pad(ddtraw, ((0, 0), (0, LANES - SSM_N_HEADS)))
    dw_zx = _matmul(u, dzx, ta=True, name="b_dwzx")
    dw_dt = _matmul(u, ddtp, ta=True, name="b_dwdt")[:, :SSM_N_HEADS]
    g["b_in_w"] = jnp.concatenate([dw_zx, dw_dt], axis=1)[None]
    du = _matmul(dzx, w_in[:, :nzx], tb=True, name="b_du_zx")
    du = _matmul(ddtp, w_dt, tb=True, resid=du, name="b_du_dt")
    return du, g


def _local_step(x, positions, p, target, plan=None):
    B, S, D = x.shape
    T = B * S
    cos, sin = _rope_tables(positions)
    h = x.reshape(T, D)
    tape = []
    for i in range(DEPTH):
        kind, j = i % 3, i // 3
        u = _rmsnorm_fwd(h, p["norm_mix_w"][i], f"l{i}_norm_mix")
        if kind == 0:
            h1, saved = _swa_fwd(u, h, p, j, B, S, cos, sin, f"a{j}", plan)
        elif kind == 1:
            h1, saved = _mamba_fwd(u, h, p, B, S, plan)
        else:
            h1, saved = _dil_fwd(u, h, p, B, S, cos, sin, plan)
        u2 = _rmsnorm_fwd(h1, p["norm_mlp_w"][i], f"l{i}_norm_mlp")
        r, s = _matmul(u2, p["mlp_w_up"][i], out_dtype=BF16, relu2=True, name=f"l{i}_up", plan=plan)
        h2 = _matmul(s, p["mlp_w_down"][i], resid=h1, name=f"l{i}_down", plan=plan)
        tape.append((h, u, saved, h1, u2, r, s))
        h = h2
    dh, dwf, loss = _final_loss(h, target.reshape(T, D), p["final_norm_w"])
    grads = {"final_norm_w": dwf[0]}
    per_layer = {n: [None] * DEPTH for n in ("norm_mix_w", "norm_mlp_w", "mlp_w_up", "mlp_w_down")}
    a_grads = [None, None]
    for i in reversed(range(DEPTH)):
        kind, j = i % 3, i // 3
        h0, u, saved, h1, u2, r, s = tape[i]
        da = _matmul(dh, p["mlp_w_down"][i], tb=True, out_dtype=BF16, mul=r, mul_scale=2.0, name=f"l{i}_da")
        per_layer["mlp_w_down"][i] = _matmul(s, dh, ta=True, name=f"l{i}_dwdown")
        per_layer["mlp_w_up"][i] = _matmul(u2, da, ta=True, name=f"l{i}_dwup")
        du2 = _matmul(da, p["mlp_w_up"][i], tb=True, name=f"l{i}_du2")
        dh1, dnw = _rmsnorm_bwd(h1, du2, p["norm_mlp_w"][i], dh, f"l{i}_norm_mlp_bwd")
        per_layer["norm_mlp_w"][i] = dnw[0]
        if kind == 0:
            du, g = _swa_bwd(dh1, u, saved, p, j, B, S, cos, sin, f"a{j}", plan)
            a_grads[j] = g
            big = {"a_w_qkv": g["a_w_qkv"], "a_w_o": g["a_w_o"]}
        elif kind == 1:
            du, g = _mamba_bwd(dh1, u, saved, p, B, S, plan)
            grads.update(g)
            big = {"b_in_w": g["b_in_w"][0], "b_out_w": g["b_out_w"][0]}
        else:
            du, g = _dil_bwd(dh1, u, saved, p, B, S, cos, sin, plan)
            grads.update(g)
            big = {"c_w_qkv": g["c_w_qkv"][0], "c_w_o": g["c_w_o"][0]}
        dh, dnw = _rmsnorm_bwd(h0, du, p["norm_mix_w"][i], dh1, f"l{i}_norm_mix_bwd")
        per_layer["norm_mix_w"][i] = dnw[0]
        if plan is not None:
            plan.layer_grads(i, dict(big, mlp_w_up=per_layer["mlp_w_up"][i], mlp_w_down=per_layer["mlp_w_down"][i]))
    for n in ("norm_mix_w", "norm_mlp_w"):
        grads[n] = jnp.stack(per_layer[n], axis=0)
    for n in ("mlp_w_up", "mlp_w_down"):
        grads[n] = per_layer[n]
    for n in ("a_b_qkv", "a_sinks", "a_b_o"):
        grads[n] = jnp.stack([a_grads[0][n], a_grads[1][n]], axis=0)
    for n in ("a_w_qkv", "a_w_o"):
        grads[n] = [a_grads[0][n], a_grads[1][n]]
    for n in ("b_in_w", "b_out_w", "c_w_qkv", "c_w_o"):
        grads[n] = [grads[n][0]]
    return loss, dh.reshape(B, S, D), grads


GATHER_HOSTS = {"a0_attn": (1, ("mlp_w_up",)), "l0_up": (1, ("b_in_w",)), "l0_down": (1, ("b_out_w", "mlp_w_down")),
                "b_ssd_fwd": (2, None), "c_qkv": (3, None)}
REDUCE_HOSTS = {3: (("c_dwqkv", None),), 2: (("b_ssd_bwd", None),),
                1: (("a0_dq", ("b_in_w", "b_out_w")), ("a0_dkv", ("mlp_w_up", "mlp_w_down"))), 0: ()}


class _Plan:
    def __init__(self, w, m, v, p, dev, chip, core):
        self.w, self.m, self.v, self.p, self.dev, self.chip, self.core = w, m, v, p, dev, chip, core
        self.pending = {}
        self.res = {n: None for n in BIG_KIND}
        self._install(0, None)(_gather(self._gather_items(0, None), "gather_l0"))
        for host, (i, only) in GATHER_HOSTS.items():
            self.pending[host] = (_gather_comm(self._gather_items(i, only)), self._install(i, only))

    def _names(self, i, only):
        return [(n, l) for n, l in _layer_big(i) if only is None or n in only]

    def _gather_items(self, i, only):
        items = []
        for n, l in self._names(i, only):
            kind, s2 = BIG_KIND[n], self.w[n].shape[1:]
            placed = _place(self.w[n], l, kind, _full2d(kind, s2), self.dev, f"place_l{i}_{n}")
            items.append((placed, kind, _block_size(kind, s2), _full2d(kind, s2), True))
        return items

    def _install(self, i, only):
        def done(fulls):
            for (n, l), t in zip(self._names(i, only), fulls):
                self.p[n][l] = _from_slots(t, 1) if BIG_KIND[n] == "slot" else t
        return done

    def take(self, host):
        return self.pending[host][0] if host in self.pending else None

    def give(self, host, results):
        self.pending.pop(host)[1](results)

    def layer_grads(self, i, grads):
        names = self._names(i, None)
        items = []
        for n, _ in names:
            kind, s2 = BIG_KIND[n], self.w[n].shape[1:]
            items.append((_to_slots(grads[n], 1) if kind == "slot" else grads[n], kind, _block_size(kind, s2), s2))
        sib = _reduce_d2d(items, f"reduce_d2d_l{i}")
        parts = {n: _pair_sum(it[0], s, it[1], self.core, f"pair_sum_l{i}_{n}") for (n, _), it, s in zip(names, items, sib)}

        def update(sel):
            def done(recv):
                for (n, l), r in zip(sel, recv):
                    self.res[n] = _adamw(parts[n], r, self.w[n], self.m[n], self.v[n], l, self.res[n], self.chip,
                                         f"adamw_l{i}_{n}")
            return done

        if not REDUCE_HOSTS[i]:
            update(names)(_run_comm(_reduce_ici_comm([parts[n] for n, _ in names]), f"reduce_ici_l{i}"))
        for host, only in REDUCE_HOSTS[i]:
            sel = self._names(i, only)
            self.pending[host] = (_reduce_ici_comm([parts[n] for n, _ in sel]), update(sel))

    def flush(self):
        for host in list(self.pending):
            comm, done = self.pending.pop(host)
            done(_run_comm(comm, f"comm_{host}"))


def kernel(x, positions, norm_mix_w, norm_mlp_w, a_w_qkv, a_b_qkv, a_sinks, a_w_o, a_b_o, b_in_w, b_conv_w, b_conv_b, b_dt_bias, b_a_log, b_d, b_norm_w, b_out_w, c_w_qkv, c_w_o, mlp_w_up, mlp_w_down, final_norm_w, loss_target, m_norm_mix_w, m_norm_mlp_w, m_a_w_qkv, m_a_b_qkv, m_a_sinks, m_a_w_o, m_a_b_o, m_b_in_w, m_b_conv_w, m_b_conv_b, m_b_dt_bias, m_b_a_log, m_b_d, m_b_norm_w, m_b_out_w, m_c_w_qkv, m_c_w_o, m_mlp_w_up, m_mlp_w_down, m_final_norm_w, v_norm_mix_w, v_norm_mlp_w, v_a_w_qkv, v_a_b_qkv, v_a_sinks, v_a_w_o, v_a_b_o, v_b_in_w, v_b_conv_w, v_b_conv_b, v_b_dt_bias, v_b_a_log, v_b_d, v_b_norm_w, v_b_out_w, v_c_w_qkv, v_c_w_o, v_mlp_w_up, v_mlp_w_down, v_final_norm_w):
    w = dict(zip(W_NAMES, (norm_mix_w, norm_mlp_w, a_w_qkv, a_b_qkv, a_sinks, a_w_o, a_b_o, b_in_w, b_conv_w, b_conv_b,
                           b_dt_bias, b_a_log, b_d, b_norm_w, b_out_w, c_w_qkv, c_w_o, mlp_w_up, mlp_w_down, final_norm_w)))
    m = dict(zip(W_NAMES, (m_norm_mix_w, m_norm_mlp_w, m_a_w_qkv, m_a_b_qkv, m_a_sinks, m_a_w_o, m_a_b_o, m_b_in_w,
                           m_b_conv_w, m_b_conv_b, m_b_dt_bias, m_b_a_log, m_b_d, m_b_norm_w, m_b_out_w, m_c_w_qkv, m_c_w_o,
                           m_mlp_w_up, m_mlp_w_down, m_final_norm_w)))
    v = dict(zip(W_NAMES, (v_norm_mix_w, v_norm_mlp_w, v_a_w_qkv, v_a_b_qkv, v_a_sinks, v_a_w_o, v_a_b_o, v_b_in_w,
                           v_b_conv_w, v_b_conv_b, v_b_dt_bias, v_b_a_log, v_b_d, v_b_norm_w, v_b_out_w, v_c_w_qkv, v_c_w_o,
                           v_mlp_w_up, v_mlp_w_down, v_final_norm_w)))
    px, py, pc = lax.axis_index("x"), lax.axis_index("y"), lax.axis_index("c")
    me = 4 * px + 2 * py + pc
    dev, chip, core = (t.astype(jnp.int32).reshape(1) for t in (me, 2 * px + py, pc))

    trio = tuple(SMALL_SHARDED)
    got = _gather([(d[n], "slot", None, (N_DEV,) + d[n].shape, False) for n in trio for d in (w, m, v)], "gather_small")
    slots = {n: got[3 * i:3 * i + 3] for i, n in enumerate(trio)}
    p = {n: w[n] for n in SMALL_REPLICATED}
    for n in trio:
        p[n] = _from_slots(slots[n][0], SMALL_SHARDED[n])
    for n in BIG_KIND:
        p[n] = [None] * w[n].shape[0]
    plan = _Plan(w, m, v, p, dev, chip, core)
    loss_part, dx, grads = _local_step(x, positions, p, loss_target, plan)
    loss = lax.psum(loss_part[0, 0], AXES)
    plan.flush()
    out = {n: list(plan.res[n]) for n in BIG_KIND}

    small = SMALL_REPLICATED + trio
    as2d = lambda t: t.reshape(1, -1) if t.ndim == 1 else t
    g_sm = [as2d(grads[n]) for n in SMALL_REPLICATED] + [_to_slots(grads[n].reshape(p[n].shape), SMALL_SHARDED[n]) for n in trio]
    gathered = _gather([(g, "slot", None, (N_DEV,) + g.shape, False) for g in g_sm], "gather_small_grads")
    ws = [as2d(w[n]) for n in SMALL_REPLICATED] + [slots[n][0] for n in trio]
    ms = [as2d(m[n]) for n in SMALL_REPLICATED] + [slots[n][1] for n in trio]
    vs = [as2d(v[n]) for n in SMALL_REPLICATED] + [slots[n][2] for n in trio]
    sm_out = _small_adamw(gathered, ws, ms, vs)
    for i, n in enumerate(small):
        if n in SMALL_SHARDED:
            out[n] = [lax.dynamic_index_in_dim(sm_out[k][i], me, 0, keepdims=False) for k in range(4)]
        else:
            out[n] = [sm_out[k][i].reshape(w[n].shape) for k in range(4)]
    return (loss, dx, *[out[n][0] for n in W_NAMES], *[out[n][1] for n in W_NAMES], *[out[n][2] for n in W_NAMES],
            *[out[n][3] for n in W_NAMES])
```

```python
import functools
import math

import jax
import jax.numpy as jnp
import numpy as np
from jax import lax
from jax.experimental import pallas as pl
from jax.experimental.pallas import tpu as pltpu

F32 = jnp.float32
BF16 = jnp.bfloat16
SDS = jax.ShapeDtypeStruct

D_MODEL = 1024
DEPTH = 4
BLOCK = 128
ROPE_THETA = 10000.0
NORM_EPS = 1e-5
HEAD_DIM = 64
A_N_HEADS = 16
A_N_KV = 2
A_WINDOW = 128
A_Q_DIM = 1024
A_KV_DIM = 128
SSM_D_INNER = 2048
SSM_N_HEADS = 32
SSM_N_GROUPS = 8
SSM_HG = 4
SSM_D_STATE = 128
SSM_CONV = 4
SSM_CHUNK = 128
SSM_BC_DIM = 1024
SSM_CONV_DIM = 4096
C_PATTERNS = ((128, 1), (512, 4), (2048, 16))
C_HEADS = 16
ADAM_LR, ADAM_B1, ADAM_B2, ADAM_EPS, ADAM_WD, ADAM_STEP = 0.001, 0.9, 0.999, 1e-08, 0.01, 10

N_DEV = 8
AXES = ("x", "y", "c")
LANES = 128
VMEM_LIMIT = 56 * 1024 * 1024
STREAM_VMEM = 16 * 1024 * 1024
NEG = -1e30

NN = (((1,), (0,)), ((), ()))
NT = (((1,), (1,)), ((), ()))
TN = (((0,), (0,)), ((), ()))
HI = lax.Precision.HIGHEST


def _pick(n, cap, mult=LANES):
    best = None
    for t in range(mult, min(n, cap) + 1, mult):
        if n % t == 0:
            best = t
    return best if best is not None else n


def _params(sem):
    return pltpu.CompilerParams(dimension_semantics=sem, vmem_limit_bytes=VMEM_LIMIT)


def _bf(x):
    return x if x.dtype == BF16 else x.astype(BF16)


def _rot_half(y):
    n = y.shape[-1]
    lane = lax.broadcasted_iota(jnp.int32, y.shape, y.ndim - 1)
    return jnp.where((lane % HEAD_DIM) < HEAD_DIM // 2, -pltpu.roll(y, n - 32, y.ndim - 1), pltpu.roll(y, 32, y.ndim - 1))


def _rope(y, cos, sin, sign):
    reps = y.shape[-1] // LANES
    c = jnp.tile(cos, (1, reps)) if reps > 1 else cos
    s = jnp.tile(sin, (1, reps)) if reps > 1 else sin
    return y * c + sign * (_rot_half(y) * s)


MESH = pl.DeviceIdType.MESH
ANY = pl.BlockSpec(memory_space=pl.ANY)


class _Comm:
    def __init__(self, inputs, out_shapes, aliases, sems, phases):
        self.inputs, self.out_shapes, self.aliases, self.sems, self.phases = inputs, out_shapes, aliases, sems, phases


def _pc(body, args, *, out_shape, grid, in_specs, out_specs, name, sem, scratch_shapes=(), comm=None):
    single = not isinstance(out_shape, (tuple, list))
    outs, ospecs = ([out_shape], [out_specs]) if single else (list(out_shape), list(out_specs))
    unpack = (lambda r: r[0]) if single else (lambda r: tuple(r))
    if comm is None:
        res = pl.pallas_call(body, out_shape=outs, grid=grid, in_specs=list(in_specs), out_specs=ospecs,
                             scratch_shapes=list(scratch_shapes), name=name, compiler_params=_params(sem))(*args)
        return unpack(res)
    n_in, n_out, n_scr = len(in_specs), len(outs), len(scratch_shapes)
    c_in, c_out = len(comm.inputs), len(comm.out_shapes)
    total = math.prod(grid)
    steps = [min(total - 1, int(f * total)) for f, _ in comm.phases[:-1]]

    def wrapped(*refs):
        ins, cins = refs[:n_in], refs[n_in:n_in + c_in]
        o = refs[n_in + c_in:n_in + c_in + n_out]
        couts = refs[n_in + c_in + n_out:n_in + c_in + n_out + c_out]
        rest = refs[n_in + c_in + n_out + c_out:]
        scr, csems = rest[:n_scr], rest[n_scr:]
        step = pl.program_id(0)
        for ax in range(1, len(grid)):
            step = step * grid[ax] + pl.program_id(ax)
        for (_, fn), st in zip(comm.phases[:-1], steps):
            @pl.when(step == st)
            def _(fn=fn):
                fn(cins, couts, csems)
        body(*ins, *o, *scr)

        @pl.when(step == total - 1)
        def _():
            comm.phases[-1][1](cins, couts, csems)

    res = pl.pallas_call(
        wrapped, out_shape=outs + list(comm.out_shapes), grid=grid, in_specs=list(in_specs) + [ANY] * c_in,
        out_specs=ospecs + [ANY] * c_out, scratch_shapes=list(scratch_shapes) + list(comm.sems),
        input_output_aliases={n_in + i: n_out + j for i, j in comm.aliases.items()}, name=name,
        compiler_params=_params(("arbitrary",) * len(grid)),
    )(*args, *comm.inputs)
    return unpack(res[:n_out]), list(res[n_out:])


def _hosted(plan, name, run):
    comm = plan.take(name) if plan is not None else None
    if comm is None:
        return run(None)
    res, extra = run(comm)
    plan.give(name, extra)
    return res


MM_VMEM = 40 * 1024 * 1024
HBM_BYTES_PER_US = 2.5e6
STEP_US = 0.35


def _divisors(n, cands):
    return [c for c in cands if c <= n and n % c == 0] or [n]


def _mm_tiles(M, N, K, sa, sb, out_bytes, extra_bytes):
    best = None
    for tm in _divisors(M, (2048, 1024, 512, 256)):
        for tn in _divisors(N, (1024, 640, 512, 256, 128)):
            for tk in _divisors(K, (K, 2048, 1024, 640, 512)):
                nk = K // tk
                vmem = 2 * tm * tk * sa + 2 * tk * tn * sb + tm * tn * (2 * (out_bytes + extra_bytes) + 8 + (4 if nk > 1 else 0))
                if vmem > MM_VMEM:
                    continue
                a_traffic = M * K * sa * (1 if nk == 1 else N // tn)
                b_traffic = K * N * sb * (1 if (nk == 1 and N == tn) else M // tm)
                steps = (M // tm) * (N // tn) * nk
                cost = (a_traffic + b_traffic + M * N * (out_bytes + extra_bytes)) / HBM_BYTES_PER_US + steps * STEP_US
                cost += (M // tm) * (N // tn) * (nk - 1) * tm * tn * 8 / (4 * HBM_BYTES_PER_US)
                if best is None or cost < best[0]:
                    best = (cost, tm, tn, tk)
    assert best is not None, (M, N, K)
    return best[1:]


def _matmul(a, b, *, ta=False, tb=False, out_dtype=F32, bias=None, resid=None, mul=None, mul_scale=1.0,
            relu2=False, rope=None, rope_cols=0, name="mm", plan=None):
    M = a.shape[1] if ta else a.shape[0]
    K = a.shape[0] if ta else a.shape[1]
    N = b.shape[0] if tb else b.shape[1]
    assert (b.shape[1] if tb else b.shape[0]) == K
    out_bytes = jnp.dtype(out_dtype).itemsize * (2 if relu2 else 1)
    extra_bytes = (4 if resid is not None else 0) + (mul.dtype.itemsize if mul is not None else 0)
    tm, tn, tk = _mm_tiles(M, N, K, a.dtype.itemsize, b.dtype.itemsize, out_bytes, extra_bytes)
    nk = K // tk
    dims = (((0 if ta else 1,), (1 if tb else 0,)), ((), ()))

    def body(*refs):
        it = iter(refs)
        a_ref, b_ref = next(it), next(it)
        bias_ref = next(it) if bias is not None else None
        resid_ref = next(it) if resid is not None else None
        mul_ref = next(it) if mul is not None else None
        cos_ref, sin_ref = (next(it), next(it)) if rope is not None else (None, None)
        o_ref = next(it)
        o2_ref = next(it) if relu2 else None
        acc_ref = next(it) if nk > 1 else None
        k = pl.program_id(2)
        part = lax.dot_general(_bf(a_ref[...]), _bf(b_ref[...]), dims, preferred_element_type=F32)
        if nk > 1:
            @pl.when(k == 0)
            def _():
                acc_ref[...] = part

            @pl.when(k > 0)
            def _():
                acc_ref[...] += part

        @pl.when(k == nk - 1)
        def _():
            y = acc_ref[...] if nk > 1 else part
            if bias_ref is not None:
                y = y + bias_ref[...]
            if rope is not None:
                col = pl.program_id(1) * tn + lax.broadcasted_iota(jnp.int32, y.shape, 1)
                y = jnp.where(col < rope_cols, _rope(y, cos_ref[...], sin_ref[...], 1.0), y)
            if mul_ref is not None:
                y = y * (mul_ref[...].astype(F32) * mul_scale)
            if resid_ref is not None:
                y = y + resid_ref[...]
            if relu2:
                r = jnp.maximum(y, 0.0)
                o_ref[...] = r.astype(o_ref.dtype)
                o2_ref[...] = (r * r).astype(o2_ref.dtype)
            else:
                o_ref[...] = y.astype(o_ref.dtype)

    a_spec = pl.BlockSpec((tk, tm), lambda i, j, k: (k, i)) if ta else pl.BlockSpec((tm, tk), lambda i, j, k: (i, k))
    b_spec = pl.BlockSpec((tn, tk), lambda i, j, k: (j, k)) if tb else pl.BlockSpec((tk, tn), lambda i, j, k: (k, j))
    mn_spec = pl.BlockSpec((tm, tn), lambda i, j, k: (i, j))
    in_specs, args = [a_spec, b_spec], [a, b]
    if bias is not None:
        in_specs.append(pl.BlockSpec((1, tn), lambda i, j, k: (0, j)))
        args.append(bias)
    if resid is not None:
        in_specs.append(mn_spec)
        args.append(resid)
    if mul is not None:
        in_specs.append(mn_spec)
        args.append(mul)
    if rope is not None:
        in_specs += [pl.BlockSpec((tm, LANES), lambda i, j, k: (i, 0))] * 2
        args += [rope[0], rope[1]]
    out_shape = SDS((M, N), out_dtype)
    out_specs = mn_spec
    if relu2:
        out_shape, out_specs = (out_shape, out_shape), (mn_spec, mn_spec)
    return _hosted(plan, name, lambda comm: _pc(
        body, args, out_shape=out_shape, grid=(M // tm, N // tn, nk), in_specs=in_specs, out_specs=out_specs,
        scratch_shapes=[pltpu.VMEM((tm, tn), F32)] if nk > 1 else [], name=name, sem=("parallel", "parallel", "arbitrary"),
        comm=comm))


def _colsum(x, name):
    T, N = x.shape
    tm = _pick(T, 1024, 8)

    def body(x_ref, o_ref):
        s = jnp.sum(x_ref[...].astype(F32), axis=0, keepdims=True)

        @pl.when(pl.program_id(0) == 0)
        def _():
            o_ref[...] = s

        @pl.when(pl.program_id(0) > 0)
        def _():
            o_ref[...] += s

    return pl.pallas_call(
        body, out_shape=SDS((1, N), F32), grid=(T // tm,),
        in_specs=[pl.BlockSpec((tm, N), lambda i: (i, 0))], out_specs=pl.BlockSpec((1, N), lambda i: (0, 0)),
        name=name, compiler_params=_params(("arbitrary",)),
    )(x)


def _rmsnorm_fwd(h, w, name):
    T, D = h.shape
    tm = _pick(T, 512, 8)

    def body(h_ref, w_ref, o_ref):
        x = h_ref[...]
        rstd = lax.rsqrt(jnp.mean(x * x, axis=-1, keepdims=True) + NORM_EPS)
        o_ref[...] = (x * rstd * w_ref[...]).astype(BF16)

    return pl.pallas_call(
        body, out_shape=SDS((T, D), BF16), grid=(T // tm,),
        in_specs=[pl.BlockSpec((tm, D), lambda i: (i, 0)), pl.BlockSpec((1, D), lambda i: (0, 0))],
        out_specs=pl.BlockSpec((tm, D), lambda i: (i, 0)), name=name, compiler_params=_params(("parallel",)),
    )(h, w.reshape(1, D))


def _rmsnorm_bwd(h, du, w, dres, name):
    T, D = h.shape
    tm = _pick(T, 512, 8)

    def body(h_ref, du_ref, w_ref, dres_ref, dh_ref, dw_ref):
        x = h_ref[...]
        du_ = du_ref[...].astype(F32)
        rstd = lax.rsqrt(jnp.mean(x * x, axis=-1, keepdims=True) + NORM_EPS)
        g = du_ * w_ref[...]
        dh_ref[...] = dres_ref[...] + rstd * g - x * (rstd * rstd * rstd) * jnp.mean(g * x, axis=-1, keepdims=True)
        dw = jnp.sum(du_ * x * rstd, axis=0, keepdims=True)

        @pl.when(pl.program_id(0) == 0)
        def _():
            dw_ref[...] = dw

        @pl.when(pl.program_id(0) > 0)
        def _():
            dw_ref[...] += dw

    row = pl.BlockSpec((tm, D), lambda i: (i, 0))
    vec = pl.BlockSpec((1, D), lambda i: (0, 0))
    return pl.pallas_call(
        body, out_shape=(SDS((T, D), F32), SDS((1, D), F32)), grid=(T // tm,),
        in_specs=[row, row, vec, row], out_specs=(row, vec), name=name, compiler_params=_params(("arbitrary",)),
    )(h, du, w.reshape(1, D), dres)


def _final_loss(h, target, w):
    T, D = h.shape
    tm = _pick(T, 512, 8)

    def body(h_ref, t_ref, w_ref, dh_ref, dw_ref, loss_ref):
        x = h_ref[...]
        rstd = lax.rsqrt(jnp.mean(x * x, axis=-1, keepdims=True) + NORM_EPS)
        xn = x * rstd
        err = xn * w_ref[...] - t_ref[...]
        part = 0.5 * jnp.sum(jnp.mean(err * err, axis=-1, keepdims=True), axis=0, keepdims=True)
        dy = err * (1.0 / D)
        g = dy * w_ref[...]
        dh_ref[...] = rstd * g - x * (rstd * rstd * rstd) * jnp.mean(g * x, axis=-1, keepdims=True)
        dw = jnp.sum(dy * xn, axis=0, keepdims=True)
        lp = jnp.broadcast_to(part, (1, LANES))

        @pl.when(pl.program_id(0) == 0)
        def _():
            dw_ref[...] = dw
            loss_ref[...] = lp

        @pl.when(pl.program_id(0) > 0)
        def _():
            dw_ref[...] += dw
            loss_ref[...] += lp

    row = pl.BlockSpec((tm, D), lambda i: (i, 0))
    vec = pl.BlockSpec((1, D), lambda i: (0, 0))
    return pl.pallas_call(
        body, out_shape=(SDS((T, D), F32), SDS((1, D), F32), SDS((1, LANES), F32)), grid=(T // tm,),
        in_specs=[row, row, vec], out_specs=(row, vec, pl.BlockSpec((1, LANES), lambda i: (0, 0))),
        name="final_loss", compiler_params=_params(("arbitrary",)),
    )(h, target, w.reshape(1, D))


def _band_mask(i_blk, max_dist, first_ok):
    qi = lax.broadcasted_iota(jnp.int32, (BLOCK, 2 * BLOCK), 0)
    kj = lax.broadcasted_iota(jnp.int32, (BLOCK, 2 * BLOCK), 1)
    dist = qi + BLOCK - kj
    ok = (dist >= 0) & (dist <= max_dist)
    return ok & ((kj >= BLOCK) | first_ok)


def _pair(t, i):
    return t[:, LANES * i:LANES * (i + 1)]


def _low_half(shape):
    return lax.broadcasted_iota(jnp.int32, shape, len(shape) - 1) < HEAD_DIM


def _stack_heads(t):
    lo = _low_half(t.shape)
    z = jnp.zeros_like(t)
    return jnp.concatenate([jnp.where(lo, t, z), jnp.where(lo, z, t)], axis=0)


def _swap_halves(t):
    return jnp.concatenate([t[:, HEAD_DIM:], t[:, :HEAD_DIM]], axis=1)


def _kv_operand(kv, kv_swapped, h0, n_kv, n_heads):
    R = n_heads // n_kv
    if R == 1:
        return _pair(kv, h0 // 2)
    assert kv.shape[1] == LANES and R % 2 == 0, "grouped queries: one 128-lane tile of kv heads, both heads of a pair in one group"
    g = h0 // R
    t, ts = _pair(kv, g // 2), _pair(kv_swapped, g // 2)
    lo = _low_half(t.shape)
    return jnp.where(lo, t, ts) if g % 2 == 0 else jnp.where(lo, ts, t)


def _lane_place(cols):
    m = cols[0].shape[0]
    lane = lax.broadcasted_iota(jnp.int32, (m, LANES), 1)
    out = jnp.zeros((m, LANES), F32)
    for h, c in enumerate(cols):
        out = jnp.where(lane == h, c, out)
    return out


def _attn_specs(B, S, d, C, n_heads, n_kv, q_col, k_col, v_col):
    kvw = n_kv * HEAD_DIM
    qw = n_heads * HEAD_DIM
    cq, ck = (C // qw if d > 1 else 0), (C // kvw if d > 1 else 0)
    q_spec = pl.BlockSpec((1, BLOCK, qw), lambda b, r, i: (b, i, r * cq + q_col // qw))
    kc = pl.BlockSpec((1, BLOCK, kvw), lambda b, r, i: (b, i, r * ck + k_col // kvw))
    kp = pl.BlockSpec((1, BLOCK, kvw), lambda b, r, i: (b, jnp.maximum(i - 1, 0), r * ck + k_col // kvw))
    vc = pl.BlockSpec((1, BLOCK, kvw), lambda b, r, i: (b, i, r * ck + v_col // kvw))
    vp = pl.BlockSpec((1, BLOCK, kvw), lambda b, r, i: (b, jnp.maximum(i - 1, 0), r * ck + v_col // kvw))
    return q_spec, kp, kc, vp, vc


def _attn_fwd(qkv, B, S, d, *, n_heads, n_kv, q_col, k_col, v_col, max_dist, sinks, name, plan=None):
    C = qkv.shape[1]
    Ls = S // d
    nb = Ls // BLOCK
    qw = n_heads * HEAD_DIM
    R = n_heads // n_kv
    qkv3 = qkv.reshape(B, Ls, d * C)
    scale = HEAD_DIM ** -0.5

    def body(*refs):
        if sinks is not None:
            sink_ref, q_ref, kp_ref, kc_ref, vp_ref, vc_ref, o_ref, lse_ref = refs
        else:
            q_ref, kp_ref, kc_ref, vp_ref, vc_ref, o_ref, lse_ref = refs
        i = pl.program_id(2)
        mask1 = _band_mask(i, max_dist, i > 0)
        mask = jnp.concatenate([mask1, mask1], axis=0)
        q = q_ref[0]
        kk = jnp.concatenate([kp_ref[0], kc_ref[0]], axis=0)
        vv = jnp.concatenate([vp_ref[0], vc_ref[0]], axis=0)
        kks, vvs = (_swap_halves(kk), _swap_halves(vv)) if R > 1 else (None, None)
        lo = _low_half((BLOCK, LANES))
        top = lax.broadcasted_iota(jnp.int32, (2 * BLOCK, 1), 0) < BLOCK
        lses, tiles = [], []
        for t in range(n_heads // 2):
            k2 = _kv_operand(kk, kks, 2 * t, n_kv, n_heads)
            v2 = _kv_operand(vv, vvs, 2 * t, n_kv, n_heads)
            s = lax.dot_general(_stack_heads(_pair(q, t)), k2, NT, preferred_element_type=F32) * scale
            s = jnp.where(mask, s, NEG)
            m = jnp.max(s, axis=-1, keepdims=True)
            if sinks is not None:
                sk = jnp.where(top, sink_ref[2 * t], sink_ref[2 * t + 1])
                m = jnp.maximum(m, sk)
            p = jnp.exp(s - m)
            den = jnp.sum(p, axis=-1, keepdims=True)
            if sinks is not None:
                den = den + jnp.exp(sk - m)
            lse2 = m + jnp.log(den)
            o2 = jnp.dot((p / den).astype(BF16), v2, preferred_element_type=F32)
            tiles.append(jnp.where(lo, o2[:BLOCK], o2[BLOCK:]))
            lses += [lse2[:BLOCK], lse2[BLOCK:]]
        o_ref[0] = jnp.concatenate(tiles, axis=-1)
        lse_ref[0] = _lane_place(lses)

    specs = list(_attn_specs(B, S, d, C, n_heads, n_kv, q_col, k_col, v_col))
    args = [qkv3] * 5
    if sinks is not None:
        specs = [pl.BlockSpec(memory_space=pltpu.SMEM)] + specs
        args = [sinks] + args
    o3, lse3 = _hosted(plan, name, lambda comm: _pc(
        body, args, out_shape=(SDS((B, Ls, d * qw), F32), SDS((B, Ls, d * LANES), F32)), grid=(B, d, nb), in_specs=specs,
        out_specs=(pl.BlockSpec((1, BLOCK, qw), lambda b, r, i: (b, i, r)), pl.BlockSpec((1, BLOCK, LANES), lambda b, r, i: (b, i, r))),
        name=name, sem=("parallel", "parallel", "parallel"), comm=comm))
    return o3.reshape(B * S, qw), lse3.reshape(B * S, LANES)


def _attn_dq(qkv, do, lse, delta, cos, sin, B, S, d, *, n_heads, n_kv, q_col, k_col, v_col, max_dist, name, plan=None):
    C = qkv.shape[1]
    Ls = S // d
    nb = Ls // BLOCK
    qw = n_heads * HEAD_DIM
    R = n_heads // n_kv
    scale = HEAD_DIM ** -0.5

    def body(q_ref, kp_ref, kc_ref, vp_ref, vc_ref, do_ref, lse_ref, dl_ref, cos_ref, sin_ref, dq_ref):
        i = pl.program_id(2)
        mask1 = _band_mask(i, max_dist, i > 0)
        mask = jnp.concatenate([mask1, mask1], axis=0)
        q = q_ref[0]
        do_ = do_ref[0]
        kk = jnp.concatenate([kp_ref[0], kc_ref[0]], axis=0)
        vv = jnp.concatenate([vp_ref[0], vc_ref[0]], axis=0)
        kks, vvs = (_swap_halves(kk), _swap_halves(vv)) if R > 1 else (None, None)
        lo = _low_half((BLOCK, LANES))
        lse_t, dl_t = lse_ref[0], dl_ref[0]
        tiles = []
        for t in range(n_heads // 2):
            k2 = _kv_operand(kk, kks, 2 * t, n_kv, n_heads)
            v2 = _kv_operand(vv, vvs, 2 * t, n_kv, n_heads)
            lse2 = jnp.concatenate([lse_t[:, 2 * t:2 * t + 1], lse_t[:, 2 * t + 1:2 * t + 2]], axis=0)
            dl2 = jnp.concatenate([dl_t[:, 2 * t:2 * t + 1], dl_t[:, 2 * t + 1:2 * t + 2]], axis=0)
            s = lax.dot_general(_stack_heads(_pair(q, t)), k2, NT, preferred_element_type=F32) * scale
            p = jnp.where(mask, jnp.exp(s - lse2), 0.0)
            dp = lax.dot_general(_stack_heads(_pair(do_, t)), v2, NT, preferred_element_type=F32)
            ds = p * (dp - dl2)
            dq2 = jnp.dot(ds.astype(BF16), k2, preferred_element_type=F32) * scale
            tiles.append(jnp.where(lo, dq2[:BLOCK], dq2[BLOCK:]))
        dq = jnp.concatenate(tiles, axis=-1)
        dq_ref[0] = _rope(dq, cos_ref[0], sin_ref[0], -1.0).astype(BF16)

    qs, kp, kc, vp, vc = _attn_specs(B, S, d, C, n_heads, n_kv, q_col, k_col, v_col)
    row_q = pl.BlockSpec((1, BLOCK, qw), lambda b, r, i: (b, i, r))
    row_l = pl.BlockSpec((1, BLOCK, LANES), lambda b, r, i: (b, i, r))
    qkv3 = qkv.reshape(B, Ls, d * C)
    v3 = lambda t, w: t.reshape(B, Ls, d * w)
    args = (qkv3, qkv3, qkv3, qkv3, qkv3, v3(do, qw), v3(lse, LANES), v3(delta, LANES), v3(cos, LANES), v3(sin, LANES))
    dq3 = _hosted(plan, name, lambda comm: _pc(
        body, args, out_shape=SDS((B, Ls, d * qw), BF16), grid=(B, d, nb),
        in_specs=[qs, kp, kc, vp, vc, row_q, row_l, row_l, row_l, row_l], out_specs=row_q,
        name=name, sem=("parallel", "parallel", "parallel"), comm=comm))
    return dq3.reshape(B * S, qw)


def _attn_dkv(qkv, do, lse, delta, cos, sin, B, S, d, *, n_heads, n_kv, q_col, k_col, v_col, max_dist, name, plan=None):
    C = qkv.shape[1]
    Ls = S // d
    nb = Ls // BLOCK
    qw = n_heads * HEAD_DIM
    kvw = n_kv * HEAD_DIM
    R = n_heads // n_kv
    scale = HEAD_DIM ** -0.5
    cq, ck = (C // qw if d > 1 else 0), (C // kvw if d > 1 else 0)

    def body(k_ref, v_ref, q0_ref, q1_ref, do0_ref, do1_ref, lse0_ref, lse1_ref, dl0_ref, dl1_ref, cos_ref, sin_ref,
             dk_ref, dv_ref):
        j = pl.program_id(2)
        kj = lax.broadcasted_iota(jnp.int32, (BLOCK, BLOCK), 0)
        qi = lax.broadcasted_iota(jnp.int32, (BLOCK, BLOCK), 1)
        dist0 = qi - kj
        dist1 = qi + BLOCK - kj
        mask0 = (dist0 >= 0) & (dist0 <= max_dist)
        mask1 = (dist1 <= max_dist) & (j + 1 < nb)
        kb, vb = k_ref[0], v_ref[0]
        kbs, vbs = (_swap_halves(kb), _swap_halves(vb)) if R > 1 else (None, None)
        sides = ((q0_ref[0], do0_ref[0], lse0_ref[0].T, dl0_ref[0].T, mask0), (q1_ref[0], do1_ref[0], lse1_ref[0].T, dl1_ref[0].T, mask1))
        n_acc = n_kv if R > 1 else n_kv // 2
        dks = [jnp.zeros((BLOCK, LANES), F32) for _ in range(n_acc)]
        dvs = [jnp.zeros((BLOCK, LANES), F32) for _ in range(n_acc)]
        for t in range(n_heads // 2):
            k2 = _kv_operand(kb, kbs, 2 * t, n_kv, n_heads)
            v2 = _kv_operand(vb, vbs, 2 * t, n_kv, n_heads)
            a = (2 * t) // R if R > 1 else t
            for (q, do_, lse_r, dl_r, mask) in sides:
                q2, do2 = _stack_heads(_pair(q, t)), _stack_heads(_pair(do_, t))
                s = lax.dot_general(k2, q2, NT, preferred_element_type=F32) * scale
                dp = lax.dot_general(v2, do2, NT, preferred_element_type=F32)
                ps, dss = [], []
                for half in (0, 1):
                    h = 2 * t + half
                    sl = slice(BLOCK * half, BLOCK * (half + 1))
                    p = jnp.where(mask, jnp.exp(s[:, sl] - lse_r[h:h + 1, :]), 0.0)
                    ps.append(p)
                    dss.append(p * (dp[:, sl] - dl_r[h:h + 1, :]))
                dvs[a] = dvs[a] + jnp.dot(jnp.concatenate(ps, axis=1).astype(BF16), do2, preferred_element_type=F32)
                dks[a] = dks[a] + jnp.dot(jnp.concatenate(dss, axis=1).astype(BF16), q2, preferred_element_type=F32)
        if R > 1:
            lo = _low_half((BLOCK, LANES))
            fold = lambda x: x + pltpu.roll(x, HEAD_DIM, 1)
            dks = [jnp.where(lo, fold(dks[2 * t]), fold(dks[2 * t + 1])) for t in range(n_kv // 2)]
            dvs = [jnp.where(lo, fold(dvs[2 * t]), fold(dvs[2 * t + 1])) for t in range(n_kv // 2)]
        dk_t = jnp.concatenate(dks, axis=-1) * scale
        dk_ref[0] = _rope(dk_t, cos_ref[0], sin_ref[0], -1.0).astype(BF16)
        dv_ref[0] = jnp.concatenate(dvs, axis=-1).astype(BF16)

    nxt = lambda j: jnp.minimum(j + 1, nb - 1)
    k_spec = pl.BlockSpec((1, BLOCK, kvw), lambda b, r, j: (b, j, r * ck + k_col // kvw))
    v_spec = pl.BlockSpec((1, BLOCK, kvw), lambda b, r, j: (b, j, r * ck + v_col // kvw))
    q0 = pl.BlockSpec((1, BLOCK, qw), lambda b, r, j: (b, j, r * cq + q_col // qw))
    q1 = pl.BlockSpec((1, BLOCK, qw), lambda b, r, j: (b, nxt(j), r * cq + q_col // qw))
    w0 = lambda w: pl.BlockSpec((1, BLOCK, w), lambda b, r, j: (b, j, r))
    w1 = lambda w: pl.BlockSpec((1, BLOCK, w), lambda b, r, j: (b, nxt(j), r))
    qkv3 = qkv.reshape(B, Ls, d * C)
    v3 = lambda t, w: t.reshape(B, Ls, d * w)
    do3, lse3, dl3 = v3(do, qw), v3(lse, LANES), v3(delta, LANES)
    args = (qkv3, qkv3, qkv3, qkv3, do3, do3, lse3, lse3, dl3, dl3, v3(cos, LANES), v3(sin, LANES))
    dk3, dv3 = _hosted(plan, name, lambda comm: _pc(
        body, args, out_shape=(SDS((B, Ls, d * kvw), BF16), SDS((B, Ls, d * kvw), BF16)), grid=(B, d, nb),
        in_specs=[k_spec, v_spec, q0, q1, w0(qw), w1(qw), w0(LANES), w1(LANES), w0(LANES), w1(LANES), w0(LANES), w0(LANES)],
        out_specs=(w0(kvw), w0(kvw)), name=name, sem=("parallel", "parallel", "parallel"), comm=comm))
    return dk3.reshape(B * S, kvw), dv3.reshape(B * S, kvw)


def _head_expand():
    r = lax.broadcasted_iota(jnp.int32, (LANES, C_HEADS * HEAD_DIM), 0)
    c = lax.broadcasted_iota(jnp.int32, (LANES, C_HEADS * HEAD_DIM), 1)
    return jnp.where(c // HEAD_DIM == r, 1.0, 0.0).astype(F32)


def _delta(do, o, lse=None, sinks_row=None, name="delta"):
    T, W = do.shape
    tm = _pick(T, 512, 8)
    with_sink = sinks_row is not None

    def body(*refs):
        if with_sink:
            do_ref, o_ref, lse_ref, sk_ref, dl_ref, dob_ref, ds_ref = refs
        else:
            do_ref, o_ref, dl_ref, dob_ref = refs
        do_ = do_ref[...]
        dl = lax.dot_general(do_ * o_ref[...], _head_expand(), NT, preferred_element_type=F32, precision=HI)
        dl_ref[...] = dl
        dob_ref[...] = do_.astype(BF16)
        if with_sink:
            lane = lax.broadcasted_iota(jnp.int32, dl.shape, 1)
            contrib = jnp.where(lane < A_N_HEADS, -jnp.exp(sk_ref[...] - lse_ref[...]) * dl, 0.0)
            part = jnp.sum(contrib, axis=0, keepdims=True)

            @pl.when(pl.program_id(0) == 0)
            def _():
                ds_ref[...] = part

            @pl.when(pl.program_id(0) > 0)
            def _():
                ds_ref[...] += part

    row_w = pl.BlockSpec((tm, W), lambda i: (i, 0))
    row_l = pl.BlockSpec((tm, LANES), lambda i: (i, 0))
    vec_l = pl.BlockSpec((1, LANES), lambda i: (0, 0))
    if with_sink:
        return pl.pallas_call(
            body, out_shape=(SDS((T, LANES), F32), SDS((T, W), BF16), SDS((1, LANES), F32)), grid=(T // tm,),
            in_specs=[row_w, row_w, row_l, vec_l], out_specs=(row_l, row_w, vec_l), name=name,
            compiler_params=_params(("arbitrary",)),
        )(do, o, lse, sinks_row)
    return pl.pallas_call(
        body, out_shape=(SDS((T, LANES), F32), SDS((T, W), BF16)), grid=(T // tm,),
        in_specs=[row_w, row_w], out_specs=(row_l, row_w), name=name, compiler_params=_params(("parallel",)),
    )(do, o)


def _merge(os_, lses):
    T, W = os_[0].shape
    tm = _pick(T, 512, 8)

    def body(o0, o1, o2, l0, l1, l2, o_ref, lse_ref):
        ls = [l0[...], l1[...], l2[...]]
        m = jnp.maximum(jnp.maximum(ls[0], ls[1]), ls[2])
        ws = [jnp.exp(l - m) for l in ls]
        tot = ws[0] + ws[1] + ws[2]
        lse_ref[...] = m + jnp.log(tot)
        e = _head_expand()
        acc = jnp.zeros((tm, W), F32)
        for w, o in zip(ws, (o0, o1, o2)):
            acc = acc + jnp.dot(w / tot, e, preferred_element_type=F32, precision=HI) * o[...]
        o_ref[...] = acc

    row_w = pl.BlockSpec((tm, W), lambda i: (i, 0))
    row_l = pl.BlockSpec((tm, LANES), lambda i: (i, 0))
    return pl.pallas_call(
        body, out_shape=(SDS((T, W), F32), SDS((T, LANES), F32)), grid=(T // tm,),
        in_specs=[row_w] * 3 + [row_l] * 3, out_specs=(row_w, row_l), name="c_merge", compiler_params=_params(("parallel",)),
    )(*os_, *lses)


CONV_TC = 256


def _conv_pre(x, w, bias):
    row = lax.broadcasted_iota(jnp.int32, x.shape, 0)
    acc = x * w[SSM_CONV - 1:SSM_CONV, :] + bias
    for k in range(1, SSM_CONV):
        acc = acc + jnp.where(row >= k, pltpu.roll(x, k, 0), 0.0) * w[SSM_CONV - 1 - k:SSM_CONV - k, :]
    return acc


def _conv_fwd(zx3, w, bias):
    B, S, _ = zx3.shape
    off = SSM_D_INNER // CONV_TC

    def body(x_ref, w_ref, b_ref, o_ref):
        v = _conv_pre(x_ref[0], w_ref[...], b_ref[...])
        o_ref[0] = v * jax.nn.sigmoid(v)

    return pl.pallas_call(
        body, out_shape=SDS((B, S, SSM_CONV_DIM), F32), grid=(B, SSM_CONV_DIM // CONV_TC),
        in_specs=[pl.BlockSpec((1, S, CONV_TC), lambda b, j: (b, 0, j + off)),
                  pl.BlockSpec((SSM_CONV, CONV_TC), lambda b, j: (0, j)), pl.BlockSpec((1, CONV_TC), lambda b, j: (0, j))],
        out_specs=pl.BlockSpec((1, S, CONV_TC), lambda b, j: (b, 0, j)), name="b_conv_fwd",
        compiler_params=_params(("parallel", "parallel")),
    )(zx3, w, bias)


def _conv_bwd(zx3, dxc, w, bias, col0, name):
    B, S, n = dxc.shape
    tc = _pick(n, CONV_TC)
    off_x = (SSM_D_INNER + col0) // tc
    off_w = col0 // tc

    def body(x_ref, d_ref, w_ref, b_ref, dx_ref, dw_ref, db_ref):
        x = x_ref[0]
        wv = w_ref[...]
        v = _conv_pre(x, wv, b_ref[...])
        sg = jax.nn.sigmoid(v)
        dc = d_ref[0] * (sg * (1.0 + v * (1.0 - sg)))
        row = lax.broadcasted_iota(jnp.int32, x.shape, 0)
        dx = dc * wv[SSM_CONV - 1:SSM_CONV, :]
        dws = [jnp.sum(dc * x, axis=0, keepdims=True)]
        for k in range(1, SSM_CONV):
            dx = dx + jnp.where(row < S - k, pltpu.roll(dc, S - k, 0), 0.0) * wv[SSM_CONV - 1 - k:SSM_CONV - k, :]
            dws.append(jnp.sum(dc * jnp.where(row >= k, pltpu.roll(x, k, 0), 0.0), axis=0, keepdims=True))
        dx_ref[0] = dx.astype(BF16)
        ridx = lax.broadcasted_iota(jnp.int32, (SSM_CONV, tc), 0)
        dw = jnp.zeros((SSM_CONV, tc), F32)
        for k in range(SSM_CONV):
            dw = jnp.where(ridx == SSM_CONV - 1 - k, dws[k], dw)
        db = jnp.sum(dc, axis=0, keepdims=True)

        @pl.when(pl.program_id(1) == 0)
        def _():
            dw_ref[...] = dw
            db_ref[...] = db

        @pl.when(pl.program_id(1) > 0)
        def _():
            dw_ref[...] += dw
            db_ref[...] += db

    return pl.pallas_call(
        body, out_shape=(SDS((B, S, n), BF16), SDS((SSM_CONV, n), F32), SDS((1, n), F32)), grid=(n // tc, B),
        in_specs=[pl.BlockSpec((1, S, tc), lambda j, b: (b, 0, j + off_x)), pl.BlockSpec((1, S, tc), lambda j, b: (b, 0, j)),
                  pl.BlockSpec((SSM_CONV, tc), lambda j, b: (0, j + off_w)), pl.BlockSpec((1, tc), lambda j, b: (0, j + off_w))],
        out_specs=(pl.BlockSpec((1, S, tc), lambda j, b: (b, 0, j)), pl.BlockSpec((SSM_CONV, tc), lambda j, b: (0, j)),
                   pl.BlockSpec((1, tc), lambda j, b: (0, j))),
        name=name, compiler_params=_params(("parallel", "arbitrary")),
    )(zx3, dxc, w, bias)


def _ssd_common(x, Bm, Cm, dtc_raw, dtr_raw, pr, pc):
    Q = SSM_CHUNK
    zc = dtc_raw + pr[0:1, :]
    dt_c = jax.nn.softplus(zc)
    dt_r = jax.nn.softplus(dtr_raw + pc[:, 0:1])
    A_r = -jnp.exp(pr[1:2, :])
    A_c = -jnp.exp(pc[:, 1:2])
    row = lax.broadcasted_iota(jnp.int32, (Q, Q), 0)
    col = lax.broadcasted_iota(jnp.int32, (Q, Q), 1)
    tril = jnp.where(row >= col, 1.0, 0.0).astype(F32)
    cs_c = jnp.dot(tril, dt_c * A_r, preferred_element_type=F32, precision=HI)
    cs_r = lax.dot_general(dt_r * A_c, tril, NT, preferred_element_type=F32, precision=HI)
    return zc, dt_c, A_r, cs_c, cs_r, row, col, tril


def _ssd_fwd(xc3, dtc, dtr, prow, pcol, plan=None):
    B, S, _ = xc3.shape
    Q, G, HG, P, N = SSM_CHUNK, SSM_N_GROUPS, SSM_HG, HEAD_DIM, SSM_D_STATE
    nc = S // Q
    xw = HG * P

    def body(x_ref, b_ref, c_ref, dtc_ref, dtr_ref, pr_ref, pc_ref, y_ref, st_ref, state):
        c = pl.program_id(2)

        @pl.when(c == 0)
        def _():
            state[...] = jnp.zeros_like(state)

        x, Bm, Cm = x_ref[0], b_ref[0], c_ref[0]
        pr = pr_ref[0]
        _, dt_c, _, cs_c, cs_r, row, col, _ = _ssd_common(x, Bm, Cm, dtc_ref[0, 0], dtr_ref[0, 0], pr, pc_ref[0])
        Bb, Cb = Bm.astype(BF16), Cm.astype(BF16)
        CB = lax.dot_general(Cb, Bb, NT, preferred_element_type=F32)
        ys = []
        for hg in range(HG):
            xh = x[:, P * hg:P * (hg + 1)]
            xt = xh * dt_c[:, hg:hg + 1]
            csc, csr = cs_c[:, hg:hg + 1], cs_r[hg:hg + 1, :]
            L = jnp.where(row >= col, jnp.exp(jnp.minimum(csc - csr, 0.0)), 0.0)
            ydiag = jnp.dot((CB * L).astype(BF16), xt.astype(BF16), preferred_element_type=F32)
            Sh = state[hg]
            yoff = lax.dot_general(Cb, Sh.astype(BF16), NT, preferred_element_type=F32) * jnp.exp(csc)
            ys.append(ydiag + yoff + pr[2:3, hg:hg + 1] * xh)
            st_ref[0, 0, 0, P * hg:P * (hg + 1), :] = Sh
            csq = csc[Q - 1:Q, :]
            upd = lax.dot_general((xt * jnp.exp(csq - csc)).astype(BF16), Bb, TN, preferred_element_type=F32)
            state[hg] = Sh * jnp.exp(csq) + upd
        y_ref[0] = jnp.concatenate([jnp.concatenate(ys[0:2], axis=-1), jnp.concatenate(ys[2:4], axis=-1)], axis=-1)

    bo, co = SSM_D_INNER // N, (SSM_D_INNER + SSM_BC_DIM) // N
    return _hosted(plan, "b_ssd_fwd", lambda comm: _pc(
        body, (xc3, xc3, xc3, dtc, dtr, prow, pcol),
        out_shape=(SDS((B, S, SSM_D_INNER), F32), SDS((B, G, nc, xw, N), F32)), grid=(G, B, nc),
        in_specs=[pl.BlockSpec((1, Q, xw), lambda g, b, c: (b, c, g)), pl.BlockSpec((1, Q, N), lambda g, b, c: (b, c, bo + g)),
                  pl.BlockSpec((1, Q, N), lambda g, b, c: (b, c, co + g)), pl.BlockSpec((1, 1, Q, HG), lambda g, b, c: (b, g, c, 0)),
                  pl.BlockSpec((1, 1, HG, Q), lambda g, b, c: (b, g, 0, c)), pl.BlockSpec((1, 3, HG), lambda g, b, c: (g, 0, 0)),
                  pl.BlockSpec((1, HG, 3), lambda g, b, c: (g, 0, 0))],
        out_specs=(pl.BlockSpec((1, Q, xw), lambda g, b, c: (b, c, g)), pl.BlockSpec((1, 1, 1, xw, N), lambda g, b, c: (b, g, c, 0, 0))),
        scratch_shapes=[pltpu.VMEM((HG, P, N), F32)], name="b_ssd_fwd", sem=("parallel", "arbitrary", "arbitrary"), comm=comm))


def _ssd_bwd(xc3, dtc, dtr, prow, pcol, states, dy3, plan=None):
    B, S, _ = xc3.shape
    Q, G, HG, P, N = SSM_CHUNK, SSM_N_GROUPS, SSM_HG, HEAD_DIM, SSM_D_STATE
    nc = S // Q
    xw = HG * P

    def body(x_ref, b_ref, c_ref, dtc_ref, dtr_ref, pr_ref, pc_ref, st_ref, dy_ref,
             dx_ref, db_ref, dc_ref, ddt_ref, dpar_ref, dstate):
        bi, ci = pl.program_id(1), pl.program_id(2)

        @pl.when(ci == 0)
        def _():
            dstate[...] = jnp.zeros_like(dstate)

        x, Bm, Cm, dy = x_ref[0], b_ref[0], c_ref[0], dy_ref[0]
        pr = pr_ref[0]
        zc, dt_c, A_r, cs_c, cs_r, row, col, tril = _ssd_common(x, Bm, Cm, dtc_ref[0, 0], dtr_ref[0, 0], pr, pc_ref[0])
        Bb, Cb = Bm.astype(BF16), Cm.astype(BF16)
        CB = lax.dot_general(Cb, Bb, NT, preferred_element_type=F32)
        CBt = lax.dot_general(Bb, Cb, NT, preferred_element_type=F32)
        lane4 = lax.broadcasted_iota(jnp.int32, (Q, HG), 1)
        lane4r = lax.broadcasted_iota(jnp.int32, (1, HG), 1)
        rowq = lax.broadcasted_iota(jnp.int32, (Q, 1), 0)
        dB = jnp.zeros((Q, N), F32)
        dC = jnp.zeros((Q, N), F32)
        dcs4 = jnp.zeros((Q, HG), F32)
        dtx4 = jnp.zeros((Q, HG), F32)
        dD4 = jnp.zeros((1, HG), F32)
        dxts, xhs, dyhs = [], [], []
        for hg in range(HG):
            xh = x[:, P * hg:P * (hg + 1)]
            dyh = dy[:, P * hg:P * (hg + 1)]
            xt = xh * dt_c[:, hg:hg + 1]
            xtb, dyb = xt.astype(BF16), dyh.astype(BF16)
            csc, csr = cs_c[:, hg:hg + 1], cs_r[hg:hg + 1, :]
            L = jnp.where(row >= col, jnp.exp(jnp.minimum(csc - csr, 0.0)), 0.0)
            Lt = jnp.where(col >= row, jnp.exp(jnp.minimum(csr - csc, 0.0)), 0.0)
            M, Mt = CB * L, CBt * Lt
            Sh = st_ref[0, 0, 0, P * hg:P * (hg + 1), :]
            dSh = dstate[hg]
            Shb, dShb = Sh.astype(BF16), dSh.astype(BF16)
            ecs = jnp.exp(csc)
            csq = csc[Q - 1:Q, :]
            dec = jnp.exp(csq - csc)
            dxt = jnp.dot(Mt.astype(BF16), dyb, preferred_element_type=F32)
            dxt = dxt + lax.dot_general(Bb, dShb, NT, preferred_element_type=F32) * dec
            Gm = lax.dot_general(dyb, xtb, NT, preferred_element_type=F32)
            Gt = lax.dot_general(xtb, dyb, NT, preferred_element_type=F32)
            dC = dC + jnp.dot((Gm * L).astype(BF16), Bb, preferred_element_type=F32)
            dB = dB + jnp.dot((Gt * Lt).astype(BF16), Cb, preferred_element_type=F32)
            dC = dC + jnp.dot(dyb, Shb, preferred_element_type=F32) * ecs
            dBst = jnp.dot(xtb, dShb, preferred_element_type=F32) * dec
            dB = dB + dBst
            dcs = jnp.sum(Gm * M, axis=1, keepdims=True) - jnp.sum(Gt * Mt, axis=1, keepdims=True)
            yoff = lax.dot_general(Cb, Shb, NT, preferred_element_type=F32) * ecs
            dcs = dcs + jnp.sum(yoff * dyh, axis=1, keepdims=True)
            r = jnp.sum(dBst * Bm, axis=1, keepdims=True)
            dcs = dcs - r
            extra = jnp.sum(r, axis=0, keepdims=True) + jnp.exp(csq) * jnp.sum(
                jnp.sum(dSh * Sh, axis=1, keepdims=True), axis=0, keepdims=True)
            dcs = dcs + jnp.where(rowq == Q - 1, extra, 0.0)
            dcs4 = jnp.where(lane4 == hg, dcs, dcs4)
            dtx4 = jnp.where(lane4 == hg, jnp.sum(dxt * xh, axis=1, keepdims=True), dtx4)
            dD4 = jnp.where(lane4r == hg, jnp.sum(jnp.sum(dyh * xh, axis=1, keepdims=True), axis=0, keepdims=True), dD4)
            dstate[hg] = dSh * jnp.exp(csq) + lax.dot_general((dyh * ecs).astype(BF16), Cb, TN, preferred_element_type=F32)
            dxts.append(dxt)
            xhs.append(xh)
            dyhs.append(dyh)
        da4 = lax.dot_general(tril, dcs4, TN, preferred_element_type=F32, precision=HI)
        ddt4 = da4 * A_r + dtx4
        ddtraw = ddt4 * jax.nn.sigmoid(zc)
        ddt_ref[0, 0] = ddtraw
        dxs = [dxts[hg] * dt_c[:, hg:hg + 1] + pr[2:3, hg:hg + 1] * dyhs[hg] for hg in range(HG)]
        dx_ref[0] = jnp.concatenate([jnp.concatenate(dxs[0:2], axis=-1), jnp.concatenate(dxs[2:4], axis=-1)], axis=-1)
        db_ref[0] = dB
        dc_ref[0] = dC
        d_bias = jnp.sum(ddtraw, axis=0, keepdims=True)
        d_alog = jnp.sum(da4 * dt_c, axis=0, keepdims=True) * A_r
        r3 = lax.broadcasted_iota(jnp.int32, (3, HG), 0)
        dpar = jnp.where(r3 == 0, d_bias, jnp.where(r3 == 1, d_alog, dD4))
        first = (bi == 0) & (ci == 0)

        @pl.when(first)
        def _():
            dpar_ref[0] = dpar

        @pl.when(jnp.logical_not(first))
        def _():
            dpar_ref[0] += dpar

    rc = lambda c: nc - 1 - c
    bo, co = SSM_D_INNER // N, (SSM_D_INNER + SSM_BC_DIM) // N
    return _hosted(plan, "b_ssd_bwd", lambda comm: _pc(
        body, (xc3, xc3, xc3, dtc, dtr, prow, pcol, states, dy3),
        out_shape=(SDS((B, S, SSM_D_INNER), F32), SDS((B, S, SSM_BC_DIM), F32), SDS((B, S, SSM_BC_DIM), F32),
                   SDS((B, G, S, HG), F32), SDS((G, 3, HG), F32)),
        grid=(G, B, nc),
        in_specs=[pl.BlockSpec((1, Q, xw), lambda g, b, c: (b, rc(c), g)), pl.BlockSpec((1, Q, N), lambda g, b, c: (b, rc(c), bo + g)),
                  pl.BlockSpec((1, Q, N), lambda g, b, c: (b, rc(c), co + g)), pl.BlockSpec((1, 1, Q, HG), lambda g, b, c: (b, g, rc(c), 0)),
                  pl.BlockSpec((1, 1, HG, Q), lambda g, b, c: (b, g, 0, rc(c))), pl.BlockSpec((1, 3, HG), lambda g, b, c: (g, 0, 0)),
                  pl.BlockSpec((1, HG, 3), lambda g, b, c: (g, 0, 0)),
                  pl.BlockSpec((1, 1, 1, xw, N), lambda g, b, c: (b, g, rc(c), 0, 0)), pl.BlockSpec((1, Q, xw), lambda g, b, c: (b, rc(c), g))],
        out_specs=(pl.BlockSpec((1, Q, xw), lambda g, b, c: (b, rc(c), g)), pl.BlockSpec((1, Q, N), lambda g, b, c: (b, rc(c), g)),
                   pl.BlockSpec((1, Q, N), lambda g, b, c: (b, rc(c), g)), pl.BlockSpec((1, 1, Q, HG), lambda g, b, c: (b, g, rc(c), 0)),
                   pl.BlockSpec((1, 3, HG), lambda g, b, c: (g, 0, 0))),
        scratch_shapes=[pltpu.VMEM((HG, P, N), F32)], name="b_ssd_bwd", sem=("parallel", "arbitrary", "arbitrary"), comm=comm))


GN_W = SSM_D_INNER // SSM_N_GROUPS


def _gate_fwd(y, zx, nw):
    T = y.shape[0]
    tm = _pick(T, 256, 8)

    def body(y_ref, z_ref, w_ref, o_ref):
        z = z_ref[...]
        gt = y_ref[...] * (z * jax.nn.sigmoid(z))
        outs = []
        for k in range(SSM_N_GROUPS):
            gk = gt[:, GN_W * k:GN_W * (k + 1)]
            outs.append(gk * lax.rsqrt(jnp.mean(gk * gk, axis=-1, keepdims=True) + NORM_EPS))
        o_ref[...] = (jnp.concatenate(outs, axis=-1) * w_ref[...]).astype(BF16)

    row = pl.BlockSpec((tm, SSM_D_INNER), lambda i: (i, 0))
    return pl.pallas_call(
        body, out_shape=SDS((T, SSM_D_INNER), BF16), grid=(T // tm,),
        in_specs=[row, row, pl.BlockSpec((1, SSM_D_INNER), lambda i: (0, 0))], out_specs=row, name="b_gate_fwd",
        compiler_params=_params(("parallel",)),
    )(y, zx, nw)


def _gate_bwd(dgn, y, zx, nw):
    T = y.shape[0]
    tm = _pick(T, 256, 8)

    def body(d_ref, y_ref, z_ref, w_ref, dy_ref, dz_ref, dw_ref):
        z, yv, w = z_ref[...], y_ref[...], w_ref[...]
        sg = jax.nn.sigmoid(z)
        sz = z * sg
        gt = yv * sz
        gw = d_ref[...] * w
        dgts, dws = [], []
        for k in range(SSM_N_GROUPS):
            sl = slice(GN_W * k, GN_W * (k + 1))
            gk, gwk = gt[:, sl], gw[:, sl]
            rstd = lax.rsqrt(jnp.mean(gk * gk, axis=-1, keepdims=True) + NORM_EPS)
            dgts.append(rstd * gwk - gk * (rstd * rstd * rstd) * jnp.mean(gwk * gk, axis=-1, keepdims=True))
            dws.append(jnp.sum(d_ref[:, sl] * gk * rstd, axis=0, keepdims=True))
        dgt = jnp.concatenate(dgts, axis=-1)
        dy_ref[...] = dgt * sz
        dz_ref[...] = (dgt * yv * (sg * (1.0 + z * (1.0 - sg)))).astype(BF16)
        dw = jnp.concatenate(dws, axis=-1)

        @pl.when(pl.program_id(0) == 0)
        def _():
            dw_ref[...] = dw

        @pl.when(pl.program_id(0) > 0)
        def _():
            dw_ref[...] += dw

    row = pl.BlockSpec((tm, SSM_D_INNER), lambda i: (i, 0))
    vec = pl.BlockSpec((1, SSM_D_INNER), lambda i: (0, 0))
    return pl.pallas_call(
        body, out_shape=(SDS((T, SSM_D_INNER), F32), SDS((T, SSM_D_INNER), BF16), SDS((1, SSM_D_INNER), F32)), grid=(T // tm,),
        in_specs=[row, row, row, vec], out_specs=(row, row, vec), name="b_gate_bwd", compiler_params=_params(("arbitrary",)),
    )(dgn, y, zx, nw)


N_CHIPS = 4


def _dev_block(ref, kind, j, size):
    if kind == "slot":
        return ref.at[j]
    start = pl.multiple_of(j * size, size)
    nd = len(ref.shape)
    if kind == "col":
        return ref.at[(slice(None),) * (nd - 1) + (pl.ds(start, size),)]
    return ref.at[(slice(None),) * (nd - 2) + (pl.ds(start, size), slice(None))]


def _dma_sems(n, k):
    return [pltpu.SemaphoreType.DMA((n, k)), pltpu.SemaphoreType.DMA((n, k)), pltpu.SemaphoreType.DMA((n, k))]


def _place(shard, layer, kind, full_shape, dev, name):
    k, n = shard.shape[1:]
    tr = _pick(k, 512, 16)
    nb = k // tr

    def body(dev_ref, s_ref, o_ref):
        if kind == "slot":
            o_ref[0] = s_ref[0].astype(BF16)
        else:
            o_ref[...] = s_ref[0].astype(BF16)

    out_spec = {"slot": pl.BlockSpec((1, tr, n), lambda i, d: (d[0], i, 0)),
                "row": pl.BlockSpec((tr, n), lambda i, d: (d[0] * nb + i, 0)),
                "col": pl.BlockSpec((tr, n), lambda i, d: (i, d[0]))}[kind]
    return pl.pallas_call(
        body, out_shape=SDS(full_shape, BF16),
        grid_spec=pltpu.PrefetchScalarGridSpec(
            num_scalar_prefetch=1, grid=(nb,), in_specs=[pl.BlockSpec((1, tr, n), lambda i, d: (layer, i, 0))], out_specs=out_spec),
        name=name, compiler_params=_params(("arbitrary",)),
    )(dev, shard)


def _run_comm(comm, name):
    c_in = len(comm.inputs)

    def body(*refs):
        cins, couts, sems = refs[:c_in], refs[c_in:c_in + len(comm.out_shapes)], refs[c_in + len(comm.out_shapes):]
        for _, fn in comm.phases:
            fn(cins, couts, sems)

    return pl.pallas_call(
        body, out_shape=list(comm.out_shapes), in_specs=[ANY] * c_in, out_specs=[ANY] * len(comm.out_shapes),
        input_output_aliases=dict(comm.aliases), scratch_shapes=list(comm.sems), name=name,
    )(*comm.inputs)


def _gather_comm(items, mid=0.7):
    n = len(items)

    def tools(srcs, dsts, sems):
        send_sems, recv_sems, local_sems = sems
        px, py, pc = lax.axis_index("x"), lax.axis_index("y"), lax.axis_index("c")
        me, sibling = (px, py, pc), (px, py, 1 - pc)
        chips = [(1 - px, py), (px, 1 - py), (1 - px, 1 - py)]

        def blk(a, dev):
            return _dev_block(dsts[a], items[a][1], 4 * dev[0] + 2 * dev[1] + dev[2], items[a][2])

        def copy(a, k, block, to, src=None):
            return pltpu.make_async_remote_copy(
                src_ref=blk(a, block) if src is None else src, dst_ref=blk(a, block),
                send_sem=send_sems.at[a, k], recv_sem=recv_sems.at[a, k], device_id=to, device_id_type=MESH)

        def mine():
            return [pltpu.make_async_copy(srcs[a], blk(a, me), local_sems.at[a, 0]) for a in range(n) if not items[a][4]]

        def first():
            out = []
            for a in range(n):
                src = blk(a, me) if items[a][4] else srcs[a]
                out.append(copy(a, 0, me, sibling, src=src))
                out += [copy(a, 1 + j, me, (*chip, pc), src=src) for j, chip in enumerate(chips)]
            return out

        def passed():
            return [copy(a, 4 + j, (*chip, pc), sibling) for j, chip in enumerate(chips) for a in range(n)]

        return me, sibling, chips, pc, copy, mine, first, passed

    def start(srcs, dsts, sems):
        *_, mine, first, _ = tools(srcs, dsts, sems)
        for cp in mine() + first():
            cp.start()

    def forward(srcs, dsts, sems):
        me, _, chips, pc, copy, _, _, passed = tools(srcs, dsts, sems)
        fwd = passed()
        for j, chip in enumerate(chips):
            for a in range(n):
                copy(a, 1 + j, (*chip, pc), me).wait_recv()
                fwd[j * n + a].start()

    def finish(srcs, dsts, sems):
        me, sibling, chips, pc, copy, mine, first, passed = tools(srcs, dsts, sems)
        for a in range(n):
            copy(a, 0, sibling, me).wait_recv()
            for j, chip in enumerate(chips):
                copy(a, 4 + j, (*chip, 1 - pc), me).wait_recv()
        for cp in first() + passed():
            cp.wait_send()
        for cp in mine():
            cp.wait()

    return _Comm([it[0] for it in items], [SDS(it[3], it[0].dtype) for it in items],
                 {a: a for a in range(n) if items[a][4]}, _dma_sems(n, 7), [(0.0, start), (mid, forward), (1.0, finish)])


def _gather(items, name):
    return _run_comm(_gather_comm(items), name)


def _reduce_d2d(items, name):
    n = len(items)

    def body(*refs):
        gs, gots = refs[:n], refs[n:2 * n]
        send_sems, recv_sems, _ = refs[2 * n:]
        px, py, pc = lax.axis_index("x"), lax.axis_index("y"), lax.axis_index("c")
        copies = []
        for a in range(n):
            _, kind, size, _ = items[a]
            for q in range(N_CHIPS):
                copies.append(pltpu.make_async_remote_copy(
                    src_ref=_dev_block(gs[a], kind, 2 * q + 1 - pc, size), dst_ref=gots[a].at[q], send_sem=send_sems.at[a, q],
                    recv_sem=recv_sems.at[a, q], device_id=(px, py, 1 - pc), device_id_type=MESH))
        for cp in copies:
            cp.start()
        for cp in copies:
            cp.wait()

    return pl.pallas_call(
        body, out_shape=[SDS((N_CHIPS,) + tuple(it[3]), F32) for it in items], in_specs=[ANY] * n, out_specs=[ANY] * n,
        scratch_shapes=_dma_sems(n, N_CHIPS), name=name,
    )(*[it[0] for it in items])


def _pair_sum(g, got, kind, core, name):
    _, k, n = got.shape
    tr = _pick(k, max(16, STREAM_VMEM // (2 * n * 10)), 16)
    nb = k // tr

    def body(c_ref, g_ref, s_ref, o_ref):
        mine = g_ref[0] if kind == "slot" else g_ref[...]
        o_ref[0] = (mine + s_ref[0]).astype(BF16)

    g_spec = {"slot": pl.BlockSpec((1, tr, n), lambda q, i, c: (2 * q + c[0], i, 0)),
              "row": pl.BlockSpec((tr, n), lambda q, i, c: ((2 * q + c[0]) * nb + i, 0)),
              "col": pl.BlockSpec((tr, n), lambda q, i, c: (i, 2 * q + c[0]))}[kind]
    part = pl.BlockSpec((1, tr, n), lambda q, i, c: (q, i, 0))
    return pl.pallas_call(
        body, out_shape=SDS((N_CHIPS, k, n), BF16),
        grid_spec=pltpu.PrefetchScalarGridSpec(num_scalar_prefetch=1, grid=(N_CHIPS, nb), in_specs=[g_spec, part], out_specs=part),
        name=name, compiler_params=_params(("arbitrary", "arbitrary")),
    )(core, g, got)


def _reduce_ici_comm(parts):
    n = len(parts)

    def copies(ps, rs, sems, arriving):
        send_sems, recv_sems, _ = sems
        px, py, pc = lax.axis_index("x"), lax.axis_index("y"), lax.axis_index("c")
        my_chip = 2 * px + py
        out = []
        for a in range(n):
            for k in range(1, N_CHIPS):
                qx, qy = px ^ (k >> 1), py ^ (k & 1)
                q = 2 * qx + qy
                out.append(pltpu.make_async_remote_copy(
                    src_ref=ps[a].at[q], dst_ref=rs[a].at[q] if arriving else rs[a].at[my_chip], send_sem=send_sems.at[a, k - 1],
                    recv_sem=recv_sems.at[a, k - 1], device_id=(qx, qy, pc), device_id_type=MESH))
        return out

    def start(ps, rs, sems):
        for cp in copies(ps, rs, sems, False):
            cp.start()

    def finish(ps, rs, sems):
        for cp in copies(ps, rs, sems, True):
            cp.wait_recv()
        for cp in copies(ps, rs, sems, False):
            cp.wait_send()

    return _Comm(list(parts), [SDS(p.shape, p.dtype) for p in parts], {}, _dma_sems(n, N_CHIPS - 1),
                 [(0.0, start), (1.0, finish)])


def _adam_update(g, w, m, v):
    c1 = 1.0 - ADAM_B1 ** ADAM_STEP
    c2 = 1.0 - ADAM_B2 ** ADAM_STEP
    nm = ADAM_B1 * m + (1.0 - ADAM_B1) * g
    nv = ADAM_B2 * v + (1.0 - ADAM_B2) * (g * g)
    delta = -ADAM_LR * ((nm / c1) / (jnp.sqrt(nv / c2) + ADAM_EPS) + ADAM_WD * w)
    return delta, nm, nv


def _adamw(parts, recv, w, m, v, layer, prev, chip, name):
    _, R, C = w.shape
    row_bytes = 2 * C * (N_CHIPS * 2 + 7 * 4)
    tr = _pick(R, max(16, STREAM_VMEM // row_bytes), 16)
    n_prev = 0 if prev is None else 4

    def body(ch_ref, own_ref, r1_ref, r2_ref, r3_ref, w_ref, m_ref, v_ref, *rest):
        g_ref, d_ref, nm_ref, nv_ref = rest[n_prev:]
        g = own_ref[0].astype(F32)
        for r_ref in (r1_ref, r2_ref, r3_ref):
            g = g + r_ref[0].astype(F32)
        g_ref[0] = g
        d_ref[0], nm_ref[0], nv_ref[0] = _adam_update(g, w_ref[0], m_ref[0], v_ref[0])

    lay = pl.BlockSpec((1, tr, C), lambda i, ch: (layer, i, 0))
    other = lambda k: pl.BlockSpec((1, tr, C), lambda i, ch: (ch[0] ^ k, i, 0))
    out = SDS(w.shape, F32)
    return pl.pallas_call(
        body, out_shape=(out, out, out, out),
        grid_spec=pltpu.PrefetchScalarGridSpec(
            num_scalar_prefetch=1, grid=(R // tr,),
            in_specs=[pl.BlockSpec((1, tr, C), lambda i, ch: (ch[0], i, 0)), other(2), other(1), other(3), lay, lay, lay]
            + [ANY] * n_prev,
            out_specs=(lay, lay, lay, lay)),
        input_output_aliases={8 + k: k for k in range(n_prev)},
        name=name, compiler_params=_params(("arbitrary",)),
    )(chip, parts, recv, recv, recv, w, m, v, *(prev or ()))


def _small_adamw(gathered, ws, ms, vs):
    n = len(ws)

    def body(*refs):
        g_in, w_in, m_in, v_in = refs[:n], refs[n:2 * n], refs[2 * n:3 * n], refs[3 * n:4 * n]
        outs = refs[4 * n:]
        for i in range(n):
            g = g_in[i][0]
            for dev in range(1, N_DEV):
                g = g + g_in[i][dev]
            d, nm, nv = _adam_update(g, w_in[i][...], m_in[i][...], v_in[i][...])
            outs[i][...] = g
            outs[n + i][...] = d
            outs[2 * n + i][...] = nm
            outs[3 * n + i][...] = nv

    shapes = [SDS(w.shape, F32) for w in ws]
    outs = pl.pallas_call(body, out_shape=shapes * 4, name="small_adamw")(*gathered, *ws, *ms, *vs)
    return outs[:n], outs[n:2 * n], outs[2 * n:3 * n], outs[3 * n:]


W_NAMES = ("norm_mix_w", "norm_mlp_w", "a_w_qkv", "a_b_qkv", "a_sinks", "a_w_o", "a_b_o", "b_in_w", "b_conv_w", "b_conv_b",
           "b_dt_bias", "b_a_log", "b_d", "b_norm_w", "b_out_w", "c_w_qkv", "c_w_o", "mlp_w_up", "mlp_w_down", "final_norm_w")
BIG_KIND = {"a_w_qkv": "slot", "a_w_o": "row", "b_in_w": "slot", "b_out_w": "row", "c_w_qkv": "col", "c_w_o": "row",
            "mlp_w_up": "col", "mlp_w_down": "row"}
SMALL_SHARDED = {"a_b_qkv": 1, "a_b_o": 1, "b_conv_w": 2}
SMALL_REPLICATED = ("norm_mix_w", "norm_mlp_w", "a_sinks", "b_conv_b", "b_dt_bias", "b_a_log", "b_d", "b_norm_w", "final_norm_w")


def _layer_big(i):
    kind, j = i % 3, i // 3
    mix = {0: [("a_w_qkv", j), ("a_w_o", j)], 1: [("b_in_w", 0), ("b_out_w", 0)], 2: [("c_w_qkv", 0), ("c_w_o", 0)]}[kind]
    return mix + [("mlp_w_up", i), ("mlp_w_down", i)]


def _block_size(kind, shard2d):
    return {"slot": None, "row": shard2d[0], "col": shard2d[1]}[kind]


def _full2d(kind, shard2d):
    k, n = shard2d
    return {"slot": (N_DEV, k, n), "row": (N_DEV * k, n), "col": (k, N_DEV * n)}[kind]


def _from_slots(t, ax):
    s = t.shape[1:]
    return jnp.moveaxis(t, 0, ax).reshape(s[:ax] + (N_DEV * s[ax],) + s[ax + 1:])


def _to_slots(g, ax):
    s = g.shape
    return jnp.moveaxis(g.reshape(s[:ax] + (N_DEV, s[ax] // N_DEV) + s[ax + 1:]), ax, 0)


def _rope_tables(positions):
    half = HEAD_DIM // 2
    inv = ROPE_THETA ** (-(jnp.arange(LANES, dtype=jnp.int32) % half).astype(F32) / half)
    ang = positions.astype(F32).reshape(-1, 1) * inv
    return jnp.cos(ang), jnp.sin(ang)


def _swa_fwd(u, h, p, j, B, S, cos, sin, tag, plan=None):
    qkv = _matmul(u, p["a_w_qkv"][j], out_dtype=BF16, bias=p["a_b_qkv"][j][None], rope=(cos, sin),
                  rope_cols=A_Q_DIM + A_KV_DIM, name=f"{tag}_qkv")
    o, lse = _attn_fwd(qkv, B, S, 1, n_heads=A_N_HEADS, n_kv=A_N_KV, q_col=0, k_col=A_Q_DIM, v_col=A_Q_DIM + A_KV_DIM,
                       max_dist=A_WINDOW - 1, sinks=p["a_sinks"][j], name=f"{tag}_attn", plan=plan)
    h1 = _matmul(o, p["a_w_o"][j], bias=p["a_b_o"][j][None], resid=h, name=f"{tag}_o")
    return h1, (qkv, o, lse)


def _swa_bwd(dh1, u, saved, p, j, B, S, cos, sin, tag, plan=None):
    qkv, o, lse = saved
    kw = dict(n_heads=A_N_HEADS, n_kv=A_N_KV, q_col=0, k_col=A_Q_DIM, v_col=A_Q_DIM + A_KV_DIM, max_dist=A_WINDOW - 1)
    g = {}
    do = _matmul(dh1, p["a_w_o"][j], tb=True, name=f"{tag}_do")
    g["a_w_o"] = _matmul(o, dh1, ta=True, name=f"{tag}_dwo")
    g["a_b_o"] = _colsum(dh1, f"{tag}_dbo")[0]
    sk = jnp.pad(p["a_sinks"][j], (0, LANES - A_N_HEADS))[None]
    delta, dob, dsink = _delta(do, o, lse, sk, name=f"{tag}_delta")
    g["a_sinks"] = dsink[0, :A_N_HEADS]
    dq = _attn_dq(qkv, dob, lse, delta, cos, sin, B, S, 1, name=f"{tag}_dq", plan=plan, **kw)
    dk, dv = _attn_dkv(qkv, dob, lse, delta, cos, sin, B, S, 1, name=f"{tag}_dkv", plan=plan, **kw)
    dqkv = jnp.concatenate([dq, dk, dv], axis=1)
    g["a_w_qkv"] = _matmul(u, dqkv, ta=True, name=f"{tag}_dwqkv")
    g["a_b_qkv"] = _colsum(dqkv, f"{tag}_dbqkv")[0]
    du = _matmul(dqkv, p["a_w_qkv"][j], tb=True, name=f"{tag}_du")
    return du, g


def _group_cols(gi, qkv):
    W = C_HEADS * HEAD_DIM
    if C_PATTERNS[gi][1] == 1:
        return qkv, (gi * W, (3 + gi) * W, (6 + gi) * W)
    part = jnp.concatenate([qkv[:, (3 * j + gi) * W:(3 * j + gi + 1) * W] for j in range(3)], axis=1)
    return part, (0, W, 2 * W)


def _dil_fwd(u, h, p, B, S, cos, sin, plan=None):
    W = C_HEADS * HEAD_DIM
    qkv = _matmul(u, p["c_w_qkv"][0], out_dtype=BF16, rope=(cos, sin), rope_cols=6 * W, name="c_qkv", plan=plan)
    os_, lses, parts = [], [], []
    for gi, (window, dil) in enumerate(C_PATTERNS):
        part, (qc, kc, vc) = _group_cols(gi, qkv)
        o, lse = _attn_fwd(part, B, S, dil, n_heads=C_HEADS, n_kv=C_HEADS, q_col=qc, k_col=kc, v_col=vc,
                           max_dist=window // dil, sinks=None, name=f"c_attn{gi}")
        os_.append(o)
        lses.append(lse)
        parts.append((part, (qc, kc, vc)))
    o, lse = _merge(os_, lses)
    h1 = _matmul(o, p["c_w_o"][0], resid=h, name="c_o")
    return h1, (parts, o, lse)


def _dil_bwd(dh1, u, saved, p, B, S, cos, sin, plan=None):
    parts, o, lse = saved
    g = {}
    do = _matmul(dh1, p["c_w_o"][0], tb=True, name="c_do")
    g["c_w_o"] = _matmul(o, dh1, ta=True, name="c_dwo")[None]
    delta, dob = _delta(do, o, name="c_delta")
    dqs, dks, dvs = [], [], []
    for gi, (window, dil) in enumerate(C_PATTERNS):
        part, (qc, kc, vc) = parts[gi]
        kw = dict(n_heads=C_HEADS, n_kv=C_HEADS, q_col=qc, k_col=kc, v_col=vc, max_dist=window // dil)
        dqs.append(_attn_dq(part, dob, lse, delta, cos, sin, B, S, dil, name=f"c_dq{gi}", **kw))
        dk, dv = _attn_dkv(part, dob, lse, delta, cos, sin, B, S, dil, name=f"c_dkv{gi}", **kw)
        dks.append(dk)
        dvs.append(dv)
    dqkv = jnp.concatenate(dqs + dks + dvs, axis=1)
    g["c_w_qkv"] = _matmul(u, dqkv, ta=True, name="c_dwqkv", plan=plan)[None]
    du = _matmul(dqkv, p["c_w_qkv"][0], tb=True, name="c_du")
    return du, g


def _ssm_params(p):
    par = jnp.stack([p["b_dt_bias"][0], p["b_a_log"][0], p["b_d"][0]], axis=0)
    prow = par.reshape(3, SSM_N_GROUPS, SSM_HG).transpose(1, 0, 2)
    return prow, prow.transpose(0, 2, 1)


def _mamba_fwd(u, h, p, B, S, plan=None):
    T = B * S
    G, HG = SSM_N_GROUPS, SSM_HG
    w_in = p["b_in_w"][0]
    nzx = SSM_D_INNER + SSM_CONV_DIM
    w_dt = jnp.pad(w_in[:, nzx:], ((0, 0), (0, LANES - SSM_N_HEADS)))
    zx = _matmul(u, w_in[:, :nzx], name="b_zx")
    dtraw = _matmul(u, w_dt, name="b_dt")[:, :SSM_N_HEADS]
    dtc = dtraw.reshape(B, S, G, HG).transpose(0, 2, 1, 3)
    dtr = dtraw.reshape(B, S, G, HG).transpose(0, 2, 3, 1)
    prow, pcol = _ssm_params(p)
    zx3 = zx.reshape(B, S, nzx)
    xc3 = _conv_fwd(zx3, p["b_conv_w"][0], p["b_conv_b"])
    y3, states = _ssd_fwd(xc3, dtc, dtr, prow, pcol, plan=plan)
    y = y3.reshape(T, SSM_D_INNER)
    gn = _gate_fwd(y, zx, p["b_norm_w"])
    h1 = _matmul(gn, p["b_out_w"][0], resid=h, name="b_out")
    return h1, (zx, dtc, dtr, xc3, y, states, gn, w_dt)


def _mamba_bwd(dh1, u, saved, p, B, S, plan=None):
    T = B * S
    zx, dtc, dtr, xc3, y, states, gn, w_dt = saved
    nzx = SSM_D_INNER + SSM_CONV_DIM
    w_in = p["b_in_w"][0]
    prow, pcol = _ssm_params(p)
    g = {}
    dgn = _matmul(dh1, p["b_out_w"][0], tb=True, name="b_dgn")
    g["b_out_w"] = _matmul(gn, dh1, ta=True, name="b_dwout")[None]
    dy, dz, dnw = _gate_bwd(dgn, y, zx, p["b_norm_w"])
    g["b_norm_w"] = dnw
    dx3, dB3, dC3, ddt, dpar = _ssd_bwd(xc3, dtc, dtr, prow, pcol, states, dy.reshape(B, S, SSM_D_INNER), plan=plan)
    dpar = dpar.transpose(1, 0, 2).reshape(3, SSM_N_HEADS)
    g["b_dt_bias"], g["b_a_log"], g["b_d"] = dpar[0:1], dpar[1:2], dpar[2:3]
    zx3 = zx.reshape(B, S, nzx)
    cw, cb = p["b_conv_w"][0], p["b_conv_b"]
    parts, dws, dbs = [], [], []
    for col0, dpart, nm in ((0, dx3, "b_conv_bwd_x"), (SSM_D_INNER, dB3, "b_conv_bwd_b"),
                            (SSM_D_INNER + SSM_BC_DIM, dC3, "b_conv_bwd_c")):
        dxp, dw, db = _conv_bwd(zx3, dpart, cw, cb, col0, nm)
        parts.append(dxp.reshape(T, -1))
        dws.append(dw)
        dbs.append(db)
    g["b_conv_w"] = jnp.concatenate(dws, axis=1)[None]
    g["b_conv_b"] = jnp.concatenate(dbs, axis=1)
    dzx = jnp.concatenate([dz] + parts, axis=1)
    ddtraw = ddt.transpose(0, 2, 1, 3).reshape(T, SSM_N_HEADS)
    ddtp = jnp.pad(ddtraw, ((0, 0), (0, LANES - SSM_N_HEADS)))
    dw_zx = _matmul(u, dzx, ta=True, name="b_dwzx")
    dw_dt = _matmul(u, ddtp, ta=True, name="b_dwdt")[:, :SSM_N_HEADS]
    g["b_in_w"] = jnp.concatenate([dw_zx, dw_dt], axis=1)[None]
    du = _matmul(dzx, w_in[:, :nzx], tb=True, name="b_du_zx")
    du = _matmul(ddtp, w_dt, tb=True, resid=du, name="b_du_dt")
    return du, g


def _local_step(x, positions, p, target, plan=None):
    B, S, D = x.shape
    T = B * S
    cos, sin = _rope_tables(positions)
    h = x.reshape(T, D)
    tape = []
    for i in range(DEPTH):
        kind, j = i % 3, i // 3
        u = _rmsnorm_fwd(h, p["norm_mix_w"][i], f"l{i}_norm_mix")
        if kind == 0:
            h1, saved = _swa_fwd(u, h, p, j, B, S, cos, sin, f"a{j}", plan)
        elif kind == 1:
            h1, saved = _mamba_fwd(u, h, p, B, S, plan)
        else:
            h1, saved = _dil_fwd(u, h, p, B, S, cos, sin, plan)
        u2 = _rmsnorm_fwd(h1, p["norm_mlp_w"][i], f"l{i}_norm_mlp")
        r, s = _matmul(u2, p["mlp_w_up"][i], out_dtype=BF16, relu2=True, name=f"l{i}_up", plan=plan)
        h2 = _matmul(s, p["mlp_w_down"][i], resid=h1, name=f"l{i}_down", plan=plan)
        tape.append((h, u, saved, h1, u2, r, s))
        h = h2
    dh, dwf, loss = _final_loss(h, target.reshape(T, D), p["final_norm_w"])
    grads = {"final_norm_w": dwf[0]}
    per_layer = {n: [None] * DEPTH for n in ("norm_mix_w", "norm_mlp_w", "mlp_w_up", "mlp_w_down")}
    a_grads = [None, None]
    for i in reversed(range(DEPTH)):
        kind, j = i % 3, i // 3
        h0, u, saved, h1, u2, r, s = tape[i]
        da = _matmul(dh, p["mlp_w_down"][i], tb=True, out_dtype=BF16, mul=r, mul_scale=2.0, name=f"l{i}_da")
        per_layer["mlp_w_down"][i] = _matmul(s, dh, ta=True, name=f"l{i}_dwdown")
        per_layer["mlp_w_up"][i] = _matmul(u2, da, ta=True, name=f"l{i}_dwup")
        du2 = _matmul(da, p["mlp_w_up"][i], tb=True, name=f"l{i}_du2")
        dh1, dnw = _rmsnorm_bwd(h1, du2, p["norm_mlp_w"][i], dh, f"l{i}_norm_mlp_bwd")
        per_layer["norm_mlp_w"][i] = dnw[0]
        if kind == 0:
            du, g = _swa_bwd(dh1, u, saved, p, j, B, S, cos, sin, f"a{j}", plan)
            a_grads[j] = g
            big = {"a_w_qkv": g["a_w_qkv"], "a_w_o": g["a_w_o"]}
        elif kind == 1:
            du, g = _mamba_bwd(dh1, u, saved, p, B, S, plan)
            grads.update(g)
            big = {"b_in_w": g["b_in_w"][0], "b_out_w": g["b_out_w"][0]}
        else:
            du, g = _dil_bwd(dh1, u, saved, p, B, S, cos, sin, plan)
            grads.update(g)
            big = {"c_w_qkv": g["c_w_qkv"][0], "c_w_o": g["c_w_o"][0]}
        dh, dnw = _rmsnorm_bwd(h0, du, p["norm_mix_w"][i], dh1, f"l{i}_norm_mix_bwd")
        per_layer["norm_mix_w"][i] = dnw[0]
        if plan is not None:
            plan.layer_grads(i, dict(big, mlp_w_up=per_layer["mlp_w_up"][i], mlp_w_down=per_layer["mlp_w_down"][i]))
    for n in ("norm_mix_w", "norm_mlp_w"):
        grads[n] = jnp.stack(per_layer[n], axis=0)
    for n in ("mlp_w_up", "mlp_w_down"):
        grads[n] = per_layer[n]
    for n in ("a_b_qkv", "a_sinks", "a_b_o"):
        grads[n] = jnp.stack([a_grads[0][n], a_grads[1][n]], axis=0)
    for n in ("a_w_qkv", "a_w_o"):
        grads[n] = [a_grads[0][n], a_grads[1][n]]
    for n in ("b_in_w", "b_out_w", "c_w_qkv", "c_w_o"):
        grads[n] = [grads[n][0]]
    return loss, dh.reshape(B, S, D), grads


GATHER_HOSTS = {"a0_attn": (1, ("mlp_w_up",)), "l0_up": (1, ("b_in_w",)), "l0_down": (1, ("b_out_w", "mlp_w_down")),
                "b_ssd_fwd": (2, None), "c_qkv": (3, None)}
REDUCE_HOSTS = {3: (("c_dwqkv", None),), 2: (("b_ssd_bwd", None),),
                1: (("a0_dq", ("b_in_w", "b_out_w")), ("a0_dkv", ("mlp_w_up", "mlp_w_down"))), 0: ()}


class _Plan:
    def __init__(self, w, m, v, p, dev, chip, core):
        self.w, self.m, self.v, self.p, self.dev, self.chip, self.core = w, m, v, p, dev, chip, core
        self.pending = {}
        self.res = {n: None for n in BIG_KIND}
        self._install(0, None)(_gather(self._gather_items(0, None), "gather_l0"))
        for host, (i, only) in GATHER_HOSTS.items():
            self.pending[host] = (_gather_comm(self._gather_items(i, only)), self._install(i, only))

    def _names(self, i, only):
        return [(n, l) for n, l in _layer_big(i) if only is None or n in only]

    def _gather_items(self, i, only):
        items = []
        for n, l in self._names(i, only):
            kind, s2 = BIG_KIND[n], self.w[n].shape[1:]
            placed = _place(self.w[n], l, kind, _full2d(kind, s2), self.dev, f"place_l{i}_{n}")
            items.append((placed, kind, _block_size(kind, s2), _full2d(kind, s2), True))
        return items

    def _install(self, i, only):
        def done(fulls):
            for (n, l), t in zip(self._names(i, only), fulls):
                self.p[n][l] = _from_slots(t, 1) if BIG_KIND[n] == "slot" else t
        return done

    def take(self, host):
        return self.pending[host][0] if host in self.pending else None

    def give(self, host, results):
        self.pending.pop(host)[1](results)

    def layer_grads(self, i, grads):
        names = self._names(i, None)
        items = []
        for n, _ in names:
            kind, s2 = BIG_KIND[n], self.w[n].shape[1:]
            items.append((_to_slots(grads[n], 1) if kind == "slot" else grads[n], kind, _block_size(kind, s2), s2))
        sib = _reduce_d2d(items, f"reduce_d2d_l{i}")
        parts = {n: _pair_sum(it[0], s, it[1], self.core, f"pair_sum_l{i}_{n}") for (n, _), it, s in zip(names, items, sib)}

        def update(sel):
            def done(recv):
                for (n, l), r in zip(sel, recv):
                    self.res[n] = _adamw(parts[n], r, self.w[n], self.m[n], self.v[n], l, self.res[n], self.chip,
                                         f"adamw_l{i}_{n}")
            return done

        if not REDUCE_HOSTS[i]:
            update(names)(_run_comm(_reduce_ici_comm([parts[n] for n, _ in names]), f"reduce_ici_l{i}"))
        for host, only in REDUCE_HOSTS[i]:
            sel = self._names(i, only)
            self.pending[host] = (_reduce_ici_comm([parts[n] for n, _ in sel]), update(sel))

    def flush(self):
        for host in list(self.pending):
            comm, done = self.pending.pop(host)
            done(_run_comm(comm, f"comm_{host}"))


def kernel(x, positions, norm_mix_w, norm_mlp_w, a_w_qkv, a_b_qkv, a_sinks, a_w_o, a_b_o, b_in_w, b_conv_w, b_conv_b, b_dt_bias, b_a_log, b_d, b_norm_w, b_out_w, c_w_qkv, c_w_o, mlp_w_up, mlp_w_down, final_norm_w, loss_target, m_norm_mix_w, m_norm_mlp_w, m_a_w_qkv, m_a_b_qkv, m_a_sinks, m_a_w_o, m_a_b_o, m_b_in_w, m_b_conv_w, m_b_conv_b, m_b_dt_bias, m_b_a_log, m_b_d, m_b_norm_w, m_b_out_w, m_c_w_qkv, m_c_w_o, m_mlp_w_up, m_mlp_w_down, m_final_norm_w, v_norm_mix_w, v_norm_mlp_w, v_a_w_qkv, v_a_b_qkv, v_a_sinks, v_a_w_o, v_a_b_o, v_b_in_w, v_b_conv_w, v_b_conv_b, v_b_dt_bias, v_b_a_log, v_b_d, v_b_norm_w, v_b_out_w, v_c_w_qkv, v_c_w_o, v_mlp_w_up, v_mlp_w_down, v_final_norm_w):
    w = dict(zip(W_NAMES, (norm_mix_w, norm_mlp_w, a_w_qkv, a_b_qkv, a_sinks, a_w_o, a_b_o, b_in_w, b_conv_w, b_conv_b,
                           b_dt_bias, b_a_log, b_d, b_norm_w, b_out_w, c_w_qkv, c_w_o, mlp_w_up, mlp_w_down, final_norm_w)))
    m = dict(zip(W_NAMES, (m_norm_mix_w, m_norm_mlp_w, m_a_w_qkv, m_a_b_qkv, m_a_sinks, m_a_w_o, m_a_b_o, m_b_in_w,
                           m_b_conv_w, m_b_conv_b, m_b_dt_bias, m_b_a_log, m_b_d, m_b_norm_w, m_b_out_w, m_c_w_qkv, m_c_w_o,
                           m_mlp_w_up, m_mlp_w_down, m_final_norm_w)))
    v = dict(zip(W_NAMES, (v_norm_mix_w, v_norm_mlp_w, v_a_w_qkv, v_a_b_qkv, v_a_sinks, v_a_w_o, v_a_b_o, v_b_in_w,
                           v_b_conv_w, v_b_conv_b, v_b_dt_bias, v_b_a_log, v_b_d, v_b_norm_w, v_b_out_w, v_c_w_qkv, v_c_w_o,
                           v_mlp_w_up, v_mlp_w_down, v_final_norm_w)))
    px, py, pc = lax.axis_index("x"), lax.axis_index("y"), lax.axis_index("c")
    me = 4 * px + 2 * py + pc
    dev, chip, core = (t.astype(jnp.int32).reshape(1) for t in (me, 2 * px + py, pc))

    trio = tuple(SMALL_SHARDED)
    got = _gather([(d[n], "slot", None, (N_DEV,) + d[n].shape, False) for n in trio for d in (w, m, v)], "gather_small")
    slots = {n: got[3 * i:3 * i + 3] for i, n in enumerate(trio)}
    p = {n: w[n] for n in SMALL_REPLICATED}
    for n in trio:
        p[n] = _from_slots(slots[n][0], SMALL_SHARDED[n])
    for n in BIG_KIND:
        p[n] = [None] * w[n].shape[0]
    plan = _Plan(w, m, v, p, dev, chip, core)
    loss_part, dx, grads = _local_step(x, positions, p, loss_target, plan)
    loss = lax.psum(loss_part[0, 0], AXES)
    plan.flush()
    out = {n: list(plan.res[n]) for n in BIG_KIND}

    small = SMALL_REPLICATED + trio
    as2d = lambda t: t.reshape(1, -1) if t.ndim == 1 else t
    g_sm = [as2d(grads[n]) for n in SMALL_REPLICATED] + [_to_slots(grads[n].reshape(p[n].shape), SMALL_SHARDED[n]) for n in trio]
    gathered = _gather([(g, "slot", None, (N_DEV,) + g.shape, False) for g in g_sm], "gather_small_grads")
    ws = [as2d(w[n]) for n in SMALL_REPLICATED] + [slots[n][0] for n in trio]
    ms = [as2d(m[n]) for n in SMALL_REPLICATED] + [slots[n][1] for n in trio]
    vs = [as2d(v[n]) for n in SMALL_REPLICATED] + [slots[n][2] for n in trio]
    sm_out = _small_adamw(gathered, ws, ms, vs)
    for i, n in enumerate(small):
        if n in SMALL_SHARDED:
            out[n] = [lax.dynamic_index_in_dim(sm_out[k][i], me, 0, keepdims=False) for k in range(4)]
        else:
            out[n] = [sm_out[k][i].reshape(w[n].shape) for k in range(4)]
    return (loss, dx, *[out[n][0] for n in W_NAMES], *[out[n][1] for n in W_NAMES], *[out[n][2] for n in W_NAMES],
            *[out[n][3] for n in W_NAMES])
```

```python
import functools
import math

import jax
import jax.numpy as jnp
import numpy as np
from jax import lax
from jax.experimental import pallas as pl
from jax.experimental.pallas import tpu as pltpu

F32 = jnp.float32
BF16 = jnp.bfloat16
SDS = jax.ShapeDtypeStruct

D_MODEL = 1024
DEPTH = 4
BLOCK = 128
ROPE_THETA = 10000.0
NORM_EPS = 1e-5
HEAD_DIM = 64
A_N_HEADS = 16
A_N_KV = 2
A_WINDOW = 128
A_Q_DIM = 1024
A_KV_DIM = 128
SSM_D_INNER = 2048
SSM_N_HEADS = 32
SSM_N_GROUPS = 8
SSM_HG = 4
SSM_D_STATE = 128
SSM_CONV = 4
SSM_CHUNK = 128
SSM_BC_DIM = 1024
SSM_CONV_DIM = 4096
C_PATTERNS = ((128, 1), (512, 4), (2048, 16))
C_HEADS = 16
ADAM_LR, ADAM_B1, ADAM_B2, ADAM_EPS, ADAM_WD, ADAM_STEP = 0.001, 0.9, 0.999, 1e-08, 0.01, 10

N_DEV = 8
AXES = ("x", "y", "c")
LANES = 128
VMEM_LIMIT = 56 * 1024 * 1024
STREAM_VMEM = 16 * 1024 * 1024
NEG = -1e30

NN = (((1,), (0,)), ((), ()))
NT = (((1,), (1,)), ((), ()))
TN = (((0,), (0,)), ((), ()))
HI = lax.Precision.HIGHEST


def _pick(n, cap, mult=LANES):
    best = None
    for t in range(mult, min(n, cap) + 1, mult):
        if n % t == 0:
            best = t
    return best if best is not None else n


def _params(sem):
    return pltpu.CompilerParams(dimension_semantics=sem, vmem_limit_bytes=VMEM_LIMIT)


def _bf(x):
    return x if x.dtype == BF16 else x.astype(BF16)


def _rot_half(y):
    n = y.shape[-1]
    lane = lax.broadcasted_iota(jnp.int32, y.shape, y.ndim - 1)
    return jnp.where((lane % HEAD_DIM) < HEAD_DIM // 2, -pltpu.roll(y, n - 32, y.ndim - 1), pltpu.roll(y, 32, y.ndim - 1))


def _rope(y, cos, sin, sign):
    reps = y.shape[-1] // LANES
    c = jnp.tile(cos, (1, reps)) if reps > 1 else cos
    s = jnp.tile(sin, (1, reps)) if reps > 1 else sin
    return y * c + sign * (_rot_half(y) * s)


MESH = pl.DeviceIdType.MESH
ANY = pl.BlockSpec(memory_space=pl.ANY)


class _Comm:
    def __init__(self, inputs, out_shapes, aliases, sems, phases):
        self.inputs, self.out_shapes, self.aliases, self.sems, self.phases = inputs, out_shapes, aliases, sems, phases

    @staticmethod
    def merge(comms):
        if len(comms) == 1:
            return comms[0]
        ins, outs, aliases, sems, spans = [], [], {}, [], []
        for c in comms:
            aliases.update({len(ins) + i: len(outs) + j for i, j in c.aliases.items()})
            spans.append((len(ins), len(ins) + len(c.inputs), len(outs), len(outs) + len(c.out_shapes), len(sems),
                          len(sems) + len(c.sems)))
            ins, outs, sems = ins + list(c.inputs), outs + list(c.out_shapes), sems + list(c.sems)
        phases = []
        for f in sorted({f for c in comms for f, _ in c.phases}):
            todo = [(fn, sp) for c, sp in zip(comms, spans) for g, fn in c.phases if g == f]

            def run(cins, couts, csems, todo=todo):
                for fn, (i0, i1, o0, o1, s0, s1) in todo:
                    fn(cins[i0:i1], couts[o0:o1], csems[s0:s1])
            phases.append((f, run))
        return _Comm(ins, outs, aliases, sems, phases)


def _pc(body, args, *, out_shape, grid, in_specs, out_specs, name, sem, scratch_shapes=(), comm=None):
    single = not isinstance(out_shape, (tuple, list))
    outs, ospecs = ([out_shape], [out_specs]) if single else (list(out_shape), list(out_specs))
    unpack = (lambda r: r[0]) if single else (lambda r: tuple(r))
    if comm is None:
        res = pl.pallas_call(body, out_shape=outs, grid=grid, in_specs=list(in_specs), out_specs=ospecs,
                             scratch_shapes=list(scratch_shapes), name=name, compiler_params=_params(sem))(*args)
        return unpack(res)
    n_in, n_out, n_scr = len(in_specs), len(outs), len(scratch_shapes)
    c_in, c_out = len(comm.inputs), len(comm.out_shapes)
    total = math.prod(grid)
    steps = [min(total - 1, int(f * total)) for f, _ in comm.phases[:-1]]

    def wrapped(*refs):
        ins, cins = refs[:n_in], refs[n_in:n_in + c_in]
        o = refs[n_in + c_in:n_in + c_in + n_out]
        couts = refs[n_in + c_in + n_out:n_in + c_in + n_out + c_out]
        rest = refs[n_in + c_in + n_out + c_out:]
        scr, csems = rest[:n_scr], rest[n_scr:]
        step = pl.program_id(0)
        for ax in range(1, len(grid)):
            step = step * grid[ax] + pl.program_id(ax)
        for (_, fn), st in zip(comm.phases[:-1], steps):
            @pl.when(step == st)
            def _(fn=fn):
                fn(cins, couts, csems)
        body(*ins, *o, *scr)

        @pl.when(step == total - 1)
        def _():
            comm.phases[-1][1](cins, couts, csems)

    res = pl.pallas_call(
        wrapped, out_shape=outs + list(comm.out_shapes), grid=grid, in_specs=list(in_specs) + [ANY] * c_in,
        out_specs=ospecs + [ANY] * c_out, scratch_shapes=list(scratch_shapes) + list(comm.sems),
        input_output_aliases={n_in + i: n_out + j for i, j in comm.aliases.items()}, name=name,
        compiler_params=_params(("arbitrary",) * len(grid)),
    )(*args, *comm.inputs)
    return unpack(res[:n_out]), list(res[n_out:])


def _hosted(plan, name, run):
    comm = plan.take(name) if plan is not None else None
    if comm is None:
        return run(None)
    res, extra = run(comm)
    plan.give(name, extra)
    return res


MM_VMEM = 40 * 1024 * 1024
HBM_BYTES_PER_US = 2.5e6
STEP_US = 0.35


def _divisors(n, cands):
    return [c for c in cands if c <= n and n % c == 0] or [n]


def _mm_tiles(M, N, K, sa, sb, out_bytes, extra_bytes):
    best = None
    for tm in _divisors(M, (2048, 1024, 512, 256)):
        for tn in _divisors(N, (1024, 640, 512, 256, 128)):
            for tk in _divisors(K, (K, 2048, 1024, 640, 512)):
                nk = K // tk
                vmem = 2 * tm * tk * sa + 2 * tk * tn * sb + tm * tn * (2 * (out_bytes + extra_bytes) + 8 + (4 if nk > 1 else 0))
                if vmem > MM_VMEM:
                    continue
                a_traffic = M * K * sa * (1 if nk == 1 else N // tn)
                b_traffic = K * N * sb * (1 if (nk == 1 and N == tn) else M // tm)
                steps = (M // tm) * (N // tn) * nk
                cost = (a_traffic + b_traffic + M * N * (out_bytes + extra_bytes)) / HBM_BYTES_PER_US + steps * STEP_US
                cost += (M // tm) * (N // tn) * (nk - 1) * tm * tn * 8 / (4 * HBM_BYTES_PER_US)
                if best is None or cost < best[0]:
                    best = (cost, tm, tn, tk)
    assert best is not None, (M, N, K)
    return best[1:]


def _matmul(a, b, *, ta=False, tb=False, out_dtype=F32, bias=None, resid=None, mul=None, mul_scale=1.0,
            relu2=False, rope=None, rope_cols=0, name="mm", plan=None):
    M = a.shape[1] if ta else a.shape[0]
    K = a.shape[0] if ta else a.shape[1]
    N = b.shape[0] if tb else b.shape[1]
    assert (b.shape[1] if tb else b.shape[0]) == K
    out_bytes = jnp.dtype(out_dtype).itemsize * (2 if relu2 else 1)
    extra_bytes = (4 if resid is not None else 0) + (mul.dtype.itemsize if mul is not None else 0)
    tm, tn, tk = _mm_tiles(M, N, K, a.dtype.itemsize, b.dtype.itemsize, out_bytes, extra_bytes)
    nk = K // tk
    dims = (((0 if ta else 1,), (1 if tb else 0,)), ((), ()))

    def body(*refs):
        it = iter(refs)
        a_ref, b_ref = next(it), next(it)
        bias_ref = next(it) if bias is not None else None
        resid_ref = next(it) if resid is not None else None
        mul_ref = next(it) if mul is not None else None
        cos_ref, sin_ref = (next(it), next(it)) if rope is not None else (None, None)
        o_ref = next(it)
        o2_ref = next(it) if relu2 else None
        acc_ref = next(it) if nk > 1 else None
        k = pl.program_id(2)
        part = lax.dot_general(_bf(a_ref[...]), _bf(b_ref[...]), dims, preferred_element_type=F32)
        if nk > 1:
            @pl.when(k == 0)
            def _():
                acc_ref[...] = part

            @pl.when(k > 0)
            def _():
                acc_ref[...] += part

        @pl.when(k == nk - 1)
        def _():
            y = acc_ref[...] if nk > 1 else part
            if bias_ref is not None:
                y = y + bias_ref[...]
            if rope is not None:
                col = pl.program_id(1) * tn + lax.broadcasted_iota(jnp.int32, y.shape, 1)
                y = jnp.where(col < rope_cols, _rope(y, cos_ref[...], sin_ref[...], 1.0), y)
            if mul_ref is not None:
                y = y * (mul_ref[...].astype(F32) * mul_scale)
            if resid_ref is not None:
                y = y + resid_ref[...]
            if relu2:
                r = jnp.maximum(y, 0.0)
                o_ref[...] = r.astype(o_ref.dtype)
                o2_ref[...] = (r * r).astype(o2_ref.dtype)
            else:
                o_ref[...] = y.astype(o_ref.dtype)

    a_spec = pl.BlockSpec((tk, tm), lambda i, j, k: (k, i)) if ta else pl.BlockSpec((tm, tk), lambda i, j, k: (i, k))
    b_spec = pl.BlockSpec((tn, tk), lambda i, j, k: (j, k)) if tb else pl.BlockSpec((tk, tn), lambda i, j, k: (k, j))
    mn_spec = pl.BlockSpec((tm, tn), lambda i, j, k: (i, j))
    in_specs, args = [a_spec, b_spec], [a, b]
    if bias is not None:
        in_specs.append(pl.BlockSpec((1, tn), lambda i, j, k: (0, j)))
        args.append(bias)
    if resid is not None:
        in_specs.append(mn_spec)
        args.append(resid)
    if mul is not None:
        in_specs.append(mn_spec)
        args.append(mul)
    if rope is not None:
        in_specs += [pl.BlockSpec((tm, LANES), lambda i, j, k: (i, 0))] * 2
        args += [rope[0], rope[1]]
    out_shape = SDS((M, N), out_dtype)
    out_specs = mn_spec
    if relu2:
        out_shape, out_specs = (out_shape, out_shape), (mn_spec, mn_spec)
    return _hosted(plan, name, lambda comm: _pc(
        body, args, out_shape=out_shape, grid=(M // tm, N // tn, nk), in_specs=in_specs, out_specs=out_specs,
        scratch_shapes=[pltpu.VMEM((tm, tn), F32)] if nk > 1 else [], name=name, sem=("parallel", "parallel", "arbitrary"),
        comm=comm))


def _colsum(x, name):
    T, N = x.shape
    tm = _pick(T, 1024, 8)

    def body(x_ref, o_ref):
        s = jnp.sum(x_ref[...].astype(F32), axis=0, keepdims=True)

        @pl.when(pl.program_id(0) == 0)
        def _():
            o_ref[...] = s

        @pl.when(pl.program_id(0) > 0)
        def _():
            o_ref[...] += s

    return pl.pallas_call(
        body, out_shape=SDS((1, N), F32), grid=(T // tm,),
        in_specs=[pl.BlockSpec((tm, N), lambda i: (i, 0))], out_specs=pl.BlockSpec((1, N), lambda i: (0, 0)),
        name=name, compiler_params=_params(("arbitrary",)),
    )(x)


def _rmsnorm_fwd(h, w, name):
    T, D = h.shape
    tm = _pick(T, 512, 8)

    def body(h_ref, w_ref, o_ref):
        x = h_ref[...]
        rstd = lax.rsqrt(jnp.mean(x * x, axis=-1, keepdims=True) + NORM_EPS)
        o_ref[...] = (x * rstd * w_ref[...]).astype(BF16)

    return pl.pallas_call(
        body, out_shape=SDS((T, D), BF16), grid=(T // tm,),
        in_specs=[pl.BlockSpec((tm, D), lambda i: (i, 0)), pl.BlockSpec((1, D), lambda i: (0, 0))],
        out_specs=pl.BlockSpec((tm, D), lambda i: (i, 0)), name=name, compiler_params=_params(("parallel",)),
    )(h, w.reshape(1, D))


def _rmsnorm_bwd(h, du, w, dres, name):
    T, D = h.shape
    tm = _pick(T, 512, 8)

    def body(h_ref, du_ref, w_ref, dres_ref, dh_ref, dw_ref):
        x = h_ref[...]
        du_ = du_ref[...].astype(F32)
        rstd = lax.rsqrt(jnp.mean(x * x, axis=-1, keepdims=True) + NORM_EPS)
        g = du_ * w_ref[...]
        dh_ref[...] = dres_ref[...] + rstd * g - x * (rstd * rstd * rstd) * jnp.mean(g * x, axis=-1, keepdims=True)
        dw = jnp.sum(du_ * x * rstd, axis=0, keepdims=True)

        @pl.when(pl.program_id(0) == 0)
        def _():
            dw_ref[...] = dw

        @pl.when(pl.program_id(0) > 0)
        def _():
            dw_ref[...] += dw

    row = pl.BlockSpec((tm, D), lambda i: (i, 0))
    vec = pl.BlockSpec((1, D), lambda i: (0, 0))
    return pl.pallas_call(
        body, out_shape=(SDS((T, D), F32), SDS((1, D), F32)), grid=(T // tm,),
        in_specs=[row, row, vec, row], out_specs=(row, vec), name=name, compiler_params=_params(("arbitrary",)),
    )(h, du, w.reshape(1, D), dres)


def _final_loss(h, target, w):
    T, D = h.shape
    tm = _pick(T, 512, 8)

    def body(h_ref, t_ref, w_ref, dh_ref, dw_ref, loss_ref):
        x = h_ref[...]
        rstd = lax.rsqrt(jnp.mean(x * x, axis=-1, keepdims=True) + NORM_EPS)
        xn = x * rstd
        err = xn * w_ref[...] - t_ref[...]
        part = 0.5 * jnp.sum(jnp.mean(err * err, axis=-1, keepdims=True), axis=0, keepdims=True)
        dy = err * (1.0 / D)
        g = dy * w_ref[...]
        dh_ref[...] = rstd * g - x * (rstd * rstd * rstd) * jnp.mean(g * x, axis=-1, keepdims=True)
        dw = jnp.sum(dy * xn, axis=0, keepdims=True)
        lp = jnp.broadcast_to(part, (1, LANES))

        @pl.when(pl.program_id(0) == 0)
        def _():
            dw_ref[...] = dw
            loss_ref[...] = lp

        @pl.when(pl.program_id(0) > 0)
        def _():
            dw_ref[...] += dw
            loss_ref[...] += lp

    row = pl.BlockSpec((tm, D), lambda i: (i, 0))
    vec = pl.BlockSpec((1, D), lambda i: (0, 0))
    return pl.pallas_call(
        body, out_shape=(SDS((T, D), F32), SDS((1, D), F32), SDS((1, LANES), F32)), grid=(T // tm,),
        in_specs=[row, row, vec], out_specs=(row, vec, pl.BlockSpec((1, LANES), lambda i: (0, 0))),
        name="final_loss", compiler_params=_params(("arbitrary",)),
    )(h, target, w.reshape(1, D))


def _band_mask(i_blk, max_dist, first_ok):
    qi = lax.broadcasted_iota(jnp.int32, (BLOCK, 2 * BLOCK), 0)
    kj = lax.broadcasted_iota(jnp.int32, (BLOCK, 2 * BLOCK), 1)
    dist = qi + BLOCK - kj
    ok = (dist >= 0) & (dist <= max_dist)
    return ok & ((kj >= BLOCK) | first_ok)


def _pair(t, i):
    return t[:, LANES * i:LANES * (i + 1)]


def _low_half(shape):
    return lax.broadcasted_iota(jnp.int32, shape, len(shape) - 1) < HEAD_DIM


def _stack_heads(t):
    lo = _low_half(t.shape)
    z = jnp.zeros_like(t)
    return jnp.concatenate([jnp.where(lo, t, z), jnp.where(lo, z, t)], axis=0)


def _swap_halves(t):
    return jnp.concatenate([t[:, HEAD_DIM:], t[:, :HEAD_DIM]], axis=1)


def _kv_operand(kv, kv_swapped, h0, n_kv, n_heads):
    R = n_heads // n_kv
    if R == 1:
        return _pair(kv, h0 // 2)
    assert kv.shape[1] == LANES and R % 2 == 0, "grouped queries: one 128-lane tile of kv heads, both heads of a pair in one group"
    g = h0 // R
    t, ts = _pair(kv, g // 2), _pair(kv_swapped, g // 2)
    lo = _low_half(t.shape)
    return jnp.where(lo, t, ts) if g % 2 == 0 else jnp.where(lo, ts, t)


def _lane_place(cols):
    m = cols[0].shape[0]
    lane = lax.broadcasted_iota(jnp.int32, (m, LANES), 1)
    out = jnp.zeros((m, LANES), F32)
    for h, c in enumerate(cols):
        out = jnp.where(lane == h, c, out)
    return out


def _attn_specs(B, S, d, C, n_heads, n_kv, q_col, k_col, v_col):
    kvw = n_kv * HEAD_DIM
    qw = n_heads * HEAD_DIM
    cq, ck = (C // qw if d > 1 else 0), (C // kvw if d > 1 else 0)
    q_spec = pl.BlockSpec((1, BLOCK, qw), lambda b, r, i: (b, i, r * cq + q_col // qw))
    kc = pl.BlockSpec((1, BLOCK, kvw), lambda b, r, i: (b, i, r * ck + k_col // kvw))
    kp = pl.BlockSpec((1, BLOCK, kvw), lambda b, r, i: (b, jnp.maximum(i - 1, 0), r * ck + k_col // kvw))
    vc = pl.BlockSpec((1, BLOCK, kvw), lambda b, r, i: (b, i, r * ck + v_col // kvw))
    vp = pl.BlockSpec((1, BLOCK, kvw), lambda b, r, i: (b, jnp.maximum(i - 1, 0), r * ck + v_col // kvw))
    return q_spec, kp, kc, vp, vc


def _attn_fwd(qkv, B, S, d, *, n_heads, n_kv, q_col, k_col, v_col, max_dist, sinks, name, plan=None):
    C = qkv.shape[1]
    Ls = S // d
    nb = Ls // BLOCK
    qw = n_heads * HEAD_DIM
    R = n_heads // n_kv
    qkv3 = qkv.reshape(B, Ls, d * C)
    scale = HEAD_DIM ** -0.5

    def body(*refs):
        if sinks is not None:
            sink_ref, q_ref, kp_ref, kc_ref, vp_ref, vc_ref, o_ref, lse_ref = refs
        else:
            q_ref, kp_ref, kc_ref, vp_ref, vc_ref, o_ref, lse_ref = refs
        i = pl.program_id(2)
        mask1 = _band_mask(i, max_dist, i > 0)
        mask = jnp.concatenate([mask1, mask1], axis=0)
        q = q_ref[0]
        kk = jnp.concatenate([kp_ref[0], kc_ref[0]], axis=0)
        vv = jnp.concatenate([vp_ref[0], vc_ref[0]], axis=0)
        kks, vvs = (_swap_halves(kk), _swap_halves(vv)) if R > 1 else (None, None)
        lo = _low_half((BLOCK, LANES))
        top = lax.broadcasted_iota(jnp.int32, (2 * BLOCK, 1), 0) < BLOCK
        lses, tiles = [], []
        for t in range(n_heads // 2):
            k2 = _kv_operand(kk, kks, 2 * t, n_kv, n_heads)
            v2 = _kv_operand(vv, vvs, 2 * t, n_kv, n_heads)
            s = lax.dot_general(_stack_heads(_pair(q, t)), k2, NT, preferred_element_type=F32) * scale
            s = jnp.where(mask, s, NEG)
            m = jnp.max(s, axis=-1, keepdims=True)
            if sinks is not None:
                sk = jnp.where(top, sink_ref[2 * t], sink_ref[2 * t + 1])
                m = jnp.maximum(m, sk)
            p = jnp.exp(s - m)
            den = jnp.sum(p, axis=-1, keepdims=True)
            if sinks is not None:
                den = den + jnp.exp(sk - m)
            lse2 = m + jnp.log(den)
            o2 = jnp.dot((p / den).astype(BF16), v2, preferred_element_type=F32)
            tiles.append(jnp.where(lo, o2[:BLOCK], o2[BLOCK:]))
            lses += [lse2[:BLOCK], lse2[BLOCK:]]
        o_ref[0] = jnp.concatenate(tiles, axis=-1)
        lse_ref[0] = _lane_place(lses)

    specs = list(_attn_specs(B, S, d, C, n_heads, n_kv, q_col, k_col, v_col))
    args = [qkv3] * 5
    if sinks is not None:
        specs = [pl.BlockSpec(memory_space=pltpu.SMEM)] + specs
        args = [sinks] + args
    o3, lse3 = _hosted(plan, name, lambda comm: _pc(
        body, args, out_shape=(SDS((B, Ls, d * qw), F32), SDS((B, Ls, d * LANES), F32)), grid=(B, d, nb), in_specs=specs,
        out_specs=(pl.BlockSpec((1, BLOCK, qw), lambda b, r, i: (b, i, r)), pl.BlockSpec((1, BLOCK, LANES), lambda b, r, i: (b, i, r))),
        name=name, sem=("parallel", "parallel", "parallel"), comm=comm))
    return o3.reshape(B * S, qw), lse3.reshape(B * S, LANES)


def _attn_dq(qkv, do, lse, delta, cos, sin, B, S, d, *, n_heads, n_kv, q_col, k_col, v_col, max_dist, name, plan=None):
    C = qkv.shape[1]
    Ls = S // d
    nb = Ls // BLOCK
    qw = n_heads * HEAD_DIM
    R = n_heads // n_kv
    scale = HEAD_DIM ** -0.5

    def body(q_ref, kp_ref, kc_ref, vp_ref, vc_ref, do_ref, lse_ref, dl_ref, cos_ref, sin_ref, dq_ref):
        i = pl.program_id(2)
        mask1 = _band_mask(i, max_dist, i > 0)
        mask = jnp.concatenate([mask1, mask1], axis=0)
        q = q_ref[0]
        do_ = do_ref[0]
        kk = jnp.concatenate([kp_ref[0], kc_ref[0]], axis=0)
        vv = jnp.concatenate([vp_ref[0], vc_ref[0]], axis=0)
        kks, vvs = (_swap_halves(kk), _swap_halves(vv)) if R > 1 else (None, None)
        lo = _low_half((BLOCK, LANES))
        lse_t, dl_t = lse_ref[0], dl_ref[0]
        tiles = []
        for t in range(n_heads // 2):
            k2 = _kv_operand(kk, kks, 2 * t, n_kv, n_heads)
            v2 = _kv_operand(vv, vvs, 2 * t, n_kv, n_heads)
            lse2 = jnp.concatenate([lse_t[:, 2 * t:2 * t + 1], lse_t[:, 2 * t + 1:2 * t + 2]], axis=0)
            dl2 = jnp.concatenate([dl_t[:, 2 * t:2 * t + 1], dl_t[:, 2 * t + 1:2 * t + 2]], axis=0)
            s = lax.dot_general(_stack_heads(_pair(q, t)), k2, NT, preferred_element_type=F32) * scale
            p = jnp.where(mask, jnp.exp(s - lse2), 0.0)
            dp = lax.dot_general(_stack_heads(_pair(do_, t)), v2, NT, preferred_element_type=F32)
            ds = p * (dp - dl2)
            dq2 = jnp.dot(ds.astype(BF16), k2, preferred_element_type=F32) * scale
            tiles.append(jnp.where(lo, dq2[:BLOCK], dq2[BLOCK:]))
        dq = jnp.concatenate(tiles, axis=-1)
        dq_ref[0] = _rope(dq, cos_ref[0], sin_ref[0], -1.0).astype(BF16)

    qs, kp, kc, vp, vc = _attn_specs(B, S, d, C, n_heads, n_kv, q_col, k_col, v_col)
    row_q = pl.BlockSpec((1, BLOCK, qw), lambda b, r, i: (b, i, r))
    row_l = pl.BlockSpec((1, BLOCK, LANES), lambda b, r, i: (b, i, r))
    qkv3 = qkv.reshape(B, Ls, d * C)
    v3 = lambda t, w: t.reshape(B, Ls, d * w)
    args = (qkv3, qkv3, qkv3, qkv3, qkv3, v3(do, qw), v3(lse, LANES), v3(delta, LANES), v3(cos, LANES), v3(sin, LANES))
    dq3 = _hosted(plan, name, lambda comm: _pc(
        body, args, out_shape=SDS((B, Ls, d * qw), BF16), grid=(B, d, nb),
        in_specs=[qs, kp, kc, vp, vc, row_q, row_l, row_l, row_l, row_l], out_specs=row_q,
        name=name, sem=("parallel", "parallel", "parallel"), comm=comm))
    return dq3.reshape(B * S, qw)


def _attn_dkv(qkv, do, lse, delta, cos, sin, B, S, d, *, n_heads, n_kv, q_col, k_col, v_col, max_dist, name, plan=None):
    C = qkv.shape[1]
    Ls = S // d
    nb = Ls // BLOCK
    qw = n_heads * HEAD_DIM
    kvw = n_kv * HEAD_DIM
    R = n_heads // n_kv
    scale = HEAD_DIM ** -0.5
    cq, ck = (C // qw if d > 1 else 0), (C // kvw if d > 1 else 0)

    def body(k_ref, v_ref, q0_ref, q1_ref, do0_ref, do1_ref, lse0_ref, lse1_ref, dl0_ref, dl1_ref, cos_ref, sin_ref,
             dk_ref, dv_ref):
        j = pl.program_id(2)
        kj = lax.broadcasted_iota(jnp.int32, (BLOCK, BLOCK), 0)
        qi = lax.broadcasted_iota(jnp.int32, (BLOCK, BLOCK), 1)
        dist0 = qi - kj
        dist1 = qi + BLOCK - kj
        mask0 = (dist0 >= 0) & (dist0 <= max_dist)
        mask1 = (dist1 <= max_dist) & (j + 1 < nb)
        kb, vb = k_ref[0], v_ref[0]
        kbs, vbs = (_swap_halves(kb), _swap_halves(vb)) if R > 1 else (None, None)
        sides = ((q0_ref[0], do0_ref[0], lse0_ref[0].T, dl0_ref[0].T, mask0), (q1_ref[0], do1_ref[0], lse1_ref[0].T, dl1_ref[0].T, mask1))
        n_acc = n_kv if R > 1 else n_kv // 2
        dks = [jnp.zeros((BLOCK, LANES), F32) for _ in range(n_acc)]
        dvs = [jnp.zeros((BLOCK, LANES), F32) for _ in range(n_acc)]
        for t in range(n_heads // 2):
            k2 = _kv_operand(kb, kbs, 2 * t, n_kv, n_heads)
            v2 = _kv_operand(vb, vbs, 2 * t, n_kv, n_heads)
            a = (2 * t) // R if R > 1 else t
            for (q, do_, lse_r, dl_r, mask) in sides:
                q2, do2 = _stack_heads(_pair(q, t)), _stack_heads(_pair(do_, t))
                s = lax.dot_general(k2, q2, NT, preferred_element_type=F32) * scale
                dp = lax.dot_general(v2, do2, NT, preferred_element_type=F32)
                ps, dss = [], []
                for half in (0, 1):
                    h = 2 * t + half
                    sl = slice(BLOCK * half, BLOCK * (half + 1))
                    p = jnp.where(mask, jnp.exp(s[:, sl] - lse_r[h:h + 1, :]), 0.0)
                    ps.append(p)
                    dss.append(p * (dp[:, sl] - dl_r[h:h + 1, :]))
                dvs[a] = dvs[a] + jnp.dot(jnp.concatenate(ps, axis=1).astype(BF16), do2, preferred_element_type=F32)
                dks[a] = dks[a] + jnp.dot(jnp.concatenate(dss, axis=1).astype(BF16), q2, preferred_element_type=F32)
        if R > 1:
            lo = _low_half((BLOCK, LANES))
            fold = lambda x: x + pltpu.roll(x, HEAD_DIM, 1)
            dks = [jnp.where(lo, fold(dks[2 * t]), fold(dks[2 * t + 1])) for t in range(n_kv // 2)]
            dvs = [jnp.where(lo, fold(dvs[2 * t]), fold(dvs[2 * t + 1])) for t in range(n_kv // 2)]
        dk_t = jnp.concatenate(dks, axis=-1) * scale
        dk_ref[0] = _rope(dk_t, cos_ref[0], sin_ref[0], -1.0).astype(BF16)
        dv_ref[0] = jnp.concatenate(dvs, axis=-1).astype(BF16)

    nxt = lambda j: jnp.minimum(j + 1, nb - 1)
    k_spec = pl.BlockSpec((1, BLOCK, kvw), lambda b, r, j: (b, j, r * ck + k_col // kvw))
    v_spec = pl.BlockSpec((1, BLOCK, kvw), lambda b, r, j: (b, j, r * ck + v_col // kvw))
    q0 = pl.BlockSpec((1, BLOCK, qw), lambda b, r, j: (b, j, r * cq + q_col // qw))
    q1 = pl.BlockSpec((1, BLOCK, qw), lambda b, r, j: (b, nxt(j), r * cq + q_col // qw))
    w0 = lambda w: pl.BlockSpec((1, BLOCK, w), lambda b, r, j: (b, j, r))
    w1 = lambda w: pl.BlockSpec((1, BLOCK, w), lambda b, r, j: (b, nxt(j), r))
    qkv3 = qkv.reshape(B, Ls, d * C)
    v3 = lambda t, w: t.reshape(B, Ls, d * w)
    do3, lse3, dl3 = v3(do, qw), v3(lse, LANES), v3(delta, LANES)
    args = (qkv3, qkv3, qkv3, qkv3, do3, do3, lse3, lse3, dl3, dl3, v3(cos, LANES), v3(sin, LANES))
    dk3, dv3 = _hosted(plan, name, lambda comm: _pc(
        body, args, out_shape=(SDS((B, Ls, d * kvw), BF16), SDS((B, Ls, d * kvw), BF16)), grid=(B, d, nb),
        in_specs=[k_spec, v_spec, q0, q1, w0(qw), w1(qw), w0(LANES), w1(LANES), w0(LANES), w1(LANES), w0(LANES), w0(LANES)],
        out_specs=(w0(kvw), w0(kvw)), name=name, sem=("parallel", "parallel", "parallel"), comm=comm))
    return dk3.reshape(B * S, kvw), dv3.reshape(B * S, kvw)


def _head_expand():
    r = lax.broadcasted_iota(jnp.int32, (LANES, C_HEADS * HEAD_DIM), 0)
    c = lax.broadcasted_iota(jnp.int32, (LANES, C_HEADS * HEAD_DIM), 1)
    return jnp.where(c // HEAD_DIM == r, 1.0, 0.0).astype(F32)


def _delta(do, o, lse=None, sinks_row=None, name="delta"):
    T, W = do.shape
    tm = _pick(T, 512, 8)
    with_sink = sinks_row is not None

    def body(*refs):
        if with_sink:
            do_ref, o_ref, lse_ref, sk_ref, dl_ref, dob_ref, ds_ref = refs
        else:
            do_ref, o_ref, dl_ref, dob_ref = refs
        do_ = do_ref[...]
        dl = lax.dot_general(do_ * o_ref[...], _head_expand(), NT, preferred_element_type=F32, precision=HI)
        dl_ref[...] = dl
        dob_ref[...] = do_.astype(BF16)
        if with_sink:
            lane = lax.broadcasted_iota(jnp.int32, dl.shape, 1)
            contrib = jnp.where(lane < A_N_HEADS, -jnp.exp(sk_ref[...] - lse_ref[...]) * dl, 0.0)
            part = jnp.sum(contrib, axis=0, keepdims=True)

            @pl.when(pl.program_id(0) == 0)
            def _():
                ds_ref[...] = part

            @pl.when(pl.program_id(0) > 0)
            def _():
                ds_ref[...] += part

    row_w = pl.BlockSpec((tm, W), lambda i: (i, 0))
    row_l = pl.BlockSpec((tm, LANES), lambda i: (i, 0))
    vec_l = pl.BlockSpec((1, LANES), lambda i: (0, 0))
    if with_sink:
        return pl.pallas_call(
            body, out_shape=(SDS((T, LANES), F32), SDS((T, W), BF16), SDS((1, LANES), F32)), grid=(T // tm,),
            in_specs=[row_w, row_w, row_l, vec_l], out_specs=(row_l, row_w, vec_l), name=name,
            compiler_params=_params(("arbitrary",)),
        )(do, o, lse, sinks_row)
    return pl.pallas_call(
        body, out_shape=(SDS((T, LANES), F32), SDS((T, W), BF16)), grid=(T // tm,),
        in_specs=[row_w, row_w], out_specs=(row_l, row_w), name=name, compiler_params=_params(("parallel",)),
    )(do, o)


def _merge(os_, lses):
    T, W = os_[0].shape
    tm = _pick(T, 512, 8)

    def body(o0, o1, o2, l0, l1, l2, o_ref, lse_ref):
        ls = [l0[...], l1[...], l2[...]]
        m = jnp.maximum(jnp.maximum(ls[0], ls[1]), ls[2])
        ws = [jnp.exp(l - m) for l in ls]
        tot = ws[0] + ws[1] + ws[2]
        lse_ref[...] = m + jnp.log(tot)
        e = _head_expand()
        acc = jnp.zeros((tm, W), F32)
        for w, o in zip(ws, (o0, o1, o2)):
            acc = acc + jnp.dot(w / tot, e, preferred_element_type=F32, precision=HI) * o[...]
        o_ref[...] = acc

    row_w = pl.BlockSpec((tm, W), lambda i: (i, 0))
    row_l = pl.BlockSpec((tm, LANES), lambda i: (i, 0))
    return pl.pallas_call(
        body, out_shape=(SDS((T, W), F32), SDS((T, LANES), F32)), grid=(T // tm,),
        in_specs=[row_w] * 3 + [row_l] * 3, out_specs=(row_w, row_l), name="c_merge", compiler_params=_params(("parallel",)),
    )(*os_, *lses)


CONV_TC = 256


def _conv_pre(x, w, bias):
    row = lax.broadcasted_iota(jnp.int32, x.shape, 0)
    acc = x * w[SSM_CONV - 1:SSM_CONV, :] + bias
    for k in range(1, SSM_CONV):
        acc = acc + jnp.where(row >= k, pltpu.roll(x, k, 0), 0.0) * w[SSM_CONV - 1 - k:SSM_CONV - k, :]
    return acc


def _conv_fwd(zx3, w, bias):
    B, S, _ = zx3.shape
    off = SSM_D_INNER // CONV_TC

    def body(x_ref, w_ref, b_ref, o_ref):
        v = _conv_pre(x_ref[0], w_ref[...], b_ref[...])
        o_ref[0] = v * jax.nn.sigmoid(v)

    return pl.pallas_call(
        body, out_shape=SDS((B, S, SSM_CONV_DIM), F32), grid=(B, SSM_CONV_DIM // CONV_TC),
        in_specs=[pl.BlockSpec((1, S, CONV_TC), lambda b, j: (b, 0, j + off)),
                  pl.BlockSpec((SSM_CONV, CONV_TC), lambda b, j: (0, j)), pl.BlockSpec((1, CONV_TC), lambda b, j: (0, j))],
        out_specs=pl.BlockSpec((1, S, CONV_TC), lambda b, j: (b, 0, j)), name="b_conv_fwd",
        compiler_params=_params(("parallel", "parallel")),
    )(zx3, w, bias)


def _conv_bwd(zx3, dxc, w, bias, col0, name):
    B, S, n = dxc.shape
    tc = _pick(n, CONV_TC)
    off_x = (SSM_D_INNER + col0) // tc
    off_w = col0 // tc

    def body(x_ref, d_ref, w_ref, b_ref, dx_ref, dw_ref, db_ref):
        x = x_ref[0]
        wv = w_ref[...]
        v = _conv_pre(x, wv, b_ref[...])
        sg = jax.nn.sigmoid(v)
        dc = d_ref[0] * (sg * (1.0 + v * (1.0 - sg)))
        row = lax.broadcasted_iota(jnp.int32, x.shape, 0)
        dx = dc * wv[SSM_CONV - 1:SSM_CONV, :]
        dws = [jnp.sum(dc * x, axis=0, keepdims=True)]
        for k in range(1, SSM_CONV):
            dx = dx + jnp.where(row < S - k, pltpu.roll(dc, S - k, 0), 0.0) * wv[SSM_CONV - 1 - k:SSM_CONV - k, :]
            dws.append(jnp.sum(dc * jnp.where(row >= k, pltpu.roll(x, k, 0), 0.0), axis=0, keepdims=True))
        dx_ref[0] = dx.astype(BF16)
        ridx = lax.broadcasted_iota(jnp.int32, (SSM_CONV, tc), 0)
        dw = jnp.zeros((SSM_CONV, tc), F32)
        for k in range(SSM_CONV):
            dw = jnp.where(ridx == SSM_CONV - 1 - k, dws[k], dw)
        db = jnp.sum(dc, axis=0, keepdims=True)

        @pl.when(pl.program_id(1) == 0)
        def _():
            dw_ref[...] = dw
            db_ref[...] = db

        @pl.when(pl.program_id(1) > 0)
        def _():
            dw_ref[...] += dw
            db_ref[...] += db

    return pl.pallas_call(
        body, out_shape=(SDS((B, S, n), BF16), SDS((SSM_CONV, n), F32), SDS((1, n), F32)), grid=(n // tc, B),
        in_specs=[pl.BlockSpec((1, S, tc), lambda j, b: (b, 0, j + off_x)), pl.BlockSpec((1, S, tc), lambda j, b: (b, 0, j)),
                  pl.BlockSpec((SSM_CONV, tc), lambda j, b: (0, j + off_w)), pl.BlockSpec((1, tc), lambda j, b: (0, j + off_w))],
        out_specs=(pl.BlockSpec((1, S, tc), lambda j, b: (b, 0, j)), pl.BlockSpec((SSM_CONV, tc), lambda j, b: (0, j)),
                   pl.BlockSpec((1, tc), lambda j, b: (0, j))),
        name=name, compiler_params=_params(("parallel", "arbitrary")),
    )(zx3, dxc, w, bias)


def _ssd_common(x, Bm, Cm, dtc_raw, dtr_raw, pr, pc):
    Q = SSM_CHUNK
    zc = dtc_raw + pr[0:1, :]
    dt_c = jax.nn.softplus(zc)
    dt_r = jax.nn.softplus(dtr_raw + pc[:, 0:1])
    A_r = -jnp.exp(pr[1:2, :])
    A_c = -jnp.exp(pc[:, 1:2])
    row = lax.broadcasted_iota(jnp.int32, (Q, Q), 0)
    col = lax.broadcasted_iota(jnp.int32, (Q, Q), 1)
    tril = jnp.where(row >= col, 1.0, 0.0).astype(F32)
    cs_c = jnp.dot(tril, dt_c * A_r, preferred_element_type=F32, precision=HI)
    cs_r = lax.dot_general(dt_r * A_c, tril, NT, preferred_element_type=F32, precision=HI)
    return zc, dt_c, A_r, cs_c, cs_r, row, col, tril


def _ssd_fwd(xc3, dtc, dtr, prow, pcol, plan=None):
    B, S, _ = xc3.shape
    Q, G, HG, P, N = SSM_CHUNK, SSM_N_GROUPS, SSM_HG, HEAD_DIM, SSM_D_STATE
    nc = S // Q
    xw = HG * P

    def body(x_ref, b_ref, c_ref, dtc_ref, dtr_ref, pr_ref, pc_ref, y_ref, st_ref, state):
        c = pl.program_id(2)

        @pl.when(c == 0)
        def _():
            state[...] = jnp.zeros_like(state)

        x, Bm, Cm = x_ref[0], b_ref[0], c_ref[0]
        pr = pr_ref[0]
        _, dt_c, _, cs_c, cs_r, row, col, _ = _ssd_common(x, Bm, Cm, dtc_ref[0, 0], dtr_ref[0, 0], pr, pc_ref[0])
        Bb, Cb = Bm.astype(BF16), Cm.astype(BF16)
        CB = lax.dot_general(Cb, Bb, NT, preferred_element_type=F32)
        ys = []
        for hg in range(HG):
            xh = x[:, P * hg:P * (hg + 1)]
            xt = xh * dt_c[:, hg:hg + 1]
            csc, csr = cs_c[:, hg:hg + 1], cs_r[hg:hg + 1, :]
            L = jnp.where(row >= col, jnp.exp(jnp.minimum(csc - csr, 0.0)), 0.0)
            ydiag = jnp.dot((CB * L).astype(BF16), xt.astype(BF16), preferred_element_type=F32)
            Sh = state[hg]
            yoff = lax.dot_general(Cb, Sh.astype(BF16), NT, preferred_element_type=F32) * jnp.exp(csc)
            ys.append(ydiag + yoff + pr[2:3, hg:hg + 1] * xh)
            st_ref[0, 0, 0, P * hg:P * (hg + 1), :] = Sh
            csq = csc[Q - 1:Q, :]
            upd = lax.dot_general((xt * jnp.exp(csq - csc)).astype(BF16), Bb, TN, preferred_element_type=F32)
            state[hg] = Sh * jnp.exp(csq) + upd
        y_ref[0] = jnp.concatenate([jnp.concatenate(ys[0:2], axis=-1), jnp.concatenate(ys[2:4], axis=-1)], axis=-1)

    bo, co = SSM_D_INNER // N, (SSM_D_INNER + SSM_BC_DIM) // N
    return _hosted(plan, "b_ssd_fwd", lambda comm: _pc(
        body, (xc3, xc3, xc3, dtc, dtr, prow, pcol),
        out_shape=(SDS((B, S, SSM_D_INNER), F32), SDS((B, G, nc, xw, N), F32)), grid=(G, B, nc),
        in_specs=[pl.BlockSpec((1, Q, xw), lambda g, b, c: (b, c, g)), pl.BlockSpec((1, Q, N), lambda g, b, c: (b, c, bo + g)),
                  pl.BlockSpec((1, Q, N), lambda g, b, c: (b, c, co + g)), pl.BlockSpec((1, 1, Q, HG), lambda g, b, c: (b, g, c, 0)),
                  pl.BlockSpec((1, 1, HG, Q), lambda g, b, c: (b, g, 0, c)), pl.BlockSpec((1, 3, HG), lambda g, b, c: (g, 0, 0)),
                  pl.BlockSpec((1, HG, 3), lambda g, b, c: (g, 0, 0))],
        out_specs=(pl.BlockSpec((1, Q, xw), lambda g, b, c: (b, c, g)), pl.BlockSpec((1, 1, 1, xw, N), lambda g, b, c: (b, g, c, 0, 0))),
        scratch_shapes=[pltpu.VMEM((HG, P, N), F32)], name="b_ssd_fwd", sem=("parallel", "arbitrary", "arbitrary"), comm=comm))


def _ssd_bwd(xc3, dtc, dtr, prow, pcol, states, dy3, plan=None):
    B, S, _ = xc3.shape
    Q, G, HG, P, N = SSM_CHUNK, SSM_N_GROUPS, SSM_HG, HEAD_DIM, SSM_D_STATE
    nc = S // Q
    xw = HG * P

    def body(x_ref, b_ref, c_ref, dtc_ref, dtr_ref, pr_ref, pc_ref, st_ref, dy_ref,
             dx_ref, db_ref, dc_ref, ddt_ref, dpar_ref, dstate):
        bi, ci = pl.program_id(1), pl.program_id(2)

        @pl.when(ci == 0)
        def _():
            dstate[...] = jnp.zeros_like(dstate)

        x, Bm, Cm, dy = x_ref[0], b_ref[0], c_ref[0], dy_ref[0]
        pr = pr_ref[0]
        zc, dt_c, A_r, cs_c, cs_r, row, col, tril = _ssd_common(x, Bm, Cm, dtc_ref[0, 0], dtr_ref[0, 0], pr, pc_ref[0])
        Bb, Cb = Bm.astype(BF16), Cm.astype(BF16)
        CB = lax.dot_general(Cb, Bb, NT, preferred_element_type=F32)
        CBt = lax.dot_general(Bb, Cb, NT, preferred_element_type=F32)
        lane4 = lax.broadcasted_iota(jnp.int32, (Q, HG), 1)
        lane4r = lax.broadcasted_iota(jnp.int32, (1, HG), 1)
        rowq = lax.broadcasted_iota(jnp.int32, (Q, 1), 0)
        dB = jnp.zeros((Q, N), F32)
        dC = jnp.zeros((Q, N), F32)
        dcs4 = jnp.zeros((Q, HG), F32)
        dtx4 = jnp.zeros((Q, HG), F32)
        dD4 = jnp.zeros((1, HG), F32)
        dxts, xhs, dyhs = [], [], []
        for hg in range(HG):
            xh = x[:, P * hg:P * (hg + 1)]
            dyh = dy[:, P * hg:P * (hg + 1)]
            xt = xh * dt_c[:, hg:hg + 1]
            xtb, dyb = xt.astype(BF16), dyh.astype(BF16)
            csc, csr = cs_c[:, hg:hg + 1], cs_r[hg:hg + 1, :]
            L = jnp.where(row >= col, jnp.exp(jnp.minimum(csc - csr, 0.0)), 0.0)
            Lt = jnp.where(col >= row, jnp.exp(jnp.minimum(csr - csc, 0.0)), 0.0)
            M, Mt = CB * L, CBt * Lt
            Sh = st_ref[0, 0, 0, P * hg:P * (hg + 1), :]
            dSh = dstate[hg]
            Shb, dShb = Sh.astype(BF16), dSh.astype(BF16)
            ecs = jnp.exp(csc)
            csq = csc[Q - 1:Q, :]
            dec = jnp.exp(csq - csc)
            dxt = jnp.dot(Mt.astype(BF16), dyb, preferred_element_type=F32)
            dxt = dxt + lax.dot_general(Bb, dShb, NT, preferred_element_type=F32) * dec
            Gm = lax.dot_general(dyb, xtb, NT, preferred_element_type=F32)
            Gt = lax.dot_general(xtb, dyb, NT, preferred_element_type=F32)
            dC = dC + jnp.dot((Gm * L).astype(BF16), Bb, preferred_element_type=F32)
            dB = dB + jnp.dot((Gt * Lt).astype(BF16), Cb, preferred_element_type=F32)
            dC = dC + jnp.dot(dyb, Shb, preferred_element_type=F32) * ecs
            dBst = jnp.dot(xtb, dShb, preferred_element_type=F32) * dec
            dB = dB + dBst
            dcs = jnp.sum(Gm * M, axis=1, keepdims=True) - jnp.sum(Gt * Mt, axis=1, keepdims=True)
            yoff = lax.dot_general(Cb, Shb, NT, preferred_element_type=F32) * ecs
            dcs = dcs + jnp.sum(yoff * dyh, axis=1, keepdims=True)
            r = jnp.sum(dBst * Bm, axis=1, keepdims=True)
            dcs = dcs - r
            extra = jnp.sum(r, axis=0, keepdims=True) + jnp.exp(csq) * jnp.sum(
                jnp.sum(dSh * Sh, axis=1, keepdims=True), axis=0, keepdims=True)
            dcs = dcs + jnp.where(rowq == Q - 1, extra, 0.0)
            dcs4 = jnp.where(lane4 == hg, dcs, dcs4)
            dtx4 = jnp.where(lane4 == hg, jnp.sum(dxt * xh, axis=1, keepdims=True), dtx4)
            dD4 = jnp.where(lane4r == hg, jnp.sum(jnp.sum(dyh * xh, axis=1, keepdims=True), axis=0, keepdims=True), dD4)
            dstate[hg] = dSh * jnp.exp(csq) + lax.dot_general((dyh * ecs).astype(BF16), Cb, TN, preferred_element_type=F32)
            dxts.append(dxt)
            xhs.append(xh)
            dyhs.append(dyh)
        da4 = lax.dot_general(tril, dcs4, TN, preferred_element_type=F32, precision=HI)
        ddt4 = da4 * A_r + dtx4
        ddtraw = ddt4 * jax.nn.sigmoid(zc)
        ddt_ref[0, 0] = ddtraw
        dxs = [dxts[hg] * dt_c[:, hg:hg + 1] + pr[2:3, hg:hg + 1] * dyhs[hg] for hg in range(HG)]
        dx_ref[0] = jnp.concatenate([jnp.concatenate(dxs[0:2], axis=-1), jnp.concatenate(dxs[2:4], axis=-1)], axis=-1)
        db_ref[0] = dB
        dc_ref[0] = dC
        d_bias = jnp.sum(ddtraw, axis=0, keepdims=True)
        d_alog = jnp.sum(da4 * dt_c, axis=0, keepdims=True) * A_r
        r3 = lax.broadcasted_iota(jnp.int32, (3, HG), 0)
        dpar = jnp.where(r3 == 0, d_bias, jnp.where(r3 == 1, d_alog, dD4))
        first = (bi == 0) & (ci == 0)

        @pl.when(first)
        def _():
            dpar_ref[0] = dpar

        @pl.when(jnp.logical_not(first))
        def _():
            dpar_ref[0] += dpar

    rc = lambda c: nc - 1 - c
    bo, co = SSM_D_INNER // N, (SSM_D_INNER + SSM_BC_DIM) // N
    return _hosted(plan, "b_ssd_bwd", lambda comm: _pc(
        body, (xc3, xc3, xc3, dtc, dtr, prow, pcol, states, dy3),
        out_shape=(SDS((B, S, SSM_D_INNER), F32), SDS((B, S, SSM_BC_DIM), F32), SDS((B, S, SSM_BC_DIM), F32),
                   SDS((B, G, S, HG), F32), SDS((G, 3, HG), F32)),
        grid=(G, B, nc),
        in_specs=[pl.BlockSpec((1, Q, xw), lambda g, b, c: (b, rc(c), g)), pl.BlockSpec((1, Q, N), lambda g, b, c: (b, rc(c), bo + g)),
                  pl.BlockSpec((1, Q, N), lambda g, b, c: (b, rc(c), co + g)), pl.BlockSpec((1, 1, Q, HG), lambda g, b, c: (b, g, rc(c), 0)),
                  pl.BlockSpec((1, 1, HG, Q), lambda g, b, c: (b, g, 0, rc(c))), pl.BlockSpec((1, 3, HG), lambda g, b, c: (g, 0, 0)),
                  pl.BlockSpec((1, HG, 3), lambda g, b, c: (g, 0, 0)),
                  pl.BlockSpec((1, 1, 1, xw, N), lambda g, b, c: (b, g, rc(c), 0, 0)), pl.BlockSpec((1, Q, xw), lambda g, b, c: (b, rc(c), g))],
        out_specs=(pl.BlockSpec((1, Q, xw), lambda g, b, c: (b, rc(c), g)), pl.BlockSpec((1, Q, N), lambda g, b, c: (b, rc(c), g)),
                   pl.BlockSpec((1, Q, N), lambda g, b, c: (b, rc(c), g)), pl.BlockSpec((1, 1, Q, HG), lambda g, b, c: (b, g, rc(c), 0)),
                   pl.BlockSpec((1, 3, HG), lambda g, b, c: (g, 0, 0))),
        scratch_shapes=[pltpu.VMEM((HG, P, N), F32)], name="b_ssd_bwd", sem=("parallel", "arbitrary", "arbitrary"), comm=comm))


GN_W = SSM_D_INNER // SSM_N_GROUPS


def _gate_fwd(y, zx, nw):
    T = y.shape[0]
    tm = _pick(T, 256, 8)

    def body(y_ref, z_ref, w_ref, o_ref):
        z = z_ref[...]
        gt = y_ref[...] * (z * jax.nn.sigmoid(z))
        outs = []
        for k in range(SSM_N_GROUPS):
            gk = gt[:, GN_W * k:GN_W * (k + 1)]
            outs.append(gk * lax.rsqrt(jnp.mean(gk * gk, axis=-1, keepdims=True) + NORM_EPS))
        o_ref[...] = (jnp.concatenate(outs, axis=-1) * w_ref[...]).astype(BF16)

    row = pl.BlockSpec((tm, SSM_D_INNER), lambda i: (i, 0))
    return pl.pallas_call(
        body, out_shape=SDS((T, SSM_D_INNER), BF16), grid=(T // tm,),
        in_specs=[row, row, pl.BlockSpec((1, SSM_D_INNER), lambda i: (0, 0))], out_specs=row, name="b_gate_fwd",
        compiler_params=_params(("parallel",)),
    )(y, zx, nw)


def _gate_bwd(dgn, y, zx, nw):
    T = y.shape[0]
    tm = _pick(T, 256, 8)

    def body(d_ref, y_ref, z_ref, w_ref, dy_ref, dz_ref, dw_ref):
        z, yv, w = z_ref[...], y_ref[...], w_ref[...]
        sg = jax.nn.sigmoid(z)
        sz = z * sg
        gt = yv * sz
        gw = d_ref[...] * w
        dgts, dws = [], []
        for k in range(SSM_N_GROUPS):
            sl = slice(GN_W * k, GN_W * (k + 1))
            gk, gwk = gt[:, sl], gw[:, sl]
            rstd = lax.rsqrt(jnp.mean(gk * gk, axis=-1, keepdims=True) + NORM_EPS)
            dgts.append(rstd * gwk - gk * (rstd * rstd * rstd) * jnp.mean(gwk * gk, axis=-1, keepdims=True))
            dws.append(jnp.sum(d_ref[:, sl] * gk * rstd, axis=0, keepdims=True))
        dgt = jnp.concatenate(dgts, axis=-1)
        dy_ref[...] = dgt * sz
        dz_ref[...] = (dgt * yv * (sg * (1.0 + z * (1.0 - sg)))).astype(BF16)
        dw = jnp.concatenate(dws, axis=-1)

        @pl.when(pl.program_id(0) == 0)
        def _():
            dw_ref[...] = dw

        @pl.when(pl.program_id(0) > 0)
        def _():
            dw_ref[...] += dw

    row = pl.BlockSpec((tm, SSM_D_INNER), lambda i: (i, 0))
    vec = pl.BlockSpec((1, SSM_D_INNER), lambda i: (0, 0))
    return pl.pallas_call(
        body, out_shape=(SDS((T, SSM_D_INNER), F32), SDS((T, SSM_D_INNER), BF16), SDS((1, SSM_D_INNER), F32)), grid=(T // tm,),
        in_specs=[row, row, row, vec], out_specs=(row, row, vec), name="b_gate_bwd", compiler_params=_params(("arbitrary",)),
    )(dgn, y, zx, nw)


N_CHIPS = 4


def _dev_block(ref, kind, j, size):
    if kind == "slot":
        return ref.at[j]
    start = pl.multiple_of(j * size, size)
    nd = len(ref.shape)
    if kind == "col":
        return ref.at[(slice(None),) * (nd - 1) + (pl.ds(start, size),)]
    return ref.at[(slice(None),) * (nd - 2) + (pl.ds(start, size), slice(None))]


def _dma_sems(n, k):
    return [pltpu.SemaphoreType.DMA((n, k)), pltpu.SemaphoreType.DMA((n, k)), pltpu.SemaphoreType.DMA((n, k))]


def _place(shard, layer, kind, full_shape, dev, name):
    k, n = shard.shape[1:]
    tr = _pick(k, 512, 16)
    nb = k // tr

    def body(dev_ref, s_ref, o_ref):
        if kind == "slot":
            o_ref[0] = s_ref[0].astype(BF16)
        else:
            o_ref[...] = s_ref[0].astype(BF16)

    out_spec = {"slot": pl.BlockSpec((1, tr, n), lambda i, d: (d[0], i, 0)),
                "row": pl.BlockSpec((tr, n), lambda i, d: (d[0] * nb + i, 0)),
                "col": pl.BlockSpec((tr, n), lambda i, d: (i, d[0]))}[kind]
    return pl.pallas_call(
        body, out_shape=SDS(full_shape, BF16),
        grid_spec=pltpu.PrefetchScalarGridSpec(
            num_scalar_prefetch=1, grid=(nb,), in_specs=[pl.BlockSpec((1, tr, n), lambda i, d: (layer, i, 0))], out_specs=out_spec),
        name=name, compiler_params=_params(("arbitrary",)),
    )(dev, shard)


def _run_comm(comm, name):
    c_in = len(comm.inputs)

    def body(*refs):
        cins, couts, sems = refs[:c_in], refs[c_in:c_in + len(comm.out_shapes)], refs[c_in + len(comm.out_shapes):]
        for _, fn in comm.phases:
            fn(cins, couts, sems)

    return pl.pallas_call(
        body, out_shape=list(comm.out_shapes), in_specs=[ANY] * c_in, out_specs=[ANY] * len(comm.out_shapes),
        input_output_aliases=dict(comm.aliases), scratch_shapes=list(comm.sems), name=name,
    )(*comm.inputs)


def _gather_comm(items, mid=0.7):
    n = len(items)

    def tools(srcs, dsts, sems):
        send_sems, recv_sems, local_sems = sems
        px, py, pc = lax.axis_index("x"), lax.axis_index("y"), lax.axis_index("c")
        me, sibling = (px, py, pc), (px, py, 1 - pc)
        chips = [(1 - px, py), (px, 1 - py), (1 - px, 1 - py)]

        def blk(a, dev):
            return _dev_block(dsts[a], items[a][1], 4 * dev[0] + 2 * dev[1] + dev[2], items[a][2])

        def copy(a, k, block, to, src=None):
            return pltpu.make_async_remote_copy(
                src_ref=blk(a, block) if src is None else src, dst_ref=blk(a, block),
                send_sem=send_sems.at[a, k], recv_sem=recv_sems.at[a, k], device_id=to, device_id_type=MESH)

        def mine():
            return [pltpu.make_async_copy(srcs[a], blk(a, me), local_sems.at[a, 0]) for a in range(n) if not items[a][4]]

        def first():
            out = []
            for a in range(n):
                src = blk(a, me) if items[a][4] else srcs[a]
                out.append(copy(a, 0, me, sibling, src=src))
                out += [copy(a, 1 + j, me, (*chip, pc), src=src) for j, chip in enumerate(chips)]
            return out

        def passed():
            return [copy(a, 4 + j, (*chip, pc), sibling) for j, chip in enumerate(chips) for a in range(n)]

        return me, sibling, chips, pc, copy, mine, first, passed

    def start(srcs, dsts, sems):
        *_, mine, first, _ = tools(srcs, dsts, sems)
        for cp in mine() + first():
            cp.start()

    def forward(srcs, dsts, sems):
        me, _, chips, pc, copy, _, _, passed = tools(srcs, dsts, sems)
        fwd = passed()
        for j, chip in enumerate(chips):
            for a in range(n):
                copy(a, 1 + j, (*chip, pc), me).wait_recv()
                fwd[j * n + a].start()

    def finish(srcs, dsts, sems):
        me, sibling, chips, pc, copy, mine, first, passed = tools(srcs, dsts, sems)
        for a in range(n):
            copy(a, 0, sibling, me).wait_recv()
            for j, chip in enumerate(chips):
                copy(a, 4 + j, (*chip, 1 - pc), me).wait_recv()
        for cp in first() + passed():
            cp.wait_send()
        for cp in mine():
            cp.wait()

    return _Comm([it[0] for it in items], [SDS(it[3], it[0].dtype) for it in items],
                 {a: a for a in range(n) if items[a][4]}, _dma_sems(n, 7), [(0.0, start), (mid, forward), (1.0, finish)])


def _gather(items, name):
    return _run_comm(_gather_comm(items), name)


def _reduce_d2d_comm(items):
    n = len(items)

    def copies(gs, gots, sems):
        send_sems, recv_sems, _ = sems
        px, py, pc = lax.axis_index("x"), lax.axis_index("y"), lax.axis_index("c")
        out = []
        for a in range(n):
            _, kind, size, _ = items[a]
            for q in range(N_CHIPS):
                out.append(pltpu.make_async_remote_copy(
                    src_ref=_dev_block(gs[a], kind, 2 * q + 1 - pc, size), dst_ref=gots[a].at[q], send_sem=send_sems.at[a, q],
                    recv_sem=recv_sems.at[a, q], device_id=(px, py, 1 - pc), device_id_type=MESH))
        return out

    def start(gs, gots, sems):
        for cp in copies(gs, gots, sems):
            cp.start()

    def finish(gs, gots, sems):
        for cp in copies(gs, gots, sems):
            cp.wait()

    return _Comm([it[0] for it in items], [SDS((N_CHIPS,) + tuple(it[3]), F32) for it in items], {},
                 _dma_sems(n, N_CHIPS), [(0.0, start), (1.0, finish)])


def _pair_sum(g, got, kind, core, name):
    _, k, n = got.shape
    tr = _pick(k, max(16, STREAM_VMEM // (2 * n * 10)), 16)
    nb = k // tr

    def body(c_ref, g_ref, s_ref, o_ref):
        mine = g_ref[0] if kind == "slot" else g_ref[...]
        o_ref[0] = (mine + s_ref[0]).astype(BF16)

    g_spec = {"slot": pl.BlockSpec((1, tr, n), lambda q, i, c: (2 * q + c[0], i, 0)),
              "row": pl.BlockSpec((tr, n), lambda q, i, c: ((2 * q + c[0]) * nb + i, 0)),
              "col": pl.BlockSpec((tr, n), lambda q, i, c: (i, 2 * q + c[0]))}[kind]
    part = pl.BlockSpec((1, tr, n), lambda q, i, c: (q, i, 0))
    return pl.pallas_call(
        body, out_shape=SDS((N_CHIPS, k, n), BF16),
        grid_spec=pltpu.PrefetchScalarGridSpec(num_scalar_prefetch=1, grid=(N_CHIPS, nb), in_specs=[g_spec, part], out_specs=part),
        name=name, compiler_params=_params(("arbitrary", "arbitrary")),
    )(core, g, got)


def _reduce_ici_comm(parts):
    n = len(parts)

    def copies(ps, rs, sems, arriving):
        send_sems, recv_sems, _ = sems
        px, py, pc = lax.axis_index("x"), lax.axis_index("y"), lax.axis_index("c")
        my_chip = 2 * px + py
        out = []
        for a in range(n):
            for k in range(1, N_CHIPS):
                qx, qy = px ^ (k >> 1), py ^ (k & 1)
                q = 2 * qx + qy
                out.append(pltpu.make_async_remote_copy(
                    src_ref=ps[a].at[q], dst_ref=rs[a].at[q] if arriving else rs[a].at[my_chip], send_sem=send_sems.at[a, k - 1],
                    recv_sem=recv_sems.at[a, k - 1], device_id=(qx, qy, pc), device_id_type=MESH))
        return out

    def start(ps, rs, sems):
        for cp in copies(ps, rs, sems, False):
            cp.start()

    def finish(ps, rs, sems):
        for cp in copies(ps, rs, sems, True):
            cp.wait_recv()
        for cp in copies(ps, rs, sems, False):
            cp.wait_send()

    return _Comm(list(parts), [SDS(p.shape, p.dtype) for p in parts], {}, _dma_sems(n, N_CHIPS - 1),
                 [(0.0, start), (1.0, finish)])


def _adam_update(g, w, m, v):
    c1 = 1.0 - ADAM_B1 ** ADAM_STEP
    c2 = 1.0 - ADAM_B2 ** ADAM_STEP
    nm = ADAM_B1 * m + (1.0 - ADAM_B1) * g
    nv = ADAM_B2 * v + (1.0 - ADAM_B2) * (g * g)
    delta = -ADAM_LR * ((nm / c1) / (jnp.sqrt(nv / c2) + ADAM_EPS) + ADAM_WD * w)
    return delta, nm, nv


def _adamw(parts, recv, w, m, v, layer, prev, chip, name):
    _, R, C = w.shape
    row_bytes = 2 * C * (N_CHIPS * 2 + 7 * 4)
    tr = _pick(R, max(16, STREAM_VMEM // row_bytes), 16)
    n_prev = 0 if prev is None else 4

    def body(ch_ref, own_ref, r1_ref, r2_ref, r3_ref, w_ref, m_ref, v_ref, *rest):
        g_ref, d_ref, nm_ref, nv_ref = rest[n_prev:]
        g = own_ref[0].astype(F32)
        for r_ref in (r1_ref, r2_ref, r3_ref):
            g = g + r_ref[0].astype(F32)
        g_ref[0] = g
        d_ref[0], nm_ref[0], nv_ref[0] = _adam_update(g, w_ref[0], m_ref[0], v_ref[0])

    lay = pl.BlockSpec((1, tr, C), lambda i, ch: (layer, i, 0))
    other = lambda k: pl.BlockSpec((1, tr, C), lambda i, ch: (ch[0] ^ k, i, 0))
    out = SDS(w.shape, F32)
    return pl.pallas_call(
        body, out_shape=(out, out, out, out),
        grid_spec=pltpu.PrefetchScalarGridSpec(
            num_scalar_prefetch=1, grid=(R // tr,),
            in_specs=[pl.BlockSpec((1, tr, C), lambda i, ch: (ch[0], i, 0)), other(2), other(1), other(3), lay, lay, lay]
            + [ANY] * n_prev,
            out_specs=(lay, lay, lay, lay)),
        input_output_aliases={8 + k: k for k in range(n_prev)},
        name=name, compiler_params=_params(("arbitrary",)),
    )(chip, parts, recv, recv, recv, w, m, v, *(prev or ()))


def _small_adamw(gathered, ws, ms, vs):
    n = len(ws)

    def body(*refs):
        g_in, w_in, m_in, v_in = refs[:n], refs[n:2 * n], refs[2 * n:3 * n], refs[3 * n:4 * n]
        outs = refs[4 * n:]
        for i in range(n):
            g = g_in[i][0]
            for dev in range(1, N_DEV):
                g = g + g_in[i][dev]
            d, nm, nv = _adam_update(g, w_in[i][...], m_in[i][...], v_in[i][...])
            outs[i][...] = g
            outs[n + i][...] = d
            outs[2 * n + i][...] = nm
            outs[3 * n + i][...] = nv

    shapes = [SDS(w.shape, F32) for w in ws]
    outs = pl.pallas_call(body, out_shape=shapes * 4, name="small_adamw")(*gathered, *ws, *ms, *vs)
    return outs[:n], outs[n:2 * n], outs[2 * n:3 * n], outs[3 * n:]


W_NAMES = ("norm_mix_w", "norm_mlp_w", "a_w_qkv", "a_b_qkv", "a_sinks", "a_w_o", "a_b_o", "b_in_w", "b_conv_w", "b_conv_b",
           "b_dt_bias", "b_a_log", "b_d", "b_norm_w", "b_out_w", "c_w_qkv", "c_w_o", "mlp_w_up", "mlp_w_down", "final_norm_w")
BIG_KIND = {"a_w_qkv": "slot", "a_w_o": "row", "b_in_w": "slot", "b_out_w": "row", "c_w_qkv": "col", "c_w_o": "row",
            "mlp_w_up": "col", "mlp_w_down": "row"}
SMALL_SHARDED = {"a_b_qkv": 1, "a_b_o": 1, "b_conv_w": 2}
SMALL_REPLICATED = ("norm_mix_w", "norm_mlp_w", "a_sinks", "b_conv_b", "b_dt_bias", "b_a_log", "b_d", "b_norm_w", "final_norm_w")


def _layer_big(i):
    kind, j = i % 3, i // 3
    mix = {0: [("a_w_qkv", j), ("a_w_o", j)], 1: [("b_in_w", 0), ("b_out_w", 0)], 2: [("c_w_qkv", 0), ("c_w_o", 0)]}[kind]
    return mix + [("mlp_w_up", i), ("mlp_w_down", i)]


def _block_size(kind, shard2d):
    return {"slot": None, "row": shard2d[0], "col": shard2d[1]}[kind]


def _full2d(kind, shard2d):
    k, n = shard2d
    return {"slot": (N_DEV, k, n), "row": (N_DEV * k, n), "col": (k, N_DEV * n)}[kind]


def _from_slots(t, ax):
    s = t.shape[1:]
    return jnp.moveaxis(t, 0, ax).reshape(s[:ax] + (N_DEV * s[ax],) + s[ax + 1:])


def _to_slots(g, ax):
    s = g.shape
    return jnp.moveaxis(g.reshape(s[:ax] + (N_DEV, s[ax] // N_DEV) + s[ax + 1:]), ax, 0)


def _rope_tables(positions):
    half = HEAD_DIM // 2
    inv = ROPE_THETA ** (-(jnp.arange(LANES, dtype=jnp.int32) % half).astype(F32) / half)
    ang = positions.astype(F32).reshape(-1, 1) * inv
    return jnp.cos(ang), jnp.sin(ang)


def _swa_fwd(u, h, p, j, B, S, cos, sin, tag, plan=None):
    qkv = _matmul(u, p["a_w_qkv"][j], out_dtype=BF16, bias=p["a_b_qkv"][j][None], rope=(cos, sin),
                  rope_cols=A_Q_DIM + A_KV_DIM, name=f"{tag}_qkv", plan=plan)
    o, lse = _attn_fwd(qkv, B, S, 1, n_heads=A_N_HEADS, n_kv=A_N_KV, q_col=0, k_col=A_Q_DIM, v_col=A_Q_DIM + A_KV_DIM,
                       max_dist=A_WINDOW - 1, sinks=p["a_sinks"][j], name=f"{tag}_attn", plan=plan)
    h1 = _matmul(o, p["a_w_o"][j], bias=p["a_b_o"][j][None], resid=h, name=f"{tag}_o")
    return h1, (qkv, o, lse)


def _swa_bwd(dh1, u, saved, p, j, B, S, cos, sin, tag, plan=None):
    qkv, o, lse = saved
    kw = dict(n_heads=A_N_HEADS, n_kv=A_N_KV, q_col=0, k_col=A_Q_DIM, v_col=A_Q_DIM + A_KV_DIM, max_dist=A_WINDOW - 1)
    g = {}
    do = _matmul(dh1, p["a_w_o"][j], tb=True, name=f"{tag}_do")
    g["a_w_o"] = _matmul(o, dh1, ta=True, name=f"{tag}_dwo")
    g["a_b_o"] = _colsum(dh1, f"{tag}_dbo")[0]
    sk = jnp.pad(p["a_sinks"][j], (0, LANES - A_N_HEADS))[None]
    delta, dob, dsink = _delta(do, o, lse, sk, name=f"{tag}_delta")
    g["a_sinks"] = dsink[0, :A_N_HEADS]
    dq = _attn_dq(qkv, dob, lse, delta, cos, sin, B, S, 1, name=f"{tag}_dq", plan=plan, **kw)
    dk, dv = _attn_dkv(qkv, dob, lse, delta, cos, sin, B, S, 1, name=f"{tag}_dkv", plan=plan, **kw)
    dqkv = jnp.concatenate([dq, dk, dv], axis=1)
    g["a_w_qkv"] = _matmul(u, dqkv, ta=True, name=f"{tag}_dwqkv")
    g["a_b_qkv"] = _colsum(dqkv, f"{tag}_dbqkv")[0]
    if plan is not None:
        plan.grads(3 * j, "mix", {"a_w_qkv": g["a_w_qkv"], "a_w_o": g["a_w_o"]})
    du = _matmul(dqkv, p["a_w_qkv"][j], tb=True, name=f"{tag}_du", plan=plan)
    return du, g


def _group_cols(gi, qkv):
    W = C_HEADS * HEAD_DIM
    if C_PATTERNS[gi][1] == 1:
        return qkv, (gi * W, (3 + gi) * W, (6 + gi) * W)
    part = jnp.concatenate([qkv[:, (3 * j + gi) * W:(3 * j + gi + 1) * W] for j in range(3)], axis=1)
    return part, (0, W, 2 * W)


def _dil_fwd(u, h, p, B, S, cos, sin, plan=None):
    W = C_HEADS * HEAD_DIM
    qkv = _matmul(u, p["c_w_qkv"][0], out_dtype=BF16, rope=(cos, sin), rope_cols=6 * W, name="c_qkv", plan=plan)
    os_, lses, parts = [], [], []
    for gi, (window, dil) in enumerate(C_PATTERNS):
        part, (qc, kc, vc) = _group_cols(gi, qkv)
        o, lse = _attn_fwd(part, B, S, dil, n_heads=C_HEADS, n_kv=C_HEADS, q_col=qc, k_col=kc, v_col=vc,
                           max_dist=window // dil, sinks=None, name=f"c_attn{gi}")
        os_.append(o)
        lses.append(lse)
        parts.append((part, (qc, kc, vc)))
    o, lse = _merge(os_, lses)
    h1 = _matmul(o, p["c_w_o"][0], resid=h, name="c_o")
    return h1, (parts, o, lse)


def _dil_bwd(dh1, u, saved, p, B, S, cos, sin, plan=None):
    parts, o, lse = saved
    g = {}
    do = _matmul(dh1, p["c_w_o"][0], tb=True, name="c_do")
    g["c_w_o"] = _matmul(o, dh1, ta=True, name="c_dwo")[None]
    delta, dob = _delta(do, o, name="c_delta")
    dqs, dks, dvs = [], [], []
    for gi, (window, dil) in enumerate(C_PATTERNS):
        part, (qc, kc, vc) = parts[gi]
        kw = dict(n_heads=C_HEADS, n_kv=C_HEADS, q_col=qc, k_col=kc, v_col=vc, max_dist=window // dil)
        dqs.append(_attn_dq(part, dob, lse, delta, cos, sin, B, S, dil, name=f"c_dq{gi}", **kw))
        dk, dv = _attn_dkv(part, dob, lse, delta, cos, sin, B, S, dil, name=f"c_dkv{gi}", **kw)
        dks.append(dk)
        dvs.append(dv)
    dqkv = jnp.concatenate(dqs + dks + dvs, axis=1)
    g["c_w_qkv"] = _matmul(u, dqkv, ta=True, name="c_dwqkv", plan=plan)[None]
    if plan is not None:
        plan.grads(2, "mix", {"c_w_qkv": g["c_w_qkv"][0], "c_w_o": g["c_w_o"][0]})
    du = _matmul(dqkv, p["c_w_qkv"][0], tb=True, name="c_du", plan=plan)
    return du, g


def _ssm_params(p):
    par = jnp.stack([p["b_dt_bias"][0], p["b_a_log"][0], p["b_d"][0]], axis=0)
    prow = par.reshape(3, SSM_N_GROUPS, SSM_HG).transpose(1, 0, 2)
    return prow, prow.transpose(0, 2, 1)


def _mamba_fwd(u, h, p, B, S, plan=None):
    T = B * S
    G, HG = SSM_N_GROUPS, SSM_HG
    w_in = p["b_in_w"][0]
    nzx = SSM_D_INNER + SSM_CONV_DIM
    w_dt = jnp.pad(w_in[:, nzx:], ((0, 0), (0, LANES - SSM_N_HEADS)))
    zx = _matmul(u, w_in[:, :nzx], name="b_zx", plan=plan)
    dtraw = _matmul(u, w_dt, name="b_dt")[:, :SSM_N_HEADS]
    dtc = dtraw.reshape(B, S, G, HG).transpose(0, 2, 1, 3)
    dtr = dtraw.reshape(B, S, G, HG).transpose(0, 2, 3, 1)
    prow, pcol = _ssm_params(p)
    zx3 = zx.reshape(B, S, nzx)
    xc3 = _conv_fwd(zx3, p["b_conv_w"][0], p["b_conv_b"])
    y3, states = _ssd_fwd(xc3, dtc, dtr, prow, pcol, plan=plan)
    y = y3.reshape(T, SSM_D_INNER)
    gn = _gate_fwd(y, zx, p["b_norm_w"])
    h1 = _matmul(gn, p["b_out_w"][0], resid=h, name="b_out")
    return h1, (zx, dtc, dtr, xc3, y, states, gn, w_dt)


def _mamba_bwd(dh1, u, saved, p, B, S, plan=None):
    T = B * S
    zx, dtc, dtr, xc3, y, states, gn, w_dt = saved
    nzx = SSM_D_INNER + SSM_CONV_DIM
    w_in = p["b_in_w"][0]
    prow, pcol = _ssm_params(p)
    g = {}
    dgn = _matmul(dh1, p["b_out_w"][0], tb=True, name="b_dgn")
    g["b_out_w"] = _matmul(gn, dh1, ta=True, name="b_dwout")[None]
    dy, dz, dnw = _gate_bwd(dgn, y, zx, p["b_norm_w"])
    g["b_norm_w"] = dnw
    dx3, dB3, dC3, ddt, dpar = _ssd_bwd(xc3, dtc, dtr, prow, pcol, states, dy.reshape(B, S, SSM_D_INNER), plan=plan)
    dpar = dpar.transpose(1, 0, 2).reshape(3, SSM_N_HEADS)
    g["b_dt_bias"], g["b_a_log"], g["b_d"] = dpar[0:1], dpar[1:2], dpar[2:3]
    zx3 = zx.reshape(B, S, nzx)
    cw, cb = p["b_conv_w"][0], p["b_conv_b"]
    parts, dws, dbs = [], [], []
    for col0, dpart, nm in ((0, dx3, "b_conv_bwd_x"), (SSM_D_INNER, dB3, "b_conv_bwd_b"),
                            (SSM_D_INNER + SSM_BC_DIM, dC3, "b_conv_bwd_c")):
        dxp, dw, db = _conv_bwd(zx3, dpart, cw, cb, col0, nm)
        parts.append(dxp.reshape(T, -1))
        dws.append(dw)
        dbs.append(db)
    g["b_conv_w"] = jnp.concatenate(dws, axis=1)[None]
    g["b_conv_b"] = jnp.concatenate(dbs, axis=1)
    dzx = jnp.concatenate([dz] + parts, axis=1)
    ddtraw = ddt.transpose(0, 2, 1, 3).reshape(T, SSM_N_HEADS)
    ddtp = jnp.pad(ddtraw, ((0, 0), (0, LANES - SSM_N_HEADS)))
    dw_zx = _matmul(u, dzx, ta=True, name="b_dwzx")
    dw_dt = _matmul(u, ddtp, ta=True, name="b_dwdt")[:, :SSM_N_HEADS]
    g["b_in_w"] = jnp.concatenate([dw_zx, dw_dt], axis=1)[None]
    if plan is not None:
        plan.grads(1, "mix", {"b_in_w": g["b_in_w"][0], "b_out_w": g["b_out_w"][0]})
    du = _matmul(dzx, w_in[:, :nzx], tb=True, name="b_du_zx", plan=plan)
    du = _matmul(ddtp, w_dt, tb=True, resid=du, name="b_du_dt")
    return du, g


def _local_step(x, positions, p, target, plan=None):
    B, S, D = x.shape
    T = B * S
    cos, sin = _rope_tables(positions)
    h = x.reshape(T, D)
    tape = []
    for i in range(DEPTH):
        kind, j = i % 3, i // 3
        u = _rmsnorm_fwd(h, p["norm_mix_w"][i], f"l{i}_norm_mix")
        if kind == 0:
            h1, saved = _swa_fwd(u, h, p, j, B, S, cos, sin, f"a{j}", plan)
        elif kind == 1:
            h1, saved = _mamba_fwd(u, h, p, B, S, plan)
        else:
            h1, saved = _dil_fwd(u, h, p, B, S, cos, sin, plan)
        u2 = _rmsnorm_fwd(h1, p["norm_mlp_w"][i], f"l{i}_norm_mlp")
        r, s = _matmul(u2, p["mlp_w_up"][i], out_dtype=BF16, relu2=True, name=f"l{i}_up", plan=plan)
        h2 = _matmul(s, p["mlp_w_down"][i], resid=h1, name=f"l{i}_down", plan=plan)
        tape.append((h, u, saved, h1, u2, r, s))
        h = h2
    dh, dwf, loss = _final_loss(h, target.reshape(T, D), p["final_norm_w"])
    grads = {"final_norm_w": dwf[0]}
    per_layer = {n: [None] * DEPTH for n in ("norm_mix_w", "norm_mlp_w", "mlp_w_up", "mlp_w_down")}
    a_grads = [None, None]
    for i in reversed(range(DEPTH)):
        kind, j = i % 3, i // 3
        h0, u, saved, h1, u2, r, s = tape[i]
        da = _matmul(dh, p["mlp_w_down"][i], tb=True, out_dtype=BF16, mul=r, mul_scale=2.0, name=f"l{i}_da", plan=plan)
        per_layer["mlp_w_down"][i] = _matmul(s, dh, ta=True, name=f"l{i}_dwdown")
        per_layer["mlp_w_up"][i] = _matmul(u2, da, ta=True, name=f"l{i}_dwup")
        if plan is not None:
            plan.grads(i, "mlp", {"mlp_w_up": per_layer["mlp_w_up"][i], "mlp_w_down": per_layer["mlp_w_down"][i]})
        du2 = _matmul(da, p["mlp_w_up"][i], tb=True, name=f"l{i}_du2", plan=plan)
        dh1, dnw = _rmsnorm_bwd(h1, du2, p["norm_mlp_w"][i], dh, f"l{i}_norm_mlp_bwd")
        per_layer["norm_mlp_w"][i] = dnw[0]
        if kind == 0:
            du, g = _swa_bwd(dh1, u, saved, p, j, B, S, cos, sin, f"a{j}", plan)
            a_grads[j] = g
        elif kind == 1:
            du, g = _mamba_bwd(dh1, u, saved, p, B, S, plan)
            grads.update(g)
        else:
            du, g = _dil_bwd(dh1, u, saved, p, B, S, cos, sin, plan)
            grads.update(g)
        dh, dnw = _rmsnorm_bwd(h0, du, p["norm_mix_w"][i], dh1, f"l{i}_norm_mix_bwd")
        per_layer["norm_mix_w"][i] = dnw[0]
    for n in ("norm_mix_w", "norm_mlp_w"):
        grads[n] = jnp.stack(per_layer[n], axis=0)
    for n in ("mlp_w_up", "mlp_w_down"):
        grads[n] = per_layer[n]
    for n in ("a_b_qkv", "a_sinks", "a_b_o"):
        grads[n] = jnp.stack([a_grads[0][n], a_grads[1][n]], axis=0)
    for n in ("a_w_qkv", "a_w_o"):
        grads[n] = [a_grads[0][n], a_grads[1][n]]
    for n in ("b_in_w", "b_out_w", "c_w_qkv", "c_w_o"):
        grads[n] = [grads[n][0]]
    return loss, dh.reshape(B, S, D), grads


MIX = {0: ("a_w_qkv", "a_w_o"), 1: ("b_in_w", "b_out_w"), 2: ("c_w_qkv", "c_w_o")}
MLP = ("mlp_w_up", "mlp_w_down")
GATHER_FIRST = (0, MIX[0])
GATHER_HOSTS = {"a0_qkv": ((0, ("mlp_w_up",)),), "a0_attn": ((0, ("mlp_w_down",)),), "l0_up": ((1, ("b_in_w",)),),
                "l0_down": ((1, ("b_out_w",)),), "b_zx": ((1, ("mlp_w_up",)),),
                "b_ssd_fwd": ((1, ("mlp_w_down",)), (2, None)), "c_qkv": ((3, None),)}
REDUCE_HOSTS = {(3, "mlp"): ("l3_du2", "a1_dkv"), (3, "mix"): ("a1_du", "l2_da"),
                (2, "mlp"): ("l2_du2", "c_dwqkv"), (2, "mix"): ("c_du", "b_ssd_bwd"),
                (1, "mlp"): ("l1_du2", "b_ssd_bwd"), (1, "mix"): ("b_du_zx", "a0_dq"),
                (0, "mlp"): ("l0_du2", "a0_dkv"), (0, "mix"): ("a0_du", None)}


class _Plan:
    def __init__(self, w, m, v, p, dev, chip, core):
        self.w, self.m, self.v, self.p, self.dev, self.chip, self.core = w, m, v, p, dev, chip, core
        self.pending = {}
        self.res = {n: None for n in BIG_KIND}
        self._install(*GATHER_FIRST)(_gather(self._gather_items(*GATHER_FIRST), "gather_first"))
        for host, groups in GATHER_HOSTS.items():
            for i, only in groups:
                self._wait_for(host, _gather_comm(self._gather_items(i, only)), self._install(i, only))

    def _wait_for(self, host, comm, done):
        self.pending.setdefault(host, []).append((comm, done))

    def _names(self, i, only):
        return [(n, l) for n, l in _layer_big(i) if only is None or n in only]

    def _gather_items(self, i, only):
        items = []
        for n, l in self._names(i, only):
            kind, s2 = BIG_KIND[n], self.w[n].shape[1:]
            placed = _place(self.w[n], l, kind, _full2d(kind, s2), self.dev, f"place_l{i}_{n}")
            items.append((placed, kind, _block_size(kind, s2), _full2d(kind, s2), True))
        return items

    def _install(self, i, only):
        def done(fulls):
            for (n, l), t in zip(self._names(i, only), fulls):
                self.p[n][l] = _from_slots(t, 1) if BIG_KIND[n] == "slot" else t
        return done

    def take(self, host):
        return _Comm.merge([c for c, _ in self.pending[host]]) if host in self.pending else None

    def give(self, host, results):
        for comm, done in self.pending.pop(host):
            done(results[:len(comm.out_shapes)])
            results = results[len(comm.out_shapes):]

    def grads(self, i, group, grads):
        names = self._names(i, MLP if group == "mlp" else MIX[i % 3])
        items = []
        for n, _ in names:
            kind, s2 = BIG_KIND[n], self.w[n].shape[1:]
            items.append((_to_slots(grads[n], 1) if kind == "slot" else grads[n], kind, _block_size(kind, s2), s2))
        d2d_host, ici_host = REDUCE_HOSTS[(i, group)]
        tag = f"l{i}_{group}"

        def update(parts):
            def done(recv):
                for (n, l), pt, r in zip(names, parts, recv):
                    self.res[n] = _adamw(pt, r, self.w[n], self.m[n], self.v[n], l, self.res[n], self.chip, f"adamw_l{i}_{n}")
            return done

        def second(sib):
            parts = [_pair_sum(it[0], s, it[1], self.core, f"pair_sum_l{i}_{n}") for (n, _), it, s in zip(names, items, sib)]
            self._send(ici_host, _reduce_ici_comm(parts), update(parts), f"reduce_ici_{tag}")

        self._send(d2d_host, _reduce_d2d_comm(items), second, f"reduce_d2d_{tag}")

    def _send(self, host, comm, done, name):
        if host is None:
            done(_run_comm(comm, name))
        else:
            self._wait_for(host, comm, done)

    def flush(self):
        late = 0
        while self.pending:
            host = next(iter(self.pending))
            for comm, done in self.pending.pop(host):
                done(_run_comm(comm, f"late_{late}_{host}"))
                late += 1


def kernel(x, positions, norm_mix_w, norm_mlp_w, a_w_qkv, a_b_qkv, a_sinks, a_w_o, a_b_o, b_in_w, b_conv_w, b_conv_b, b_dt_bias, b_a_log, b_d, b_norm_w, b_out_w, c_w_qkv, c_w_o, mlp_w_up, mlp_w_down, final_norm_w, loss_target, m_norm_mix_w, m_norm_mlp_w, m_a_w_qkv, m_a_b_qkv, m_a_sinks, m_a_w_o, m_a_b_o, m_b_in_w, m_b_conv_w, m_b_conv_b, m_b_dt_bias, m_b_a_log, m_b_d, m_b_norm_w, m_b_out_w, m_c_w_qkv, m_c_w_o, m_mlp_w_up, m_mlp_w_down, m_final_norm_w, v_norm_mix_w, v_norm_mlp_w, v_a_w_qkv, v_a_b_qkv, v_a_sinks, v_a_w_o, v_a_b_o, v_b_in_w, v_b_conv_w, v_b_conv_b, v_b_dt_bias, v_b_a_log, v_b_d, v_b_norm_w, v_b_out_w, v_c_w_qkv, v_c_w_o, v_mlp_w_up, v_mlp_w_down, v_final_norm_w):
    w = dict(zip(W_NAMES, (norm_mix_w, norm_mlp_w, a_w_qkv, a_b_qkv, a_sinks, a_w_o, a_b_o, b_in_w, b_conv_w, b_conv_b,
                           b_dt_bias, b_a_log, b_d, b_norm_w, b_out_w, c_w_qkv, c_w_o, mlp_w_up, mlp_w_down, final_norm_w)))
    m = dict(zip(W_NAMES, (m_norm_mix_w, m_norm_mlp_w, m_a_w_qkv, m_a_b_qkv, m_a_sinks, m_a_w_o, m_a_b_o, m_b_in_w,
                           m_b_conv_w, m_b_conv_b, m_b_dt_bias, m_b_a_log, m_b_d, m_b_norm_w, m_b_out_w, m_c_w_qkv, m_c_w_o,
                           m_mlp_w_up, m_mlp_w_down, m_final_norm_w)))
    v = dict(zip(W_NAMES, (v_norm_mix_w, v_norm_mlp_w, v_a_w_qkv, v_a_b_qkv, v_a_sinks, v_a_w_o, v_a_b_o, v_b_in_w,
                           v_b_conv_w, v_b_conv_b, v_b_dt_bias, v_b_a_log, v_b_d, v_b_norm_w, v_b_out_w, v_c_w_qkv, v_c_w_o,
                           v_mlp_w_up, v_mlp_w_down, v_final_norm_w)))
    px, py, pc = lax.axis_index("x"), lax.axis_index("y"), lax.axis_index("c")
    me = 4 * px + 2 * py + pc
    dev, chip, core = (t.astype(jnp.int32).reshape(1) for t in (me, 2 * px + py, pc))

    trio = tuple(SMALL_SHARDED)
    got = _gather([(d[n], "slot", None, (N_DEV,) + d[n].shape, False) for n in trio for d in (w, m, v)], "gather_small")
    slots = {n: got[3 * i:3 * i + 3] for i, n in enumerate(trio)}
    p = {n: w[n] for n in SMALL_REPLICATED}
    for n in trio:
        p[n] = _from_slots(slots[n][0], SMALL_SHARDED[n])
    for n in BIG_KIND:
        p[n] = [None] * w[n].shape[0]
    plan = _Plan(w, m, v, p, dev, chip, core)
    loss_part, dx, grads = _local_step(x, positions, p, loss_target, plan)
    loss = lax.psum(loss_part[0, 0], AXES)
    plan.flush()
    out = {n: list(plan.res[n]) for n in BIG_KIND}

    small = SMALL_REPLICATED + trio
    as2d = lambda t: t.reshape(1, -1) if t.ndim == 1 else t
    g_sm = [as2d(grads[n]) for n in SMALL_REPLICATED] + [_to_slots(grads[n].reshape(p[n].shape), SMALL_SHARDED[n]) for n in trio]
    gathered = _gather([(g, "slot", None, (N_DEV,) + g.shape, False) for g in g_sm], "gather_small_grads")
    ws = [as2d(w[n]) for n in SMALL_REPLICATED] + [slots[n][0] for n in trio]
    ms = [as2d(m[n]) for n in SMALL_REPLICATED] + [slots[n][1] for n in trio]
    vs = [as2d(v[n]) for n in SMALL_REPLICATED] + [slots[n][2] for n in trio]
    sm_out = _small_adamw(gathered, ws, ms, vs)
    for i, n in enumerate(small):
        if n in SMALL_SHARDED:
            out[n] = [lax.dynamic_index_in_dim(sm_out[k][i], me, 0, keepdims=False) for k in range(4)]
        else:
            out[n] = [sm_out[k][i].reshape(w[n].shape) for k in range(4)]
    return (loss, dx, *[out[n][0] for n in W_NAMES], *[out[n][1] for n in W_NAMES], *[out[n][2] for n in W_NAMES],
            *[out[n][3] for n in W_NAMES])
```

```python
import functools
import math

import jax
import jax.numpy as jnp
import numpy as np
from jax import lax
from jax.experimental import pallas as pl
from jax.experimental.pallas import tpu as pltpu

F32 = jnp.float32
BF16 = jnp.bfloat16
SDS = jax.ShapeDtypeStruct

D_MODEL = 1024
DEPTH = 4
BLOCK = 128
ROPE_THETA = 10000.0
NORM_EPS = 1e-5
HEAD_DIM = 64
A_N_HEADS = 16
A_N_KV = 2
A_WINDOW = 128
A_Q_DIM = 1024
A_KV_DIM = 128
SSM_D_INNER = 2048
SSM_N_HEADS = 32
SSM_N_GROUPS = 8
SSM_HG = 4
SSM_D_STATE = 128
SSM_CONV = 4
SSM_CHUNK = 128
SSM_BC_DIM = 1024
SSM_CONV_DIM = 4096
C_PATTERNS = ((128, 1), (512, 4), (2048, 16))
C_HEADS = 16
ADAM_LR, ADAM_B1, ADAM_B2, ADAM_EPS, ADAM_WD, ADAM_STEP = 0.001, 0.9, 0.999, 1e-08, 0.01, 10

N_DEV = 8
AXES = ("x", "y", "c")
LANES = 128
VMEM_LIMIT = 56 * 1024 * 1024
STREAM_VMEM = 16 * 1024 * 1024
NEG = -1e30

NN = (((1,), (0,)), ((), ()))
NT = (((1,), (1,)), ((), ()))
TN = (((0,), (0,)), ((), ()))
HI = lax.Precision.HIGHEST


def _pick(n, cap, mult=LANES):
    best = None
    for t in range(mult, min(n, cap) + 1, mult):
        if n % t == 0:
            best = t
    return best if best is not None else n


def _params(sem):
    return pltpu.CompilerParams(dimension_semantics=sem, vmem_limit_bytes=VMEM_LIMIT)


def _bf(x):
    return x if x.dtype == BF16 else x.astype(BF16)


def _rot_half(y):
    n = y.shape[-1]
    lane = lax.broadcasted_iota(jnp.int32, y.shape, y.ndim - 1)
    return jnp.where((lane % HEAD_DIM) < HEAD_DIM // 2, -pltpu.roll(y, n - 32, y.ndim - 1), pltpu.roll(y, 32, y.ndim - 1))


def _rope(y, cos, sin, sign):
    reps = y.shape[-1] // LANES
    c = jnp.tile(cos, (1, reps)) if reps > 1 else cos
    s = jnp.tile(sin, (1, reps)) if reps > 1 else sin
    return y * c + sign * (_rot_half(y) * s)


MESH = pl.DeviceIdType.MESH
ANY = pl.BlockSpec(memory_space=pl.ANY)


class _Comm:
    def __init__(self, inputs, out_shapes, aliases, sems, phases):
        self.inputs, self.out_shapes, self.aliases, self.sems, self.phases = inputs, out_shapes, aliases, sems, phases

    @staticmethod
    def merge(comms):
        if len(comms) == 1:
            return comms[0]
        ins, outs, aliases, sems, spans = [], [], {}, [], []
        for c in comms:
            aliases.update({len(ins) + i: len(outs) + j for i, j in c.aliases.items()})
            spans.append((len(ins), len(ins) + len(c.inputs), len(outs), len(outs) + len(c.out_shapes), len(sems),
                          len(sems) + len(c.sems)))
            ins, outs, sems = ins + list(c.inputs), outs + list(c.out_shapes), sems + list(c.sems)
        phases = []
        for f in sorted({f for c in comms for f, _ in c.phases}):
            todo = [(fn, sp) for c, sp in zip(comms, spans) for g, fn in c.phases if g == f]

            def run(cins, couts, csems, todo=todo):
                for fn, (i0, i1, o0, o1, s0, s1) in todo:
                    fn(cins[i0:i1], couts[o0:o1], csems[s0:s1])
            phases.append((f, run))
        return _Comm(ins, outs, aliases, sems, phases)


def _pc(body, args, *, out_shape, grid, in_specs, out_specs, name, sem, scratch_shapes=(), comm=None):
    single = not isinstance(out_shape, (tuple, list))
    outs, ospecs = ([out_shape], [out_specs]) if single else (list(out_shape), list(out_specs))
    unpack = (lambda r: r[0]) if single else (lambda r: tuple(r))
    if comm is None:
        res = pl.pallas_call(body, out_shape=outs, grid=grid, in_specs=list(in_specs), out_specs=ospecs,
                             scratch_shapes=list(scratch_shapes), name=name, compiler_params=_params(sem))(*args)
        return unpack(res)
    n_in, n_out, n_scr = len(in_specs), len(outs), len(scratch_shapes)
    c_in, c_out = len(comm.inputs), len(comm.out_shapes)
    total = math.prod(grid)
    steps = [min(total - 1, int(f * total)) for f, _ in comm.phases[:-1]]

    def wrapped(*refs):
        ins, cins = refs[:n_in], refs[n_in:n_in + c_in]
        o = refs[n_in + c_in:n_in + c_in + n_out]
        couts = refs[n_in + c_in + n_out:n_in + c_in + n_out + c_out]
        rest = refs[n_in + c_in + n_out + c_out:]
        scr, csems = rest[:n_scr], rest[n_scr:]
        step = pl.program_id(0)
        for ax in range(1, len(grid)):
            step = step * grid[ax] + pl.program_id(ax)
        for (_, fn), st in zip(comm.phases[:-1], steps):
            @pl.when(step == st)
            def _(fn=fn):
                fn(cins, couts, csems)
        body(*ins, *o, *scr)

        @pl.when(step == total - 1)
        def _():
            comm.phases[-1][1](cins, couts, csems)

    res = pl.pallas_call(
        wrapped, out_shape=outs + list(comm.out_shapes), grid=grid, in_specs=list(in_specs) + [ANY] * c_in,
        out_specs=ospecs + [ANY] * c_out, scratch_shapes=list(scratch_shapes) + list(comm.sems),
        input_output_aliases={n_in + i: n_out + j for i, j in comm.aliases.items()}, name=name,
        compiler_params=_params(("arbitrary",) * len(grid)),
    )(*args, *comm.inputs)
    return unpack(res[:n_out]), list(res[n_out:])


def _hosted(plan, name, run):
    comm = plan.take(name) if plan is not None else None
    if comm is None:
        return run(None)
    res, extra = run(comm)
    plan.give(name, extra)
    return res


MM_VMEM = 40 * 1024 * 1024
HBM_BYTES_PER_US = 2.5e6
STEP_US = 0.35


def _divisors(n, cands):
    return [c for c in cands if c <= n and n % c == 0] or [n]


def _mm_tiles(M, N, K, sa, sb, out_bytes, extra_bytes, full_rows=False):
    best = None
    for tm in _divisors(M, (2048, 1024, 512, 256)):
        for tn in ([N] if full_rows else _divisors(N, (1024, 640, 512, 256, 128))):
            for tk in _divisors(K, (K, 2048, 1024, 640, 512)):
                nk = K // tk
                vmem = 2 * tm * tk * sa + 2 * tk * tn * sb + tm * tn * (2 * (out_bytes + extra_bytes) + 8 + (4 if nk > 1 else 0))
                if vmem > MM_VMEM:
                    continue
                a_traffic = M * K * sa * (1 if nk == 1 else N // tn)
                b_traffic = K * N * sb * (1 if (nk == 1 and N == tn) else M // tm)
                steps = (M // tm) * (N // tn) * nk
                cost = (a_traffic + b_traffic + M * N * (out_bytes + extra_bytes)) / HBM_BYTES_PER_US + steps * STEP_US
                cost += (M // tm) * (N // tn) * (nk - 1) * tm * tn * 8 / (4 * HBM_BYTES_PER_US)
                if best is None or cost < best[0]:
                    best = (cost, tm, tn, tk)
    assert best is not None, (M, N, K)
    return best[1:]


def _matmul(a, b, *, ta=False, tb=False, out_dtype=F32, bias=None, resid=None, mul=None, mul_scale=1.0,
            relu2=False, rope=None, rope_cols=0, norm_out=None, norm_bwd=None, name="mm", plan=None):
    M = a.shape[1] if ta else a.shape[0]
    K = a.shape[0] if ta else a.shape[1]
    N = b.shape[0] if tb else b.shape[1]
    assert (b.shape[1] if tb else b.shape[0]) == K
    two_out = relu2 or norm_out is not None
    out_bytes = jnp.dtype(out_dtype).itemsize * (2 if relu2 else 1) + (2 if norm_out is not None else 0)
    extra_bytes = (4 if resid is not None else 0) + (mul.dtype.itemsize if mul is not None else 0) + (8 if norm_bwd else 0)
    rows = norm_out is not None or norm_bwd is not None
    tm, tn, tk = _mm_tiles(M, N, K, a.dtype.itemsize, b.dtype.itemsize, out_bytes, extra_bytes, full_rows=rows)
    nk = K // tk
    dims = (((0 if ta else 1,), (1 if tb else 0,)), ((), ()))

    def body(*refs):
        it = iter(refs)
        a_ref, b_ref = next(it), next(it)
        bias_ref = next(it) if bias is not None else None
        resid_ref = next(it) if resid is not None else None
        mul_ref = next(it) if mul is not None else None
        cos_ref, sin_ref = (next(it), next(it)) if rope is not None else (None, None)
        nw_ref = next(it) if rows else None
        h_ref, dres_ref = (next(it), next(it)) if norm_bwd is not None else (None, None)
        o_ref = next(it)
        o2_ref = next(it) if two_out or norm_bwd is not None else None
        acc_ref = next(it) if nk > 1 else None
        k = pl.program_id(2)
        part = lax.dot_general(_bf(a_ref[...]), _bf(b_ref[...]), dims, preferred_element_type=F32)
        if nk > 1:
            @pl.when(k == 0)
            def _():
                acc_ref[...] = part

            @pl.when(k > 0)
            def _():
                acc_ref[...] += part

        @pl.when(k == nk - 1)
        def _():
            y = acc_ref[...] if nk > 1 else part
            if bias_ref is not None:
                y = y + bias_ref[...]
            if rope is not None:
                col = pl.program_id(1) * tn + lax.broadcasted_iota(jnp.int32, y.shape, 1)
                y = jnp.where(col < rope_cols, _rope(y, cos_ref[...], sin_ref[...], 1.0), y)
            if mul_ref is not None:
                y = y * (mul_ref[...].astype(F32) * mul_scale)
            if resid_ref is not None:
                y = y + resid_ref[...]
            if relu2:
                r = jnp.maximum(y, 0.0)
                o_ref[...] = r.astype(o_ref.dtype)
                o2_ref[...] = (r * r).astype(o2_ref.dtype)
            elif norm_bwd is not None:
                x = h_ref[...]
                rstd = lax.rsqrt(jnp.mean(x * x, axis=-1, keepdims=True) + NORM_EPS)
                g = y * nw_ref[...]
                o_ref[...] = dres_ref[...] + rstd * g - x * (rstd * rstd * rstd) * jnp.mean(g * x, axis=-1, keepdims=True)
                dw = jnp.sum(y * x * rstd, axis=0, keepdims=True)
                first = pl.program_id(0) == 0

                @pl.when(first)
                def _():
                    o2_ref[...] = dw

                @pl.when(jnp.logical_not(first))
                def _():
                    o2_ref[...] += dw
            else:
                o_ref[...] = y.astype(o_ref.dtype)
                if norm_out is not None:
                    rstd = lax.rsqrt(jnp.mean(y * y, axis=-1, keepdims=True) + NORM_EPS)
                    o2_ref[...] = (y * rstd * nw_ref[...]).astype(BF16)

    a_spec = pl.BlockSpec((tk, tm), lambda i, j, k: (k, i)) if ta else pl.BlockSpec((tm, tk), lambda i, j, k: (i, k))
    b_spec = pl.BlockSpec((tn, tk), lambda i, j, k: (j, k)) if tb else pl.BlockSpec((tk, tn), lambda i, j, k: (k, j))
    mn_spec = pl.BlockSpec((tm, tn), lambda i, j, k: (i, j))
    in_specs, args = [a_spec, b_spec], [a, b]
    if bias is not None:
        in_specs.append(pl.BlockSpec((1, tn), lambda i, j, k: (0, j)))
        args.append(bias)
    if resid is not None:
        in_specs.append(mn_spec)
        args.append(resid)
    if mul is not None:
        in_specs.append(mn_spec)
        args.append(mul)
    if rope is not None:
        in_specs += [pl.BlockSpec((tm, LANES), lambda i, j, k: (i, 0))] * 2
        args += [rope[0], rope[1]]
    vec_spec = pl.BlockSpec((1, tn), lambda i, j, k: (0, j))
    if rows:
        in_specs.append(vec_spec)
        args.append((norm_out if norm_out is not None else norm_bwd[1]).reshape(1, N))
    if norm_bwd is not None:
        in_specs += [mn_spec, mn_spec]
        args += [norm_bwd[0], norm_bwd[2]]
    out_shape = SDS((M, N), out_dtype)
    out_specs = mn_spec
    if relu2:
        out_shape, out_specs = (out_shape, out_shape), (mn_spec, mn_spec)
    elif norm_out is not None:
        out_shape, out_specs = (out_shape, SDS((M, N), BF16)), (mn_spec, mn_spec)
    elif norm_bwd is not None:
        out_shape, out_specs = (out_shape, SDS((1, N), F32)), (mn_spec, vec_spec)
    sem = ("arbitrary",) * 3 if norm_bwd is not None else ("parallel", "parallel", "arbitrary")
    return _hosted(plan, name, lambda comm: _pc(
        body, args, out_shape=out_shape, grid=(M // tm, N // tn, nk), in_specs=in_specs, out_specs=out_specs,
        scratch_shapes=[pltpu.VMEM((tm, tn), F32)] if nk > 1 else [], name=name, sem=sem, comm=comm))


def _colsum(x, name):
    T, N = x.shape
    tm = _pick(T, 1024, 8)

    def body(x_ref, o_ref):
        s = jnp.sum(x_ref[...].astype(F32), axis=0, keepdims=True)

        @pl.when(pl.program_id(0) == 0)
        def _():
            o_ref[...] = s

        @pl.when(pl.program_id(0) > 0)
        def _():
            o_ref[...] += s

    return pl.pallas_call(
        body, out_shape=SDS((1, N), F32), grid=(T // tm,),
        in_specs=[pl.BlockSpec((tm, N), lambda i: (i, 0))], out_specs=pl.BlockSpec((1, N), lambda i: (0, 0)),
        name=name, compiler_params=_params(("arbitrary",)),
    )(x)


def _rmsnorm_fwd(h, w, name):
    T, D = h.shape
    tm = _pick(T, 512, 8)

    def body(h_ref, w_ref, o_ref):
        x = h_ref[...]
        rstd = lax.rsqrt(jnp.mean(x * x, axis=-1, keepdims=True) + NORM_EPS)
        o_ref[...] = (x * rstd * w_ref[...]).astype(BF16)

    return pl.pallas_call(
        body, out_shape=SDS((T, D), BF16), grid=(T // tm,),
        in_specs=[pl.BlockSpec((tm, D), lambda i: (i, 0)), pl.BlockSpec((1, D), lambda i: (0, 0))],
        out_specs=pl.BlockSpec((tm, D), lambda i: (i, 0)), name=name, compiler_params=_params(("parallel",)),
    )(h, w.reshape(1, D))


def _final_loss(h, target, w):
    T, D = h.shape
    tm = _pick(T, 512, 8)

    def body(h_ref, t_ref, w_ref, dh_ref, dw_ref, loss_ref):
        x = h_ref[...]
        rstd = lax.rsqrt(jnp.mean(x * x, axis=-1, keepdims=True) + NORM_EPS)
        xn = x * rstd
        err = xn * w_ref[...] - t_ref[...]
        part = 0.5 * jnp.sum(jnp.mean(err * err, axis=-1, keepdims=True), axis=0, keepdims=True)
        dy = err * (1.0 / D)
        g = dy * w_ref[...]
        dh_ref[...] = rstd * g - x * (rstd * rstd * rstd) * jnp.mean(g * x, axis=-1, keepdims=True)
        dw = jnp.sum(dy * xn, axis=0, keepdims=True)
        lp = jnp.broadcast_to(part, (1, LANES))

        @pl.when(pl.program_id(0) == 0)
        def _():
            dw_ref[...] = dw
            loss_ref[...] = lp

        @pl.when(pl.program_id(0) > 0)
        def _():
            dw_ref[...] += dw
            loss_ref[...] += lp

    row = pl.BlockSpec((tm, D), lambda i: (i, 0))
    vec = pl.BlockSpec((1, D), lambda i: (0, 0))
    return pl.pallas_call(
        body, out_shape=(SDS((T, D), F32), SDS((1, D), F32), SDS((1, LANES), F32)), grid=(T // tm,),
        in_specs=[row, row, vec], out_specs=(row, vec, pl.BlockSpec((1, LANES), lambda i: (0, 0))),
        name="final_loss", compiler_params=_params(("arbitrary",)),
    )(h, target, w.reshape(1, D))


def _band_mask(i_blk, max_dist, first_ok):
    qi = lax.broadcasted_iota(jnp.int32, (BLOCK, 2 * BLOCK), 0)
    kj = lax.broadcasted_iota(jnp.int32, (BLOCK, 2 * BLOCK), 1)
    dist = qi + BLOCK - kj
    ok = (dist >= 0) & (dist <= max_dist)
    return ok & ((kj >= BLOCK) | first_ok)


def _pair(t, i):
    return t[:, LANES * i:LANES * (i + 1)]


def _low_half(shape):
    return lax.broadcasted_iota(jnp.int32, shape, len(shape) - 1) < HEAD_DIM


def _stack_heads(t):
    lo = _low_half(t.shape)
    z = jnp.zeros_like(t)
    return jnp.concatenate([jnp.where(lo, t, z), jnp.where(lo, z, t)], axis=0)


def _swap_halves(t):
    return jnp.concatenate([t[:, HEAD_DIM:], t[:, :HEAD_DIM]], axis=1)


def _kv_operand(kv, kv_swapped, h0, n_kv, n_heads):
    R = n_heads // n_kv
    if R == 1:
        return _pair(kv, h0 // 2)
    assert kv.shape[1] == LANES and R % 2 == 0, "grouped queries: one 128-lane tile of kv heads, both heads of a pair in one group"
    g = h0 // R
    t, ts = _pair(kv, g // 2), _pair(kv_swapped, g // 2)
    lo = _low_half(t.shape)
    return jnp.where(lo, t, ts) if g % 2 == 0 else jnp.where(lo, ts, t)


def _lane_place(cols):
    m = cols[0].shape[0]
    lane = lax.broadcasted_iota(jnp.int32, (m, LANES), 1)
    out = jnp.zeros((m, LANES), F32)
    for h, c in enumerate(cols):
        out = jnp.where(lane == h, c, out)
    return out


def _attn_specs(B, S, d, C, n_heads, n_kv, q_col, k_col, v_col):
    kvw = n_kv * HEAD_DIM
    qw = n_heads * HEAD_DIM
    cq, ck = (C // qw if d > 1 else 0), (C // kvw if d > 1 else 0)
    q_spec = pl.BlockSpec((1, BLOCK, qw), lambda b, r, i: (b, i, r * cq + q_col // qw))
    kc = pl.BlockSpec((1, BLOCK, kvw), lambda b, r, i: (b, i, r * ck + k_col // kvw))
    kp = pl.BlockSpec((1, BLOCK, kvw), lambda b, r, i: (b, jnp.maximum(i - 1, 0), r * ck + k_col // kvw))
    vc = pl.BlockSpec((1, BLOCK, kvw), lambda b, r, i: (b, i, r * ck + v_col // kvw))
    vp = pl.BlockSpec((1, BLOCK, kvw), lambda b, r, i: (b, jnp.maximum(i - 1, 0), r * ck + v_col // kvw))
    return q_spec, kp, kc, vp, vc


def _attn_fwd(qkv, B, S, d, *, n_heads, n_kv, q_col, k_col, v_col, max_dist, sinks, name, plan=None):
    C = qkv.shape[1]
    Ls = S // d
    nb = Ls // BLOCK
    qw = n_heads * HEAD_DIM
    R = n_heads // n_kv
    qkv3 = qkv.reshape(B, Ls, d * C)
    scale = HEAD_DIM ** -0.5

    def body(*refs):
        if sinks is not None:
            sink_ref, q_ref, kp_ref, kc_ref, vp_ref, vc_ref, o_ref, lse_ref = refs
        else:
            q_ref, kp_ref, kc_ref, vp_ref, vc_ref, o_ref, lse_ref = refs
        i = pl.program_id(2)
        mask1 = _band_mask(i, max_dist, i > 0)
        mask = jnp.concatenate([mask1, mask1], axis=0)
        q = q_ref[0]
        kk = jnp.concatenate([kp_ref[0], kc_ref[0]], axis=0)
        vv = jnp.concatenate([vp_ref[0], vc_ref[0]], axis=0)
        kks, vvs = (_swap_halves(kk), _swap_halves(vv)) if R > 1 else (None, None)
        lo = _low_half((BLOCK, LANES))
        top = lax.broadcasted_iota(jnp.int32, (2 * BLOCK, 1), 0) < BLOCK
        lses, tiles = [], []
        for t in range(n_heads // 2):
            k2 = _kv_operand(kk, kks, 2 * t, n_kv, n_heads)
            v2 = _kv_operand(vv, vvs, 2 * t, n_kv, n_heads)
            s = lax.dot_general(_stack_heads(_pair(q, t)), k2, NT, preferred_element_type=F32) * scale
            s = jnp.where(mask, s, NEG)
            m = jnp.max(s, axis=-1, keepdims=True)
            if sinks is not None:
                sk = jnp.where(top, sink_ref[2 * t], sink_ref[2 * t + 1])
                m = jnp.maximum(m, sk)
            p = jnp.exp(s - m)
            den = jnp.sum(p, axis=-1, keepdims=True)
            if sinks is not None:
                den = den + jnp.exp(sk - m)
            lse2 = m + jnp.log(den)
            o2 = jnp.dot((p / den).astype(BF16), v2, preferred_element_type=F32)
            tiles.append(jnp.where(lo, o2[:BLOCK], o2[BLOCK:]))
            lses += [lse2[:BLOCK], lse2[BLOCK:]]
        o_ref[0] = jnp.concatenate(tiles, axis=-1)
        lse_ref[0] = _lane_place(lses)

    specs = list(_attn_specs(B, S, d, C, n_heads, n_kv, q_col, k_col, v_col))
    args = [qkv3] * 5
    if sinks is not None:
        specs = [pl.BlockSpec(memory_space=pltpu.SMEM)] + specs
        args = [sinks] + args
    o3, lse3 = _hosted(plan, name, lambda comm: _pc(
        body, args, out_shape=(SDS((B, Ls, d * qw), F32), SDS((B, Ls, d * LANES), F32)), grid=(B, d, nb), in_specs=specs,
        out_specs=(pl.BlockSpec((1, BLOCK, qw), lambda b, r, i: (b, i, r)), pl.BlockSpec((1, BLOCK, LANES), lambda b, r, i: (b, i, r))),
        name=name, sem=("parallel", "parallel", "parallel"), comm=comm))
    return o3.reshape(B * S, qw), lse3.reshape(B * S, LANES)


def _attn_dq(qkv, do, lse, delta, cos, sin, B, S, d, *, n_heads, n_kv, q_col, k_col, v_col, max_dist, name, plan=None):
    C = qkv.shape[1]
    Ls = S // d
    nb = Ls // BLOCK
    qw = n_heads * HEAD_DIM
    R = n_heads // n_kv
    scale = HEAD_DIM ** -0.5

    def body(q_ref, kp_ref, kc_ref, vp_ref, vc_ref, do_ref, lse_ref, dl_ref, cos_ref, sin_ref, dq_ref):
        i = pl.program_id(2)
        mask1 = _band_mask(i, max_dist, i > 0)
        mask = jnp.concatenate([mask1, mask1], axis=0)
        q = q_ref[0]
        do_ = do_ref[0]
        kk = jnp.concatenate([kp_ref[0], kc_ref[0]], axis=0)
        vv = jnp.concatenate([vp_ref[0], vc_ref[0]], axis=0)
        kks, vvs = (_swap_halves(kk), _swap_halves(vv)) if R > 1 else (None, None)
        lo = _low_half((BLOCK, LANES))
        lse_t, dl_t = lse_ref[0], dl_ref[0]
        tiles = []
        for t in range(n_heads // 2):
            k2 = _kv_operand(kk, kks, 2 * t, n_kv, n_heads)
            v2 = _kv_operand(vv, vvs, 2 * t, n_kv, n_heads)
            lse2 = jnp.concatenate([lse_t[:, 2 * t:2 * t + 1], lse_t[:, 2 * t + 1:2 * t + 2]], axis=0)
            dl2 = jnp.concatenate([dl_t[:, 2 * t:2 * t + 1], dl_t[:, 2 * t + 1:2 * t + 2]], axis=0)
            s = lax.dot_general(_stack_heads(_pair(q, t)), k2, NT, preferred_element_type=F32) * scale
            p = jnp.where(mask, jnp.exp(s - lse2), 0.0)
            dp = lax.dot_general(_stack_heads(_pair(do_, t)), v2, NT, preferred_element_type=F32)
            ds = p * (dp - dl2)
            dq2 = jnp.dot(ds.astype(BF16), k2, preferred_element_type=F32) * scale
            tiles.append(jnp.where(lo, dq2[:BLOCK], dq2[BLOCK:]))
        dq = jnp.concatenate(tiles, axis=-1)
        dq_ref[0] = _rope(dq, cos_ref[0], sin_ref[0], -1.0).astype(BF16)

    qs, kp, kc, vp, vc = _attn_specs(B, S, d, C, n_heads, n_kv, q_col, k_col, v_col)
    row_q = pl.BlockSpec((1, BLOCK, qw), lambda b, r, i: (b, i, r))
    row_l = pl.BlockSpec((1, BLOCK, LANES), lambda b, r, i: (b, i, r))
    qkv3 = qkv.reshape(B, Ls, d * C)
    v3 = lambda t, w: t.reshape(B, Ls, d * w)
    args = (qkv3, qkv3, qkv3, qkv3, qkv3, v3(do, qw), v3(lse, LANES), v3(delta, LANES), v3(cos, LANES), v3(sin, LANES))
    dq3 = _hosted(plan, name, lambda comm: _pc(
        body, args, out_shape=SDS((B, Ls, d * qw), BF16), grid=(B, d, nb),
        in_specs=[qs, kp, kc, vp, vc, row_q, row_l, row_l, row_l, row_l], out_specs=row_q,
        name=name, sem=("parallel", "parallel", "parallel"), comm=comm))
    return dq3.reshape(B * S, qw)


def _attn_dkv(qkv, do, lse, delta, cos, sin, B, S, d, *, n_heads, n_kv, q_col, k_col, v_col, max_dist, name, plan=None):
    C = qkv.shape[1]
    Ls = S // d
    nb = Ls // BLOCK
    qw = n_heads * HEAD_DIM
    kvw = n_kv * HEAD_DIM
    R = n_heads // n_kv
    scale = HEAD_DIM ** -0.5
    cq, ck = (C // qw if d > 1 else 0), (C // kvw if d > 1 else 0)

    def body(k_ref, v_ref, q0_ref, q1_ref, do0_ref, do1_ref, lse0_ref, lse1_ref, dl0_ref, dl1_ref, cos_ref, sin_ref,
             dk_ref, dv_ref):
        j = pl.program_id(2)
        kj = lax.broadcasted_iota(jnp.int32, (BLOCK, BLOCK), 0)
        qi = lax.broadcasted_iota(jnp.int32, (BLOCK, BLOCK), 1)
        dist0 = qi - kj
        dist1 = qi + BLOCK - kj
        mask0 = (dist0 >= 0) & (dist0 <= max_dist)
        mask1 = (dist1 <= max_dist) & (j + 1 < nb)
        kb, vb = k_ref[0], v_ref[0]
        kbs, vbs = (_swap_halves(kb), _swap_halves(vb)) if R > 1 else (None, None)
        sides = ((q0_ref[0], do0_ref[0], lse0_ref[0].T, dl0_ref[0].T, mask0), (q1_ref[0], do1_ref[0], lse1_ref[0].T, dl1_ref[0].T, mask1))
        n_acc = n_kv if R > 1 else n_kv // 2
        dks = [jnp.zeros((BLOCK, LANES), F32) for _ in range(n_acc)]
        dvs = [jnp.zeros((BLOCK, LANES), F32) for _ in range(n_acc)]
        for t in range(n_heads // 2):
            k2 = _kv_operand(kb, kbs, 2 * t, n_kv, n_heads)
            v2 = _kv_operand(vb, vbs, 2 * t, n_kv, n_heads)
            a = (2 * t) // R if R > 1 else t
            for (q, do_, lse_r, dl_r, mask) in sides:
                q2, do2 = _stack_heads(_pair(q, t)), _stack_heads(_pair(do_, t))
                s = lax.dot_general(k2, q2, NT, preferred_element_type=F32) * scale
                dp = lax.dot_general(v2, do2, NT, preferred_element_type=F32)
                ps, dss = [], []
                for half in (0, 1):
                    h = 2 * t + half
                    sl = slice(BLOCK * half, BLOCK * (half + 1))
                    p = jnp.where(mask, jnp.exp(s[:, sl] - lse_r[h:h + 1, :]), 0.0)
                    ps.append(p)
                    dss.append(p * (dp[:, sl] - dl_r[h:h + 1, :]))
                dvs[a] = dvs[a] + jnp.dot(jnp.concatenate(ps, axis=1).astype(BF16), do2, preferred_element_type=F32)
                dks[a] = dks[a] + jnp.dot(jnp.concatenate(dss, axis=1).astype(BF16), q2, preferred_element_type=F32)
        if R > 1:
            lo = _low_half((BLOCK, LANES))
            fold = lambda x: x + pltpu.roll(x, HEAD_DIM, 1)
            dks = [jnp.where(lo, fold(dks[2 * t]), fold(dks[2 * t + 1])) for t in range(n_kv // 2)]
            dvs = [jnp.where(lo, fold(dvs[2 * t]), fold(dvs[2 * t + 1])) for t in range(n_kv // 2)]
        dk_t = jnp.concatenate(dks, axis=-1) * scale
        dk_ref[0] = _rope(dk_t, cos_ref[0], sin_ref[0], -1.0).astype(BF16)
        dv_ref[0] = jnp.concatenate(dvs, axis=-1).astype(BF16)

    nxt = lambda j: jnp.minimum(j + 1, nb - 1)
    k_spec = pl.BlockSpec((1, BLOCK, kvw), lambda b, r, j: (b, j, r * ck + k_col // kvw))
    v_spec = pl.BlockSpec((1, BLOCK, kvw), lambda b, r, j: (b, j, r * ck + v_col // kvw))
    q0 = pl.BlockSpec((1, BLOCK, qw), lambda b, r, j: (b, j, r * cq + q_col // qw))
    q1 = pl.BlockSpec((1, BLOCK, qw), lambda b, r, j: (b, nxt(j), r * cq + q_col // qw))
    w0 = lambda w: pl.BlockSpec((1, BLOCK, w), lambda b, r, j: (b, j, r))
    w1 = lambda w: pl.BlockSpec((1, BLOCK, w), lambda b, r, j: (b, nxt(j), r))
    qkv3 = qkv.reshape(B, Ls, d * C)
    v3 = lambda t, w: t.reshape(B, Ls, d * w)
    do3, lse3, dl3 = v3(do, qw), v3(lse, LANES), v3(delta, LANES)
    args = (qkv3, qkv3, qkv3, qkv3, do3, do3, lse3, lse3, dl3, dl3, v3(cos, LANES), v3(sin, LANES))
    dk3, dv3 = _hosted(plan, name, lambda comm: _pc(
        body, args, out_shape=(SDS((B, Ls, d * kvw), BF16), SDS((B, Ls, d * kvw), BF16)), grid=(B, d, nb),
        in_specs=[k_spec, v_spec, q0, q1, w0(qw), w1(qw), w0(LANES), w1(LANES), w0(LANES), w1(LANES), w0(LANES), w0(LANES)],
        out_specs=(w0(kvw), w0(kvw)), name=name, sem=("parallel", "parallel", "parallel"), comm=comm))
    return dk3.reshape(B * S, kvw), dv3.reshape(B * S, kvw)


def _head_expand():
    r = lax.broadcasted_iota(jnp.int32, (LANES, C_HEADS * HEAD_DIM), 0)
    c = lax.broadcasted_iota(jnp.int32, (LANES, C_HEADS * HEAD_DIM), 1)
    return jnp.where(c // HEAD_DIM == r, 1.0, 0.0).astype(F32)


def _delta(do, o, lse=None, sinks_row=None, name="delta"):
    T, W = do.shape
    tm = _pick(T, 512, 8)
    with_sink = sinks_row is not None

    def body(*refs):
        if with_sink:
            do_ref, o_ref, lse_ref, sk_ref, dl_ref, dob_ref, ds_ref = refs
        else:
            do_ref, o_ref, dl_ref, dob_ref = refs
        do_ = do_ref[...]
        dl = lax.dot_general(do_ * o_ref[...], _head_expand(), NT, preferred_element_type=F32, precision=HI)
        dl_ref[...] = dl
        dob_ref[...] = do_.astype(BF16)
        if with_sink:
            lane = lax.broadcasted_iota(jnp.int32, dl.shape, 1)
            contrib = jnp.where(lane < A_N_HEADS, -jnp.exp(sk_ref[...] - lse_ref[...]) * dl, 0.0)
            part = jnp.sum(contrib, axis=0, keepdims=True)

            @pl.when(pl.program_id(0) == 0)
            def _():
                ds_ref[...] = part

            @pl.when(pl.program_id(0) > 0)
            def _():
                ds_ref[...] += part

    row_w = pl.BlockSpec((tm, W), lambda i: (i, 0))
    row_l = pl.BlockSpec((tm, LANES), lambda i: (i, 0))
    vec_l = pl.BlockSpec((1, LANES), lambda i: (0, 0))
    if with_sink:
        return pl.pallas_call(
            body, out_shape=(SDS((T, LANES), F32), SDS((T, W), BF16), SDS((1, LANES), F32)), grid=(T // tm,),
            in_specs=[row_w, row_w, row_l, vec_l], out_specs=(row_l, row_w, vec_l), name=name,
            compiler_params=_params(("arbitrary",)),
        )(do, o, lse, sinks_row)
    return pl.pallas_call(
        body, out_shape=(SDS((T, LANES), F32), SDS((T, W), BF16)), grid=(T // tm,),
        in_specs=[row_w, row_w], out_specs=(row_l, row_w), name=name, compiler_params=_params(("parallel",)),
    )(do, o)


def _merge(os_, lses):
    T, W = os_[0].shape
    tm = _pick(T, 512, 8)

    def body(o0, o1, o2, l0, l1, l2, o_ref, lse_ref):
        ls = [l0[...], l1[...], l2[...]]
        m = jnp.maximum(jnp.maximum(ls[0], ls[1]), ls[2])
        ws = [jnp.exp(l - m) for l in ls]
        tot = ws[0] + ws[1] + ws[2]
        lse_ref[...] = m + jnp.log(tot)
        e = _head_expand()
        acc = jnp.zeros((tm, W), F32)
        for w, o in zip(ws, (o0, o1, o2)):
            acc = acc + jnp.dot(w / tot, e, preferred_element_type=F32, precision=HI) * o[...]
        o_ref[...] = acc

    row_w = pl.BlockSpec((tm, W), lambda i: (i, 0))
    row_l = pl.BlockSpec((tm, LANES), lambda i: (i, 0))
    return pl.pallas_call(
        body, out_shape=(SDS((T, W), F32), SDS((T, LANES), F32)), grid=(T // tm,),
        in_specs=[row_w] * 3 + [row_l] * 3, out_specs=(row_w, row_l), name="c_merge", compiler_params=_params(("parallel",)),
    )(*os_, *lses)


CONV_TC = 256


def _conv_pre(x, w, bias):
    row = lax.broadcasted_iota(jnp.int32, x.shape, 0)
    acc = x * w[SSM_CONV - 1:SSM_CONV, :] + bias
    for k in range(1, SSM_CONV):
        acc = acc + jnp.where(row >= k, pltpu.roll(x, k, 0), 0.0) * w[SSM_CONV - 1 - k:SSM_CONV - k, :]
    return acc


def _conv_fwd(zx3, w, bias):
    B, S, _ = zx3.shape
    off = SSM_D_INNER // CONV_TC

    def body(x_ref, w_ref, b_ref, o_ref):
        v = _conv_pre(x_ref[0], w_ref[...], b_ref[...])
        o_ref[0] = v * jax.nn.sigmoid(v)

    return pl.pallas_call(
        body, out_shape=SDS((B, S, SSM_CONV_DIM), F32), grid=(B, SSM_CONV_DIM // CONV_TC),
        in_specs=[pl.BlockSpec((1, S, CONV_TC), lambda b, j: (b, 0, j + off)),
                  pl.BlockSpec((SSM_CONV, CONV_TC), lambda b, j: (0, j)), pl.BlockSpec((1, CONV_TC), lambda b, j: (0, j))],
        out_specs=pl.BlockSpec((1, S, CONV_TC), lambda b, j: (b, 0, j)), name="b_conv_fwd",
        compiler_params=_params(("parallel", "parallel")),
    )(zx3, w, bias)


def _conv_bwd(zx3, dxc, w, bias, col0, name):
    B, S, n = dxc.shape
    tc = _pick(n, CONV_TC)
    off_x = (SSM_D_INNER + col0) // tc
    off_w = col0 // tc

    def body(x_ref, d_ref, w_ref, b_ref, dx_ref, dw_ref, db_ref):
        x = x_ref[0]
        wv = w_ref[...]
        v = _conv_pre(x, wv, b_ref[...])
        sg = jax.nn.sigmoid(v)
        dc = d_ref[0] * (sg * (1.0 + v * (1.0 - sg)))
        row = lax.broadcasted_iota(jnp.int32, x.shape, 0)
        dx = dc * wv[SSM_CONV - 1:SSM_CONV, :]
        dws = [jnp.sum(dc * x, axis=0, keepdims=True)]
        for k in range(1, SSM_CONV):
            dx = dx + jnp.where(row < S - k, pltpu.roll(dc, S - k, 0), 0.0) * wv[SSM_CONV - 1 - k:SSM_CONV - k, :]
            dws.append(jnp.sum(dc * jnp.where(row >= k, pltpu.roll(x, k, 0), 0.0), axis=0, keepdims=True))
        dx_ref[0] = dx.astype(BF16)
        ridx = lax.broadcasted_iota(jnp.int32, (SSM_CONV, tc), 0)
        dw = jnp.zeros((SSM_CONV, tc), F32)
        for k in range(SSM_CONV):
            dw = jnp.where(ridx == SSM_CONV - 1 - k, dws[k], dw)
        db = jnp.sum(dc, axis=0, keepdims=True)

        @pl.when(pl.program_id(1) == 0)
        def _():
            dw_ref[...] = dw
            db_ref[...] = db

        @pl.when(pl.program_id(1) > 0)
        def _():
            dw_ref[...] += dw
            db_ref[...] += db

    return pl.pallas_call(
        body, out_shape=(SDS((B, S, n), BF16), SDS((SSM_CONV, n), F32), SDS((1, n), F32)), grid=(n // tc, B),
        in_specs=[pl.BlockSpec((1, S, tc), lambda j, b: (b, 0, j + off_x)), pl.BlockSpec((1, S, tc), lambda j, b: (b, 0, j)),
                  pl.BlockSpec((SSM_CONV, tc), lambda j, b: (0, j + off_w)), pl.BlockSpec((1, tc), lambda j, b: (0, j + off_w))],
        out_specs=(pl.BlockSpec((1, S, tc), lambda j, b: (b, 0, j)), pl.BlockSpec((SSM_CONV, tc), lambda j, b: (0, j)),
                   pl.BlockSpec((1, tc), lambda j, b: (0, j))),
        name=name, compiler_params=_params(("parallel", "arbitrary")),
    )(zx3, dxc, w, bias)


def _ssd_common(x, Bm, Cm, dtc_raw, dtr_raw, pr, pc):
    Q = SSM_CHUNK
    zc = dtc_raw + pr[0:1, :]
    dt_c = jax.nn.softplus(zc)
    dt_r = jax.nn.softplus(dtr_raw + pc[:, 0:1])
    A_r = -jnp.exp(pr[1:2, :])
    A_c = -jnp.exp(pc[:, 1:2])
    row = lax.broadcasted_iota(jnp.int32, (Q, Q), 0)
    col = lax.broadcasted_iota(jnp.int32, (Q, Q), 1)
    tril = jnp.where(row >= col, 1.0, 0.0).astype(F32)
    cs_c = jnp.dot(tril, dt_c * A_r, preferred_element_type=F32, precision=HI)
    cs_r = lax.dot_general(dt_r * A_c, tril, NT, preferred_element_type=F32, precision=HI)
    return zc, dt_c, A_r, cs_c, cs_r, row, col, tril


def _ssd_fwd(xc3, dtc, dtr, prow, pcol, plan=None):
    B, S, _ = xc3.shape
    Q, G, HG, P, N = SSM_CHUNK, SSM_N_GROUPS, SSM_HG, HEAD_DIM, SSM_D_STATE
    nc = S // Q
    xw = HG * P

    def body(x_ref, b_ref, c_ref, dtc_ref, dtr_ref, pr_ref, pc_ref, y_ref, st_ref, state):
        c = pl.program_id(2)

        @pl.when(c == 0)
        def _():
            state[...] = jnp.zeros_like(state)

        x, Bm, Cm = x_ref[0], b_ref[0], c_ref[0]
        pr = pr_ref[0]
        _, dt_c, _, cs_c, cs_r, row, col, _ = _ssd_common(x, Bm, Cm, dtc_ref[0, 0], dtr_ref[0, 0], pr, pc_ref[0])
        Bb, Cb = Bm.astype(BF16), Cm.astype(BF16)
        CB = lax.dot_general(Cb, Bb, NT, preferred_element_type=F32)
        ys = []
        for hg in range(HG):
            xh = x[:, P * hg:P * (hg + 1)]
            xt = xh * dt_c[:, hg:hg + 1]
            csc, csr = cs_c[:, hg:hg + 1], cs_r[hg:hg + 1, :]
            L = jnp.where(row >= col, jnp.exp(jnp.minimum(csc - csr, 0.0)), 0.0)
            ydiag = jnp.dot((CB * L).astype(BF16), xt.astype(BF16), preferred_element_type=F32)
            Sh = state[hg]
            yoff = lax.dot_general(Cb, Sh.astype(BF16), NT, preferred_element_type=F32) * jnp.exp(csc)
            ys.append(ydiag + yoff + pr[2:3, hg:hg + 1] * xh)
            st_ref[0, 0, 0, P * hg:P * (hg + 1), :] = Sh
            csq = csc[Q - 1:Q, :]
            upd = lax.dot_general((xt * jnp.exp(csq - csc)).astype(BF16), Bb, TN, preferred_element_type=F32)
            state[hg] = Sh * jnp.exp(csq) + upd
        y_ref[0] = jnp.concatenate([jnp.concatenate(ys[0:2], axis=-1), jnp.concatenate(ys[2:4], axis=-1)], axis=-1)

    bo, co = SSM_D_INNER // N, (SSM_D_INNER + SSM_BC_DIM) // N
    return _hosted(plan, "b_ssd_fwd", lambda comm: _pc(
        body, (xc3, xc3, xc3, dtc, dtr, prow, pcol),
        out_shape=(SDS((B, S, SSM_D_INNER), F32), SDS((B, G, nc, xw, N), F32)), grid=(G, B, nc),
        in_specs=[pl.BlockSpec((1, Q, xw), lambda g, b, c: (b, c, g)), pl.BlockSpec((1, Q, N), lambda g, b, c: (b, c, bo + g)),
                  pl.BlockSpec((1, Q, N), lambda g, b, c: (b, c, co + g)), pl.BlockSpec((1, 1, Q, HG), lambda g, b, c: (b, g, c, 0)),
                  pl.BlockSpec((1, 1, HG, Q), lambda g, b, c: (b, g, 0, c)), pl.BlockSpec((1, 3, HG), lambda g, b, c: (g, 0, 0)),
                  pl.BlockSpec((1, HG, 3), lambda g, b, c: (g, 0, 0))],
        out_specs=(pl.BlockSpec((1, Q, xw), lambda g, b, c: (b, c, g)), pl.BlockSpec((1, 1, 1, xw, N), lambda g, b, c: (b, g, c, 0, 0))),
        scratch_shapes=[pltpu.VMEM((HG, P, N), F32)], name="b_ssd_fwd", sem=("parallel", "arbitrary", "arbitrary"), comm=comm))


def _ssd_bwd(xc3, dtc, dtr, prow, pcol, states, dy3, plan=None):
    B, S, _ = xc3.shape
    Q, G, HG, P, N = SSM_CHUNK, SSM_N_GROUPS, SSM_HG, HEAD_DIM, SSM_D_STATE
    nc = S // Q
    xw = HG * P

    def body(x_ref, b_ref, c_ref, dtc_ref, dtr_ref, pr_ref, pc_ref, st_ref, dy_ref,
             dx_ref, db_ref, dc_ref, ddt_ref, dpar_ref, dstate):
        bi, ci = pl.program_id(1), pl.program_id(2)

        @pl.when(ci == 0)
        def _():
            dstate[...] = jnp.zeros_like(dstate)

        x, Bm, Cm, dy = x_ref[0], b_ref[0], c_ref[0], dy_ref[0]
        pr = pr_ref[0]
        zc, dt_c, A_r, cs_c, cs_r, row, col, tril = _ssd_common(x, Bm, Cm, dtc_ref[0, 0], dtr_ref[0, 0], pr, pc_ref[0])
        Bb, Cb = Bm.astype(BF16), Cm.astype(BF16)
        CB = lax.dot_general(Cb, Bb, NT, preferred_element_type=F32)
        CBt = lax.dot_general(Bb, Cb, NT, preferred_element_type=F32)
        lane4 = lax.broadcasted_iota(jnp.int32, (Q, HG), 1)
        lane4r = lax.broadcasted_iota(jnp.int32, (1, HG), 1)
        rowq = lax.broadcasted_iota(jnp.int32, (Q, 1), 0)
        dB = jnp.zeros((Q, N), F32)
        dC = jnp.zeros((Q, N), F32)
        dcs4 = jnp.zeros((Q, HG), F32)
        dtx4 = jnp.zeros((Q, HG), F32)
        dD4 = jnp.zeros((1, HG), F32)
        dxts, xhs, dyhs = [], [], []
        for hg in range(HG):
            xh = x[:, P * hg:P * (hg + 1)]
            dyh = dy[:, P * hg:P * (hg + 1)]
            xt = xh * dt_c[:, hg:hg + 1]
            xtb, dyb = xt.astype(BF16), dyh.astype(BF16)
            csc, csr = cs_c[:, hg:hg + 1], cs_r[hg:hg + 1, :]
            L = jnp.where(row >= col, jnp.exp(jnp.minimum(csc - csr, 0.0)), 0.0)
            Lt = jnp.where(col >= row, jnp.exp(jnp.minimum(csr - csc, 0.0)), 0.0)
            M, Mt = CB * L, CBt * Lt
            Sh = st_ref[0, 0, 0, P * hg:P * (hg + 1), :]
            dSh = dstate[hg]
            Shb, dShb = Sh.astype(BF16), dSh.astype(BF16)
            ecs = jnp.exp(csc)
            csq = csc[Q - 1:Q, :]
            dec = jnp.exp(csq - csc)
            dxt = jnp.dot(Mt.astype(BF16), dyb, preferred_element_type=F32)
            dxt = dxt + lax.dot_general(Bb, dShb, NT, preferred_element_type=F32) * dec
            Gm = lax.dot_general(dyb, xtb, NT, preferred_element_type=F32)
            Gt = lax.dot_general(xtb, dyb, NT, preferred_element_type=F32)
            dC = dC + jnp.dot((Gm * L).astype(BF16), Bb, preferred_element_type=F32)
            dB = dB + jnp.dot((Gt * Lt).astype(BF16), Cb, preferred_element_type=F32)
            dC = dC + jnp.dot(dyb, Shb, preferred_element_type=F32) * ecs
            dBst = jnp.dot(xtb, dShb, preferred_element_type=F32) * dec
            dB = dB + dBst
            dcs = jnp.sum(Gm * M, axis=1, keepdims=True) - jnp.sum(Gt * Mt, axis=1, keepdims=True)
            yoff = lax.dot_general(Cb, Shb, NT, preferred_element_type=F32) * ecs
            dcs = dcs + jnp.sum(yoff * dyh, axis=1, keepdims=True)
            r = jnp.sum(dBst * Bm, axis=1, keepdims=True)
            dcs = dcs - r
            extra = jnp.sum(r, axis=0, keepdims=True) + jnp.exp(csq) * jnp.sum(
                jnp.sum(dSh * Sh, axis=1, keepdims=True), axis=0, keepdims=True)
            dcs = dcs + jnp.where(rowq == Q - 1, extra, 0.0)
            dcs4 = jnp.where(lane4 == hg, dcs, dcs4)
            dtx4 = jnp.where(lane4 == hg, jnp.sum(dxt * xh, axis=1, keepdims=True), dtx4)
            dD4 = jnp.where(lane4r == hg, jnp.sum(jnp.sum(dyh * xh, axis=1, keepdims=True), axis=0, keepdims=True), dD4)
            dstate[hg] = dSh * jnp.exp(csq) + lax.dot_general((dyh * ecs).astype(BF16), Cb, TN, preferred_element_type=F32)
            dxts.append(dxt)
            xhs.append(xh)
            dyhs.append(dyh)
        da4 = lax.dot_general(tril, dcs4, TN, preferred_element_type=F32, precision=HI)
        ddt4 = da4 * A_r + dtx4
        ddtraw = ddt4 * jax.nn.sigmoid(zc)
        ddt_ref[0, 0] = ddtraw
        dxs = [dxts[hg] * dt_c[:, hg:hg + 1] + pr[2:3, hg:hg + 1] * dyhs[hg] for hg in range(HG)]
        dx_ref[0] = jnp.concatenate([jnp.concatenate(dxs[0:2], axis=-1), jnp.concatenate(dxs[2:4], axis=-1)], axis=-1)
        db_ref[0] = dB
        dc_ref[0] = dC
        d_bias = jnp.sum(ddtraw, axis=0, keepdims=True)
        d_alog = jnp.sum(da4 * dt_c, axis=0, keepdims=True) * A_r
        r3 = lax.broadcasted_iota(jnp.int32, (3, HG), 0)
        dpar = jnp.where(r3 == 0, d_bias, jnp.where(r3 == 1, d_alog, dD4))
        first = (bi == 0) & (ci == 0)

        @pl.when(first)
        def _():
            dpar_ref[0] = dpar

        @pl.when(jnp.logical_not(first))
        def _():
            dpar_ref[0] += dpar

    rc = lambda c: nc - 1 - c
    bo, co = SSM_D_INNER // N, (SSM_D_INNER + SSM_BC_DIM) // N
    return _hosted(plan, "b_ssd_bwd", lambda comm: _pc(
        body, (xc3, xc3, xc3, dtc, dtr, prow, pcol, states, dy3),
        out_shape=(SDS((B, S, SSM_D_INNER), F32), SDS((B, S, SSM_BC_DIM), F32), SDS((B, S, SSM_BC_DIM), F32),
                   SDS((B, G, S, HG), F32), SDS((G, 3, HG), F32)),
        grid=(G, B, nc),
        in_specs=[pl.BlockSpec((1, Q, xw), lambda g, b, c: (b, rc(c), g)), pl.BlockSpec((1, Q, N), lambda g, b, c: (b, rc(c), bo + g)),
                  pl.BlockSpec((1, Q, N), lambda g, b, c: (b, rc(c), co + g)), pl.BlockSpec((1, 1, Q, HG), lambda g, b, c: (b, g, rc(c), 0)),
                  pl.BlockSpec((1, 1, HG, Q), lambda g, b, c: (b, g, 0, rc(c))), pl.BlockSpec((1, 3, HG), lambda g, b, c: (g, 0, 0)),
                  pl.BlockSpec((1, HG, 3), lambda g, b, c: (g, 0, 0)),
                  pl.BlockSpec((1, 1, 1, xw, N), lambda g, b, c: (b, g, rc(c), 0, 0)), pl.BlockSpec((1, Q, xw), lambda g, b, c: (b, rc(c), g))],
        out_specs=(pl.BlockSpec((1, Q, xw), lambda g, b, c: (b, rc(c), g)), pl.BlockSpec((1, Q, N), lambda g, b, c: (b, rc(c), g)),
                   pl.BlockSpec((1, Q, N), lambda g, b, c: (b, rc(c), g)), pl.BlockSpec((1, 1, Q, HG), lambda g, b, c: (b, g, rc(c), 0)),
                   pl.BlockSpec((1, 3, HG), lambda g, b, c: (g, 0, 0))),
        scratch_shapes=[pltpu.VMEM((HG, P, N), F32)], name="b_ssd_bwd", sem=("parallel", "arbitrary", "arbitrary"), comm=comm))


GN_W = SSM_D_INNER // SSM_N_GROUPS


def _gate_fwd(y, zx, nw):
    T = y.shape[0]
    tm = _pick(T, 256, 8)

    def body(y_ref, z_ref, w_ref, o_ref):
        z = z_ref[...]
        gt = y_ref[...] * (z * jax.nn.sigmoid(z))
        outs = []
        for k in range(SSM_N_GROUPS):
            gk = gt[:, GN_W * k:GN_W * (k + 1)]
            outs.append(gk * lax.rsqrt(jnp.mean(gk * gk, axis=-1, keepdims=True) + NORM_EPS))
        o_ref[...] = (jnp.concatenate(outs, axis=-1) * w_ref[...]).astype(BF16)

    row = pl.BlockSpec((tm, SSM_D_INNER), lambda i: (i, 0))
    return pl.pallas_call(
        body, out_shape=SDS((T, SSM_D_INNER), BF16), grid=(T // tm,),
        in_specs=[row, row, pl.BlockSpec((1, SSM_D_INNER), lambda i: (0, 0))], out_specs=row, name="b_gate_fwd",
        compiler_params=_params(("parallel",)),
    )(y, zx, nw)


def _gate_bwd(dgn, y, zx, nw):
    T = y.shape[0]
    tm = _pick(T, 256, 8)

    def body(d_ref, y_ref, z_ref, w_ref, dy_ref, dz_ref, dw_ref):
        z, yv, w = z_ref[...], y_ref[...], w_ref[...]
        sg = jax.nn.sigmoid(z)
        sz = z * sg
        gt = yv * sz
        gw = d_ref[...] * w
        dgts, dws = [], []
        for k in range(SSM_N_GROUPS):
            sl = slice(GN_W * k, GN_W * (k + 1))
            gk, gwk = gt[:, sl], gw[:, sl]
            rstd = lax.rsqrt(jnp.mean(gk * gk, axis=-1, keepdims=True) + NORM_EPS)
            dgts.append(rstd * gwk - gk * (rstd * rstd * rstd) * jnp.mean(gwk * gk, axis=-1, keepdims=True))
            dws.append(jnp.sum(d_ref[:, sl] * gk * rstd, axis=0, keepdims=True))
        dgt = jnp.concatenate(dgts, axis=-1)
        dy_ref[...] = dgt * sz
        dz_ref[...] = (dgt * yv * (sg * (1.0 + z * (1.0 - sg)))).astype(BF16)
        dw = jnp.concatenate(dws, axis=-1)

        @pl.when(pl.program_id(0) == 0)
        def _():
            dw_ref[...] = dw

        @pl.when(pl.program_id(0) > 0)
        def _():
            dw_ref[...] += dw

    row = pl.BlockSpec((tm, SSM_D_INNER), lambda i: (i, 0))
    vec = pl.BlockSpec((1, SSM_D_INNER), lambda i: (0, 0))
    return pl.pallas_call(
        body, out_shape=(SDS((T, SSM_D_INNER), F32), SDS((T, SSM_D_INNER), BF16), SDS((1, SSM_D_INNER), F32)), grid=(T // tm,),
        in_specs=[row, row, row, vec], out_specs=(row, row, vec), name="b_gate_bwd", compiler_params=_params(("arbitrary",)),
    )(dgn, y, zx, nw)


N_CHIPS = 4


def _dev_block(ref, kind, j, size):
    if kind == "slot":
        return ref.at[j]
    start = pl.multiple_of(j * size, size)
    nd = len(ref.shape)
    if kind == "col":
        return ref.at[(slice(None),) * (nd - 1) + (pl.ds(start, size),)]
    return ref.at[(slice(None),) * (nd - 2) + (pl.ds(start, size), slice(None))]


def _dma_sems(n, k):
    return [pltpu.SemaphoreType.DMA((n, k)), pltpu.SemaphoreType.DMA((n, k)), pltpu.SemaphoreType.DMA((n, k))]


def _place(shard, layer, kind, full_shape, dev, name):
    k, n = shard.shape[1:]
    tr = _pick(k, 512, 16)
    nb = k // tr

    def body(dev_ref, s_ref, o_ref):
        if kind == "slot":
            o_ref[0] = s_ref[0].astype(BF16)
        else:
            o_ref[...] = s_ref[0].astype(BF16)

    out_spec = {"slot": pl.BlockSpec((1, tr, n), lambda i, d: (d[0], i, 0)),
                "row": pl.BlockSpec((tr, n), lambda i, d: (d[0] * nb + i, 0)),
                "col": pl.BlockSpec((tr, n), lambda i, d: (i, d[0]))}[kind]
    return pl.pallas_call(
        body, out_shape=SDS(full_shape, BF16),
        grid_spec=pltpu.PrefetchScalarGridSpec(
            num_scalar_prefetch=1, grid=(nb,), in_specs=[pl.BlockSpec((1, tr, n), lambda i, d: (layer, i, 0))], out_specs=out_spec),
        name=name, compiler_params=_params(("arbitrary",)),
    )(dev, shard)


def _run_comm(comm, name):
    c_in = len(comm.inputs)

    def body(*refs):
        cins, couts, sems = refs[:c_in], refs[c_in:c_in + len(comm.out_shapes)], refs[c_in + len(comm.out_shapes):]
        for _, fn in comm.phases:
            fn(cins, couts, sems)

    return pl.pallas_call(
        body, out_shape=list(comm.out_shapes), in_specs=[ANY] * c_in, out_specs=[ANY] * len(comm.out_shapes),
        input_output_aliases=dict(comm.aliases), scratch_shapes=list(comm.sems), name=name,
    )(*comm.inputs)


def _gather_comm(items, mid=0.7):
    n = len(items)

    def tools(srcs, dsts, sems):
        send_sems, recv_sems, local_sems = sems
        px, py, pc = lax.axis_index("x"), lax.axis_index("y"), lax.axis_index("c")
        me, sibling = (px, py, pc), (px, py, 1 - pc)
        chips = [(1 - px, py), (px, 1 - py), (1 - px, 1 - py)]

        def blk(a, dev):
            return _dev_block(dsts[a], items[a][1], 4 * dev[0] + 2 * dev[1] + dev[2], items[a][2])

        def copy(a, k, block, to, src=None):
            return pltpu.make_async_remote_copy(
                src_ref=blk(a, block) if src is None else src, dst_ref=blk(a, block),
                send_sem=send_sems.at[a, k], recv_sem=recv_sems.at[a, k], device_id=to, device_id_type=MESH)

        def mine():
            return [pltpu.make_async_copy(srcs[a], blk(a, me), local_sems.at[a, 0]) for a in range(n) if not items[a][4]]

        def first():
            out = []
            for a in range(n):
                src = blk(a, me) if items[a][4] else srcs[a]
                out.append(copy(a, 0, me, sibling, src=src))
                out += [copy(a, 1 + j, me, (*chip, pc), src=src) for j, chip in enumerate(chips)]
            return out

        def passed():
            return [copy(a, 4 + j, (*chip, pc), sibling) for j, chip in enumerate(chips) for a in range(n)]

        return me, sibling, chips, pc, copy, mine, first, passed

    def start(srcs, dsts, sems):
        *_, mine, first, _ = tools(srcs, dsts, sems)
        for cp in mine() + first():
            cp.start()

    def forward(srcs, dsts, sems):
        me, _, chips, pc, copy, _, _, passed = tools(srcs, dsts, sems)
        fwd = passed()
        for j, chip in enumerate(chips):
            for a in range(n):
                copy(a, 1 + j, (*chip, pc), me).wait_recv()
                fwd[j * n + a].start()

    def finish(srcs, dsts, sems):
        me, sibling, chips, pc, copy, mine, first, passed = tools(srcs, dsts, sems)
        for a in range(n):
            copy(a, 0, sibling, me).wait_recv()
            for j, chip in enumerate(chips):
                copy(a, 4 + j, (*chip, 1 - pc), me).wait_recv()
        for cp in first() + passed():
            cp.wait_send()
        for cp in mine():
            cp.wait()

    return _Comm([it[0] for it in items], [SDS(it[3], it[0].dtype) for it in items],
                 {a: a for a in range(n) if items[a][4]}, _dma_sems(n, 7), [(0.0, start), (mid, forward), (1.0, finish)])


def _gather(items, name):
    return _run_comm(_gather_comm(items), name)


def _reduce_d2d_comm(items):
    n = len(items)

    def copies(gs, gots, sems):
        send_sems, recv_sems, _ = sems
        px, py, pc = lax.axis_index("x"), lax.axis_index("y"), lax.axis_index("c")
        out = []
        for a in range(n):
            _, kind, size, _ = items[a]
            for q in range(N_CHIPS):
                out.append(pltpu.make_async_remote_copy(
                    src_ref=_dev_block(gs[a], kind, 2 * q + 1 - pc, size), dst_ref=gots[a].at[q], send_sem=send_sems.at[a, q],
                    recv_sem=recv_sems.at[a, q], device_id=(px, py, 1 - pc), device_id_type=MESH))
        return out

    def start(gs, gots, sems):
        for cp in copies(gs, gots, sems):
            cp.start()

    def finish(gs, gots, sems):
        for cp in copies(gs, gots, sems):
            cp.wait()

    return _Comm([it[0] for it in items], [SDS((N_CHIPS,) + tuple(it[3]), F32) for it in items], {},
                 _dma_sems(n, N_CHIPS), [(0.0, start), (1.0, finish)])


def _pair_sum(g, got, kind, core, name):
    _, k, n = got.shape
    tr = _pick(k, max(16, STREAM_VMEM // (2 * n * 10)), 16)
    nb = k // tr

    def body(c_ref, g_ref, s_ref, o_ref):
        mine = g_ref[0] if kind == "slot" else g_ref[...]
        o_ref[0] = (mine + s_ref[0]).astype(BF16)

    g_spec = {"slot": pl.BlockSpec((1, tr, n), lambda q, i, c: (2 * q + c[0], i, 0)),
              "row": pl.BlockSpec((tr, n), lambda q, i, c: ((2 * q + c[0]) * nb + i, 0)),
              "col": pl.BlockSpec((tr, n), lambda q, i, c: (i, 2 * q + c[0]))}[kind]
    part = pl.BlockSpec((1, tr, n), lambda q, i, c: (q, i, 0))
    return pl.pallas_call(
        body, out_shape=SDS((N_CHIPS, k, n), BF16),
        grid_spec=pltpu.PrefetchScalarGridSpec(num_scalar_prefetch=1, grid=(N_CHIPS, nb), in_specs=[g_spec, part], out_specs=part),
        name=name, compiler_params=_params(("arbitrary", "arbitrary")),
    )(core, g, got)


def _reduce_ici_comm(parts):
    n = len(parts)

    def copies(ps, rs, sems, arriving):
        send_sems, recv_sems, _ = sems
        px, py, pc = lax.axis_index("x"), lax.axis_index("y"), lax.axis_index("c")
        my_chip = 2 * px + py
        out = []
        for a in range(n):
            for k in range(1, N_CHIPS):
                qx, qy = px ^ (k >> 1), py ^ (k & 1)
                q = 2 * qx + qy
                out.append(pltpu.make_async_remote_copy(
                    src_ref=ps[a].at[q], dst_ref=rs[a].at[q] if arriving else rs[a].at[my_chip], send_sem=send_sems.at[a, k - 1],
                    recv_sem=recv_sems.at[a, k - 1], device_id=(qx, qy, pc), device_id_type=MESH))
        return out

    def start(ps, rs, sems):
        for cp in copies(ps, rs, sems, False):
            cp.start()

    def finish(ps, rs, sems):
        for cp in copies(ps, rs, sems, True):
            cp.wait_recv()
        for cp in copies(ps, rs, sems, False):
            cp.wait_send()

    return _Comm(list(parts), [SDS(p.shape, p.dtype) for p in parts], {}, _dma_sems(n, N_CHIPS - 1),
                 [(0.0, start), (1.0, finish)])


def _adam_update(g, w, m, v):
    c1 = 1.0 - ADAM_B1 ** ADAM_STEP
    c2 = 1.0 - ADAM_B2 ** ADAM_STEP
    nm = ADAM_B1 * m + (1.0 - ADAM_B1) * g
    nv = ADAM_B2 * v + (1.0 - ADAM_B2) * (g * g)
    delta = -ADAM_LR * ((nm / c1) / (jnp.sqrt(nv / c2) + ADAM_EPS) + ADAM_WD * w)
    return delta, nm, nv


def _adamw(parts, recv, w, m, v, layer, prev, chip, name):
    _, R, C = w.shape
    row_bytes = 2 * C * (N_CHIPS * 2 + 7 * 4)
    tr = _pick(R, max(16, STREAM_VMEM // row_bytes), 16)
    n_prev = 0 if prev is None else 4

    def body(ch_ref, own_ref, r1_ref, r2_ref, r3_ref, w_ref, m_ref, v_ref, *rest):
        g_ref, d_ref, nm_ref, nv_ref = rest[n_prev:]
        g = own_ref[0].astype(F32)
        for r_ref in (r1_ref, r2_ref, r3_ref):
            g = g + r_ref[0].astype(F32)
        g_ref[0] = g
        d_ref[0], nm_ref[0], nv_ref[0] = _adam_update(g, w_ref[0], m_ref[0], v_ref[0])

    lay = pl.BlockSpec((1, tr, C), lambda i, ch: (layer, i, 0))
    other = lambda k: pl.BlockSpec((1, tr, C), lambda i, ch: (ch[0] ^ k, i, 0))
    out = SDS(w.shape, F32)
    return pl.pallas_call(
        body, out_shape=(out, out, out, out),
        grid_spec=pltpu.PrefetchScalarGridSpec(
            num_scalar_prefetch=1, grid=(R // tr,),
            in_specs=[pl.BlockSpec((1, tr, C), lambda i, ch: (ch[0], i, 0)), other(2), other(1), other(3), lay, lay, lay]
            + [ANY] * n_prev,
            out_specs=(lay, lay, lay, lay)),
        input_output_aliases={8 + k: k for k in range(n_prev)},
        name=name, compiler_params=_params(("arbitrary",)),
    )(chip, parts, recv, recv, recv, w, m, v, *(prev or ()))


def _small_adamw(gathered, ws, ms, vs):
    n = len(ws)

    def body(*refs):
        g_in, w_in, m_in, v_in = refs[:n], refs[n:2 * n], refs[2 * n:3 * n], refs[3 * n:4 * n]
        outs = refs[4 * n:]
        for i in range(n):
            g = g_in[i][0]
            for dev in range(1, N_DEV):
                g = g + g_in[i][dev]
            d, nm, nv = _adam_update(g, w_in[i][...], m_in[i][...], v_in[i][...])
            outs[i][...] = g
            outs[n + i][...] = d
            outs[2 * n + i][...] = nm
            outs[3 * n + i][...] = nv

    shapes = [SDS(w.shape, F32) for w in ws]
    outs = pl.pallas_call(body, out_shape=shapes * 4, name="small_adamw")(*gathered, *ws, *ms, *vs)
    return outs[:n], outs[n:2 * n], outs[2 * n:3 * n], outs[3 * n:]


W_NAMES = ("norm_mix_w", "norm_mlp_w", "a_w_qkv", "a_b_qkv", "a_sinks", "a_w_o", "a_b_o", "b_in_w", "b_conv_w", "b_conv_b",
           "b_dt_bias", "b_a_log", "b_d", "b_norm_w", "b_out_w", "c_w_qkv", "c_w_o", "mlp_w_up", "mlp_w_down", "final_norm_w")
BIG_KIND = {"a_w_qkv": "slot", "a_w_o": "row", "b_in_w": "slot", "b_out_w": "row", "c_w_qkv": "col", "c_w_o": "row",
            "mlp_w_up": "col", "mlp_w_down": "row"}
SMALL_SHARDED = {"a_b_qkv": 1, "a_b_o": 1, "b_conv_w": 2}
SMALL_REPLICATED = ("norm_mix_w", "norm_mlp_w", "a_sinks", "b_conv_b", "b_dt_bias", "b_a_log", "b_d", "b_norm_w", "final_norm_w")


def _layer_big(i):
    kind, j = i % 3, i // 3
    mix = {0: [("a_w_qkv", j), ("a_w_o", j)], 1: [("b_in_w", 0), ("b_out_w", 0)], 2: [("c_w_qkv", 0), ("c_w_o", 0)]}[kind]
    return mix + [("mlp_w_up", i), ("mlp_w_down", i)]


def _block_size(kind, shard2d):
    return {"slot": None, "row": shard2d[0], "col": shard2d[1]}[kind]


def _full2d(kind, shard2d):
    k, n = shard2d
    return {"slot": (N_DEV, k, n), "row": (N_DEV * k, n), "col": (k, N_DEV * n)}[kind]


def _from_slots(t, ax):
    s = t.shape[1:]
    return jnp.moveaxis(t, 0, ax).reshape(s[:ax] + (N_DEV * s[ax],) + s[ax + 1:])


def _to_slots(g, ax):
    s = g.shape
    return jnp.moveaxis(g.reshape(s[:ax] + (N_DEV, s[ax] // N_DEV) + s[ax + 1:]), ax, 0)


def _rope_tables(positions):
    half = HEAD_DIM // 2
    inv = ROPE_THETA ** (-(jnp.arange(LANES, dtype=jnp.int32) % half).astype(F32) / half)
    ang = positions.astype(F32).reshape(-1, 1) * inv
    return jnp.cos(ang), jnp.sin(ang)


def _swa_fwd(u, h, p, j, B, S, cos, sin, tag, nw, plan=None):
    qkv = _matmul(u, p["a_w_qkv"][j], out_dtype=BF16, bias=p["a_b_qkv"][j][None], rope=(cos, sin),
                  rope_cols=A_Q_DIM + A_KV_DIM, name=f"{tag}_qkv", plan=plan)
    o, lse = _attn_fwd(qkv, B, S, 1, n_heads=A_N_HEADS, n_kv=A_N_KV, q_col=0, k_col=A_Q_DIM, v_col=A_Q_DIM + A_KV_DIM,
                       max_dist=A_WINDOW - 1, sinks=p["a_sinks"][j], name=f"{tag}_attn", plan=plan)
    h1, u2 = _matmul(o, p["a_w_o"][j], bias=p["a_b_o"][j][None], resid=h, norm_out=nw, name=f"{tag}_o")
    return h1, u2, (qkv, o, lse)


def _swa_bwd(dh1, u, saved, p, j, B, S, cos, sin, tag, norm, plan=None):
    qkv, o, lse = saved
    kw = dict(n_heads=A_N_HEADS, n_kv=A_N_KV, q_col=0, k_col=A_Q_DIM, v_col=A_Q_DIM + A_KV_DIM, max_dist=A_WINDOW - 1)
    g = {}
    do = _matmul(dh1, p["a_w_o"][j], tb=True, name=f"{tag}_do")
    g["a_w_o"] = _matmul(o, dh1, ta=True, name=f"{tag}_dwo")
    g["a_b_o"] = _colsum(dh1, f"{tag}_dbo")[0]
    sk = jnp.pad(p["a_sinks"][j], (0, LANES - A_N_HEADS))[None]
    delta, dob, dsink = _delta(do, o, lse, sk, name=f"{tag}_delta")
    g["a_sinks"] = dsink[0, :A_N_HEADS]
    dq = _attn_dq(qkv, dob, lse, delta, cos, sin, B, S, 1, name=f"{tag}_dq", plan=plan, **kw)
    dk, dv = _attn_dkv(qkv, dob, lse, delta, cos, sin, B, S, 1, name=f"{tag}_dkv", plan=plan, **kw)
    dqkv = jnp.concatenate([dq, dk, dv], axis=1)
    g["a_w_qkv"] = _matmul(u, dqkv, ta=True, name=f"{tag}_dwqkv")
    g["a_b_qkv"] = _colsum(dqkv, f"{tag}_dbqkv")[0]
    if plan is not None:
        plan.grads(3 * j, "mix", {"a_w_qkv": g["a_w_qkv"], "a_w_o": g["a_w_o"]})
    dh, dnw = _matmul(dqkv, p["a_w_qkv"][j], tb=True, norm_bwd=(norm[0], norm[1], dh1), name=f"{tag}_du", plan=plan)
    return dh, dnw, g


def _group_cols(gi, qkv):
    W = C_HEADS * HEAD_DIM
    if C_PATTERNS[gi][1] == 1:
        return qkv, (gi * W, (3 + gi) * W, (6 + gi) * W)
    part = jnp.concatenate([qkv[:, (3 * j + gi) * W:(3 * j + gi + 1) * W] for j in range(3)], axis=1)
    return part, (0, W, 2 * W)


def _dil_fwd(u, h, p, B, S, cos, sin, nw, plan=None):
    W = C_HEADS * HEAD_DIM
    qkv = _matmul(u, p["c_w_qkv"][0], out_dtype=BF16, rope=(cos, sin), rope_cols=6 * W, name="c_qkv", plan=plan)
    os_, lses, parts = [], [], []
    for gi, (window, dil) in enumerate(C_PATTERNS):
        part, (qc, kc, vc) = _group_cols(gi, qkv)
        o, lse = _attn_fwd(part, B, S, dil, n_heads=C_HEADS, n_kv=C_HEADS, q_col=qc, k_col=kc, v_col=vc,
                           max_dist=window // dil, sinks=None, name=f"c_attn{gi}")
        os_.append(o)
        lses.append(lse)
        parts.append((part, (qc, kc, vc)))
    o, lse = _merge(os_, lses)
    h1, u2 = _matmul(o, p["c_w_o"][0], resid=h, norm_out=nw, name="c_o")
    return h1, u2, (parts, o, lse)


def _dil_bwd(dh1, u, saved, p, B, S, cos, sin, norm, plan=None):
    parts, o, lse = saved
    g = {}
    do = _matmul(dh1, p["c_w_o"][0], tb=True, name="c_do")
    g["c_w_o"] = _matmul(o, dh1, ta=True, name="c_dwo")[None]
    delta, dob = _delta(do, o, name="c_delta")
    dqs, dks, dvs = [], [], []
    for gi, (window, dil) in enumerate(C_PATTERNS):
        part, (qc, kc, vc) = parts[gi]
        kw = dict(n_heads=C_HEADS, n_kv=C_HEADS, q_col=qc, k_col=kc, v_col=vc, max_dist=window // dil)
        dqs.append(_attn_dq(part, dob, lse, delta, cos, sin, B, S, dil, name=f"c_dq{gi}", **kw))
        dk, dv = _attn_dkv(part, dob, lse, delta, cos, sin, B, S, dil, name=f"c_dkv{gi}", **kw)
        dks.append(dk)
        dvs.append(dv)
    dqkv = jnp.concatenate(dqs + dks + dvs, axis=1)
    g["c_w_qkv"] = _matmul(u, dqkv, ta=True, name="c_dwqkv", plan=plan)[None]
    if plan is not None:
        plan.grads(2, "mix", {"c_w_qkv": g["c_w_qkv"][0], "c_w_o": g["c_w_o"][0]})
    dh, dnw = _matmul(dqkv, p["c_w_qkv"][0], tb=True, norm_bwd=(norm[0], norm[1], dh1), name="c_du", plan=plan)
    return dh, dnw, g


def _ssm_params(p):
    par = jnp.stack([p["b_dt_bias"][0], p["b_a_log"][0], p["b_d"][0]], axis=0)
    prow = par.reshape(3, SSM_N_GROUPS, SSM_HG).transpose(1, 0, 2)
    return prow, prow.transpose(0, 2, 1)


def _mamba_fwd(u, h, p, B, S, nw, plan=None):
    T = B * S
    G, HG = SSM_N_GROUPS, SSM_HG
    w_in = p["b_in_w"][0]
    nzx = SSM_D_INNER + SSM_CONV_DIM
    w_dt = jnp.pad(w_in[:, nzx:], ((0, 0), (0, LANES - SSM_N_HEADS)))
    zx = _matmul(u, w_in[:, :nzx], name="b_zx", plan=plan)
    dtraw = _matmul(u, w_dt, name="b_dt")[:, :SSM_N_HEADS]
    dtc = dtraw.reshape(B, S, G, HG).transpose(0, 2, 1, 3)
    dtr = dtraw.reshape(B, S, G, HG).transpose(0, 2, 3, 1)
    prow, pcol = _ssm_params(p)
    zx3 = zx.reshape(B, S, nzx)
    xc3 = _conv_fwd(zx3, p["b_conv_w"][0], p["b_conv_b"])
    y3, states = _ssd_fwd(xc3, dtc, dtr, prow, pcol, plan=plan)
    y = y3.reshape(T, SSM_D_INNER)
    gn = _gate_fwd(y, zx, p["b_norm_w"])
    h1, u2 = _matmul(gn, p["b_out_w"][0], resid=h, norm_out=nw, name="b_out")
    return h1, u2, (zx, dtc, dtr, xc3, y, states, gn, w_dt)


def _mamba_bwd(dh1, u, saved, p, B, S, norm, plan=None):
    T = B * S
    zx, dtc, dtr, xc3, y, states, gn, w_dt = saved
    nzx = SSM_D_INNER + SSM_CONV_DIM
    w_in = p["b_in_w"][0]
    prow, pcol = _ssm_params(p)
    g = {}
    dgn = _matmul(dh1, p["b_out_w"][0], tb=True, name="b_dgn")
    g["b_out_w"] = _matmul(gn, dh1, ta=True, name="b_dwout")[None]
    dy, dz, dnw = _gate_bwd(dgn, y, zx, p["b_norm_w"])
    g["b_norm_w"] = dnw
    dx3, dB3, dC3, ddt, dpar = _ssd_bwd(xc3, dtc, dtr, prow, pcol, states, dy.reshape(B, S, SSM_D_INNER), plan=plan)
    dpar = dpar.transpose(1, 0, 2).reshape(3, SSM_N_HEADS)
    g["b_dt_bias"], g["b_a_log"], g["b_d"] = dpar[0:1], dpar[1:2], dpar[2:3]
    zx3 = zx.reshape(B, S, nzx)
    cw, cb = p["b_conv_w"][0], p["b_conv_b"]
    parts, dws, dbs = [], [], []
    for col0, dpart, nm in ((0, dx3, "b_conv_bwd_x"), (SSM_D_INNER, dB3, "b_conv_bwd_b"),
                            (SSM_D_INNER + SSM_BC_DIM, dC3, "b_conv_bwd_c")):
        dxp, dw, db = _conv_bwd(zx3, dpart, cw, cb, col0, nm)
        parts.append(dxp.reshape(T, -1))
        dws.append(dw)
        dbs.append(db)
    g["b_conv_w"] = jnp.concatenate(dws, axis=1)[None]
    g["b_conv_b"] = jnp.concatenate(dbs, axis=1)
    dzx = jnp.concatenate([dz] + parts, axis=1)
    ddtraw = ddt.transpose(0, 2, 1, 3).reshape(T, SSM_N_HEADS)
    ddtp = jnp.pad(ddtraw, ((0, 0), (0, LANES - SSM_N_HEADS)))
    dw_zx = _matmul(u, dzx, ta=True, name="b_dwzx")
    dw_dt = _matmul(u, ddtp, ta=True, name="b_dwdt")[:, :SSM_N_HEADS]
    g["b_in_w"] = jnp.concatenate([dw_zx, dw_dt], axis=1)[None]
    if plan is not None:
        plan.grads(1, "mix", {"b_in_w": g["b_in_w"][0], "b_out_w": g["b_out_w"][0]})
    du = _matmul(dzx, w_in[:, :nzx], tb=True, name="b_du_zx", plan=plan)
    dh, dnw = _matmul(ddtp, w_dt, tb=True, resid=du, norm_bwd=(norm[0], norm[1], dh1), name="b_du_dt")
    return dh, dnw, g


def _local_step(x, positions, p, target, plan=None):
    B, S, D = x.shape
    T = B * S
    cos, sin = _rope_tables(positions)
    h = x.reshape(T, D)
    tape = []
    u = _rmsnorm_fwd(h, p["norm_mix_w"][0], "l0_norm_mix")
    for i in range(DEPTH):
        kind, j = i % 3, i // 3
        nw = p["norm_mlp_w"][i]
        if kind == 0:
            h1, u2, saved = _swa_fwd(u, h, p, j, B, S, cos, sin, f"a{j}", nw, plan)
        elif kind == 1:
            h1, u2, saved = _mamba_fwd(u, h, p, B, S, nw, plan)
        else:
            h1, u2, saved = _dil_fwd(u, h, p, B, S, cos, sin, nw, plan)
        r, s = _matmul(u2, p["mlp_w_up"][i], out_dtype=BF16, relu2=True, name=f"l{i}_up", plan=plan)
        if i + 1 < DEPTH:
            h2, u_next = _matmul(s, p["mlp_w_down"][i], resid=h1, norm_out=p["norm_mix_w"][i + 1], name=f"l{i}_down", plan=plan)
        else:
            h2, u_next = _matmul(s, p["mlp_w_down"][i], resid=h1, name=f"l{i}_down", plan=plan), None
        tape.append((h, u, saved, h1, u2, r, s))
        h, u = h2, u_next
    dh, dwf, loss = _final_loss(h, target.reshape(T, D), p["final_norm_w"])
    grads = {"final_norm_w": dwf[0]}
    per_layer = {n: [None] * DEPTH for n in ("norm_mix_w", "norm_mlp_w", "mlp_w_up", "mlp_w_down")}
    a_grads = [None, None]
    for i in reversed(range(DEPTH)):
        kind, j = i % 3, i // 3
        h0, u, saved, h1, u2, r, s = tape[i]
        da = _matmul(dh, p["mlp_w_down"][i], tb=True, out_dtype=BF16, mul=r, mul_scale=2.0, name=f"l{i}_da", plan=plan)
        per_layer["mlp_w_down"][i] = _matmul(s, dh, ta=True, name=f"l{i}_dwdown")
        per_layer["mlp_w_up"][i] = _matmul(u2, da, ta=True, name=f"l{i}_dwup")
        if plan is not None:
            plan.grads(i, "mlp", {"mlp_w_up": per_layer["mlp_w_up"][i], "mlp_w_down": per_layer["mlp_w_down"][i]})
        dh1, dnw = _matmul(da, p["mlp_w_up"][i], tb=True, norm_bwd=(h1, p["norm_mlp_w"][i], dh), name=f"l{i}_du2", plan=plan)
        per_layer["norm_mlp_w"][i] = dnw[0]
        norm = (h0, p["norm_mix_w"][i])
        if kind == 0:
            dh, dnw, g = _swa_bwd(dh1, u, saved, p, j, B, S, cos, sin, f"a{j}", norm, plan)
            a_grads[j] = g
        elif kind == 1:
            dh, dnw, g = _mamba_bwd(dh1, u, saved, p, B, S, norm, plan)
            grads.update(g)
        else:
            dh, dnw, g = _dil_bwd(dh1, u, saved, p, B, S, cos, sin, norm, plan)
            grads.update(g)
        per_layer["norm_mix_w"][i] = dnw[0]
    for n in ("norm_mix_w", "norm_mlp_w"):
        grads[n] = jnp.stack(per_layer[n], axis=0)
    for n in ("mlp_w_up", "mlp_w_down"):
        grads[n] = per_layer[n]
    for n in ("a_b_qkv", "a_sinks", "a_b_o"):
        grads[n] = jnp.stack([a_grads[0][n], a_grads[1][n]], axis=0)
    for n in ("a_w_qkv", "a_w_o"):
        grads[n] = [a_grads[0][n], a_grads[1][n]]
    for n in ("b_in_w", "b_out_w", "c_w_qkv", "c_w_o"):
        grads[n] = [grads[n][0]]
    return loss, dh.reshape(B, S, D), grads


MIX = {0: ("a_w_qkv", "a_w_o"), 1: ("b_in_w", "b_out_w"), 2: ("c_w_qkv", "c_w_o")}
MLP = ("mlp_w_up", "mlp_w_down")
GATHER_FIRST = (0, MIX[0])
GATHER_HOSTS = {"a0_qkv": ((0, ("mlp_w_up",)),), "a0_attn": ((0, ("mlp_w_down",)),), "l0_up": ((1, ("b_in_w",)),),
                "l0_down": ((1, ("b_out_w",)),), "b_zx": ((1, ("mlp_w_up",)),),
                "b_ssd_fwd": ((1, ("mlp_w_down",)), (2, None)), "c_qkv": ((3, None),)}
REDUCE_HOSTS = {(3, "mlp"): ("l3_du2", "a1_dkv"), (3, "mix"): ("a1_du", "l2_da"),
                (2, "mlp"): ("l2_du2", "c_dwqkv"), (2, "mix"): ("c_du", "b_ssd_bwd"),
                (1, "mlp"): ("l1_du2", "b_ssd_bwd"), (1, "mix"): ("b_du_zx", "a0_dq"),
                (0, "mlp"): ("l0_du2", "a0_dkv"), (0, "mix"): ("a0_du", None)}


class _Plan:
    def __init__(self, w, m, v, p, dev, chip, core):
        self.w, self.m, self.v, self.p, self.dev, self.chip, self.core = w, m, v, p, dev, chip, core
        self.pending = {}
        self.res = {n: None for n in BIG_KIND}
        self._install(*GATHER_FIRST)(_gather(self._gather_items(*GATHER_FIRST), "gather_first"))
        for host, groups in GATHER_HOSTS.items():
            for i, only in groups:
                self._wait_for(host, _gather_comm(self._gather_items(i, only)), self._install(i, only))

    def _wait_for(self, host, comm, done):
        self.pending.setdefault(host, []).append((comm, done))

    def _names(self, i, only):
        return [(n, l) for n, l in _layer_big(i) if only is None or n in only]

    def _gather_items(self, i, only):
        items = []
        for n, l in self._names(i, only):
            kind, s2 = BIG_KIND[n], self.w[n].shape[1:]
            placed = _place(self.w[n], l, kind, _full2d(kind, s2), self.dev, f"place_l{i}_{n}")
            items.append((placed, kind, _block_size(kind, s2), _full2d(kind, s2), True))
        return items

    def _install(self, i, only):
        def done(fulls):
            for (n, l), t in zip(self._names(i, only), fulls):
                self.p[n][l] = _from_slots(t, 1) if BIG_KIND[n] == "slot" else t
        return done

    def take(self, host):
        return _Comm.merge([c for c, _ in self.pending[host]]) if host in self.pending else None

    def give(self, host, results):
        for comm, done in self.pending.pop(host):
            done(results[:len(comm.out_shapes)])
            results = results[len(comm.out_shapes):]

    def grads(self, i, group, grads):
        names = self._names(i, MLP if group == "mlp" else MIX[i % 3])
        items = []
        for n, _ in names:
            kind, s2 = BIG_KIND[n], self.w[n].shape[1:]
            items.append((_to_slots(grads[n], 1) if kind == "slot" else grads[n], kind, _block_size(kind, s2), s2))
        d2d_host, ici_host = REDUCE_HOSTS[(i, group)]
        tag = f"l{i}_{group}"

        def update(parts):
            def done(recv):
                for (n, l), pt, r in zip(names, parts, recv):
                    self.res[n] = _adamw(pt, r, self.w[n], self.m[n], self.v[n], l, self.res[n], self.chip, f"adamw_l{i}_{n}")
            return done

        def second(sib):
            parts = [_pair_sum(it[0], s, it[1], self.core, f"pair_sum_l{i}_{n}") for (n, _), it, s in zip(names, items, sib)]
            self._send(ici_host, _reduce_ici_comm(parts), update(parts), f"reduce_ici_{tag}")

        self._send(d2d_host, _reduce_d2d_comm(items), second, f"reduce_d2d_{tag}")

    def _send(self, host, comm, done, name):
        if host is None:
            done(_run_comm(comm, name))
        else:
            self._wait_for(host, comm, done)

    def flush(self):
        late = 0
        while self.pending:
            host = next(iter(self.pending))
            for comm, done in self.pending.pop(host):
                done(_run_comm(comm, f"late_{late}_{host}"))
                late += 1


def kernel(x, positions, norm_mix_w, norm_mlp_w, a_w_qkv, a_b_qkv, a_sinks, a_w_o, a_b_o, b_in_w, b_conv_w, b_conv_b, b_dt_bias, b_a_log, b_d, b_norm_w, b_out_w, c_w_qkv, c_w_o, mlp_w_up, mlp_w_down, final_norm_w, loss_target, m_norm_mix_w, m_norm_mlp_w, m_a_w_qkv, m_a_b_qkv, m_a_sinks, m_a_w_o, m_a_b_o, m_b_in_w, m_b_conv_w, m_b_conv_b, m_b_dt_bias, m_b_a_log, m_b_d, m_b_norm_w, m_b_out_w, m_c_w_qkv, m_c_w_o, m_mlp_w_up, m_mlp_w_down, m_final_norm_w, v_norm_mix_w, v_norm_mlp_w, v_a_w_qkv, v_a_b_qkv, v_a_sinks, v_a_w_o, v_a_b_o, v_b_in_w, v_b_conv_w, v_b_conv_b, v_b_dt_bias, v_b_a_log, v_b_d, v_b_norm_w, v_b_out_w, v_c_w_qkv, v_c_w_o, v_mlp_w_up, v_mlp_w_down, v_final_norm_w):
    w = dict(zip(W_NAMES, (norm_mix_w, norm_mlp_w, a_w_qkv, a_b_qkv, a_sinks, a_w_o, a_b_o, b_in_w, b_conv_w, b_conv_b,
                           b_dt_bias, b_a_log, b_d, b_norm_w, b_out_w, c_w_qkv, c_w_o, mlp_w_up, mlp_w_down, final_norm_w)))
    m = dict(zip(W_NAMES, (m_norm_mix_w, m_norm_mlp_w, m_a_w_qkv, m_a_b_qkv, m_a_sinks, m_a_w_o, m_a_b_o, m_b_in_w,
                           m_b_conv_w, m_b_conv_b, m_b_dt_bias, m_b_a_log, m_b_d, m_b_norm_w, m_b_out_w, m_c_w_qkv, m_c_w_o,
                           m_mlp_w_up, m_mlp_w_down, m_final_norm_w)))
    v = dict(zip(W_NAMES, (v_norm_mix_w, v_norm_mlp_w, v_a_w_qkv, v_a_b_qkv, v_a_sinks, v_a_w_o, v_a_b_o, v_b_in_w,
                           v_b_conv_w, v_b_conv_b, v_b_dt_bias, v_b_a_log, v_b_d, v_b_norm_w, v_b_out_w, v_c_w_qkv, v_c_w_o,
                           v_mlp_w_up, v_mlp_w_down, v_final_norm_w)))
    px, py, pc = lax.axis_index("x"), lax.axis_index("y"), lax.axis_index("c")
    me = 4 * px + 2 * py + pc
    dev, chip, core = (t.astype(jnp.int32).reshape(1) for t in (me, 2 * px + py, pc))

    trio = tuple(SMALL_SHARDED)
    got = _gather([(d[n], "slot", None, (N_DEV,) + d[n].shape, False) for n in trio for d in (w, m, v)], "gather_small")
    slots = {n: got[3 * i:3 * i + 3] for i, n in enumerate(trio)}
    p = {n: w[n] for n in SMALL_REPLICATED}
    for n in trio:
        p[n] = _from_slots(slots[n][0], SMALL_SHARDED[n])
    for n in BIG_KIND:
        p[n] = [None] * w[n].shape[0]
    plan = _Plan(w, m, v, p, dev, chip, core)
    loss_part, dx, grads = _local_step(x, positions, p, loss_target, plan)
    loss = lax.psum(loss_part[0, 0], AXES)
    plan.flush()
    out = {n: list(plan.res[n]) for n in BIG_KIND}

    small = SMALL_REPLICATED + trio
    as2d = lambda t: t.reshape(1, -1) if t.ndim == 1 else t
    g_sm = [as2d(grads[n]) for n in SMALL_REPLICATED] + [_to_slots(grads[n].reshape(p[n].shape), SMALL_SHARDED[n]) for n in trio]
    gathered = _gather([(g, "slot", None, (N_DEV,) + g.shape, False) for g in g_sm], "gather_small_grads")
    ws = [as2d(w[n]) for n in SMALL_REPLICATED] + [slots[n][0] for n in trio]
    ms = [as2d(m[n]) for n in SMALL_REPLICATED] + [slots[n][1] for n in trio]
    vs = [as2d(v[n]) for n in SMALL_REPLICATED] + [slots[n][2] for n in trio]
    sm_out = _small_adamw(gathered, ws, ms, vs)
    for i, n in enumerate(small):
        if n in SMALL_SHARDED:
            out[n] = [lax.dynamic_index_in_dim(sm_out[k][i], me, 0, keepdims=False) for k in range(4)]
        else:
            out[n] = [sm_out[k][i].reshape(w[n].shape) for k in range(4)]
    return (loss, dx, *[out[n][0] for n in W_NAMES], *[out[n][1] for n in W_NAMES], *[out[n][2] for n in W_NAMES],
            *[out[n][3] for n in W_NAMES])
```

```python
import functools
import math

import jax
import jax.numpy as jnp
import numpy as np
from jax import lax
from jax.experimental import pallas as pl
from jax.experimental.pallas import tpu as pltpu

F32 = jnp.float32
BF16 = jnp.bfloat16
SDS = jax.ShapeDtypeStruct

D_MODEL = 1024
DEPTH = 4
BLOCK = 128
ROPE_THETA = 10000.0
NORM_EPS = 1e-5
HEAD_DIM = 64
A_N_HEADS = 16
A_N_KV = 2
A_WINDOW = 128
A_Q_DIM = 1024
A_KV_DIM = 128
SSM_D_INNER = 2048
SSM_N_HEADS = 32
SSM_N_GROUPS = 8
SSM_HG = 4
SSM_D_STATE = 128
SSM_CONV = 4
SSM_CHUNK = 128
SSM_BC_DIM = 1024
SSM_CONV_DIM = 4096
C_PATTERNS = ((128, 1), (512, 4), (2048, 16))
C_HEADS = 16
ADAM_LR, ADAM_B1, ADAM_B2, ADAM_EPS, ADAM_WD, ADAM_STEP = 0.001, 0.9, 0.999, 1e-08, 0.01, 10

N_DEV = 8
AXES = ("x", "y", "c")
LANES = 128
VMEM_LIMIT = 56 * 1024 * 1024
STREAM_VMEM = 16 * 1024 * 1024
NEG = -1e30

NN = (((1,), (0,)), ((), ()))
NT = (((1,), (1,)), ((), ()))
TN = (((0,), (0,)), ((), ()))
HI = lax.Precision.HIGHEST


def _pick(n, cap, mult=LANES):
    best = None
    for t in range(mult, min(n, cap) + 1, mult):
        if n % t == 0:
            best = t
    return best if best is not None else n


def _params(sem):
    return pltpu.CompilerParams(dimension_semantics=sem, vmem_limit_bytes=VMEM_LIMIT)


def _bf(x):
    return x if x.dtype == BF16 else x.astype(BF16)


def _rot_half(y):
    n = y.shape[-1]
    lane = lax.broadcasted_iota(jnp.int32, y.shape, y.ndim - 1)
    return jnp.where((lane % HEAD_DIM) < HEAD_DIM // 2, -pltpu.roll(y, n - 32, y.ndim - 1), pltpu.roll(y, 32, y.ndim - 1))


def _rope(y, cos, sin, sign):
    reps = y.shape[-1] // LANES
    c = jnp.tile(cos, (1, reps)) if reps > 1 else cos
    s = jnp.tile(sin, (1, reps)) if reps > 1 else sin
    return y * c + sign * (_rot_half(y) * s)


MESH = pl.DeviceIdType.MESH
ANY = pl.BlockSpec(memory_space=pl.ANY)


class _Comm:
    def __init__(self, inputs, out_shapes, aliases, sems, phases):
        self.inputs, self.out_shapes, self.aliases, self.sems, self.phases = inputs, out_shapes, aliases, sems, phases

    @staticmethod
    def merge(comms):
        if len(comms) == 1:
            return comms[0]
        ins, outs, aliases, sems, spans = [], [], {}, [], []
        for c in comms:
            aliases.update({len(ins) + i: len(outs) + j for i, j in c.aliases.items()})
            spans.append((len(ins), len(ins) + len(c.inputs), len(outs), len(outs) + len(c.out_shapes), len(sems),
                          len(sems) + len(c.sems)))
            ins, outs, sems = ins + list(c.inputs), outs + list(c.out_shapes), sems + list(c.sems)
        phases = []
        for f in sorted({f for c in comms for f, _ in c.phases}):
            todo = [(fn, sp) for c, sp in zip(comms, spans) for g, fn in c.phases if g == f]

            def run(cins, couts, csems, todo=todo):
                for fn, (i0, i1, o0, o1, s0, s1) in todo:
                    fn(cins[i0:i1], couts[o0:o1], csems[s0:s1])
            phases.append((f, run))
        return _Comm(ins, outs, aliases, sems, phases)


def _pc(body, args, *, out_shape, grid, in_specs, out_specs, name, sem, scratch_shapes=(), comm=None):
    single = not isinstance(out_shape, (tuple, list))
    outs, ospecs = ([out_shape], [out_specs]) if single else (list(out_shape), list(out_specs))
    unpack = (lambda r: r[0]) if single else (lambda r: tuple(r))
    if comm is None:
        res = pl.pallas_call(body, out_shape=outs, grid=grid, in_specs=list(in_specs), out_specs=ospecs,
                             scratch_shapes=list(scratch_shapes), name=name, compiler_params=_params(sem))(*args)
        return unpack(res)
    n_in, n_out, n_scr = len(in_specs), len(outs), len(scratch_shapes)
    c_in, c_out = len(comm.inputs), len(comm.out_shapes)
    total = math.prod(grid)
    steps = [min(total - 1, int(f * total)) for f, _ in comm.phases[:-1]]

    def wrapped(*refs):
        ins, cins = refs[:n_in], refs[n_in:n_in + c_in]
        o = refs[n_in + c_in:n_in + c_in + n_out]
        couts = refs[n_in + c_in + n_out:n_in + c_in + n_out + c_out]
        rest = refs[n_in + c_in + n_out + c_out:]
        scr, csems = rest[:n_scr], rest[n_scr:]
        step = pl.program_id(0)
        for ax in range(1, len(grid)):
            step = step * grid[ax] + pl.program_id(ax)
        for (_, fn), st in zip(comm.phases[:-1], steps):
            @pl.when(step == st)
            def _(fn=fn):
                fn(cins, couts, csems)
        body(*ins, *o, *scr)

        @pl.when(step == total - 1)
        def _():
            comm.phases[-1][1](cins, couts, csems)

    res = pl.pallas_call(
        wrapped, out_shape=outs + list(comm.out_shapes), grid=grid, in_specs=list(in_specs) + [ANY] * c_in,
        out_specs=ospecs + [ANY] * c_out, scratch_shapes=list(scratch_shapes) + list(comm.sems),
        input_output_aliases={n_in + i: n_out + j for i, j in comm.aliases.items()}, name=name,
        compiler_params=_params(("arbitrary",) * len(grid)),
    )(*args, *comm.inputs)
    return unpack(res[:n_out]), list(res[n_out:])


def _hosted(plan, name, run):
    comm = plan.take(name) if plan is not None else None
    if comm is None:
        return run(None)
    res, extra = run(comm)
    plan.give(name, extra)
    return res


MM_VMEM = 40 * 1024 * 1024
HBM_BYTES_PER_US = 2.5e6
STEP_US = 0.35


def _divisors(n, cands):
    return [c for c in cands if c <= n and n % c == 0] or [n]


def _mm_tiles(M, N, K, sa, sb, out_bytes, extra_bytes, full_rows=False):
    best = None
    for tm in _divisors(M, (2048, 1024, 512, 256)):
        for tn in ([N] if full_rows else _divisors(N, (1024, 640, 512, 256, 128))):
            for tk in _divisors(K, (K, K // 2, K // 3, K // 4, 2048, 1024, 640, 512)):
                if tk != K and tk % LANES:
                    continue
                nk = K // tk
                vmem = 2 * tm * tk * sa + 2 * tk * tn * sb + tm * tn * (2 * (out_bytes + extra_bytes) + 8 + (4 if nk > 1 else 0))
                if vmem > MM_VMEM:
                    continue
                a_traffic = M * K * sa * (1 if nk == 1 else N // tn)
                b_traffic = K * N * sb * (1 if (nk == 1 and N == tn) else M // tm)
                steps = (M // tm) * (N // tn) * nk
                cost = (a_traffic + b_traffic + M * N * (out_bytes + extra_bytes)) / HBM_BYTES_PER_US + steps * STEP_US
                cost += (M // tm) * (N // tn) * (nk - 1) * tm * tn * 8 / (4 * HBM_BYTES_PER_US)
                if best is None or cost < best[0]:
                    best = (cost, tm, tn, tk)
    assert best is not None, (M, N, K)
    return best[1:]


def _matmul(a, b, *, ta=False, tb=False, out_dtype=F32, bias=None, resid=None, mul=None, mul_scale=1.0,
            relu2=False, rope=None, rope_cols=0, norm_out=None, norm_bwd=None, name="mm", plan=None):
    M = a.shape[1] if ta else a.shape[0]
    K = a.shape[0] if ta else a.shape[1]
    N = b.shape[0] if tb else b.shape[1]
    assert (b.shape[1] if tb else b.shape[0]) == K
    two_out = relu2 or norm_out is not None
    out_bytes = jnp.dtype(out_dtype).itemsize * (2 if relu2 else 1) + (2 if norm_out is not None else 0)
    extra_bytes = (4 if resid is not None else 0) + (mul.dtype.itemsize if mul is not None else 0) + (8 if norm_bwd else 0)
    rows = norm_out is not None or norm_bwd is not None
    tm, tn, tk = _mm_tiles(M, N, K, a.dtype.itemsize, b.dtype.itemsize, out_bytes, extra_bytes, full_rows=rows)
    nk = K // tk
    dims = (((0 if ta else 1,), (1 if tb else 0,)), ((), ()))

    def body(*refs):
        it = iter(refs)
        a_ref, b_ref = next(it), next(it)
        bias_ref = next(it) if bias is not None else None
        resid_ref = next(it) if resid is not None else None
        mul_ref = next(it) if mul is not None else None
        cos_ref, sin_ref = (next(it), next(it)) if rope is not None else (None, None)
        nw_ref = next(it) if rows else None
        h_ref, dres_ref = (next(it), next(it)) if norm_bwd is not None else (None, None)
        o_ref = next(it)
        o2_ref = next(it) if two_out or norm_bwd is not None else None
        acc_ref = next(it) if nk > 1 else None
        k = pl.program_id(2)
        part = lax.dot_general(_bf(a_ref[...]), _bf(b_ref[...]), dims, preferred_element_type=F32)
        if nk > 1:
            @pl.when(k == 0)
            def _():
                acc_ref[...] = part

            @pl.when(k > 0)
            def _():
                acc_ref[...] += part

        @pl.when(k == nk - 1)
        def _():
            y = acc_ref[...] if nk > 1 else part
            if bias_ref is not None:
                y = y + bias_ref[...]
            if rope is not None and rope_cols % tn == 0 and not (two_out or rows or mul is not None or resid is not None):
                rotated = pl.program_id(1) * tn < rope_cols

                @pl.when(rotated)
                def _():
                    o_ref[...] = _rope(y, cos_ref[...], sin_ref[...], 1.0).astype(o_ref.dtype)

                @pl.when(jnp.logical_not(rotated))
                def _():
                    o_ref[...] = y.astype(o_ref.dtype)
                return
            if rope is not None:
                col = pl.program_id(1) * tn + lax.broadcasted_iota(jnp.int32, y.shape, 1)
                y = jnp.where(col < rope_cols, _rope(y, cos_ref[...], sin_ref[...], 1.0), y)
            if mul_ref is not None:
                y = y * (mul_ref[...].astype(F32) * mul_scale)
            if resid_ref is not None:
                y = y + resid_ref[...]
            if relu2:
                r = jnp.maximum(y, 0.0)
                o_ref[...] = r.astype(o_ref.dtype)
                o2_ref[...] = (r * r).astype(o2_ref.dtype)
            elif norm_bwd is not None:
                x = h_ref[...]
                rstd = lax.rsqrt(jnp.mean(x * x, axis=-1, keepdims=True) + NORM_EPS)
                g = y * nw_ref[...]
                o_ref[...] = dres_ref[...] + rstd * g - x * (rstd * rstd * rstd) * jnp.mean(g * x, axis=-1, keepdims=True)
                dw = jnp.sum(y * x * rstd, axis=0, keepdims=True)
                first = pl.program_id(0) == 0

                @pl.when(first)
                def _():
                    o2_ref[...] = dw

                @pl.when(jnp.logical_not(first))
                def _():
                    o2_ref[...] += dw
            else:
                o_ref[...] = y.astype(o_ref.dtype)
                if norm_out is not None:
                    rstd = lax.rsqrt(jnp.mean(y * y, axis=-1, keepdims=True) + NORM_EPS)
                    o2_ref[...] = (y * rstd * nw_ref[...]).astype(BF16)

    a_spec = pl.BlockSpec((tk, tm), lambda i, j, k: (k, i)) if ta else pl.BlockSpec((tm, tk), lambda i, j, k: (i, k))
    b_spec = pl.BlockSpec((tn, tk), lambda i, j, k: (j, k)) if tb else pl.BlockSpec((tk, tn), lambda i, j, k: (k, j))
    mn_spec = pl.BlockSpec((tm, tn), lambda i, j, k: (i, j))
    in_specs, args = [a_spec, b_spec], [a, b]
    if bias is not None:
        in_specs.append(pl.BlockSpec((1, tn), lambda i, j, k: (0, j)))
        args.append(bias)
    if resid is not None:
        in_specs.append(mn_spec)
        args.append(resid)
    if mul is not None:
        in_specs.append(mn_spec)
        args.append(mul)
    if rope is not None:
        in_specs += [pl.BlockSpec((tm, LANES), lambda i, j, k: (i, 0))] * 2
        args += [rope[0], rope[1]]
    vec_spec = pl.BlockSpec((1, tn), lambda i, j, k: (0, j))
    if rows:
        in_specs.append(vec_spec)
        args.append((norm_out if norm_out is not None else norm_bwd[1]).reshape(1, N))
    if norm_bwd is not None:
        in_specs += [mn_spec, mn_spec]
        args += [norm_bwd[0], norm_bwd[2]]
    out_shape = SDS((M, N), out_dtype)
    out_specs = mn_spec
    if relu2:
        out_shape, out_specs = (out_shape, out_shape), (mn_spec, mn_spec)
    elif norm_out is not None:
        out_shape, out_specs = (out_shape, SDS((M, N), BF16)), (mn_spec, mn_spec)
    elif norm_bwd is not None:
        out_shape, out_specs = (out_shape, SDS((1, N), F32)), (mn_spec, vec_spec)
    sem = ("arbitrary",) * 3 if norm_bwd is not None else ("parallel", "parallel", "arbitrary")
    return _hosted(plan, name, lambda comm: _pc(
        body, args, out_shape=out_shape, grid=(M // tm, N // tn, nk), in_specs=in_specs, out_specs=out_specs,
        scratch_shapes=[pltpu.VMEM((tm, tn), F32)] if nk > 1 else [], name=name, sem=sem, comm=comm))


def _colsum(x, name):
    T, N = x.shape
    tm = _pick(T, 1024, 8)

    def body(x_ref, o_ref):
        s = jnp.sum(x_ref[...].astype(F32), axis=0, keepdims=True)

        @pl.when(pl.program_id(0) == 0)
        def _():
            o_ref[...] = s

        @pl.when(pl.program_id(0) > 0)
        def _():
            o_ref[...] += s

    return pl.pallas_call(
        body, out_shape=SDS((1, N), F32), grid=(T // tm,),
        in_specs=[pl.BlockSpec((tm, N), lambda i: (i, 0))], out_specs=pl.BlockSpec((1, N), lambda i: (0, 0)),
        name=name, compiler_params=_params(("arbitrary",)),
    )(x)


def _rmsnorm_fwd(h, w, name):
    T, D = h.shape
    tm = _pick(T, 512, 8)

    def body(h_ref, w_ref, o_ref):
        x = h_ref[...]
        rstd = lax.rsqrt(jnp.mean(x * x, axis=-1, keepdims=True) + NORM_EPS)
        o_ref[...] = (x * rstd * w_ref[...]).astype(BF16)

    return pl.pallas_call(
        body, out_shape=SDS((T, D), BF16), grid=(T // tm,),
        in_specs=[pl.BlockSpec((tm, D), lambda i: (i, 0)), pl.BlockSpec((1, D), lambda i: (0, 0))],
        out_specs=pl.BlockSpec((tm, D), lambda i: (i, 0)), name=name, compiler_params=_params(("parallel",)),
    )(h, w.reshape(1, D))


def _final_loss(h, target, w):
    T, D = h.shape
    tm = _pick(T, 512, 8)

    def body(h_ref, t_ref, w_ref, dh_ref, dw_ref, loss_ref):
        x = h_ref[...]
        rstd = lax.rsqrt(jnp.mean(x * x, axis=-1, keepdims=True) + NORM_EPS)
        xn = x * rstd
        err = xn * w_ref[...] - t_ref[...]
        part = 0.5 * jnp.sum(jnp.mean(err * err, axis=-1, keepdims=True), axis=0, keepdims=True)
        dy = err * (1.0 / D)
        g = dy * w_ref[...]
        dh_ref[...] = rstd * g - x * (rstd * rstd * rstd) * jnp.mean(g * x, axis=-1, keepdims=True)
        dw = jnp.sum(dy * xn, axis=0, keepdims=True)
        lp = jnp.broadcast_to(part, (1, LANES))

        @pl.when(pl.program_id(0) == 0)
        def _():
            dw_ref[...] = dw
            loss_ref[...] = lp

        @pl.when(pl.program_id(0) > 0)
        def _():
            dw_ref[...] += dw
            loss_ref[...] += lp

    row = pl.BlockSpec((tm, D), lambda i: (i, 0))
    vec = pl.BlockSpec((1, D), lambda i: (0, 0))
    return pl.pallas_call(
        body, out_shape=(SDS((T, D), F32), SDS((1, D), F32), SDS((1, LANES), F32)), grid=(T // tm,),
        in_specs=[row, row, vec], out_specs=(row, vec, pl.BlockSpec((1, LANES), lambda i: (0, 0))),
        name="final_loss", compiler_params=_params(("arbitrary",)),
    )(h, target, w.reshape(1, D))


def _band_mask(i_blk, max_dist, first_ok):
    qi = lax.broadcasted_iota(jnp.int32, (BLOCK, 2 * BLOCK), 0)
    kj = lax.broadcasted_iota(jnp.int32, (BLOCK, 2 * BLOCK), 1)
    dist = qi + BLOCK - kj
    ok = (dist >= 0) & (dist <= max_dist)
    return ok & ((kj >= BLOCK) | first_ok)


def _pair(t, i):
    return t[:, LANES * i:LANES * (i + 1)]


def _low_half(shape):
    return lax.broadcasted_iota(jnp.int32, shape, len(shape) - 1) < HEAD_DIM


def _stack_heads(t):
    lo = _low_half(t.shape)
    z = jnp.zeros_like(t)
    return jnp.concatenate([jnp.where(lo, t, z), jnp.where(lo, z, t)], axis=0)


def _swap_halves(t):
    return jnp.concatenate([t[:, HEAD_DIM:], t[:, :HEAD_DIM]], axis=1)


def _kv_operand(kv, kv_swapped, h0, n_kv, n_heads):
    R = n_heads // n_kv
    if R == 1:
        return _pair(kv, h0 // 2)
    assert kv.shape[1] == LANES and R % 2 == 0, "grouped queries: one 128-lane tile of kv heads, both heads of a pair in one group"
    g = h0 // R
    t, ts = _pair(kv, g // 2), _pair(kv_swapped, g // 2)
    lo = _low_half(t.shape)
    return jnp.where(lo, t, ts) if g % 2 == 0 else jnp.where(lo, ts, t)


def _lane_place(cols):
    m = cols[0].shape[0]
    lane = lax.broadcasted_iota(jnp.int32, (m, LANES), 1)
    out = jnp.zeros((m, LANES), F32)
    for h, c in enumerate(cols):
        out = jnp.where(lane == h, c, out)
    return out


def _attn_specs(B, S, d, C, n_heads, n_kv, q_col, k_col, v_col):
    kvw = n_kv * HEAD_DIM
    qw = n_heads * HEAD_DIM
    cq, ck = (C // qw if d > 1 else 0), (C // kvw if d > 1 else 0)
    q_spec = pl.BlockSpec((1, BLOCK, qw), lambda b, r, i: (b, i, r * cq + q_col // qw))
    kc = pl.BlockSpec((1, BLOCK, kvw), lambda b, r, i: (b, i, r * ck + k_col // kvw))
    kp = pl.BlockSpec((1, BLOCK, kvw), lambda b, r, i: (b, jnp.maximum(i - 1, 0), r * ck + k_col // kvw))
    vc = pl.BlockSpec((1, BLOCK, kvw), lambda b, r, i: (b, i, r * ck + v_col // kvw))
    vp = pl.BlockSpec((1, BLOCK, kvw), lambda b, r, i: (b, jnp.maximum(i - 1, 0), r * ck + v_col // kvw))
    return q_spec, kp, kc, vp, vc


def _attn_fwd(qkv, B, S, d, *, n_heads, n_kv, q_col, k_col, v_col, max_dist, sinks, name, plan=None):
    C = qkv.shape[1]
    Ls = S // d
    nb = Ls // BLOCK
    qw = n_heads * HEAD_DIM
    R = n_heads // n_kv
    qkv3 = qkv.reshape(B, Ls, d * C)
    scale = HEAD_DIM ** -0.5

    def body(*refs):
        if sinks is not None:
            sink_ref, q_ref, kp_ref, kc_ref, vp_ref, vc_ref, o_ref, lse_ref = refs
        else:
            q_ref, kp_ref, kc_ref, vp_ref, vc_ref, o_ref, lse_ref = refs
        i = pl.program_id(2)
        mask1 = _band_mask(i, max_dist, i > 0)
        mask = jnp.concatenate([mask1, mask1], axis=0)
        q = q_ref[0]
        kk = jnp.concatenate([kp_ref[0], kc_ref[0]], axis=0)
        vv = jnp.concatenate([vp_ref[0], vc_ref[0]], axis=0)
        kks, vvs = (_swap_halves(kk), _swap_halves(vv)) if R > 1 else (None, None)
        lo = _low_half((BLOCK, LANES))
        top = lax.broadcasted_iota(jnp.int32, (2 * BLOCK, 1), 0) < BLOCK
        lses, tiles = [], []
        for t in range(n_heads // 2):
            k2 = _kv_operand(kk, kks, 2 * t, n_kv, n_heads)
            v2 = _kv_operand(vv, vvs, 2 * t, n_kv, n_heads)
            s = lax.dot_general(_stack_heads(_pair(q, t)), k2, NT, preferred_element_type=F32) * scale
            s = jnp.where(mask, s, NEG)
            m = jnp.max(s, axis=-1, keepdims=True)
            if sinks is not None:
                sk = jnp.where(top, sink_ref[2 * t], sink_ref[2 * t + 1])
                m = jnp.maximum(m, sk)
            p = jnp.exp(s - m)
            den = jnp.sum(p, axis=-1, keepdims=True)
            if sinks is not None:
                den = den + jnp.exp(sk - m)
            lse2 = m + jnp.log(den)
            o2 = jnp.dot((p / den).astype(BF16), v2, preferred_element_type=F32)
            tiles.append(jnp.where(lo, o2[:BLOCK], o2[BLOCK:]))
            lses += [lse2[:BLOCK], lse2[BLOCK:]]
        o_ref[0] = jnp.concatenate(tiles, axis=-1)
        lse_ref[0] = _lane_place(lses)

    specs = list(_attn_specs(B, S, d, C, n_heads, n_kv, q_col, k_col, v_col))
    args = [qkv3] * 5
    if sinks is not None:
        specs = [pl.BlockSpec(memory_space=pltpu.SMEM)] + specs
        args = [sinks] + args
    o3, lse3 = _hosted(plan, name, lambda comm: _pc(
        body, args, out_shape=(SDS((B, Ls, d * qw), F32), SDS((B, Ls, d * LANES), F32)), grid=(B, d, nb), in_specs=specs,
        out_specs=(pl.BlockSpec((1, BLOCK, qw), lambda b, r, i: (b, i, r)), pl.BlockSpec((1, BLOCK, LANES), lambda b, r, i: (b, i, r))),
        name=name, sem=("parallel", "parallel", "parallel"), comm=comm))
    return o3.reshape(B * S, qw), lse3.reshape(B * S, LANES)


def _attn_dq(qkv, do, lse, delta, cos, sin, B, S, d, *, n_heads, n_kv, q_col, k_col, v_col, max_dist, name, plan=None):
    C = qkv.shape[1]
    Ls = S // d
    nb = Ls // BLOCK
    qw = n_heads * HEAD_DIM
    R = n_heads // n_kv
    scale = HEAD_DIM ** -0.5

    def body(q_ref, kp_ref, kc_ref, vp_ref, vc_ref, do_ref, lse_ref, dl_ref, cos_ref, sin_ref, dq_ref):
        i = pl.program_id(2)
        mask1 = _band_mask(i, max_dist, i > 0)
        mask = jnp.concatenate([mask1, mask1], axis=0)
        q = q_ref[0]
        do_ = do_ref[0]
        kk = jnp.concatenate([kp_ref[0], kc_ref[0]], axis=0)
        vv = jnp.concatenate([vp_ref[0], vc_ref[0]], axis=0)
        kks, vvs = (_swap_halves(kk), _swap_halves(vv)) if R > 1 else (None, None)
        lo = _low_half((BLOCK, LANES))
        lse_t, dl_t = lse_ref[0], dl_ref[0]
        tiles = []
        for t in range(n_heads // 2):
            k2 = _kv_operand(kk, kks, 2 * t, n_kv, n_heads)
            v2 = _kv_operand(vv, vvs, 2 * t, n_kv, n_heads)
            lse2 = jnp.concatenate([lse_t[:, 2 * t:2 * t + 1], lse_t[:, 2 * t + 1:2 * t + 2]], axis=0)
            dl2 = jnp.concatenate([dl_t[:, 2 * t:2 * t + 1], dl_t[:, 2 * t + 1:2 * t + 2]], axis=0)
            s = lax.dot_general(_stack_heads(_pair(q, t)), k2, NT, preferred_element_type=F32) * scale
            p = jnp.where(mask, jnp.exp(s - lse2), 0.0)
            dp = lax.dot_general(_stack_heads(_pair(do_, t)), v2, NT, preferred_element_type=F32)
            ds = p * (dp - dl2)
            dq2 = jnp.dot(ds.astype(BF16), k2, preferred_element_type=F32) * scale
            tiles.append(jnp.where(lo, dq2[:BLOCK], dq2[BLOCK:]))
        dq = jnp.concatenate(tiles, axis=-1)
        dq_ref[0] = _rope(dq, cos_ref[0], sin_ref[0], -1.0).astype(BF16)

    qs, kp, kc, vp, vc = _attn_specs(B, S, d, C, n_heads, n_kv, q_col, k_col, v_col)
    row_q = pl.BlockSpec((1, BLOCK, qw), lambda b, r, i: (b, i, r))
    row_l = pl.BlockSpec((1, BLOCK, LANES), lambda b, r, i: (b, i, r))
    qkv3 = qkv.reshape(B, Ls, d * C)
    v3 = lambda t, w: t.reshape(B, Ls, d * w)
    args = (qkv3, qkv3, qkv3, qkv3, qkv3, v3(do, qw), v3(lse, LANES), v3(delta, LANES), v3(cos, LANES), v3(sin, LANES))
    dq3 = _hosted(plan, name, lambda comm: _pc(
        body, args, out_shape=SDS((B, Ls, d * qw), BF16), grid=(B, d, nb),
        in_specs=[qs, kp, kc, vp, vc, row_q, row_l, row_l, row_l, row_l], out_specs=row_q,
        name=name, sem=("parallel", "parallel", "parallel"), comm=comm))
    return dq3.reshape(B * S, qw)


def _attn_dkv(qkv, do, lse, delta, cos, sin, B, S, d, *, n_heads, n_kv, q_col, k_col, v_col, max_dist, name, plan=None):
    C = qkv.shape[1]
    Ls = S // d
    nb = Ls // BLOCK
    qw = n_heads * HEAD_DIM
    kvw = n_kv * HEAD_DIM
    R = n_heads // n_kv
    scale = HEAD_DIM ** -0.5
    cq, ck = (C // qw if d > 1 else 0), (C // kvw if d > 1 else 0)

    def body(k_ref, v_ref, q0_ref, q1_ref, do0_ref, do1_ref, lse0_ref, lse1_ref, dl0_ref, dl1_ref, cos_ref, sin_ref,
             dk_ref, dv_ref):
        j = pl.program_id(2)
        kj = lax.broadcasted_iota(jnp.int32, (BLOCK, BLOCK), 0)
        qi = lax.broadcasted_iota(jnp.int32, (BLOCK, BLOCK), 1)
        dist0 = qi - kj
        dist1 = qi + BLOCK - kj
        mask0 = (dist0 >= 0) & (dist0 <= max_dist)
        mask1 = (dist1 <= max_dist) & (j + 1 < nb)
        kb, vb = k_ref[0], v_ref[0]
        kbs, vbs = (_swap_halves(kb), _swap_halves(vb)) if R > 1 else (None, None)
        sides = ((q0_ref[0], do0_ref[0], lse0_ref[0].T, dl0_ref[0].T, mask0), (q1_ref[0], do1_ref[0], lse1_ref[0].T, dl1_ref[0].T, mask1))
        n_acc = n_kv if R > 1 else n_kv // 2
        dks = [jnp.zeros((BLOCK, LANES), F32) for _ in range(n_acc)]
        dvs = [jnp.zeros((BLOCK, LANES), F32) for _ in range(n_acc)]
        for t in range(n_heads // 2):
            k2 = _kv_operand(kb, kbs, 2 * t, n_kv, n_heads)
            v2 = _kv_operand(vb, vbs, 2 * t, n_kv, n_heads)
            a = (2 * t) // R if R > 1 else t
            for (q, do_, lse_r, dl_r, mask) in sides:
                q2, do2 = _stack_heads(_pair(q, t)), _stack_heads(_pair(do_, t))
                s = lax.dot_general(k2, q2, NT, preferred_element_type=F32) * scale
                dp = lax.dot_general(v2, do2, NT, preferred_element_type=F32)
                ps, dss = [], []
                for half in (0, 1):
                    h = 2 * t + half
                    sl = slice(BLOCK * half, BLOCK * (half + 1))
                    p = jnp.where(mask, jnp.exp(s[:, sl] - lse_r[h:h + 1, :]), 0.0)
                    ps.append(p)
                    dss.append(p * (dp[:, sl] - dl_r[h:h + 1, :]))
                dvs[a] = dvs[a] + jnp.dot(jnp.concatenate(ps, axis=1).astype(BF16), do2, preferred_element_type=F32)
                dks[a] = dks[a] + jnp.dot(jnp.concatenate(dss, axis=1).astype(BF16), q2, preferred_element_type=F32)
        if R > 1:
            lo = _low_half((BLOCK, LANES))
            fold = lambda x: x + pltpu.roll(x, HEAD_DIM, 1)
            dks = [jnp.where(lo, fold(dks[2 * t]), fold(dks[2 * t + 1])) for t in range(n_kv // 2)]
            dvs = [jnp.where(lo, fold(dvs[2 * t]), fold(dvs[2 * t + 1])) for t in range(n_kv // 2)]
        dk_t = jnp.concatenate(dks, axis=-1) * scale
        dk_ref[0] = _rope(dk_t, cos_ref[0], sin_ref[0], -1.0).astype(BF16)
        dv_ref[0] = jnp.concatenate(dvs, axis=-1).astype(BF16)

    nxt = lambda j: jnp.minimum(j + 1, nb - 1)
    k_spec = pl.BlockSpec((1, BLOCK, kvw), lambda b, r, j: (b, j, r * ck + k_col // kvw))
    v_spec = pl.BlockSpec((1, BLOCK, kvw), lambda b, r, j: (b, j, r * ck + v_col // kvw))
    q0 = pl.BlockSpec((1, BLOCK, qw), lambda b, r, j: (b, j, r * cq + q_col // qw))
    q1 = pl.BlockSpec((1, BLOCK, qw), lambda b, r, j: (b, nxt(j), r * cq + q_col // qw))
    w0 = lambda w: pl.BlockSpec((1, BLOCK, w), lambda b, r, j: (b, j, r))
    w1 = lambda w: pl.BlockSpec((1, BLOCK, w), lambda b, r, j: (b, nxt(j), r))
    qkv3 = qkv.reshape(B, Ls, d * C)
    v3 = lambda t, w: t.reshape(B, Ls, d * w)
    do3, lse3, dl3 = v3(do, qw), v3(lse, LANES), v3(delta, LANES)
    args = (qkv3, qkv3, qkv3, qkv3, do3, do3, lse3, lse3, dl3, dl3, v3(cos, LANES), v3(sin, LANES))
    dk3, dv3 = _hosted(plan, name, lambda comm: _pc(
        body, args, out_shape=(SDS((B, Ls, d * kvw), BF16), SDS((B, Ls, d * kvw), BF16)), grid=(B, d, nb),
        in_specs=[k_spec, v_spec, q0, q1, w0(qw), w1(qw), w0(LANES), w1(LANES), w0(LANES), w1(LANES), w0(LANES), w0(LANES)],
        out_specs=(w0(kvw), w0(kvw)), name=name, sem=("parallel", "parallel", "parallel"), comm=comm))
    return dk3.reshape(B * S, kvw), dv3.reshape(B * S, kvw)


def _head_expand():
    r = lax.broadcasted_iota(jnp.int32, (LANES, C_HEADS * HEAD_DIM), 0)
    c = lax.broadcasted_iota(jnp.int32, (LANES, C_HEADS * HEAD_DIM), 1)
    return jnp.where(c // HEAD_DIM == r, 1.0, 0.0).astype(F32)


def _delta(do, o, lse=None, sinks_row=None, name="delta"):
    T, W = do.shape
    tm = _pick(T, 512, 8)
    with_sink = sinks_row is not None

    def body(*refs):
        if with_sink:
            do_ref, o_ref, lse_ref, sk_ref, dl_ref, dob_ref, ds_ref = refs
        else:
            do_ref, o_ref, dl_ref, dob_ref = refs
        do_ = do_ref[...]
        dl = lax.dot_general(do_ * o_ref[...], _head_expand(), NT, preferred_element_type=F32, precision=HI)
        dl_ref[...] = dl
        dob_ref[...] = do_.astype(BF16)
        if with_sink:
            lane = lax.broadcasted_iota(jnp.int32, dl.shape, 1)
            contrib = jnp.where(lane < A_N_HEADS, -jnp.exp(sk_ref[...] - lse_ref[...]) * dl, 0.0)
            part = jnp.sum(contrib, axis=0, keepdims=True)

            @pl.when(pl.program_id(0) == 0)
            def _():
                ds_ref[...] = part

            @pl.when(pl.program_id(0) > 0)
            def _():
                ds_ref[...] += part

    row_w = pl.BlockSpec((tm, W), lambda i: (i, 0))
    row_l = pl.BlockSpec((tm, LANES), lambda i: (i, 0))
    vec_l = pl.BlockSpec((1, LANES), lambda i: (0, 0))
    if with_sink:
        return pl.pallas_call(
            body, out_shape=(SDS((T, LANES), F32), SDS((T, W), BF16), SDS((1, LANES), F32)), grid=(T // tm,),
            in_specs=[row_w, row_w, row_l, vec_l], out_specs=(row_l, row_w, vec_l), name=name,
            compiler_params=_params(("arbitrary",)),
        )(do, o, lse, sinks_row)
    return pl.pallas_call(
        body, out_shape=(SDS((T, LANES), F32), SDS((T, W), BF16)), grid=(T // tm,),
        in_specs=[row_w, row_w], out_specs=(row_l, row_w), name=name, compiler_params=_params(("parallel",)),
    )(do, o)


def _merge(os_, lses):
    T, W = os_[0].shape
    tm = _pick(T, 512, 8)

    def body(o0, o1, o2, l0, l1, l2, o_ref, lse_ref):
        ls = [l0[...], l1[...], l2[...]]
        m = jnp.maximum(jnp.maximum(ls[0], ls[1]), ls[2])
        ws = [jnp.exp(l - m) for l in ls]
        tot = ws[0] + ws[1] + ws[2]
        lse_ref[...] = m + jnp.log(tot)
        e = _head_expand()
        acc = jnp.zeros((tm, W), F32)
        for w, o in zip(ws, (o0, o1, o2)):
            acc = acc + jnp.dot(w / tot, e, preferred_element_type=F32, precision=HI) * o[...]
        o_ref[...] = acc

    row_w = pl.BlockSpec((tm, W), lambda i: (i, 0))
    row_l = pl.BlockSpec((tm, LANES), lambda i: (i, 0))
    return pl.pallas_call(
        body, out_shape=(SDS((T, W), F32), SDS((T, LANES), F32)), grid=(T // tm,),
        in_specs=[row_w] * 3 + [row_l] * 3, out_specs=(row_w, row_l), name="c_merge", compiler_params=_params(("parallel",)),
    )(*os_, *lses)


CONV_TC = 256


def _conv_pre(x, w, bias):
    row = lax.broadcasted_iota(jnp.int32, x.shape, 0)
    acc = x * w[SSM_CONV - 1:SSM_CONV, :] + bias
    for k in range(1, SSM_CONV):
        acc = acc + jnp.where(row >= k, pltpu.roll(x, k, 0), 0.0) * w[SSM_CONV - 1 - k:SSM_CONV - k, :]
    return acc


def _conv_fwd(zx3, w, bias):
    B, S, _ = zx3.shape
    off = SSM_D_INNER // CONV_TC

    def body(x_ref, w_ref, b_ref, o_ref):
        v = _conv_pre(x_ref[0], w_ref[...], b_ref[...])
        o_ref[0] = v * jax.nn.sigmoid(v)

    return pl.pallas_call(
        body, out_shape=SDS((B, S, SSM_CONV_DIM), F32), grid=(B, SSM_CONV_DIM // CONV_TC),
        in_specs=[pl.BlockSpec((1, S, CONV_TC), lambda b, j: (b, 0, j + off)),
                  pl.BlockSpec((SSM_CONV, CONV_TC), lambda b, j: (0, j)), pl.BlockSpec((1, CONV_TC), lambda b, j: (0, j))],
        out_specs=pl.BlockSpec((1, S, CONV_TC), lambda b, j: (b, 0, j)), name="b_conv_fwd",
        compiler_params=_params(("parallel", "parallel")),
    )(zx3, w, bias)


def _conv_bwd(zx3, dxc, w, bias, col0, name):
    B, S, n = dxc.shape
    tc = _pick(n, CONV_TC)
    off_x = (SSM_D_INNER + col0) // tc
    off_w = col0 // tc

    def body(x_ref, d_ref, w_ref, b_ref, dx_ref, dw_ref, db_ref):
        x = x_ref[0]
        wv = w_ref[...]
        v = _conv_pre(x, wv, b_ref[...])
        sg = jax.nn.sigmoid(v)
        dc = d_ref[0] * (sg * (1.0 + v * (1.0 - sg)))
        row = lax.broadcasted_iota(jnp.int32, x.shape, 0)
        dx = dc * wv[SSM_CONV - 1:SSM_CONV, :]
        dws = [jnp.sum(dc * x, axis=0, keepdims=True)]
        for k in range(1, SSM_CONV):
            dx = dx + jnp.where(row < S - k, pltpu.roll(dc, S - k, 0), 0.0) * wv[SSM_CONV - 1 - k:SSM_CONV - k, :]
            dws.append(jnp.sum(dc * jnp.where(row >= k, pltpu.roll(x, k, 0), 0.0), axis=0, keepdims=True))
        dx_ref[0] = dx.astype(BF16)
        ridx = lax.broadcasted_iota(jnp.int32, (SSM_CONV, tc), 0)
        dw = jnp.zeros((SSM_CONV, tc), F32)
        for k in range(SSM_CONV):
            dw = jnp.where(ridx == SSM_CONV - 1 - k, dws[k], dw)
        db = jnp.sum(dc, axis=0, keepdims=True)

        @pl.when(pl.program_id(1) == 0)
        def _():
            dw_ref[...] = dw
            db_ref[...] = db

        @pl.when(pl.program_id(1) > 0)
        def _():
            dw_ref[...] += dw
            db_ref[...] += db

    return pl.pallas_call(
        body, out_shape=(SDS((B, S, n), BF16), SDS((SSM_CONV, n), F32), SDS((1, n), F32)), grid=(n // tc, B),
        in_specs=[pl.BlockSpec((1, S, tc), lambda j, b: (b, 0, j + off_x)), pl.BlockSpec((1, S, tc), lambda j, b: (b, 0, j)),
                  pl.BlockSpec((SSM_CONV, tc), lambda j, b: (0, j + off_w)), pl.BlockSpec((1, tc), lambda j, b: (0, j + off_w))],
        out_specs=(pl.BlockSpec((1, S, tc), lambda j, b: (b, 0, j)), pl.BlockSpec((SSM_CONV, tc), lambda j, b: (0, j)),
                   pl.BlockSpec((1, tc), lambda j, b: (0, j))),
        name=name, compiler_params=_params(("parallel", "arbitrary")),
    )(zx3, dxc, w, bias)


def _ssd_common(x, Bm, Cm, dtc_raw, dtr_raw, pr, pc):
    Q = SSM_CHUNK
    zc = dtc_raw + pr[0:1, :]
    dt_c = jax.nn.softplus(zc)
    dt_r = jax.nn.softplus(dtr_raw + pc[:, 0:1])
    A_r = -jnp.exp(pr[1:2, :])
    A_c = -jnp.exp(pc[:, 1:2])
    row = lax.broadcasted_iota(jnp.int32, (Q, Q), 0)
    col = lax.broadcasted_iota(jnp.int32, (Q, Q), 1)
    tril = jnp.where(row >= col, 1.0, 0.0).astype(F32)
    cs_c = jnp.dot(tril, dt_c * A_r, preferred_element_type=F32, precision=HI)
    cs_r = lax.dot_general(dt_r * A_c, tril, NT, preferred_element_type=F32, precision=HI)
    return zc, dt_c, A_r, cs_c, cs_r, row, col, tril


def _ssd_fwd(xc3, dtc, dtr, prow, pcol, plan=None):
    B, S, _ = xc3.shape
    Q, G, HG, P, N = SSM_CHUNK, SSM_N_GROUPS, SSM_HG, HEAD_DIM, SSM_D_STATE
    nc = S // Q
    xw = HG * P

    def body(x_ref, b_ref, c_ref, dtc_ref, dtr_ref, pr_ref, pc_ref, y_ref, st_ref, state):
        c = pl.program_id(2)

        @pl.when(c == 0)
        def _():
            state[...] = jnp.zeros_like(state)

        x, Bm, Cm = x_ref[0], b_ref[0], c_ref[0]
        pr = pr_ref[0]
        _, dt_c, _, cs_c, cs_r, row, col, _ = _ssd_common(x, Bm, Cm, dtc_ref[0, 0], dtr_ref[0, 0], pr, pc_ref[0])
        Bb, Cb = Bm.astype(BF16), Cm.astype(BF16)
        CB = lax.dot_general(Cb, Bb, NT, preferred_element_type=F32)
        ys = []
        for hg in range(HG):
            xh = x[:, P * hg:P * (hg + 1)]
            xt = xh * dt_c[:, hg:hg + 1]
            csc, csr = cs_c[:, hg:hg + 1], cs_r[hg:hg + 1, :]
            L = jnp.where(row >= col, jnp.exp(jnp.minimum(csc - csr, 0.0)), 0.0)
            ydiag = jnp.dot((CB * L).astype(BF16), xt.astype(BF16), preferred_element_type=F32)
            Sh = state[hg]
            yoff = lax.dot_general(Cb, Sh.astype(BF16), NT, preferred_element_type=F32) * jnp.exp(csc)
            ys.append(ydiag + yoff + pr[2:3, hg:hg + 1] * xh)
            st_ref[0, 0, 0, P * hg:P * (hg + 1), :] = Sh
            csq = csc[Q - 1:Q, :]
            upd = lax.dot_general((xt * jnp.exp(csq - csc)).astype(BF16), Bb, TN, preferred_element_type=F32)
            state[hg] = Sh * jnp.exp(csq) + upd
        y_ref[0] = jnp.concatenate([jnp.concatenate(ys[0:2], axis=-1), jnp.concatenate(ys[2:4], axis=-1)], axis=-1)

    bo, co = SSM_D_INNER // N, (SSM_D_INNER + SSM_BC_DIM) // N
    return _hosted(plan, "b_ssd_fwd", lambda comm: _pc(
        body, (xc3, xc3, xc3, dtc, dtr, prow, pcol),
        out_shape=(SDS((B, S, SSM_D_INNER), F32), SDS((B, G, nc, xw, N), F32)), grid=(G, B, nc),
        in_specs=[pl.BlockSpec((1, Q, xw), lambda g, b, c: (b, c, g)), pl.BlockSpec((1, Q, N), lambda g, b, c: (b, c, bo + g)),
                  pl.BlockSpec((1, Q, N), lambda g, b, c: (b, c, co + g)), pl.BlockSpec((1, 1, Q, HG), lambda g, b, c: (b, g, c, 0)),
                  pl.BlockSpec((1, 1, HG, Q), lambda g, b, c: (b, g, 0, c)), pl.BlockSpec((1, 3, HG), lambda g, b, c: (g, 0, 0)),
                  pl.BlockSpec((1, HG, 3), lambda g, b, c: (g, 0, 0))],
        out_specs=(pl.BlockSpec((1, Q, xw), lambda g, b, c: (b, c, g)), pl.BlockSpec((1, 1, 1, xw, N), lambda g, b, c: (b, g, c, 0, 0))),
        scratch_shapes=[pltpu.VMEM((HG, P, N), F32)], name="b_ssd_fwd", sem=("parallel", "arbitrary", "arbitrary"), comm=comm))


def _ssd_bwd(xc3, dtc, dtr, prow, pcol, states, dy3, plan=None):
    B, S, _ = xc3.shape
    Q, G, HG, P, N = SSM_CHUNK, SSM_N_GROUPS, SSM_HG, HEAD_DIM, SSM_D_STATE
    nc = S // Q
    xw = HG * P

    def body(x_ref, b_ref, c_ref, dtc_ref, dtr_ref, pr_ref, pc_ref, st_ref, dy_ref,
             dx_ref, db_ref, dc_ref, ddt_ref, dpar_ref, dstate):
        bi, ci = pl.program_id(1), pl.program_id(2)

        @pl.when(ci == 0)
        def _():
            dstate[...] = jnp.zeros_like(dstate)

        x, Bm, Cm, dy = x_ref[0], b_ref[0], c_ref[0], dy_ref[0]
        pr = pr_ref[0]
        zc, dt_c, A_r, cs_c, cs_r, row, col, tril = _ssd_common(x, Bm, Cm, dtc_ref[0, 0], dtr_ref[0, 0], pr, pc_ref[0])
        Bb, Cb = Bm.astype(BF16), Cm.astype(BF16)
        CB = lax.dot_general(Cb, Bb, NT, preferred_element_type=F32)
        CBt = lax.dot_general(Bb, Cb, NT, preferred_element_type=F32)
        lane4 = lax.broadcasted_iota(jnp.int32, (Q, HG), 1)
        lane4r = lax.broadcasted_iota(jnp.int32, (1, HG), 1)
        rowq = lax.broadcasted_iota(jnp.int32, (Q, 1), 0)
        dB = jnp.zeros((Q, N), F32)
        dC = jnp.zeros((Q, N), F32)
        dcs4 = jnp.zeros((Q, HG), F32)
        dtx4 = jnp.zeros((Q, HG), F32)
        dD4 = jnp.zeros((1, HG), F32)
        dxts, xhs, dyhs = [], [], []
        for hg in range(HG):
            xh = x[:, P * hg:P * (hg + 1)]
            dyh = dy[:, P * hg:P * (hg + 1)]
            xt = xh * dt_c[:, hg:hg + 1]
            xtb, dyb = xt.astype(BF16), dyh.astype(BF16)
            csc, csr = cs_c[:, hg:hg + 1], cs_r[hg:hg + 1, :]
            L = jnp.where(row >= col, jnp.exp(jnp.minimum(csc - csr, 0.0)), 0.0)
            Lt = jnp.where(col >= row, jnp.exp(jnp.minimum(csr - csc, 0.0)), 0.0)
            M, Mt = CB * L, CBt * Lt
            Sh = st_ref[0, 0, 0, P * hg:P * (hg + 1), :]
            dSh = dstate[hg]
            Shb, dShb = Sh.astype(BF16), dSh.astype(BF16)
            ecs = jnp.exp(csc)
            csq = csc[Q - 1:Q, :]
            dec = jnp.exp(csq - csc)
            dxt = jnp.dot(Mt.astype(BF16), dyb, preferred_element_type=F32)
            dxt = dxt + lax.dot_general(Bb, dShb, NT, preferred_element_type=F32) * dec
            Gm = lax.dot_general(dyb, xtb, NT, preferred_element_type=F32)
            Gt = lax.dot_general(xtb, dyb, NT, preferred_element_type=F32)
            dC = dC + jnp.dot((Gm * L).astype(BF16), Bb, preferred_element_type=F32)
            dB = dB + jnp.dot((Gt * Lt).astype(BF16), Cb, preferred_element_type=F32)
            dC = dC + jnp.dot(dyb, Shb, preferred_element_type=F32) * ecs
            dBst = jnp.dot(xtb, dShb, preferred_element_type=F32) * dec
            dB = dB + dBst
            dcs = jnp.sum(Gm * M, axis=1, keepdims=True) - jnp.sum(Gt * Mt, axis=1, keepdims=True)
            yoff = lax.dot_general(Cb, Shb, NT, preferred_element_type=F32) * ecs
            dcs = dcs + jnp.sum(yoff * dyh, axis=1, keepdims=True)
            r = jnp.sum(dBst * Bm, axis=1, keepdims=True)
            dcs = dcs - r
            extra = jnp.sum(r, axis=0, keepdims=True) + jnp.exp(csq) * jnp.sum(
                jnp.sum(dSh * Sh, axis=1, keepdims=True), axis=0, keepdims=True)
            dcs = dcs + jnp.where(rowq == Q - 1, extra, 0.0)
            dcs4 = jnp.where(lane4 == hg, dcs, dcs4)
            dtx4 = jnp.where(lane4 == hg, jnp.sum(dxt * xh, axis=1, keepdims=True), dtx4)
            dD4 = jnp.where(lane4r == hg, jnp.sum(jnp.sum(dyh * xh, axis=1, keepdims=True), axis=0, keepdims=True), dD4)
            dstate[hg] = dSh * jnp.exp(csq) + lax.dot_general((dyh * ecs).astype(BF16), Cb, TN, preferred_element_type=F32)
            dxts.append(dxt)
            xhs.append(xh)
            dyhs.append(dyh)
        da4 = lax.dot_general(tril, dcs4, TN, preferred_element_type=F32, precision=HI)
        ddt4 = da4 * A_r + dtx4
        ddtraw = ddt4 * jax.nn.sigmoid(zc)
        ddt_ref[0, 0] = ddtraw
        dxs = [dxts[hg] * dt_c[:, hg:hg + 1] + pr[2:3, hg:hg + 1] * dyhs[hg] for hg in range(HG)]
        dx_ref[0] = jnp.concatenate([jnp.concatenate(dxs[0:2], axis=-1), jnp.concatenate(dxs[2:4], axis=-1)], axis=-1)
        db_ref[0] = dB
        dc_ref[0] = dC
        d_bias = jnp.sum(ddtraw, axis=0, keepdims=True)
        d_alog = jnp.sum(da4 * dt_c, axis=0, keepdims=True) * A_r
        r3 = lax.broadcasted_iota(jnp.int32, (3, HG), 0)
        dpar = jnp.where(r3 == 0, d_bias, jnp.where(r3 == 1, d_alog, dD4))
        first = (bi == 0) & (ci == 0)

        @pl.when(first)
        def _():
            dpar_ref[0] = dpar

        @pl.when(jnp.logical_not(first))
        def _():
            dpar_ref[0] += dpar

    rc = lambda c: nc - 1 - c
    bo, co = SSM_D_INNER // N, (SSM_D_INNER + SSM_BC_DIM) // N
    return _hosted(plan, "b_ssd_bwd", lambda comm: _pc(
        body, (xc3, xc3, xc3, dtc, dtr, prow, pcol, states, dy3),
        out_shape=(SDS((B, S, SSM_D_INNER), F32), SDS((B, S, SSM_BC_DIM), F32), SDS((B, S, SSM_BC_DIM), F32),
                   SDS((B, G, S, HG), F32), SDS((G, 3, HG), F32)),
        grid=(G, B, nc),
        in_specs=[pl.BlockSpec((1, Q, xw), lambda g, b, c: (b, rc(c), g)), pl.BlockSpec((1, Q, N), lambda g, b, c: (b, rc(c), bo + g)),
                  pl.BlockSpec((1, Q, N), lambda g, b, c: (b, rc(c), co + g)), pl.BlockSpec((1, 1, Q, HG), lambda g, b, c: (b, g, rc(c), 0)),
                  pl.BlockSpec((1, 1, HG, Q), lambda g, b, c: (b, g, 0, rc(c))), pl.BlockSpec((1, 3, HG), lambda g, b, c: (g, 0, 0)),
                  pl.BlockSpec((1, HG, 3), lambda g, b, c: (g, 0, 0)),
                  pl.BlockSpec((1, 1, 1, xw, N), lambda g, b, c: (b, g, rc(c), 0, 0)), pl.BlockSpec((1, Q, xw), lambda g, b, c: (b, rc(c), g))],
        out_specs=(pl.BlockSpec((1, Q, xw), lambda g, b, c: (b, rc(c), g)), pl.BlockSpec((1, Q, N), lambda g, b, c: (b, rc(c), g)),
                   pl.BlockSpec((1, Q, N), lambda g, b, c: (b, rc(c), g)), pl.BlockSpec((1, 1, Q, HG), lambda g, b, c: (b, g, rc(c), 0)),
                   pl.BlockSpec((1, 3, HG), lambda g, b, c: (g, 0, 0))),
        scratch_shapes=[pltpu.VMEM((HG, P, N), F32)], name="b_ssd_bwd", sem=("parallel", "arbitrary", "arbitrary"), comm=comm))


GN_W = SSM_D_INNER // SSM_N_GROUPS


def _gate_fwd(y, zx, nw):
    T = y.shape[0]
    tm = _pick(T, 256, 8)

    def body(y_ref, z_ref, w_ref, o_ref):
        z = z_ref[...]
        gt = y_ref[...] * (z * jax.nn.sigmoid(z))
        outs = []
        for k in range(SSM_N_GROUPS):
            gk = gt[:, GN_W * k:GN_W * (k + 1)]
            outs.append(gk * lax.rsqrt(jnp.mean(gk * gk, axis=-1, keepdims=True) + NORM_EPS))
        o_ref[...] = (jnp.concatenate(outs, axis=-1) * w_ref[...]).astype(BF16)

    row = pl.BlockSpec((tm, SSM_D_INNER), lambda i: (i, 0))
    return pl.pallas_call(
        body, out_shape=SDS((T, SSM_D_INNER), BF16), grid=(T // tm,),
        in_specs=[row, row, pl.BlockSpec((1, SSM_D_INNER), lambda i: (0, 0))], out_specs=row, name="b_gate_fwd",
        compiler_params=_params(("parallel",)),
    )(y, zx, nw)


def _gate_bwd(dgn, y, zx, nw):
    T = y.shape[0]
    tm = _pick(T, 256, 8)

    def body(d_ref, y_ref, z_ref, w_ref, dy_ref, dz_ref, dw_ref):
        z, yv, w = z_ref[...], y_ref[...], w_ref[...]
        sg = jax.nn.sigmoid(z)
        sz = z * sg
        gt = yv * sz
        gw = d_ref[...] * w
        dgts, dws = [], []
        for k in range(SSM_N_GROUPS):
            sl = slice(GN_W * k, GN_W * (k + 1))
            gk, gwk = gt[:, sl], gw[:, sl]
            rstd = lax.rsqrt(jnp.mean(gk * gk, axis=-1, keepdims=True) + NORM_EPS)
            dgts.append(rstd * gwk - gk * (rstd * rstd * rstd) * jnp.mean(gwk * gk, axis=-1, keepdims=True))
            dws.append(jnp.sum(d_ref[:, sl] * gk * rstd, axis=0, keepdims=True))
        dgt = jnp.concatenate(dgts, axis=-1)
        dy_ref[...] = dgt * sz
        dz_ref[...] = (dgt * yv * (sg * (1.0 + z * (1.0 - sg)))).astype(BF16)
        dw = jnp.concatenate(dws, axis=-1)

        @pl.when(pl.program_id(0) == 0)
        def _():
            dw_ref[...] = dw

        @pl.when(pl.program_id(0) > 0)
        def _():
            dw_ref[...] += dw

    row = pl.BlockSpec((tm, SSM_D_INNER), lambda i: (i, 0))
    vec = pl.BlockSpec((1, SSM_D_INNER), lambda i: (0, 0))
    return pl.pallas_call(
        body, out_shape=(SDS((T, SSM_D_INNER), F32), SDS((T, SSM_D_INNER), BF16), SDS((1, SSM_D_INNER), F32)), grid=(T // tm,),
        in_specs=[row, row, row, vec], out_specs=(row, row, vec), name="b_gate_bwd", compiler_params=_params(("arbitrary",)),
    )(dgn, y, zx, nw)


N_CHIPS = 4


def _dev_block(ref, kind, j, size):
    if kind == "slot":
        return ref.at[j]
    start = pl.multiple_of(j * size, size)
    nd = len(ref.shape)
    if kind == "col":
        return ref.at[(slice(None),) * (nd - 1) + (pl.ds(start, size),)]
    return ref.at[(slice(None),) * (nd - 2) + (pl.ds(start, size), slice(None))]


def _dma_sems(n, k):
    return [pltpu.SemaphoreType.DMA((n, k)), pltpu.SemaphoreType.DMA((n, k)), pltpu.SemaphoreType.DMA((n, k))]


def _place(shard, layer, kind, full_shape, dev, name):
    k, n = shard.shape[1:]
    tr = _pick(k, 512, 16)
    nb = k // tr

    def body(dev_ref, s_ref, o_ref):
        if kind == "slot":
            o_ref[0] = s_ref[0].astype(BF16)
        else:
            o_ref[...] = s_ref[0].astype(BF16)

    out_spec = {"slot": pl.BlockSpec((1, tr, n), lambda i, d: (d[0], i, 0)),
                "row": pl.BlockSpec((tr, n), lambda i, d: (d[0] * nb + i, 0)),
                "col": pl.BlockSpec((tr, n), lambda i, d: (i, d[0]))}[kind]
    return pl.pallas_call(
        body, out_shape=SDS(full_shape, BF16),
        grid_spec=pltpu.PrefetchScalarGridSpec(
            num_scalar_prefetch=1, grid=(nb,), in_specs=[pl.BlockSpec((1, tr, n), lambda i, d: (layer, i, 0))], out_specs=out_spec),
        name=name, compiler_params=_params(("arbitrary",)),
    )(dev, shard)


def _run_comm(comm, name):
    c_in = len(comm.inputs)

    def body(*refs):
        cins, couts, sems = refs[:c_in], refs[c_in:c_in + len(comm.out_shapes)], refs[c_in + len(comm.out_shapes):]
        for _, fn in comm.phases:
            fn(cins, couts, sems)

    return pl.pallas_call(
        body, out_shape=list(comm.out_shapes), in_specs=[ANY] * c_in, out_specs=[ANY] * len(comm.out_shapes),
        input_output_aliases=dict(comm.aliases), scratch_shapes=list(comm.sems), name=name,
    )(*comm.inputs)


def _gather_comm(items, mid=0.7):
    n = len(items)

    def tools(srcs, dsts, sems):
        send_sems, recv_sems, local_sems = sems
        px, py, pc = lax.axis_index("x"), lax.axis_index("y"), lax.axis_index("c")
        me, sibling = (px, py, pc), (px, py, 1 - pc)
        chips = [(1 - px, py), (px, 1 - py), (1 - px, 1 - py)]

        def blk(a, dev):
            return _dev_block(dsts[a], items[a][1], 4 * dev[0] + 2 * dev[1] + dev[2], items[a][2])

        def copy(a, k, block, to, src=None):
            return pltpu.make_async_remote_copy(
                src_ref=blk(a, block) if src is None else src, dst_ref=blk(a, block),
                send_sem=send_sems.at[a, k], recv_sem=recv_sems.at[a, k], device_id=to, device_id_type=MESH)

        def mine():
            return [pltpu.make_async_copy(srcs[a], blk(a, me), local_sems.at[a, 0]) for a in range(n) if not items[a][4]]

        def first():
            out = []
            for a in range(n):
                src = blk(a, me) if items[a][4] else srcs[a]
                out.append(copy(a, 0, me, sibling, src=src))
                out += [copy(a, 1 + j, me, (*chip, pc), src=src) for j, chip in enumerate(chips)]
            return out

        def passed():
            return [copy(a, 4 + j, (*chip, pc), sibling) for j, chip in enumerate(chips) for a in range(n)]

        return me, sibling, chips, pc, copy, mine, first, passed

    def start(srcs, dsts, sems):
        *_, mine, first, _ = tools(srcs, dsts, sems)
        for cp in mine() + first():
            cp.start()

    def forward(srcs, dsts, sems):
        me, _, chips, pc, copy, _, _, passed = tools(srcs, dsts, sems)
        fwd = passed()
        for j, chip in enumerate(chips):
            for a in range(n):
                copy(a, 1 + j, (*chip, pc), me).wait_recv()
                fwd[j * n + a].start()

    def finish(srcs, dsts, sems):
        me, sibling, chips, pc, copy, mine, first, passed = tools(srcs, dsts, sems)
        for a in range(n):
            copy(a, 0, sibling, me).wait_recv()
            for j, chip in enumerate(chips):
                copy(a, 4 + j, (*chip, 1 - pc), me).wait_recv()
        for cp in first() + passed():
            cp.wait_send()
        for cp in mine():
            cp.wait()

    return _Comm([it[0] for it in items], [SDS(it[3], it[0].dtype) for it in items],
                 {a: a for a in range(n) if items[a][4]}, _dma_sems(n, 7), [(0.0, start), (mid, forward), (1.0, finish)])


def _gather(items, name):
    return _run_comm(_gather_comm(items), name)


def _reduce_d2d_comm(items):
    n = len(items)

    def copies(gs, gots, sems):
        send_sems, recv_sems, _ = sems
        px, py, pc = lax.axis_index("x"), lax.axis_index("y"), lax.axis_index("c")
        out = []
        for a in range(n):
            _, kind, size, _ = items[a]
            for q in range(N_CHIPS):
                out.append(pltpu.make_async_remote_copy(
                    src_ref=_dev_block(gs[a], kind, 2 * q + 1 - pc, size), dst_ref=gots[a].at[q], send_sem=send_sems.at[a, q],
                    recv_sem=recv_sems.at[a, q], device_id=(px, py, 1 - pc), device_id_type=MESH))
        return out

    def start(gs, gots, sems):
        for cp in copies(gs, gots, sems):
            cp.start()

    def finish(gs, gots, sems):
        for cp in copies(gs, gots, sems):
            cp.wait()

    return _Comm([it[0] for it in items], [SDS((N_CHIPS,) + tuple(it[3]), F32) for it in items], {},
                 _dma_sems(n, N_CHIPS), [(0.0, start), (1.0, finish)])


def _pair_sum(g, got, kind, core, name):
    _, k, n = got.shape
    tr = _pick(k, max(16, STREAM_VMEM // (2 * n * 10)), 16)
    nb = k // tr

    def body(c_ref, g_ref, s_ref, o_ref):
        mine = g_ref[0] if kind == "slot" else g_ref[...]
        o_ref[0] = (mine + s_ref[0]).astype(BF16)

    g_spec = {"slot": pl.BlockSpec((1, tr, n), lambda q, i, c: (2 * q + c[0], i, 0)),
              "row": pl.BlockSpec((tr, n), lambda q, i, c: ((2 * q + c[0]) * nb + i, 0)),
              "col": pl.BlockSpec((tr, n), lambda q, i, c: (i, 2 * q + c[0]))}[kind]
    part = pl.BlockSpec((1, tr, n), lambda q, i, c: (q, i, 0))
    return pl.pallas_call(
        body, out_shape=SDS((N_CHIPS, k, n), BF16),
        grid_spec=pltpu.PrefetchScalarGridSpec(num_scalar_prefetch=1, grid=(N_CHIPS, nb), in_specs=[g_spec, part], out_specs=part),
        name=name, compiler_params=_params(("arbitrary", "arbitrary")),
    )(core, g, got)


def _reduce_ici_comm(parts):
    n = len(parts)

    def copies(ps, rs, sems, arriving):
        send_sems, recv_sems, _ = sems
        px, py, pc = lax.axis_index("x"), lax.axis_index("y"), lax.axis_index("c")
        my_chip = 2 * px + py
        out = []
        for a in range(n):
            for k in range(1, N_CHIPS):
                qx, qy = px ^ (k >> 1), py ^ (k & 1)
                q = 2 * qx + qy
                out.append(pltpu.make_async_remote_copy(
                    src_ref=ps[a].at[q], dst_ref=rs[a].at[q] if arriving else rs[a].at[my_chip], send_sem=send_sems.at[a, k - 1],
                    recv_sem=recv_sems.at[a, k - 1], device_id=(qx, qy, pc), device_id_type=MESH))
        return out

    def start(ps, rs, sems):
        for cp in copies(ps, rs, sems, False):
            cp.start()

    def finish(ps, rs, sems):
        for cp in copies(ps, rs, sems, True):
            cp.wait_recv()
        for cp in copies(ps, rs, sems, False):
            cp.wait_send()

    return _Comm(list(parts), [SDS(p.shape, p.dtype) for p in parts], {}, _dma_sems(n, N_CHIPS - 1),
                 [(0.0, start), (1.0, finish)])


def _adam_update(g, w, m, v):
    c1 = 1.0 - ADAM_B1 ** ADAM_STEP
    c2 = 1.0 - ADAM_B2 ** ADAM_STEP
    nm = ADAM_B1 * m + (1.0 - ADAM_B1) * g
    nv = ADAM_B2 * v + (1.0 - ADAM_B2) * (g * g)
    delta = -ADAM_LR * ((nm / c1) / (jnp.sqrt(nv / c2) + ADAM_EPS) + ADAM_WD * w)
    return delta, nm, nv


def _adamw(parts, recv, w, m, v, layer, prev, chip, name):
    _, R, C = w.shape
    row_bytes = 2 * C * (N_CHIPS * 2 + 7 * 4)
    tr = _pick(R, max(16, STREAM_VMEM // row_bytes), 16)
    n_prev = 0 if prev is None else 4

    def body(ch_ref, own_ref, r1_ref, r2_ref, r3_ref, w_ref, m_ref, v_ref, *rest):
        g_ref, d_ref, nm_ref, nv_ref = rest[n_prev:]
        g = own_ref[0].astype(F32)
        for r_ref in (r1_ref, r2_ref, r3_ref):
            g = g + r_ref[0].astype(F32)
        g_ref[0] = g
        d_ref[0], nm_ref[0], nv_ref[0] = _adam_update(g, w_ref[0], m_ref[0], v_ref[0])

    lay = pl.BlockSpec((1, tr, C), lambda i, ch: (layer, i, 0))
    other = lambda k: pl.BlockSpec((1, tr, C), lambda i, ch: (ch[0] ^ k, i, 0))
    out = SDS(w.shape, F32)
    return pl.pallas_call(
        body, out_shape=(out, out, out, out),
        grid_spec=pltpu.PrefetchScalarGridSpec(
            num_scalar_prefetch=1, grid=(R // tr,),
            in_specs=[pl.BlockSpec((1, tr, C), lambda i, ch: (ch[0], i, 0)), other(2), other(1), other(3), lay, lay, lay]
            + [ANY] * n_prev,
            out_specs=(lay, lay, lay, lay)),
        input_output_aliases={8 + k: k for k in range(n_prev)},
        name=name, compiler_params=_params(("arbitrary",)),
    )(chip, parts, recv, recv, recv, w, m, v, *(prev or ()))


def _small_adamw(gathered, ws, ms, vs):
    n = len(ws)

    def body(*refs):
        g_in, w_in, m_in, v_in = refs[:n], refs[n:2 * n], refs[2 * n:3 * n], refs[3 * n:4 * n]
        outs = refs[4 * n:]
        for i in range(n):
            g = g_in[i][0]
            for dev in range(1, N_DEV):
                g = g + g_in[i][dev]
            d, nm, nv = _adam_update(g, w_in[i][...], m_in[i][...], v_in[i][...])
            outs[i][...] = g
            outs[n + i][...] = d
            outs[2 * n + i][...] = nm
            outs[3 * n + i][...] = nv

    shapes = [SDS(w.shape, F32) for w in ws]
    outs = pl.pallas_call(body, out_shape=shapes * 4, name="small_adamw")(*gathered, *ws, *ms, *vs)
    return outs[:n], outs[n:2 * n], outs[2 * n:3 * n], outs[3 * n:]


W_NAMES = ("norm_mix_w", "norm_mlp_w", "a_w_qkv", "a_b_qkv", "a_sinks", "a_w_o", "a_b_o", "b_in_w", "b_conv_w", "b_conv_b",
           "b_dt_bias", "b_a_log", "b_d", "b_norm_w", "b_out_w", "c_w_qkv", "c_w_o", "mlp_w_up", "mlp_w_down", "final_norm_w")
BIG_KIND = {"a_w_qkv": "slot", "a_w_o": "row", "b_in_w": "slot", "b_out_w": "row", "c_w_qkv": "col", "c_w_o": "row",
            "mlp_w_up": "col", "mlp_w_down": "row"}
SMALL_SHARDED = {"a_b_qkv": 1, "a_b_o": 1, "b_conv_w": 2}
SMALL_REPLICATED = ("norm_mix_w", "norm_mlp_w", "a_sinks", "b_conv_b", "b_dt_bias", "b_a_log", "b_d", "b_norm_w", "final_norm_w")


def _layer_big(i):
    kind, j = i % 3, i // 3
    mix = {0: [("a_w_qkv", j), ("a_w_o", j)], 1: [("b_in_w", 0), ("b_out_w", 0)], 2: [("c_w_qkv", 0), ("c_w_o", 0)]}[kind]
    return mix + [("mlp_w_up", i), ("mlp_w_down", i)]


def _block_size(kind, shard2d):
    return {"slot": None, "row": shard2d[0], "col": shard2d[1]}[kind]


def _full2d(kind, shard2d):
    k, n = shard2d
    return {"slot": (N_DEV, k, n), "row": (N_DEV * k, n), "col": (k, N_DEV * n)}[kind]


def _from_slots(t, ax):
    s = t.shape[1:]
    return jnp.moveaxis(t, 0, ax).reshape(s[:ax] + (N_DEV * s[ax],) + s[ax + 1:])


def _to_slots(g, ax):
    s = g.shape
    return jnp.moveaxis(g.reshape(s[:ax] + (N_DEV, s[ax] // N_DEV) + s[ax + 1:]), ax, 0)


def _rope_tables(positions):
    half = HEAD_DIM // 2
    inv = ROPE_THETA ** (-(jnp.arange(LANES, dtype=jnp.int32) % half).astype(F32) / half)
    ang = positions.astype(F32).reshape(-1, 1) * inv
    return jnp.cos(ang), jnp.sin(ang)


def _swa_fwd(u, h, p, j, B, S, cos, sin, tag, nw, plan=None):
    qkv = _matmul(u, p["a_w_qkv"][j], out_dtype=BF16, bias=p["a_b_qkv"][j][None], rope=(cos, sin),
                  rope_cols=A_Q_DIM + A_KV_DIM, name=f"{tag}_qkv", plan=plan)
    o, lse = _attn_fwd(qkv, B, S, 1, n_heads=A_N_HEADS, n_kv=A_N_KV, q_col=0, k_col=A_Q_DIM, v_col=A_Q_DIM + A_KV_DIM,
                       max_dist=A_WINDOW - 1, sinks=p["a_sinks"][j], name=f"{tag}_attn", plan=plan)
    h1, u2 = _matmul(o, p["a_w_o"][j], bias=p["a_b_o"][j][None], resid=h, norm_out=nw, name=f"{tag}_o")
    return h1, u2, (qkv, o, lse)


def _swa_bwd(dh1, u, saved, p, j, B, S, cos, sin, tag, norm, plan=None):
    qkv, o, lse = saved
    kw = dict(n_heads=A_N_HEADS, n_kv=A_N_KV, q_col=0, k_col=A_Q_DIM, v_col=A_Q_DIM + A_KV_DIM, max_dist=A_WINDOW - 1)
    g = {}
    do = _matmul(dh1, p["a_w_o"][j], tb=True, name=f"{tag}_do")
    g["a_w_o"] = _matmul(o, dh1, ta=True, name=f"{tag}_dwo")
    g["a_b_o"] = _colsum(dh1, f"{tag}_dbo")[0]
    sk = jnp.pad(p["a_sinks"][j], (0, LANES - A_N_HEADS))[None]
    delta, dob, dsink = _delta(do, o, lse, sk, name=f"{tag}_delta")
    g["a_sinks"] = dsink[0, :A_N_HEADS]
    dq = _attn_dq(qkv, dob, lse, delta, cos, sin, B, S, 1, name=f"{tag}_dq", plan=plan, **kw)
    dk, dv = _attn_dkv(qkv, dob, lse, delta, cos, sin, B, S, 1, name=f"{tag}_dkv", plan=plan, **kw)
    dqkv = jnp.concatenate([dq, dk, dv], axis=1)
    g["a_w_qkv"] = _matmul(u, dqkv, ta=True, name=f"{tag}_dwqkv")
    g["a_b_qkv"] = _colsum(dqkv, f"{tag}_dbqkv")[0]
    if plan is not None:
        plan.grads(3 * j, "mix", {"a_w_qkv": g["a_w_qkv"], "a_w_o": g["a_w_o"]})
    dh, dnw = _matmul(dqkv, p["a_w_qkv"][j], tb=True, norm_bwd=(norm[0], norm[1], dh1), name=f"{tag}_du", plan=plan)
    return dh, dnw, g


def _group_cols(gi, qkv):
    W = C_HEADS * HEAD_DIM
    if C_PATTERNS[gi][1] == 1:
        return qkv, (gi * W, (3 + gi) * W, (6 + gi) * W)
    part = jnp.concatenate([qkv[:, (3 * j + gi) * W:(3 * j + gi + 1) * W] for j in range(3)], axis=1)
    return part, (0, W, 2 * W)


def _dil_fwd(u, h, p, B, S, cos, sin, nw, plan=None):
    W = C_HEADS * HEAD_DIM
    qkv = _matmul(u, p["c_w_qkv"][0], out_dtype=BF16, rope=(cos, sin), rope_cols=6 * W, name="c_qkv", plan=plan)
    os_, lses, parts = [], [], []
    for gi, (window, dil) in enumerate(C_PATTERNS):
        part, (qc, kc, vc) = _group_cols(gi, qkv)
        o, lse = _attn_fwd(part, B, S, dil, n_heads=C_HEADS, n_kv=C_HEADS, q_col=qc, k_col=kc, v_col=vc,
                           max_dist=window // dil, sinks=None, name=f"c_attn{gi}")
        os_.append(o)
        lses.append(lse)
        parts.append((part, (qc, kc, vc)))
    o, lse = _merge(os_, lses)
    h1, u2 = _matmul(o, p["c_w_o"][0], resid=h, norm_out=nw, name="c_o")
    return h1, u2, (parts, o, lse)


def _dil_bwd(dh1, u, saved, p, B, S, cos, sin, norm, plan=None):
    parts, o, lse = saved
    g = {}
    do = _matmul(dh1, p["c_w_o"][0], tb=True, name="c_do")
    g["c_w_o"] = _matmul(o, dh1, ta=True, name="c_dwo")[None]
    delta, dob = _delta(do, o, name="c_delta")
    dqs, dks, dvs = [], [], []
    for gi, (window, dil) in enumerate(C_PATTERNS):
        part, (qc, kc, vc) = parts[gi]
        kw = dict(n_heads=C_HEADS, n_kv=C_HEADS, q_col=qc, k_col=kc, v_col=vc, max_dist=window // dil)
        dqs.append(_attn_dq(part, dob, lse, delta, cos, sin, B, S, dil, name=f"c_dq{gi}", **kw))
        dk, dv = _attn_dkv(part, dob, lse, delta, cos, sin, B, S, dil, name=f"c_dkv{gi}", **kw)
        dks.append(dk)
        dvs.append(dv)
    dqkv = jnp.concatenate(dqs + dks + dvs, axis=1)
    g["c_w_qkv"] = _matmul(u, dqkv, ta=True, name="c_dwqkv", plan=plan)[None]
    if plan is not None:
        plan.grads(2, "mix", {"c_w_qkv": g["c_w_qkv"][0], "c_w_o": g["c_w_o"][0]})
    dh, dnw = _matmul(dqkv, p["c_w_qkv"][0], tb=True, norm_bwd=(norm[0], norm[1], dh1), name="c_du", plan=plan)
    return dh, dnw, g


def _ssm_params(p):
    par = jnp.stack([p["b_dt_bias"][0], p["b_a_log"][0], p["b_d"][0]], axis=0)
    prow = par.reshape(3, SSM_N_GROUPS, SSM_HG).transpose(1, 0, 2)
    return prow, prow.transpose(0, 2, 1)


def _mamba_fwd(u, h, p, B, S, nw, plan=None):
    T = B * S
    G, HG = SSM_N_GROUPS, SSM_HG
    w_in = p["b_in_w"][0]
    nzx = SSM_D_INNER + SSM_CONV_DIM
    w_dt = jnp.pad(w_in[:, nzx:], ((0, 0), (0, LANES - SSM_N_HEADS)))
    zx = _matmul(u, w_in[:, :nzx], name="b_zx", plan=plan)
    dtraw = _matmul(u, w_dt, name="b_dt")[:, :SSM_N_HEADS]
    dtc = dtraw.reshape(B, S, G, HG).transpose(0, 2, 1, 3)
    dtr = dtraw.reshape(B, S, G, HG).transpose(0, 2, 3, 1)
    prow, pcol = _ssm_params(p)
    zx3 = zx.reshape(B, S, nzx)
    xc3 = _conv_fwd(zx3, p["b_conv_w"][0], p["b_conv_b"])
    y3, states = _ssd_fwd(xc3, dtc, dtr, prow, pcol, plan=plan)
    y = y3.reshape(T, SSM_D_INNER)
    gn = _gate_fwd(y, zx, p["b_norm_w"])
    h1, u2 = _matmul(gn, p["b_out_w"][0], resid=h, norm_out=nw, name="b_out")
    return h1, u2, (zx, dtc, dtr, xc3, y, states, gn, w_dt)


def _mamba_bwd(dh1, u, saved, p, B, S, norm, plan=None):
    T = B * S
    zx, dtc, dtr, xc3, y, states, gn, w_dt = saved
    nzx = SSM_D_INNER + SSM_CONV_DIM
    w_in = p["b_in_w"][0]
    prow, pcol = _ssm_params(p)
    g = {}
    dgn = _matmul(dh1, p["b_out_w"][0], tb=True, name="b_dgn")
    g["b_out_w"] = _matmul(gn, dh1, ta=True, name="b_dwout")[None]
    dy, dz, dnw = _gate_bwd(dgn, y, zx, p["b_norm_w"])
    g["b_norm_w"] = dnw
    dx3, dB3, dC3, ddt, dpar = _ssd_bwd(xc3, dtc, dtr, prow, pcol, states, dy.reshape(B, S, SSM_D_INNER), plan=plan)
    dpar = dpar.transpose(1, 0, 2).reshape(3, SSM_N_HEADS)
    g["b_dt_bias"], g["b_a_log"], g["b_d"] = dpar[0:1], dpar[1:2], dpar[2:3]
    zx3 = zx.reshape(B, S, nzx)
    cw, cb = p["b_conv_w"][0], p["b_conv_b"]
    parts, dws, dbs = [], [], []
    for col0, dpart, nm in ((0, dx3, "b_conv_bwd_x"), (SSM_D_INNER, dB3, "b_conv_bwd_b"),
                            (SSM_D_INNER + SSM_BC_DIM, dC3, "b_conv_bwd_c")):
        dxp, dw, db = _conv_bwd(zx3, dpart, cw, cb, col0, nm)
        parts.append(dxp.reshape(T, -1))
        dws.append(dw)
        dbs.append(db)
    g["b_conv_w"] = jnp.concatenate(dws, axis=1)[None]
    g["b_conv_b"] = jnp.concatenate(dbs, axis=1)
    dzx = jnp.concatenate([dz] + parts, axis=1)
    ddtraw = ddt.transpose(0, 2, 1, 3).reshape(T, SSM_N_HEADS)
    ddtp = jnp.pad(ddtraw, ((0, 0), (0, LANES - SSM_N_HEADS)))
    dw_zx = _matmul(u, dzx, ta=True, name="b_dwzx")
    dw_dt = _matmul(u, ddtp, ta=True, name="b_dwdt")[:, :SSM_N_HEADS]
    g["b_in_w"] = jnp.concatenate([dw_zx, dw_dt], axis=1)[None]
    if plan is not None:
        plan.grads(1, "mix", {"b_in_w": g["b_in_w"][0], "b_out_w": g["b_out_w"][0]})
    du = _matmul(dzx, w_in[:, :nzx], tb=True, name="b_du_zx", plan=plan)
    dh, dnw = _matmul(ddtp, w_dt, tb=True, resid=du, norm_bwd=(norm[0], norm[1], dh1), name="b_du_dt")
    return dh, dnw, g


def _local_step(x, positions, p, target, plan=None):
    B, S, D = x.shape
    T = B * S
    cos, sin = _rope_tables(positions)
    h = x.reshape(T, D)
    tape = []
    u = _rmsnorm_fwd(h, p["norm_mix_w"][0], "l0_norm_mix")
    for i in range(DEPTH):
        kind, j = i % 3, i // 3
        nw = p["norm_mlp_w"][i]
        if kind == 0:
            h1, u2, saved = _swa_fwd(u, h, p, j, B, S, cos, sin, f"a{j}", nw, plan)
        elif kind == 1:
            h1, u2, saved = _mamba_fwd(u, h, p, B, S, nw, plan)
        else:
            h1, u2, saved = _dil_fwd(u, h, p, B, S, cos, sin, nw, plan)
        r, s = _matmul(u2, p["mlp_w_up"][i], out_dtype=BF16, relu2=True, name=f"l{i}_up", plan=plan)
        if i + 1 < DEPTH:
            h2, u_next = _matmul(s, p["mlp_w_down"][i], resid=h1, norm_out=p["norm_mix_w"][i + 1], name=f"l{i}_down", plan=plan)
        else:
            h2, u_next = _matmul(s, p["mlp_w_down"][i], resid=h1, name=f"l{i}_down", plan=plan), None
        tape.append((h, u, saved, h1, u2, r, s))
        h, u = h2, u_next
    dh, dwf, loss = _final_loss(h, target.reshape(T, D), p["final_norm_w"])
    grads = {"final_norm_w": dwf[0]}
    per_layer = {n: [None] * DEPTH for n in ("norm_mix_w", "norm_mlp_w", "mlp_w_up", "mlp_w_down")}
    a_grads = [None, None]
    for i in reversed(range(DEPTH)):
        kind, j = i % 3, i // 3
        h0, u, saved, h1, u2, r, s = tape[i]
        da = _matmul(dh, p["mlp_w_down"][i], tb=True, out_dtype=BF16, mul=r, mul_scale=2.0, name=f"l{i}_da", plan=plan)
        per_layer["mlp_w_down"][i] = _matmul(s, dh, ta=True, name=f"l{i}_dwdown")
        per_layer["mlp_w_up"][i] = _matmul(u2, da, ta=True, name=f"l{i}_dwup")
        if plan is not None:
            plan.grads(i, "mlp", {"mlp_w_up": per_layer["mlp_w_up"][i], "mlp_w_down": per_layer["mlp_w_down"][i]})
        dh1, dnw = _matmul(da, p["mlp_w_up"][i], tb=True, norm_bwd=(h1, p["norm_mlp_w"][i], dh), name=f"l{i}_du2", plan=plan)
        per_layer["norm_mlp_w"][i] = dnw[0]
        norm = (h0, p["norm_mix_w"][i])
        if kind == 0:
            dh, dnw, g = _swa_bwd(dh1, u, saved, p, j, B, S, cos, sin, f"a{j}", norm, plan)
            a_grads[j] = g
        elif kind == 1:
            dh, dnw, g = _mamba_bwd(dh1, u, saved, p, B, S, norm, plan)
            grads.update(g)
        else:
            dh, dnw, g = _dil_bwd(dh1, u, saved, p, B, S, cos, sin, norm, plan)
            grads.update(g)
        per_layer["norm_mix_w"][i] = dnw[0]
    for n in ("norm_mix_w", "norm_mlp_w"):
        grads[n] = jnp.stack(per_layer[n], axis=0)
    for n in ("mlp_w_up", "mlp_w_down"):
        grads[n] = per_layer[n]
    for n in ("a_b_qkv", "a_sinks", "a_b_o"):
        grads[n] = jnp.stack([a_grads[0][n], a_grads[1][n]], axis=0)
    for n in ("a_w_qkv", "a_w_o"):
        grads[n] = [a_grads[0][n], a_grads[1][n]]
    for n in ("b_in_w", "b_out_w", "c_w_qkv", "c_w_o"):
        grads[n] = [grads[n][0]]
    return loss, dh.reshape(B, S, D), grads


MIX = {0: ("a_w_qkv", "a_w_o"), 1: ("b_in_w", "b_out_w"), 2: ("c_w_qkv", "c_w_o")}
MLP = ("mlp_w_up", "mlp_w_down")
GATHER_FIRST = (0, MIX[0])
GATHER_HOSTS = {"a0_qkv": ((0, ("mlp_w_up",)),), "a0_attn": ((0, ("mlp_w_down",)),), "l0_up": ((1, ("b_in_w",)),),
                "l0_down": ((1, ("b_out_w",)),), "b_zx": ((1, ("mlp_w_up",)),),
                "b_ssd_fwd": ((1, ("mlp_w_down",)), (2, None)), "c_qkv": ((3, None),)}
REDUCE_HOSTS = {(3, "mlp"): ("l3_du2", "a1_dkv"), (3, "mix"): ("a1_du", "l2_da"),
                (2, "mlp"): ("l2_du2", "c_dwqkv"), (2, "mix"): ("c_du", "b_ssd_bwd"),
                (1, "mlp"): ("l1_du2", "b_ssd_bwd"), (1, "mix"): ("b_du_zx", "a0_dq"),
                (0, "mlp"): ("l0_du2", "a0_dkv"), (0, "mix"): (None, None)}


class _Plan:
    def __init__(self, w, m, v, p, dev, chip, core):
        self.w, self.m, self.v, self.p, self.dev, self.chip, self.core = w, m, v, p, dev, chip, core
        self.pending = {}
        self.res = {n: None for n in BIG_KIND}
        self._install(*GATHER_FIRST)(_gather(self._gather_items(*GATHER_FIRST), "gather_first"))
        for host, groups in GATHER_HOSTS.items():
            for i, only in groups:
                self._wait_for(host, _gather_comm(self._gather_items(i, only)), self._install(i, only))

    def _wait_for(self, host, comm, done):
        self.pending.setdefault(host, []).append((comm, done))

    def _names(self, i, only):
        return [(n, l) for n, l in _layer_big(i) if only is None or n in only]

    def _gather_items(self, i, only):
        items = []
        for n, l in self._names(i, only):
            kind, s2 = BIG_KIND[n], self.w[n].shape[1:]
            placed = _place(self.w[n], l, kind, _full2d(kind, s2), self.dev, f"place_l{i}_{n}")
            items.append((placed, kind, _block_size(kind, s2), _full2d(kind, s2), True))
        return items

    def _install(self, i, only):
        def done(fulls):
            for (n, l), t in zip(self._names(i, only), fulls):
                self.p[n][l] = _from_slots(t, 1) if BIG_KIND[n] == "slot" else t
        return done

    def take(self, host):
        return _Comm.merge([c for c, _ in self.pending[host]]) if host in self.pending else None

    def give(self, host, results):
        for comm, done in self.pending.pop(host):
            done(results[:len(comm.out_shapes)])
            results = results[len(comm.out_shapes):]

    def grads(self, i, group, grads):
        names = self._names(i, MLP if group == "mlp" else MIX[i % 3])
        items = []
        for n, _ in names:
            kind, s2 = BIG_KIND[n], self.w[n].shape[1:]
            items.append((_to_slots(grads[n], 1) if kind == "slot" else grads[n], kind, _block_size(kind, s2), s2))
        d2d_host, ici_host = REDUCE_HOSTS[(i, group)]
        tag = f"l{i}_{group}"

        def update(parts):
            def done(recv):
                for (n, l), pt, r in zip(names, parts, recv):
                    self.res[n] = _adamw(pt, r, self.w[n], self.m[n], self.v[n], l, self.res[n], self.chip, f"adamw_l{i}_{n}")
            return done

        def second(sib):
            parts = [_pair_sum(it[0], s, it[1], self.core, f"pair_sum_l{i}_{n}") for (n, _), it, s in zip(names, items, sib)]
            self._send(ici_host, _reduce_ici_comm(parts), update(parts), f"reduce_ici_{tag}")

        self._send(d2d_host, _reduce_d2d_comm(items), second, f"reduce_d2d_{tag}")

    def _send(self, host, comm, done, name):
        if host is None:
            done(_run_comm(comm, name))
        else:
            self._wait_for(host, comm, done)

    def flush(self):
        late = 0
        while self.pending:
            host = next(iter(self.pending))
            for comm, done in self.pending.pop(host):
                done(_run_comm(comm, f"late_{late}_{host}"))
                late += 1


def kernel(x, positions, norm_mix_w, norm_mlp_w, a_w_qkv, a_b_qkv, a_sinks, a_w_o, a_b_o, b_in_w, b_conv_w, b_conv_b, b_dt_bias, b_a_log, b_d, b_norm_w, b_out_w, c_w_qkv, c_w_o, mlp_w_up, mlp_w_down, final_norm_w, loss_target, m_norm_mix_w, m_norm_mlp_w, m_a_w_qkv, m_a_b_qkv, m_a_sinks, m_a_w_o, m_a_b_o, m_b_in_w, m_b_conv_w, m_b_conv_b, m_b_dt_bias, m_b_a_log, m_b_d, m_b_norm_w, m_b_out_w, m_c_w_qkv, m_c_w_o, m_mlp_w_up, m_mlp_w_down, m_final_norm_w, v_norm_mix_w, v_norm_mlp_w, v_a_w_qkv, v_a_b_qkv, v_a_sinks, v_a_w_o, v_a_b_o, v_b_in_w, v_b_conv_w, v_b_conv_b, v_b_dt_bias, v_b_a_log, v_b_d, v_b_norm_w, v_b_out_w, v_c_w_qkv, v_c_w_o, v_mlp_w_up, v_mlp_w_down, v_final_norm_w):
    w = dict(zip(W_NAMES, (norm_mix_w, norm_mlp_w, a_w_qkv, a_b_qkv, a_sinks, a_w_o, a_b_o, b_in_w, b_conv_w, b_conv_b,
                           b_dt_bias, b_a_log, b_d, b_norm_w, b_out_w, c_w_qkv, c_w_o, mlp_w_up, mlp_w_down, final_norm_w)))
    m = dict(zip(W_NAMES, (m_norm_mix_w, m_norm_mlp_w, m_a_w_qkv, m_a_b_qkv, m_a_sinks, m_a_w_o, m_a_b_o, m_b_in_w,
                           m_b_conv_w, m_b_conv_b, m_b_dt_bias, m_b_a_log, m_b_d, m_b_norm_w, m_b_out_w, m_c_w_qkv, m_c_w_o,
                           m_mlp_w_up, m_mlp_w_down, m_final_norm_w)))
    v = dict(zip(W_NAMES, (v_norm_mix_w, v_norm_mlp_w, v_a_w_qkv, v_a_b_qkv, v_a_sinks, v_a_w_o, v_a_b_o, v_b_in_w,
                           v_b_conv_w, v_b_conv_b, v_b_dt_bias, v_b_a_log, v_b_d, v_b_norm_w, v_b_out_w, v_c_w_qkv, v_c_w_o,
                           v_mlp_w_up, v_mlp_w_down, v_final_norm_w)))
    px, py, pc = lax.axis_index("x"), lax.axis_index("y"), lax.axis_index("c")
    me = 4 * px + 2 * py + pc
    dev, chip, core = (t.astype(jnp.int32).reshape(1) for t in (me, 2 * px + py, pc))

    trio = tuple(SMALL_SHARDED)
    got = _gather([(d[n], "slot", None, (N_DEV,) + d[n].shape, False) for n in trio for d in (w, m, v)], "gather_small")
    slots = {n: got[3 * i:3 * i + 3] for i, n in enumerate(trio)}
    p = {n: w[n] for n in SMALL_REPLICATED}
    for n in trio:
        p[n] = _from_slots(slots[n][0], SMALL_SHARDED[n])
    for n in BIG_KIND:
        p[n] = [None] * w[n].shape[0]
    plan = _Plan(w, m, v, p, dev, chip, core)
    loss_part, dx, grads = _local_step(x, positions, p, loss_target, plan)
    loss = lax.psum(loss_part[0, 0], AXES)
    plan.flush()
    out = {n: list(plan.res[n]) for n in BIG_KIND}

    small = SMALL_REPLICATED + trio
    as2d = lambda t: t.reshape(1, -1) if t.ndim == 1 else t
    g_sm = [as2d(grads[n]) for n in SMALL_REPLICATED] + [_to_slots(grads[n].reshape(p[n].shape), SMALL_SHARDED[n]) for n in trio]
    gathered = _gather([(g, "slot", None, (N_DEV,) + g.shape, False) for g in g_sm], "gather_small_grads")
    ws = [as2d(w[n]) for n in SMALL_REPLICATED] + [slots[n][0] for n in trio]
    ms = [as2d(m[n]) for n in SMALL_REPLICATED] + [slots[n][1] for n in trio]
    vs = [as2d(v[n]) for n in SMALL_REPLICATED] + [slots[n][2] for n in trio]
    sm_out = _small_adamw(gathered, ws, ms, vs)
    for i, n in enumerate(small):
        if n in SMALL_SHARDED:
            out[n] = [lax.dynamic_index_in_dim(sm_out[k][i], me, 0, keepdims=False) for k in range(4)]
        else:
            out[n] = [sm_out[k][i].reshape(w[n].shape) for k in range(4)]
    return (loss, dx, *[out[n][0] for n in W_NAMES], *[out[n][1] for n in W_NAMES], *[out[n][2] for n in W_NAMES],
            *[out[n][3] for n in W_NAMES])
```

```python
import math

import jax
import jax.numpy as jnp
from jax import lax
from jax.experimental import pallas as pl
from jax.experimental.pallas import tpu as pltpu

F32 = jnp.float32
BF16 = jnp.bfloat16
SDS = jax.ShapeDtypeStruct

D_MODEL = 1024
DEPTH = 4
BLOCK = 128
ROPE_THETA = 10000.0
NORM_EPS = 1e-5
HEAD_DIM = 64
A_N_HEADS = 16
A_N_KV = 2
A_WINDOW = 128
A_Q_DIM = 1024
A_KV_DIM = 128
SSM_D_INNER = 2048
SSM_N_HEADS = 32
SSM_N_GROUPS = 8
SSM_HG = 4
SSM_D_STATE = 128
SSM_CONV = 4
SSM_CHUNK = 128
SSM_BC_DIM = 1024
SSM_CONV_DIM = 4096
C_PATTERNS = ((128, 1), (512, 4), (2048, 16))
C_HEADS = 16
ADAM_LR, ADAM_B1, ADAM_B2, ADAM_EPS, ADAM_WD, ADAM_STEP = 0.001, 0.9, 0.999, 1e-08, 0.01, 10

N_DEV = 8
AXES = ("x", "y", "c")
LANES = 128
VMEM_LIMIT = 56 * 1024 * 1024
STREAM_VMEM = 16 * 1024 * 1024
NEG = -1e30

NN = (((1,), (0,)), ((), ()))
NT = (((1,), (1,)), ((), ()))
TN = (((0,), (0,)), ((), ()))
HI = lax.Precision.HIGHEST


def _pick(n, cap, mult=LANES):
    best = None
    for t in range(mult, min(n, cap) + 1, mult):
        if n % t == 0:
            best = t
    return best if best is not None else n


def _params(sem):
    return pltpu.CompilerParams(dimension_semantics=sem, vmem_limit_bytes=VMEM_LIMIT)


def _bf(x):
    return x if x.dtype == BF16 else x.astype(BF16)


def _rot_half(y):
    n = y.shape[-1]
    lane = lax.broadcasted_iota(jnp.int32, y.shape, y.ndim - 1)
    return jnp.where((lane % HEAD_DIM) < HEAD_DIM // 2, -pltpu.roll(y, n - 32, y.ndim - 1), pltpu.roll(y, 32, y.ndim - 1))


def _rope(y, cos, sin, sign):
    reps = y.shape[-1] // LANES
    c = jnp.tile(cos, (1, reps)) if reps > 1 else cos
    s = jnp.tile(sin, (1, reps)) if reps > 1 else sin
    return y * c + sign * (_rot_half(y) * s)


MESH = pl.DeviceIdType.MESH
ANY = pl.BlockSpec(memory_space=pl.ANY)


class _Comm:
    def __init__(self, inputs, out_shapes, aliases, sems, phases):
        self.inputs, self.out_shapes, self.aliases, self.sems, self.phases = inputs, out_shapes, aliases, sems, phases

    @staticmethod
    def merge(comms):
        if len(comms) == 1:
            return comms[0]
        ins, outs, aliases, sems, spans = [], [], {}, [], []
        for c in comms:
            aliases.update({len(ins) + i: len(outs) + j for i, j in c.aliases.items()})
            spans.append((len(ins), len(ins) + len(c.inputs), len(outs), len(outs) + len(c.out_shapes), len(sems),
                          len(sems) + len(c.sems)))
            ins, outs, sems = ins + list(c.inputs), outs + list(c.out_shapes), sems + list(c.sems)
        phases = []
        for f in sorted({f for c in comms for f, _ in c.phases}):
            todo = [(fn, sp) for c, sp in zip(comms, spans) for g, fn in c.phases if g == f]

            def run(cins, couts, csems, todo=todo):
                for fn, (i0, i1, o0, o1, s0, s1) in todo:
                    fn(cins[i0:i1], couts[o0:o1], csems[s0:s1])
            phases.append((f, run))
        return _Comm(ins, outs, aliases, sems, phases)


def _pc(body, args, *, out_shape, grid, in_specs, out_specs, name, sem, scratch_shapes=(), comm=None):
    single = not isinstance(out_shape, (tuple, list))
    outs, ospecs = ([out_shape], [out_specs]) if single else (list(out_shape), list(out_specs))
    unpack = (lambda r: r[0]) if single else (lambda r: tuple(r))
    if comm is None:
        res = pl.pallas_call(body, out_shape=outs, grid=grid, in_specs=list(in_specs), out_specs=ospecs,
                             scratch_shapes=list(scratch_shapes), name=name, compiler_params=_params(sem))(*args)
        return unpack(res)
    n_in, n_out, n_scr = len(in_specs), len(outs), len(scratch_shapes)
    c_in, c_out = len(comm.inputs), len(comm.out_shapes)
    total = math.prod(grid)
    steps = [min(total - 1, int(f * total)) for f, _ in comm.phases[:-1]]

    def wrapped(*refs):
        ins, cins = refs[:n_in], refs[n_in:n_in + c_in]
        o = refs[n_in + c_in:n_in + c_in + n_out]
        couts = refs[n_in + c_in + n_out:n_in + c_in + n_out + c_out]
        rest = refs[n_in + c_in + n_out + c_out:]
        scr, csems = rest[:n_scr], rest[n_scr:]
        step = pl.program_id(0)
        for ax in range(1, len(grid)):
            step = step * grid[ax] + pl.program_id(ax)
        for (_, fn), st in zip(comm.phases[:-1], steps):
            @pl.when(step == st)
            def _(fn=fn):
                fn(cins, couts, csems)
        body(*ins, *o, *scr)

        @pl.when(step == total - 1)
        def _():
            comm.phases[-1][1](cins, couts, csems)

    res = pl.pallas_call(
        wrapped, out_shape=outs + list(comm.out_shapes), grid=grid, in_specs=list(in_specs) + [ANY] * c_in,
        out_specs=ospecs + [ANY] * c_out, scratch_shapes=list(scratch_shapes) + list(comm.sems),
        input_output_aliases={n_in + i: n_out + j for i, j in comm.aliases.items()}, name=name,
        compiler_params=_params(("arbitrary",) * len(grid)),
    )(*args, *comm.inputs)
    return unpack(res[:n_out]), list(res[n_out:])


def _hosted(plan, name, run):
    comm = plan.take(name) if plan is not None else None
    if comm is None:
        return run(None)
    res, extra = run(comm)
    plan.give(name, extra)
    return res


MM_VMEM = 40 * 1024 * 1024
HBM_BYTES_PER_US = 2.5e6
STEP_US = 0.35


def _divisors(n, cands):
    return [c for c in cands if c <= n and n % c == 0] or [n]


def _mm_tiles(M, N, K, sa, sb, out_bytes, extra_bytes, full_rows=False):
    best = None
    for tm in _divisors(M, (2048, 1024, 512, 256)):
        for tn in ([N] if full_rows else _divisors(N, (1024, 640, 512, 256, 128))):
            for tk in _divisors(K, (K, K // 2, K // 3, K // 4, 2048, 1024, 640, 512)):
                if tk != K and tk % LANES:
                    continue
                nk = K // tk
                vmem = 2 * tm * tk * sa + 2 * tk * tn * sb + tm * tn * (2 * (out_bytes + extra_bytes) + 8 + (4 if nk > 1 else 0))
                if vmem > MM_VMEM:
                    continue
                a_traffic = M * K * sa * (1 if nk == 1 else N // tn)
                b_traffic = K * N * sb * (1 if (nk == 1 and N == tn) else M // tm)
                steps = (M // tm) * (N // tn) * nk
                cost = (a_traffic + b_traffic + M * N * (out_bytes + extra_bytes)) / HBM_BYTES_PER_US + steps * STEP_US
                cost += (M // tm) * (N // tn) * (nk - 1) * tm * tn * 8 / (4 * HBM_BYTES_PER_US)
                if best is None or cost < best[0]:
                    best = (cost, tm, tn, tk)
    assert best is not None, (M, N, K)
    return best[1:]


def _matmul(a, b, *, ta=False, tb=False, out_dtype=F32, bias=None, resid=None, mul=None, mul_scale=1.0,
            relu2=False, rope=None, rope_cols=0, norm_out=None, norm_bwd=None, name="mm", plan=None):
    M = a.shape[1] if ta else a.shape[0]
    K = a.shape[0] if ta else a.shape[1]
    N = b.shape[0] if tb else b.shape[1]
    assert (b.shape[1] if tb else b.shape[0]) == K
    two_out = relu2 or norm_out is not None
    out_bytes = jnp.dtype(out_dtype).itemsize * (2 if relu2 else 1) + (2 if norm_out is not None else 0)
    extra_bytes = (4 if resid is not None else 0) + (mul.dtype.itemsize if mul is not None else 0) + (8 if norm_bwd else 0)
    rows = norm_out is not None or norm_bwd is not None
    tm, tn, tk = _mm_tiles(M, N, K, a.dtype.itemsize, b.dtype.itemsize, out_bytes, extra_bytes, full_rows=rows)
    nk = K // tk
    dims = (((0 if ta else 1,), (1 if tb else 0,)), ((), ()))

    def body(*refs):
        it = iter(refs)
        a_ref, b_ref = next(it), next(it)
        bias_ref = next(it) if bias is not None else None
        resid_ref = next(it) if resid is not None else None
        mul_ref = next(it) if mul is not None else None
        cos_ref, sin_ref = (next(it), next(it)) if rope is not None else (None, None)
        nw_ref = next(it) if rows else None
        h_ref, dres_ref = (next(it), next(it)) if norm_bwd is not None else (None, None)
        o_ref = next(it)
        o2_ref = next(it) if two_out or norm_bwd is not None else None
        acc_ref = next(it) if nk > 1 else None
        k = pl.program_id(2)
        part = lax.dot_general(_bf(a_ref[...]), _bf(b_ref[...]), dims, preferred_element_type=F32)
        if nk > 1:
            @pl.when(k == 0)
            def _():
                acc_ref[...] = part

            @pl.when(k > 0)
            def _():
                acc_ref[...] += part

        @pl.when(k == nk - 1)
        def _():
            y = acc_ref[...] if nk > 1 else part
            if bias_ref is not None:
                y = y + bias_ref[...]
            if rope is not None and rope_cols % tn == 0 and not (two_out or rows or mul is not None or resid is not None):
                rotated = pl.program_id(1) * tn < rope_cols

                @pl.when(rotated)
                def _():
                    o_ref[...] = _rope(y, cos_ref[...], sin_ref[...], 1.0).astype(o_ref.dtype)

                @pl.when(jnp.logical_not(rotated))
                def _():
                    o_ref[...] = y.astype(o_ref.dtype)
                return
            if rope is not None:
                col = pl.program_id(1) * tn + lax.broadcasted_iota(jnp.int32, y.shape, 1)
                y = jnp.where(col < rope_cols, _rope(y, cos_ref[...], sin_ref[...], 1.0), y)
            if mul_ref is not None:
                y = y * (mul_ref[...].astype(F32) * mul_scale)
            if resid_ref is not None:
                y = y + resid_ref[...]
            if relu2:
                r = jnp.maximum(y, 0.0)
                o_ref[...] = r.astype(o_ref.dtype)
                o2_ref[...] = (r * r).astype(o2_ref.dtype)
            elif norm_bwd is not None:
                x = h_ref[...]
                rstd = lax.rsqrt(jnp.mean(x * x, axis=-1, keepdims=True) + NORM_EPS)
                g = y * nw_ref[...]
                o_ref[...] = dres_ref[...] + rstd * g - x * (rstd * rstd * rstd) * jnp.mean(g * x, axis=-1, keepdims=True)
                dw = jnp.sum(y * x * rstd, axis=0, keepdims=True)
                first = pl.program_id(0) == 0

                @pl.when(first)
                def _():
                    o2_ref[...] = dw

                @pl.when(jnp.logical_not(first))
                def _():
                    o2_ref[...] += dw
            else:
                o_ref[...] = y.astype(o_ref.dtype)
                if norm_out is not None:
                    rstd = lax.rsqrt(jnp.mean(y * y, axis=-1, keepdims=True) + NORM_EPS)
                    o2_ref[...] = (y * rstd * nw_ref[...]).astype(BF16)

    a_spec = pl.BlockSpec((tk, tm), lambda i, j, k: (k, i)) if ta else pl.BlockSpec((tm, tk), lambda i, j, k: (i, k))
    b_spec = pl.BlockSpec((tn, tk), lambda i, j, k: (j, k)) if tb else pl.BlockSpec((tk, tn), lambda i, j, k: (k, j))
    mn_spec = pl.BlockSpec((tm, tn), lambda i, j, k: (i, j))
    in_specs, args = [a_spec, b_spec], [a, b]
    if bias is not None:
        in_specs.append(pl.BlockSpec((1, tn), lambda i, j, k: (0, j)))
        args.append(bias)
    if resid is not None:
        in_specs.append(mn_spec)
        args.append(resid)
    if mul is not None:
        in_specs.append(mn_spec)
        args.append(mul)
    if rope is not None:
        in_specs += [pl.BlockSpec((tm, LANES), lambda i, j, k: (i, 0))] * 2
        args += [rope[0], rope[1]]
    vec_spec = pl.BlockSpec((1, tn), lambda i, j, k: (0, j))
    if rows:
        in_specs.append(vec_spec)
        args.append((norm_out if norm_out is not None else norm_bwd[1]).reshape(1, N))
    if norm_bwd is not None:
        in_specs += [mn_spec, mn_spec]
        args += [norm_bwd[0], norm_bwd[2]]
    out_shape = SDS((M, N), out_dtype)
    out_specs = mn_spec
    if relu2:
        out_shape, out_specs = (out_shape, out_shape), (mn_spec, mn_spec)
    elif norm_out is not None:
        out_shape, out_specs = (out_shape, SDS((M, N), BF16)), (mn_spec, mn_spec)
    elif norm_bwd is not None:
        out_shape, out_specs = (out_shape, SDS((1, N), F32)), (mn_spec, vec_spec)
    sem = ("arbitrary",) * 3 if norm_bwd is not None else ("parallel", "parallel", "arbitrary")
    return _hosted(plan, name, lambda comm: _pc(
        body, args, out_shape=out_shape, grid=(M // tm, N // tn, nk), in_specs=in_specs, out_specs=out_specs,
        scratch_shapes=[pltpu.VMEM((tm, tn), F32)] if nk > 1 else [], name=name, sem=sem, comm=comm))


def _colsum(x, name):
    T, N = x.shape
    tm = _pick(T, 1024, 8)

    def body(x_ref, o_ref):
        s = jnp.sum(x_ref[...].astype(F32), axis=0, keepdims=True)

        @pl.when(pl.program_id(0) == 0)
        def _():
            o_ref[...] = s

        @pl.when(pl.program_id(0) > 0)
        def _():
            o_ref[...] += s

    return pl.pallas_call(
        body, out_shape=SDS((1, N), F32), grid=(T // tm,),
        in_specs=[pl.BlockSpec((tm, N), lambda i: (i, 0))], out_specs=pl.BlockSpec((1, N), lambda i: (0, 0)),
        name=name, compiler_params=_params(("arbitrary",)),
    )(x)


def _rmsnorm_fwd(h, w, name):
    T, D = h.shape
    tm = _pick(T, 512, 8)

    def body(h_ref, w_ref, o_ref):
        x = h_ref[...]
        rstd = lax.rsqrt(jnp.mean(x * x, axis=-1, keepdims=True) + NORM_EPS)
        o_ref[...] = (x * rstd * w_ref[...]).astype(BF16)

    return pl.pallas_call(
        body, out_shape=SDS((T, D), BF16), grid=(T // tm,),
        in_specs=[pl.BlockSpec((tm, D), lambda i: (i, 0)), pl.BlockSpec((1, D), lambda i: (0, 0))],
        out_specs=pl.BlockSpec((tm, D), lambda i: (i, 0)), name=name, compiler_params=_params(("parallel",)),
    )(h, w.reshape(1, D))


def _final_loss(h, target, w):
    T, D = h.shape
    tm = _pick(T, 512, 8)

    def body(h_ref, t_ref, w_ref, dh_ref, dw_ref, loss_ref):
        x = h_ref[...]
        rstd = lax.rsqrt(jnp.mean(x * x, axis=-1, keepdims=True) + NORM_EPS)
        xn = x * rstd
        err = xn * w_ref[...] - t_ref[...]
        part = 0.5 * jnp.sum(jnp.mean(err * err, axis=-1, keepdims=True), axis=0, keepdims=True)
        dy = err * (1.0 / D)
        g = dy * w_ref[...]
        dh_ref[...] = rstd * g - x * (rstd * rstd * rstd) * jnp.mean(g * x, axis=-1, keepdims=True)
        dw = jnp.sum(dy * xn, axis=0, keepdims=True)
        lp = jnp.broadcast_to(part, (1, LANES))

        @pl.when(pl.program_id(0) == 0)
        def _():
            dw_ref[...] = dw
            loss_ref[...] = lp

        @pl.when(pl.program_id(0) > 0)
        def _():
            dw_ref[...] += dw
            loss_ref[...] += lp

    row = pl.BlockSpec((tm, D), lambda i: (i, 0))
    vec = pl.BlockSpec((1, D), lambda i: (0, 0))
    return pl.pallas_call(
        body, out_shape=(SDS((T, D), F32), SDS((1, D), F32), SDS((1, LANES), F32)), grid=(T // tm,),
        in_specs=[row, row, vec], out_specs=(row, vec, pl.BlockSpec((1, LANES), lambda i: (0, 0))),
        name="final_loss", compiler_params=_params(("arbitrary",)),
    )(h, target, w.reshape(1, D))


def _band_mask(i_blk, max_dist, first_ok):
    qi = lax.broadcasted_iota(jnp.int32, (BLOCK, 2 * BLOCK), 0)
    kj = lax.broadcasted_iota(jnp.int32, (BLOCK, 2 * BLOCK), 1)
    dist = qi + BLOCK - kj
    ok = (dist >= 0) & (dist <= max_dist)
    return ok & ((kj >= BLOCK) | first_ok)


def _pair(t, i):
    return t[:, LANES * i:LANES * (i + 1)]


def _low_half(shape):
    return lax.broadcasted_iota(jnp.int32, shape, len(shape) - 1) < HEAD_DIM


def _stack_heads(t):
    lo = _low_half(t.shape)
    z = jnp.zeros_like(t)
    return jnp.concatenate([jnp.where(lo, t, z), jnp.where(lo, z, t)], axis=0)


def _swap_halves(t):
    return jnp.concatenate([t[:, HEAD_DIM:], t[:, :HEAD_DIM]], axis=1)


def _kv_operand(kv, kv_swapped, h0, n_kv, n_heads):
    R = n_heads // n_kv
    if R == 1:
        return _pair(kv, h0 // 2)
    assert kv.shape[1] == LANES and R % 2 == 0, "grouped queries: one 128-lane tile of kv heads, both heads of a pair in one group"
    g = h0 // R
    t, ts = _pair(kv, g // 2), _pair(kv_swapped, g // 2)
    lo = _low_half(t.shape)
    return jnp.where(lo, t, ts) if g % 2 == 0 else jnp.where(lo, ts, t)


def _lane_place(cols):
    m = cols[0].shape[0]
    lane = lax.broadcasted_iota(jnp.int32, (m, LANES), 1)
    out = jnp.zeros((m, LANES), F32)
    for h, c in enumerate(cols):
        out = jnp.where(lane == h, c, out)
    return out


def _attn_specs(B, S, d, C, n_heads, n_kv, q_col, k_col, v_col):
    kvw = n_kv * HEAD_DIM
    qw = n_heads * HEAD_DIM
    cq, ck = (C // qw if d > 1 else 0), (C // kvw if d > 1 else 0)
    q_spec = pl.BlockSpec((1, BLOCK, qw), lambda b, r, i: (b, i, r * cq + q_col // qw))
    kc = pl.BlockSpec((1, BLOCK, kvw), lambda b, r, i: (b, i, r * ck + k_col // kvw))
    kp = pl.BlockSpec((1, BLOCK, kvw), lambda b, r, i: (b, jnp.maximum(i - 1, 0), r * ck + k_col // kvw))
    vc = pl.BlockSpec((1, BLOCK, kvw), lambda b, r, i: (b, i, r * ck + v_col // kvw))
    vp = pl.BlockSpec((1, BLOCK, kvw), lambda b, r, i: (b, jnp.maximum(i - 1, 0), r * ck + v_col // kvw))
    return q_spec, kp, kc, vp, vc


def _attn_fwd(qkv, B, S, d, *, n_heads, n_kv, q_col, k_col, v_col, max_dist, sinks, name, plan=None):
    C = qkv.shape[1]
    Ls = S // d
    nb = Ls // BLOCK
    qw = n_heads * HEAD_DIM
    R = n_heads // n_kv
    qkv3 = qkv.reshape(B, Ls, d * C)
    scale = HEAD_DIM ** -0.5

    def body(*refs):
        if sinks is not None:
            sink_ref, q_ref, kp_ref, kc_ref, vp_ref, vc_ref, o_ref, lse_ref = refs
        else:
            q_ref, kp_ref, kc_ref, vp_ref, vc_ref, o_ref, lse_ref = refs
        i = pl.program_id(2)
        mask1 = _band_mask(i, max_dist, i > 0)
        mask = jnp.concatenate([mask1, mask1], axis=0)
        q = q_ref[0]
        kk = jnp.concatenate([kp_ref[0], kc_ref[0]], axis=0)
        vv = jnp.concatenate([vp_ref[0], vc_ref[0]], axis=0)
        kks, vvs = (_swap_halves(kk), _swap_halves(vv)) if R > 1 else (None, None)
        lo = _low_half((BLOCK, LANES))
        top = lax.broadcasted_iota(jnp.int32, (2 * BLOCK, 1), 0) < BLOCK
        lses, tiles = [], []
        for t in range(n_heads // 2):
            k2 = _kv_operand(kk, kks, 2 * t, n_kv, n_heads)
            v2 = _kv_operand(vv, vvs, 2 * t, n_kv, n_heads)
            s = lax.dot_general(_stack_heads(_pair(q, t)), k2, NT, preferred_element_type=F32) * scale
            s = jnp.where(mask, s, NEG)
            m = jnp.max(s, axis=-1, keepdims=True)
            if sinks is not None:
                sk = jnp.where(top, sink_ref[2 * t], sink_ref[2 * t + 1])
                m = jnp.maximum(m, sk)
            p = jnp.exp(s - m)
            den = jnp.sum(p, axis=-1, keepdims=True)
            if sinks is not None:
                den = den + jnp.exp(sk - m)
            lse2 = m + jnp.log(den)
            o2 = jnp.dot((p / den).astype(BF16), v2, preferred_element_type=F32)
            tiles.append(jnp.where(lo, o2[:BLOCK], o2[BLOCK:]))
            lses += [lse2[:BLOCK], lse2[BLOCK:]]
        o_ref[0] = jnp.concatenate(tiles, axis=-1)
        lse_ref[0] = _lane_place(lses)

    specs = list(_attn_specs(B, S, d, C, n_heads, n_kv, q_col, k_col, v_col))
    args = [qkv3] * 5
    if sinks is not None:
        specs = [pl.BlockSpec(memory_space=pltpu.SMEM)] + specs
        args = [sinks] + args
    o3, lse3 = _hosted(plan, name, lambda comm: _pc(
        body, args, out_shape=(SDS((B, Ls, d * qw), F32), SDS((B, Ls, d * LANES), F32)), grid=(B, d, nb), in_specs=specs,
        out_specs=(pl.BlockSpec((1, BLOCK, qw), lambda b, r, i: (b, i, r)), pl.BlockSpec((1, BLOCK, LANES), lambda b, r, i: (b, i, r))),
        name=name, sem=("parallel", "parallel", "parallel"), comm=comm))
    return o3.reshape(B * S, qw), lse3.reshape(B * S, LANES)


def _attn_dq(qkv, do, lse, delta, cos, sin, B, S, d, *, n_heads, n_kv, q_col, k_col, v_col, max_dist, name, plan=None):
    C = qkv.shape[1]
    Ls = S // d
    nb = Ls // BLOCK
    qw = n_heads * HEAD_DIM
    R = n_heads // n_kv
    scale = HEAD_DIM ** -0.5

    def body(q_ref, kp_ref, kc_ref, vp_ref, vc_ref, do_ref, lse_ref, dl_ref, cos_ref, sin_ref, dq_ref):
        i = pl.program_id(2)
        mask1 = _band_mask(i, max_dist, i > 0)
        mask = jnp.concatenate([mask1, mask1], axis=0)
        q = q_ref[0]
        do_ = do_ref[0]
        kk = jnp.concatenate([kp_ref[0], kc_ref[0]], axis=0)
        vv = jnp.concatenate([vp_ref[0], vc_ref[0]], axis=0)
        kks, vvs = (_swap_halves(kk), _swap_halves(vv)) if R > 1 else (None, None)
        lo = _low_half((BLOCK, LANES))
        lse_t, dl_t = lse_ref[0], dl_ref[0]
        tiles = []
        for t in range(n_heads // 2):
            k2 = _kv_operand(kk, kks, 2 * t, n_kv, n_heads)
            v2 = _kv_operand(vv, vvs, 2 * t, n_kv, n_heads)
            lse2 = jnp.concatenate([lse_t[:, 2 * t:2 * t + 1], lse_t[:, 2 * t + 1:2 * t + 2]], axis=0)
            dl2 = jnp.concatenate([dl_t[:, 2 * t:2 * t + 1], dl_t[:, 2 * t + 1:2 * t + 2]], axis=0)
            s = lax.dot_general(_stack_heads(_pair(q, t)), k2, NT, preferred_element_type=F32) * scale
            p = jnp.where(mask, jnp.exp(s - lse2), 0.0)
            dp = lax.dot_general(_stack_heads(_pair(do_, t)), v2, NT, preferred_element_type=F32)
            ds = p * (dp - dl2)
            dq2 = jnp.dot(ds.astype(BF16), k2, preferred_element_type=F32) * scale
            tiles.append(jnp.where(lo, dq2[:BLOCK], dq2[BLOCK:]))
        dq = jnp.concatenate(tiles, axis=-1)
        dq_ref[0] = _rope(dq, cos_ref[0], sin_ref[0], -1.0).astype(BF16)

    qs, kp, kc, vp, vc = _attn_specs(B, S, d, C, n_heads, n_kv, q_col, k_col, v_col)
    row_q = pl.BlockSpec((1, BLOCK, qw), lambda b, r, i: (b, i, r))
    row_l = pl.BlockSpec((1, BLOCK, LANES), lambda b, r, i: (b, i, r))
    qkv3 = qkv.reshape(B, Ls, d * C)
    v3 = lambda t, w: t.reshape(B, Ls, d * w)
    args = (qkv3, qkv3, qkv3, qkv3, qkv3, v3(do, qw), v3(lse, LANES), v3(delta, LANES), v3(cos, LANES), v3(sin, LANES))
    dq3 = _hosted(plan, name, lambda comm: _pc(
        body, args, out_shape=SDS((B, Ls, d * qw), BF16), grid=(B, d, nb),
        in_specs=[qs, kp, kc, vp, vc, row_q, row_l, row_l, row_l, row_l], out_specs=row_q,
        name=name, sem=("parallel", "parallel", "parallel"), comm=comm))
    return dq3.reshape(B * S, qw)


def _attn_dkv(qkv, do, lse, delta, cos, sin, B, S, d, *, n_heads, n_kv, q_col, k_col, v_col, max_dist, name, plan=None):
    C = qkv.shape[1]
    Ls = S // d
    nb = Ls // BLOCK
    qw = n_heads * HEAD_DIM
    kvw = n_kv * HEAD_DIM
    R = n_heads // n_kv
    scale = HEAD_DIM ** -0.5
    cq, ck = (C // qw if d > 1 else 0), (C // kvw if d > 1 else 0)

    def body(k_ref, v_ref, q0_ref, q1_ref, do0_ref, do1_ref, lse0_ref, lse1_ref, dl0_ref, dl1_ref, cos_ref, sin_ref,
             dk_ref, dv_ref):
        j = pl.program_id(2)
        kj = lax.broadcasted_iota(jnp.int32, (BLOCK, BLOCK), 0)
        qi = lax.broadcasted_iota(jnp.int32, (BLOCK, BLOCK), 1)
        dist0 = qi - kj
        dist1 = qi + BLOCK - kj
        mask0 = (dist0 >= 0) & (dist0 <= max_dist)
        mask1 = (dist1 <= max_dist) & (j + 1 < nb)
        kb, vb = k_ref[0], v_ref[0]
        kbs, vbs = (_swap_halves(kb), _swap_halves(vb)) if R > 1 else (None, None)
        sides = ((q0_ref[0], do0_ref[0], lse0_ref[0].T, dl0_ref[0].T, mask0), (q1_ref[0], do1_ref[0], lse1_ref[0].T, dl1_ref[0].T, mask1))
        n_acc = n_kv if R > 1 else n_kv // 2
        dks = [jnp.zeros((BLOCK, LANES), F32) for _ in range(n_acc)]
        dvs = [jnp.zeros((BLOCK, LANES), F32) for _ in range(n_acc)]
        for t in range(n_heads // 2):
            k2 = _kv_operand(kb, kbs, 2 * t, n_kv, n_heads)
            v2 = _kv_operand(vb, vbs, 2 * t, n_kv, n_heads)
            a = (2 * t) // R if R > 1 else t
            for (q, do_, lse_r, dl_r, mask) in sides:
                q2, do2 = _stack_heads(_pair(q, t)), _stack_heads(_pair(do_, t))
                s = lax.dot_general(k2, q2, NT, preferred_element_type=F32) * scale
                dp = lax.dot_general(v2, do2, NT, preferred_element_type=F32)
                ps, dss = [], []
                for half in (0, 1):
                    h = 2 * t + half
                    sl = slice(BLOCK * half, BLOCK * (half + 1))
                    p = jnp.where(mask, jnp.exp(s[:, sl] - lse_r[h:h + 1, :]), 0.0)
                    ps.append(p)
                    dss.append(p * (dp[:, sl] - dl_r[h:h + 1, :]))
                dvs[a] = dvs[a] + jnp.dot(jnp.concatenate(ps, axis=1).astype(BF16), do2, preferred_element_type=F32)
                dks[a] = dks[a] + jnp.dot(jnp.concatenate(dss, axis=1).astype(BF16), q2, preferred_element_type=F32)
        if R > 1:
            lo = _low_half((BLOCK, LANES))
            fold = lambda x: x + pltpu.roll(x, HEAD_DIM, 1)
            dks = [jnp.where(lo, fold(dks[2 * t]), fold(dks[2 * t + 1])) for t in range(n_kv // 2)]
            dvs = [jnp.where(lo, fold(dvs[2 * t]), fold(dvs[2 * t + 1])) for t in range(n_kv // 2)]
        dk_t = jnp.concatenate(dks, axis=-1) * scale
        dk_ref[0] = _rope(dk_t, cos_ref[0], sin_ref[0], -1.0).astype(BF16)
        dv_ref[0] = jnp.concatenate(dvs, axis=-1).astype(BF16)

    nxt = lambda j: jnp.minimum(j + 1, nb - 1)
    k_spec = pl.BlockSpec((1, BLOCK, kvw), lambda b, r, j: (b, j, r * ck + k_col // kvw))
    v_spec = pl.BlockSpec((1, BLOCK, kvw), lambda b, r, j: (b, j, r * ck + v_col // kvw))
    q0 = pl.BlockSpec((1, BLOCK, qw), lambda b, r, j: (b, j, r * cq + q_col // qw))
    q1 = pl.BlockSpec((1, BLOCK, qw), lambda b, r, j: (b, nxt(j), r * cq + q_col // qw))
    w0 = lambda w: pl.BlockSpec((1, BLOCK, w), lambda b, r, j: (b, j, r))
    w1 = lambda w: pl.BlockSpec((1, BLOCK, w), lambda b, r, j: (b, nxt(j), r))
    qkv3 = qkv.reshape(B, Ls, d * C)
    v3 = lambda t, w: t.reshape(B, Ls, d * w)
    do3, lse3, dl3 = v3(do, qw), v3(lse, LANES), v3(delta, LANES)
    args = (qkv3, qkv3, qkv3, qkv3, do3, do3, lse3, lse3, dl3, dl3, v3(cos, LANES), v3(sin, LANES))
    dk3, dv3 = _hosted(plan, name, lambda comm: _pc(
        body, args, out_shape=(SDS((B, Ls, d * kvw), BF16), SDS((B, Ls, d * kvw), BF16)), grid=(B, d, nb),
        in_specs=[k_spec, v_spec, q0, q1, w0(qw), w1(qw), w0(LANES), w1(LANES), w0(LANES), w1(LANES), w0(LANES), w0(LANES)],
        out_specs=(w0(kvw), w0(kvw)), name=name, sem=("parallel", "parallel", "parallel"), comm=comm))
    return dk3.reshape(B * S, kvw), dv3.reshape(B * S, kvw)


def _head_expand():
    r = lax.broadcasted_iota(jnp.int32, (LANES, C_HEADS * HEAD_DIM), 0)
    c = lax.broadcasted_iota(jnp.int32, (LANES, C_HEADS * HEAD_DIM), 1)
    return jnp.where(c // HEAD_DIM == r, 1.0, 0.0).astype(F32)


def _delta(do, o, lse=None, sinks_row=None, name="delta"):
    T, W = do.shape
    tm = _pick(T, 512, 8)
    with_sink = sinks_row is not None

    def body(*refs):
        if with_sink:
            do_ref, o_ref, lse_ref, sk_ref, dl_ref, dob_ref, ds_ref = refs
        else:
            do_ref, o_ref, dl_ref, dob_ref = refs
        do_ = do_ref[...]
        dl = lax.dot_general(do_ * o_ref[...], _head_expand(), NT, preferred_element_type=F32, precision=HI)
        dl_ref[...] = dl
        dob_ref[...] = do_.astype(BF16)
        if with_sink:
            lane = lax.broadcasted_iota(jnp.int32, dl.shape, 1)
            contrib = jnp.where(lane < A_N_HEADS, -jnp.exp(sk_ref[...] - lse_ref[...]) * dl, 0.0)
            part = jnp.sum(contrib, axis=0, keepdims=True)

            @pl.when(pl.program_id(0) == 0)
            def _():
                ds_ref[...] = part

            @pl.when(pl.program_id(0) > 0)
            def _():
                ds_ref[...] += part

    row_w = pl.BlockSpec((tm, W), lambda i: (i, 0))
    row_l = pl.BlockSpec((tm, LANES), lambda i: (i, 0))
    vec_l = pl.BlockSpec((1, LANES), lambda i: (0, 0))
    if with_sink:
        return pl.pallas_call(
            body, out_shape=(SDS((T, LANES), F32), SDS((T, W), BF16), SDS((1, LANES), F32)), grid=(T // tm,),
            in_specs=[row_w, row_w, row_l, vec_l], out_specs=(row_l, row_w, vec_l), name=name,
            compiler_params=_params(("arbitrary",)),
        )(do, o, lse, sinks_row)
    return pl.pallas_call(
        body, out_shape=(SDS((T, LANES), F32), SDS((T, W), BF16)), grid=(T // tm,),
        in_specs=[row_w, row_w], out_specs=(row_l, row_w), name=name, compiler_params=_params(("parallel",)),
    )(do, o)


def _merge(os_, lses):
    T, W = os_[0].shape
    tm = _pick(T, 512, 8)

    def body(o0, o1, o2, l0, l1, l2, o_ref, lse_ref):
        ls = [l0[...], l1[...], l2[...]]
        m = jnp.maximum(jnp.maximum(ls[0], ls[1]), ls[2])
        ws = [jnp.exp(l - m) for l in ls]
        tot = ws[0] + ws[1] + ws[2]
        lse_ref[...] = m + jnp.log(tot)
        e = _head_expand()
        acc = jnp.zeros((tm, W), F32)
        for w, o in zip(ws, (o0, o1, o2)):
            acc = acc + jnp.dot(w / tot, e, preferred_element_type=F32, precision=HI) * o[...]
        o_ref[...] = acc

    row_w = pl.BlockSpec((tm, W), lambda i: (i, 0))
    row_l = pl.BlockSpec((tm, LANES), lambda i: (i, 0))
    return pl.pallas_call(
        body, out_shape=(SDS((T, W), F32), SDS((T, LANES), F32)), grid=(T // tm,),
        in_specs=[row_w] * 3 + [row_l] * 3, out_specs=(row_w, row_l), name="c_merge", compiler_params=_params(("parallel",)),
    )(*os_, *lses)


CONV_TC = 256


def _conv_pre(x, w, bias):
    row = lax.broadcasted_iota(jnp.int32, x.shape, 0)
    acc = x * w[SSM_CONV - 1:SSM_CONV, :] + bias
    for k in range(1, SSM_CONV):
        acc = acc + jnp.where(row >= k, pltpu.roll(x, k, 0), 0.0) * w[SSM_CONV - 1 - k:SSM_CONV - k, :]
    return acc


def _conv_fwd(zx3, w, bias):
    B, S, _ = zx3.shape
    off = SSM_D_INNER // CONV_TC

    def body(x_ref, w_ref, b_ref, o_ref):
        v = _conv_pre(x_ref[0], w_ref[...], b_ref[...])
        o_ref[0] = v * jax.nn.sigmoid(v)

    return pl.pallas_call(
        body, out_shape=SDS((B, S, SSM_CONV_DIM), F32), grid=(B, SSM_CONV_DIM // CONV_TC),
        in_specs=[pl.BlockSpec((1, S, CONV_TC), lambda b, j: (b, 0, j + off)),
                  pl.BlockSpec((SSM_CONV, CONV_TC), lambda b, j: (0, j)), pl.BlockSpec((1, CONV_TC), lambda b, j: (0, j))],
        out_specs=pl.BlockSpec((1, S, CONV_TC), lambda b, j: (b, 0, j)), name="b_conv_fwd",
        compiler_params=_params(("parallel", "parallel")),
    )(zx3, w, bias)


def _conv_bwd(zx3, dxc, w, bias, col0, name):
    B, S, n = dxc.shape
    tc = _pick(n, CONV_TC)
    off_x = (SSM_D_INNER + col0) // tc
    off_w = col0 // tc

    def body(x_ref, d_ref, w_ref, b_ref, dx_ref, dw_ref, db_ref):
        x = x_ref[0]
        wv = w_ref[...]
        v = _conv_pre(x, wv, b_ref[...])
        sg = jax.nn.sigmoid(v)
        dc = d_ref[0] * (sg * (1.0 + v * (1.0 - sg)))
        row = lax.broadcasted_iota(jnp.int32, x.shape, 0)
        dx = dc * wv[SSM_CONV - 1:SSM_CONV, :]
        dws = [jnp.sum(dc * x, axis=0, keepdims=True)]
        for k in range(1, SSM_CONV):
            dx = dx + jnp.where(row < S - k, pltpu.roll(dc, S - k, 0), 0.0) * wv[SSM_CONV - 1 - k:SSM_CONV - k, :]
            dws.append(jnp.sum(dc * jnp.where(row >= k, pltpu.roll(x, k, 0), 0.0), axis=0, keepdims=True))
        dx_ref[0] = dx.astype(BF16)
        ridx = lax.broadcasted_iota(jnp.int32, (SSM_CONV, tc), 0)
        dw = jnp.zeros((SSM_CONV, tc), F32)
        for k in range(SSM_CONV):
            dw = jnp.where(ridx == SSM_CONV - 1 - k, dws[k], dw)
        db = jnp.sum(dc, axis=0, keepdims=True)

        @pl.when(pl.program_id(1) == 0)
        def _():
            dw_ref[...] = dw
            db_ref[...] = db

        @pl.when(pl.program_id(1) > 0)
        def _():
            dw_ref[...] += dw
            db_ref[...] += db

    return pl.pallas_call(
        body, out_shape=(SDS((B, S, n), BF16), SDS((SSM_CONV, n), F32), SDS((1, n), F32)), grid=(n // tc, B),
        in_specs=[pl.BlockSpec((1, S, tc), lambda j, b: (b, 0, j + off_x)), pl.BlockSpec((1, S, tc), lambda j, b: (b, 0, j)),
                  pl.BlockSpec((SSM_CONV, tc), lambda j, b: (0, j + off_w)), pl.BlockSpec((1, tc), lambda j, b: (0, j + off_w))],
        out_specs=(pl.BlockSpec((1, S, tc), lambda j, b: (b, 0, j)), pl.BlockSpec((SSM_CONV, tc), lambda j, b: (0, j)),
                   pl.BlockSpec((1, tc), lambda j, b: (0, j))),
        name=name, compiler_params=_params(("parallel", "arbitrary")),
    )(zx3, dxc, w, bias)


def _ssd_common(x, Bm, Cm, dtc_raw, dtr_raw, pr, pc):
    Q = SSM_CHUNK
    zc = dtc_raw + pr[0:1, :]
    dt_c = jax.nn.softplus(zc)
    dt_r = jax.nn.softplus(dtr_raw + pc[:, 0:1])
    A_r = -jnp.exp(pr[1:2, :])
    A_c = -jnp.exp(pc[:, 1:2])
    row = lax.broadcasted_iota(jnp.int32, (Q, Q), 0)
    col = lax.broadcasted_iota(jnp.int32, (Q, Q), 1)
    tril = jnp.where(row >= col, 1.0, 0.0).astype(F32)
    cs_c = jnp.dot(tril, dt_c * A_r, preferred_element_type=F32, precision=HI)
    cs_r = lax.dot_general(dt_r * A_c, tril, NT, preferred_element_type=F32, precision=HI)
    return zc, dt_c, A_r, cs_c, cs_r, row, col, tril


def _ssd_fwd(xc3, dtc, dtr, prow, pcol, plan=None):
    B, S, _ = xc3.shape
    Q, G, HG, P, N = SSM_CHUNK, SSM_N_GROUPS, SSM_HG, HEAD_DIM, SSM_D_STATE
    nc = S // Q
    xw = HG * P

    def body(x_ref, b_ref, c_ref, dtc_ref, dtr_ref, pr_ref, pc_ref, y_ref, st_ref, state):
        c = pl.program_id(2)

        @pl.when(c == 0)
        def _():
            state[...] = jnp.zeros_like(state)

        x, Bm, Cm = x_ref[0], b_ref[0], c_ref[0]
        pr = pr_ref[0]
        _, dt_c, _, cs_c, cs_r, row, col, _ = _ssd_common(x, Bm, Cm, dtc_ref[0, 0], dtr_ref[0, 0], pr, pc_ref[0])
        Bb, Cb = Bm.astype(BF16), Cm.astype(BF16)
        CB = lax.dot_general(Cb, Bb, NT, preferred_element_type=F32)
        ys = []
        for hg in range(HG):
            xh = x[:, P * hg:P * (hg + 1)]
            xt = xh * dt_c[:, hg:hg + 1]
            csc, csr = cs_c[:, hg:hg + 1], cs_r[hg:hg + 1, :]
            L = jnp.where(row >= col, jnp.exp(jnp.minimum(csc - csr, 0.0)), 0.0)
            ydiag = jnp.dot((CB * L).astype(BF16), xt.astype(BF16), preferred_element_type=F32)
            Sh = state[hg]
            yoff = lax.dot_general(Cb, Sh.astype(BF16), NT, preferred_element_type=F32) * jnp.exp(csc)
            ys.append(ydiag + yoff + pr[2:3, hg:hg + 1] * xh)
            st_ref[0, 0, 0, P * hg:P * (hg + 1), :] = Sh
            csq = csc[Q - 1:Q, :]
            upd = lax.dot_general((xt * jnp.exp(csq - csc)).astype(BF16), Bb, TN, preferred_element_type=F32)
            state[hg] = Sh * jnp.exp(csq) + upd
        y_ref[0] = jnp.concatenate([jnp.concatenate(ys[0:2], axis=-1), jnp.concatenate(ys[2:4], axis=-1)], axis=-1)

    bo, co = SSM_D_INNER // N, (SSM_D_INNER + SSM_BC_DIM) // N
    return _hosted(plan, "b_ssd_fwd", lambda comm: _pc(
        body, (xc3, xc3, xc3, dtc, dtr, prow, pcol),
        out_shape=(SDS((B, S, SSM_D_INNER), F32), SDS((B, G, nc, xw, N), F32)), grid=(G, B, nc),
        in_specs=[pl.BlockSpec((1, Q, xw), lambda g, b, c: (b, c, g)), pl.BlockSpec((1, Q, N), lambda g, b, c: (b, c, bo + g)),
                  pl.BlockSpec((1, Q, N), lambda g, b, c: (b, c, co + g)), pl.BlockSpec((1, 1, Q, HG), lambda g, b, c: (b, g, c, 0)),
                  pl.BlockSpec((1, 1, HG, Q), lambda g, b, c: (b, g, 0, c)), pl.BlockSpec((1, 3, HG), lambda g, b, c: (g, 0, 0)),
                  pl.BlockSpec((1, HG, 3), lambda g, b, c: (g, 0, 0))],
        out_specs=(pl.BlockSpec((1, Q, xw), lambda g, b, c: (b, c, g)), pl.BlockSpec((1, 1, 1, xw, N), lambda g, b, c: (b, g, c, 0, 0))),
        scratch_shapes=[pltpu.VMEM((HG, P, N), F32)], name="b_ssd_fwd", sem=("parallel", "arbitrary", "arbitrary"), comm=comm))


def _ssd_bwd(xc3, dtc, dtr, prow, pcol, states, dy3, plan=None):
    B, S, _ = xc3.shape
    Q, G, HG, P, N = SSM_CHUNK, SSM_N_GROUPS, SSM_HG, HEAD_DIM, SSM_D_STATE
    nc = S // Q
    xw = HG * P

    def body(x_ref, b_ref, c_ref, dtc_ref, dtr_ref, pr_ref, pc_ref, st_ref, dy_ref,
             dx_ref, db_ref, dc_ref, ddt_ref, dpar_ref, dstate):
        bi, ci = pl.program_id(1), pl.program_id(2)

        @pl.when(ci == 0)
        def _():
            dstate[...] = jnp.zeros_like(dstate)

        x, Bm, Cm, dy = x_ref[0], b_ref[0], c_ref[0], dy_ref[0]
        pr = pr_ref[0]
        zc, dt_c, A_r, cs_c, cs_r, row, col, tril = _ssd_common(x, Bm, Cm, dtc_ref[0, 0], dtr_ref[0, 0], pr, pc_ref[0])
        Bb, Cb = Bm.astype(BF16), Cm.astype(BF16)
        CB = lax.dot_general(Cb, Bb, NT, preferred_element_type=F32)
        CBt = lax.dot_general(Bb, Cb, NT, preferred_element_type=F32)
        lane4 = lax.broadcasted_iota(jnp.int32, (Q, HG), 1)
        lane4r = lax.broadcasted_iota(jnp.int32, (1, HG), 1)
        rowq = lax.broadcasted_iota(jnp.int32, (Q, 1), 0)
        dB = jnp.zeros((Q, N), F32)
        dC = jnp.zeros((Q, N), F32)
        dcs4 = jnp.zeros((Q, HG), F32)
        dtx4 = jnp.zeros((Q, HG), F32)
        dD4 = jnp.zeros((1, HG), F32)
        dxts, xhs, dyhs = [], [], []
        for hg in range(HG):
            xh = x[:, P * hg:P * (hg + 1)]
            dyh = dy[:, P * hg:P * (hg + 1)]
            xt = xh * dt_c[:, hg:hg + 1]
            xtb, dyb = xt.astype(BF16), dyh.astype(BF16)
            csc, csr = cs_c[:, hg:hg + 1], cs_r[hg:hg + 1, :]
            L = jnp.where(row >= col, jnp.exp(jnp.minimum(csc - csr, 0.0)), 0.0)
            Lt = jnp.where(col >= row, jnp.exp(jnp.minimum(csr - csc, 0.0)), 0.0)
            M, Mt = CB * L, CBt * Lt
            Sh = st_ref[0, 0, 0, P * hg:P * (hg + 1), :]
            dSh = dstate[hg]
            Shb, dShb = Sh.astype(BF16), dSh.astype(BF16)
            ecs = jnp.exp(csc)
            csq = csc[Q - 1:Q, :]
            dec = jnp.exp(csq - csc)
            dxt = jnp.dot(Mt.astype(BF16), dyb, preferred_element_type=F32)
            dxt = dxt + lax.dot_general(Bb, dShb, NT, preferred_element_type=F32) * dec
            Gm = lax.dot_general(dyb, xtb, NT, preferred_element_type=F32)
            Gt = lax.dot_general(xtb, dyb, NT, preferred_element_type=F32)
            dC = dC + jnp.dot((Gm * L).astype(BF16), Bb, preferred_element_type=F32)
            dB = dB + jnp.dot((Gt * Lt).astype(BF16), Cb, preferred_element_type=F32)
            dC = dC + jnp.dot(dyb, Shb, preferred_element_type=F32) * ecs
            dBst = jnp.dot(xtb, dShb, preferred_element_type=F32) * dec
            dB = dB + dBst
            dcs = jnp.sum(Gm * M, axis=1, keepdims=True) - jnp.sum(Gt * Mt, axis=1, keepdims=True)
            yoff = lax.dot_general(Cb, Shb, NT, preferred_element_type=F32) * ecs
            dcs = dcs + jnp.sum(yoff * dyh, axis=1, keepdims=True)
            r = jnp.sum(dBst * Bm, axis=1, keepdims=True)
            dcs = dcs - r
            extra = jnp.sum(r, axis=0, keepdims=True) + jnp.exp(csq) * jnp.sum(
                jnp.sum(dSh * Sh, axis=1, keepdims=True), axis=0, keepdims=True)
            dcs = dcs + jnp.where(rowq == Q - 1, extra, 0.0)
            dcs4 = jnp.where(lane4 == hg, dcs, dcs4)
            dtx4 = jnp.where(lane4 == hg, jnp.sum(dxt * xh, axis=1, keepdims=True), dtx4)
            dD4 = jnp.where(lane4r == hg, jnp.sum(jnp.sum(dyh * xh, axis=1, keepdims=True), axis=0, keepdims=True), dD4)
            dstate[hg] = dSh * jnp.exp(csq) + lax.dot_general((dyh * ecs).astype(BF16), Cb, TN, preferred_element_type=F32)
            dxts.append(dxt)
            xhs.append(xh)
            dyhs.append(dyh)
        da4 = lax.dot_general(tril, dcs4, TN, preferred_element_type=F32, precision=HI)
        ddt4 = da4 * A_r + dtx4
        ddtraw = ddt4 * jax.nn.sigmoid(zc)
        ddt_ref[0, 0] = ddtraw
        dxs = [dxts[hg] * dt_c[:, hg:hg + 1] + pr[2:3, hg:hg + 1] * dyhs[hg] for hg in range(HG)]
        dx_ref[0] = jnp.concatenate([jnp.concatenate(dxs[0:2], axis=-1), jnp.concatenate(dxs[2:4], axis=-1)], axis=-1)
        db_ref[0] = dB
        dc_ref[0] = dC
        d_bias = jnp.sum(ddtraw, axis=0, keepdims=True)
        d_alog = jnp.sum(da4 * dt_c, axis=0, keepdims=True) * A_r
        r3 = lax.broadcasted_iota(jnp.int32, (3, HG), 0)
        dpar = jnp.where(r3 == 0, d_bias, jnp.where(r3 == 1, d_alog, dD4))
        first = (bi == 0) & (ci == 0)

        @pl.when(first)
        def _():
            dpar_ref[0] = dpar

        @pl.when(jnp.logical_not(first))
        def _():
            dpar_ref[0] += dpar

    rc = lambda c: nc - 1 - c
    bo, co = SSM_D_INNER // N, (SSM_D_INNER + SSM_BC_DIM) // N
    return _hosted(plan, "b_ssd_bwd", lambda comm: _pc(
        body, (xc3, xc3, xc3, dtc, dtr, prow, pcol, states, dy3),
        out_shape=(SDS((B, S, SSM_D_INNER), F32), SDS((B, S, SSM_BC_DIM), F32), SDS((B, S, SSM_BC_DIM), F32),
                   SDS((B, G, S, HG), F32), SDS((G, 3, HG), F32)),
        grid=(G, B, nc),
        in_specs=[pl.BlockSpec((1, Q, xw), lambda g, b, c: (b, rc(c), g)), pl.BlockSpec((1, Q, N), lambda g, b, c: (b, rc(c), bo + g)),
                  pl.BlockSpec((1, Q, N), lambda g, b, c: (b, rc(c), co + g)), pl.BlockSpec((1, 1, Q, HG), lambda g, b, c: (b, g, rc(c), 0)),
                  pl.BlockSpec((1, 1, HG, Q), lambda g, b, c: (b, g, 0, rc(c))), pl.BlockSpec((1, 3, HG), lambda g, b, c: (g, 0, 0)),
                  pl.BlockSpec((1, HG, 3), lambda g, b, c: (g, 0, 0)),
                  pl.BlockSpec((1, 1, 1, xw, N), lambda g, b, c: (b, g, rc(c), 0, 0)), pl.BlockSpec((1, Q, xw), lambda g, b, c: (b, rc(c), g))],
        out_specs=(pl.BlockSpec((1, Q, xw), lambda g, b, c: (b, rc(c), g)), pl.BlockSpec((1, Q, N), lambda g, b, c: (b, rc(c), g)),
                   pl.BlockSpec((1, Q, N), lambda g, b, c: (b, rc(c), g)), pl.BlockSpec((1, 1, Q, HG), lambda g, b, c: (b, g, rc(c), 0)),
                   pl.BlockSpec((1, 3, HG), lambda g, b, c: (g, 0, 0))),
        scratch_shapes=[pltpu.VMEM((HG, P, N), F32)], name="b_ssd_bwd", sem=("parallel", "arbitrary", "arbitrary"), comm=comm))


GN_W = SSM_D_INNER // SSM_N_GROUPS


def _gate_fwd(y, zx, nw):
    T = y.shape[0]
    tm = _pick(T, 256, 8)

    def body(y_ref, z_ref, w_ref, o_ref):
        z = z_ref[...]
        gt = y_ref[...] * (z * jax.nn.sigmoid(z))
        outs = []
        for k in range(SSM_N_GROUPS):
            gk = gt[:, GN_W * k:GN_W * (k + 1)]
            outs.append(gk * lax.rsqrt(jnp.mean(gk * gk, axis=-1, keepdims=True) + NORM_EPS))
        o_ref[...] = (jnp.concatenate(outs, axis=-1) * w_ref[...]).astype(BF16)

    row = pl.BlockSpec((tm, SSM_D_INNER), lambda i: (i, 0))
    return pl.pallas_call(
        body, out_shape=SDS((T, SSM_D_INNER), BF16), grid=(T // tm,),
        in_specs=[row, row, pl.BlockSpec((1, SSM_D_INNER), lambda i: (0, 0))], out_specs=row, name="b_gate_fwd",
        compiler_params=_params(("parallel",)),
    )(y, zx, nw)


def _gate_bwd(dgn, y, zx, nw):
    T = y.shape[0]
    tm = _pick(T, 256, 8)

    def body(d_ref, y_ref, z_ref, w_ref, dy_ref, dz_ref, dw_ref):
        z, yv, w = z_ref[...], y_ref[...], w_ref[...]
        sg = jax.nn.sigmoid(z)
        sz = z * sg
        gt = yv * sz
        gw = d_ref[...] * w
        dgts, dws = [], []
        for k in range(SSM_N_GROUPS):
            sl = slice(GN_W * k, GN_W * (k + 1))
            gk, gwk = gt[:, sl], gw[:, sl]
            rstd = lax.rsqrt(jnp.mean(gk * gk, axis=-1, keepdims=True) + NORM_EPS)
            dgts.append(rstd * gwk - gk * (rstd * rstd * rstd) * jnp.mean(gwk * gk, axis=-1, keepdims=True))
            dws.append(jnp.sum(d_ref[:, sl] * gk * rstd, axis=0, keepdims=True))
        dgt = jnp.concatenate(dgts, axis=-1)
        dy_ref[...] = dgt * sz
        dz_ref[...] = (dgt * yv * (sg * (1.0 + z * (1.0 - sg)))).astype(BF16)
        dw = jnp.concatenate(dws, axis=-1)

        @pl.when(pl.program_id(0) == 0)
        def _():
            dw_ref[...] = dw

        @pl.when(pl.program_id(0) > 0)
        def _():
            dw_ref[...] += dw

    row = pl.BlockSpec((tm, SSM_D_INNER), lambda i: (i, 0))
    vec = pl.BlockSpec((1, SSM_D_INNER), lambda i: (0, 0))
    return pl.pallas_call(
        body, out_shape=(SDS((T, SSM_D_INNER), F32), SDS((T, SSM_D_INNER), BF16), SDS((1, SSM_D_INNER), F32)), grid=(T // tm,),
        in_specs=[row, row, row, vec], out_specs=(row, row, vec), name="b_gate_bwd", compiler_params=_params(("arbitrary",)),
    )(dgn, y, zx, nw)


N_CHIPS = 4


def _dev_block(ref, kind, j, size):
    if kind == "slot":
        return ref.at[j]
    start = pl.multiple_of(j * size, size)
    nd = len(ref.shape)
    if kind == "col":
        return ref.at[(slice(None),) * (nd - 1) + (pl.ds(start, size),)]
    return ref.at[(slice(None),) * (nd - 2) + (pl.ds(start, size), slice(None))]


def _dma_sems(n, k):
    return [pltpu.SemaphoreType.DMA((n, k)), pltpu.SemaphoreType.DMA((n, k)), pltpu.SemaphoreType.DMA((n, k))]


def _place(shard, layer, kind, full_shape, dev, name):
    k, n = shard.shape[1:]
    tr = _pick(k, 512, 16)
    nb = k // tr

    def body(dev_ref, s_ref, o_ref):
        if kind == "slot":
            o_ref[0] = s_ref[0].astype(BF16)
        else:
            o_ref[...] = s_ref[0].astype(BF16)

    out_spec = {"slot": pl.BlockSpec((1, tr, n), lambda i, d: (d[0], i, 0)),
                "row": pl.BlockSpec((tr, n), lambda i, d: (d[0] * nb + i, 0)),
                "col": pl.BlockSpec((tr, n), lambda i, d: (i, d[0]))}[kind]
    return pl.pallas_call(
        body, out_shape=SDS(full_shape, BF16),
        grid_spec=pltpu.PrefetchScalarGridSpec(
            num_scalar_prefetch=1, grid=(nb,), in_specs=[pl.BlockSpec((1, tr, n), lambda i, d: (layer, i, 0))], out_specs=out_spec),
        name=name, compiler_params=_params(("arbitrary",)),
    )(dev, shard)


def _run_comm(comm, name):
    c_in = len(comm.inputs)

    def body(*refs):
        cins, couts, sems = refs[:c_in], refs[c_in:c_in + len(comm.out_shapes)], refs[c_in + len(comm.out_shapes):]
        for _, fn in comm.phases:
            fn(cins, couts, sems)

    return pl.pallas_call(
        body, out_shape=list(comm.out_shapes), in_specs=[ANY] * c_in, out_specs=[ANY] * len(comm.out_shapes),
        input_output_aliases=dict(comm.aliases), scratch_shapes=list(comm.sems), name=name,
    )(*comm.inputs)


def _gather_comm(items, mid=0.7):
    n = len(items)

    def tools(srcs, dsts, sems):
        send_sems, recv_sems, local_sems = sems
        px, py, pc = lax.axis_index("x"), lax.axis_index("y"), lax.axis_index("c")
        me, sibling = (px, py, pc), (px, py, 1 - pc)
        chips = [(1 - px, py), (px, 1 - py), (1 - px, 1 - py)]

        def blk(a, dev):
            return _dev_block(dsts[a], items[a][1], 4 * dev[0] + 2 * dev[1] + dev[2], items[a][2])

        def copy(a, k, block, to, src=None):
            return pltpu.make_async_remote_copy(
                src_ref=blk(a, block) if src is None else src, dst_ref=blk(a, block),
                send_sem=send_sems.at[a, k], recv_sem=recv_sems.at[a, k], device_id=to, device_id_type=MESH)

        def mine():
            return [pltpu.make_async_copy(srcs[a], blk(a, me), local_sems.at[a, 0]) for a in range(n) if not items[a][4]]

        def first():
            out = []
            for a in range(n):
                src = blk(a, me) if items[a][4] else srcs[a]
                out.append(copy(a, 0, me, sibling, src=src))
                out += [copy(a, 1 + j, me, (*chip, pc), src=src) for j, chip in enumerate(chips)]
            return out

        def passed():
            return [copy(a, 4 + j, (*chip, pc), sibling) for j, chip in enumerate(chips) for a in range(n)]

        return me, sibling, chips, pc, copy, mine, first, passed

    def start(srcs, dsts, sems):
        *_, mine, first, _ = tools(srcs, dsts, sems)
        for cp in mine() + first():
            cp.start()

    def forward(srcs, dsts, sems):
        me, _, chips, pc, copy, _, _, passed = tools(srcs, dsts, sems)
        fwd = passed()
        for j, chip in enumerate(chips):
            for a in range(n):
                copy(a, 1 + j, (*chip, pc), me).wait_recv()
                fwd[j * n + a].start()

    def finish(srcs, dsts, sems):
        me, sibling, chips, pc, copy, mine, first, passed = tools(srcs, dsts, sems)
        for a in range(n):
            copy(a, 0, sibling, me).wait_recv()
            for j, chip in enumerate(chips):
                copy(a, 4 + j, (*chip, 1 - pc), me).wait_recv()
        for cp in first() + passed():
            cp.wait_send()
        for cp in mine():
            cp.wait()

    return _Comm([it[0] for it in items], [SDS(it[3], it[0].dtype) for it in items],
                 {a: a for a in range(n) if items[a][4]}, _dma_sems(n, 7), [(0.0, start), (mid, forward), (1.0, finish)])


def _gather(items, name):
    return _run_comm(_gather_comm(items), name)


def _reduce_d2d_comm(items):
    n = len(items)

    def copies(gs, gots, sems):
        send_sems, recv_sems, _ = sems
        px, py, pc = lax.axis_index("x"), lax.axis_index("y"), lax.axis_index("c")
        out = []
        for a in range(n):
            _, kind, size, _ = items[a]
            for q in range(N_CHIPS):
                out.append(pltpu.make_async_remote_copy(
                    src_ref=_dev_block(gs[a], kind, 2 * q + 1 - pc, size), dst_ref=gots[a].at[q], send_sem=send_sems.at[a, q],
                    recv_sem=recv_sems.at[a, q], device_id=(px, py, 1 - pc), device_id_type=MESH))
        return out

    def start(gs, gots, sems):
        for cp in copies(gs, gots, sems):
            cp.start()

    def finish(gs, gots, sems):
        for cp in copies(gs, gots, sems):
            cp.wait()

    return _Comm([it[0] for it in items], [SDS((N_CHIPS,) + tuple(it[3]), F32) for it in items], {},
                 _dma_sems(n, N_CHIPS), [(0.0, start), (1.0, finish)])


def _pair_sum(g, got, kind, core, name):
    _, k, n = got.shape
    tr = _pick(k, max(16, STREAM_VMEM // (2 * n * 10)), 16)
    nb = k // tr

    def body(c_ref, g_ref, s_ref, o_ref):
        mine = g_ref[0] if kind == "slot" else g_ref[...]
        o_ref[0] = (mine + s_ref[0]).astype(BF16)

    g_spec = {"slot": pl.BlockSpec((1, tr, n), lambda q, i, c: (2 * q + c[0], i, 0)),
              "row": pl.BlockSpec((tr, n), lambda q, i, c: ((2 * q + c[0]) * nb + i, 0)),
              "col": pl.BlockSpec((tr, n), lambda q, i, c: (i, 2 * q + c[0]))}[kind]
    part = pl.BlockSpec((1, tr, n), lambda q, i, c: (q, i, 0))
    return pl.pallas_call(
        body, out_shape=SDS((N_CHIPS, k, n), BF16),
        grid_spec=pltpu.PrefetchScalarGridSpec(num_scalar_prefetch=1, grid=(N_CHIPS, nb), in_specs=[g_spec, part], out_specs=part),
        name=name, compiler_params=_params(("arbitrary", "arbitrary")),
    )(core, g, got)


def _reduce_ici_comm(parts):
    n = len(parts)

    def copies(ps, rs, sems, arriving):
        send_sems, recv_sems, _ = sems
        px, py, pc = lax.axis_index("x"), lax.axis_index("y"), lax.axis_index("c")
        my_chip = 2 * px + py
        out = []
        for a in range(n):
            for k in range(1, N_CHIPS):
                qx, qy = px ^ (k >> 1), py ^ (k & 1)
                q = 2 * qx + qy
                out.append(pltpu.make_async_remote_copy(
                    src_ref=ps[a].at[q], dst_ref=rs[a].at[q] if arriving else rs[a].at[my_chip], send_sem=send_sems.at[a, k - 1],
                    recv_sem=recv_sems.at[a, k - 1], device_id=(qx, qy, pc), device_id_type=MESH))
        return out

    def start(ps, rs, sems):
        for cp in copies(ps, rs, sems, False):
            cp.start()

    def finish(ps, rs, sems):
        for cp in copies(ps, rs, sems, True):
            cp.wait_recv()
        for cp in copies(ps, rs, sems, False):
            cp.wait_send()

    return _Comm(list(parts), [SDS(p.shape, p.dtype) for p in parts], {}, _dma_sems(n, N_CHIPS - 1),
                 [(0.0, start), (1.0, finish)])


def _adam_update(g, w, m, v):
    c1 = 1.0 - ADAM_B1 ** ADAM_STEP
    c2 = 1.0 - ADAM_B2 ** ADAM_STEP
    nm = ADAM_B1 * m + (1.0 - ADAM_B1) * g
    nv = ADAM_B2 * v + (1.0 - ADAM_B2) * (g * g)
    delta = -ADAM_LR * ((nm / c1) / (jnp.sqrt(nv / c2) + ADAM_EPS) + ADAM_WD * w)
    return delta, nm, nv


def _adamw(parts, recv, w, m, v, layer, prev, chip, name):
    _, R, C = w.shape
    row_bytes = 2 * C * (N_CHIPS * 2 + 7 * 4)
    tr = _pick(R, max(16, STREAM_VMEM // row_bytes), 16)
    n_prev = 0 if prev is None else 4

    def body(ch_ref, own_ref, r1_ref, r2_ref, r3_ref, w_ref, m_ref, v_ref, *rest):
        g_ref, d_ref, nm_ref, nv_ref = rest[n_prev:]
        g = own_ref[0].astype(F32)
        for r_ref in (r1_ref, r2_ref, r3_ref):
            g = g + r_ref[0].astype(F32)
        g_ref[0] = g
        d_ref[0], nm_ref[0], nv_ref[0] = _adam_update(g, w_ref[0], m_ref[0], v_ref[0])

    lay = pl.BlockSpec((1, tr, C), lambda i, ch: (layer, i, 0))
    other = lambda k: pl.BlockSpec((1, tr, C), lambda i, ch: (ch[0] ^ k, i, 0))
    out = SDS(w.shape, F32)
    return pl.pallas_call(
        body, out_shape=(out, out, out, out),
        grid_spec=pltpu.PrefetchScalarGridSpec(
            num_scalar_prefetch=1, grid=(R // tr,),
            in_specs=[pl.BlockSpec((1, tr, C), lambda i, ch: (ch[0], i, 0)), other(2), other(1), other(3), lay, lay, lay]
            + [ANY] * n_prev,
            out_specs=(lay, lay, lay, lay)),
        input_output_aliases={8 + k: k for k in range(n_prev)},
        name=name, compiler_params=_params(("arbitrary",)),
    )(chip, parts, recv, recv, recv, w, m, v, *(prev or ()))


def _small_adamw(gathered, ws, ms, vs):
    n = len(ws)

    def body(*refs):
        g_in, w_in, m_in, v_in = refs[:n], refs[n:2 * n], refs[2 * n:3 * n], refs[3 * n:4 * n]
        outs = refs[4 * n:]
        for i in range(n):
            g = g_in[i][0]
            for dev in range(1, N_DEV):
                g = g + g_in[i][dev]
            d, nm, nv = _adam_update(g, w_in[i][...], m_in[i][...], v_in[i][...])
            outs[i][...] = g
            outs[n + i][...] = d
            outs[2 * n + i][...] = nm
            outs[3 * n + i][...] = nv

    shapes = [SDS(w.shape, F32) for w in ws]
    outs = pl.pallas_call(body, out_shape=shapes * 4, name="small_adamw")(*gathered, *ws, *ms, *vs)
    return outs[:n], outs[n:2 * n], outs[2 * n:3 * n], outs[3 * n:]


W_NAMES = ("norm_mix_w", "norm_mlp_w", "a_w_qkv", "a_b_qkv", "a_sinks", "a_w_o", "a_b_o", "b_in_w", "b_conv_w", "b_conv_b",
           "b_dt_bias", "b_a_log", "b_d", "b_norm_w", "b_out_w", "c_w_qkv", "c_w_o", "mlp_w_up", "mlp_w_down", "final_norm_w")
BIG_KIND = {"a_w_qkv": "slot", "a_w_o": "row", "b_in_w": "slot", "b_out_w": "row", "c_w_qkv": "col", "c_w_o": "row",
            "mlp_w_up": "col", "mlp_w_down": "row"}
SMALL_SHARDED = {"a_b_qkv": 1, "a_b_o": 1, "b_conv_w": 2}
SMALL_REPLICATED = ("norm_mix_w", "norm_mlp_w", "a_sinks", "b_conv_b", "b_dt_bias", "b_a_log", "b_d", "b_norm_w", "final_norm_w")


def _layer_big(i):
    kind, j = i % 3, i // 3
    mix = {0: [("a_w_qkv", j), ("a_w_o", j)], 1: [("b_in_w", 0), ("b_out_w", 0)], 2: [("c_w_qkv", 0), ("c_w_o", 0)]}[kind]
    return mix + [("mlp_w_up", i), ("mlp_w_down", i)]


def _block_size(kind, shard2d):
    return {"slot": None, "row": shard2d[0], "col": shard2d[1]}[kind]


def _full2d(kind, shard2d):
    k, n = shard2d
    return {"slot": (N_DEV, k, n), "row": (N_DEV * k, n), "col": (k, N_DEV * n)}[kind]


def _from_slots(t, ax):
    s = t.shape[1:]
    return jnp.moveaxis(t, 0, ax).reshape(s[:ax] + (N_DEV * s[ax],) + s[ax + 1:])


def _to_slots(g, ax):
    s = g.shape
    return jnp.moveaxis(g.reshape(s[:ax] + (N_DEV, s[ax] // N_DEV) + s[ax + 1:]), ax, 0)


def _rope_tables(positions):
    half = HEAD_DIM // 2
    inv = ROPE_THETA ** (-(jnp.arange(LANES, dtype=jnp.int32) % half).astype(F32) / half)
    ang = positions.astype(F32).reshape(-1, 1) * inv
    return jnp.cos(ang), jnp.sin(ang)


def _swa_fwd(u, h, p, j, B, S, cos, sin, tag, nw, plan=None):
    qkv = _matmul(u, p["a_w_qkv"][j], out_dtype=BF16, bias=p["a_b_qkv"][j][None], rope=(cos, sin),
                  rope_cols=A_Q_DIM + A_KV_DIM, name=f"{tag}_qkv", plan=plan)
    o, lse = _attn_fwd(qkv, B, S, 1, n_heads=A_N_HEADS, n_kv=A_N_KV, q_col=0, k_col=A_Q_DIM, v_col=A_Q_DIM + A_KV_DIM,
                       max_dist=A_WINDOW - 1, sinks=p["a_sinks"][j], name=f"{tag}_attn", plan=plan)
    h1, u2 = _matmul(o, p["a_w_o"][j], bias=p["a_b_o"][j][None], resid=h, norm_out=nw, name=f"{tag}_o")
    return h1, u2, (qkv, o, lse)


def _swa_bwd(dh1, u, saved, p, j, B, S, cos, sin, tag, norm, plan=None):
    qkv, o, lse = saved
    kw = dict(n_heads=A_N_HEADS, n_kv=A_N_KV, q_col=0, k_col=A_Q_DIM, v_col=A_Q_DIM + A_KV_DIM, max_dist=A_WINDOW - 1)
    g = {}
    do = _matmul(dh1, p["a_w_o"][j], tb=True, name=f"{tag}_do")
    g["a_w_o"] = _matmul(o, dh1, ta=True, name=f"{tag}_dwo")
    g["a_b_o"] = _colsum(dh1, f"{tag}_dbo")[0]
    sk = jnp.pad(p["a_sinks"][j], (0, LANES - A_N_HEADS))[None]
    delta, dob, dsink = _delta(do, o, lse, sk, name=f"{tag}_delta")
    g["a_sinks"] = dsink[0, :A_N_HEADS]
    dq = _attn_dq(qkv, dob, lse, delta, cos, sin, B, S, 1, name=f"{tag}_dq", plan=plan, **kw)
    dk, dv = _attn_dkv(qkv, dob, lse, delta, cos, sin, B, S, 1, name=f"{tag}_dkv", plan=plan, **kw)
    dqkv = jnp.concatenate([dq, dk, dv], axis=1)
    g["a_w_qkv"] = _matmul(u, dqkv, ta=True, name=f"{tag}_dwqkv")
    g["a_b_qkv"] = _colsum(dqkv, f"{tag}_dbqkv")[0]
    if plan is not None:
        plan.grads(3 * j, "mix", {"a_w_qkv": g["a_w_qkv"], "a_w_o": g["a_w_o"]})
    dh, dnw = _matmul(dqkv, p["a_w_qkv"][j], tb=True, norm_bwd=(norm[0], norm[1], dh1), name=f"{tag}_du", plan=plan)
    return dh, dnw, g


def _group_cols(gi, qkv):
    W = C_HEADS * HEAD_DIM
    if C_PATTERNS[gi][1] == 1:
        return qkv, (gi * W, (3 + gi) * W, (6 + gi) * W)
    part = jnp.concatenate([qkv[:, (3 * j + gi) * W:(3 * j + gi + 1) * W] for j in range(3)], axis=1)
    return part, (0, W, 2 * W)


def _dil_fwd(u, h, p, B, S, cos, sin, nw, plan=None):
    W = C_HEADS * HEAD_DIM
    qkv = _matmul(u, p["c_w_qkv"][0], out_dtype=BF16, rope=(cos, sin), rope_cols=6 * W, name="c_qkv", plan=plan)
    os_, lses, parts = [], [], []
    for gi, (window, dil) in enumerate(C_PATTERNS):
        part, (qc, kc, vc) = _group_cols(gi, qkv)
        o, lse = _attn_fwd(part, B, S, dil, n_heads=C_HEADS, n_kv=C_HEADS, q_col=qc, k_col=kc, v_col=vc,
                           max_dist=window // dil, sinks=None, name=f"c_attn{gi}", plan=plan)
        os_.append(o)
        lses.append(lse)
        parts.append((part, (qc, kc, vc)))
    o, lse = _merge(os_, lses)
    h1, u2 = _matmul(o, p["c_w_o"][0], resid=h, norm_out=nw, name="c_o")
    return h1, u2, (parts, o, lse)


def _dil_bwd(dh1, u, saved, p, B, S, cos, sin, norm, plan=None):
    parts, o, lse = saved
    g = {}
    do = _matmul(dh1, p["c_w_o"][0], tb=True, name="c_do")
    g["c_w_o"] = _matmul(o, dh1, ta=True, name="c_dwo")[None]
    delta, dob = _delta(do, o, name="c_delta")
    dqs, dks, dvs = [], [], []
    for gi, (window, dil) in enumerate(C_PATTERNS):
        part, (qc, kc, vc) = parts[gi]
        kw = dict(n_heads=C_HEADS, n_kv=C_HEADS, q_col=qc, k_col=kc, v_col=vc, max_dist=window // dil)
        dqs.append(_attn_dq(part, dob, lse, delta, cos, sin, B, S, dil, name=f"c_dq{gi}", **kw))
        dk, dv = _attn_dkv(part, dob, lse, delta, cos, sin, B, S, dil, name=f"c_dkv{gi}", **kw)
        dks.append(dk)
        dvs.append(dv)
    dqkv = jnp.concatenate(dqs + dks + dvs, axis=1)
    g["c_w_qkv"] = _matmul(u, dqkv, ta=True, name="c_dwqkv", plan=plan)[None]
    if plan is not None:
        plan.grads(2, "mix", {"c_w_qkv": g["c_w_qkv"][0], "c_w_o": g["c_w_o"][0]})
    dh, dnw = _matmul(dqkv, p["c_w_qkv"][0], tb=True, norm_bwd=(norm[0], norm[1], dh1), name="c_du", plan=plan)
    return dh, dnw, g


def _ssm_params(p):
    par = jnp.stack([p["b_dt_bias"][0], p["b_a_log"][0], p["b_d"][0]], axis=0)
    prow = par.reshape(3, SSM_N_GROUPS, SSM_HG).transpose(1, 0, 2)
    return prow, prow.transpose(0, 2, 1)


def _mamba_fwd(u, h, p, B, S, nw, plan=None):
    T = B * S
    G, HG = SSM_N_GROUPS, SSM_HG
    w_in = p["b_in_w"][0]
    nzx = SSM_D_INNER + SSM_CONV_DIM
    w_dt = jnp.pad(w_in[:, nzx:], ((0, 0), (0, LANES - SSM_N_HEADS)))
    zx = _matmul(u, w_in[:, :nzx], name="b_zx", plan=plan)
    dtraw = _matmul(u, w_dt, name="b_dt")[:, :SSM_N_HEADS]
    dtc = dtraw.reshape(B, S, G, HG).transpose(0, 2, 1, 3)
    dtr = dtraw.reshape(B, S, G, HG).transpose(0, 2, 3, 1)
    prow, pcol = _ssm_params(p)
    zx3 = zx.reshape(B, S, nzx)
    xc3 = _conv_fwd(zx3, p["b_conv_w"][0], p["b_conv_b"])
    y3, states = _ssd_fwd(xc3, dtc, dtr, prow, pcol, plan=plan)
    y = y3.reshape(T, SSM_D_INNER)
    gn = _gate_fwd(y, zx, p["b_norm_w"])
    h1, u2 = _matmul(gn, p["b_out_w"][0], resid=h, norm_out=nw, name="b_out", plan=plan)
    return h1, u2, (zx, dtc, dtr, xc3, y, states, gn, w_dt)


def _mamba_bwd(dh1, u, saved, p, B, S, norm, plan=None):
    T = B * S
    zx, dtc, dtr, xc3, y, states, gn, w_dt = saved
    nzx = SSM_D_INNER + SSM_CONV_DIM
    w_in = p["b_in_w"][0]
    prow, pcol = _ssm_params(p)
    g = {}
    dgn = _matmul(dh1, p["b_out_w"][0], tb=True, name="b_dgn")
    g["b_out_w"] = _matmul(gn, dh1, ta=True, name="b_dwout")[None]
    dy, dz, dnw = _gate_bwd(dgn, y, zx, p["b_norm_w"])
    g["b_norm_w"] = dnw
    dx3, dB3, dC3, ddt, dpar = _ssd_bwd(xc3, dtc, dtr, prow, pcol, states, dy.reshape(B, S, SSM_D_INNER), plan=plan)
    dpar = dpar.transpose(1, 0, 2).reshape(3, SSM_N_HEADS)
    g["b_dt_bias"], g["b_a_log"], g["b_d"] = dpar[0:1], dpar[1:2], dpar[2:3]
    zx3 = zx.reshape(B, S, nzx)
    cw, cb = p["b_conv_w"][0], p["b_conv_b"]
    parts, dws, dbs = [], [], []
    for col0, dpart, nm in ((0, dx3, "b_conv_bwd_x"), (SSM_D_INNER, dB3, "b_conv_bwd_b"),
                            (SSM_D_INNER + SSM_BC_DIM, dC3, "b_conv_bwd_c")):
        dxp, dw, db = _conv_bwd(zx3, dpart, cw, cb, col0, nm)
        parts.append(dxp.reshape(T, -1))
        dws.append(dw)
        dbs.append(db)
    g["b_conv_w"] = jnp.concatenate(dws, axis=1)[None]
    g["b_conv_b"] = jnp.concatenate(dbs, axis=1)
    dzx = jnp.concatenate([dz] + parts, axis=1)
    ddtraw = ddt.transpose(0, 2, 1, 3).reshape(T, SSM_N_HEADS)
    ddtp = jnp.pad(ddtraw, ((0, 0), (0, LANES - SSM_N_HEADS)))
    dw_zx = _matmul(u, dzx, ta=True, name="b_dwzx")
    dw_dt = _matmul(u, ddtp, ta=True, name="b_dwdt")[:, :SSM_N_HEADS]
    g["b_in_w"] = jnp.concatenate([dw_zx, dw_dt], axis=1)[None]
    if plan is not None:
        plan.grads(1, "mix", {"b_in_w": g["b_in_w"][0], "b_out_w": g["b_out_w"][0]})
    du = _matmul(dzx, w_in[:, :nzx], tb=True, name="b_du_zx", plan=plan)
    dh, dnw = _matmul(ddtp, w_dt, tb=True, resid=du, norm_bwd=(norm[0], norm[1], dh1), name="b_du_dt")
    return dh, dnw, g


def _local_step(x, positions, p, target, plan=None):
    B, S, D = x.shape
    T = B * S
    cos, sin = _rope_tables(positions)
    h = x.reshape(T, D)
    tape = []
    u = _rmsnorm_fwd(h, p["norm_mix_w"][0], "l0_norm_mix")
    for i in range(DEPTH):
        kind, j = i % 3, i // 3
        nw = p["norm_mlp_w"][i]
        if kind == 0:
            h1, u2, saved = _swa_fwd(u, h, p, j, B, S, cos, sin, f"a{j}", nw, plan)
        elif kind == 1:
            h1, u2, saved = _mamba_fwd(u, h, p, B, S, nw, plan)
        else:
            h1, u2, saved = _dil_fwd(u, h, p, B, S, cos, sin, nw, plan)
        r, s = _matmul(u2, p["mlp_w_up"][i], out_dtype=BF16, relu2=True, name=f"l{i}_up", plan=plan)
        if i + 1 < DEPTH:
            h2, u_next = _matmul(s, p["mlp_w_down"][i], resid=h1, norm_out=p["norm_mix_w"][i + 1], name=f"l{i}_down", plan=plan)
        else:
            h2, u_next = _matmul(s, p["mlp_w_down"][i], resid=h1, name=f"l{i}_down", plan=plan), None
        tape.append((h, u, saved, h1, u2, r, s))
        h, u = h2, u_next
    dh, dwf, loss = _final_loss(h, target.reshape(T, D), p["final_norm_w"])
    grads = {"final_norm_w": dwf[0]}
    per_layer = {n: [None] * DEPTH for n in ("norm_mix_w", "norm_mlp_w", "mlp_w_up", "mlp_w_down")}
    a_grads = [None, None]
    for i in reversed(range(DEPTH)):
        kind, j = i % 3, i // 3
        h0, u, saved, h1, u2, r, s = tape[i]
        da = _matmul(dh, p["mlp_w_down"][i], tb=True, out_dtype=BF16, mul=r, mul_scale=2.0, name=f"l{i}_da", plan=plan)
        per_layer["mlp_w_down"][i] = _matmul(s, dh, ta=True, name=f"l{i}_dwdown")
        per_layer["mlp_w_up"][i] = _matmul(u2, da, ta=True, name=f"l{i}_dwup")
        if plan is not None:
            plan.grads(i, "mlp", {"mlp_w_up": per_layer["mlp_w_up"][i], "mlp_w_down": per_layer["mlp_w_down"][i]})
        dh1, dnw = _matmul(da, p["mlp_w_up"][i], tb=True, norm_bwd=(h1, p["norm_mlp_w"][i], dh), name=f"l{i}_du2", plan=plan)
        per_layer["norm_mlp_w"][i] = dnw[0]
        norm = (h0, p["norm_mix_w"][i])
        if kind == 0:
            dh, dnw, g = _swa_bwd(dh1, u, saved, p, j, B, S, cos, sin, f"a{j}", norm, plan)
            a_grads[j] = g
        elif kind == 1:
            dh, dnw, g = _mamba_bwd(dh1, u, saved, p, B, S, norm, plan)
            grads.update(g)
        else:
            dh, dnw, g = _dil_bwd(dh1, u, saved, p, B, S, cos, sin, norm, plan)
            grads.update(g)
        per_layer["norm_mix_w"][i] = dnw[0]
    for n in ("norm_mix_w", "norm_mlp_w"):
        grads[n] = jnp.stack(per_layer[n], axis=0)
    for n in ("mlp_w_up", "mlp_w_down"):
        grads[n] = per_layer[n]
    for n in ("a_b_qkv", "a_sinks", "a_b_o"):
        grads[n] = jnp.stack([a_grads[0][n], a_grads[1][n]], axis=0)
    for n in ("a_w_qkv", "a_w_o"):
        grads[n] = [a_grads[0][n], a_grads[1][n]]
    for n in ("b_in_w", "b_out_w", "c_w_qkv", "c_w_o"):
        grads[n] = [grads[n][0]]
    return loss, dh.reshape(B, S, D), grads


MIX = {0: ("a_w_qkv", "a_w_o"), 1: ("b_in_w", "b_out_w"), 2: ("c_w_qkv", "c_w_o")}
MLP = ("mlp_w_up", "mlp_w_down")
GATHER_FIRST = (0, MIX[0])
GATHER_HOSTS = {"a0_qkv": ((0, ("mlp_w_up",)),), "a0_attn": ((0, ("mlp_w_down",)),), "l0_up": ((1, ("b_in_w",)),),
                "l0_down": ((1, ("b_out_w",)),), "b_zx": ((1, ("mlp_w_up",)),),
                "b_ssd_fwd": ((1, ("mlp_w_down",)), (2, ("c_w_qkv",))), "b_out": ((2, ("c_w_o",)),),
                "l1_up": ((2, ("mlp_w_up",)),), "l1_down": ((2, ("mlp_w_down",)),),
                "c_qkv": ((3, MLP),), "c_attn1": ((3, MIX[0]),)}
REDUCE_HOSTS = {(3, "mlp"): ("l3_du2", "a1_dkv"), (3, "mix"): ("a1_du", "l2_da"),
                (2, "mlp"): ("l2_du2", "c_dwqkv"), (2, "mix"): ("c_du", "b_ssd_bwd"),
                (1, "mlp"): ("l1_du2", "b_ssd_bwd"), (1, "mix"): ("b_du_zx", "a0_dq"),
                (0, "mlp"): ("l0_du2", "a0_dkv"), (0, "mix"): (None, None)}


class _Plan:
    def __init__(self, w, m, v, p, dev, chip, core):
        self.w, self.m, self.v, self.p, self.dev, self.chip, self.core = w, m, v, p, dev, chip, core
        self.pending = {}
        self.res = {n: None for n in BIG_KIND}
        self._install(*GATHER_FIRST)(_gather(self._gather_items(*GATHER_FIRST), "gather_first"))
        for host, groups in GATHER_HOSTS.items():
            for i, only in groups:
                self._wait_for(host, _gather_comm(self._gather_items(i, only)), self._install(i, only))

    def _wait_for(self, host, comm, done):
        self.pending.setdefault(host, []).append((comm, done))

    def _names(self, i, only):
        return [(n, l) for n, l in _layer_big(i) if only is None or n in only]

    def _gather_items(self, i, only):
        items = []
        for n, l in self._names(i, only):
            kind, s2 = BIG_KIND[n], self.w[n].shape[1:]
            placed = _place(self.w[n], l, kind, _full2d(kind, s2), self.dev, f"place_l{i}_{n}")
            items.append((placed, kind, _block_size(kind, s2), _full2d(kind, s2), True))
        return items

    def _install(self, i, only):
        def done(fulls):
            for (n, l), t in zip(self._names(i, only), fulls):
                self.p[n][l] = _from_slots(t, 1) if BIG_KIND[n] == "slot" else t
        return done

    def take(self, host):
        return _Comm.merge([c for c, _ in self.pending[host]]) if host in self.pending else None

    def give(self, host, results):
        for comm, done in self.pending.pop(host):
            done(results[:len(comm.out_shapes)])
            results = results[len(comm.out_shapes):]

    def grads(self, i, group, grads):
        names = self._names(i, MLP if group == "mlp" else MIX[i % 3])
        items = []
        for n, _ in names:
            kind, s2 = BIG_KIND[n], self.w[n].shape[1:]
            items.append((_to_slots(grads[n], 1) if kind == "slot" else grads[n], kind, _block_size(kind, s2), s2))
        d2d_host, ici_host = REDUCE_HOSTS[(i, group)]
        tag = f"l{i}_{group}"

        def update(parts):
            def done(recv):
                for (n, l), pt, r in zip(names, parts, recv):
                    self.res[n] = _adamw(pt, r, self.w[n], self.m[n], self.v[n], l, self.res[n], self.chip, f"adamw_l{i}_{n}")
            return done

        def second(sib):
            parts = [_pair_sum(it[0], s, it[1], self.core, f"pair_sum_l{i}_{n}") for (n, _), it, s in zip(names, items, sib)]
            self._send(ici_host, _reduce_ici_comm(parts), update(parts), f"reduce_ici_{tag}")

        self._send(d2d_host, _reduce_d2d_comm(items), second, f"reduce_d2d_{tag}")

    def _send(self, host, comm, done, name):
        if host is None:
            done(_run_comm(comm, name))
        else:
            self._wait_for(host, comm, done)

    def flush(self):
        late = 0
        while self.pending:
            host = next(iter(self.pending))
            for comm, done in self.pending.pop(host):
                done(_run_comm(comm, f"late_{late}_{host}"))
                late += 1


def kernel(x, positions, norm_mix_w, norm_mlp_w, a_w_qkv, a_b_qkv, a_sinks, a_w_o, a_b_o, b_in_w, b_conv_w, b_conv_b, b_dt_bias, b_a_log, b_d, b_norm_w, b_out_w, c_w_qkv, c_w_o, mlp_w_up, mlp_w_down, final_norm_w, loss_target, m_norm_mix_w, m_norm_mlp_w, m_a_w_qkv, m_a_b_qkv, m_a_sinks, m_a_w_o, m_a_b_o, m_b_in_w, m_b_conv_w, m_b_conv_b, m_b_dt_bias, m_b_a_log, m_b_d, m_b_norm_w, m_b_out_w, m_c_w_qkv, m_c_w_o, m_mlp_w_up, m_mlp_w_down, m_final_norm_w, v_norm_mix_w, v_norm_mlp_w, v_a_w_qkv, v_a_b_qkv, v_a_sinks, v_a_w_o, v_a_b_o, v_b_in_w, v_b_conv_w, v_b_conv_b, v_b_dt_bias, v_b_a_log, v_b_d, v_b_norm_w, v_b_out_w, v_c_w_qkv, v_c_w_o, v_mlp_w_up, v_mlp_w_down, v_final_norm_w):
    w = dict(zip(W_NAMES, (norm_mix_w, norm_mlp_w, a_w_qkv, a_b_qkv, a_sinks, a_w_o, a_b_o, b_in_w, b_conv_w, b_conv_b,
                           b_dt_bias, b_a_log, b_d, b_norm_w, b_out_w, c_w_qkv, c_w_o, mlp_w_up, mlp_w_down, final_norm_w)))
    m = dict(zip(W_NAMES, (m_norm_mix_w, m_norm_mlp_w, m_a_w_qkv, m_a_b_qkv, m_a_sinks, m_a_w_o, m_a_b_o, m_b_in_w,
                           m_b_conv_w, m_b_conv_b, m_b_dt_bias, m_b_a_log, m_b_d, m_b_norm_w, m_b_out_w, m_c_w_qkv, m_c_w_o,
                           m_mlp_w_up, m_mlp_w_down, m_final_norm_w)))
    v = dict(zip(W_NAMES, (v_norm_mix_w, v_norm_mlp_w, v_a_w_qkv, v_a_b_qkv, v_a_sinks, v_a_w_o, v_a_b_o, v_b_in_w,
                           v_b_conv_w, v_b_conv_b, v_b_dt_bias, v_b_a_log, v_b_d, v_b_norm_w, v_b_out_w, v_c_w_qkv, v_c_w_o,
                           v_mlp_w_up, v_mlp_w_down, v_final_norm_w)))
    px, py, pc = lax.axis_index("x"), lax.axis_index("y"), lax.axis_index("c")
    me = 4 * px + 2 * py + pc
    dev, chip, core = (t.astype(jnp.int32).reshape(1) for t in (me, 2 * px + py, pc))

    trio = tuple(SMALL_SHARDED)
    got = _gather([(d[n], "slot", None, (N_DEV,) + d[n].shape, False) for n in trio for d in (w, m, v)], "gather_small")
    slots = {n: got[3 * i:3 * i + 3] for i, n in enumerate(trio)}
    p = {n: w[n] for n in SMALL_REPLICATED}
    for n in trio:
        p[n] = _from_slots(slots[n][0], SMALL_SHARDED[n])
    for n in BIG_KIND:
        p[n] = [None] * w[n].shape[0]
    plan = _Plan(w, m, v, p, dev, chip, core)
    loss_part, dx, grads = _local_step(x, positions, p, loss_target, plan)
    loss = lax.psum(loss_part[0, 0], AXES)
    plan.flush()
    out = {n: list(plan.res[n]) for n in BIG_KIND}

    small = SMALL_REPLICATED + trio
    as2d = lambda t: t.reshape(1, -1) if t.ndim == 1 else t
    g_sm = [as2d(grads[n]) for n in SMALL_REPLICATED] + [_to_slots(grads[n].reshape(p[n].shape), SMALL_SHARDED[n]) for n in trio]
    gathered = _gather([(g, "slot", None, (N_DEV,) + g.shape, False) for g in g_sm], "gather_small_grads")
    ws = [as2d(w[n]) for n in SMALL_REPLICATED] + [slots[n][0] for n in trio]
    ms = [as2d(m[n]) for n in SMALL_REPLICATED] + [slots[n][1] for n in trio]
    vs = [as2d(v[n]) for n in SMALL_REPLICATED] + [slots[n][2] for n in trio]
    sm_out = _small_adamw(gathered, ws, ms, vs)
    for i, n in enumerate(small):
        if n in SMALL_SHARDED:
            out[n] = [lax.dynamic_index_in_dim(sm_out[k][i], me, 0, keepdims=False) for k in range(4)]
        else:
            out[n] = [sm_out[k][i].reshape(w[n].shape) for k in range(4)]
    return (loss, dx, *[out[n][0] for n in W_NAMES], *[out[n][1] for n in W_NAMES], *[out[n][2] for n in W_NAMES],
            *[out[n][3] for n in W_NAMES])
```

```python
import math

import jax
import jax.numpy as jnp
from jax import lax
from jax.experimental import pallas as pl
from jax.experimental.pallas import tpu as pltpu

F32 = jnp.float32
BF16 = jnp.bfloat16
SDS = jax.ShapeDtypeStruct

D_MODEL = 1024
DEPTH = 4
BLOCK = 128
ROPE_THETA = 10000.0
NORM_EPS = 1e-5
HEAD_DIM = 64
A_N_HEADS = 16
A_N_KV = 2
A_WINDOW = 128
A_Q_DIM = 1024
A_KV_DIM = 128
SSM_D_INNER = 2048
SSM_N_HEADS = 32
SSM_N_GROUPS = 8
SSM_HG = 4
SSM_D_STATE = 128
SSM_CONV = 4
SSM_CHUNK = 128
SSM_BC_DIM = 1024
SSM_CONV_DIM = 4096
C_PATTERNS = ((128, 1), (512, 4), (2048, 16))
C_HEADS = 16
ADAM_LR, ADAM_B1, ADAM_B2, ADAM_EPS, ADAM_WD, ADAM_STEP = 0.001, 0.9, 0.999, 1e-08, 0.01, 10

N_DEV = 8
AXES = ("x", "y", "c")
LANES = 128
VMEM_LIMIT = 56 * 1024 * 1024
STREAM_VMEM = 16 * 1024 * 1024
NEG = -1e30

NN = (((1,), (0,)), ((), ()))
NT = (((1,), (1,)), ((), ()))
TN = (((0,), (0,)), ((), ()))
HI = lax.Precision.HIGHEST


def _pick(n, cap, mult=LANES):
    best = None
    for t in range(mult, min(n, cap) + 1, mult):
        if n % t == 0:
            best = t
    return best if best is not None else n


def _params(sem):
    return pltpu.CompilerParams(dimension_semantics=sem, vmem_limit_bytes=VMEM_LIMIT)


def _bf(x):
    return x if x.dtype == BF16 else x.astype(BF16)


def _rot_half(y):
    n = y.shape[-1]
    lane = lax.broadcasted_iota(jnp.int32, y.shape, y.ndim - 1)
    return jnp.where((lane % HEAD_DIM) < HEAD_DIM // 2, -pltpu.roll(y, n - 32, y.ndim - 1), pltpu.roll(y, 32, y.ndim - 1))


def _rope(y, cos, sin, sign):
    reps = y.shape[-1] // LANES
    c = jnp.tile(cos, (1, reps)) if reps > 1 else cos
    s = jnp.tile(sin, (1, reps)) if reps > 1 else sin
    return y * c + sign * (_rot_half(y) * s)


MESH = pl.DeviceIdType.MESH
ANY = pl.BlockSpec(memory_space=pl.ANY)


class _Comm:
    def __init__(self, inputs, out_shapes, aliases, sems, phases):
        self.inputs, self.out_shapes, self.aliases, self.sems, self.phases = inputs, out_shapes, aliases, sems, phases

    @staticmethod
    def merge(comms):
        if len(comms) == 1:
            return comms[0]
        ins, outs, aliases, sems, spans = [], [], {}, [], []
        for c in comms:
            aliases.update({len(ins) + i: len(outs) + j for i, j in c.aliases.items()})
            spans.append((len(ins), len(ins) + len(c.inputs), len(outs), len(outs) + len(c.out_shapes), len(sems),
                          len(sems) + len(c.sems)))
            ins, outs, sems = ins + list(c.inputs), outs + list(c.out_shapes), sems + list(c.sems)
        phases = []
        for f in sorted({f for c in comms for f, _ in c.phases}):
            todo = [(fn, sp) for c, sp in zip(comms, spans) for g, fn in c.phases if g == f]

            def run(cins, couts, csems, todo=todo):
                for fn, (i0, i1, o0, o1, s0, s1) in todo:
                    fn(cins[i0:i1], couts[o0:o1], csems[s0:s1])
            phases.append((f, run))
        return _Comm(ins, outs, aliases, sems, phases)


def _pc(body, args, *, out_shape, grid, in_specs, out_specs, name, sem, scratch_shapes=(), comm=None):
    single = not isinstance(out_shape, (tuple, list))
    outs, ospecs = ([out_shape], [out_specs]) if single else (list(out_shape), list(out_specs))
    unpack = (lambda r: r[0]) if single else (lambda r: tuple(r))
    if comm is None:
        res = pl.pallas_call(body, out_shape=outs, grid=grid, in_specs=list(in_specs), out_specs=ospecs,
                             scratch_shapes=list(scratch_shapes), name=name, compiler_params=_params(sem))(*args)
        return unpack(res)
    n_in, n_out, n_scr = len(in_specs), len(outs), len(scratch_shapes)
    c_in, c_out = len(comm.inputs), len(comm.out_shapes)
    total = math.prod(grid)
    steps = [min(total - 1, int(f * total)) for f, _ in comm.phases[:-1]]

    def wrapped(*refs):
        ins, cins = refs[:n_in], refs[n_in:n_in + c_in]
        o = refs[n_in + c_in:n_in + c_in + n_out]
        couts = refs[n_in + c_in + n_out:n_in + c_in + n_out + c_out]
        rest = refs[n_in + c_in + n_out + c_out:]
        scr, csems = rest[:n_scr], rest[n_scr:]
        step = pl.program_id(0)
        for ax in range(1, len(grid)):
            step = step * grid[ax] + pl.program_id(ax)
        for (_, fn), st in zip(comm.phases[:-1], steps):
            @pl.when(step == st)
            def _(fn=fn):
                fn(cins, couts, csems)
        body(*ins, *o, *scr)

        @pl.when(step == total - 1)
        def _():
            comm.phases[-1][1](cins, couts, csems)

    res = pl.pallas_call(
        wrapped, out_shape=outs + list(comm.out_shapes), grid=grid, in_specs=list(in_specs) + [ANY] * c_in,
        out_specs=ospecs + [ANY] * c_out, scratch_shapes=list(scratch_shapes) + list(comm.sems),
        input_output_aliases={n_in + i: n_out + j for i, j in comm.aliases.items()}, name=name,
        compiler_params=_params(("arbitrary",) * len(grid)),
    )(*args, *comm.inputs)
    return unpack(res[:n_out]), list(res[n_out:])


def _hosted(plan, name, run):
    comm = plan.take(name) if plan is not None else None
    if comm is None:
        return run(None)
    res, extra = run(comm)
    plan.give(name, extra)
    return res


MM_VMEM = 40 * 1024 * 1024
HBM_BYTES_PER_US = 2.5e6
STEP_US = 0.35


def _divisors(n, cands):
    return [c for c in cands if c <= n and n % c == 0] or [n]


def _mm_tiles(M, N, K, sa, sb, out_bytes, extra_bytes, full_rows=False):
    best = None
    for tm in _divisors(M, (2048, 1024, 512, 256)):
        for tn in ([N] if full_rows else _divisors(N, (1024, 640, 512, 256, 128))):
            for tk in _divisors(K, (K, K // 2, K // 3, K // 4, 2048, 1024, 640, 512)):
                if tk != K and tk % LANES:
                    continue
                nk = K // tk
                vmem = 2 * tm * tk * sa + 2 * tk * tn * sb + tm * tn * (2 * (out_bytes + extra_bytes) + 8 + (4 if nk > 1 else 0))
                if vmem > MM_VMEM:
                    continue
                a_traffic = M * K * sa * (1 if nk == 1 else N // tn)
                b_traffic = K * N * sb * (1 if (nk == 1 and N == tn) else M // tm)
                steps = (M // tm) * (N // tn) * nk
                cost = (a_traffic + b_traffic + M * N * (out_bytes + extra_bytes)) / HBM_BYTES_PER_US + steps * STEP_US
                cost += (M // tm) * (N // tn) * (nk - 1) * tm * tn * 8 / (4 * HBM_BYTES_PER_US)
                if best is None or cost < best[0]:
                    best = (cost, tm, tn, tk)
    assert best is not None, (M, N, K)
    return best[1:]


def _matmul(a, b, *, ta=False, tb=False, out_dtype=F32, bias=None, resid=None, mul=None, mul_scale=1.0,
            relu2=False, rope=None, rope_cols=0, norm_out=None, norm_bwd=None, name="mm", plan=None):
    M = a.shape[1] if ta else a.shape[0]
    K = a.shape[0] if ta else a.shape[1]
    N = b.shape[0] if tb else b.shape[1]
    assert (b.shape[1] if tb else b.shape[0]) == K
    two_out = relu2 or norm_out is not None
    out_bytes = jnp.dtype(out_dtype).itemsize * (2 if relu2 else 1) + (2 if norm_out is not None else 0)
    extra_bytes = (4 if resid is not None else 0) + (mul.dtype.itemsize if mul is not None else 0) + (8 if norm_bwd else 0)
    rows = norm_out is not None or norm_bwd is not None
    tm, tn, tk = _mm_tiles(M, N, K, a.dtype.itemsize, b.dtype.itemsize, out_bytes, extra_bytes, full_rows=rows)
    nk = K // tk
    dims = (((0 if ta else 1,), (1 if tb else 0,)), ((), ()))

    def body(*refs):
        it = iter(refs)
        a_ref, b_ref = next(it), next(it)
        bias_ref = next(it) if bias is not None else None
        resid_ref = next(it) if resid is not None else None
        mul_ref = next(it) if mul is not None else None
        cos_ref, sin_ref = (next(it), next(it)) if rope is not None else (None, None)
        nw_ref = next(it) if rows else None
        h_ref, dres_ref = (next(it), next(it)) if norm_bwd is not None else (None, None)
        o_ref = next(it)
        o2_ref = next(it) if two_out or norm_bwd is not None else None
        acc_ref = next(it) if nk > 1 else None
        k = pl.program_id(2)
        part = lax.dot_general(_bf(a_ref[...]), _bf(b_ref[...]), dims, preferred_element_type=F32)
        if nk > 1:
            @pl.when(k == 0)
            def _():
                acc_ref[...] = part

            @pl.when(k > 0)
            def _():
                acc_ref[...] += part

        @pl.when(k == nk - 1)
        def _():
            y = acc_ref[...] if nk > 1 else part
            if bias_ref is not None:
                y = y + bias_ref[...]
            if rope is not None and rope_cols % tn == 0 and not (two_out or rows or mul is not None or resid is not None):
                rotated = pl.program_id(1) * tn < rope_cols

                @pl.when(rotated)
                def _():
                    o_ref[...] = _rope(y, cos_ref[...], sin_ref[...], 1.0).astype(o_ref.dtype)

                @pl.when(jnp.logical_not(rotated))
                def _():
                    o_ref[...] = y.astype(o_ref.dtype)
                return
            if rope is not None:
                col = pl.program_id(1) * tn + lax.broadcasted_iota(jnp.int32, y.shape, 1)
                y = jnp.where(col < rope_cols, _rope(y, cos_ref[...], sin_ref[...], 1.0), y)
            if mul_ref is not None:
                y = y * (mul_ref[...].astype(F32) * mul_scale)
            if resid_ref is not None:
                y = y + resid_ref[...]
            if relu2:
                r = jnp.maximum(y, 0.0)
                o_ref[...] = r.astype(o_ref.dtype)
                o2_ref[...] = (r * r).astype(o2_ref.dtype)
            elif norm_bwd is not None:
                x = h_ref[...]
                rstd = lax.rsqrt(jnp.mean(x * x, axis=-1, keepdims=True) + NORM_EPS)
                g = y * nw_ref[...]
                o_ref[...] = dres_ref[...] + rstd * g - x * (rstd * rstd * rstd) * jnp.mean(g * x, axis=-1, keepdims=True)
                dw = jnp.sum(y * x * rstd, axis=0, keepdims=True)
                first = pl.program_id(0) == 0

                @pl.when(first)
                def _():
                    o2_ref[...] = dw

                @pl.when(jnp.logical_not(first))
                def _():
                    o2_ref[...] += dw
            else:
                o_ref[...] = y.astype(o_ref.dtype)
                if norm_out is not None:
                    rstd = lax.rsqrt(jnp.mean(y * y, axis=-1, keepdims=True) + NORM_EPS)
                    o2_ref[...] = (y * rstd * nw_ref[...]).astype(BF16)

    a_spec = pl.BlockSpec((tk, tm), lambda i, j, k: (k, i)) if ta else pl.BlockSpec((tm, tk), lambda i, j, k: (i, k))
    b_spec = pl.BlockSpec((tn, tk), lambda i, j, k: (j, k)) if tb else pl.BlockSpec((tk, tn), lambda i, j, k: (k, j))
    mn_spec = pl.BlockSpec((tm, tn), lambda i, j, k: (i, j))
    in_specs, args = [a_spec, b_spec], [a, b]
    if bias is not None:
        in_specs.append(pl.BlockSpec((1, tn), lambda i, j, k: (0, j)))
        args.append(bias)
    if resid is not None:
        in_specs.append(mn_spec)
        args.append(resid)
    if mul is not None:
        in_specs.append(mn_spec)
        args.append(mul)
    if rope is not None:
        in_specs += [pl.BlockSpec((tm, LANES), lambda i, j, k: (i, 0))] * 2
        args += [rope[0], rope[1]]
    vec_spec = pl.BlockSpec((1, tn), lambda i, j, k: (0, j))
    if rows:
        in_specs.append(vec_spec)
        args.append((norm_out if norm_out is not None else norm_bwd[1]).reshape(1, N))
    if norm_bwd is not None:
        in_specs += [mn_spec, mn_spec]
        args += [norm_bwd[0], norm_bwd[2]]
    out_shape = SDS((M, N), out_dtype)
    out_specs = mn_spec
    if relu2:
        out_shape, out_specs = (out_shape, out_shape), (mn_spec, mn_spec)
    elif norm_out is not None:
        out_shape, out_specs = (out_shape, SDS((M, N), BF16)), (mn_spec, mn_spec)
    elif norm_bwd is not None:
        out_shape, out_specs = (out_shape, SDS((1, N), F32)), (mn_spec, vec_spec)
    sem = ("arbitrary",) * 3 if norm_bwd is not None else ("parallel", "parallel", "arbitrary")
    return _hosted(plan, name, lambda comm: _pc(
        body, args, out_shape=out_shape, grid=(M // tm, N // tn, nk), in_specs=in_specs, out_specs=out_specs,
        scratch_shapes=[pltpu.VMEM((tm, tn), F32)] if nk > 1 else [], name=name, sem=sem, comm=comm))


def _colsum(x, name):
    T, N = x.shape
    tm = _pick(T, 1024, 8)

    def body(x_ref, o_ref):
        s = jnp.sum(x_ref[...].astype(F32), axis=0, keepdims=True)

        @pl.when(pl.program_id(0) == 0)
        def _():
            o_ref[...] = s

        @pl.when(pl.program_id(0) > 0)
        def _():
            o_ref[...] += s

    return pl.pallas_call(
        body, out_shape=SDS((1, N), F32), grid=(T // tm,),
        in_specs=[pl.BlockSpec((tm, N), lambda i: (i, 0))], out_specs=pl.BlockSpec((1, N), lambda i: (0, 0)),
        name=name, compiler_params=_params(("arbitrary",)),
    )(x)


def _rmsnorm_fwd(h, w, name):
    T, D = h.shape
    tm = _pick(T, 512, 8)

    def body(h_ref, w_ref, o_ref):
        x = h_ref[...]
        rstd = lax.rsqrt(jnp.mean(x * x, axis=-1, keepdims=True) + NORM_EPS)
        o_ref[...] = (x * rstd * w_ref[...]).astype(BF16)

    return pl.pallas_call(
        body, out_shape=SDS((T, D), BF16), grid=(T // tm,),
        in_specs=[pl.BlockSpec((tm, D), lambda i: (i, 0)), pl.BlockSpec((1, D), lambda i: (0, 0))],
        out_specs=pl.BlockSpec((tm, D), lambda i: (i, 0)), name=name, compiler_params=_params(("parallel",)),
    )(h, w.reshape(1, D))


def _final_loss(h, target, w):
    T, D = h.shape
    tm = _pick(T, 512, 8)

    def body(h_ref, t_ref, w_ref, dh_ref, dw_ref, loss_ref):
        x = h_ref[...]
        rstd = lax.rsqrt(jnp.mean(x * x, axis=-1, keepdims=True) + NORM_EPS)
        xn = x * rstd
        err = xn * w_ref[...] - t_ref[...]
        part = 0.5 * jnp.sum(jnp.mean(err * err, axis=-1, keepdims=True), axis=0, keepdims=True)
        dy = err * (1.0 / D)
        g = dy * w_ref[...]
        dh_ref[...] = rstd * g - x * (rstd * rstd * rstd) * jnp.mean(g * x, axis=-1, keepdims=True)
        dw = jnp.sum(dy * xn, axis=0, keepdims=True)
        lp = jnp.broadcast_to(part, (1, LANES))

        @pl.when(pl.program_id(0) == 0)
        def _():
            dw_ref[...] = dw
            loss_ref[...] = lp

        @pl.when(pl.program_id(0) > 0)
        def _():
            dw_ref[...] += dw
            loss_ref[...] += lp

    row = pl.BlockSpec((tm, D), lambda i: (i, 0))
    vec = pl.BlockSpec((1, D), lambda i: (0, 0))
    return pl.pallas_call(
        body, out_shape=(SDS((T, D), F32), SDS((1, D), F32), SDS((1, LANES), F32)), grid=(T // tm,),
        in_specs=[row, row, vec], out_specs=(row, vec, pl.BlockSpec((1, LANES), lambda i: (0, 0))),
        name="final_loss", compiler_params=_params(("arbitrary",)),
    )(h, target, w.reshape(1, D))


def _band_mask(i_blk, max_dist, first_ok):
    qi = lax.broadcasted_iota(jnp.int32, (BLOCK, 2 * BLOCK), 0)
    kj = lax.broadcasted_iota(jnp.int32, (BLOCK, 2 * BLOCK), 1)
    dist = qi + BLOCK - kj
    ok = (dist >= 0) & (dist <= max_dist)
    return ok & ((kj >= BLOCK) | first_ok)


def _pair(t, i):
    return t[:, LANES * i:LANES * (i + 1)]


def _low_half(shape):
    return lax.broadcasted_iota(jnp.int32, shape, len(shape) - 1) < HEAD_DIM


def _stack_heads(t):
    lo = _low_half(t.shape)
    z = jnp.zeros_like(t)
    return jnp.concatenate([jnp.where(lo, t, z), jnp.where(lo, z, t)], axis=0)


def _swap_halves(t):
    return jnp.concatenate([t[:, HEAD_DIM:], t[:, :HEAD_DIM]], axis=1)


def _kv_operand(kv, kv_swapped, h0, n_kv, n_heads):
    R = n_heads // n_kv
    if R == 1:
        return _pair(kv, h0 // 2)
    assert kv.shape[1] == LANES and R % 2 == 0, "grouped queries: one 128-lane tile of kv heads, both heads of a pair in one group"
    g = h0 // R
    t, ts = _pair(kv, g // 2), _pair(kv_swapped, g // 2)
    lo = _low_half(t.shape)
    return jnp.where(lo, t, ts) if g % 2 == 0 else jnp.where(lo, ts, t)


def _lane_place(cols):
    m = cols[0].shape[0]
    lane = lax.broadcasted_iota(jnp.int32, (m, LANES), 1)
    out = jnp.zeros((m, LANES), F32)
    for h, c in enumerate(cols):
        out = jnp.where(lane == h, c, out)
    return out


def _attn_specs(B, S, d, C, n_heads, n_kv, q_col, k_col, v_col):
    kvw = n_kv * HEAD_DIM
    qw = n_heads * HEAD_DIM
    cq, ck = (C // qw if d > 1 else 0), (C // kvw if d > 1 else 0)
    q_spec = pl.BlockSpec((1, BLOCK, qw), lambda b, r, i: (b, i, r * cq + q_col // qw))
    kc = pl.BlockSpec((1, BLOCK, kvw), lambda b, r, i: (b, i, r * ck + k_col // kvw))
    kp = pl.BlockSpec((1, BLOCK, kvw), lambda b, r, i: (b, jnp.maximum(i - 1, 0), r * ck + k_col // kvw))
    vc = pl.BlockSpec((1, BLOCK, kvw), lambda b, r, i: (b, i, r * ck + v_col // kvw))
    vp = pl.BlockSpec((1, BLOCK, kvw), lambda b, r, i: (b, jnp.maximum(i - 1, 0), r * ck + v_col // kvw))
    return q_spec, kp, kc, vp, vc


def _attn_fwd(qkv, B, S, d, *, n_heads, n_kv, q_col, k_col, v_col, max_dist, sinks, name, plan=None):
    C = qkv.shape[1]
    Ls = S // d
    nb = Ls // BLOCK
    qw = n_heads * HEAD_DIM
    R = n_heads // n_kv
    qkv3 = qkv.reshape(B, Ls, d * C)
    scale = HEAD_DIM ** -0.5

    def body(*refs):
        if sinks is not None:
            sink_ref, q_ref, kp_ref, kc_ref, vp_ref, vc_ref, o_ref, lse_ref = refs
        else:
            q_ref, kp_ref, kc_ref, vp_ref, vc_ref, o_ref, lse_ref = refs
        i = pl.program_id(2)
        mask1 = _band_mask(i, max_dist, i > 0)
        mask = jnp.concatenate([mask1, mask1], axis=0)
        q = q_ref[0]
        kk = jnp.concatenate([kp_ref[0], kc_ref[0]], axis=0)
        vv = jnp.concatenate([vp_ref[0], vc_ref[0]], axis=0)
        kks, vvs = (_swap_halves(kk), _swap_halves(vv)) if R > 1 else (None, None)
        lo = _low_half((BLOCK, LANES))
        top = lax.broadcasted_iota(jnp.int32, (2 * BLOCK, 1), 0) < BLOCK
        lses, tiles = [], []
        for t in range(n_heads // 2):
            k2 = _kv_operand(kk, kks, 2 * t, n_kv, n_heads)
            v2 = _kv_operand(vv, vvs, 2 * t, n_kv, n_heads)
            s = lax.dot_general(_stack_heads(_pair(q, t)), k2, NT, preferred_element_type=F32) * scale
            s = jnp.where(mask, s, NEG)
            m = jnp.max(s, axis=-1, keepdims=True)
            if sinks is not None:
                sk = jnp.where(top, sink_ref[2 * t], sink_ref[2 * t + 1])
                m = jnp.maximum(m, sk)
            p = jnp.exp(s - m)
            den = jnp.sum(p, axis=-1, keepdims=True)
            if sinks is not None:
                den = den + jnp.exp(sk - m)
            lse2 = m + jnp.log(den)
            o2 = jnp.dot((p / den).astype(BF16), v2, preferred_element_type=F32)
            tiles.append(jnp.where(lo, o2[:BLOCK], o2[BLOCK:]))
            lses += [lse2[:BLOCK], lse2[BLOCK:]]
        o_ref[0] = jnp.concatenate(tiles, axis=-1)
        lse_ref[0] = _lane_place(lses)

    specs = list(_attn_specs(B, S, d, C, n_heads, n_kv, q_col, k_col, v_col))
    args = [qkv3] * 5
    if sinks is not None:
        specs = [pl.BlockSpec(memory_space=pltpu.SMEM)] + specs
        args = [sinks] + args
    o3, lse3 = _hosted(plan, name, lambda comm: _pc(
        body, args, out_shape=(SDS((B, Ls, d * qw), F32), SDS((B, Ls, d * LANES), F32)), grid=(B, d, nb), in_specs=specs,
        out_specs=(pl.BlockSpec((1, BLOCK, qw), lambda b, r, i: (b, i, r)), pl.BlockSpec((1, BLOCK, LANES), lambda b, r, i: (b, i, r))),
        name=name, sem=("parallel", "parallel", "parallel"), comm=comm))
    return o3.reshape(B * S, qw), lse3.reshape(B * S, LANES)


def _attn_dq(qkv, do, lse, delta, cos, sin, B, S, d, *, n_heads, n_kv, q_col, k_col, v_col, max_dist, name, plan=None):
    C = qkv.shape[1]
    Ls = S // d
    nb = Ls // BLOCK
    qw = n_heads * HEAD_DIM
    R = n_heads // n_kv
    scale = HEAD_DIM ** -0.5

    def body(q_ref, kp_ref, kc_ref, vp_ref, vc_ref, do_ref, lse_ref, dl_ref, cos_ref, sin_ref, dq_ref):
        i = pl.program_id(2)
        mask1 = _band_mask(i, max_dist, i > 0)
        mask = jnp.concatenate([mask1, mask1], axis=0)
        q = q_ref[0]
        do_ = do_ref[0]
        kk = jnp.concatenate([kp_ref[0], kc_ref[0]], axis=0)
        vv = jnp.concatenate([vp_ref[0], vc_ref[0]], axis=0)
        kks, vvs = (_swap_halves(kk), _swap_halves(vv)) if R > 1 else (None, None)
        lo = _low_half((BLOCK, LANES))
        lse_t, dl_t = lse_ref[0], dl_ref[0]
        tiles = []
        for t in range(n_heads // 2):
            k2 = _kv_operand(kk, kks, 2 * t, n_kv, n_heads)
            v2 = _kv_operand(vv, vvs, 2 * t, n_kv, n_heads)
            lse2 = jnp.concatenate([lse_t[:, 2 * t:2 * t + 1], lse_t[:, 2 * t + 1:2 * t + 2]], axis=0)
            dl2 = jnp.concatenate([dl_t[:, 2 * t:2 * t + 1], dl_t[:, 2 * t + 1:2 * t + 2]], axis=0)
            s = lax.dot_general(_stack_heads(_pair(q, t)), k2, NT, preferred_element_type=F32) * scale
            p = jnp.where(mask, jnp.exp(s - lse2), 0.0)
            dp = lax.dot_general(_stack_heads(_pair(do_, t)), v2, NT, preferred_element_type=F32)
            ds = p * (dp - dl2)
            dq2 = jnp.dot(ds.astype(BF16), k2, preferred_element_type=F32) * scale
            tiles.append(jnp.where(lo, dq2[:BLOCK], dq2[BLOCK:]))
        dq = jnp.concatenate(tiles, axis=-1)
        dq_ref[0] = _rope(dq, cos_ref[0], sin_ref[0], -1.0).astype(BF16)

    qs, kp, kc, vp, vc = _attn_specs(B, S, d, C, n_heads, n_kv, q_col, k_col, v_col)
    row_q = pl.BlockSpec((1, BLOCK, qw), lambda b, r, i: (b, i, r))
    row_l = pl.BlockSpec((1, BLOCK, LANES), lambda b, r, i: (b, i, r))
    qkv3 = qkv.reshape(B, Ls, d * C)
    v3 = lambda t, w: t.reshape(B, Ls, d * w)
    args = (qkv3, qkv3, qkv3, qkv3, qkv3, v3(do, qw), v3(lse, LANES), v3(delta, LANES), v3(cos, LANES), v3(sin, LANES))
    dq3 = _hosted(plan, name, lambda comm: _pc(
        body, args, out_shape=SDS((B, Ls, d * qw), BF16), grid=(B, d, nb),
        in_specs=[qs, kp, kc, vp, vc, row_q, row_l, row_l, row_l, row_l], out_specs=row_q,
        name=name, sem=("parallel", "parallel", "parallel"), comm=comm))
    return dq3.reshape(B * S, qw)


def _attn_dkv(qkv, do, lse, delta, cos, sin, B, S, d, *, n_heads, n_kv, q_col, k_col, v_col, max_dist, name, plan=None):
    C = qkv.shape[1]
    Ls = S // d
    nb = Ls // BLOCK
    qw = n_heads * HEAD_DIM
    kvw = n_kv * HEAD_DIM
    R = n_heads // n_kv
    scale = HEAD_DIM ** -0.5
    cq, ck = (C // qw if d > 1 else 0), (C // kvw if d > 1 else 0)

    def body(k_ref, v_ref, q0_ref, q1_ref, do0_ref, do1_ref, lse0_ref, lse1_ref, dl0_ref, dl1_ref, cos_ref, sin_ref,
             dk_ref, dv_ref):
        j = pl.program_id(2)
        kj = lax.broadcasted_iota(jnp.int32, (BLOCK, BLOCK), 0)
        qi = lax.broadcasted_iota(jnp.int32, (BLOCK, BLOCK), 1)
        dist0 = qi - kj
        dist1 = qi + BLOCK - kj
        mask0 = (dist0 >= 0) & (dist0 <= max_dist)
        mask1 = (dist1 <= max_dist) & (j + 1 < nb)
        kb, vb = k_ref[0], v_ref[0]
        kbs, vbs = (_swap_halves(kb), _swap_halves(vb)) if R > 1 else (None, None)
        sides = ((q0_ref[0], do0_ref[0], lse0_ref[0].T, dl0_ref[0].T, mask0), (q1_ref[0], do1_ref[0], lse1_ref[0].T, dl1_ref[0].T, mask1))
        n_acc = n_kv if R > 1 else n_kv // 2
        dks = [jnp.zeros((BLOCK, LANES), F32) for _ in range(n_acc)]
        dvs = [jnp.zeros((BLOCK, LANES), F32) for _ in range(n_acc)]
        for t in range(n_heads // 2):
            k2 = _kv_operand(kb, kbs, 2 * t, n_kv, n_heads)
            v2 = _kv_operand(vb, vbs, 2 * t, n_kv, n_heads)
            a = (2 * t) // R if R > 1 else t
            for (q, do_, lse_r, dl_r, mask) in sides:
                q2, do2 = _stack_heads(_pair(q, t)), _stack_heads(_pair(do_, t))
                s = lax.dot_general(k2, q2, NT, preferred_element_type=F32) * scale
                dp = lax.dot_general(v2, do2, NT, preferred_element_type=F32)
                ps, dss = [], []
                for half in (0, 1):
                    h = 2 * t + half
                    sl = slice(BLOCK * half, BLOCK * (half + 1))
                    p = jnp.where(mask, jnp.exp(s[:, sl] - lse_r[h:h + 1, :]), 0.0)
                    ps.append(p)
                    dss.append(p * (dp[:, sl] - dl_r[h:h + 1, :]))
                dvs[a] = dvs[a] + jnp.dot(jnp.concatenate(ps, axis=1).astype(BF16), do2, preferred_element_type=F32)
                dks[a] = dks[a] + jnp.dot(jnp.concatenate(dss, axis=1).astype(BF16), q2, preferred_element_type=F32)
        if R > 1:
            lo = _low_half((BLOCK, LANES))
            fold = lambda x: x + pltpu.roll(x, HEAD_DIM, 1)
            dks = [jnp.where(lo, fold(dks[2 * t]), fold(dks[2 * t + 1])) for t in range(n_kv // 2)]
            dvs = [jnp.where(lo, fold(dvs[2 * t]), fold(dvs[2 * t + 1])) for t in range(n_kv // 2)]
        dk_t = jnp.concatenate(dks, axis=-1) * scale
        dk_ref[0] = _rope(dk_t, cos_ref[0], sin_ref[0], -1.0).astype(BF16)
        dv_ref[0] = jnp.concatenate(dvs, axis=-1).astype(BF16)

    nxt = lambda j: jnp.minimum(j + 1, nb - 1)
    k_spec = pl.BlockSpec((1, BLOCK, kvw), lambda b, r, j: (b, j, r * ck + k_col // kvw))
    v_spec = pl.BlockSpec((1, BLOCK, kvw), lambda b, r, j: (b, j, r * ck + v_col // kvw))
    q0 = pl.BlockSpec((1, BLOCK, qw), lambda b, r, j: (b, j, r * cq + q_col // qw))
    q1 = pl.BlockSpec((1, BLOCK, qw), lambda b, r, j: (b, nxt(j), r * cq + q_col // qw))
    w0 = lambda w: pl.BlockSpec((1, BLOCK, w), lambda b, r, j: (b, j, r))
    w1 = lambda w: pl.BlockSpec((1, BLOCK, w), lambda b, r, j: (b, nxt(j), r))
    qkv3 = qkv.reshape(B, Ls, d * C)
    v3 = lambda t, w: t.reshape(B, Ls, d * w)
    do3, lse3, dl3 = v3(do, qw), v3(lse, LANES), v3(delta, LANES)
    args = (qkv3, qkv3, qkv3, qkv3, do3, do3, lse3, lse3, dl3, dl3, v3(cos, LANES), v3(sin, LANES))
    dk3, dv3 = _hosted(plan, name, lambda comm: _pc(
        body, args, out_shape=(SDS((B, Ls, d * kvw), BF16), SDS((B, Ls, d * kvw), BF16)), grid=(B, d, nb),
        in_specs=[k_spec, v_spec, q0, q1, w0(qw), w1(qw), w0(LANES), w1(LANES), w0(LANES), w1(LANES), w0(LANES), w0(LANES)],
        out_specs=(w0(kvw), w0(kvw)), name=name, sem=("parallel", "parallel", "parallel"), comm=comm))
    return dk3.reshape(B * S, kvw), dv3.reshape(B * S, kvw)


def _head_expand():
    r = lax.broadcasted_iota(jnp.int32, (LANES, C_HEADS * HEAD_DIM), 0)
    c = lax.broadcasted_iota(jnp.int32, (LANES, C_HEADS * HEAD_DIM), 1)
    return jnp.where(c // HEAD_DIM == r, 1.0, 0.0).astype(F32)


def _delta(do, o, lse=None, sinks_row=None, name="delta"):
    T, W = do.shape
    tm = _pick(T, 512, 8)
    with_sink = sinks_row is not None

    def body(*refs):
        if with_sink:
            do_ref, o_ref, lse_ref, sk_ref, dl_ref, dob_ref, ds_ref = refs
        else:
            do_ref, o_ref, dl_ref, dob_ref = refs
        do_ = do_ref[...]
        dl = lax.dot_general(do_ * o_ref[...], _head_expand(), NT, preferred_element_type=F32, precision=HI)
        dl_ref[...] = dl
        dob_ref[...] = do_.astype(BF16)
        if with_sink:
            lane = lax.broadcasted_iota(jnp.int32, dl.shape, 1)
            contrib = jnp.where(lane < A_N_HEADS, -jnp.exp(sk_ref[...] - lse_ref[...]) * dl, 0.0)
            part = jnp.sum(contrib, axis=0, keepdims=True)

            @pl.when(pl.program_id(0) == 0)
            def _():
                ds_ref[...] = part

            @pl.when(pl.program_id(0) > 0)
            def _():
                ds_ref[...] += part

    row_w = pl.BlockSpec((tm, W), lambda i: (i, 0))
    row_l = pl.BlockSpec((tm, LANES), lambda i: (i, 0))
    vec_l = pl.BlockSpec((1, LANES), lambda i: (0, 0))
    if with_sink:
        return pl.pallas_call(
            body, out_shape=(SDS((T, LANES), F32), SDS((T, W), BF16), SDS((1, LANES), F32)), grid=(T // tm,),
            in_specs=[row_w, row_w, row_l, vec_l], out_specs=(row_l, row_w, vec_l), name=name,
            compiler_params=_params(("arbitrary",)),
        )(do, o, lse, sinks_row)
    return pl.pallas_call(
        body, out_shape=(SDS((T, LANES), F32), SDS((T, W), BF16)), grid=(T // tm,),
        in_specs=[row_w, row_w], out_specs=(row_l, row_w), name=name, compiler_params=_params(("parallel",)),
    )(do, o)


def _merge(os_, lses):
    T, W = os_[0].shape
    tm = _pick(T, 512, 8)

    def body(o0, o1, o2, l0, l1, l2, o_ref, lse_ref):
        ls = [l0[...], l1[...], l2[...]]
        m = jnp.maximum(jnp.maximum(ls[0], ls[1]), ls[2])
        ws = [jnp.exp(l - m) for l in ls]
        tot = ws[0] + ws[1] + ws[2]
        lse_ref[...] = m + jnp.log(tot)
        e = _head_expand()
        acc = jnp.zeros((tm, W), F32)
        for w, o in zip(ws, (o0, o1, o2)):
            acc = acc + jnp.dot(w / tot, e, preferred_element_type=F32, precision=HI) * o[...]
        o_ref[...] = acc

    row_w = pl.BlockSpec((tm, W), lambda i: (i, 0))
    row_l = pl.BlockSpec((tm, LANES), lambda i: (i, 0))
    return pl.pallas_call(
        body, out_shape=(SDS((T, W), F32), SDS((T, LANES), F32)), grid=(T // tm,),
        in_specs=[row_w] * 3 + [row_l] * 3, out_specs=(row_w, row_l), name="c_merge", compiler_params=_params(("parallel",)),
    )(*os_, *lses)


CONV_TC = 256


def _conv_pre(x, w, bias):
    row = lax.broadcasted_iota(jnp.int32, x.shape, 0)
    acc = x * w[SSM_CONV - 1:SSM_CONV, :] + bias
    for k in range(1, SSM_CONV):
        acc = acc + jnp.where(row >= k, pltpu.roll(x, k, 0), 0.0) * w[SSM_CONV - 1 - k:SSM_CONV - k, :]
    return acc


def _conv_fwd(zx3, w, bias):
    B, S, _ = zx3.shape
    off = SSM_D_INNER // CONV_TC

    def body(x_ref, w_ref, b_ref, o_ref):
        v = _conv_pre(x_ref[0], w_ref[...], b_ref[...])
        o_ref[0] = v * jax.nn.sigmoid(v)

    return pl.pallas_call(
        body, out_shape=SDS((B, S, SSM_CONV_DIM), F32), grid=(B, SSM_CONV_DIM // CONV_TC),
        in_specs=[pl.BlockSpec((1, S, CONV_TC), lambda b, j: (b, 0, j + off)),
                  pl.BlockSpec((SSM_CONV, CONV_TC), lambda b, j: (0, j)), pl.BlockSpec((1, CONV_TC), lambda b, j: (0, j))],
        out_specs=pl.BlockSpec((1, S, CONV_TC), lambda b, j: (b, 0, j)), name="b_conv_fwd",
        compiler_params=_params(("parallel", "parallel")),
    )(zx3, w, bias)


def _conv_bwd(zx3, dxc, w, bias, col0, name):
    B, S, n = dxc.shape
    tc = _pick(n, CONV_TC)
    off_x = (SSM_D_INNER + col0) // tc
    off_w = col0 // tc

    def body(x_ref, d_ref, w_ref, b_ref, dx_ref, dw_ref, db_ref):
        x = x_ref[0]
        wv = w_ref[...]
        v = _conv_pre(x, wv, b_ref[...])
        sg = jax.nn.sigmoid(v)
        dc = d_ref[0] * (sg * (1.0 + v * (1.0 - sg)))
        row = lax.broadcasted_iota(jnp.int32, x.shape, 0)
        dx = dc * wv[SSM_CONV - 1:SSM_CONV, :]
        dws = [jnp.sum(dc * x, axis=0, keepdims=True)]
        for k in range(1, SSM_CONV):
            dx = dx + jnp.where(row < S - k, pltpu.roll(dc, S - k, 0), 0.0) * wv[SSM_CONV - 1 - k:SSM_CONV - k, :]
            dws.append(jnp.sum(dc * jnp.where(row >= k, pltpu.roll(x, k, 0), 0.0), axis=0, keepdims=True))
        dx_ref[0] = dx.astype(BF16)
        ridx = lax.broadcasted_iota(jnp.int32, (SSM_CONV, tc), 0)
        dw = jnp.zeros((SSM_CONV, tc), F32)
        for k in range(SSM_CONV):
            dw = jnp.where(ridx == SSM_CONV - 1 - k, dws[k], dw)
        db = jnp.sum(dc, axis=0, keepdims=True)

        @pl.when(pl.program_id(1) == 0)
        def _():
            dw_ref[...] = dw
            db_ref[...] = db

        @pl.when(pl.program_id(1) > 0)
        def _():
            dw_ref[...] += dw
            db_ref[...] += db

    return pl.pallas_call(
        body, out_shape=(SDS((B, S, n), BF16), SDS((SSM_CONV, n), F32), SDS((1, n), F32)), grid=(n // tc, B),
        in_specs=[pl.BlockSpec((1, S, tc), lambda j, b: (b, 0, j + off_x)), pl.BlockSpec((1, S, tc), lambda j, b: (b, 0, j)),
                  pl.BlockSpec((SSM_CONV, tc), lambda j, b: (0, j + off_w)), pl.BlockSpec((1, tc), lambda j, b: (0, j + off_w))],
        out_specs=(pl.BlockSpec((1, S, tc), lambda j, b: (b, 0, j)), pl.BlockSpec((SSM_CONV, tc), lambda j, b: (0, j)),
                   pl.BlockSpec((1, tc), lambda j, b: (0, j))),
        name=name, compiler_params=_params(("parallel", "arbitrary")),
    )(zx3, dxc, w, bias)


def _ssd_common(x, Bm, Cm, dtc_raw, dtr_raw, pr, pc):
    Q = SSM_CHUNK
    zc = dtc_raw + pr[0:1, :]
    dt_c = jax.nn.softplus(zc)
    dt_r = jax.nn.softplus(dtr_raw + pc[:, 0:1])
    A_r = -jnp.exp(pr[1:2, :])
    A_c = -jnp.exp(pc[:, 1:2])
    row = lax.broadcasted_iota(jnp.int32, (Q, Q), 0)
    col = lax.broadcasted_iota(jnp.int32, (Q, Q), 1)
    tril = jnp.where(row >= col, 1.0, 0.0).astype(F32)
    cs_c = jnp.dot(tril, dt_c * A_r, preferred_element_type=F32, precision=HI)
    cs_r = lax.dot_general(dt_r * A_c, tril, NT, preferred_element_type=F32, precision=HI)
    return zc, dt_c, A_r, cs_c, cs_r, row, col, tril


def _ssd_fwd(xc3, dtc, dtr, prow, pcol, plan=None):
    B, S, _ = xc3.shape
    Q, G, HG, P, N = SSM_CHUNK, SSM_N_GROUPS, SSM_HG, HEAD_DIM, SSM_D_STATE
    nc = S // Q
    xw = HG * P

    def body(x_ref, b_ref, c_ref, dtc_ref, dtr_ref, pr_ref, pc_ref, y_ref, st_ref, state):
        c = pl.program_id(1)

        @pl.when(c == 0)
        def _():
            state[...] = jnp.zeros_like(state)

        pr = pr_ref[0]
        for bb in range(B):
            x, Bm, Cm = x_ref[bb], b_ref[bb], c_ref[bb]
            _, dt_c, _, cs_c, cs_r, row, col, _ = _ssd_common(x, Bm, Cm, dtc_ref[bb, 0], dtr_ref[bb, 0], pr, pc_ref[0])
            Bb, Cb = Bm.astype(BF16), Cm.astype(BF16)
            CB = lax.dot_general(Cb, Bb, NT, preferred_element_type=F32)
            ys = []
            for hg in range(HG):
                xh = x[:, P * hg:P * (hg + 1)]
                xt = xh * dt_c[:, hg:hg + 1]
                csc, csr = cs_c[:, hg:hg + 1], cs_r[hg:hg + 1, :]
                L = jnp.where(row >= col, jnp.exp(jnp.minimum(csc - csr, 0.0)), 0.0)
                ydiag = jnp.dot((CB * L).astype(BF16), xt.astype(BF16), preferred_element_type=F32)
                Sh = state[bb, hg]
                yoff = lax.dot_general(Cb, Sh.astype(BF16), NT, preferred_element_type=F32) * jnp.exp(csc)
                ys.append(ydiag + yoff + pr[2:3, hg:hg + 1] * xh)
                st_ref[bb, 0, 0, P * hg:P * (hg + 1), :] = Sh
                csq = csc[Q - 1:Q, :]
                upd = lax.dot_general((xt * jnp.exp(csq - csc)).astype(BF16), Bb, TN, preferred_element_type=F32)
                state[bb, hg] = Sh * jnp.exp(csq) + upd
            y_ref[bb] = jnp.concatenate([jnp.concatenate(ys[0:2], axis=-1), jnp.concatenate(ys[2:4], axis=-1)], axis=-1)

    bo, co = SSM_D_INNER // N, (SSM_D_INNER + SSM_BC_DIM) // N
    return _hosted(plan, "b_ssd_fwd", lambda comm: _pc(
        body, (xc3, xc3, xc3, dtc, dtr, prow, pcol),
        out_shape=(SDS((B, S, SSM_D_INNER), F32), SDS((B, G, nc, xw, N), F32)), grid=(G, nc),
        in_specs=[pl.BlockSpec((B, Q, xw), lambda g, c: (0, c, g)), pl.BlockSpec((B, Q, N), lambda g, c: (0, c, bo + g)),
                  pl.BlockSpec((B, Q, N), lambda g, c: (0, c, co + g)), pl.BlockSpec((B, 1, Q, HG), lambda g, c: (0, g, c, 0)),
                  pl.BlockSpec((B, 1, HG, Q), lambda g, c: (0, g, 0, c)), pl.BlockSpec((1, 3, HG), lambda g, c: (g, 0, 0)),
                  pl.BlockSpec((1, HG, 3), lambda g, c: (g, 0, 0))],
        out_specs=(pl.BlockSpec((B, Q, xw), lambda g, c: (0, c, g)), pl.BlockSpec((B, 1, 1, xw, N), lambda g, c: (0, g, c, 0, 0))),
        scratch_shapes=[pltpu.VMEM((B, HG, P, N), F32)], name="b_ssd_fwd", sem=("parallel", "arbitrary"), comm=comm))


def _ssd_bwd(xc3, dtc, dtr, prow, pcol, states, dy3, plan=None):
    B, S, _ = xc3.shape
    Q, G, HG, P, N = SSM_CHUNK, SSM_N_GROUPS, SSM_HG, HEAD_DIM, SSM_D_STATE
    nc = S // Q
    xw = HG * P

    def body(x_ref, b_ref, c_ref, dtc_ref, dtr_ref, pr_ref, pc_ref, st_ref, dy_ref,
             dx_ref, db_ref, dc_ref, ddt_ref, dpar_ref, dstate):
        ci = pl.program_id(1)

        @pl.when(ci == 0)
        def _():
            dstate[...] = jnp.zeros_like(dstate)

        pr = pr_ref[0]
        dpar = one_sequence(0, pr, x_ref, b_ref, c_ref, dtc_ref, dtr_ref, pc_ref, st_ref, dy_ref, dx_ref, db_ref, dc_ref,
                            ddt_ref, dstate)
        for bb in range(1, B):
            dpar = dpar + one_sequence(bb, pr, x_ref, b_ref, c_ref, dtc_ref, dtr_ref, pc_ref, st_ref, dy_ref, dx_ref, db_ref,
                                       dc_ref, ddt_ref, dstate)
        first = ci == 0

        @pl.when(first)
        def _():
            dpar_ref[0] = dpar

        @pl.when(jnp.logical_not(first))
        def _():
            dpar_ref[0] += dpar

    def one_sequence(bb, pr, x_ref, b_ref, c_ref, dtc_ref, dtr_ref, pc_ref, st_ref, dy_ref, dx_ref, db_ref, dc_ref, ddt_ref,
                     dstate):
        x, Bm, Cm, dy = x_ref[bb], b_ref[bb], c_ref[bb], dy_ref[bb]
        zc, dt_c, A_r, cs_c, cs_r, row, col, tril = _ssd_common(x, Bm, Cm, dtc_ref[bb, 0], dtr_ref[bb, 0], pr, pc_ref[0])
        Bb, Cb = Bm.astype(BF16), Cm.astype(BF16)
        CB = lax.dot_general(Cb, Bb, NT, preferred_element_type=F32)
        CBt = lax.dot_general(Bb, Cb, NT, preferred_element_type=F32)
        lane4 = lax.broadcasted_iota(jnp.int32, (Q, HG), 1)
        lane4r = lax.broadcasted_iota(jnp.int32, (1, HG), 1)
        rowq = lax.broadcasted_iota(jnp.int32, (Q, 1), 0)
        dB = jnp.zeros((Q, N), F32)
        dC = jnp.zeros((Q, N), F32)
        dcs4 = jnp.zeros((Q, HG), F32)
        dtx4 = jnp.zeros((Q, HG), F32)
        dD4 = jnp.zeros((1, HG), F32)
        dxts, xhs, dyhs = [], [], []
        for hg in range(HG):
            xh = x[:, P * hg:P * (hg + 1)]
            dyh = dy[:, P * hg:P * (hg + 1)]
            xt = xh * dt_c[:, hg:hg + 1]
            xtb, dyb = xt.astype(BF16), dyh.astype(BF16)
            csc, csr = cs_c[:, hg:hg + 1], cs_r[hg:hg + 1, :]
            L = jnp.where(row >= col, jnp.exp(jnp.minimum(csc - csr, 0.0)), 0.0)
            Lt = jnp.where(col >= row, jnp.exp(jnp.minimum(csr - csc, 0.0)), 0.0)
            M, Mt = CB * L, CBt * Lt
            Sh = st_ref[bb, 0, 0, P * hg:P * (hg + 1), :]
            dSh = dstate[bb, hg]
            Shb, dShb = Sh.astype(BF16), dSh.astype(BF16)
            ecs = jnp.exp(csc)
            csq = csc[Q - 1:Q, :]
            dec = jnp.exp(csq - csc)
            dxt = jnp.dot(Mt.astype(BF16), dyb, preferred_element_type=F32)
            dxt = dxt + lax.dot_general(Bb, dShb, NT, preferred_element_type=F32) * dec
            Gm = lax.dot_general(dyb, xtb, NT, preferred_element_type=F32)
            Gt = lax.dot_general(xtb, dyb, NT, preferred_element_type=F32)
            dC = dC + jnp.dot((Gm * L).astype(BF16), Bb, preferred_element_type=F32)
            dB = dB + jnp.dot((Gt * Lt).astype(BF16), Cb, preferred_element_type=F32)
            dC = dC + jnp.dot(dyb, Shb, preferred_element_type=F32) * ecs
            dBst = jnp.dot(xtb, dShb, preferred_element_type=F32) * dec
            dB = dB + dBst
            dcs = jnp.sum(Gm * M, axis=1, keepdims=True) - jnp.sum(Gt * Mt, axis=1, keepdims=True)
            yoff = lax.dot_general(Cb, Shb, NT, preferred_element_type=F32) * ecs
            dcs = dcs + jnp.sum(yoff * dyh, axis=1, keepdims=True)
            r = jnp.sum(dBst * Bm, axis=1, keepdims=True)
            dcs = dcs - r
            extra = jnp.sum(r, axis=0, keepdims=True) + jnp.exp(csq) * jnp.sum(
                jnp.sum(dSh * Sh, axis=1, keepdims=True), axis=0, keepdims=True)
            dcs = dcs + jnp.where(rowq == Q - 1, extra, 0.0)
            dcs4 = jnp.where(lane4 == hg, dcs, dcs4)
            dtx4 = jnp.where(lane4 == hg, jnp.sum(dxt * xh, axis=1, keepdims=True), dtx4)
            dD4 = jnp.where(lane4r == hg, jnp.sum(jnp.sum(dyh * xh, axis=1, keepdims=True), axis=0, keepdims=True), dD4)
            dstate[bb, hg] = dSh * jnp.exp(csq) + lax.dot_general((dyh * ecs).astype(BF16), Cb, TN, preferred_element_type=F32)
            dxts.append(dxt)
            xhs.append(xh)
            dyhs.append(dyh)
        da4 = lax.dot_general(tril, dcs4, TN, preferred_element_type=F32, precision=HI)
        ddt4 = da4 * A_r + dtx4
        ddtraw = ddt4 * jax.nn.sigmoid(zc)
        ddt_ref[bb, 0] = ddtraw
        dxs = [dxts[hg] * dt_c[:, hg:hg + 1] + pr[2:3, hg:hg + 1] * dyhs[hg] for hg in range(HG)]
        dx_ref[bb] = jnp.concatenate([jnp.concatenate(dxs[0:2], axis=-1), jnp.concatenate(dxs[2:4], axis=-1)], axis=-1)
        db_ref[bb] = dB
        dc_ref[bb] = dC
        d_bias = jnp.sum(ddtraw, axis=0, keepdims=True)
        d_alog = jnp.sum(da4 * dt_c, axis=0, keepdims=True) * A_r
        r3 = lax.broadcasted_iota(jnp.int32, (3, HG), 0)
        return jnp.where(r3 == 0, d_bias, jnp.where(r3 == 1, d_alog, dD4))

    rc = lambda c: nc - 1 - c
    bo, co = SSM_D_INNER // N, (SSM_D_INNER + SSM_BC_DIM) // N
    return _hosted(plan, "b_ssd_bwd", lambda comm: _pc(
        body, (xc3, xc3, xc3, dtc, dtr, prow, pcol, states, dy3),
        out_shape=(SDS((B, S, SSM_D_INNER), F32), SDS((B, S, SSM_BC_DIM), F32), SDS((B, S, SSM_BC_DIM), F32),
                   SDS((B, G, S, HG), F32), SDS((G, 3, HG), F32)),
        grid=(G, nc),
        in_specs=[pl.BlockSpec((B, Q, xw), lambda g, c: (0, rc(c), g)), pl.BlockSpec((B, Q, N), lambda g, c: (0, rc(c), bo + g)),
                  pl.BlockSpec((B, Q, N), lambda g, c: (0, rc(c), co + g)), pl.BlockSpec((B, 1, Q, HG), lambda g, c: (0, g, rc(c), 0)),
                  pl.BlockSpec((B, 1, HG, Q), lambda g, c: (0, g, 0, rc(c))), pl.BlockSpec((1, 3, HG), lambda g, c: (g, 0, 0)),
                  pl.BlockSpec((1, HG, 3), lambda g, c: (g, 0, 0)),
                  pl.BlockSpec((B, 1, 1, xw, N), lambda g, c: (0, g, rc(c), 0, 0)), pl.BlockSpec((B, Q, xw), lambda g, c: (0, rc(c), g))],
        out_specs=(pl.BlockSpec((B, Q, xw), lambda g, c: (0, rc(c), g)), pl.BlockSpec((B, Q, N), lambda g, c: (0, rc(c), g)),
                   pl.BlockSpec((B, Q, N), lambda g, c: (0, rc(c), g)), pl.BlockSpec((B, 1, Q, HG), lambda g, c: (0, g, rc(c), 0)),
                   pl.BlockSpec((1, 3, HG), lambda g, c: (g, 0, 0))),
        scratch_shapes=[pltpu.VMEM((B, HG, P, N), F32)], name="b_ssd_bwd", sem=("parallel", "arbitrary"), comm=comm))


GN_W = SSM_D_INNER // SSM_N_GROUPS


def _gate_fwd(y, zx, nw):
    T = y.shape[0]
    tm = _pick(T, 256, 8)

    def body(y_ref, z_ref, w_ref, o_ref):
        z = z_ref[...]
        gt = y_ref[...] * (z * jax.nn.sigmoid(z))
        outs = []
        for k in range(SSM_N_GROUPS):
            gk = gt[:, GN_W * k:GN_W * (k + 1)]
            outs.append(gk * lax.rsqrt(jnp.mean(gk * gk, axis=-1, keepdims=True) + NORM_EPS))
        o_ref[...] = (jnp.concatenate(outs, axis=-1) * w_ref[...]).astype(BF16)

    row = pl.BlockSpec((tm, SSM_D_INNER), lambda i: (i, 0))
    return pl.pallas_call(
        body, out_shape=SDS((T, SSM_D_INNER), BF16), grid=(T // tm,),
        in_specs=[row, row, pl.BlockSpec((1, SSM_D_INNER), lambda i: (0, 0))], out_specs=row, name="b_gate_fwd",
        compiler_params=_params(("parallel",)),
    )(y, zx, nw)


def _gate_bwd(dgn, y, zx, nw):
    T = y.shape[0]
    tm = _pick(T, 256, 8)

    def body(d_ref, y_ref, z_ref, w_ref, dy_ref, dz_ref, dw_ref):
        z, yv, w = z_ref[...], y_ref[...], w_ref[...]
        sg = jax.nn.sigmoid(z)
        sz = z * sg
        gt = yv * sz
        gw = d_ref[...] * w
        dgts, dws = [], []
        for k in range(SSM_N_GROUPS):
            sl = slice(GN_W * k, GN_W * (k + 1))
            gk, gwk = gt[:, sl], gw[:, sl]
            rstd = lax.rsqrt(jnp.mean(gk * gk, axis=-1, keepdims=True) + NORM_EPS)
            dgts.append(rstd * gwk - gk * (rstd * rstd * rstd) * jnp.mean(gwk * gk, axis=-1, keepdims=True))
            dws.append(jnp.sum(d_ref[:, sl] * gk * rstd, axis=0, keepdims=True))
        dgt = jnp.concatenate(dgts, axis=-1)
        dy_ref[...] = dgt * sz
        dz_ref[...] = (dgt * yv * (sg * (1.0 + z * (1.0 - sg)))).astype(BF16)
        dw = jnp.concatenate(dws, axis=-1)

        @pl.when(pl.program_id(0) == 0)
        def _():
            dw_ref[...] = dw

        @pl.when(pl.program_id(0) > 0)
        def _():
            dw_ref[...] += dw

    row = pl.BlockSpec((tm, SSM_D_INNER), lambda i: (i, 0))
    vec = pl.BlockSpec((1, SSM_D_INNER), lambda i: (0, 0))
    return pl.pallas_call(
        body, out_shape=(SDS((T, SSM_D_INNER), F32), SDS((T, SSM_D_INNER), BF16), SDS((1, SSM_D_INNER), F32)), grid=(T // tm,),
        in_specs=[row, row, row, vec], out_specs=(row, row, vec), name="b_gate_bwd", compiler_params=_params(("arbitrary",)),
    )(dgn, y, zx, nw)


N_CHIPS = 4


def _dev_block(ref, kind, j, size):
    if kind == "slot":
        return ref.at[j]
    start = pl.multiple_of(j * size, size)
    nd = len(ref.shape)
    if kind == "col":
        return ref.at[(slice(None),) * (nd - 1) + (pl.ds(start, size),)]
    return ref.at[(slice(None),) * (nd - 2) + (pl.ds(start, size), slice(None))]


def _dma_sems(n, k):
    return [pltpu.SemaphoreType.DMA((n, k)), pltpu.SemaphoreType.DMA((n, k)), pltpu.SemaphoreType.DMA((n, k))]


def _place(shard, layer, kind, full_shape, dev, name):
    k, n = shard.shape[1:]
    tr = _pick(k, 512, 16)
    nb = k // tr

    def body(dev_ref, s_ref, o_ref):
        if kind == "slot":
            o_ref[0] = s_ref[0].astype(BF16)
        else:
            o_ref[...] = s_ref[0].astype(BF16)

    out_spec = {"slot": pl.BlockSpec((1, tr, n), lambda i, d: (d[0], i, 0)),
                "row": pl.BlockSpec((tr, n), lambda i, d: (d[0] * nb + i, 0)),
                "col": pl.BlockSpec((tr, n), lambda i, d: (i, d[0]))}[kind]
    return pl.pallas_call(
        body, out_shape=SDS(full_shape, BF16),
        grid_spec=pltpu.PrefetchScalarGridSpec(
            num_scalar_prefetch=1, grid=(nb,), in_specs=[pl.BlockSpec((1, tr, n), lambda i, d: (layer, i, 0))], out_specs=out_spec),
        name=name, compiler_params=_params(("arbitrary",)),
    )(dev, shard)


def _run_comm(comm, name):
    c_in = len(comm.inputs)

    def body(*refs):
        cins, couts, sems = refs[:c_in], refs[c_in:c_in + len(comm.out_shapes)], refs[c_in + len(comm.out_shapes):]
        for _, fn in comm.phases:
            fn(cins, couts, sems)

    return pl.pallas_call(
        body, out_shape=list(comm.out_shapes), in_specs=[ANY] * c_in, out_specs=[ANY] * len(comm.out_shapes),
        input_output_aliases=dict(comm.aliases), scratch_shapes=list(comm.sems), name=name,
    )(*comm.inputs)


def _gather_comm(items, mid=0.7):
    n = len(items)

    def tools(srcs, dsts, sems):
        send_sems, recv_sems, local_sems = sems
        px, py, pc = lax.axis_index("x"), lax.axis_index("y"), lax.axis_index("c")
        me, sibling = (px, py, pc), (px, py, 1 - pc)
        chips = [(1 - px, py), (px, 1 - py), (1 - px, 1 - py)]

        def blk(a, dev):
            return _dev_block(dsts[a], items[a][1], 4 * dev[0] + 2 * dev[1] + dev[2], items[a][2])

        def copy(a, k, block, to, src=None):
            return pltpu.make_async_remote_copy(
                src_ref=blk(a, block) if src is None else src, dst_ref=blk(a, block),
                send_sem=send_sems.at[a, k], recv_sem=recv_sems.at[a, k], device_id=to, device_id_type=MESH)

        def mine():
            return [pltpu.make_async_copy(srcs[a], blk(a, me), local_sems.at[a, 0]) for a in range(n) if not items[a][4]]

        def first():
            out = []
            for a in range(n):
                src = blk(a, me) if items[a][4] else srcs[a]
                out.append(copy(a, 0, me, sibling, src=src))
                out += [copy(a, 1 + j, me, (*chip, pc), src=src) for j, chip in enumerate(chips)]
            return out

        def passed():
            return [copy(a, 4 + j, (*chip, pc), sibling) for j, chip in enumerate(chips) for a in range(n)]

        return me, sibling, chips, pc, copy, mine, first, passed

    def start(srcs, dsts, sems):
        *_, mine, first, _ = tools(srcs, dsts, sems)
        for cp in mine() + first():
            cp.start()

    def forward(srcs, dsts, sems):
        me, _, chips, pc, copy, _, _, passed = tools(srcs, dsts, sems)
        fwd = passed()
        for j, chip in enumerate(chips):
            for a in range(n):
                copy(a, 1 + j, (*chip, pc), me).wait_recv()
                fwd[j * n + a].start()

    def finish(srcs, dsts, sems):
        me, sibling, chips, pc, copy, mine, first, passed = tools(srcs, dsts, sems)
        for a in range(n):
            copy(a, 0, sibling, me).wait_recv()
            for j, chip in enumerate(chips):
                copy(a, 4 + j, (*chip, 1 - pc), me).wait_recv()
        for cp in first() + passed():
            cp.wait_send()
        for cp in mine():
            cp.wait()

    return _Comm([it[0] for it in items], [SDS(it[3], it[0].dtype) for it in items],
                 {a: a for a in range(n) if items[a][4]}, _dma_sems(n, 7), [(0.0, start), (mid, forward), (1.0, finish)])


def _gather(items, name):
    return _run_comm(_gather_comm(items), name)


def _reduce_d2d_comm(items):
    n = len(items)

    def copies(gs, gots, sems):
        send_sems, recv_sems, _ = sems
        px, py, pc = lax.axis_index("x"), lax.axis_index("y"), lax.axis_index("c")
        out = []
        for a in range(n):
            _, kind, size, _ = items[a]
            for q in range(N_CHIPS):
                out.append(pltpu.make_async_remote_copy(
                    src_ref=_dev_block(gs[a], kind, 2 * q + 1 - pc, size), dst_ref=gots[a].at[q], send_sem=send_sems.at[a, q],
                    recv_sem=recv_sems.at[a, q], device_id=(px, py, 1 - pc), device_id_type=MESH))
        return out

    def start(gs, gots, sems):
        for cp in copies(gs, gots, sems):
            cp.start()

    def finish(gs, gots, sems):
        for cp in copies(gs, gots, sems):
            cp.wait()

    return _Comm([it[0] for it in items], [SDS((N_CHIPS,) + tuple(it[3]), F32) for it in items], {},
                 _dma_sems(n, N_CHIPS), [(0.0, start), (1.0, finish)])


def _pair_sum(g, got, kind, core, name):
    _, k, n = got.shape
    tr = _pick(k, max(16, STREAM_VMEM // (2 * n * 10)), 16)
    nb = k // tr

    def body(c_ref, g_ref, s_ref, o_ref):
        mine = g_ref[0] if kind == "slot" else g_ref[...]
        o_ref[0] = (mine + s_ref[0]).astype(BF16)

    g_spec = {"slot": pl.BlockSpec((1, tr, n), lambda q, i, c: (2 * q + c[0], i, 0)),
              "row": pl.BlockSpec((tr, n), lambda q, i, c: ((2 * q + c[0]) * nb + i, 0)),
              "col": pl.BlockSpec((tr, n), lambda q, i, c: (i, 2 * q + c[0]))}[kind]
    part = pl.BlockSpec((1, tr, n), lambda q, i, c: (q, i, 0))
    return pl.pallas_call(
        body, out_shape=SDS((N_CHIPS, k, n), BF16),
        grid_spec=pltpu.PrefetchScalarGridSpec(num_scalar_prefetch=1, grid=(N_CHIPS, nb), in_specs=[g_spec, part], out_specs=part),
        name=name, compiler_params=_params(("arbitrary", "arbitrary")),
    )(core, g, got)


def _reduce_ici_comm(parts):
    n = len(parts)

    def copies(ps, rs, sems, arriving):
        send_sems, recv_sems, _ = sems
        px, py, pc = lax.axis_index("x"), lax.axis_index("y"), lax.axis_index("c")
        my_chip = 2 * px + py
        out = []
        for a in range(n):
            for k in range(1, N_CHIPS):
                qx, qy = px ^ (k >> 1), py ^ (k & 1)
                q = 2 * qx + qy
                out.append(pltpu.make_async_remote_copy(
                    src_ref=ps[a].at[q], dst_ref=rs[a].at[q] if arriving else rs[a].at[my_chip], send_sem=send_sems.at[a, k - 1],
                    recv_sem=recv_sems.at[a, k - 1], device_id=(qx, qy, pc), device_id_type=MESH))
        return out

    def start(ps, rs, sems):
        for cp in copies(ps, rs, sems, False):
            cp.start()

    def finish(ps, rs, sems):
        for cp in copies(ps, rs, sems, True):
            cp.wait_recv()
        for cp in copies(ps, rs, sems, False):
            cp.wait_send()

    return _Comm(list(parts), [SDS(p.shape, p.dtype) for p in parts], {}, _dma_sems(n, N_CHIPS - 1),
                 [(0.0, start), (1.0, finish)])


def _adam_update(g, w, m, v):
    c1 = 1.0 - ADAM_B1 ** ADAM_STEP
    c2 = 1.0 - ADAM_B2 ** ADAM_STEP
    nm = ADAM_B1 * m + (1.0 - ADAM_B1) * g
    nv = ADAM_B2 * v + (1.0 - ADAM_B2) * (g * g)
    delta = -ADAM_LR * ((nm / c1) / (jnp.sqrt(nv / c2) + ADAM_EPS) + ADAM_WD * w)
    return delta, nm, nv


def _adamw(parts, recv, w, m, v, layer, prev, chip, name):
    _, R, C = w.shape
    row_bytes = 2 * C * (N_CHIPS * 2 + 7 * 4)
    tr = _pick(R, max(16, STREAM_VMEM // row_bytes), 16)
    n_prev = 0 if prev is None else 4

    def body(ch_ref, own_ref, r1_ref, r2_ref, r3_ref, w_ref, m_ref, v_ref, *rest):
        g_ref, d_ref, nm_ref, nv_ref = rest[n_prev:]
        g = own_ref[0].astype(F32)
        for r_ref in (r1_ref, r2_ref, r3_ref):
            g = g + r_ref[0].astype(F32)
        g_ref[0] = g
        d_ref[0], nm_ref[0], nv_ref[0] = _adam_update(g, w_ref[0], m_ref[0], v_ref[0])

    lay = pl.BlockSpec((1, tr, C), lambda i, ch: (layer, i, 0))
    other = lambda k: pl.BlockSpec((1, tr, C), lambda i, ch: (ch[0] ^ k, i, 0))
    out = SDS(w.shape, F32)
    return pl.pallas_call(
        body, out_shape=(out, out, out, out),
        grid_spec=pltpu.PrefetchScalarGridSpec(
            num_scalar_prefetch=1, grid=(R // tr,),
            in_specs=[pl.BlockSpec((1, tr, C), lambda i, ch: (ch[0], i, 0)), other(2), other(1), other(3), lay, lay, lay]
            + [ANY] * n_prev,
            out_specs=(lay, lay, lay, lay)),
        input_output_aliases={8 + k: k for k in range(n_prev)},
        name=name, compiler_params=_params(("arbitrary",)),
    )(chip, parts, recv, recv, recv, w, m, v, *(prev or ()))


def _small_adamw(gathered, ws, ms, vs):
    n = len(ws)

    def body(*refs):
        g_in, w_in, m_in, v_in = refs[:n], refs[n:2 * n], refs[2 * n:3 * n], refs[3 * n:4 * n]
        outs = refs[4 * n:]
        for i in range(n):
            g = g_in[i][0]
            for dev in range(1, N_DEV):
                g = g + g_in[i][dev]
            d, nm, nv = _adam_update(g, w_in[i][...], m_in[i][...], v_in[i][...])
            outs[i][...] = g
            outs[n + i][...] = d
            outs[2 * n + i][...] = nm
            outs[3 * n + i][...] = nv

    shapes = [SDS(w.shape, F32) for w in ws]
    outs = pl.pallas_call(body, out_shape=shapes * 4, name="small_adamw")(*gathered, *ws, *ms, *vs)
    return outs[:n], outs[n:2 * n], outs[2 * n:3 * n], outs[3 * n:]


W_NAMES = ("norm_mix_w", "norm_mlp_w", "a_w_qkv", "a_b_qkv", "a_sinks", "a_w_o", "a_b_o", "b_in_w", "b_conv_w", "b_conv_b",
           "b_dt_bias", "b_a_log", "b_d", "b_norm_w", "b_out_w", "c_w_qkv", "c_w_o", "mlp_w_up", "mlp_w_down", "final_norm_w")
BIG_KIND = {"a_w_qkv": "slot", "a_w_o": "row", "b_in_w": "slot", "b_out_w": "row", "c_w_qkv": "col", "c_w_o": "row",
            "mlp_w_up": "col", "mlp_w_down": "row"}
SMALL_SHARDED = {"a_b_qkv": 1, "a_b_o": 1, "b_conv_w": 2}
SMALL_REPLICATED = ("norm_mix_w", "norm_mlp_w", "a_sinks", "b_conv_b", "b_dt_bias", "b_a_log", "b_d", "b_norm_w", "final_norm_w")


def _layer_big(i):
    kind, j = i % 3, i // 3
    mix = {0: [("a_w_qkv", j), ("a_w_o", j)], 1: [("b_in_w", 0), ("b_out_w", 0)], 2: [("c_w_qkv", 0), ("c_w_o", 0)]}[kind]
    return mix + [("mlp_w_up", i), ("mlp_w_down", i)]


def _block_size(kind, shard2d):
    return {"slot": None, "row": shard2d[0], "col": shard2d[1]}[kind]


def _full2d(kind, shard2d):
    k, n = shard2d
    return {"slot": (N_DEV, k, n), "row": (N_DEV * k, n), "col": (k, N_DEV * n)}[kind]


def _from_slots(t, ax):
    s = t.shape[1:]
    return jnp.moveaxis(t, 0, ax).reshape(s[:ax] + (N_DEV * s[ax],) + s[ax + 1:])


def _to_slots(g, ax):
    s = g.shape
    return jnp.moveaxis(g.reshape(s[:ax] + (N_DEV, s[ax] // N_DEV) + s[ax + 1:]), ax, 0)


def _rope_tables(positions):
    half = HEAD_DIM // 2
    inv = ROPE_THETA ** (-(jnp.arange(LANES, dtype=jnp.int32) % half).astype(F32) / half)
    ang = positions.astype(F32).reshape(-1, 1) * inv
    return jnp.cos(ang), jnp.sin(ang)


def _swa_fwd(u, h, p, j, B, S, cos, sin, tag, nw, plan=None):
    qkv = _matmul(u, p["a_w_qkv"][j], out_dtype=BF16, bias=p["a_b_qkv"][j][None], rope=(cos, sin),
                  rope_cols=A_Q_DIM + A_KV_DIM, name=f"{tag}_qkv", plan=plan)
    o, lse = _attn_fwd(qkv, B, S, 1, n_heads=A_N_HEADS, n_kv=A_N_KV, q_col=0, k_col=A_Q_DIM, v_col=A_Q_DIM + A_KV_DIM,
                       max_dist=A_WINDOW - 1, sinks=p["a_sinks"][j], name=f"{tag}_attn", plan=plan)
    h1, u2 = _matmul(o, p["a_w_o"][j], bias=p["a_b_o"][j][None], resid=h, norm_out=nw, name=f"{tag}_o")
    return h1, u2, (qkv, o, lse)


def _swa_bwd(dh1, u, saved, p, j, B, S, cos, sin, tag, norm, plan=None):
    qkv, o, lse = saved
    kw = dict(n_heads=A_N_HEADS, n_kv=A_N_KV, q_col=0, k_col=A_Q_DIM, v_col=A_Q_DIM + A_KV_DIM, max_dist=A_WINDOW - 1)
    g = {}
    do = _matmul(dh1, p["a_w_o"][j], tb=True, name=f"{tag}_do")
    g["a_w_o"] = _matmul(o, dh1, ta=True, name=f"{tag}_dwo")
    g["a_b_o"] = _colsum(dh1, f"{tag}_dbo")[0]
    sk = jnp.pad(p["a_sinks"][j], (0, LANES - A_N_HEADS))[None]
    delta, dob, dsink = _delta(do, o, lse, sk, name=f"{tag}_delta")
    g["a_sinks"] = dsink[0, :A_N_HEADS]
    dq = _attn_dq(qkv, dob, lse, delta, cos, sin, B, S, 1, name=f"{tag}_dq", plan=plan, **kw)
    dk, dv = _attn_dkv(qkv, dob, lse, delta, cos, sin, B, S, 1, name=f"{tag}_dkv", plan=plan, **kw)
    dqkv = jnp.concatenate([dq, dk, dv], axis=1)
    g["a_w_qkv"] = _matmul(u, dqkv, ta=True, name=f"{tag}_dwqkv")
    g["a_b_qkv"] = _colsum(dqkv, f"{tag}_dbqkv")[0]
    if plan is not None:
        plan.grads(3 * j, "mix", {"a_w_qkv": g["a_w_qkv"], "a_w_o": g["a_w_o"]})
    dh, dnw = _matmul(dqkv, p["a_w_qkv"][j], tb=True, norm_bwd=(norm[0], norm[1], dh1), name=f"{tag}_du", plan=plan)
    return dh, dnw, g


def _group_cols(gi, qkv):
    W = C_HEADS * HEAD_DIM
    if C_PATTERNS[gi][1] == 1:
        return qkv, (gi * W, (3 + gi) * W, (6 + gi) * W)
    part = jnp.concatenate([qkv[:, (3 * j + gi) * W:(3 * j + gi + 1) * W] for j in range(3)], axis=1)
    return part, (0, W, 2 * W)


def _dil_fwd(u, h, p, B, S, cos, sin, nw, plan=None):
    W = C_HEADS * HEAD_DIM
    qkv = _matmul(u, p["c_w_qkv"][0], out_dtype=BF16, rope=(cos, sin), rope_cols=6 * W, name="c_qkv", plan=plan)
    os_, lses, parts = [], [], []
    for gi, (window, dil) in enumerate(C_PATTERNS):
        part, (qc, kc, vc) = _group_cols(gi, qkv)
        o, lse = _attn_fwd(part, B, S, dil, n_heads=C_HEADS, n_kv=C_HEADS, q_col=qc, k_col=kc, v_col=vc,
                           max_dist=window // dil, sinks=None, name=f"c_attn{gi}", plan=plan)
        os_.append(o)
        lses.append(lse)
        parts.append((part, (qc, kc, vc)))
    o, lse = _merge(os_, lses)
    h1, u2 = _matmul(o, p["c_w_o"][0], resid=h, norm_out=nw, name="c_o")
    return h1, u2, (parts, o, lse)


def _dil_bwd(dh1, u, saved, p, B, S, cos, sin, norm, plan=None):
    parts, o, lse = saved
    g = {}
    do = _matmul(dh1, p["c_w_o"][0], tb=True, name="c_do")
    g["c_w_o"] = _matmul(o, dh1, ta=True, name="c_dwo")[None]
    delta, dob = _delta(do, o, name="c_delta")
    dqs, dks, dvs = [], [], []
    for gi, (window, dil) in enumerate(C_PATTERNS):
        part, (qc, kc, vc) = parts[gi]
        kw = dict(n_heads=C_HEADS, n_kv=C_HEADS, q_col=qc, k_col=kc, v_col=vc, max_dist=window // dil)
        dqs.append(_attn_dq(part, dob, lse, delta, cos, sin, B, S, dil, name=f"c_dq{gi}", **kw))
        dk, dv = _attn_dkv(part, dob, lse, delta, cos, sin, B, S, dil, name=f"c_dkv{gi}", **kw)
        dks.append(dk)
        dvs.append(dv)
    dqkv = jnp.concatenate(dqs + dks + dvs, axis=1)
    g["c_w_qkv"] = _matmul(u, dqkv, ta=True, name="c_dwqkv", plan=plan)[None]
    if plan is not None:
        plan.grads(2, "mix", {"c_w_qkv": g["c_w_qkv"][0], "c_w_o": g["c_w_o"][0]})
    dh, dnw = _matmul(dqkv, p["c_w_qkv"][0], tb=True, norm_bwd=(norm[0], norm[1], dh1), name="c_du", plan=plan)
    return dh, dnw, g


def _ssm_params(p):
    par = jnp.stack([p["b_dt_bias"][0], p["b_a_log"][0], p["b_d"][0]], axis=0)
    prow = par.reshape(3, SSM_N_GROUPS, SSM_HG).transpose(1, 0, 2)
    return prow, prow.transpose(0, 2, 1)


def _mamba_fwd(u, h, p, B, S, nw, plan=None):
    T = B * S
    G, HG = SSM_N_GROUPS, SSM_HG
    w_in = p["b_in_w"][0]
    nzx = SSM_D_INNER + SSM_CONV_DIM
    w_dt = jnp.pad(w_in[:, nzx:], ((0, 0), (0, LANES - SSM_N_HEADS)))
    zx = _matmul(u, w_in[:, :nzx], name="b_zx", plan=plan)
    dtraw = _matmul(u, w_dt, name="b_dt")[:, :SSM_N_HEADS]
    dtc = dtraw.reshape(B, S, G, HG).transpose(0, 2, 1, 3)
    dtr = dtraw.reshape(B, S, G, HG).transpose(0, 2, 3, 1)
    prow, pcol = _ssm_params(p)
    zx3 = zx.reshape(B, S, nzx)
    xc3 = _conv_fwd(zx3, p["b_conv_w"][0], p["b_conv_b"])
    y3, states = _ssd_fwd(xc3, dtc, dtr, prow, pcol, plan=plan)
    y = y3.reshape(T, SSM_D_INNER)
    gn = _gate_fwd(y, zx, p["b_norm_w"])
    h1, u2 = _matmul(gn, p["b_out_w"][0], resid=h, norm_out=nw, name="b_out", plan=plan)
    return h1, u2, (zx, dtc, dtr, xc3, y, states, gn, w_dt)


def _mamba_bwd(dh1, u, saved, p, B, S, norm, plan=None):
    T = B * S
    zx, dtc, dtr, xc3, y, states, gn, w_dt = saved
    nzx = SSM_D_INNER + SSM_CONV_DIM
    w_in = p["b_in_w"][0]
    prow, pcol = _ssm_params(p)
    g = {}
    dgn = _matmul(dh1, p["b_out_w"][0], tb=True, name="b_dgn")
    g["b_out_w"] = _matmul(gn, dh1, ta=True, name="b_dwout")[None]
    dy, dz, dnw = _gate_bwd(dgn, y, zx, p["b_norm_w"])
    g["b_norm_w"] = dnw
    dx3, dB3, dC3, ddt, dpar = _ssd_bwd(xc3, dtc, dtr, prow, pcol, states, dy.reshape(B, S, SSM_D_INNER), plan=plan)
    dpar = dpar.transpose(1, 0, 2).reshape(3, SSM_N_HEADS)
    g["b_dt_bias"], g["b_a_log"], g["b_d"] = dpar[0:1], dpar[1:2], dpar[2:3]
    zx3 = zx.reshape(B, S, nzx)
    cw, cb = p["b_conv_w"][0], p["b_conv_b"]
    parts, dws, dbs = [], [], []
    for col0, dpart, nm in ((0, dx3, "b_conv_bwd_x"), (SSM_D_INNER, dB3, "b_conv_bwd_b"),
                            (SSM_D_INNER + SSM_BC_DIM, dC3, "b_conv_bwd_c")):
        dxp, dw, db = _conv_bwd(zx3, dpart, cw, cb, col0, nm)
        parts.append(dxp.reshape(T, -1))
        dws.append(dw)
        dbs.append(db)
    g["b_conv_w"] = jnp.concatenate(dws, axis=1)[None]
    g["b_conv_b"] = jnp.concatenate(dbs, axis=1)
    dzx = jnp.concatenate([dz] + parts, axis=1)
    ddtraw = ddt.transpose(0, 2, 1, 3).reshape(T, SSM_N_HEADS)
    ddtp = jnp.pad(ddtraw, ((0, 0), (0, LANES - SSM_N_HEADS)))
    dw_zx = _matmul(u, dzx, ta=True, name="b_dwzx")
    dw_dt = _matmul(u, ddtp, ta=True, name="b_dwdt")[:, :SSM_N_HEADS]
    g["b_in_w"] = jnp.concatenate([dw_zx, dw_dt], axis=1)[None]
    if plan is not None:
        plan.grads(1, "mix", {"b_in_w": g["b_in_w"][0], "b_out_w": g["b_out_w"][0]})
    du = _matmul(dzx, w_in[:, :nzx], tb=True, name="b_du_zx", plan=plan)
    dh, dnw = _matmul(ddtp, w_dt, tb=True, resid=du, norm_bwd=(norm[0], norm[1], dh1), name="b_du_dt")
    return dh, dnw, g


def _local_step(x, positions, p, target, plan=None):
    B, S, D = x.shape
    T = B * S
    cos, sin = _rope_tables(positions)
    h = x.reshape(T, D)
    tape = []
    u = _rmsnorm_fwd(h, p["norm_mix_w"][0], "l0_norm_mix")
    for i in range(DEPTH):
        kind, j = i % 3, i // 3
        nw = p["norm_mlp_w"][i]
        if kind == 0:
            h1, u2, saved = _swa_fwd(u, h, p, j, B, S, cos, sin, f"a{j}", nw, plan)
        elif kind == 1:
            h1, u2, saved = _mamba_fwd(u, h, p, B, S, nw, plan)
        else:
            h1, u2, saved = _dil_fwd(u, h, p, B, S, cos, sin, nw, plan)
        r, s = _matmul(u2, p["mlp_w_up"][i], out_dtype=BF16, relu2=True, name=f"l{i}_up", plan=plan)
        if i + 1 < DEPTH:
            h2, u_next = _matmul(s, p["mlp_w_down"][i], resid=h1, norm_out=p["norm_mix_w"][i + 1], name=f"l{i}_down", plan=plan)
        else:
            h2, u_next = _matmul(s, p["mlp_w_down"][i], resid=h1, name=f"l{i}_down", plan=plan), None
        tape.append((h, u, saved, h1, u2, r, s))
        h, u = h2, u_next
    dh, dwf, loss = _final_loss(h, target.reshape(T, D), p["final_norm_w"])
    grads = {"final_norm_w": dwf[0]}
    per_layer = {n: [None] * DEPTH for n in ("norm_mix_w", "norm_mlp_w", "mlp_w_up", "mlp_w_down")}
    a_grads = [None, None]
    for i in reversed(range(DEPTH)):
        kind, j = i % 3, i // 3
        h0, u, saved, h1, u2, r, s = tape[i]
        da = _matmul(dh, p["mlp_w_down"][i], tb=True, out_dtype=BF16, mul=r, mul_scale=2.0, name=f"l{i}_da", plan=plan)
        per_layer["mlp_w_down"][i] = _matmul(s, dh, ta=True, name=f"l{i}_dwdown")
        per_layer["mlp_w_up"][i] = _matmul(u2, da, ta=True, name=f"l{i}_dwup")
        if plan is not None:
            plan.grads(i, "mlp", {"mlp_w_up": per_layer["mlp_w_up"][i], "mlp_w_down": per_layer["mlp_w_down"][i]})
        dh1, dnw = _matmul(da, p["mlp_w_up"][i], tb=True, norm_bwd=(h1, p["norm_mlp_w"][i], dh), name=f"l{i}_du2", plan=plan)
        per_layer["norm_mlp_w"][i] = dnw[0]
        norm = (h0, p["norm_mix_w"][i])
        if kind == 0:
            dh, dnw, g = _swa_bwd(dh1, u, saved, p, j, B, S, cos, sin, f"a{j}", norm, plan)
            a_grads[j] = g
        elif kind == 1:
            dh, dnw, g = _mamba_bwd(dh1, u, saved, p, B, S, norm, plan)
            grads.update(g)
        else:
            dh, dnw, g = _dil_bwd(dh1, u, saved, p, B, S, cos, sin, norm, plan)
            grads.update(g)
        per_layer["norm_mix_w"][i] = dnw[0]
    for n in ("norm_mix_w", "norm_mlp_w"):
        grads[n] = jnp.stack(per_layer[n], axis=0)
    for n in ("mlp_w_up", "mlp_w_down"):
        grads[n] = per_layer[n]
    for n in ("a_b_qkv", "a_sinks", "a_b_o"):
        grads[n] = jnp.stack([a_grads[0][n], a_grads[1][n]], axis=0)
    for n in ("a_w_qkv", "a_w_o"):
        grads[n] = [a_grads[0][n], a_grads[1][n]]
    for n in ("b_in_w", "b_out_w", "c_w_qkv", "c_w_o"):
        grads[n] = [grads[n][0]]
    return loss, dh.reshape(B, S, D), grads


MIX = {0: ("a_w_qkv", "a_w_o"), 1: ("b_in_w", "b_out_w"), 2: ("c_w_qkv", "c_w_o")}
MLP = ("mlp_w_up", "mlp_w_down")
GATHER_FIRST = (0, MIX[0])
GATHER_HOSTS = {"a0_qkv": ((0, ("mlp_w_up",)),), "a0_attn": ((0, ("mlp_w_down",)),), "l0_up": ((1, ("b_in_w",)),),
                "l0_down": ((1, ("b_out_w",)),), "b_zx": ((1, ("mlp_w_up",)),),
                "b_ssd_fwd": ((1, ("mlp_w_down",)), (2, ("c_w_qkv",))), "b_out": ((2, ("c_w_o",)),),
                "l1_up": ((2, ("mlp_w_up",)),), "l1_down": ((2, ("mlp_w_down",)),),
                "c_qkv": ((3, MLP),), "c_attn1": ((3, MIX[0]),)}
REDUCE_HOSTS = {(3, "mlp"): ("l3_du2", "a1_dkv"), (3, "mix"): ("a1_du", "l2_da"),
                (2, "mlp"): ("l2_du2", "c_dwqkv"), (2, "mix"): ("c_du", "b_ssd_bwd"),
                (1, "mlp"): ("l1_du2", "b_ssd_bwd"), (1, "mix"): ("b_du_zx", "a0_dq"),
                (0, "mlp"): ("l0_du2", "a0_dkv"), (0, "mix"): (None, None)}


class _Plan:
    def __init__(self, w, m, v, p, dev, chip, core):
        self.w, self.m, self.v, self.p, self.dev, self.chip, self.core = w, m, v, p, dev, chip, core
        self.pending = {}
        self.res = {n: None for n in BIG_KIND}
        self._install(*GATHER_FIRST)(_gather(self._gather_items(*GATHER_FIRST), "gather_first"))
        for host, groups in GATHER_HOSTS.items():
            for i, only in groups:
                self._wait_for(host, _gather_comm(self._gather_items(i, only)), self._install(i, only))

    def _wait_for(self, host, comm, done):
        self.pending.setdefault(host, []).append((comm, done))

    def _names(self, i, only):
        return [(n, l) for n, l in _layer_big(i) if only is None or n in only]

    def _gather_items(self, i, only):
        items = []
        for n, l in self._names(i, only):
            kind, s2 = BIG_KIND[n], self.w[n].shape[1:]
            placed = _place(self.w[n], l, kind, _full2d(kind, s2), self.dev, f"place_l{i}_{n}")
            items.append((placed, kind, _block_size(kind, s2), _full2d(kind, s2), True))
        return items

    def _install(self, i, only):
        def done(fulls):
            for (n, l), t in zip(self._names(i, only), fulls):
                self.p[n][l] = _from_slots(t, 1) if BIG_KIND[n] == "slot" else t
        return done

    def take(self, host):
        return _Comm.merge([c for c, _ in self.pending[host]]) if host in self.pending else None

    def give(self, host, results):
        for comm, done in self.pending.pop(host):
            done(results[:len(comm.out_shapes)])
            results = results[len(comm.out_shapes):]

    def grads(self, i, group, grads):
        names = self._names(i, MLP if group == "mlp" else MIX[i % 3])
        items = []
        for n, _ in names:
            kind, s2 = BIG_KIND[n], self.w[n].shape[1:]
            items.append((_to_slots(grads[n], 1) if kind == "slot" else grads[n], kind, _block_size(kind, s2), s2))
        d2d_host, ici_host = REDUCE_HOSTS[(i, group)]
        tag = f"l{i}_{group}"

        def update(parts):
            def done(recv):
                for (n, l), pt, r in zip(names, parts, recv):
                    self.res[n] = _adamw(pt, r, self.w[n], self.m[n], self.v[n], l, self.res[n], self.chip, f"adamw_l{i}_{n}")
            return done

        def second(sib):
            parts = [_pair_sum(it[0], s, it[1], self.core, f"pair_sum_l{i}_{n}") for (n, _), it, s in zip(names, items, sib)]
            self._send(ici_host, _reduce_ici_comm(parts), update(parts), f"reduce_ici_{tag}")

        self._send(d2d_host, _reduce_d2d_comm(items), second, f"reduce_d2d_{tag}")

    def _send(self, host, comm, done, name):
        if host is None:
            done(_run_comm(comm, name))
        else:
            self._wait_for(host, comm, done)

    def flush(self):
        late = 0
        while self.pending:
            host = next(iter(self.pending))
            for comm, done in self.pending.pop(host):
                done(_run_comm(comm, f"late_{late}_{host}"))
                late += 1


def kernel(x, positions, norm_mix_w, norm_mlp_w, a_w_qkv, a_b_qkv, a_sinks, a_w_o, a_b_o, b_in_w, b_conv_w, b_conv_b, b_dt_bias, b_a_log, b_d, b_norm_w, b_out_w, c_w_qkv, c_w_o, mlp_w_up, mlp_w_down, final_norm_w, loss_target, m_norm_mix_w, m_norm_mlp_w, m_a_w_qkv, m_a_b_qkv, m_a_sinks, m_a_w_o, m_a_b_o, m_b_in_w, m_b_conv_w, m_b_conv_b, m_b_dt_bias, m_b_a_log, m_b_d, m_b_norm_w, m_b_out_w, m_c_w_qkv, m_c_w_o, m_mlp_w_up, m_mlp_w_down, m_final_norm_w, v_norm_mix_w, v_norm_mlp_w, v_a_w_qkv, v_a_b_qkv, v_a_sinks, v_a_w_o, v_a_b_o, v_b_in_w, v_b_conv_w, v_b_conv_b, v_b_dt_bias, v_b_a_log, v_b_d, v_b_norm_w, v_b_out_w, v_c_w_qkv, v_c_w_o, v_mlp_w_up, v_mlp_w_down, v_final_norm_w):
    w = dict(zip(W_NAMES, (norm_mix_w, norm_mlp_w, a_w_qkv, a_b_qkv, a_sinks, a_w_o, a_b_o, b_in_w, b_conv_w, b_conv_b,
                           b_dt_bias, b_a_log, b_d, b_norm_w, b_out_w, c_w_qkv, c_w_o, mlp_w_up, mlp_w_down, final_norm_w)))
    m = dict(zip(W_NAMES, (m_norm_mix_w, m_norm_mlp_w, m_a_w_qkv, m_a_b_qkv, m_a_sinks, m_a_w_o, m_a_b_o, m_b_in_w,
                           m_b_conv_w, m_b_conv_b, m_b_dt_bias, m_b_a_log, m_b_d, m_b_norm_w, m_b_out_w, m_c_w_qkv, m_c_w_o,
                           m_mlp_w_up, m_mlp_w_down, m_final_norm_w)))
    v = dict(zip(W_NAMES, (v_norm_mix_w, v_norm_mlp_w, v_a_w_qkv, v_a_b_qkv, v_a_sinks, v_a_w_o, v_a_b_o, v_b_in_w,
                           v_b_conv_w, v_b_conv_b, v_b_dt_bias, v_b_a_log, v_b_d, v_b_norm_w, v_b_out_w, v_c_w_qkv, v_c_w_o,
                           v_mlp_w_up, v_mlp_w_down, v_final_norm_w)))
    px, py, pc = lax.axis_index("x"), lax.axis_index("y"), lax.axis_index("c")
    me = 4 * px + 2 * py + pc
    dev, chip, core = (t.astype(jnp.int32).reshape(1) for t in (me, 2 * px + py, pc))

    trio = tuple(SMALL_SHARDED)
    got = _gather([(d[n], "slot", None, (N_DEV,) + d[n].shape, False) for n in trio for d in (w, m, v)], "gather_small")
    slots = {n: got[3 * i:3 * i + 3] for i, n in enumerate(trio)}
    p = {n: w[n] for n in SMALL_REPLICATED}
    for n in trio:
        p[n] = _from_slots(slots[n][0], SMALL_SHARDED[n])
    for n in BIG_KIND:
        p[n] = [None] * w[n].shape[0]
    plan = _Plan(w, m, v, p, dev, chip, core)
    loss_part, dx, grads = _local_step(x, positions, p, loss_target, plan)
    loss = lax.psum(loss_part[0, 0], AXES)
    plan.flush()
    out = {n: list(plan.res[n]) for n in BIG_KIND}

    small = SMALL_REPLICATED + trio
    as2d = lambda t: t.reshape(1, -1) if t.ndim == 1 else t
    g_sm = [as2d(grads[n]) for n in SMALL_REPLICATED] + [_to_slots(grads[n].reshape(p[n].shape), SMALL_SHARDED[n]) for n in trio]
    gathered = _gather([(g, "slot", None, (N_DEV,) + g.shape, False) for g in g_sm], "gather_small_grads")
    ws = [as2d(w[n]) for n in SMALL_REPLICATED] + [slots[n][0] for n in trio]
    ms = [as2d(m[n]) for n in SMALL_REPLICATED] + [slots[n][1] for n in trio]
    vs = [as2d(v[n]) for n in SMALL_REPLICATED] + [slots[n][2] for n in trio]
    sm_out = _small_adamw(gathered, ws, ms, vs)
    for i, n in enumerate(small):
        if n in SMALL_SHARDED:
            out[n] = [lax.dynamic_index_in_dim(sm_out[k][i], me, 0, keepdims=False) for k in range(4)]
        else:
            out[n] = [sm_out[k][i].reshape(w[n].shape) for k in range(4)]
    return (loss, dx, *[out[n][0] for n in W_NAMES], *[out[n][1] for n in W_NAMES], *[out[n][2] for n in W_NAMES],
            *[out[n][3] for n in W_NAMES])
```

```python
import math

import jax
import jax.numpy as jnp
from jax import lax
from jax.experimental import pallas as pl
from jax.experimental.pallas import tpu as pltpu

F32 = jnp.float32
BF16 = jnp.bfloat16
SDS = jax.ShapeDtypeStruct

D_MODEL = 1024
DEPTH = 4
BLOCK = 128
ROPE_THETA = 10000.0
NORM_EPS = 1e-5
HEAD_DIM = 64
A_N_HEADS = 16
A_N_KV = 2
A_WINDOW = 128
A_Q_DIM = 1024
A_KV_DIM = 128
SSM_D_INNER = 2048
SSM_N_HEADS = 32
SSM_N_GROUPS = 8
SSM_HG = 4
SSM_D_STATE = 128
SSM_CONV = 4
SSM_CHUNK = 128
SSM_BC_DIM = 1024
SSM_CONV_DIM = 4096
C_PATTERNS = ((128, 1), (512, 4), (2048, 16))
C_HEADS = 16
ADAM_LR, ADAM_B1, ADAM_B2, ADAM_EPS, ADAM_WD, ADAM_STEP = 0.001, 0.9, 0.999, 1e-08, 0.01, 10

N_DEV = 8
AXES = ("x", "y", "c")
LANES = 128
VMEM_LIMIT = 56 * 1024 * 1024
STREAM_VMEM = 16 * 1024 * 1024
NEG = -1e30

NN = (((1,), (0,)), ((), ()))
NT = (((1,), (1,)), ((), ()))
TN = (((0,), (0,)), ((), ()))
HI = lax.Precision.HIGHEST


def _pick(n, cap, mult=LANES):
    best = None
    for t in range(mult, min(n, cap) + 1, mult):
        if n % t == 0:
            best = t
    return best if best is not None else n


def _params(sem):
    return pltpu.CompilerParams(dimension_semantics=sem, vmem_limit_bytes=VMEM_LIMIT)


def _bf(x):
    return x if x.dtype == BF16 else x.astype(BF16)


def _rot_half(y):
    n = y.shape[-1]
    lane = lax.broadcasted_iota(jnp.int32, y.shape, y.ndim - 1)
    return jnp.where((lane % HEAD_DIM) < HEAD_DIM // 2, -pltpu.roll(y, n - 32, y.ndim - 1), pltpu.roll(y, 32, y.ndim - 1))


def _rope(y, cos, sin, sign):
    reps = y.shape[-1] // LANES
    c = jnp.tile(cos, (1, reps)) if reps > 1 else cos
    s = jnp.tile(sin, (1, reps)) if reps > 1 else sin
    return y * c + sign * (_rot_half(y) * s)


MESH = pl.DeviceIdType.MESH
ANY = pl.BlockSpec(memory_space=pl.ANY)


class _Comm:
    def __init__(self, inputs, out_shapes, aliases, sems, phases):
        self.inputs, self.out_shapes, self.aliases, self.sems, self.phases = inputs, out_shapes, aliases, sems, phases

    @staticmethod
    def merge(comms):
        if len(comms) == 1:
            return comms[0]
        ins, outs, aliases, sems, spans = [], [], {}, [], []
        for c in comms:
            aliases.update({len(ins) + i: len(outs) + j for i, j in c.aliases.items()})
            spans.append((len(ins), len(ins) + len(c.inputs), len(outs), len(outs) + len(c.out_shapes), len(sems),
                          len(sems) + len(c.sems)))
            ins, outs, sems = ins + list(c.inputs), outs + list(c.out_shapes), sems + list(c.sems)
        phases = []
        for f in sorted({f for c in comms for f, _ in c.phases}):
            todo = [(fn, sp) for c, sp in zip(comms, spans) for g, fn in c.phases if g == f]

            def run(cins, couts, csems, todo=todo):
                for fn, (i0, i1, o0, o1, s0, s1) in todo:
                    fn(cins[i0:i1], couts[o0:o1], csems[s0:s1])
            phases.append((f, run))
        return _Comm(ins, outs, aliases, sems, phases)


def _pc(body, args, *, out_shape, grid, in_specs, out_specs, name, sem, scratch_shapes=(), comm=None):
    single = not isinstance(out_shape, (tuple, list))
    outs, ospecs = ([out_shape], [out_specs]) if single else (list(out_shape), list(out_specs))
    unpack = (lambda r: r[0]) if single else (lambda r: tuple(r))
    if comm is None:
        res = pl.pallas_call(body, out_shape=outs, grid=grid, in_specs=list(in_specs), out_specs=ospecs,
                             scratch_shapes=list(scratch_shapes), name=name, compiler_params=_params(sem))(*args)
        return unpack(res)
    n_in, n_out, n_scr = len(in_specs), len(outs), len(scratch_shapes)
    c_in, c_out = len(comm.inputs), len(comm.out_shapes)
    total = math.prod(grid)
    steps = [min(total - 1, int(f * total)) for f, _ in comm.phases[:-1]]

    def wrapped(*refs):
        ins, cins = refs[:n_in], refs[n_in:n_in + c_in]
        o = refs[n_in + c_in:n_in + c_in + n_out]
        couts = refs[n_in + c_in + n_out:n_in + c_in + n_out + c_out]
        rest = refs[n_in + c_in + n_out + c_out:]
        scr, csems = rest[:n_scr], rest[n_scr:]
        step = pl.program_id(0)
        for ax in range(1, len(grid)):
            step = step * grid[ax] + pl.program_id(ax)
        for (_, fn), st in zip(comm.phases[:-1], steps):
            @pl.when(step == st)
            def _(fn=fn):
                fn(cins, couts, csems)
        body(*ins, *o, *scr)

        @pl.when(step == total - 1)
        def _():
            comm.phases[-1][1](cins, couts, csems)

    res = pl.pallas_call(
        wrapped, out_shape=outs + list(comm.out_shapes), grid=grid, in_specs=list(in_specs) + [ANY] * c_in,
        out_specs=ospecs + [ANY] * c_out, scratch_shapes=list(scratch_shapes) + list(comm.sems),
        input_output_aliases={n_in + i: n_out + j for i, j in comm.aliases.items()}, name=name,
        compiler_params=_params(("arbitrary",) * len(grid)),
    )(*args, *comm.inputs)
    return unpack(res[:n_out]), list(res[n_out:])


def _hosted(plan, name, run):
    comm = plan.take(name) if plan is not None else None
    if comm is None:
        return run(None)
    res, extra = run(comm)
    plan.give(name, extra)
    return res


MM_VMEM = 40 * 1024 * 1024
HBM_BYTES_PER_US = 2.5e6
STEP_US = 0.35


def _divisors(n, cands):
    return [c for c in cands if c <= n and n % c == 0] or [n]


def _mm_tiles(M, N, K, sa, sb, out_bytes, extra_bytes, full_rows=False):
    best = None
    for tm in _divisors(M, (2048, 1024, 512, 256)):
        for tn in ([N] if full_rows else _divisors(N, (1024, 640, 512, 256, 128))):
            for tk in _divisors(K, (K, K // 2, K // 3, K // 4, 2048, 1024, 640, 512)):
                if tk != K and tk % LANES:
                    continue
                nk = K // tk
                vmem = 2 * tm * tk * sa + 2 * tk * tn * sb + tm * tn * (2 * (out_bytes + extra_bytes) + 8 + (4 if nk > 1 else 0))
                if vmem > MM_VMEM:
                    continue
                a_traffic = M * K * sa * (1 if nk == 1 else N // tn)
                b_traffic = K * N * sb * (1 if (nk == 1 and N == tn) else M // tm)
                steps = (M // tm) * (N // tn) * nk
                cost = (a_traffic + b_traffic + M * N * (out_bytes + extra_bytes)) / HBM_BYTES_PER_US + steps * STEP_US
                cost += (M // tm) * (N // tn) * (nk - 1) * tm * tn * 8 / (4 * HBM_BYTES_PER_US)
                if best is None or cost < best[0]:
                    best = (cost, tm, tn, tk)
    assert best is not None, (M, N, K)
    return best[1:]


def _matmul(a, b, *, ta=False, tb=False, out_dtype=F32, bias=None, resid=None, mul=None, mul_scale=1.0,
            relu2=False, rope=None, rope_cols=0, norm_out=None, norm_bwd=None, name="mm", plan=None):
    M = a.shape[1] if ta else a.shape[0]
    K = a.shape[0] if ta else a.shape[1]
    N = b.shape[0] if tb else b.shape[1]
    assert (b.shape[1] if tb else b.shape[0]) == K
    two_out = relu2 or norm_out is not None
    out_bytes = jnp.dtype(out_dtype).itemsize * (2 if relu2 else 1) + (2 if norm_out is not None else 0)
    extra_bytes = (4 if resid is not None else 0) + (mul.dtype.itemsize if mul is not None else 0) + (8 if norm_bwd else 0)
    rows = norm_out is not None or norm_bwd is not None
    tm, tn, tk = _mm_tiles(M, N, K, a.dtype.itemsize, b.dtype.itemsize, out_bytes, extra_bytes, full_rows=rows)
    nk = K // tk
    dims = (((0 if ta else 1,), (1 if tb else 0,)), ((), ()))

    def body(*refs):
        it = iter(refs)
        a_ref, b_ref = next(it), next(it)
        bias_ref = next(it) if bias is not None else None
        resid_ref = next(it) if resid is not None else None
        mul_ref = next(it) if mul is not None else None
        cos_ref, sin_ref = (next(it), next(it)) if rope is not None else (None, None)
        nw_ref = next(it) if rows else None
        h_ref, dres_ref = (next(it), next(it)) if norm_bwd is not None else (None, None)
        o_ref = next(it)
        o2_ref = next(it) if two_out or norm_bwd is not None else None
        acc_ref = next(it) if nk > 1 else None
        k = pl.program_id(2)
        part = lax.dot_general(_bf(a_ref[...]), _bf(b_ref[...]), dims, preferred_element_type=F32)
        if nk > 1:
            @pl.when(k == 0)
            def _():
                acc_ref[...] = part

            @pl.when(k > 0)
            def _():
                acc_ref[...] += part

        @pl.when(k == nk - 1)
        def _():
            y = acc_ref[...] if nk > 1 else part
            if bias_ref is not None:
                y = y + bias_ref[...]
            if rope is not None and rope_cols % tn == 0 and not (two_out or rows or mul is not None or resid is not None):
                rotated = pl.program_id(1) * tn < rope_cols

                @pl.when(rotated)
                def _():
                    o_ref[...] = _rope(y, cos_ref[...], sin_ref[...], 1.0).astype(o_ref.dtype)

                @pl.when(jnp.logical_not(rotated))
                def _():
                    o_ref[...] = y.astype(o_ref.dtype)
                return
            if rope is not None:
                col = pl.program_id(1) * tn + lax.broadcasted_iota(jnp.int32, y.shape, 1)
                y = jnp.where(col < rope_cols, _rope(y, cos_ref[...], sin_ref[...], 1.0), y)
            if mul_ref is not None:
                y = y * (mul_ref[...].astype(F32) * mul_scale)
            if resid_ref is not None:
                y = y + resid_ref[...]
            if relu2:
                r = jnp.maximum(y, 0.0)
                o_ref[...] = r.astype(o_ref.dtype)
                o2_ref[...] = (r * r).astype(o2_ref.dtype)
            elif norm_bwd is not None:
                x = h_ref[...]
                rstd = lax.rsqrt(jnp.mean(x * x, axis=-1, keepdims=True) + NORM_EPS)
                g = y * nw_ref[...]
                o_ref[...] = dres_ref[...] + rstd * g - x * (rstd * rstd * rstd) * jnp.mean(g * x, axis=-1, keepdims=True)
                dw = jnp.sum(y * x * rstd, axis=0, keepdims=True)
                first = pl.program_id(0) == 0

                @pl.when(first)
                def _():
                    o2_ref[...] = dw

                @pl.when(jnp.logical_not(first))
                def _():
                    o2_ref[...] += dw
            else:
                o_ref[...] = y.astype(o_ref.dtype)
                if norm_out is not None:
                    rstd = lax.rsqrt(jnp.mean(y * y, axis=-1, keepdims=True) + NORM_EPS)
                    o2_ref[...] = (y * rstd * nw_ref[...]).astype(BF16)

    a_spec = pl.BlockSpec((tk, tm), lambda i, j, k: (k, i)) if ta else pl.BlockSpec((tm, tk), lambda i, j, k: (i, k))
    b_spec = pl.BlockSpec((tn, tk), lambda i, j, k: (j, k)) if tb else pl.BlockSpec((tk, tn), lambda i, j, k: (k, j))
    mn_spec = pl.BlockSpec((tm, tn), lambda i, j, k: (i, j))
    in_specs, args = [a_spec, b_spec], [a, b]
    if bias is not None:
        in_specs.append(pl.BlockSpec((1, tn), lambda i, j, k: (0, j)))
        args.append(bias)
    if resid is not None:
        in_specs.append(mn_spec)
        args.append(resid)
    if mul is not None:
        in_specs.append(mn_spec)
        args.append(mul)
    if rope is not None:
        in_specs += [pl.BlockSpec((tm, LANES), lambda i, j, k: (i, 0))] * 2
        args += [rope[0], rope[1]]
    vec_spec = pl.BlockSpec((1, tn), lambda i, j, k: (0, j))
    if rows:
        in_specs.append(vec_spec)
        args.append((norm_out if norm_out is not None else norm_bwd[1]).reshape(1, N))
    if norm_bwd is not None:
        in_specs += [mn_spec, mn_spec]
        args += [norm_bwd[0], norm_bwd[2]]
    out_shape = SDS((M, N), out_dtype)
    out_specs = mn_spec
    if relu2:
        out_shape, out_specs = (out_shape, out_shape), (mn_spec, mn_spec)
    elif norm_out is not None:
        out_shape, out_specs = (out_shape, SDS((M, N), BF16)), (mn_spec, mn_spec)
    elif norm_bwd is not None:
        out_shape, out_specs = (out_shape, SDS((1, N), F32)), (mn_spec, vec_spec)
    sem = ("arbitrary",) * 3 if norm_bwd is not None else ("parallel", "parallel", "arbitrary")
    return _hosted(plan, name, lambda comm: _pc(
        body, args, out_shape=out_shape, grid=(M // tm, N // tn, nk), in_specs=in_specs, out_specs=out_specs,
        scratch_shapes=[pltpu.VMEM((tm, tn), F32)] if nk > 1 else [], name=name, sem=sem, comm=comm))


def _colsum(x, name):
    T, N = x.shape
    tm = _pick(T, 1024, 8)

    def body(x_ref, o_ref):
        s = jnp.sum(x_ref[...].astype(F32), axis=0, keepdims=True)

        @pl.when(pl.program_id(0) == 0)
        def _():
            o_ref[...] = s

        @pl.when(pl.program_id(0) > 0)
        def _():
            o_ref[...] += s

    return pl.pallas_call(
        body, out_shape=SDS((1, N), F32), grid=(T // tm,),
        in_specs=[pl.BlockSpec((tm, N), lambda i: (i, 0))], out_specs=pl.BlockSpec((1, N), lambda i: (0, 0)),
        name=name, compiler_params=_params(("arbitrary",)),
    )(x)


def _rmsnorm_fwd(h, w, name):
    T, D = h.shape
    tm = _pick(T, 512, 8)

    def body(h_ref, w_ref, o_ref):
        x = h_ref[...]
        rstd = lax.rsqrt(jnp.mean(x * x, axis=-1, keepdims=True) + NORM_EPS)
        o_ref[...] = (x * rstd * w_ref[...]).astype(BF16)

    return pl.pallas_call(
        body, out_shape=SDS((T, D), BF16), grid=(T // tm,),
        in_specs=[pl.BlockSpec((tm, D), lambda i: (i, 0)), pl.BlockSpec((1, D), lambda i: (0, 0))],
        out_specs=pl.BlockSpec((tm, D), lambda i: (i, 0)), name=name, compiler_params=_params(("parallel",)),
    )(h, w.reshape(1, D))


def _final_loss(h, target, w):
    T, D = h.shape
    tm = _pick(T, 512, 8)

    def body(h_ref, t_ref, w_ref, dh_ref, dw_ref, loss_ref):
        x = h_ref[...]
        rstd = lax.rsqrt(jnp.mean(x * x, axis=-1, keepdims=True) + NORM_EPS)
        xn = x * rstd
        err = xn * w_ref[...] - t_ref[...]
        part = 0.5 * jnp.sum(jnp.mean(err * err, axis=-1, keepdims=True), axis=0, keepdims=True)
        dy = err * (1.0 / D)
        g = dy * w_ref[...]
        dh_ref[...] = rstd * g - x * (rstd * rstd * rstd) * jnp.mean(g * x, axis=-1, keepdims=True)
        dw = jnp.sum(dy * xn, axis=0, keepdims=True)
        lp = jnp.broadcast_to(part, (1, LANES))

        @pl.when(pl.program_id(0) == 0)
        def _():
            dw_ref[...] = dw
            loss_ref[...] = lp

        @pl.when(pl.program_id(0) > 0)
        def _():
            dw_ref[...] += dw
            loss_ref[...] += lp

    row = pl.BlockSpec((tm, D), lambda i: (i, 0))
    vec = pl.BlockSpec((1, D), lambda i: (0, 0))
    return pl.pallas_call(
        body, out_shape=(SDS((T, D), F32), SDS((1, D), F32), SDS((1, LANES), F32)), grid=(T // tm,),
        in_specs=[row, row, vec], out_specs=(row, vec, pl.BlockSpec((1, LANES), lambda i: (0, 0))),
        name="final_loss", compiler_params=_params(("arbitrary",)),
    )(h, target, w.reshape(1, D))


def _band_mask(i_blk, max_dist, first_ok):
    qi = lax.broadcasted_iota(jnp.int32, (BLOCK, 2 * BLOCK), 0)
    kj = lax.broadcasted_iota(jnp.int32, (BLOCK, 2 * BLOCK), 1)
    dist = qi + BLOCK - kj
    ok = (dist >= 0) & (dist <= max_dist)
    return ok & ((kj >= BLOCK) | first_ok)


def _pair(t, i):
    return t[:, LANES * i:LANES * (i + 1)]


def _low_half(shape):
    return lax.broadcasted_iota(jnp.int32, shape, len(shape) - 1) < HEAD_DIM


def _stack_heads(t):
    lo = _low_half(t.shape)
    z = jnp.zeros_like(t)
    return jnp.concatenate([jnp.where(lo, t, z), jnp.where(lo, z, t)], axis=0)


def _swap_halves(t):
    return jnp.concatenate([t[:, HEAD_DIM:], t[:, :HEAD_DIM]], axis=1)


def _kv_operand(kv, kv_swapped, h0, n_kv, n_heads):
    R = n_heads // n_kv
    if R == 1:
        return _pair(kv, h0 // 2)
    assert kv.shape[1] == LANES and R % 2 == 0, "grouped queries: one 128-lane tile of kv heads, both heads of a pair in one group"
    g = h0 // R
    t, ts = _pair(kv, g // 2), _pair(kv_swapped, g // 2)
    lo = _low_half(t.shape)
    return jnp.where(lo, t, ts) if g % 2 == 0 else jnp.where(lo, ts, t)


def _lane_place(cols):
    m = cols[0].shape[0]
    lane = lax.broadcasted_iota(jnp.int32, (m, LANES), 1)
    out = jnp.zeros((m, LANES), F32)
    for h, c in enumerate(cols):
        out = jnp.where(lane == h, c, out)
    return out


def _attn_specs(B, S, d, C, n_heads, n_kv, q_col, k_col, v_col):
    kvw = n_kv * HEAD_DIM
    qw = n_heads * HEAD_DIM
    cq, ck = (C // qw if d > 1 else 0), (C // kvw if d > 1 else 0)
    q_spec = pl.BlockSpec((1, BLOCK, qw), lambda b, r, i: (b, i, r * cq + q_col // qw))
    kc = pl.BlockSpec((1, BLOCK, kvw), lambda b, r, i: (b, i, r * ck + k_col // kvw))
    kp = pl.BlockSpec((1, BLOCK, kvw), lambda b, r, i: (b, jnp.maximum(i - 1, 0), r * ck + k_col // kvw))
    vc = pl.BlockSpec((1, BLOCK, kvw), lambda b, r, i: (b, i, r * ck + v_col // kvw))
    vp = pl.BlockSpec((1, BLOCK, kvw), lambda b, r, i: (b, jnp.maximum(i - 1, 0), r * ck + v_col // kvw))
    return q_spec, kp, kc, vp, vc


def _attn_fwd(qkv, B, S, d, *, n_heads, n_kv, q_col, k_col, v_col, max_dist, sinks, name, plan=None):
    C = qkv.shape[1]
    Ls = S // d
    nb = Ls // BLOCK
    qw = n_heads * HEAD_DIM
    R = n_heads // n_kv
    qkv3 = qkv.reshape(B, Ls, d * C)
    scale = HEAD_DIM ** -0.5

    def body(*refs):
        if sinks is not None:
            sink_ref, q_ref, kp_ref, kc_ref, vp_ref, vc_ref, o_ref, lse_ref = refs
        else:
            q_ref, kp_ref, kc_ref, vp_ref, vc_ref, o_ref, lse_ref = refs
        i = pl.program_id(2)
        mask1 = _band_mask(i, max_dist, i > 0)
        mask = jnp.concatenate([mask1, mask1], axis=0)
        q = q_ref[0]
        kk = jnp.concatenate([kp_ref[0], kc_ref[0]], axis=0)
        vv = jnp.concatenate([vp_ref[0], vc_ref[0]], axis=0)
        kks, vvs = (_swap_halves(kk), _swap_halves(vv)) if R > 1 else (None, None)
        lo = _low_half((BLOCK, LANES))
        top = lax.broadcasted_iota(jnp.int32, (2 * BLOCK, 1), 0) < BLOCK
        lses, tiles = [], []
        for t in range(n_heads // 2):
            k2 = _kv_operand(kk, kks, 2 * t, n_kv, n_heads)
            v2 = _kv_operand(vv, vvs, 2 * t, n_kv, n_heads)
            s = lax.dot_general(_stack_heads(_pair(q, t)), k2, NT, preferred_element_type=F32) * scale
            s = jnp.where(mask, s, NEG)
            m = jnp.max(s, axis=-1, keepdims=True)
            if sinks is not None:
                sk = jnp.where(top, sink_ref[2 * t], sink_ref[2 * t + 1])
                m = jnp.maximum(m, sk)
            p = jnp.exp(s - m)
            den = jnp.sum(p, axis=-1, keepdims=True)
            if sinks is not None:
                den = den + jnp.exp(sk - m)
            lse2 = m + jnp.log(den)
            o2 = jnp.dot((p / den).astype(BF16), v2, preferred_element_type=F32)
            tiles.append(jnp.where(lo, o2[:BLOCK], o2[BLOCK:]))
            lses += [lse2[:BLOCK], lse2[BLOCK:]]
        o_ref[0] = jnp.concatenate(tiles, axis=-1)
        lse_ref[0] = _lane_place(lses)

    specs = list(_attn_specs(B, S, d, C, n_heads, n_kv, q_col, k_col, v_col))
    args = [qkv3] * 5
    if sinks is not None:
        specs = [pl.BlockSpec(memory_space=pltpu.SMEM)] + specs
        args = [sinks] + args
    o3, lse3 = _hosted(plan, name, lambda comm: _pc(
        body, args, out_shape=(SDS((B, Ls, d * qw), F32), SDS((B, Ls, d * LANES), F32)), grid=(B, d, nb), in_specs=specs,
        out_specs=(pl.BlockSpec((1, BLOCK, qw), lambda b, r, i: (b, i, r)), pl.BlockSpec((1, BLOCK, LANES), lambda b, r, i: (b, i, r))),
        name=name, sem=("parallel", "parallel", "parallel"), comm=comm))
    return o3.reshape(B * S, qw), lse3.reshape(B * S, LANES)


def _attn_dq(qkv, do, lse, delta, cos, sin, B, S, d, *, n_heads, n_kv, q_col, k_col, v_col, max_dist, name, plan=None):
    C = qkv.shape[1]
    Ls = S // d
    nb = Ls // BLOCK
    qw = n_heads * HEAD_DIM
    R = n_heads // n_kv
    scale = HEAD_DIM ** -0.5

    def body(q_ref, kp_ref, kc_ref, vp_ref, vc_ref, do_ref, lse_ref, dl_ref, cos_ref, sin_ref, dq_ref):
        i = pl.program_id(2)
        mask1 = _band_mask(i, max_dist, i > 0)
        mask = jnp.concatenate([mask1, mask1], axis=0)
        q = q_ref[0]
        do_ = do_ref[0]
        kk = jnp.concatenate([kp_ref[0], kc_ref[0]], axis=0)
        vv = jnp.concatenate([vp_ref[0], vc_ref[0]], axis=0)
        kks, vvs = (_swap_halves(kk), _swap_halves(vv)) if R > 1 else (None, None)
        lo = _low_half((BLOCK, LANES))
        lse_t, dl_t = lse_ref[0], dl_ref[0]
        tiles = []
        for t in range(n_heads // 2):
            k2 = _kv_operand(kk, kks, 2 * t, n_kv, n_heads)
            v2 = _kv_operand(vv, vvs, 2 * t, n_kv, n_heads)
            lse2 = jnp.concatenate([lse_t[:, 2 * t:2 * t + 1], lse_t[:, 2 * t + 1:2 * t + 2]], axis=0)
            dl2 = jnp.concatenate([dl_t[:, 2 * t:2 * t + 1], dl_t[:, 2 * t + 1:2 * t + 2]], axis=0)
            s = lax.dot_general(_stack_heads(_pair(q, t)), k2, NT, preferred_element_type=F32) * scale
            p = jnp.where(mask, jnp.exp(s - lse2), 0.0)
            dp = lax.dot_general(_stack_heads(_pair(do_, t)), v2, NT, preferred_element_type=F32)
            ds = p * (dp - dl2)
            dq2 = jnp.dot(ds.astype(BF16), k2, preferred_element_type=F32) * scale
            tiles.append(jnp.where(lo, dq2[:BLOCK], dq2[BLOCK:]))
        dq = jnp.concatenate(tiles, axis=-1)
        dq_ref[0] = _rope(dq, cos_ref[0], sin_ref[0], -1.0).astype(BF16)

    qs, kp, kc, vp, vc = _attn_specs(B, S, d, C, n_heads, n_kv, q_col, k_col, v_col)
    row_q = pl.BlockSpec((1, BLOCK, qw), lambda b, r, i: (b, i, r))
    row_l = pl.BlockSpec((1, BLOCK, LANES), lambda b, r, i: (b, i, r))
    qkv3 = qkv.reshape(B, Ls, d * C)
    v3 = lambda t, w: t.reshape(B, Ls, d * w)
    args = (qkv3, qkv3, qkv3, qkv3, qkv3, v3(do, qw), v3(lse, LANES), v3(delta, LANES), v3(cos, LANES), v3(sin, LANES))
    dq3 = _hosted(plan, name, lambda comm: _pc(
        body, args, out_shape=SDS((B, Ls, d * qw), BF16), grid=(B, d, nb),
        in_specs=[qs, kp, kc, vp, vc, row_q, row_l, row_l, row_l, row_l], out_specs=row_q,
        name=name, sem=("parallel", "parallel", "parallel"), comm=comm))
    return dq3.reshape(B * S, qw)


def _attn_dkv(qkv, do, lse, delta, cos, sin, B, S, d, *, n_heads, n_kv, q_col, k_col, v_col, max_dist, name, plan=None):
    C = qkv.shape[1]
    Ls = S // d
    nb = Ls // BLOCK
    qw = n_heads * HEAD_DIM
    kvw = n_kv * HEAD_DIM
    R = n_heads // n_kv
    scale = HEAD_DIM ** -0.5
    cq, ck = (C // qw if d > 1 else 0), (C // kvw if d > 1 else 0)

    def body(k_ref, v_ref, q0_ref, q1_ref, do0_ref, do1_ref, lse0_ref, lse1_ref, dl0_ref, dl1_ref, cos_ref, sin_ref,
             dk_ref, dv_ref):
        j = pl.program_id(2)
        kj = lax.broadcasted_iota(jnp.int32, (BLOCK, BLOCK), 0)
        qi = lax.broadcasted_iota(jnp.int32, (BLOCK, BLOCK), 1)
        dist0 = qi - kj
        dist1 = qi + BLOCK - kj
        mask0 = (dist0 >= 0) & (dist0 <= max_dist)
        mask1 = (dist1 <= max_dist) & (j + 1 < nb)
        kb, vb = k_ref[0], v_ref[0]
        kbs, vbs = (_swap_halves(kb), _swap_halves(vb)) if R > 1 else (None, None)
        sides = ((q0_ref[0], do0_ref[0], lse0_ref[0].T, dl0_ref[0].T, mask0), (q1_ref[0], do1_ref[0], lse1_ref[0].T, dl1_ref[0].T, mask1))
        n_acc = n_kv if R > 1 else n_kv // 2
        dks = [jnp.zeros((BLOCK, LANES), F32) for _ in range(n_acc)]
        dvs = [jnp.zeros((BLOCK, LANES), F32) for _ in range(n_acc)]
        for t in range(n_heads // 2):
            k2 = _kv_operand(kb, kbs, 2 * t, n_kv, n_heads)
            v2 = _kv_operand(vb, vbs, 2 * t, n_kv, n_heads)
            a = (2 * t) // R if R > 1 else t
            for (q, do_, lse_r, dl_r, mask) in sides:
                q2, do2 = _stack_heads(_pair(q, t)), _stack_heads(_pair(do_, t))
                s = lax.dot_general(k2, q2, NT, preferred_element_type=F32) * scale
                dp = lax.dot_general(v2, do2, NT, preferred_element_type=F32)
                ps, dss = [], []
                for half in (0, 1):
                    h = 2 * t + half
                    sl = slice(BLOCK * half, BLOCK * (half + 1))
                    p = jnp.where(mask, jnp.exp(s[:, sl] - lse_r[h:h + 1, :]), 0.0)
                    ps.append(p)
                    dss.append(p * (dp[:, sl] - dl_r[h:h + 1, :]))
                dvs[a] = dvs[a] + jnp.dot(jnp.concatenate(ps, axis=1).astype(BF16), do2, preferred_element_type=F32)
                dks[a] = dks[a] + jnp.dot(jnp.concatenate(dss, axis=1).astype(BF16), q2, preferred_element_type=F32)
        if R > 1:
            lo = _low_half((BLOCK, LANES))
            fold = lambda x: x + pltpu.roll(x, HEAD_DIM, 1)
            dks = [jnp.where(lo, fold(dks[2 * t]), fold(dks[2 * t + 1])) for t in range(n_kv // 2)]
            dvs = [jnp.where(lo, fold(dvs[2 * t]), fold(dvs[2 * t + 1])) for t in range(n_kv // 2)]
        dk_t = jnp.concatenate(dks, axis=-1) * scale
        dk_ref[0] = _rope(dk_t, cos_ref[0], sin_ref[0], -1.0).astype(BF16)
        dv_ref[0] = jnp.concatenate(dvs, axis=-1).astype(BF16)

    nxt = lambda j: jnp.minimum(j + 1, nb - 1)
    k_spec = pl.BlockSpec((1, BLOCK, kvw), lambda b, r, j: (b, j, r * ck + k_col // kvw))
    v_spec = pl.BlockSpec((1, BLOCK, kvw), lambda b, r, j: (b, j, r * ck + v_col // kvw))
    q0 = pl.BlockSpec((1, BLOCK, qw), lambda b, r, j: (b, j, r * cq + q_col // qw))
    q1 = pl.BlockSpec((1, BLOCK, qw), lambda b, r, j: (b, nxt(j), r * cq + q_col // qw))
    w0 = lambda w: pl.BlockSpec((1, BLOCK, w), lambda b, r, j: (b, j, r))
    w1 = lambda w: pl.BlockSpec((1, BLOCK, w), lambda b, r, j: (b, nxt(j), r))
    qkv3 = qkv.reshape(B, Ls, d * C)
    v3 = lambda t, w: t.reshape(B, Ls, d * w)
    do3, lse3, dl3 = v3(do, qw), v3(lse, LANES), v3(delta, LANES)
    args = (qkv3, qkv3, qkv3, qkv3, do3, do3, lse3, lse3, dl3, dl3, v3(cos, LANES), v3(sin, LANES))
    dk3, dv3 = _hosted(plan, name, lambda comm: _pc(
        body, args, out_shape=(SDS((B, Ls, d * kvw), BF16), SDS((B, Ls, d * kvw), BF16)), grid=(B, d, nb),
        in_specs=[k_spec, v_spec, q0, q1, w0(qw), w1(qw), w0(LANES), w1(LANES), w0(LANES), w1(LANES), w0(LANES), w0(LANES)],
        out_specs=(w0(kvw), w0(kvw)), name=name, sem=("parallel", "parallel", "parallel"), comm=comm))
    return dk3.reshape(B * S, kvw), dv3.reshape(B * S, kvw)


def _head_expand():
    r = lax.broadcasted_iota(jnp.int32, (LANES, C_HEADS * HEAD_DIM), 0)
    c = lax.broadcasted_iota(jnp.int32, (LANES, C_HEADS * HEAD_DIM), 1)
    return jnp.where(c // HEAD_DIM == r, 1.0, 0.0).astype(F32)


def _delta(do, o, lse=None, sinks_row=None, name="delta"):
    T, W = do.shape
    tm = _pick(T, 512, 8)
    with_sink = sinks_row is not None

    def body(*refs):
        if with_sink:
            do_ref, o_ref, lse_ref, sk_ref, dl_ref, dob_ref, ds_ref = refs
        else:
            do_ref, o_ref, dl_ref, dob_ref = refs
        do_ = do_ref[...]
        dl = lax.dot_general(do_ * o_ref[...], _head_expand(), NT, preferred_element_type=F32, precision=HI)
        dl_ref[...] = dl
        dob_ref[...] = do_.astype(BF16)
        if with_sink:
            lane = lax.broadcasted_iota(jnp.int32, dl.shape, 1)
            contrib = jnp.where(lane < A_N_HEADS, -jnp.exp(sk_ref[...] - lse_ref[...]) * dl, 0.0)
            part = jnp.sum(contrib, axis=0, keepdims=True)

            @pl.when(pl.program_id(0) == 0)
            def _():
                ds_ref[...] = part

            @pl.when(pl.program_id(0) > 0)
            def _():
                ds_ref[...] += part

    row_w = pl.BlockSpec((tm, W), lambda i: (i, 0))
    row_l = pl.BlockSpec((tm, LANES), lambda i: (i, 0))
    vec_l = pl.BlockSpec((1, LANES), lambda i: (0, 0))
    if with_sink:
        return pl.pallas_call(
            body, out_shape=(SDS((T, LANES), F32), SDS((T, W), BF16), SDS((1, LANES), F32)), grid=(T // tm,),
            in_specs=[row_w, row_w, row_l, vec_l], out_specs=(row_l, row_w, vec_l), name=name,
            compiler_params=_params(("arbitrary",)),
        )(do, o, lse, sinks_row)
    return pl.pallas_call(
        body, out_shape=(SDS((T, LANES), F32), SDS((T, W), BF16)), grid=(T // tm,),
        in_specs=[row_w, row_w], out_specs=(row_l, row_w), name=name, compiler_params=_params(("parallel",)),
    )(do, o)


def _merge(os_, lses):
    T, W = os_[0].shape
    tm = _pick(T, 512, 8)

    def body(o0, o1, o2, l0, l1, l2, o_ref, lse_ref):
        ls = [l0[...], l1[...], l2[...]]
        m = jnp.maximum(jnp.maximum(ls[0], ls[1]), ls[2])
        ws = [jnp.exp(l - m) for l in ls]
        tot = ws[0] + ws[1] + ws[2]
        lse_ref[...] = m + jnp.log(tot)
        e = _head_expand()
        acc = jnp.zeros((tm, W), F32)
        for w, o in zip(ws, (o0, o1, o2)):
            acc = acc + jnp.dot(w / tot, e, preferred_element_type=F32, precision=HI) * o[...]
        o_ref[...] = acc

    row_w = pl.BlockSpec((tm, W), lambda i: (i, 0))
    row_l = pl.BlockSpec((tm, LANES), lambda i: (i, 0))
    return pl.pallas_call(
        body, out_shape=(SDS((T, W), F32), SDS((T, LANES), F32)), grid=(T // tm,),
        in_specs=[row_w] * 3 + [row_l] * 3, out_specs=(row_w, row_l), name="c_merge", compiler_params=_params(("parallel",)),
    )(*os_, *lses)


CONV_TC = 256


def _conv_pre(x, w, bias):
    row = lax.broadcasted_iota(jnp.int32, x.shape, 0)
    acc = x * w[SSM_CONV - 1:SSM_CONV, :] + bias
    for k in range(1, SSM_CONV):
        acc = acc + jnp.where(row >= k, pltpu.roll(x, k, 0), 0.0) * w[SSM_CONV - 1 - k:SSM_CONV - k, :]
    return acc


def _conv_fwd(zx3, w, bias):
    B, S, _ = zx3.shape
    off = SSM_D_INNER // CONV_TC

    def body(x_ref, w_ref, b_ref, o_ref):
        v = _conv_pre(x_ref[0], w_ref[...], b_ref[...])
        o_ref[0] = v * jax.nn.sigmoid(v)

    return pl.pallas_call(
        body, out_shape=SDS((B, S, SSM_CONV_DIM), F32), grid=(B, SSM_CONV_DIM // CONV_TC),
        in_specs=[pl.BlockSpec((1, S, CONV_TC), lambda b, j: (b, 0, j + off)),
                  pl.BlockSpec((SSM_CONV, CONV_TC), lambda b, j: (0, j)), pl.BlockSpec((1, CONV_TC), lambda b, j: (0, j))],
        out_specs=pl.BlockSpec((1, S, CONV_TC), lambda b, j: (b, 0, j)), name="b_conv_fwd",
        compiler_params=_params(("parallel", "parallel")),
    )(zx3, w, bias)


def _conv_bwd(zx3, dxc, w, bias, col0, name):
    B, S, n = dxc.shape
    tc = _pick(n, CONV_TC)
    off_x = (SSM_D_INNER + col0) // tc
    off_w = col0 // tc

    def body(x_ref, d_ref, w_ref, b_ref, dx_ref, dw_ref, db_ref):
        x = x_ref[0]
        wv = w_ref[...]
        v = _conv_pre(x, wv, b_ref[...])
        sg = jax.nn.sigmoid(v)
        dc = d_ref[0] * (sg * (1.0 + v * (1.0 - sg)))
        row = lax.broadcasted_iota(jnp.int32, x.shape, 0)
        dx = dc * wv[SSM_CONV - 1:SSM_CONV, :]
        dws = [jnp.sum(dc * x, axis=0, keepdims=True)]
        for k in range(1, SSM_CONV):
            dx = dx + jnp.where(row < S - k, pltpu.roll(dc, S - k, 0), 0.0) * wv[SSM_CONV - 1 - k:SSM_CONV - k, :]
            dws.append(jnp.sum(dc * jnp.where(row >= k, pltpu.roll(x, k, 0), 0.0), axis=0, keepdims=True))
        dx_ref[0] = dx.astype(BF16)
        ridx = lax.broadcasted_iota(jnp.int32, (SSM_CONV, tc), 0)
        dw = jnp.zeros((SSM_CONV, tc), F32)
        for k in range(SSM_CONV):
            dw = jnp.where(ridx == SSM_CONV - 1 - k, dws[k], dw)
        db = jnp.sum(dc, axis=0, keepdims=True)

        @pl.when(pl.program_id(1) == 0)
        def _():
            dw_ref[...] = dw
            db_ref[...] = db

        @pl.when(pl.program_id(1) > 0)
        def _():
            dw_ref[...] += dw
            db_ref[...] += db

    return pl.pallas_call(
        body, out_shape=(SDS((B, S, n), BF16), SDS((SSM_CONV, n), F32), SDS((1, n), F32)), grid=(n // tc, B),
        in_specs=[pl.BlockSpec((1, S, tc), lambda j, b: (b, 0, j + off_x)), pl.BlockSpec((1, S, tc), lambda j, b: (b, 0, j)),
                  pl.BlockSpec((SSM_CONV, tc), lambda j, b: (0, j + off_w)), pl.BlockSpec((1, tc), lambda j, b: (0, j + off_w))],
        out_specs=(pl.BlockSpec((1, S, tc), lambda j, b: (b, 0, j)), pl.BlockSpec((SSM_CONV, tc), lambda j, b: (0, j)),
                   pl.BlockSpec((1, tc), lambda j, b: (0, j))),
        name=name, compiler_params=_params(("parallel", "arbitrary")),
    )(zx3, dxc, w, bias)


def _ssd_common(x, Bm, Cm, dtc_raw, dtr_raw, pr, pc):
    Q = SSM_CHUNK
    zc = dtc_raw + pr[0:1, :]
    dt_c = jax.nn.softplus(zc)
    dt_r = jax.nn.softplus(dtr_raw + pc[:, 0:1])
    A_r = -jnp.exp(pr[1:2, :])
    A_c = -jnp.exp(pc[:, 1:2])
    row = lax.broadcasted_iota(jnp.int32, (Q, Q), 0)
    col = lax.broadcasted_iota(jnp.int32, (Q, Q), 1)
    tril = jnp.where(row >= col, 1.0, 0.0).astype(F32)
    cs_c = jnp.dot(tril, dt_c * A_r, preferred_element_type=F32, precision=HI)
    cs_r = lax.dot_general(dt_r * A_c, tril, NT, preferred_element_type=F32, precision=HI)
    return zc, dt_c, A_r, cs_c, cs_r, row, col, tril


def _ssd_fwd(xc3, dtc, dtr, prow, pcol, plan=None):
    B, S, _ = xc3.shape
    Q, G, HG, P, N = SSM_CHUNK, SSM_N_GROUPS, SSM_HG, HEAD_DIM, SSM_D_STATE
    nc = S // Q
    xw = HG * P

    def body(x_ref, b_ref, c_ref, dtc_ref, dtr_ref, pr_ref, pc_ref, y_ref, st_ref, state):
        c = pl.program_id(1)

        @pl.when(c == 0)
        def _():
            state[...] = jnp.zeros_like(state)

        pr = pr_ref[0]
        for bb in range(B):
            x, Bm, Cm = x_ref[bb], b_ref[bb], c_ref[bb]
            _, dt_c, _, cs_c, cs_r, row, col, _ = _ssd_common(x, Bm, Cm, dtc_ref[bb, 0], dtr_ref[bb, 0], pr, pc_ref[0])
            Bb, Cb = Bm.astype(BF16), Cm.astype(BF16)
            CB = lax.dot_general(Cb, Bb, NT, preferred_element_type=F32)
            ys = []
            for hg in range(HG):
                xh = x[:, P * hg:P * (hg + 1)]
                xt = xh * dt_c[:, hg:hg + 1]
                csc, csr = cs_c[:, hg:hg + 1], cs_r[hg:hg + 1, :]
                L = jnp.where(row >= col, jnp.exp(jnp.minimum(csc - csr, 0.0)), 0.0)
                ydiag = jnp.dot((CB * L).astype(BF16), xt.astype(BF16), preferred_element_type=F32)
                Sh = state[bb, hg]
                yoff = lax.dot_general(Cb, Sh.astype(BF16), NT, preferred_element_type=F32) * jnp.exp(csc)
                ys.append(ydiag + yoff + pr[2:3, hg:hg + 1] * xh)
                st_ref[bb, 0, 0, P * hg:P * (hg + 1), :] = Sh
                csq = csc[Q - 1:Q, :]
                upd = lax.dot_general((xt * jnp.exp(csq - csc)).astype(BF16), Bb, TN, preferred_element_type=F32)
                state[bb, hg] = Sh * jnp.exp(csq) + upd
            y_ref[bb] = jnp.concatenate([jnp.concatenate(ys[0:2], axis=-1), jnp.concatenate(ys[2:4], axis=-1)], axis=-1)

    bo, co = SSM_D_INNER // N, (SSM_D_INNER + SSM_BC_DIM) // N
    return _hosted(plan, "b_ssd_fwd", lambda comm: _pc(
        body, (xc3, xc3, xc3, dtc, dtr, prow, pcol),
        out_shape=(SDS((B, S, SSM_D_INNER), F32), SDS((B, G, nc, xw, N), F32)), grid=(G, nc),
        in_specs=[pl.BlockSpec((B, Q, xw), lambda g, c: (0, c, g)), pl.BlockSpec((B, Q, N), lambda g, c: (0, c, bo + g)),
                  pl.BlockSpec((B, Q, N), lambda g, c: (0, c, co + g)), pl.BlockSpec((B, 1, Q, HG), lambda g, c: (0, g, c, 0)),
                  pl.BlockSpec((B, 1, HG, Q), lambda g, c: (0, g, 0, c)), pl.BlockSpec((1, 3, HG), lambda g, c: (g, 0, 0)),
                  pl.BlockSpec((1, HG, 3), lambda g, c: (g, 0, 0))],
        out_specs=(pl.BlockSpec((B, Q, xw), lambda g, c: (0, c, g)), pl.BlockSpec((B, 1, 1, xw, N), lambda g, c: (0, g, c, 0, 0))),
        scratch_shapes=[pltpu.VMEM((B, HG, P, N), F32)], name="b_ssd_fwd", sem=("parallel", "arbitrary"), comm=comm))


def _ssd_bwd(xc3, dtc, dtr, prow, pcol, states, dy3, plan=None):
    B, S, _ = xc3.shape
    Q, G, HG, P, N = SSM_CHUNK, SSM_N_GROUPS, SSM_HG, HEAD_DIM, SSM_D_STATE
    nc = S // Q
    xw = HG * P

    def body(x_ref, b_ref, c_ref, dtc_ref, dtr_ref, pr_ref, pc_ref, st_ref, dy_ref,
             dx_ref, db_ref, dc_ref, ddt_ref, dpar_ref, dstate):
        ci = pl.program_id(1)

        @pl.when(ci == 0)
        def _():
            dstate[...] = jnp.zeros_like(dstate)

        pr = pr_ref[0]
        dpar = one_sequence(0, pr, x_ref, b_ref, c_ref, dtc_ref, dtr_ref, pc_ref, st_ref, dy_ref, dx_ref, db_ref, dc_ref,
                            ddt_ref, dstate)
        for bb in range(1, B):
            dpar = dpar + one_sequence(bb, pr, x_ref, b_ref, c_ref, dtc_ref, dtr_ref, pc_ref, st_ref, dy_ref, dx_ref, db_ref,
                                       dc_ref, ddt_ref, dstate)
        first = ci == 0

        @pl.when(first)
        def _():
            dpar_ref[0] = dpar

        @pl.when(jnp.logical_not(first))
        def _():
            dpar_ref[0] += dpar

    def one_sequence(bb, pr, x_ref, b_ref, c_ref, dtc_ref, dtr_ref, pc_ref, st_ref, dy_ref, dx_ref, db_ref, dc_ref, ddt_ref,
                     dstate):
        x, Bm, Cm, dy = x_ref[bb], b_ref[bb], c_ref[bb], dy_ref[bb]
        zc, dt_c, A_r, cs_c, cs_r, row, col, tril = _ssd_common(x, Bm, Cm, dtc_ref[bb, 0], dtr_ref[bb, 0], pr, pc_ref[0])
        Bb, Cb = Bm.astype(BF16), Cm.astype(BF16)
        CB = lax.dot_general(Cb, Bb, NT, preferred_element_type=F32)
        CBt = lax.dot_general(Bb, Cb, NT, preferred_element_type=F32)
        lane4 = lax.broadcasted_iota(jnp.int32, (Q, HG), 1)
        lane4r = lax.broadcasted_iota(jnp.int32, (1, HG), 1)
        rowq = lax.broadcasted_iota(jnp.int32, (Q, 1), 0)
        dB = jnp.zeros((Q, N), F32)
        dC = jnp.zeros((Q, N), F32)
        dcs4 = jnp.zeros((Q, HG), F32)
        dtx4 = jnp.zeros((Q, HG), F32)
        dD4 = jnp.zeros((1, HG), F32)
        dxts, xhs, dyhs = [], [], []
        for hg in range(HG):
            xh = x[:, P * hg:P * (hg + 1)]
            dyh = dy[:, P * hg:P * (hg + 1)]
            xt = xh * dt_c[:, hg:hg + 1]
            xtb, dyb = xt.astype(BF16), dyh.astype(BF16)
            csc, csr = cs_c[:, hg:hg + 1], cs_r[hg:hg + 1, :]
            L = jnp.where(row >= col, jnp.exp(jnp.minimum(csc - csr, 0.0)), 0.0)
            Lt = jnp.where(col >= row, jnp.exp(jnp.minimum(csr - csc, 0.0)), 0.0)
            M, Mt = CB * L, CBt * Lt
            Sh = st_ref[bb, 0, 0, P * hg:P * (hg + 1), :]
            dSh = dstate[bb, hg]
            Shb, dShb = Sh.astype(BF16), dSh.astype(BF16)
            ecs = jnp.exp(csc)
            csq = csc[Q - 1:Q, :]
            dec = jnp.exp(csq - csc)
            dxt = jnp.dot(Mt.astype(BF16), dyb, preferred_element_type=F32)
            dxt = dxt + lax.dot_general(Bb, dShb, NT, preferred_element_type=F32) * dec
            Gm = lax.dot_general(dyb, xtb, NT, preferred_element_type=F32)
            Gt = lax.dot_general(xtb, dyb, NT, preferred_element_type=F32)
            dC = dC + jnp.dot((Gm * L).astype(BF16), Bb, preferred_element_type=F32)
            dB = dB + jnp.dot((Gt * Lt).astype(BF16), Cb, preferred_element_type=F32)
            dC = dC + jnp.dot(dyb, Shb, preferred_element_type=F32) * ecs
            dBst = jnp.dot(xtb, dShb, preferred_element_type=F32) * dec
            dB = dB + dBst
            dcs = jnp.sum(Gm * M, axis=1, keepdims=True) - jnp.sum(Gt * Mt, axis=1, keepdims=True)
            yoff = lax.dot_general(Cb, Shb, NT, preferred_element_type=F32) * ecs
            dcs = dcs + jnp.sum(yoff * dyh, axis=1, keepdims=True)
            r = jnp.sum(dBst * Bm, axis=1, keepdims=True)
            dcs = dcs - r
            extra = jnp.sum(r, axis=0, keepdims=True) + jnp.exp(csq) * jnp.sum(
                jnp.sum(dSh * Sh, axis=1, keepdims=True), axis=0, keepdims=True)
            dcs = dcs + jnp.where(rowq == Q - 1, extra, 0.0)
            dcs4 = jnp.where(lane4 == hg, dcs, dcs4)
            dtx4 = jnp.where(lane4 == hg, jnp.sum(dxt * xh, axis=1, keepdims=True), dtx4)
            dD4 = jnp.where(lane4r == hg, jnp.sum(jnp.sum(dyh * xh, axis=1, keepdims=True), axis=0, keepdims=True), dD4)
            dstate[bb, hg] = dSh * jnp.exp(csq) + lax.dot_general((dyh * ecs).astype(BF16), Cb, TN, preferred_element_type=F32)
            dxts.append(dxt)
            xhs.append(xh)
            dyhs.append(dyh)
        da4 = lax.dot_general(tril, dcs4, TN, preferred_element_type=F32, precision=HI)
        ddt4 = da4 * A_r + dtx4
        ddtraw = ddt4 * jax.nn.sigmoid(zc)
        ddt_ref[bb, 0] = ddtraw
        dxs = [dxts[hg] * dt_c[:, hg:hg + 1] + pr[2:3, hg:hg + 1] * dyhs[hg] for hg in range(HG)]
        dx_ref[bb] = jnp.concatenate([jnp.concatenate(dxs[0:2], axis=-1), jnp.concatenate(dxs[2:4], axis=-1)], axis=-1)
        db_ref[bb] = dB
        dc_ref[bb] = dC
        d_bias = jnp.sum(ddtraw, axis=0, keepdims=True)
        d_alog = jnp.sum(da4 * dt_c, axis=0, keepdims=True) * A_r
        r3 = lax.broadcasted_iota(jnp.int32, (3, HG), 0)
        return jnp.where(r3 == 0, d_bias, jnp.where(r3 == 1, d_alog, dD4))

    rc = lambda c: nc - 1 - c
    bo, co = SSM_D_INNER // N, (SSM_D_INNER + SSM_BC_DIM) // N
    return _hosted(plan, "b_ssd_bwd", lambda comm: _pc(
        body, (xc3, xc3, xc3, dtc, dtr, prow, pcol, states, dy3),
        out_shape=(SDS((B, S, SSM_D_INNER), F32), SDS((B, S, SSM_BC_DIM), F32), SDS((B, S, SSM_BC_DIM), F32),
                   SDS((B, G, S, HG), F32), SDS((G, 3, HG), F32)),
        grid=(G, nc),
        in_specs=[pl.BlockSpec((B, Q, xw), lambda g, c: (0, rc(c), g)), pl.BlockSpec((B, Q, N), lambda g, c: (0, rc(c), bo + g)),
                  pl.BlockSpec((B, Q, N), lambda g, c: (0, rc(c), co + g)), pl.BlockSpec((B, 1, Q, HG), lambda g, c: (0, g, rc(c), 0)),
                  pl.BlockSpec((B, 1, HG, Q), lambda g, c: (0, g, 0, rc(c))), pl.BlockSpec((1, 3, HG), lambda g, c: (g, 0, 0)),
                  pl.BlockSpec((1, HG, 3), lambda g, c: (g, 0, 0)),
                  pl.BlockSpec((B, 1, 1, xw, N), lambda g, c: (0, g, rc(c), 0, 0)), pl.BlockSpec((B, Q, xw), lambda g, c: (0, rc(c), g))],
        out_specs=(pl.BlockSpec((B, Q, xw), lambda g, c: (0, rc(c), g)), pl.BlockSpec((B, Q, N), lambda g, c: (0, rc(c), g)),
                   pl.BlockSpec((B, Q, N), lambda g, c: (0, rc(c), g)), pl.BlockSpec((B, 1, Q, HG), lambda g, c: (0, g, rc(c), 0)),
                   pl.BlockSpec((1, 3, HG), lambda g, c: (g, 0, 0))),
        scratch_shapes=[pltpu.VMEM((B, HG, P, N), F32)], name="b_ssd_bwd", sem=("parallel", "arbitrary"), comm=comm))


GN_W = SSM_D_INNER // SSM_N_GROUPS


def _gate_fwd(y, zx, nw):
    T = y.shape[0]
    tm = _pick(T, 256, 8)

    def body(y_ref, z_ref, w_ref, o_ref):
        z = z_ref[...]
        gt = y_ref[...] * (z * jax.nn.sigmoid(z))
        outs = []
        for k in range(SSM_N_GROUPS):
            gk = gt[:, GN_W * k:GN_W * (k + 1)]
            outs.append(gk * lax.rsqrt(jnp.mean(gk * gk, axis=-1, keepdims=True) + NORM_EPS))
        o_ref[...] = (jnp.concatenate(outs, axis=-1) * w_ref[...]).astype(BF16)

    row = pl.BlockSpec((tm, SSM_D_INNER), lambda i: (i, 0))
    return pl.pallas_call(
        body, out_shape=SDS((T, SSM_D_INNER), BF16), grid=(T // tm,),
        in_specs=[row, row, pl.BlockSpec((1, SSM_D_INNER), lambda i: (0, 0))], out_specs=row, name="b_gate_fwd",
        compiler_params=_params(("parallel",)),
    )(y, zx, nw)


def _gate_bwd(dgn, y, zx, nw):
    T = y.shape[0]
    tm = _pick(T, 256, 8)

    def body(d_ref, y_ref, z_ref, w_ref, dy_ref, dz_ref, dw_ref):
        z, yv, w = z_ref[...], y_ref[...], w_ref[...]
        sg = jax.nn.sigmoid(z)
        sz = z * sg
        gt = yv * sz
        gw = d_ref[...] * w
        dgts, dws = [], []
        for k in range(SSM_N_GROUPS):
            sl = slice(GN_W * k, GN_W * (k + 1))
            gk, gwk = gt[:, sl], gw[:, sl]
            rstd = lax.rsqrt(jnp.mean(gk * gk, axis=-1, keepdims=True) + NORM_EPS)
            dgts.append(rstd * gwk - gk * (rstd * rstd * rstd) * jnp.mean(gwk * gk, axis=-1, keepdims=True))
            dws.append(jnp.sum(d_ref[:, sl] * gk * rstd, axis=0, keepdims=True))
        dgt = jnp.concatenate(dgts, axis=-1)
        dy_ref[...] = dgt * sz
        dz_ref[...] = (dgt * yv * (sg * (1.0 + z * (1.0 - sg)))).astype(BF16)
        dw = jnp.concatenate(dws, axis=-1)

        @pl.when(pl.program_id(0) == 0)
        def _():
            dw_ref[...] = dw

        @pl.when(pl.program_id(0) > 0)
        def _():
            dw_ref[...] += dw

    row = pl.BlockSpec((tm, SSM_D_INNER), lambda i: (i, 0))
    vec = pl.BlockSpec((1, SSM_D_INNER), lambda i: (0, 0))
    return pl.pallas_call(
        body, out_shape=(SDS((T, SSM_D_INNER), F32), SDS((T, SSM_D_INNER), BF16), SDS((1, SSM_D_INNER), F32)), grid=(T // tm,),
        in_specs=[row, row, row, vec], out_specs=(row, row, vec), name="b_gate_bwd", compiler_params=_params(("arbitrary",)),
    )(dgn, y, zx, nw)


N_CHIPS = 4


def _dev_block(ref, kind, j, size):
    if kind == "slot":
        return ref.at[j]
    start = pl.multiple_of(j * size, size)
    nd = len(ref.shape)
    if kind == "col":
        return ref.at[(slice(None),) * (nd - 1) + (pl.ds(start, size),)]
    return ref.at[(slice(None),) * (nd - 2) + (pl.ds(start, size), slice(None))]


def _dma_sems(n, k):
    return [pltpu.SemaphoreType.DMA((n, k)), pltpu.SemaphoreType.DMA((n, k)), pltpu.SemaphoreType.DMA((n, k))]


def _place(shard, layer, kind, full_shape, dev, name):
    k, n = shard.shape[1:]
    tr = _pick(k, 512, 16)
    nb = k // tr

    def body(dev_ref, s_ref, o_ref):
        if kind == "slot":
            o_ref[0] = s_ref[0].astype(BF16)
        else:
            o_ref[...] = s_ref[0].astype(BF16)

    out_spec = {"slot": pl.BlockSpec((1, tr, n), lambda i, d: (d[0], i, 0)),
                "row": pl.BlockSpec((tr, n), lambda i, d: (d[0] * nb + i, 0)),
                "col": pl.BlockSpec((tr, n), lambda i, d: (i, d[0]))}[kind]
    return pl.pallas_call(
        body, out_shape=SDS(full_shape, BF16),
        grid_spec=pltpu.PrefetchScalarGridSpec(
            num_scalar_prefetch=1, grid=(nb,), in_specs=[pl.BlockSpec((1, tr, n), lambda i, d: (layer, i, 0))], out_specs=out_spec),
        name=name, compiler_params=_params(("arbitrary",)),
    )(dev, shard)


def _run_comm(comm, name):
    c_in = len(comm.inputs)

    def body(*refs):
        cins, couts, sems = refs[:c_in], refs[c_in:c_in + len(comm.out_shapes)], refs[c_in + len(comm.out_shapes):]
        for _, fn in comm.phases:
            fn(cins, couts, sems)

    return pl.pallas_call(
        body, out_shape=list(comm.out_shapes), in_specs=[ANY] * c_in, out_specs=[ANY] * len(comm.out_shapes),
        input_output_aliases=dict(comm.aliases), scratch_shapes=list(comm.sems), name=name,
    )(*comm.inputs)


def _gather_comm(items, mid=0.7):
    n = len(items)

    def tools(srcs, dsts, sems):
        send_sems, recv_sems, local_sems = sems
        px, py, pc = lax.axis_index("x"), lax.axis_index("y"), lax.axis_index("c")
        me, sibling = (px, py, pc), (px, py, 1 - pc)
        chips = [(1 - px, py), (px, 1 - py), (1 - px, 1 - py)]

        def blk(a, dev):
            return _dev_block(dsts[a], items[a][1], 4 * dev[0] + 2 * dev[1] + dev[2], items[a][2])

        def copy(a, k, block, to, src=None):
            return pltpu.make_async_remote_copy(
                src_ref=blk(a, block) if src is None else src, dst_ref=blk(a, block),
                send_sem=send_sems.at[a, k], recv_sem=recv_sems.at[a, k], device_id=to, device_id_type=MESH)

        def mine():
            return [pltpu.make_async_copy(srcs[a], blk(a, me), local_sems.at[a, 0]) for a in range(n) if not items[a][4]]

        def first():
            out = []
            for a in range(n):
                src = blk(a, me) if items[a][4] else srcs[a]
                out.append(copy(a, 0, me, sibling, src=src))
                out += [copy(a, 1 + j, me, (*chip, pc), src=src) for j, chip in enumerate(chips)]
            return out

        def passed():
            return [copy(a, 4 + j, (*chip, pc), sibling) for j, chip in enumerate(chips) for a in range(n)]

        return me, sibling, chips, pc, copy, mine, first, passed

    def start(srcs, dsts, sems):
        *_, mine, first, _ = tools(srcs, dsts, sems)
        for cp in mine() + first():
            cp.start()

    def forward(srcs, dsts, sems):
        me, _, chips, pc, copy, _, _, passed = tools(srcs, dsts, sems)
        fwd = passed()
        for j, chip in enumerate(chips):
            for a in range(n):
                copy(a, 1 + j, (*chip, pc), me).wait_recv()
                fwd[j * n + a].start()

    def finish(srcs, dsts, sems):
        me, sibling, chips, pc, copy, mine, first, passed = tools(srcs, dsts, sems)
        for a in range(n):
            copy(a, 0, sibling, me).wait_recv()
            for j, chip in enumerate(chips):
                copy(a, 4 + j, (*chip, 1 - pc), me).wait_recv()
        for cp in first() + passed():
            cp.wait_send()
        for cp in mine():
            cp.wait()

    return _Comm([it[0] for it in items], [SDS(it[3], it[0].dtype) for it in items],
                 {a: a for a in range(n) if items[a][4]}, _dma_sems(n, 7), [(0.0, start), (mid, forward), (1.0, finish)])


def _gather(items, name):
    return _run_comm(_gather_comm(items), name)


def _reduce_d2d_comm(items):
    n = len(items)

    def copies(gs, gots, sems):
        send_sems, recv_sems, _ = sems
        px, py, pc = lax.axis_index("x"), lax.axis_index("y"), lax.axis_index("c")
        out = []
        for a in range(n):
            _, kind, size, _ = items[a]
            for q in range(N_CHIPS):
                out.append(pltpu.make_async_remote_copy(
                    src_ref=_dev_block(gs[a], kind, 2 * q + 1 - pc, size), dst_ref=gots[a].at[q], send_sem=send_sems.at[a, q],
                    recv_sem=recv_sems.at[a, q], device_id=(px, py, 1 - pc), device_id_type=MESH))
        return out

    def start(gs, gots, sems):
        for cp in copies(gs, gots, sems):
            cp.start()

    def finish(gs, gots, sems):
        for cp in copies(gs, gots, sems):
            cp.wait()

    return _Comm([it[0] for it in items], [SDS((N_CHIPS,) + tuple(it[3]), F32) for it in items], {},
                 _dma_sems(n, N_CHIPS), [(0.0, start), (1.0, finish)])


def _pair_sum(g, got, kind, core, name):
    _, k, n = got.shape
    tr = _pick(k, max(16, STREAM_VMEM // (2 * n * 10)), 16)
    nb = k // tr

    def body(c_ref, g_ref, s_ref, o_ref):
        mine = g_ref[0] if kind == "slot" else g_ref[...]
        o_ref[0] = (mine + s_ref[0]).astype(BF16)

    g_spec = {"slot": pl.BlockSpec((1, tr, n), lambda q, i, c: (2 * q + c[0], i, 0)),
              "row": pl.BlockSpec((tr, n), lambda q, i, c: ((2 * q + c[0]) * nb + i, 0)),
              "col": pl.BlockSpec((tr, n), lambda q, i, c: (i, 2 * q + c[0]))}[kind]
    part = pl.BlockSpec((1, tr, n), lambda q, i, c: (q, i, 0))
    return pl.pallas_call(
        body, out_shape=SDS((N_CHIPS, k, n), BF16),
        grid_spec=pltpu.PrefetchScalarGridSpec(num_scalar_prefetch=1, grid=(N_CHIPS, nb), in_specs=[g_spec, part], out_specs=part),
        name=name, compiler_params=_params(("arbitrary", "arbitrary")),
    )(core, g, got)


def _reduce_ici_comm(parts):
    n = len(parts)

    def copies(ps, rs, sems, arriving):
        send_sems, recv_sems, _ = sems
        px, py, pc = lax.axis_index("x"), lax.axis_index("y"), lax.axis_index("c")
        my_chip = 2 * px + py
        out = []
        for a in range(n):
            for k in range(1, N_CHIPS):
                qx, qy = px ^ (k >> 1), py ^ (k & 1)
                q = 2 * qx + qy
                out.append(pltpu.make_async_remote_copy(
                    src_ref=ps[a].at[q], dst_ref=rs[a].at[q] if arriving else rs[a].at[my_chip], send_sem=send_sems.at[a, k - 1],
                    recv_sem=recv_sems.at[a, k - 1], device_id=(qx, qy, pc), device_id_type=MESH))
        return out

    def start(ps, rs, sems):
        for cp in copies(ps, rs, sems, False):
            cp.start()

    def finish(ps, rs, sems):
        for cp in copies(ps, rs, sems, True):
            cp.wait_recv()
        for cp in copies(ps, rs, sems, False):
            cp.wait_send()

    return _Comm(list(parts), [SDS(p.shape, p.dtype) for p in parts], {}, _dma_sems(n, N_CHIPS - 1),
                 [(0.0, start), (1.0, finish)])


def _adam_update(g, w, m, v):
    c1 = 1.0 - ADAM_B1 ** ADAM_STEP
    c2 = 1.0 - ADAM_B2 ** ADAM_STEP
    nm = ADAM_B1 * m + (1.0 - ADAM_B1) * g
    nv = ADAM_B2 * v + (1.0 - ADAM_B2) * (g * g)
    delta = -ADAM_LR * ((nm / c1) / (jnp.sqrt(nv / c2) + ADAM_EPS) + ADAM_WD * w)
    return delta, nm, nv


def _adamw(parts, recv, w, m, v, layer, prev, chip, name):
    _, R, C = w.shape
    row_bytes = 2 * C * (N_CHIPS * 2 + 7 * 4)
    tr = _pick(R, max(16, STREAM_VMEM // row_bytes), 16)
    n_prev = 0 if prev is None else 4

    def body(ch_ref, own_ref, r1_ref, r2_ref, r3_ref, w_ref, m_ref, v_ref, *rest):
        g_ref, d_ref, nm_ref, nv_ref = rest[n_prev:]
        g = own_ref[0].astype(F32)
        for r_ref in (r1_ref, r2_ref, r3_ref):
            g = g + r_ref[0].astype(F32)
        g_ref[0] = g
        d_ref[0], nm_ref[0], nv_ref[0] = _adam_update(g, w_ref[0], m_ref[0], v_ref[0])

    lay = pl.BlockSpec((1, tr, C), lambda i, ch: (layer, i, 0))
    other = lambda k: pl.BlockSpec((1, tr, C), lambda i, ch: (ch[0] ^ k, i, 0))
    out = SDS(w.shape, F32)
    return pl.pallas_call(
        body, out_shape=(out, out, out, out),
        grid_spec=pltpu.PrefetchScalarGridSpec(
            num_scalar_prefetch=1, grid=(R // tr,),
            in_specs=[pl.BlockSpec((1, tr, C), lambda i, ch: (ch[0], i, 0)), other(2), other(1), other(3), lay, lay, lay]
            + [ANY] * n_prev,
            out_specs=(lay, lay, lay, lay)),
        input_output_aliases={8 + k: k for k in range(n_prev)},
        name=name, compiler_params=_params(("arbitrary",)),
    )(chip, parts, recv, recv, recv, w, m, v, *(prev or ()))


def _small_adamw(gathered, ws, ms, vs):
    n = len(ws)

    def body(*refs):
        g_in, w_in, m_in, v_in = refs[:n], refs[n:2 * n], refs[2 * n:3 * n], refs[3 * n:4 * n]
        outs = refs[4 * n:]
        for i in range(n):
            g = g_in[i][0]
            for dev in range(1, N_DEV):
                g = g + g_in[i][dev]
            d, nm, nv = _adam_update(g, w_in[i][...], m_in[i][...], v_in[i][...])
            outs[i][...] = g
            outs[n + i][...] = d
            outs[2 * n + i][...] = nm
            outs[3 * n + i][...] = nv

    shapes = [SDS(w.shape, F32) for w in ws]
    outs = pl.pallas_call(body, out_shape=shapes * 4, name="small_adamw")(*gathered, *ws, *ms, *vs)
    return outs[:n], outs[n:2 * n], outs[2 * n:3 * n], outs[3 * n:]


W_NAMES = ("norm_mix_w", "norm_mlp_w", "a_w_qkv", "a_b_qkv", "a_sinks", "a_w_o", "a_b_o", "b_in_w", "b_conv_w", "b_conv_b",
           "b_dt_bias", "b_a_log", "b_d", "b_norm_w", "b_out_w", "c_w_qkv", "c_w_o", "mlp_w_up", "mlp_w_down", "final_norm_w")
BIG_KIND = {"a_w_qkv": "slot", "a_w_o": "row", "b_in_w": "slot", "b_out_w": "row", "c_w_qkv": "col", "c_w_o": "row",
            "mlp_w_up": "col", "mlp_w_down": "row"}
SMALL_SHARDED = {"a_b_qkv": 1, "a_b_o": 1, "b_conv_w": 2}
SMALL_REPLICATED = ("norm_mix_w", "norm_mlp_w", "a_sinks", "b_conv_b", "b_dt_bias", "b_a_log", "b_d", "b_norm_w", "final_norm_w")


def _layer_big(i):
    kind, j = i % 3, i // 3
    mix = {0: [("a_w_qkv", j), ("a_w_o", j)], 1: [("b_in_w", 0), ("b_out_w", 0)], 2: [("c_w_qkv", 0), ("c_w_o", 0)]}[kind]
    return mix + [("mlp_w_up", i), ("mlp_w_down", i)]


def _block_size(kind, shard2d):
    return {"slot": None, "row": shard2d[0], "col": shard2d[1]}[kind]


def _full2d(kind, shard2d):
    k, n = shard2d
    return {"slot": (N_DEV, k, n), "row": (N_DEV * k, n), "col": (k, N_DEV * n)}[kind]


def _from_slots(t, ax):
    s = t.shape[1:]
    return jnp.moveaxis(t, 0, ax).reshape(s[:ax] + (N_DEV * s[ax],) + s[ax + 1:])


def _to_slots(g, ax):
    s = g.shape
    return jnp.moveaxis(g.reshape(s[:ax] + (N_DEV, s[ax] // N_DEV) + s[ax + 1:]), ax, 0)


def _rope_tables(positions):
    half = HEAD_DIM // 2
    inv = ROPE_THETA ** (-(jnp.arange(LANES, dtype=jnp.int32) % half).astype(F32) / half)
    ang = positions.astype(F32).reshape(-1, 1) * inv
    return jnp.cos(ang), jnp.sin(ang)


def _swa_fwd(u, h, p, j, B, S, cos, sin, tag, nw, plan=None):
    qkv = _matmul(u, p["a_w_qkv"][j], out_dtype=BF16, bias=p["a_b_qkv"][j][None], rope=(cos, sin),
                  rope_cols=A_Q_DIM + A_KV_DIM, name=f"{tag}_qkv", plan=plan)
    o, lse = _attn_fwd(qkv, B, S, 1, n_heads=A_N_HEADS, n_kv=A_N_KV, q_col=0, k_col=A_Q_DIM, v_col=A_Q_DIM + A_KV_DIM,
                       max_dist=A_WINDOW - 1, sinks=p["a_sinks"][j], name=f"{tag}_attn", plan=plan)
    h1, u2 = _matmul(o, p["a_w_o"][j], bias=p["a_b_o"][j][None], resid=h, norm_out=nw, name=f"{tag}_o")
    return h1, u2, (qkv, o, lse)


def _swa_bwd(dh1, u, saved, p, j, B, S, cos, sin, tag, norm, plan=None):
    qkv, o, lse = saved
    kw = dict(n_heads=A_N_HEADS, n_kv=A_N_KV, q_col=0, k_col=A_Q_DIM, v_col=A_Q_DIM + A_KV_DIM, max_dist=A_WINDOW - 1)
    g = {}
    do = _matmul(dh1, p["a_w_o"][j], tb=True, name=f"{tag}_do")
    g["a_w_o"] = _matmul(o, dh1, ta=True, name=f"{tag}_dwo")
    g["a_b_o"] = _colsum(dh1, f"{tag}_dbo")[0]
    sk = jnp.pad(p["a_sinks"][j], (0, LANES - A_N_HEADS))[None]
    delta, dob, dsink = _delta(do, o, lse, sk, name=f"{tag}_delta")
    g["a_sinks"] = dsink[0, :A_N_HEADS]
    dq = _attn_dq(qkv, dob, lse, delta, cos, sin, B, S, 1, name=f"{tag}_dq", plan=plan, **kw)
    dk, dv = _attn_dkv(qkv, dob, lse, delta, cos, sin, B, S, 1, name=f"{tag}_dkv", plan=plan, **kw)
    dqkv = jnp.concatenate([dq, dk, dv], axis=1)
    g["a_w_qkv"] = _matmul(u, dqkv, ta=True, name=f"{tag}_dwqkv")
    g["a_b_qkv"] = _colsum(dqkv, f"{tag}_dbqkv")[0]
    if plan is not None:
        plan.grads(3 * j, "mix", {"a_w_qkv": g["a_w_qkv"], "a_w_o": g["a_w_o"]})
    dh, dnw = _matmul(dqkv, p["a_w_qkv"][j], tb=True, norm_bwd=(norm[0], norm[1], dh1), name=f"{tag}_du", plan=plan)
    return dh, dnw, g


def _group_cols(gi, qkv):
    W = C_HEADS * HEAD_DIM
    if C_PATTERNS[gi][1] == 1:
        return qkv, (gi * W, (3 + gi) * W, (6 + gi) * W)
    part = jnp.concatenate([qkv[:, (3 * j + gi) * W:(3 * j + gi + 1) * W] for j in range(3)], axis=1)
    return part, (0, W, 2 * W)


def _dil_fwd(u, h, p, B, S, cos, sin, nw, plan=None):
    W = C_HEADS * HEAD_DIM
    qkv = _matmul(u, p["c_w_qkv"][0], out_dtype=BF16, rope=(cos, sin), rope_cols=6 * W, name="c_qkv", plan=plan)
    os_, lses, parts = [], [], []
    for gi, (window, dil) in enumerate(C_PATTERNS):
        part, (qc, kc, vc) = _group_cols(gi, qkv)
        o, lse = _attn_fwd(part, B, S, dil, n_heads=C_HEADS, n_kv=C_HEADS, q_col=qc, k_col=kc, v_col=vc,
                           max_dist=window // dil, sinks=None, name=f"c_attn{gi}", plan=plan)
        os_.append(o)
        lses.append(lse)
        parts.append((part, (qc, kc, vc)))
    o, lse = _merge(os_, lses)
    h1, u2 = _matmul(o, p["c_w_o"][0], resid=h, norm_out=nw, name="c_o")
    return h1, u2, (parts, o, lse)


def _dil_bwd(dh1, u, saved, p, B, S, cos, sin, norm, plan=None):
    parts, o, lse = saved
    g = {}
    do = _matmul(dh1, p["c_w_o"][0], tb=True, name="c_do")
    g["c_w_o"] = _matmul(o, dh1, ta=True, name="c_dwo")[None]
    delta, dob = _delta(do, o, name="c_delta")
    dqs, dks, dvs = [], [], []
    for gi, (window, dil) in enumerate(C_PATTERNS):
        part, (qc, kc, vc) = parts[gi]
        kw = dict(n_heads=C_HEADS, n_kv=C_HEADS, q_col=qc, k_col=kc, v_col=vc, max_dist=window // dil)
        dqs.append(_attn_dq(part, dob, lse, delta, cos, sin, B, S, dil, name=f"c_dq{gi}", **kw))
        dk, dv = _attn_dkv(part, dob, lse, delta, cos, sin, B, S, dil, name=f"c_dkv{gi}", **kw)
        dks.append(dk)
        dvs.append(dv)
    dqkv = jnp.concatenate(dqs + dks + dvs, axis=1)
    g["c_w_qkv"] = _matmul(u, dqkv, ta=True, name="c_dwqkv", plan=plan)[None]
    if plan is not None:
        plan.grads(2, "mix", {"c_w_qkv": g["c_w_qkv"][0], "c_w_o": g["c_w_o"][0]})
    dh, dnw = _matmul(dqkv, p["c_w_qkv"][0], tb=True, norm_bwd=(norm[0], norm[1], dh1), name="c_du", plan=plan)
    return dh, dnw, g


def _ssm_params(p):
    par = jnp.stack([p["b_dt_bias"][0], p["b_a_log"][0], p["b_d"][0]], axis=0)
    prow = par.reshape(3, SSM_N_GROUPS, SSM_HG).transpose(1, 0, 2)
    return prow, prow.transpose(0, 2, 1)


def _mamba_fwd(u, h, p, B, S, nw, plan=None):
    T = B * S
    G, HG = SSM_N_GROUPS, SSM_HG
    w_in = p["b_in_w"][0]
    nzx = SSM_D_INNER + SSM_CONV_DIM
    w_dt = jnp.pad(w_in[:, nzx:], ((0, 0), (0, LANES - SSM_N_HEADS)))
    zx = _matmul(u, w_in[:, :nzx], name="b_zx", plan=plan)
    dtraw = _matmul(u, w_dt, name="b_dt")[:, :SSM_N_HEADS]
    dtc = dtraw.reshape(B, S, G, HG).transpose(0, 2, 1, 3)
    dtr = dtraw.reshape(B, S, G, HG).transpose(0, 2, 3, 1)
    prow, pcol = _ssm_params(p)
    zx3 = zx.reshape(B, S, nzx)
    xc3 = _conv_fwd(zx3, p["b_conv_w"][0], p["b_conv_b"])
    y3, states = _ssd_fwd(xc3, dtc, dtr, prow, pcol, plan=plan)
    y = y3.reshape(T, SSM_D_INNER)
    gn = _gate_fwd(y, zx, p["b_norm_w"])
    h1, u2 = _matmul(gn, p["b_out_w"][0], resid=h, norm_out=nw, name="b_out", plan=plan)
    return h1, u2, (zx, dtc, dtr, xc3, y, states, gn, w_dt)


def _mamba_bwd(dh1, u, saved, p, B, S, norm, plan=None):
    T = B * S
    zx, dtc, dtr, xc3, y, states, gn, w_dt = saved
    nzx = SSM_D_INNER + SSM_CONV_DIM
    w_in = p["b_in_w"][0]
    prow, pcol = _ssm_params(p)
    g = {}
    dgn = _matmul(dh1, p["b_out_w"][0], tb=True, name="b_dgn")
    g["b_out_w"] = _matmul(gn, dh1, ta=True, name="b_dwout")[None]
    dy, dz, dnw = _gate_bwd(dgn, y, zx, p["b_norm_w"])
    g["b_norm_w"] = dnw
    dx3, dB3, dC3, ddt, dpar = _ssd_bwd(xc3, dtc, dtr, prow, pcol, states, dy.reshape(B, S, SSM_D_INNER), plan=plan)
    dpar = dpar.transpose(1, 0, 2).reshape(3, SSM_N_HEADS)
    g["b_dt_bias"], g["b_a_log"], g["b_d"] = dpar[0:1], dpar[1:2], dpar[2:3]
    zx3 = zx.reshape(B, S, nzx)
    cw, cb = p["b_conv_w"][0], p["b_conv_b"]
    parts, dws, dbs = [], [], []
    for col0, dpart, nm in ((0, dx3, "b_conv_bwd_x"), (SSM_D_INNER, dB3, "b_conv_bwd_b"),
                            (SSM_D_INNER + SSM_BC_DIM, dC3, "b_conv_bwd_c")):
        dxp, dw, db = _conv_bwd(zx3, dpart, cw, cb, col0, nm)
        parts.append(dxp.reshape(T, -1))
        dws.append(dw)
        dbs.append(db)
    g["b_conv_w"] = jnp.concatenate(dws, axis=1)[None]
    g["b_conv_b"] = jnp.concatenate(dbs, axis=1)
    dzx = jnp.concatenate([dz] + parts, axis=1)
    ddtraw = ddt.transpose(0, 2, 1, 3).reshape(T, SSM_N_HEADS)
    ddtp = jnp.pad(ddtraw, ((0, 0), (0, LANES - SSM_N_HEADS)))
    dw_zx = _matmul(u, dzx, ta=True, name="b_dwzx")
    dw_dt = _matmul(u, ddtp, ta=True, name="b_dwdt")[:, :SSM_N_HEADS]
    g["b_in_w"] = jnp.concatenate([dw_zx, dw_dt], axis=1)[None]
    if plan is not None:
        plan.grads(1, "mix", {"b_in_w": g["b_in_w"][0], "b_out_w": g["b_out_w"][0]})
    du = _matmul(dzx, w_in[:, :nzx], tb=True, name="b_du_zx", plan=plan)
    dh, dnw = _matmul(ddtp, w_dt, tb=True, resid=du, norm_bwd=(norm[0], norm[1], dh1), name="b_du_dt")
    return dh, dnw, g


def _local_step(x, positions, p, target, plan=None):
    B, S, D = x.shape
    T = B * S
    cos, sin = _rope_tables(positions)
    h = x.reshape(T, D)
    tape = []
    u = _rmsnorm_fwd(h, p["norm_mix_w"][0], "l0_norm_mix")
    for i in range(DEPTH):
        kind, j = i % 3, i // 3
        nw = p["norm_mlp_w"][i]
        if kind == 0:
            h1, u2, saved = _swa_fwd(u, h, p, j, B, S, cos, sin, f"a{j}", nw, plan)
        elif kind == 1:
            h1, u2, saved = _mamba_fwd(u, h, p, B, S, nw, plan)
        else:
            h1, u2, saved = _dil_fwd(u, h, p, B, S, cos, sin, nw, plan)
        r, s = _matmul(u2, p["mlp_w_up"][i], out_dtype=BF16, relu2=True, name=f"l{i}_up", plan=plan)
        if i + 1 < DEPTH:
            h2, u_next = _matmul(s, p["mlp_w_down"][i], resid=h1, norm_out=p["norm_mix_w"][i + 1], name=f"l{i}_down", plan=plan)
        else:
            h2, u_next = _matmul(s, p["mlp_w_down"][i], resid=h1, name=f"l{i}_down", plan=plan), None
        tape.append((h, u, saved, h1, u2, r, s))
        h, u = h2, u_next
    dh, dwf, loss = _final_loss(h, target.reshape(T, D), p["final_norm_w"])
    grads = {"final_norm_w": dwf[0]}
    per_layer = {n: [None] * DEPTH for n in ("norm_mix_w", "norm_mlp_w", "mlp_w_up", "mlp_w_down")}
    a_grads = [None, None]
    for i in reversed(range(DEPTH)):
        kind, j = i % 3, i // 3
        h0, u, saved, h1, u2, r, s = tape[i]
        da = _matmul(dh, p["mlp_w_down"][i], tb=True, out_dtype=BF16, mul=r, mul_scale=2.0, name=f"l{i}_da", plan=plan)
        per_layer["mlp_w_down"][i] = _matmul(s, dh, ta=True, name=f"l{i}_dwdown")
        per_layer["mlp_w_up"][i] = _matmul(u2, da, ta=True, name=f"l{i}_dwup")
        if plan is not None:
            plan.grads(i, "mlp", {"mlp_w_up": per_layer["mlp_w_up"][i], "mlp_w_down": per_layer["mlp_w_down"][i]})
        dh1, dnw = _matmul(da, p["mlp_w_up"][i], tb=True, norm_bwd=(h1, p["norm_mlp_w"][i], dh), name=f"l{i}_du2", plan=plan)
        per_layer["norm_mlp_w"][i] = dnw[0]
        norm = (h0, p["norm_mix_w"][i])
        if kind == 0:
            dh, dnw, g = _swa_bwd(dh1, u, saved, p, j, B, S, cos, sin, f"a{j}", norm, plan)
            a_grads[j] = g
        elif kind == 1:
            dh, dnw, g = _mamba_bwd(dh1, u, saved, p, B, S, norm, plan)
            grads.update(g)
        else:
            dh, dnw, g = _dil_bwd(dh1, u, saved, p, B, S, cos, sin, norm, plan)
            grads.update(g)
        per_layer["norm_mix_w"][i] = dnw[0]
    for n in ("norm_mix_w", "norm_mlp_w"):
        grads[n] = jnp.stack(per_layer[n], axis=0)
    for n in ("mlp_w_up", "mlp_w_down"):
        grads[n] = per_layer[n]
    for n in ("a_b_qkv", "a_sinks", "a_b_o"):
        grads[n] = jnp.stack([a_grads[0][n], a_grads[1][n]], axis=0)
    for n in ("a_w_qkv", "a_w_o"):
        grads[n] = [a_grads[0][n], a_grads[1][n]]
    for n in ("b_in_w", "b_out_w", "c_w_qkv", "c_w_o"):
        grads[n] = [grads[n][0]]
    return loss, dh.reshape(B, S, D), grads


MIX = {0: ("a_w_qkv", "a_w_o"), 1: ("b_in_w", "b_out_w"), 2: ("c_w_qkv", "c_w_o")}
MLP = ("mlp_w_up", "mlp_w_down")
GATHER_FIRST = (0, MIX[0])
GATHER_HOSTS = {"a0_qkv": ((0, ("mlp_w_up",)),), "a0_attn": ((0, ("mlp_w_down",)),), "l0_up": ((1, ("b_in_w",)),),
                "l0_down": ((1, ("b_out_w",)),), "b_zx": ((1, ("mlp_w_up",)),),
                "b_ssd_fwd": ((1, ("mlp_w_down",)), (2, ("c_w_qkv",))), "b_out": ((2, ("c_w_o",)),),
                "l1_up": ((2, ("mlp_w_up",)),), "l1_down": ((2, ("mlp_w_down",)),),
                "c_qkv": ((3, MLP),), "c_attn1": ((3, MIX[0]),)}
REDUCE_HOSTS = {(3, "mlp"): ("l3_du2", "a1_dkv"), (3, "mix"): ("a1_du", "l2_da"),
                (2, "mlp"): ("l2_du2", "c_dwqkv"), (2, "mix"): ("c_du", "b_ssd_bwd"),
                (1, "mlp"): ("l1_du2", "b_ssd_bwd"), (1, "mix"): ("b_du_zx", {"b_in_w": "a0_dq", "b_out_w": "l0_da"}),
                (0, "mlp"): ("l0_du2", "a0_dkv"), (0, "mix"): (None, None)}


class _Plan:
    def __init__(self, w, m, v, p, dev, chip, core):
        self.w, self.m, self.v, self.p, self.dev, self.chip, self.core = w, m, v, p, dev, chip, core
        self.pending = {}
        self.res = {n: None for n in BIG_KIND}
        self._install(*GATHER_FIRST)(_gather(self._gather_items(*GATHER_FIRST), "gather_first"))
        for host, groups in GATHER_HOSTS.items():
            for i, only in groups:
                self._wait_for(host, _gather_comm(self._gather_items(i, only)), self._install(i, only))

    def _wait_for(self, host, comm, done):
        self.pending.setdefault(host, []).append((comm, done))

    def _names(self, i, only):
        return [(n, l) for n, l in _layer_big(i) if only is None or n in only]

    def _gather_items(self, i, only):
        items = []
        for n, l in self._names(i, only):
            kind, s2 = BIG_KIND[n], self.w[n].shape[1:]
            placed = _place(self.w[n], l, kind, _full2d(kind, s2), self.dev, f"place_l{i}_{n}")
            items.append((placed, kind, _block_size(kind, s2), _full2d(kind, s2), True))
        return items

    def _install(self, i, only):
        def done(fulls):
            for (n, l), t in zip(self._names(i, only), fulls):
                self.p[n][l] = _from_slots(t, 1) if BIG_KIND[n] == "slot" else t
        return done

    def take(self, host):
        return _Comm.merge([c for c, _ in self.pending[host]]) if host in self.pending else None

    def give(self, host, results):
        for comm, done in self.pending.pop(host):
            done(results[:len(comm.out_shapes)])
            results = results[len(comm.out_shapes):]

    def grads(self, i, group, grads):
        names = self._names(i, MLP if group == "mlp" else MIX[i % 3])
        items = []
        for n, _ in names:
            kind, s2 = BIG_KIND[n], self.w[n].shape[1:]
            items.append((_to_slots(grads[n], 1) if kind == "slot" else grads[n], kind, _block_size(kind, s2), s2))
        d2d_host, ici_host = REDUCE_HOSTS[(i, group)]
        tag = f"l{i}_{group}"

        def update(sel, parts):
            def done(recv):
                for (n, l), pt, r in zip(sel, parts, recv):
                    self.res[n] = _adamw(pt, r, self.w[n], self.m[n], self.v[n], l, self.res[n], self.chip, f"adamw_l{i}_{n}")
            return done

        def second(sib):
            parts = [_pair_sum(it[0], s, it[1], self.core, f"pair_sum_l{i}_{n}") for (n, _), it, s in zip(names, items, sib)]
            hosts = ici_host if isinstance(ici_host, dict) else {n: ici_host for n, _ in names}
            for h in dict.fromkeys(hosts[n] for n, _ in names):
                ks = [k for k, (n, _) in enumerate(names) if hosts[n] == h]
                sel, pts = [names[k] for k in ks], [parts[k] for k in ks]
                self._send(h, _reduce_ici_comm(pts), update(sel, pts), f"reduce_ici_{tag}_{sel[0][0]}")

        self._send(d2d_host, _reduce_d2d_comm(items), second, f"reduce_d2d_{tag}")

    def _send(self, host, comm, done, name):
        if host is None:
            done(_run_comm(comm, name))
        else:
            self._wait_for(host, comm, done)

    def flush(self):
        late = 0
        while self.pending:
            host = next(iter(self.pending))
            for comm, done in self.pending.pop(host):
                done(_run_comm(comm, f"late_{late}_{host}"))
                late += 1


def kernel(x, positions, norm_mix_w, norm_mlp_w, a_w_qkv, a_b_qkv, a_sinks, a_w_o, a_b_o, b_in_w, b_conv_w, b_conv_b, b_dt_bias, b_a_log, b_d, b_norm_w, b_out_w, c_w_qkv, c_w_o, mlp_w_up, mlp_w_down, final_norm_w, loss_target, m_norm_mix_w, m_norm_mlp_w, m_a_w_qkv, m_a_b_qkv, m_a_sinks, m_a_w_o, m_a_b_o, m_b_in_w, m_b_conv_w, m_b_conv_b, m_b_dt_bias, m_b_a_log, m_b_d, m_b_norm_w, m_b_out_w, m_c_w_qkv, m_c_w_o, m_mlp_w_up, m_mlp_w_down, m_final_norm_w, v_norm_mix_w, v_norm_mlp_w, v_a_w_qkv, v_a_b_qkv, v_a_sinks, v_a_w_o, v_a_b_o, v_b_in_w, v_b_conv_w, v_b_conv_b, v_b_dt_bias, v_b_a_log, v_b_d, v_b_norm_w, v_b_out_w, v_c_w_qkv, v_c_w_o, v_mlp_w_up, v_mlp_w_down, v_final_norm_w):
    w = dict(zip(W_NAMES, (norm_mix_w, norm_mlp_w, a_w_qkv, a_b_qkv, a_sinks, a_w_o, a_b_o, b_in_w, b_conv_w, b_conv_b,
                           b_dt_bias, b_a_log, b_d, b_norm_w, b_out_w, c_w_qkv, c_w_o, mlp_w_up, mlp_w_down, final_norm_w)))
    m = dict(zip(W_NAMES, (m_norm_mix_w, m_norm_mlp_w, m_a_w_qkv, m_a_b_qkv, m_a_sinks, m_a_w_o, m_a_b_o, m_b_in_w,
                           m_b_conv_w, m_b_conv_b, m_b_dt_bias, m_b_a_log, m_b_d, m_b_norm_w, m_b_out_w, m_c_w_qkv, m_c_w_o,
                           m_mlp_w_up, m_mlp_w_down, m_final_norm_w)))
    v = dict(zip(W_NAMES, (v_norm_mix_w, v_norm_mlp_w, v_a_w_qkv, v_a_b_qkv, v_a_sinks, v_a_w_o, v_a_b_o, v_b_in_w,
                           v_b_conv_w, v_b_conv_b, v_b_dt_bias, v_b_a_log, v_b_d, v_b_norm_w, v_b_out_w, v_c_w_qkv, v_c_w_o,
                           v_mlp_w_up, v_mlp_w_down, v_final_norm_w)))
    px, py, pc = lax.axis_index("x"), lax.axis_index("y"), lax.axis_index("c")
    me = 4 * px + 2 * py + pc
    dev, chip, core = (t.astype(jnp.int32).reshape(1) for t in (me, 2 * px + py, pc))

    trio = tuple(SMALL_SHARDED)
    got = _gather([(d[n], "slot", None, (N_DEV,) + d[n].shape, False) for n in trio for d in (w, m, v)], "gather_small")
    slots = {n: got[3 * i:3 * i + 3] for i, n in enumerate(trio)}
    p = {n: w[n] for n in SMALL_REPLICATED}
    for n in trio:
        p[n] = _from_slots(slots[n][0], SMALL_SHARDED[n])
    for n in BIG_KIND:
        p[n] = [None] * w[n].shape[0]
    plan = _Plan(w, m, v, p, dev, chip, core)
    loss_part, dx, grads = _local_step(x, positions, p, loss_target, plan)
    loss = lax.psum(loss_part[0, 0], AXES)
    plan.flush()
    out = {n: list(plan.res[n]) for n in BIG_KIND}

    small = SMALL_REPLICATED + trio
    as2d = lambda t: t.reshape(1, -1) if t.ndim == 1 else t
    g_sm = [as2d(grads[n]) for n in SMALL_REPLICATED] + [_to_slots(grads[n].reshape(p[n].shape), SMALL_SHARDED[n]) for n in trio]
    gathered = _gather([(g, "slot", None, (N_DEV,) + g.shape, False) for g in g_sm], "gather_small_grads")
    ws = [as2d(w[n]) for n in SMALL_REPLICATED] + [slots[n][0] for n in trio]
    ms = [as2d(m[n]) for n in SMALL_REPLICATED] + [slots[n][1] for n in trio]
    vs = [as2d(v[n]) for n in SMALL_REPLICATED] + [slots[n][2] for n in trio]
    sm_out = _small_adamw(gathered, ws, ms, vs)
    for i, n in enumerate(small):
        if n in SMALL_SHARDED:
            out[n] = [lax.dynamic_index_in_dim(sm_out[k][i], me, 0, keepdims=False) for k in range(4)]
        else:
            out[n] = [sm_out[k][i].reshape(w[n].shape) for k in range(4)]
    return (loss, dx, *[out[n][0] for n in W_NAMES], *[out[n][1] for n in W_NAMES], *[out[n][2] for n in W_NAMES],
            *[out[n][3] for n in W_NAMES])
```

```python
import math

import jax
import jax.numpy as jnp
from jax import lax
from jax.experimental import pallas as pl
from jax.experimental.pallas import tpu as pltpu

F32 = jnp.float32
BF16 = jnp.bfloat16
SDS = jax.ShapeDtypeStruct

D_MODEL = 1024
DEPTH = 4
BLOCK = 128
ROPE_THETA = 10000.0
NORM_EPS = 1e-5
HEAD_DIM = 64
A_N_HEADS = 16
A_N_KV = 2
A_WINDOW = 128
A_Q_DIM = 1024
A_KV_DIM = 128
SSM_D_INNER = 2048
SSM_N_HEADS = 32
SSM_N_GROUPS = 8
SSM_HG = 4
SSM_D_STATE = 128
SSM_CONV = 4
SSM_CHUNK = 128
SSM_BC_DIM = 1024
SSM_CONV_DIM = 4096
C_PATTERNS = ((128, 1), (512, 4), (2048, 16))
C_HEADS = 16
ADAM_LR, ADAM_B1, ADAM_B2, ADAM_EPS, ADAM_WD, ADAM_STEP = 0.001, 0.9, 0.999, 1e-08, 0.01, 10

N_DEV = 8
AXES = ("x", "y", "c")
LANES = 128
VMEM_LIMIT = 56 * 1024 * 1024
STREAM_VMEM = 16 * 1024 * 1024
NEG = -1e30

NN = (((1,), (0,)), ((), ()))
NT = (((1,), (1,)), ((), ()))
TN = (((0,), (0,)), ((), ()))
HI = lax.Precision.HIGHEST


def _pick(n, cap, mult=LANES):
    best = None
    for t in range(mult, min(n, cap) + 1, mult):
        if n % t == 0:
            best = t
    return best if best is not None else n


def _params(sem):
    return pltpu.CompilerParams(dimension_semantics=sem, vmem_limit_bytes=VMEM_LIMIT)


def _bf(x):
    return x if x.dtype == BF16 else x.astype(BF16)


def _rot_half(y):
    n = y.shape[-1]
    lane = lax.broadcasted_iota(jnp.int32, y.shape, y.ndim - 1)
    return jnp.where((lane % HEAD_DIM) < HEAD_DIM // 2, -pltpu.roll(y, n - 32, y.ndim - 1), pltpu.roll(y, 32, y.ndim - 1))


def _rope(y, cos, sin, sign):
    reps = y.shape[-1] // LANES
    c = jnp.tile(cos, (1, reps)) if reps > 1 else cos
    s = jnp.tile(sin, (1, reps)) if reps > 1 else sin
    return y * c + sign * (_rot_half(y) * s)


MESH = pl.DeviceIdType.MESH
ANY = pl.BlockSpec(memory_space=pl.ANY)


class _Comm:
    def __init__(self, inputs, out_shapes, aliases, sems, phases):
        self.inputs, self.out_shapes, self.aliases, self.sems, self.phases = inputs, out_shapes, aliases, sems, phases

    @staticmethod
    def merge(comms):
        if len(comms) == 1:
            return comms[0]
        ins, outs, aliases, sems, spans = [], [], {}, [], []
        for c in comms:
            aliases.update({len(ins) + i: len(outs) + j for i, j in c.aliases.items()})
            spans.append((len(ins), len(ins) + len(c.inputs), len(outs), len(outs) + len(c.out_shapes), len(sems),
                          len(sems) + len(c.sems)))
            ins, outs, sems = ins + list(c.inputs), outs + list(c.out_shapes), sems + list(c.sems)
        phases = []
        for f in sorted({f for c in comms for f, _ in c.phases}):
            todo = [(fn, sp) for c, sp in zip(comms, spans) for g, fn in c.phases if g == f]

            def run(cins, couts, csems, todo=todo):
                for fn, (i0, i1, o0, o1, s0, s1) in todo:
                    fn(cins[i0:i1], couts[o0:o1], csems[s0:s1])
            phases.append((f, run))
        return _Comm(ins, outs, aliases, sems, phases)


def _pc(body, args, *, out_shape, grid, in_specs, out_specs, name, sem, scratch_shapes=(), comm=None):
    single = not isinstance(out_shape, (tuple, list))
    outs, ospecs = ([out_shape], [out_specs]) if single else (list(out_shape), list(out_specs))
    unpack = (lambda r: r[0]) if single else (lambda r: tuple(r))
    if comm is None:
        res = pl.pallas_call(body, out_shape=outs, grid=grid, in_specs=list(in_specs), out_specs=ospecs,
                             scratch_shapes=list(scratch_shapes), name=name, compiler_params=_params(sem))(*args)
        return unpack(res)
    n_in, n_out, n_scr = len(in_specs), len(outs), len(scratch_shapes)
    c_in, c_out = len(comm.inputs), len(comm.out_shapes)
    total = math.prod(grid)
    steps = [min(total - 1, int(f * total)) for f, _ in comm.phases[:-1]]

    def wrapped(*refs):
        ins, cins = refs[:n_in], refs[n_in:n_in + c_in]
        o = refs[n_in + c_in:n_in + c_in + n_out]
        couts = refs[n_in + c_in + n_out:n_in + c_in + n_out + c_out]
        rest = refs[n_in + c_in + n_out + c_out:]
        scr, csems = rest[:n_scr], rest[n_scr:]
        step = pl.program_id(0)
        for ax in range(1, len(grid)):
            step = step * grid[ax] + pl.program_id(ax)
        for (_, fn), st in zip(comm.phases[:-1], steps):
            @pl.when(step == st)
            def _(fn=fn):
                fn(cins, couts, csems)
        body(*ins, *o, *scr)

        @pl.when(step == total - 1)
        def _():
            comm.phases[-1][1](cins, couts, csems)

    res = pl.pallas_call(
        wrapped, out_shape=outs + list(comm.out_shapes), grid=grid, in_specs=list(in_specs) + [ANY] * c_in,
        out_specs=ospecs + [ANY] * c_out, scratch_shapes=list(scratch_shapes) + list(comm.sems),
        input_output_aliases={n_in + i: n_out + j for i, j in comm.aliases.items()}, name=name,
        compiler_params=_params(("arbitrary",) * len(grid)),
    )(*args, *comm.inputs)
    return unpack(res[:n_out]), list(res[n_out:])


def _hosted(plan, name, run):
    comm = plan.take(name) if plan is not None else None
    if comm is None:
        return run(None)
    res, extra = run(comm)
    plan.give(name, extra)
    return res


MM_VMEM = 40 * 1024 * 1024
HBM_BYTES_PER_US = 2.5e6
STEP_US = 0.35


def _divisors(n, cands):
    return [c for c in cands if c <= n and n % c == 0] or [n]


def _mm_tiles(M, N, K, sa, sb, out_bytes, extra_bytes, full_rows=False):
    best = None
    for tm in _divisors(M, (2048, 1024, 512, 256)):
        for tn in ([N] if full_rows else _divisors(N, (1024, 640, 512, 256, 128))):
            for tk in _divisors(K, (K, K // 2, K // 3, K // 4, 2048, 1024, 640, 512)):
                if tk != K and tk % LANES:
                    continue
                nk = K // tk
                vmem = 2 * tm * tk * sa + 2 * tk * tn * sb + tm * tn * (2 * (out_bytes + extra_bytes) + 8 + (4 if nk > 1 else 0))
                if vmem > MM_VMEM:
                    continue
                a_traffic = M * K * sa * (1 if nk == 1 else N // tn)
                b_traffic = K * N * sb * (1 if (nk == 1 and N == tn) else M // tm)
                steps = (M // tm) * (N // tn) * nk
                cost = (a_traffic + b_traffic + M * N * (out_bytes + extra_bytes)) / HBM_BYTES_PER_US + steps * STEP_US
                cost += (M // tm) * (N // tn) * (nk - 1) * tm * tn * 8 / (4 * HBM_BYTES_PER_US)
                if best is None or cost < best[0]:
                    best = (cost, tm, tn, tk)
    assert best is not None, (M, N, K)
    return best[1:]


def _matmul(a, b, *, ta=False, tb=False, out_dtype=F32, bias=None, resid=None, mul=None, mul_scale=1.0,
            relu2=False, rope=None, rope_cols=0, norm_out=None, norm_bwd=None, name="mm", plan=None):
    M = a.shape[1] if ta else a.shape[0]
    K = a.shape[0] if ta else a.shape[1]
    N = b.shape[0] if tb else b.shape[1]
    assert (b.shape[1] if tb else b.shape[0]) == K
    two_out = relu2 or norm_out is not None
    out_bytes = jnp.dtype(out_dtype).itemsize * (2 if relu2 else 1) + (2 if norm_out is not None else 0)
    extra_bytes = (4 if resid is not None else 0) + (mul.dtype.itemsize if mul is not None else 0) + (8 if norm_bwd else 0)
    rows = norm_out is not None or norm_bwd is not None
    tm, tn, tk = _mm_tiles(M, N, K, a.dtype.itemsize, b.dtype.itemsize, out_bytes, extra_bytes, full_rows=rows)
    nk = K // tk
    dims = (((0 if ta else 1,), (1 if tb else 0,)), ((), ()))

    def body(*refs):
        it = iter(refs)
        a_ref, b_ref = next(it), next(it)
        bias_ref = next(it) if bias is not None else None
        resid_ref = next(it) if resid is not None else None
        mul_ref = next(it) if mul is not None else None
        cos_ref, sin_ref = (next(it), next(it)) if rope is not None else (None, None)
        nw_ref = next(it) if rows else None
        h_ref, dres_ref = (next(it), next(it)) if norm_bwd is not None else (None, None)
        o_ref = next(it)
        o2_ref = next(it) if two_out or norm_bwd is not None else None
        acc_ref = next(it) if nk > 1 else None
        k = pl.program_id(2)
        part = lax.dot_general(_bf(a_ref[...]), _bf(b_ref[...]), dims, preferred_element_type=F32)
        if nk > 1:
            @pl.when(k == 0)
            def _():
                acc_ref[...] = part

            @pl.when(k > 0)
            def _():
                acc_ref[...] += part

        @pl.when(k == nk - 1)
        def _():
            y = acc_ref[...] if nk > 1 else part
            if bias_ref is not None:
                y = y + bias_ref[...]
            if rope is not None and rope_cols % tn == 0 and not (two_out or rows or mul is not None or resid is not None):
                rotated = pl.program_id(1) * tn < rope_cols

                @pl.when(rotated)
                def _():
                    o_ref[...] = _rope(y, cos_ref[...], sin_ref[...], 1.0).astype(o_ref.dtype)

                @pl.when(jnp.logical_not(rotated))
                def _():
                    o_ref[...] = y.astype(o_ref.dtype)
                return
            if rope is not None:
                col = pl.program_id(1) * tn + lax.broadcasted_iota(jnp.int32, y.shape, 1)
                y = jnp.where(col < rope_cols, _rope(y, cos_ref[...], sin_ref[...], 1.0), y)
            if mul_ref is not None:
                y = y * (mul_ref[...].astype(F32) * mul_scale)
            if resid_ref is not None:
                y = y + resid_ref[...]
            if relu2:
                r = jnp.maximum(y, 0.0)
                o_ref[...] = r.astype(o_ref.dtype)
                o2_ref[...] = (r * r).astype(o2_ref.dtype)
            elif norm_bwd is not None:
                x = h_ref[...]
                rstd = lax.rsqrt(jnp.mean(x * x, axis=-1, keepdims=True) + NORM_EPS)
                g = y * nw_ref[...]
                o_ref[...] = dres_ref[...] + rstd * g - x * (rstd * rstd * rstd) * jnp.mean(g * x, axis=-1, keepdims=True)
                dw = jnp.sum(y * x * rstd, axis=0, keepdims=True)
                first = pl.program_id(0) == 0

                @pl.when(first)
                def _():
                    o2_ref[...] = dw

                @pl.when(jnp.logical_not(first))
                def _():
                    o2_ref[...] += dw
            else:
                o_ref[...] = y.astype(o_ref.dtype)
                if norm_out is not None:
                    rstd = lax.rsqrt(jnp.mean(y * y, axis=-1, keepdims=True) + NORM_EPS)
                    o2_ref[...] = (y * rstd * nw_ref[...]).astype(BF16)

    a_spec = pl.BlockSpec((tk, tm), lambda i, j, k: (k, i)) if ta else pl.BlockSpec((tm, tk), lambda i, j, k: (i, k))
    b_spec = pl.BlockSpec((tn, tk), lambda i, j, k: (j, k)) if tb else pl.BlockSpec((tk, tn), lambda i, j, k: (k, j))
    mn_spec = pl.BlockSpec((tm, tn), lambda i, j, k: (i, j))
    in_specs, args = [a_spec, b_spec], [a, b]
    if bias is not None:
        in_specs.append(pl.BlockSpec((1, tn), lambda i, j, k: (0, j)))
        args.append(bias)
    if resid is not None:
        in_specs.append(mn_spec)
        args.append(resid)
    if mul is not None:
        in_specs.append(mn_spec)
        args.append(mul)
    if rope is not None:
        in_specs += [pl.BlockSpec((tm, LANES), lambda i, j, k: (i, 0))] * 2
        args += [rope[0], rope[1]]
    vec_spec = pl.BlockSpec((1, tn), lambda i, j, k: (0, j))
    if rows:
        in_specs.append(vec_spec)
        args.append((norm_out if norm_out is not None else norm_bwd[1]).reshape(1, N))
    if norm_bwd is not None:
        in_specs += [mn_spec, mn_spec]
        args += [norm_bwd[0], norm_bwd[2]]
    out_shape = SDS((M, N), out_dtype)
    out_specs = mn_spec
    if relu2:
        out_shape, out_specs = (out_shape, out_shape), (mn_spec, mn_spec)
    elif norm_out is not None:
        out_shape, out_specs = (out_shape, SDS((M, N), BF16)), (mn_spec, mn_spec)
    elif norm_bwd is not None:
        out_shape, out_specs = (out_shape, SDS((1, N), F32)), (mn_spec, vec_spec)
    sem = ("arbitrary",) * 3 if norm_bwd is not None else ("parallel", "parallel", "arbitrary")
    return _hosted(plan, name, lambda comm: _pc(
        body, args, out_shape=out_shape, grid=(M // tm, N // tn, nk), in_specs=in_specs, out_specs=out_specs,
        scratch_shapes=[pltpu.VMEM((tm, tn), F32)] if nk > 1 else [], name=name, sem=sem, comm=comm))


def _colsum(x, name):
    T, N = x.shape
    tm = _pick(T, 1024, 8)

    def body(x_ref, o_ref):
        s = jnp.sum(x_ref[...].astype(F32), axis=0, keepdims=True)

        @pl.when(pl.program_id(0) == 0)
        def _():
            o_ref[...] = s

        @pl.when(pl.program_id(0) > 0)
        def _():
            o_ref[...] += s

    return pl.pallas_call(
        body, out_shape=SDS((1, N), F32), grid=(T // tm,),
        in_specs=[pl.BlockSpec((tm, N), lambda i: (i, 0))], out_specs=pl.BlockSpec((1, N), lambda i: (0, 0)),
        name=name, compiler_params=_params(("arbitrary",)),
    )(x)


def _rmsnorm_fwd(h, w, name):
    T, D = h.shape
    tm = _pick(T, 512, 8)

    def body(h_ref, w_ref, o_ref):
        x = h_ref[...]
        rstd = lax.rsqrt(jnp.mean(x * x, axis=-1, keepdims=True) + NORM_EPS)
        o_ref[...] = (x * rstd * w_ref[...]).astype(BF16)

    return pl.pallas_call(
        body, out_shape=SDS((T, D), BF16), grid=(T // tm,),
        in_specs=[pl.BlockSpec((tm, D), lambda i: (i, 0)), pl.BlockSpec((1, D), lambda i: (0, 0))],
        out_specs=pl.BlockSpec((tm, D), lambda i: (i, 0)), name=name, compiler_params=_params(("parallel",)),
    )(h, w.reshape(1, D))


def _final_loss(h, target, w):
    T, D = h.shape
    tm = _pick(T, 512, 8)

    def body(h_ref, t_ref, w_ref, dh_ref, dw_ref, loss_ref):
        x = h_ref[...]
        rstd = lax.rsqrt(jnp.mean(x * x, axis=-1, keepdims=True) + NORM_EPS)
        xn = x * rstd
        err = xn * w_ref[...] - t_ref[...]
        part = 0.5 * jnp.sum(jnp.mean(err * err, axis=-1, keepdims=True), axis=0, keepdims=True)
        dy = err * (1.0 / D)
        g = dy * w_ref[...]
        dh_ref[...] = rstd * g - x * (rstd * rstd * rstd) * jnp.mean(g * x, axis=-1, keepdims=True)
        dw = jnp.sum(dy * xn, axis=0, keepdims=True)
        lp = jnp.broadcast_to(part, (1, LANES))

        @pl.when(pl.program_id(0) == 0)
        def _():
            dw_ref[...] = dw
            loss_ref[...] = lp

        @pl.when(pl.program_id(0) > 0)
        def _():
            dw_ref[...] += dw
            loss_ref[...] += lp

    row = pl.BlockSpec((tm, D), lambda i: (i, 0))
    vec = pl.BlockSpec((1, D), lambda i: (0, 0))
    return pl.pallas_call(
        body, out_shape=(SDS((T, D), F32), SDS((1, D), F32), SDS((1, LANES), F32)), grid=(T // tm,),
        in_specs=[row, row, vec], out_specs=(row, vec, pl.BlockSpec((1, LANES), lambda i: (0, 0))),
        name="final_loss", compiler_params=_params(("arbitrary",)),
    )(h, target, w.reshape(1, D))


def _band_mask(i_blk, max_dist, first_ok):
    qi = lax.broadcasted_iota(jnp.int32, (BLOCK, 2 * BLOCK), 0)
    kj = lax.broadcasted_iota(jnp.int32, (BLOCK, 2 * BLOCK), 1)
    dist = qi + BLOCK - kj
    ok = (dist >= 0) & (dist <= max_dist)
    return ok & ((kj >= BLOCK) | first_ok)


def _pair(t, i):
    return t[:, LANES * i:LANES * (i + 1)]


def _low_half(shape):
    return lax.broadcasted_iota(jnp.int32, shape, len(shape) - 1) < HEAD_DIM


def _stack_heads(t):
    lo = _low_half(t.shape)
    z = jnp.zeros_like(t)
    return jnp.concatenate([jnp.where(lo, t, z), jnp.where(lo, z, t)], axis=0)


def _swap_halves(t):
    return jnp.concatenate([t[:, HEAD_DIM:], t[:, :HEAD_DIM]], axis=1)


def _kv_operand(kv, kv_swapped, h0, n_kv, n_heads):
    R = n_heads // n_kv
    if R == 1:
        return _pair(kv, h0 // 2)
    assert kv.shape[1] == LANES and R % 2 == 0, "grouped queries: one 128-lane tile of kv heads, both heads of a pair in one group"
    g = h0 // R
    t, ts = _pair(kv, g // 2), _pair(kv_swapped, g // 2)
    lo = _low_half(t.shape)
    return jnp.where(lo, t, ts) if g % 2 == 0 else jnp.where(lo, ts, t)


def _lane_place(cols):
    m = cols[0].shape[0]
    lane = lax.broadcasted_iota(jnp.int32, (m, LANES), 1)
    out = jnp.zeros((m, LANES), F32)
    for h, c in enumerate(cols):
        out = jnp.where(lane == h, c, out)
    return out


def _attn_specs(B, S, d, C, n_heads, n_kv, q_col, k_col, v_col):
    kvw = n_kv * HEAD_DIM
    qw = n_heads * HEAD_DIM
    cq, ck = (C // qw if d > 1 else 0), (C // kvw if d > 1 else 0)
    q_spec = pl.BlockSpec((1, BLOCK, qw), lambda b, r, i: (b, i, r * cq + q_col // qw))
    kc = pl.BlockSpec((1, BLOCK, kvw), lambda b, r, i: (b, i, r * ck + k_col // kvw))
    kp = pl.BlockSpec((1, BLOCK, kvw), lambda b, r, i: (b, jnp.maximum(i - 1, 0), r * ck + k_col // kvw))
    vc = pl.BlockSpec((1, BLOCK, kvw), lambda b, r, i: (b, i, r * ck + v_col // kvw))
    vp = pl.BlockSpec((1, BLOCK, kvw), lambda b, r, i: (b, jnp.maximum(i - 1, 0), r * ck + v_col // kvw))
    return q_spec, kp, kc, vp, vc


def _attn_fwd(qkv, B, S, d, *, n_heads, n_kv, q_col, k_col, v_col, max_dist, sinks, name, plan=None):
    C = qkv.shape[1]
    Ls = S // d
    nb = Ls // BLOCK
    qw = n_heads * HEAD_DIM
    R = n_heads // n_kv
    qkv3 = qkv.reshape(B, Ls, d * C)
    scale = HEAD_DIM ** -0.5

    def body(*refs):
        if sinks is not None:
            sink_ref, q_ref, kp_ref, kc_ref, vp_ref, vc_ref, o_ref, lse_ref = refs
        else:
            q_ref, kp_ref, kc_ref, vp_ref, vc_ref, o_ref, lse_ref = refs
        i = pl.program_id(2)
        mask1 = _band_mask(i, max_dist, i > 0)
        mask = jnp.concatenate([mask1, mask1], axis=0)
        q = q_ref[0]
        kk = jnp.concatenate([kp_ref[0], kc_ref[0]], axis=0)
        vv = jnp.concatenate([vp_ref[0], vc_ref[0]], axis=0)
        kks, vvs = (_swap_halves(kk), _swap_halves(vv)) if R > 1 else (None, None)
        lo = _low_half((BLOCK, LANES))
        top = lax.broadcasted_iota(jnp.int32, (2 * BLOCK, 1), 0) < BLOCK
        lses, tiles = [], []
        for t in range(n_heads // 2):
            k2 = _kv_operand(kk, kks, 2 * t, n_kv, n_heads)
            v2 = _kv_operand(vv, vvs, 2 * t, n_kv, n_heads)
            s = lax.dot_general(_stack_heads(_pair(q, t)), k2, NT, preferred_element_type=F32) * scale
            s = jnp.where(mask, s, NEG)
            m = jnp.max(s, axis=-1, keepdims=True)
            if sinks is not None:
                sk = jnp.where(top, sink_ref[2 * t], sink_ref[2 * t + 1])
                m = jnp.maximum(m, sk)
            p = jnp.exp(s - m)
            den = jnp.sum(p, axis=-1, keepdims=True)
            if sinks is not None:
                den = den + jnp.exp(sk - m)
            lse2 = m + jnp.log(den)
            o2 = jnp.dot((p / den).astype(BF16), v2, preferred_element_type=F32)
            tiles.append(jnp.where(lo, o2[:BLOCK], o2[BLOCK:]))
            lses += [lse2[:BLOCK], lse2[BLOCK:]]
        o_ref[0] = jnp.concatenate(tiles, axis=-1)
        lse_ref[0] = _lane_place(lses)

    specs = list(_attn_specs(B, S, d, C, n_heads, n_kv, q_col, k_col, v_col))
    args = [qkv3] * 5
    if sinks is not None:
        specs = [pl.BlockSpec(memory_space=pltpu.SMEM)] + specs
        args = [sinks] + args
    o3, lse3 = _hosted(plan, name, lambda comm: _pc(
        body, args, out_shape=(SDS((B, Ls, d * qw), F32), SDS((B, Ls, d * LANES), F32)), grid=(B, d, nb), in_specs=specs,
        out_specs=(pl.BlockSpec((1, BLOCK, qw), lambda b, r, i: (b, i, r)), pl.BlockSpec((1, BLOCK, LANES), lambda b, r, i: (b, i, r))),
        name=name, sem=("parallel", "parallel", "parallel"), comm=comm))
    return o3.reshape(B * S, qw), lse3.reshape(B * S, LANES)


def _attn_dq(qkv, do, lse, delta, cos, sin, B, S, d, *, n_heads, n_kv, q_col, k_col, v_col, max_dist, name, plan=None):
    C = qkv.shape[1]
    Ls = S // d
    nb = Ls // BLOCK
    qw = n_heads * HEAD_DIM
    R = n_heads // n_kv
    scale = HEAD_DIM ** -0.5

    def body(q_ref, kp_ref, kc_ref, vp_ref, vc_ref, do_ref, lse_ref, dl_ref, cos_ref, sin_ref, dq_ref):
        i = pl.program_id(2)
        mask1 = _band_mask(i, max_dist, i > 0)
        mask = jnp.concatenate([mask1, mask1], axis=0)
        q = q_ref[0]
        do_ = do_ref[0]
        kk = jnp.concatenate([kp_ref[0], kc_ref[0]], axis=0)
        vv = jnp.concatenate([vp_ref[0], vc_ref[0]], axis=0)
        kks, vvs = (_swap_halves(kk), _swap_halves(vv)) if R > 1 else (None, None)
        lo = _low_half((BLOCK, LANES))
        lse_t, dl_t = lse_ref[0], dl_ref[0]
        tiles = []
        for t in range(n_heads // 2):
            k2 = _kv_operand(kk, kks, 2 * t, n_kv, n_heads)
            v2 = _kv_operand(vv, vvs, 2 * t, n_kv, n_heads)
            lse2 = jnp.concatenate([lse_t[:, 2 * t:2 * t + 1], lse_t[:, 2 * t + 1:2 * t + 2]], axis=0)
            dl2 = jnp.concatenate([dl_t[:, 2 * t:2 * t + 1], dl_t[:, 2 * t + 1:2 * t + 2]], axis=0)
            s = lax.dot_general(_stack_heads(_pair(q, t)), k2, NT, preferred_element_type=F32) * scale
            p = jnp.where(mask, jnp.exp(s - lse2), 0.0)
            dp = lax.dot_general(_stack_heads(_pair(do_, t)), v2, NT, preferred_element_type=F32)
            ds = p * (dp - dl2)
            dq2 = jnp.dot(ds.astype(BF16), k2, preferred_element_type=F32) * scale
            tiles.append(jnp.where(lo, dq2[:BLOCK], dq2[BLOCK:]))
        dq = jnp.concatenate(tiles, axis=-1)
        dq_ref[0] = _rope(dq, cos_ref[0], sin_ref[0], -1.0).astype(BF16)

    qs, kp, kc, vp, vc = _attn_specs(B, S, d, C, n_heads, n_kv, q_col, k_col, v_col)
    row_q = pl.BlockSpec((1, BLOCK, qw), lambda b, r, i: (b, i, r))
    row_l = pl.BlockSpec((1, BLOCK, LANES), lambda b, r, i: (b, i, r))
    qkv3 = qkv.reshape(B, Ls, d * C)
    v3 = lambda t, w: t.reshape(B, Ls, d * w)
    args = (qkv3, qkv3, qkv3, qkv3, qkv3, v3(do, qw), v3(lse, LANES), v3(delta, LANES), v3(cos, LANES), v3(sin, LANES))
    dq3 = _hosted(plan, name, lambda comm: _pc(
        body, args, out_shape=SDS((B, Ls, d * qw), BF16), grid=(B, d, nb),
        in_specs=[qs, kp, kc, vp, vc, row_q, row_l, row_l, row_l, row_l], out_specs=row_q,
        name=name, sem=("parallel", "parallel", "parallel"), comm=comm))
    return dq3.reshape(B * S, qw)


def _attn_dkv(qkv, do, lse, delta, cos, sin, B, S, d, *, n_heads, n_kv, q_col, k_col, v_col, max_dist, name, plan=None):
    C = qkv.shape[1]
    Ls = S // d
    nb = Ls // BLOCK
    qw = n_heads * HEAD_DIM
    kvw = n_kv * HEAD_DIM
    R = n_heads // n_kv
    scale = HEAD_DIM ** -0.5
    cq, ck = (C // qw if d > 1 else 0), (C // kvw if d > 1 else 0)

    def body(k_ref, v_ref, q0_ref, q1_ref, do0_ref, do1_ref, lse0_ref, lse1_ref, dl0_ref, dl1_ref, cos_ref, sin_ref,
             dk_ref, dv_ref):
        j = pl.program_id(2)
        kj = lax.broadcasted_iota(jnp.int32, (BLOCK, BLOCK), 0)
        qi = lax.broadcasted_iota(jnp.int32, (BLOCK, BLOCK), 1)
        dist0 = qi - kj
        dist1 = qi + BLOCK - kj
        mask0 = (dist0 >= 0) & (dist0 <= max_dist)
        mask1 = (dist1 <= max_dist) & (j + 1 < nb)
        kb, vb = k_ref[0], v_ref[0]
        kbs, vbs = (_swap_halves(kb), _swap_halves(vb)) if R > 1 else (None, None)
        sides = ((q0_ref[0], do0_ref[0], lse0_ref[0].T, dl0_ref[0].T, mask0), (q1_ref[0], do1_ref[0], lse1_ref[0].T, dl1_ref[0].T, mask1))
        n_acc = n_kv if R > 1 else n_kv // 2
        dks = [jnp.zeros((BLOCK, LANES), F32) for _ in range(n_acc)]
        dvs = [jnp.zeros((BLOCK, LANES), F32) for _ in range(n_acc)]
        for t in range(n_heads // 2):
            k2 = _kv_operand(kb, kbs, 2 * t, n_kv, n_heads)
            v2 = _kv_operand(vb, vbs, 2 * t, n_kv, n_heads)
            a = (2 * t) // R if R > 1 else t
            for (q, do_, lse_r, dl_r, mask) in sides:
                q2, do2 = _stack_heads(_pair(q, t)), _stack_heads(_pair(do_, t))
                s = lax.dot_general(k2, q2, NT, preferred_element_type=F32) * scale
                dp = lax.dot_general(v2, do2, NT, preferred_element_type=F32)
                ps, dss = [], []
                for half in (0, 1):
                    h = 2 * t + half
                    sl = slice(BLOCK * half, BLOCK * (half + 1))
                    p = jnp.where(mask, jnp.exp(s[:, sl] - lse_r[h:h + 1, :]), 0.0)
                    ps.append(p)
                    dss.append(p * (dp[:, sl] - dl_r[h:h + 1, :]))
                dvs[a] = dvs[a] + jnp.dot(jnp.concatenate(ps, axis=1).astype(BF16), do2, preferred_element_type=F32)
                dks[a] = dks[a] + jnp.dot(jnp.concatenate(dss, axis=1).astype(BF16), q2, preferred_element_type=F32)
        if R > 1:
            lo = _low_half((BLOCK, LANES))
            fold = lambda x: x + pltpu.roll(x, HEAD_DIM, 1)
            dks = [jnp.where(lo, fold(dks[2 * t]), fold(dks[2 * t + 1])) for t in range(n_kv // 2)]
            dvs = [jnp.where(lo, fold(dvs[2 * t]), fold(dvs[2 * t + 1])) for t in range(n_kv // 2)]
        dk_t = jnp.concatenate(dks, axis=-1) * scale
        dk_ref[0] = _rope(dk_t, cos_ref[0], sin_ref[0], -1.0).astype(BF16)
        dv_ref[0] = jnp.concatenate(dvs, axis=-1).astype(BF16)

    nxt = lambda j: jnp.minimum(j + 1, nb - 1)
    k_spec = pl.BlockSpec((1, BLOCK, kvw), lambda b, r, j: (b, j, r * ck + k_col // kvw))
    v_spec = pl.BlockSpec((1, BLOCK, kvw), lambda b, r, j: (b, j, r * ck + v_col // kvw))
    q0 = pl.BlockSpec((1, BLOCK, qw), lambda b, r, j: (b, j, r * cq + q_col // qw))
    q1 = pl.BlockSpec((1, BLOCK, qw), lambda b, r, j: (b, nxt(j), r * cq + q_col // qw))
    w0 = lambda w: pl.BlockSpec((1, BLOCK, w), lambda b, r, j: (b, j, r))
    w1 = lambda w: pl.BlockSpec((1, BLOCK, w), lambda b, r, j: (b, nxt(j), r))
    qkv3 = qkv.reshape(B, Ls, d * C)
    v3 = lambda t, w: t.reshape(B, Ls, d * w)
    do3, lse3, dl3 = v3(do, qw), v3(lse, LANES), v3(delta, LANES)
    args = (qkv3, qkv3, qkv3, qkv3, do3, do3, lse3, lse3, dl3, dl3, v3(cos, LANES), v3(sin, LANES))
    dk3, dv3 = _hosted(plan, name, lambda comm: _pc(
        body, args, out_shape=(SDS((B, Ls, d * kvw), BF16), SDS((B, Ls, d * kvw), BF16)), grid=(B, d, nb),
        in_specs=[k_spec, v_spec, q0, q1, w0(qw), w1(qw), w0(LANES), w1(LANES), w0(LANES), w1(LANES), w0(LANES), w0(LANES)],
        out_specs=(w0(kvw), w0(kvw)), name=name, sem=("parallel", "parallel", "parallel"), comm=comm))
    return dk3.reshape(B * S, kvw), dv3.reshape(B * S, kvw)


def _head_expand():
    r = lax.broadcasted_iota(jnp.int32, (LANES, C_HEADS * HEAD_DIM), 0)
    c = lax.broadcasted_iota(jnp.int32, (LANES, C_HEADS * HEAD_DIM), 1)
    return jnp.where(c // HEAD_DIM == r, 1.0, 0.0).astype(F32)


def _delta(do, o, lse=None, sinks_row=None, name="delta"):
    T, W = do.shape
    tm = _pick(T, 512, 8)
    with_sink = sinks_row is not None

    def body(*refs):
        if with_sink:
            do_ref, o_ref, lse_ref, sk_ref, dl_ref, dob_ref, ds_ref = refs
        else:
            do_ref, o_ref, dl_ref, dob_ref = refs
        do_ = do_ref[...]
        dl = lax.dot_general(do_ * o_ref[...], _head_expand(), NT, preferred_element_type=F32, precision=HI)
        dl_ref[...] = dl
        dob_ref[...] = do_.astype(BF16)
        if with_sink:
            lane = lax.broadcasted_iota(jnp.int32, dl.shape, 1)
            contrib = jnp.where(lane < A_N_HEADS, -jnp.exp(sk_ref[...] - lse_ref[...]) * dl, 0.0)
            part = jnp.sum(contrib, axis=0, keepdims=True)

            @pl.when(pl.program_id(0) == 0)
            def _():
                ds_ref[...] = part

            @pl.when(pl.program_id(0) > 0)
            def _():
                ds_ref[...] += part

    row_w = pl.BlockSpec((tm, W), lambda i: (i, 0))
    row_l = pl.BlockSpec((tm, LANES), lambda i: (i, 0))
    vec_l = pl.BlockSpec((1, LANES), lambda i: (0, 0))
    if with_sink:
        return pl.pallas_call(
            body, out_shape=(SDS((T, LANES), F32), SDS((T, W), BF16), SDS((1, LANES), F32)), grid=(T // tm,),
            in_specs=[row_w, row_w, row_l, vec_l], out_specs=(row_l, row_w, vec_l), name=name,
            compiler_params=_params(("arbitrary",)),
        )(do, o, lse, sinks_row)
    return pl.pallas_call(
        body, out_shape=(SDS((T, LANES), F32), SDS((T, W), BF16)), grid=(T // tm,),
        in_specs=[row_w, row_w], out_specs=(row_l, row_w), name=name, compiler_params=_params(("parallel",)),
    )(do, o)


def _merge(os_, lses):
    T, W = os_[0].shape
    tm = _pick(T, 512, 8)

    def body(o0, o1, o2, l0, l1, l2, o_ref, lse_ref):
        ls = [l0[...], l1[...], l2[...]]
        m = jnp.maximum(jnp.maximum(ls[0], ls[1]), ls[2])
        ws = [jnp.exp(l - m) for l in ls]
        tot = ws[0] + ws[1] + ws[2]
        lse_ref[...] = m + jnp.log(tot)
        e = _head_expand()
        acc = jnp.zeros((tm, W), F32)
        for w, o in zip(ws, (o0, o1, o2)):
            acc = acc + jnp.dot(w / tot, e, preferred_element_type=F32, precision=HI) * o[...]
        o_ref[...] = acc

    row_w = pl.BlockSpec((tm, W), lambda i: (i, 0))
    row_l = pl.BlockSpec((tm, LANES), lambda i: (i, 0))
    return pl.pallas_call(
        body, out_shape=(SDS((T, W), F32), SDS((T, LANES), F32)), grid=(T // tm,),
        in_specs=[row_w] * 3 + [row_l] * 3, out_specs=(row_w, row_l), name="c_merge", compiler_params=_params(("parallel",)),
    )(*os_, *lses)


CONV_TC = 256


def _conv_pre(x, w, bias):
    row = lax.broadcasted_iota(jnp.int32, x.shape, 0)
    acc = x * w[SSM_CONV - 1:SSM_CONV, :] + bias
    for k in range(1, SSM_CONV):
        acc = acc + jnp.where(row >= k, pltpu.roll(x, k, 0), 0.0) * w[SSM_CONV - 1 - k:SSM_CONV - k, :]
    return acc


def _conv_fwd(zx3, w, bias):
    B, S, _ = zx3.shape
    off = SSM_D_INNER // CONV_TC

    def body(x_ref, w_ref, b_ref, o_ref):
        v = _conv_pre(x_ref[0], w_ref[...], b_ref[...])
        o_ref[0] = v * jax.nn.sigmoid(v)

    return pl.pallas_call(
        body, out_shape=SDS((B, S, SSM_CONV_DIM), F32), grid=(B, SSM_CONV_DIM // CONV_TC),
        in_specs=[pl.BlockSpec((1, S, CONV_TC), lambda b, j: (b, 0, j + off)),
                  pl.BlockSpec((SSM_CONV, CONV_TC), lambda b, j: (0, j)), pl.BlockSpec((1, CONV_TC), lambda b, j: (0, j))],
        out_specs=pl.BlockSpec((1, S, CONV_TC), lambda b, j: (b, 0, j)), name="b_conv_fwd",
        compiler_params=_params(("parallel", "parallel")),
    )(zx3, w, bias)


def _conv_bwd(zx3, dxc, w, bias, col0, name):
    B, S, n = dxc.shape
    tc = _pick(n, CONV_TC)
    off_x = (SSM_D_INNER + col0) // tc
    off_w = col0 // tc

    def body(x_ref, d_ref, w_ref, b_ref, dx_ref, dw_ref, db_ref):
        x = x_ref[0]
        wv = w_ref[...]
        v = _conv_pre(x, wv, b_ref[...])
        sg = jax.nn.sigmoid(v)
        dc = d_ref[0] * (sg * (1.0 + v * (1.0 - sg)))
        row = lax.broadcasted_iota(jnp.int32, x.shape, 0)
        dx = dc * wv[SSM_CONV - 1:SSM_CONV, :]
        dws = [jnp.sum(dc * x, axis=0, keepdims=True)]
        for k in range(1, SSM_CONV):
            dx = dx + jnp.where(row < S - k, pltpu.roll(dc, S - k, 0), 0.0) * wv[SSM_CONV - 1 - k:SSM_CONV - k, :]
            dws.append(jnp.sum(dc * jnp.where(row >= k, pltpu.roll(x, k, 0), 0.0), axis=0, keepdims=True))
        dx_ref[0] = dx.astype(BF16)
        ridx = lax.broadcasted_iota(jnp.int32, (SSM_CONV, tc), 0)
        dw = jnp.zeros((SSM_CONV, tc), F32)
        for k in range(SSM_CONV):
            dw = jnp.where(ridx == SSM_CONV - 1 - k, dws[k], dw)
        db = jnp.sum(dc, axis=0, keepdims=True)

        @pl.when(pl.program_id(1) == 0)
        def _():
            dw_ref[...] = dw
            db_ref[...] = db

        @pl.when(pl.program_id(1) > 0)
        def _():
            dw_ref[...] += dw
            db_ref[...] += db

    return pl.pallas_call(
        body, out_shape=(SDS((B, S, n), BF16), SDS((SSM_CONV, n), F32), SDS((1, n), F32)), grid=(n // tc, B),
        in_specs=[pl.BlockSpec((1, S, tc), lambda j, b: (b, 0, j + off_x)), pl.BlockSpec((1, S, tc), lambda j, b: (b, 0, j)),
                  pl.BlockSpec((SSM_CONV, tc), lambda j, b: (0, j + off_w)), pl.BlockSpec((1, tc), lambda j, b: (0, j + off_w))],
        out_specs=(pl.BlockSpec((1, S, tc), lambda j, b: (b, 0, j)), pl.BlockSpec((SSM_CONV, tc), lambda j, b: (0, j)),
                   pl.BlockSpec((1, tc), lambda j, b: (0, j))),
        name=name, compiler_params=_params(("parallel", "arbitrary")),
    )(zx3, dxc, w, bias)


def _ssd_common(x, Bm, Cm, dtc_raw, dtr_raw, pr, pc):
    Q = SSM_CHUNK
    zc = dtc_raw + pr[0:1, :]
    dt_c = jax.nn.softplus(zc)
    dt_r = jax.nn.softplus(dtr_raw + pc[:, 0:1])
    A_r = -jnp.exp(pr[1:2, :])
    A_c = -jnp.exp(pc[:, 1:2])
    row = lax.broadcasted_iota(jnp.int32, (Q, Q), 0)
    col = lax.broadcasted_iota(jnp.int32, (Q, Q), 1)
    tril = jnp.where(row >= col, 1.0, 0.0).astype(F32)
    cs_c = jnp.dot(tril, dt_c * A_r, preferred_element_type=F32, precision=HI)
    cs_r = lax.dot_general(dt_r * A_c, tril, NT, preferred_element_type=F32, precision=HI)
    return zc, dt_c, A_r, cs_c, cs_r, row, col, tril


def _ssd_fwd(xc3, dtc, dtr, prow, pcol, plan=None):
    B, S, _ = xc3.shape
    Q, G, HG, P, N = SSM_CHUNK, SSM_N_GROUPS, SSM_HG, HEAD_DIM, SSM_D_STATE
    nc = S // Q
    xw = HG * P

    def body(x_ref, b_ref, c_ref, dtc_ref, dtr_ref, pr_ref, pc_ref, y_ref, st_ref, state):
        c = pl.program_id(1)

        @pl.when(c == 0)
        def _():
            state[...] = jnp.zeros_like(state)

        pr = pr_ref[0]
        for bb in range(B):
            x, Bm, Cm = x_ref[bb], b_ref[bb], c_ref[bb]
            _, dt_c, _, cs_c, cs_r, row, col, _ = _ssd_common(x, Bm, Cm, dtc_ref[bb, 0], dtr_ref[bb, 0], pr, pc_ref[0])
            Bb, Cb = Bm.astype(BF16), Cm.astype(BF16)
            CB = lax.dot_general(Cb, Bb, NT, preferred_element_type=F32)
            ys = []
            for hg in range(HG):
                xh = x[:, P * hg:P * (hg + 1)]
                xt = xh * dt_c[:, hg:hg + 1]
                csc, csr = cs_c[:, hg:hg + 1], cs_r[hg:hg + 1, :]
                L = jnp.where(row >= col, jnp.exp(jnp.minimum(csc - csr, 0.0)), 0.0)
                ydiag = jnp.dot((CB * L).astype(BF16), xt.astype(BF16), preferred_element_type=F32)
                Sh = state[bb, hg]
                yoff = lax.dot_general(Cb, Sh.astype(BF16), NT, preferred_element_type=F32) * jnp.exp(csc)
                ys.append(ydiag + yoff + pr[2:3, hg:hg + 1] * xh)
                st_ref[bb, 0, 0, P * hg:P * (hg + 1), :] = Sh
                csq = csc[Q - 1:Q, :]
                upd = lax.dot_general((xt * jnp.exp(csq - csc)).astype(BF16), Bb, TN, preferred_element_type=F32)
                state[bb, hg] = Sh * jnp.exp(csq) + upd
            y_ref[bb] = jnp.concatenate([jnp.concatenate(ys[0:2], axis=-1), jnp.concatenate(ys[2:4], axis=-1)], axis=-1)

    bo, co = SSM_D_INNER // N, (SSM_D_INNER + SSM_BC_DIM) // N
    return _hosted(plan, "b_ssd_fwd", lambda comm: _pc(
        body, (xc3, xc3, xc3, dtc, dtr, prow, pcol),
        out_shape=(SDS((B, S, SSM_D_INNER), F32), SDS((B, G, nc, xw, N), F32)), grid=(G, nc),
        in_specs=[pl.BlockSpec((B, Q, xw), lambda g, c: (0, c, g)), pl.BlockSpec((B, Q, N), lambda g, c: (0, c, bo + g)),
                  pl.BlockSpec((B, Q, N), lambda g, c: (0, c, co + g)), pl.BlockSpec((B, 1, Q, HG), lambda g, c: (0, g, c, 0)),
                  pl.BlockSpec((B, 1, HG, Q), lambda g, c: (0, g, 0, c)), pl.BlockSpec((1, 3, HG), lambda g, c: (g, 0, 0)),
                  pl.BlockSpec((1, HG, 3), lambda g, c: (g, 0, 0))],
        out_specs=(pl.BlockSpec((B, Q, xw), lambda g, c: (0, c, g)), pl.BlockSpec((B, 1, 1, xw, N), lambda g, c: (0, g, c, 0, 0))),
        scratch_shapes=[pltpu.VMEM((B, HG, P, N), F32)], name="b_ssd_fwd", sem=("parallel", "arbitrary"), comm=comm))


def _ssd_bwd(xc3, dtc, dtr, prow, pcol, states, dy3, plan=None):
    B, S, _ = xc3.shape
    Q, G, HG, P, N = SSM_CHUNK, SSM_N_GROUPS, SSM_HG, HEAD_DIM, SSM_D_STATE
    nc = S // Q
    xw = HG * P

    def body(x_ref, b_ref, c_ref, dtc_ref, dtr_ref, pr_ref, pc_ref, st_ref, dy_ref,
             dx_ref, db_ref, dc_ref, ddt_ref, dpar_ref, dstate):
        ci = pl.program_id(1)

        @pl.when(ci == 0)
        def _():
            dstate[...] = jnp.zeros_like(dstate)

        pr = pr_ref[0]
        dpar = one_sequence(0, pr, x_ref, b_ref, c_ref, dtc_ref, dtr_ref, pc_ref, st_ref, dy_ref, dx_ref, db_ref, dc_ref,
                            ddt_ref, dstate)
        for bb in range(1, B):
            dpar = dpar + one_sequence(bb, pr, x_ref, b_ref, c_ref, dtc_ref, dtr_ref, pc_ref, st_ref, dy_ref, dx_ref, db_ref,
                                       dc_ref, ddt_ref, dstate)
        first = ci == 0

        @pl.when(first)
        def _():
            dpar_ref[0] = dpar

        @pl.when(jnp.logical_not(first))
        def _():
            dpar_ref[0] += dpar

    def one_sequence(bb, pr, x_ref, b_ref, c_ref, dtc_ref, dtr_ref, pc_ref, st_ref, dy_ref, dx_ref, db_ref, dc_ref, ddt_ref,
                     dstate):
        x, Bm, Cm, dy = x_ref[bb], b_ref[bb], c_ref[bb], dy_ref[bb]
        zc, dt_c, A_r, cs_c, cs_r, row, col, tril = _ssd_common(x, Bm, Cm, dtc_ref[bb, 0], dtr_ref[bb, 0], pr, pc_ref[0])
        Bb, Cb = Bm.astype(BF16), Cm.astype(BF16)
        CB = lax.dot_general(Cb, Bb, NT, preferred_element_type=F32)
        CBt = lax.dot_general(Bb, Cb, NT, preferred_element_type=F32)
        lane4 = lax.broadcasted_iota(jnp.int32, (Q, HG), 1)
        lane4r = lax.broadcasted_iota(jnp.int32, (1, HG), 1)
        rowq = lax.broadcasted_iota(jnp.int32, (Q, 1), 0)
        dB = jnp.zeros((Q, N), F32)
        dC = jnp.zeros((Q, N), F32)
        dcs4 = jnp.zeros((Q, HG), F32)
        dtx4 = jnp.zeros((Q, HG), F32)
        dD4 = jnp.zeros((1, HG), F32)
        dxts, xhs, dyhs = [], [], []
        for hg in range(HG):
            xh = x[:, P * hg:P * (hg + 1)]
            dyh = dy[:, P * hg:P * (hg + 1)]
            xt = xh * dt_c[:, hg:hg + 1]
            xtb, dyb = xt.astype(BF16), dyh.astype(BF16)
            csc, csr = cs_c[:, hg:hg + 1], cs_r[hg:hg + 1, :]
            L = jnp.where(row >= col, jnp.exp(jnp.minimum(csc - csr, 0.0)), 0.0)
            Lt = jnp.where(col >= row, jnp.exp(jnp.minimum(csr - csc, 0.0)), 0.0)
            M, Mt = CB * L, CBt * Lt
            Sh = st_ref[bb, 0, 0, P * hg:P * (hg + 1), :]
            dSh = dstate[bb, hg]
            Shb, dShb = Sh.astype(BF16), dSh.astype(BF16)
            ecs = jnp.exp(csc)
            csq = csc[Q - 1:Q, :]
            dec = jnp.exp(csq - csc)
            dxt = jnp.dot(Mt.astype(BF16), dyb, preferred_element_type=F32)
            dxt = dxt + lax.dot_general(Bb, dShb, NT, preferred_element_type=F32) * dec
            Gm = lax.dot_general(dyb, xtb, NT, preferred_element_type=F32)
            Gt = lax.dot_general(xtb, dyb, NT, preferred_element_type=F32)
            dC = dC + jnp.dot((Gm * L).astype(BF16), Bb, preferred_element_type=F32)
            dB = dB + jnp.dot((Gt * Lt).astype(BF16), Cb, preferred_element_type=F32)
            dC = dC + jnp.dot(dyb, Shb, preferred_element_type=F32) * ecs
            dBst = jnp.dot(xtb, dShb, preferred_element_type=F32) * dec
            dB = dB + dBst
            dcs = jnp.sum(Gm * M, axis=1, keepdims=True) - jnp.sum(Gt * Mt, axis=1, keepdims=True)
            yoff = lax.dot_general(Cb, Shb, NT, preferred_element_type=F32) * ecs
            dcs = dcs + jnp.sum(yoff * dyh, axis=1, keepdims=True)
            r = jnp.sum(dBst * Bm, axis=1, keepdims=True)
            dcs = dcs - r
            extra = jnp.sum(r, axis=0, keepdims=True) + jnp.exp(csq) * jnp.sum(
                jnp.sum(dSh * Sh, axis=1, keepdims=True), axis=0, keepdims=True)
            dcs = dcs + jnp.where(rowq == Q - 1, extra, 0.0)
            dcs4 = jnp.where(lane4 == hg, dcs, dcs4)
            dtx4 = jnp.where(lane4 == hg, jnp.sum(dxt * xh, axis=1, keepdims=True), dtx4)
            dD4 = jnp.where(lane4r == hg, jnp.sum(jnp.sum(dyh * xh, axis=1, keepdims=True), axis=0, keepdims=True), dD4)
            dstate[bb, hg] = dSh * jnp.exp(csq) + lax.dot_general((dyh * ecs).astype(BF16), Cb, TN, preferred_element_type=F32)
            dxts.append(dxt)
            xhs.append(xh)
            dyhs.append(dyh)
        da4 = lax.dot_general(tril, dcs4, TN, preferred_element_type=F32, precision=HI)
        ddt4 = da4 * A_r + dtx4
        ddtraw = ddt4 * jax.nn.sigmoid(zc)
        ddt_ref[bb, 0] = ddtraw
        dxs = [dxts[hg] * dt_c[:, hg:hg + 1] + pr[2:3, hg:hg + 1] * dyhs[hg] for hg in range(HG)]
        dx_ref[bb] = jnp.concatenate([jnp.concatenate(dxs[0:2], axis=-1), jnp.concatenate(dxs[2:4], axis=-1)], axis=-1)
        db_ref[bb] = dB
        dc_ref[bb] = dC
        d_bias = jnp.sum(ddtraw, axis=0, keepdims=True)
        d_alog = jnp.sum(da4 * dt_c, axis=0, keepdims=True) * A_r
        r3 = lax.broadcasted_iota(jnp.int32, (3, HG), 0)
        return jnp.where(r3 == 0, d_bias, jnp.where(r3 == 1, d_alog, dD4))

    rc = lambda c: nc - 1 - c
    bo, co = SSM_D_INNER // N, (SSM_D_INNER + SSM_BC_DIM) // N
    return _hosted(plan, "b_ssd_bwd", lambda comm: _pc(
        body, (xc3, xc3, xc3, dtc, dtr, prow, pcol, states, dy3),
        out_shape=(SDS((B, S, SSM_D_INNER), F32), SDS((B, S, SSM_BC_DIM), F32), SDS((B, S, SSM_BC_DIM), F32),
                   SDS((B, G, S, HG), F32), SDS((G, 3, HG), F32)),
        grid=(G, nc),
        in_specs=[pl.BlockSpec((B, Q, xw), lambda g, c: (0, rc(c), g)), pl.BlockSpec((B, Q, N), lambda g, c: (0, rc(c), bo + g)),
                  pl.BlockSpec((B, Q, N), lambda g, c: (0, rc(c), co + g)), pl.BlockSpec((B, 1, Q, HG), lambda g, c: (0, g, rc(c), 0)),
                  pl.BlockSpec((B, 1, HG, Q), lambda g, c: (0, g, 0, rc(c))), pl.BlockSpec((1, 3, HG), lambda g, c: (g, 0, 0)),
                  pl.BlockSpec((1, HG, 3), lambda g, c: (g, 0, 0)),
                  pl.BlockSpec((B, 1, 1, xw, N), lambda g, c: (0, g, rc(c), 0, 0)), pl.BlockSpec((B, Q, xw), lambda g, c: (0, rc(c), g))],
        out_specs=(pl.BlockSpec((B, Q, xw), lambda g, c: (0, rc(c), g)), pl.BlockSpec((B, Q, N), lambda g, c: (0, rc(c), g)),
                   pl.BlockSpec((B, Q, N), lambda g, c: (0, rc(c), g)), pl.BlockSpec((B, 1, Q, HG), lambda g, c: (0, g, rc(c), 0)),
                   pl.BlockSpec((1, 3, HG), lambda g, c: (g, 0, 0))),
        scratch_shapes=[pltpu.VMEM((B, HG, P, N), F32)], name="b_ssd_bwd", sem=("parallel", "arbitrary"), comm=comm))


GN_W = SSM_D_INNER // SSM_N_GROUPS


def _gate_fwd(y, zx, nw):
    T = y.shape[0]
    tm = _pick(T, 256, 8)

    def body(y_ref, z_ref, w_ref, o_ref):
        z = z_ref[...]
        gt = y_ref[...] * (z * jax.nn.sigmoid(z))
        outs = []
        for k in range(SSM_N_GROUPS):
            gk = gt[:, GN_W * k:GN_W * (k + 1)]
            outs.append(gk * lax.rsqrt(jnp.mean(gk * gk, axis=-1, keepdims=True) + NORM_EPS))
        o_ref[...] = (jnp.concatenate(outs, axis=-1) * w_ref[...]).astype(BF16)

    row = pl.BlockSpec((tm, SSM_D_INNER), lambda i: (i, 0))
    return pl.pallas_call(
        body, out_shape=SDS((T, SSM_D_INNER), BF16), grid=(T // tm,),
        in_specs=[row, row, pl.BlockSpec((1, SSM_D_INNER), lambda i: (0, 0))], out_specs=row, name="b_gate_fwd",
        compiler_params=_params(("parallel",)),
    )(y, zx, nw)


def _gate_bwd(dgn, y, zx, nw):
    T = y.shape[0]
    tm = _pick(T, 256, 8)

    def body(d_ref, y_ref, z_ref, w_ref, dy_ref, dz_ref, dw_ref):
        z, yv, w = z_ref[...], y_ref[...], w_ref[...]
        sg = jax.nn.sigmoid(z)
        sz = z * sg
        gt = yv * sz
        gw = d_ref[...] * w
        dgts, dws = [], []
        for k in range(SSM_N_GROUPS):
            sl = slice(GN_W * k, GN_W * (k + 1))
            gk, gwk = gt[:, sl], gw[:, sl]
            rstd = lax.rsqrt(jnp.mean(gk * gk, axis=-1, keepdims=True) + NORM_EPS)
            dgts.append(rstd * gwk - gk * (rstd * rstd * rstd) * jnp.mean(gwk * gk, axis=-1, keepdims=True))
            dws.append(jnp.sum(d_ref[:, sl] * gk * rstd, axis=0, keepdims=True))
        dgt = jnp.concatenate(dgts, axis=-1)
        dy_ref[...] = dgt * sz
        dz_ref[...] = (dgt * yv * (sg * (1.0 + z * (1.0 - sg)))).astype(BF16)
        dw = jnp.concatenate(dws, axis=-1)

        @pl.when(pl.program_id(0) == 0)
        def _():
            dw_ref[...] = dw

        @pl.when(pl.program_id(0) > 0)
        def _():
            dw_ref[...] += dw

    row = pl.BlockSpec((tm, SSM_D_INNER), lambda i: (i, 0))
    vec = pl.BlockSpec((1, SSM_D_INNER), lambda i: (0, 0))
    return pl.pallas_call(
        body, out_shape=(SDS((T, SSM_D_INNER), F32), SDS((T, SSM_D_INNER), BF16), SDS((1, SSM_D_INNER), F32)), grid=(T // tm,),
        in_specs=[row, row, row, vec], out_specs=(row, row, vec), name="b_gate_bwd", compiler_params=_params(("arbitrary",)),
    )(dgn, y, zx, nw)


N_CHIPS = 4


def _dev_block(ref, kind, j, size):
    if kind == "slot":
        return ref.at[j]
    start = pl.multiple_of(j * size, size)
    nd = len(ref.shape)
    if kind == "col":
        return ref.at[(slice(None),) * (nd - 1) + (pl.ds(start, size),)]
    return ref.at[(slice(None),) * (nd - 2) + (pl.ds(start, size), slice(None))]


def _dma_sems(n, k):
    return [pltpu.SemaphoreType.DMA((n, k)), pltpu.SemaphoreType.DMA((n, k)), pltpu.SemaphoreType.DMA((n, k))]


def _place(shard, layer, kind, full_shape, dev, name):
    k, n = shard.shape[1:]
    tr = _pick(k, 512, 16)
    nb = k // tr

    def body(dev_ref, s_ref, o_ref):
        if kind == "slot":
            o_ref[0] = s_ref[0].astype(BF16)
        else:
            o_ref[...] = s_ref[0].astype(BF16)

    out_spec = {"slot": pl.BlockSpec((1, tr, n), lambda i, d: (d[0], i, 0)),
                "row": pl.BlockSpec((tr, n), lambda i, d: (d[0] * nb + i, 0)),
                "col": pl.BlockSpec((tr, n), lambda i, d: (i, d[0]))}[kind]
    return pl.pallas_call(
        body, out_shape=SDS(full_shape, BF16),
        grid_spec=pltpu.PrefetchScalarGridSpec(
            num_scalar_prefetch=1, grid=(nb,), in_specs=[pl.BlockSpec((1, tr, n), lambda i, d: (layer, i, 0))], out_specs=out_spec),
        name=name, compiler_params=_params(("arbitrary",)),
    )(dev, shard)


def _run_comm(comm, name):
    c_in = len(comm.inputs)

    def body(*refs):
        cins, couts, sems = refs[:c_in], refs[c_in:c_in + len(comm.out_shapes)], refs[c_in + len(comm.out_shapes):]
        for _, fn in comm.phases:
            fn(cins, couts, sems)

    return pl.pallas_call(
        body, out_shape=list(comm.out_shapes), in_specs=[ANY] * c_in, out_specs=[ANY] * len(comm.out_shapes),
        input_output_aliases=dict(comm.aliases), scratch_shapes=list(comm.sems), name=name,
    )(*comm.inputs)


def _gather_comm(items, mid=0.7):
    n = len(items)

    def tools(srcs, dsts, sems):
        send_sems, recv_sems, local_sems = sems
        px, py, pc = lax.axis_index("x"), lax.axis_index("y"), lax.axis_index("c")
        me, sibling = (px, py, pc), (px, py, 1 - pc)
        chips = [(1 - px, py), (px, 1 - py), (1 - px, 1 - py)]

        def blk(a, dev):
            return _dev_block(dsts[a], items[a][1], 4 * dev[0] + 2 * dev[1] + dev[2], items[a][2])

        def copy(a, k, block, to, src=None):
            return pltpu.make_async_remote_copy(
                src_ref=blk(a, block) if src is None else src, dst_ref=blk(a, block),
                send_sem=send_sems.at[a, k], recv_sem=recv_sems.at[a, k], device_id=to, device_id_type=MESH)

        def mine():
            return [pltpu.make_async_copy(srcs[a], blk(a, me), local_sems.at[a, 0]) for a in range(n) if not items[a][4]]

        def first():
            out = []
            for a in range(n):
                src = blk(a, me) if items[a][4] else srcs[a]
                out.append(copy(a, 0, me, sibling, src=src))
                out += [copy(a, 1 + j, me, (*chip, pc), src=src) for j, chip in enumerate(chips)]
            return out

        def passed():
            return [copy(a, 4 + j, (*chip, pc), sibling) for j, chip in enumerate(chips) for a in range(n)]

        return me, sibling, chips, pc, copy, mine, first, passed

    def start(srcs, dsts, sems):
        *_, mine, first, _ = tools(srcs, dsts, sems)
        for cp in mine() + first():
            cp.start()

    def forward(srcs, dsts, sems):
        me, _, chips, pc, copy, _, _, passed = tools(srcs, dsts, sems)
        fwd = passed()
        for j, chip in enumerate(chips):
            for a in range(n):
                copy(a, 1 + j, (*chip, pc), me).wait_recv()
                fwd[j * n + a].start()

    def finish(srcs, dsts, sems):
        me, sibling, chips, pc, copy, mine, first, passed = tools(srcs, dsts, sems)
        for a in range(n):
            copy(a, 0, sibling, me).wait_recv()
            for j, chip in enumerate(chips):
                copy(a, 4 + j, (*chip, 1 - pc), me).wait_recv()
        for cp in first() + passed():
            cp.wait_send()
        for cp in mine():
            cp.wait()

    return _Comm([it[0] for it in items], [SDS(it[3], it[0].dtype) for it in items],
                 {a: a for a in range(n) if items[a][4]}, _dma_sems(n, 7), [(0.0, start), (mid, forward), (1.0, finish)])


def _gather(items, name):
    return _run_comm(_gather_comm(items), name)


def _reduce_d2d_comm(items):
    n = len(items)

    def copies(gs, gots, sems):
        send_sems, recv_sems, _ = sems
        px, py, pc = lax.axis_index("x"), lax.axis_index("y"), lax.axis_index("c")
        out = []
        for a in range(n):
            _, kind, size, _ = items[a]
            for q in range(N_CHIPS):
                out.append(pltpu.make_async_remote_copy(
                    src_ref=_dev_block(gs[a], kind, 2 * q + 1 - pc, size), dst_ref=gots[a].at[q], send_sem=send_sems.at[a, q],
                    recv_sem=recv_sems.at[a, q], device_id=(px, py, 1 - pc), device_id_type=MESH))
        return out

    def start(gs, gots, sems):
        for cp in copies(gs, gots, sems):
            cp.start()

    def finish(gs, gots, sems):
        for cp in copies(gs, gots, sems):
            cp.wait()

    return _Comm([it[0] for it in items], [SDS((N_CHIPS,) + tuple(it[3]), F32) for it in items], {},
                 _dma_sems(n, N_CHIPS), [(0.0, start), (1.0, finish)])


def _pair_sum(g, got, kind, core, name):
    _, k, n = got.shape
    tr = _pick(k, max(16, STREAM_VMEM // (2 * n * 10)), 16)
    nb = k // tr

    def body(c_ref, g_ref, s_ref, o_ref):
        mine = g_ref[0] if kind == "slot" else g_ref[...]
        o_ref[0] = (mine + s_ref[0]).astype(BF16)

    g_spec = {"slot": pl.BlockSpec((1, tr, n), lambda q, i, c: (2 * q + c[0], i, 0)),
              "row": pl.BlockSpec((tr, n), lambda q, i, c: ((2 * q + c[0]) * nb + i, 0)),
              "col": pl.BlockSpec((tr, n), lambda q, i, c: (i, 2 * q + c[0]))}[kind]
    part = pl.BlockSpec((1, tr, n), lambda q, i, c: (q, i, 0))
    return pl.pallas_call(
        body, out_shape=SDS((N_CHIPS, k, n), BF16),
        grid_spec=pltpu.PrefetchScalarGridSpec(num_scalar_prefetch=1, grid=(N_CHIPS, nb), in_specs=[g_spec, part], out_specs=part),
        name=name, compiler_params=_params(("arbitrary", "arbitrary")),
    )(core, g, got)


def _reduce_ici_comm(parts):
    n = len(parts)

    def copies(ps, rs, sems, arriving):
        send_sems, recv_sems, _ = sems
        px, py, pc = lax.axis_index("x"), lax.axis_index("y"), lax.axis_index("c")
        my_chip = 2 * px + py
        out = []
        for a in range(n):
            for k in range(1, N_CHIPS):
                qx, qy = px ^ (k >> 1), py ^ (k & 1)
                q = 2 * qx + qy
                out.append(pltpu.make_async_remote_copy(
                    src_ref=ps[a].at[q], dst_ref=rs[a].at[q] if arriving else rs[a].at[my_chip], send_sem=send_sems.at[a, k - 1],
                    recv_sem=recv_sems.at[a, k - 1], device_id=(qx, qy, pc), device_id_type=MESH))
        return out

    def start(ps, rs, sems):
        for cp in copies(ps, rs, sems, False):
            cp.start()

    def finish(ps, rs, sems):
        for cp in copies(ps, rs, sems, True):
            cp.wait_recv()
        for cp in copies(ps, rs, sems, False):
            cp.wait_send()

    return _Comm(list(parts), [SDS(p.shape, p.dtype) for p in parts], {}, _dma_sems(n, N_CHIPS - 1),
                 [(0.0, start), (1.0, finish)])


def _adam_update(g, w, m, v):
    c1 = 1.0 - ADAM_B1 ** ADAM_STEP
    c2 = 1.0 - ADAM_B2 ** ADAM_STEP
    nm = ADAM_B1 * m + (1.0 - ADAM_B1) * g
    nv = ADAM_B2 * v + (1.0 - ADAM_B2) * (g * g)
    delta = -ADAM_LR * ((nm / c1) / (jnp.sqrt(nv / c2) + ADAM_EPS) + ADAM_WD * w)
    return delta, nm, nv


def _adamw(parts, recv, w, m, v, layer, prev, chip, name):
    _, R, C = w.shape
    row_bytes = 2 * C * (N_CHIPS * 2 + 7 * 4)
    tr = _pick(R, max(16, STREAM_VMEM // row_bytes), 16)
    n_prev = 0 if prev is None else 4

    def body(ch_ref, own_ref, r1_ref, r2_ref, r3_ref, w_ref, m_ref, v_ref, *rest):
        g_ref, d_ref, nm_ref, nv_ref = rest[n_prev:]
        g = own_ref[0].astype(F32)
        for r_ref in (r1_ref, r2_ref, r3_ref):
            g = g + r_ref[0].astype(F32)
        g_ref[0] = g
        d_ref[0], nm_ref[0], nv_ref[0] = _adam_update(g, w_ref[0], m_ref[0], v_ref[0])

    lay = pl.BlockSpec((1, tr, C), lambda i, ch: (layer, i, 0))
    other = lambda k: pl.BlockSpec((1, tr, C), lambda i, ch: (ch[0] ^ k, i, 0))
    out = SDS(w.shape, F32)
    return pl.pallas_call(
        body, out_shape=(out, out, out, out),
        grid_spec=pltpu.PrefetchScalarGridSpec(
            num_scalar_prefetch=1, grid=(R // tr,),
            in_specs=[pl.BlockSpec((1, tr, C), lambda i, ch: (ch[0], i, 0)), other(2), other(1), other(3), lay, lay, lay]
            + [ANY] * n_prev,
            out_specs=(lay, lay, lay, lay)),
        input_output_aliases={8 + k: k for k in range(n_prev)},
        name=name, compiler_params=_params(("arbitrary",)),
    )(chip, parts, recv, recv, recv, w, m, v, *(prev or ()))


def _small_adamw(gathered, ws, ms, vs):
    n = len(ws)

    def body(*refs):
        g_in, w_in, m_in, v_in = refs[:n], refs[n:2 * n], refs[2 * n:3 * n], refs[3 * n:4 * n]
        outs = refs[4 * n:]
        for i in range(n):
            g = g_in[i][0]
            for dev in range(1, N_DEV):
                g = g + g_in[i][dev]
            d, nm, nv = _adam_update(g, w_in[i][...], m_in[i][...], v_in[i][...])
            outs[i][...] = g
            outs[n + i][...] = d
            outs[2 * n + i][...] = nm
            outs[3 * n + i][...] = nv

    shapes = [SDS(w.shape, F32) for w in ws]
    outs = pl.pallas_call(body, out_shape=shapes * 4, name="small_adamw")(*gathered, *ws, *ms, *vs)
    return outs[:n], outs[n:2 * n], outs[2 * n:3 * n], outs[3 * n:]


W_NAMES = ("norm_mix_w", "norm_mlp_w", "a_w_qkv", "a_b_qkv", "a_sinks", "a_w_o", "a_b_o", "b_in_w", "b_conv_w", "b_conv_b",
           "b_dt_bias", "b_a_log", "b_d", "b_norm_w", "b_out_w", "c_w_qkv", "c_w_o", "mlp_w_up", "mlp_w_down", "final_norm_w")
BIG_KIND = {"a_w_qkv": "slot", "a_w_o": "row", "b_in_w": "slot", "b_out_w": "row", "c_w_qkv": "col", "c_w_o": "row",
            "mlp_w_up": "col", "mlp_w_down": "row"}
SMALL_SHARDED = {"a_b_qkv": 1, "a_b_o": 1, "b_conv_w": 2}
SMALL_REPLICATED = ("norm_mix_w", "norm_mlp_w", "a_sinks", "b_conv_b", "b_dt_bias", "b_a_log", "b_d", "b_norm_w", "final_norm_w")


def _layer_big(i):
    kind, j = i % 3, i // 3
    mix = {0: [("a_w_qkv", j), ("a_w_o", j)], 1: [("b_in_w", 0), ("b_out_w", 0)], 2: [("c_w_qkv", 0), ("c_w_o", 0)]}[kind]
    return mix + [("mlp_w_up", i), ("mlp_w_down", i)]


def _block_size(kind, shard2d):
    return {"slot": None, "row": shard2d[0], "col": shard2d[1]}[kind]


def _full2d(kind, shard2d):
    k, n = shard2d
    return {"slot": (N_DEV, k, n), "row": (N_DEV * k, n), "col": (k, N_DEV * n)}[kind]


def _from_slots(t, ax):
    s = t.shape[1:]
    return jnp.moveaxis(t, 0, ax).reshape(s[:ax] + (N_DEV * s[ax],) + s[ax + 1:])


def _to_slots(g, ax):
    s = g.shape
    return jnp.moveaxis(g.reshape(s[:ax] + (N_DEV, s[ax] // N_DEV) + s[ax + 1:]), ax, 0)


def _rope_tables(positions):
    half = HEAD_DIM // 2
    inv = ROPE_THETA ** (-(jnp.arange(LANES, dtype=jnp.int32) % half).astype(F32) / half)
    ang = positions.astype(F32).reshape(-1, 1) * inv
    return jnp.cos(ang), jnp.sin(ang)


def _swa_fwd(u, h, p, j, B, S, cos, sin, tag, nw, plan=None):
    qkv = _matmul(u, p["a_w_qkv"][j], out_dtype=BF16, bias=p["a_b_qkv"][j][None], rope=(cos, sin),
                  rope_cols=A_Q_DIM + A_KV_DIM, name=f"{tag}_qkv", plan=plan)
    o, lse = _attn_fwd(qkv, B, S, 1, n_heads=A_N_HEADS, n_kv=A_N_KV, q_col=0, k_col=A_Q_DIM, v_col=A_Q_DIM + A_KV_DIM,
                       max_dist=A_WINDOW - 1, sinks=p["a_sinks"][j], name=f"{tag}_attn", plan=plan)
    h1, u2 = _matmul(o, p["a_w_o"][j], bias=p["a_b_o"][j][None], resid=h, norm_out=nw, name=f"{tag}_o")
    return h1, u2, (qkv, o, lse)


def _swa_bwd(dh1, u, saved, p, j, B, S, cos, sin, tag, norm, plan=None):
    qkv, o, lse = saved
    kw = dict(n_heads=A_N_HEADS, n_kv=A_N_KV, q_col=0, k_col=A_Q_DIM, v_col=A_Q_DIM + A_KV_DIM, max_dist=A_WINDOW - 1)
    g = {}
    do = _matmul(dh1, p["a_w_o"][j], tb=True, name=f"{tag}_do")
    g["a_w_o"] = _matmul(o, dh1, ta=True, name=f"{tag}_dwo")
    g["a_b_o"] = _colsum(dh1, f"{tag}_dbo")[0]
    sk = jnp.pad(p["a_sinks"][j], (0, LANES - A_N_HEADS))[None]
    delta, dob, dsink = _delta(do, o, lse, sk, name=f"{tag}_delta")
    g["a_sinks"] = dsink[0, :A_N_HEADS]
    dq = _attn_dq(qkv, dob, lse, delta, cos, sin, B, S, 1, name=f"{tag}_dq", plan=plan, **kw)
    dk, dv = _attn_dkv(qkv, dob, lse, delta, cos, sin, B, S, 1, name=f"{tag}_dkv", plan=plan, **kw)
    dqkv = jnp.concatenate([dq, dk, dv], axis=1)
    g["a_w_qkv"] = _matmul(u, dqkv, ta=True, name=f"{tag}_dwqkv")
    g["a_b_qkv"] = _colsum(dqkv, f"{tag}_dbqkv")[0]
    if plan is not None:
        plan.grads(3 * j, "mix", {"a_w_qkv": g["a_w_qkv"], "a_w_o": g["a_w_o"]})
    dh, dnw = _matmul(dqkv, p["a_w_qkv"][j], tb=True, norm_bwd=(norm[0], norm[1], dh1), name=f"{tag}_du", plan=plan)
    return dh, dnw, g


def _group_cols(gi, qkv):
    W = C_HEADS * HEAD_DIM
    if C_PATTERNS[gi][1] == 1:
        return qkv, (gi * W, (3 + gi) * W, (6 + gi) * W)
    part = jnp.concatenate([qkv[:, (3 * j + gi) * W:(3 * j + gi + 1) * W] for j in range(3)], axis=1)
    return part, (0, W, 2 * W)


def _dil_fwd(u, h, p, B, S, cos, sin, nw, plan=None):
    W = C_HEADS * HEAD_DIM
    qkv = _matmul(u, p["c_w_qkv"][0], out_dtype=BF16, rope=(cos, sin), rope_cols=6 * W, name="c_qkv", plan=plan)
    os_, lses, parts = [], [], []
    for gi, (window, dil) in enumerate(C_PATTERNS):
        part, (qc, kc, vc) = _group_cols(gi, qkv)
        o, lse = _attn_fwd(part, B, S, dil, n_heads=C_HEADS, n_kv=C_HEADS, q_col=qc, k_col=kc, v_col=vc,
                           max_dist=window // dil, sinks=None, name=f"c_attn{gi}", plan=plan)
        os_.append(o)
        lses.append(lse)
        parts.append((part, (qc, kc, vc)))
    o, lse = _merge(os_, lses)
    h1, u2 = _matmul(o, p["c_w_o"][0], resid=h, norm_out=nw, name="c_o")
    return h1, u2, (parts, o, lse)


def _dil_bwd(dh1, u, saved, p, B, S, cos, sin, norm, plan=None):
    parts, o, lse = saved
    g = {}
    do = _matmul(dh1, p["c_w_o"][0], tb=True, name="c_do")
    g["c_w_o"] = _matmul(o, dh1, ta=True, name="c_dwo")[None]
    delta, dob = _delta(do, o, name="c_delta")
    dqs, dks, dvs = [], [], []
    for gi, (window, dil) in enumerate(C_PATTERNS):
        part, (qc, kc, vc) = parts[gi]
        kw = dict(n_heads=C_HEADS, n_kv=C_HEADS, q_col=qc, k_col=kc, v_col=vc, max_dist=window // dil)
        dqs.append(_attn_dq(part, dob, lse, delta, cos, sin, B, S, dil, name=f"c_dq{gi}", **kw))
        dk, dv = _attn_dkv(part, dob, lse, delta, cos, sin, B, S, dil, name=f"c_dkv{gi}", **kw)
        dks.append(dk)
        dvs.append(dv)
    dqkv = jnp.concatenate(dqs + dks + dvs, axis=1)
    g["c_w_qkv"] = _matmul(u, dqkv, ta=True, name="c_dwqkv", plan=plan)[None]
    if plan is not None:
        plan.grads(2, "mix", {"c_w_qkv": g["c_w_qkv"][0], "c_w_o": g["c_w_o"][0]})
    dh, dnw = _matmul(dqkv, p["c_w_qkv"][0], tb=True, norm_bwd=(norm[0], norm[1], dh1), name="c_du", plan=plan)
    return dh, dnw, g


def _ssm_params(p):
    par = jnp.stack([p["b_dt_bias"][0], p["b_a_log"][0], p["b_d"][0]], axis=0)
    prow = par.reshape(3, SSM_N_GROUPS, SSM_HG).transpose(1, 0, 2)
    return prow, prow.transpose(0, 2, 1)


def _mamba_fwd(u, h, p, B, S, nw, plan=None):
    T = B * S
    G, HG = SSM_N_GROUPS, SSM_HG
    w_in = p["b_in_w"][0]
    nzx = SSM_D_INNER + SSM_CONV_DIM
    w_dt = jnp.pad(w_in[:, nzx:], ((0, 0), (0, LANES - SSM_N_HEADS)))
    zx = _matmul(u, w_in[:, :nzx], name="b_zx", plan=plan)
    dtraw = _matmul(u, w_dt, name="b_dt")[:, :SSM_N_HEADS]
    dtc = dtraw.reshape(B, S, G, HG).transpose(0, 2, 1, 3)
    dtr = dtraw.reshape(B, S, G, HG).transpose(0, 2, 3, 1)
    prow, pcol = _ssm_params(p)
    zx3 = zx.reshape(B, S, nzx)
    xc3 = _conv_fwd(zx3, p["b_conv_w"][0], p["b_conv_b"])
    y3, states = _ssd_fwd(xc3, dtc, dtr, prow, pcol, plan=plan)
    y = y3.reshape(T, SSM_D_INNER)
    gn = _gate_fwd(y, zx, p["b_norm_w"])
    h1, u2 = _matmul(gn, p["b_out_w"][0], resid=h, norm_out=nw, name="b_out", plan=plan)
    return h1, u2, (zx, dtc, dtr, xc3, y, states, gn, w_dt)


def _mamba_bwd(dh1, u, saved, p, B, S, norm, plan=None):
    T = B * S
    zx, dtc, dtr, xc3, y, states, gn, w_dt = saved
    nzx = SSM_D_INNER + SSM_CONV_DIM
    w_in = p["b_in_w"][0]
    prow, pcol = _ssm_params(p)
    g = {}
    dgn = _matmul(dh1, p["b_out_w"][0], tb=True, name="b_dgn")
    g["b_out_w"] = _matmul(gn, dh1, ta=True, name="b_dwout")[None]
    dy, dz, dnw = _gate_bwd(dgn, y, zx, p["b_norm_w"])
    g["b_norm_w"] = dnw
    dx3, dB3, dC3, ddt, dpar = _ssd_bwd(xc3, dtc, dtr, prow, pcol, states, dy.reshape(B, S, SSM_D_INNER), plan=plan)
    dpar = dpar.transpose(1, 0, 2).reshape(3, SSM_N_HEADS)
    g["b_dt_bias"], g["b_a_log"], g["b_d"] = dpar[0:1], dpar[1:2], dpar[2:3]
    zx3 = zx.reshape(B, S, nzx)
    cw, cb = p["b_conv_w"][0], p["b_conv_b"]
    parts, dws, dbs = [], [], []
    for col0, dpart, nm in ((0, dx3, "b_conv_bwd_x"), (SSM_D_INNER, dB3, "b_conv_bwd_b"),
                            (SSM_D_INNER + SSM_BC_DIM, dC3, "b_conv_bwd_c")):
        dxp, dw, db = _conv_bwd(zx3, dpart, cw, cb, col0, nm)
        parts.append(dxp.reshape(T, -1))
        dws.append(dw)
        dbs.append(db)
    g["b_conv_w"] = jnp.concatenate(dws, axis=1)[None]
    g["b_conv_b"] = jnp.concatenate(dbs, axis=1)
    dzx = jnp.concatenate([dz] + parts, axis=1)
    ddtraw = ddt.transpose(0, 2, 1, 3).reshape(T, SSM_N_HEADS)
    ddtp = jnp.pad(ddtraw, ((0, 0), (0, LANES - SSM_N_HEADS)))
    dw_zx = _matmul(u, dzx, ta=True, name="b_dwzx")
    dw_dt = _matmul(u, ddtp, ta=True, name="b_dwdt")[:, :SSM_N_HEADS]
    g["b_in_w"] = jnp.concatenate([dw_zx, dw_dt], axis=1)[None]
    if plan is not None:
        plan.grads(1, "mix", {"b_in_w": g["b_in_w"][0], "b_out_w": g["b_out_w"][0]})
    du = _matmul(dzx, w_in[:, :nzx], tb=True, name="b_du_zx", plan=plan)
    dh, dnw = _matmul(ddtp, w_dt, tb=True, resid=du, norm_bwd=(norm[0], norm[1], dh1), name="b_du_dt")
    return dh, dnw, g


def _local_step(x, positions, p, target, plan=None):
    B, S, D = x.shape
    T = B * S
    cos, sin = _rope_tables(positions)
    h = x.reshape(T, D)
    tape = []
    u = _rmsnorm_fwd(h, p["norm_mix_w"][0], "l0_norm_mix")
    for i in range(DEPTH):
        kind, j = i % 3, i // 3
        nw = p["norm_mlp_w"][i]
        if kind == 0:
            h1, u2, saved = _swa_fwd(u, h, p, j, B, S, cos, sin, f"a{j}", nw, plan)
        elif kind == 1:
            h1, u2, saved = _mamba_fwd(u, h, p, B, S, nw, plan)
        else:
            h1, u2, saved = _dil_fwd(u, h, p, B, S, cos, sin, nw, plan)
        r, s = _matmul(u2, p["mlp_w_up"][i], out_dtype=BF16, relu2=True, name=f"l{i}_up", plan=plan)
        if i + 1 < DEPTH:
            h2, u_next = _matmul(s, p["mlp_w_down"][i], resid=h1, norm_out=p["norm_mix_w"][i + 1], name=f"l{i}_down", plan=plan)
        else:
            h2, u_next = _matmul(s, p["mlp_w_down"][i], resid=h1, name=f"l{i}_down", plan=plan), None
        tape.append((h, u, saved, h1, u2, r, s))
        h, u = h2, u_next
    dh, dwf, loss = _final_loss(h, target.reshape(T, D), p["final_norm_w"])
    grads = {"final_norm_w": dwf[0]}
    per_layer = {n: [None] * DEPTH for n in ("norm_mix_w", "norm_mlp_w", "mlp_w_up", "mlp_w_down")}
    a_grads = [None, None]
    for i in reversed(range(DEPTH)):
        kind, j = i % 3, i // 3
        h0, u, saved, h1, u2, r, s = tape[i]
        da = _matmul(dh, p["mlp_w_down"][i], tb=True, out_dtype=BF16, mul=r, mul_scale=2.0, name=f"l{i}_da", plan=plan)
        per_layer["mlp_w_down"][i] = _matmul(s, dh, ta=True, name=f"l{i}_dwdown")
        per_layer["mlp_w_up"][i] = _matmul(u2, da, ta=True, name=f"l{i}_dwup")
        if plan is not None:
            plan.grads(i, "mlp", {"mlp_w_up": per_layer["mlp_w_up"][i], "mlp_w_down": per_layer["mlp_w_down"][i]})
        dh1, dnw = _matmul(da, p["mlp_w_up"][i], tb=True, norm_bwd=(h1, p["norm_mlp_w"][i], dh), name=f"l{i}_du2", plan=plan)
        per_layer["norm_mlp_w"][i] = dnw[0]
        norm = (h0, p["norm_mix_w"][i])
        if kind == 0:
            dh, dnw, g = _swa_bwd(dh1, u, saved, p, j, B, S, cos, sin, f"a{j}", norm, plan)
            a_grads[j] = g
        elif kind == 1:
            dh, dnw, g = _mamba_bwd(dh1, u, saved, p, B, S, norm, plan)
            grads.update(g)
        else:
            dh, dnw, g = _dil_bwd(dh1, u, saved, p, B, S, cos, sin, norm, plan)
            grads.update(g)
        per_layer["norm_mix_w"][i] = dnw[0]
    for n in ("norm_mix_w", "norm_mlp_w"):
        grads[n] = jnp.stack(per_layer[n], axis=0)
    for n in ("mlp_w_up", "mlp_w_down"):
        grads[n] = per_layer[n]
    for n in ("a_b_qkv", "a_sinks", "a_b_o"):
        grads[n] = jnp.stack([a_grads[0][n], a_grads[1][n]], axis=0)
    for n in ("a_w_qkv", "a_w_o"):
        grads[n] = [a_grads[0][n], a_grads[1][n]]
    for n in ("b_in_w", "b_out_w", "c_w_qkv", "c_w_o"):
        grads[n] = [grads[n][0]]
    return loss, dh.reshape(B, S, D), grads


MIX = {0: ("a_w_qkv", "a_w_o"), 1: ("b_in_w", "b_out_w"), 2: ("c_w_qkv", "c_w_o")}
MLP = ("mlp_w_up", "mlp_w_down")
GATHER_FIRST = (0, MIX[0])
GATHER_HOSTS = {"a0_qkv": ((0, ("mlp_w_up",)),), "a0_attn": ((0, ("mlp_w_down",)),), "l0_up": ((1, ("b_in_w",)),),
                "l0_down": ((1, ("b_out_w",)),), "b_zx": ((1, ("mlp_w_up",)),),
                "b_ssd_fwd": ((1, ("mlp_w_down",)), (2, ("c_w_qkv",))), "b_out": ((2, ("c_w_o",)),),
                "l1_up": ((2, ("mlp_w_up",)),), "l1_down": ((2, ("mlp_w_down",)),),
                "c_qkv": ((3, MLP),), "c_attn1": ((3, MIX[0]),)}
GATHER_LONG_HOSTS = ("b_ssd_fwd", "c_qkv")
REDUCE_HOSTS = {(3, "mlp"): ("l3_du2", "a1_dkv"), (3, "mix"): ("a1_du", "l2_da"),
                (2, "mlp"): ("l2_du2", "c_dwqkv"), (2, "mix"): ("c_du", "b_ssd_bwd"),
                (1, "mlp"): ("l1_du2", "b_ssd_bwd"), (1, "mix"): ("b_du_zx", {"b_in_w": "a0_dq", "b_out_w": "l0_da"}),
                (0, "mlp"): ("l0_du2", "a0_dkv"), (0, "mix"): (None, None)}


class _Plan:
    def __init__(self, w, m, v, p, dev, chip, core):
        self.w, self.m, self.v, self.p, self.dev, self.chip, self.core = w, m, v, p, dev, chip, core
        self.pending = {}
        self.res = {n: None for n in BIG_KIND}
        self._install(*GATHER_FIRST)(_gather(self._gather_items(*GATHER_FIRST), "gather_first"))
        for host, groups in GATHER_HOSTS.items():
            for i, only in groups:
                mid = 0.7 if host in GATHER_LONG_HOSTS else 0.9
                self._wait_for(host, _gather_comm(self._gather_items(i, only), mid), self._install(i, only))

    def _wait_for(self, host, comm, done):
        self.pending.setdefault(host, []).append((comm, done))

    def _names(self, i, only):
        return [(n, l) for n, l in _layer_big(i) if only is None or n in only]

    def _gather_items(self, i, only):
        items = []
        for n, l in self._names(i, only):
            kind, s2 = BIG_KIND[n], self.w[n].shape[1:]
            placed = _place(self.w[n], l, kind, _full2d(kind, s2), self.dev, f"place_l{i}_{n}")
            items.append((placed, kind, _block_size(kind, s2), _full2d(kind, s2), True))
        return items

    def _install(self, i, only):
        def done(fulls):
            for (n, l), t in zip(self._names(i, only), fulls):
                self.p[n][l] = _from_slots(t, 1) if BIG_KIND[n] == "slot" else t
        return done

    def take(self, host):
        return _Comm.merge([c for c, _ in self.pending[host]]) if host in self.pending else None

    def give(self, host, results):
        for comm, done in self.pending.pop(host):
            done(results[:len(comm.out_shapes)])
            results = results[len(comm.out_shapes):]

    def grads(self, i, group, grads):
        names = self._names(i, MLP if group == "mlp" else MIX[i % 3])
        items = []
        for n, _ in names:
            kind, s2 = BIG_KIND[n], self.w[n].shape[1:]
            items.append((_to_slots(grads[n], 1) if kind == "slot" else grads[n], kind, _block_size(kind, s2), s2))
        d2d_host, ici_host = REDUCE_HOSTS[(i, group)]
        tag = f"l{i}_{group}"

        def update(sel, parts):
            def done(recv):
                for (n, l), pt, r in zip(sel, parts, recv):
                    self.res[n] = _adamw(pt, r, self.w[n], self.m[n], self.v[n], l, self.res[n], self.chip, f"adamw_l{i}_{n}")
            return done

        def second(sib):
            parts = [_pair_sum(it[0], s, it[1], self.core, f"pair_sum_l{i}_{n}") for (n, _), it, s in zip(names, items, sib)]
            hosts = ici_host if isinstance(ici_host, dict) else {n: ici_host for n, _ in names}
            for h in dict.fromkeys(hosts[n] for n, _ in names):
                ks = [k for k, (n, _) in enumerate(names) if hosts[n] == h]
                sel, pts = [names[k] for k in ks], [parts[k] for k in ks]
                self._send(h, _reduce_ici_comm(pts), update(sel, pts), f"reduce_ici_{tag}_{sel[0][0]}")

        self._send(d2d_host, _reduce_d2d_comm(items), second, f"reduce_d2d_{tag}")

    def _send(self, host, comm, done, name):
        if host is None:
            done(_run_comm(comm, name))
        else:
            self._wait_for(host, comm, done)

    def flush(self):
        late = 0
        while self.pending:
            host = next(iter(self.pending))
            for comm, done in self.pending.pop(host):
                done(_run_comm(comm, f"late_{late}_{host}"))
                late += 1


def kernel(x, positions, norm_mix_w, norm_mlp_w, a_w_qkv, a_b_qkv, a_sinks, a_w_o, a_b_o, b_in_w, b_conv_w, b_conv_b, b_dt_bias, b_a_log, b_d, b_norm_w, b_out_w, c_w_qkv, c_w_o, mlp_w_up, mlp_w_down, final_norm_w, loss_target, m_norm_mix_w, m_norm_mlp_w, m_a_w_qkv, m_a_b_qkv, m_a_sinks, m_a_w_o, m_a_b_o, m_b_in_w, m_b_conv_w, m_b_conv_b, m_b_dt_bias, m_b_a_log, m_b_d, m_b_norm_w, m_b_out_w, m_c_w_qkv, m_c_w_o, m_mlp_w_up, m_mlp_w_down, m_final_norm_w, v_norm_mix_w, v_norm_mlp_w, v_a_w_qkv, v_a_b_qkv, v_a_sinks, v_a_w_o, v_a_b_o, v_b_in_w, v_b_conv_w, v_b_conv_b, v_b_dt_bias, v_b_a_log, v_b_d, v_b_norm_w, v_b_out_w, v_c_w_qkv, v_c_w_o, v_mlp_w_up, v_mlp_w_down, v_final_norm_w):
    w = dict(zip(W_NAMES, (norm_mix_w, norm_mlp_w, a_w_qkv, a_b_qkv, a_sinks, a_w_o, a_b_o, b_in_w, b_conv_w, b_conv_b,
                           b_dt_bias, b_a_log, b_d, b_norm_w, b_out_w, c_w_qkv, c_w_o, mlp_w_up, mlp_w_down, final_norm_w)))
    m = dict(zip(W_NAMES, (m_norm_mix_w, m_norm_mlp_w, m_a_w_qkv, m_a_b_qkv, m_a_sinks, m_a_w_o, m_a_b_o, m_b_in_w,
                           m_b_conv_w, m_b_conv_b, m_b_dt_bias, m_b_a_log, m_b_d, m_b_norm_w, m_b_out_w, m_c_w_qkv, m_c_w_o,
                           m_mlp_w_up, m_mlp_w_down, m_final_norm_w)))
    v = dict(zip(W_NAMES, (v_norm_mix_w, v_norm_mlp_w, v_a_w_qkv, v_a_b_qkv, v_a_sinks, v_a_w_o, v_a_b_o, v_b_in_w,
                           v_b_conv_w, v_b_conv_b, v_b_dt_bias, v_b_a_log, v_b_d, v_b_norm_w, v_b_out_w, v_c_w_qkv, v_c_w_o,
                           v_mlp_w_up, v_mlp_w_down, v_final_norm_w)))
    px, py, pc = lax.axis_index("x"), lax.axis_index("y"), lax.axis_index("c")
    me = 4 * px + 2 * py + pc
    dev, chip, core = (t.astype(jnp.int32).reshape(1) for t in (me, 2 * px + py, pc))

    trio = tuple(SMALL_SHARDED)
    got = _gather([(d[n], "slot", None, (N_DEV,) + d[n].shape, False) for n in trio for d in (w, m, v)], "gather_small")
    slots = {n: got[3 * i:3 * i + 3] for i, n in enumerate(trio)}
    p = {n: w[n] for n in SMALL_REPLICATED}
    for n in trio:
        p[n] = _from_slots(slots[n][0], SMALL_SHARDED[n])
    for n in BIG_KIND:
        p[n] = [None] * w[n].shape[0]
    plan = _Plan(w, m, v, p, dev, chip, core)
    loss_part, dx, grads = _local_step(x, positions, p, loss_target, plan)
    loss = lax.psum(loss_part[0, 0], AXES)
    plan.flush()
    out = {n: list(plan.res[n]) for n in BIG_KIND}

    small = SMALL_REPLICATED + trio
    as2d = lambda t: t.reshape(1, -1) if t.ndim == 1 else t
    g_sm = [as2d(grads[n]) for n in SMALL_REPLICATED] + [_to_slots(grads[n].reshape(p[n].shape), SMALL_SHARDED[n]) for n in trio]
    gathered = _gather([(g, "slot", None, (N_DEV,) + g.shape, False) for g in g_sm], "gather_small_grads")
    ws = [as2d(w[n]) for n in SMALL_REPLICATED] + [slots[n][0] for n in trio]
    ms = [as2d(m[n]) for n in SMALL_REPLICATED] + [slots[n][1] for n in trio]
    vs = [as2d(v[n]) for n in SMALL_REPLICATED] + [slots[n][2] for n in trio]
    sm_out = _small_adamw(gathered, ws, ms, vs)
    for i, n in enumerate(small):
        if n in SMALL_SHARDED:
            out[n] = [lax.dynamic_index_in_dim(sm_out[k][i], me, 0, keepdims=False) for k in range(4)]
        else:
            out[n] = [sm_out[k][i].reshape(w[n].shape) for k in range(4)]
    return (loss, dx, *[out[n][0] for n in W_NAMES], *[out[n][1] for n in W_NAMES], *[out[n][2] for n in W_NAMES],
            *[out[n][3] for n in W_NAMES])
```

```python
import math

import jax
import jax.numpy as jnp
from jax import lax
from jax.experimental import pallas as pl
from jax.experimental.pallas import tpu as pltpu

F32 = jnp.float32
BF16 = jnp.bfloat16
SDS = jax.ShapeDtypeStruct

D_MODEL = 1024
DEPTH = 4
BLOCK = 128
ROPE_THETA = 10000.0
NORM_EPS = 1e-5
HEAD_DIM = 64
A_N_HEADS = 16
A_N_KV = 2
A_WINDOW = 128
A_Q_DIM = 1024
A_KV_DIM = 128
SSM_D_INNER = 2048
SSM_N_HEADS = 32
SSM_N_GROUPS = 8
SSM_HG = 4
SSM_D_STATE = 128
SSM_CONV = 4
SSM_CHUNK = 128
SSM_BC_DIM = 1024
SSM_CONV_DIM = 4096
C_PATTERNS = ((128, 1), (512, 4), (2048, 16))
C_HEADS = 16
ADAM_LR, ADAM_B1, ADAM_B2, ADAM_EPS, ADAM_WD, ADAM_STEP = 0.001, 0.9, 0.999, 1e-08, 0.01, 10

N_DEV = 8
AXES = ("x", "y", "c")
LANES = 128
VMEM_LIMIT = 56 * 1024 * 1024
STREAM_VMEM = 16 * 1024 * 1024
NEG = -1e30

NN = (((1,), (0,)), ((), ()))
NT = (((1,), (1,)), ((), ()))
TN = (((0,), (0,)), ((), ()))
HI = lax.Precision.HIGHEST


def _pick(n, cap, mult=LANES):
    best = None
    for t in range(mult, min(n, cap) + 1, mult):
        if n % t == 0:
            best = t
    return best if best is not None else n


def _params(sem):
    return pltpu.CompilerParams(dimension_semantics=sem, vmem_limit_bytes=VMEM_LIMIT)


def _bf(x):
    return x if x.dtype == BF16 else x.astype(BF16)


def _rot_half(y):
    n = y.shape[-1]
    lane = lax.broadcasted_iota(jnp.int32, y.shape, y.ndim - 1)
    return jnp.where((lane % HEAD_DIM) < HEAD_DIM // 2, -pltpu.roll(y, n - 32, y.ndim - 1), pltpu.roll(y, 32, y.ndim - 1))


def _rope(y, cos, sin, sign):
    reps = y.shape[-1] // LANES
    c = jnp.tile(cos, (1, reps)) if reps > 1 else cos
    s = jnp.tile(sin, (1, reps)) if reps > 1 else sin
    return y * c + sign * (_rot_half(y) * s)


MESH = pl.DeviceIdType.MESH
ANY = pl.BlockSpec(memory_space=pl.ANY)


class _Comm:
    def __init__(self, inputs, out_shapes, aliases, sems, phases):
        self.inputs, self.out_shapes, self.aliases, self.sems, self.phases = inputs, out_shapes, aliases, sems, phases

    @staticmethod
    def merge(comms):
        if len(comms) == 1:
            return comms[0]
        ins, outs, aliases, sems, spans = [], [], {}, [], []
        for c in comms:
            aliases.update({len(ins) + i: len(outs) + j for i, j in c.aliases.items()})
            spans.append((len(ins), len(ins) + len(c.inputs), len(outs), len(outs) + len(c.out_shapes), len(sems),
                          len(sems) + len(c.sems)))
            ins, outs, sems = ins + list(c.inputs), outs + list(c.out_shapes), sems + list(c.sems)
        phases = []
        for f in sorted({f for c in comms for f, _ in c.phases}):
            todo = [(fn, sp) for c, sp in zip(comms, spans) for g, fn in c.phases if g == f]

            def run(cins, couts, csems, todo=todo):
                for fn, (i0, i1, o0, o1, s0, s1) in todo:
                    fn(cins[i0:i1], couts[o0:o1], csems[s0:s1])
            phases.append((f, run))
        return _Comm(ins, outs, aliases, sems, phases)


def _pc(body, args, *, out_shape, grid, in_specs, out_specs, name, sem, scratch_shapes=(), comm=None):
    single = not isinstance(out_shape, (tuple, list))
    outs, ospecs = ([out_shape], [out_specs]) if single else (list(out_shape), list(out_specs))
    unpack = (lambda r: r[0]) if single else (lambda r: tuple(r))
    if comm is None:
        res = pl.pallas_call(body, out_shape=outs, grid=grid, in_specs=list(in_specs), out_specs=ospecs,
                             scratch_shapes=list(scratch_shapes), name=name, compiler_params=_params(sem))(*args)
        return unpack(res)
    n_in, n_out, n_scr = len(in_specs), len(outs), len(scratch_shapes)
    c_in, c_out = len(comm.inputs), len(comm.out_shapes)
    total = math.prod(grid)
    steps = [min(total - 1, int(f * total)) for f, _ in comm.phases[:-1]]

    def wrapped(*refs):
        ins, cins = refs[:n_in], refs[n_in:n_in + c_in]
        o = refs[n_in + c_in:n_in + c_in + n_out]
        couts = refs[n_in + c_in + n_out:n_in + c_in + n_out + c_out]
        rest = refs[n_in + c_in + n_out + c_out:]
        scr, csems = rest[:n_scr], rest[n_scr:]
        step = pl.program_id(0)
        for ax in range(1, len(grid)):
            step = step * grid[ax] + pl.program_id(ax)
        for (_, fn), st in zip(comm.phases[:-1], steps):
            @pl.when(step == st)
            def _(fn=fn):
                fn(cins, couts, csems)
        body(*ins, *o, *scr)

        @pl.when(step == total - 1)
        def _():
            comm.phases[-1][1](cins, couts, csems)

    res = pl.pallas_call(
        wrapped, out_shape=outs + list(comm.out_shapes), grid=grid, in_specs=list(in_specs) + [ANY] * c_in,
        out_specs=ospecs + [ANY] * c_out, scratch_shapes=list(scratch_shapes) + list(comm.sems),
        input_output_aliases={n_in + i: n_out + j for i, j in comm.aliases.items()}, name=name,
        compiler_params=_params(("arbitrary",) * len(grid)),
    )(*args, *comm.inputs)
    return unpack(res[:n_out]), list(res[n_out:])


def _hosted(plan, name, run):
    comm = plan.take(name) if plan is not None else None
    if comm is None:
        return run(None)
    res, extra = run(comm)
    plan.give(name, extra)
    return res


MM_VMEM = 40 * 1024 * 1024
HBM_BYTES_PER_US = 2.5e6
STEP_US = 0.35


def _divisors(n, cands):
    return [c for c in cands if c <= n and n % c == 0] or [n]


def _mm_tiles(M, N, K, sa, sb, out_bytes, extra_bytes, full_rows=False):
    best = None
    for tm in _divisors(M, (2048, 1024, 512, 256)):
        for tn in ([N] if full_rows else _divisors(N, (1024, 640, 512, 256, 128))):
            for tk in _divisors(K, (K, K // 2, K // 3, K // 4, 2048, 1024, 640, 512)):
                if tk != K and tk % LANES:
                    continue
                nk = K // tk
                vmem = 2 * tm * tk * sa + 2 * tk * tn * sb + tm * tn * (2 * (out_bytes + extra_bytes) + 8 + (4 if nk > 1 else 0))
                if vmem > MM_VMEM:
                    continue
                a_traffic = M * K * sa * (1 if nk == 1 else N // tn)
                b_traffic = K * N * sb * (1 if (nk == 1 and N == tn) else M // tm)
                steps = (M // tm) * (N // tn) * nk
                cost = (a_traffic + b_traffic + M * N * (out_bytes + extra_bytes)) / HBM_BYTES_PER_US + steps * STEP_US
                cost += (M // tm) * (N // tn) * (nk - 1) * tm * tn * 8 / (4 * HBM_BYTES_PER_US)
                if best is None or cost < best[0]:
                    best = (cost, tm, tn, tk)
    assert best is not None, (M, N, K)
    return best[1:]


def _matmul(a, b, *, ta=False, tb=False, out_dtype=F32, bias=None, resid=None, mul=None, mul_scale=1.0,
            relu2=False, rope=None, rope_cols=0, norm_out=None, norm_bwd=None, name="mm", plan=None):
    M = a.shape[1] if ta else a.shape[0]
    K = a.shape[0] if ta else a.shape[1]
    N = b.shape[0] if tb else b.shape[1]
    assert (b.shape[1] if tb else b.shape[0]) == K
    two_out = relu2 or norm_out is not None
    out_bytes = jnp.dtype(out_dtype).itemsize * (2 if relu2 else 1) + (2 if norm_out is not None else 0)
    extra_bytes = (4 if resid is not None else 0) + (mul.dtype.itemsize if mul is not None else 0) + (8 if norm_bwd else 0)
    rows = norm_out is not None or norm_bwd is not None
    tm, tn, tk = _mm_tiles(M, N, K, a.dtype.itemsize, b.dtype.itemsize, out_bytes, extra_bytes, full_rows=rows)
    nk = K // tk
    dims = (((0 if ta else 1,), (1 if tb else 0,)), ((), ()))

    def body(*refs):
        it = iter(refs)
        a_ref, b_ref = next(it), next(it)
        bias_ref = next(it) if bias is not None else None
        resid_ref = next(it) if resid is not None else None
        mul_ref = next(it) if mul is not None else None
        cos_ref, sin_ref = (next(it), next(it)) if rope is not None else (None, None)
        nw_ref = next(it) if rows else None
        h_ref, dres_ref = (next(it), next(it)) if norm_bwd is not None else (None, None)
        o_ref = next(it)
        o2_ref = next(it) if two_out or norm_bwd is not None else None
        acc_ref = next(it) if nk > 1 else None
        k = pl.program_id(2)
        part = lax.dot_general(_bf(a_ref[...]), _bf(b_ref[...]), dims, preferred_element_type=F32)
        if nk > 1:
            @pl.when(k == 0)
            def _():
                acc_ref[...] = part

            @pl.when(k > 0)
            def _():
                acc_ref[...] += part

        @pl.when(k == nk - 1)
        def _():
            y = acc_ref[...] if nk > 1 else part
            if bias_ref is not None:
                y = y + bias_ref[...]
            if rope is not None and rope_cols % tn == 0 and not (two_out or rows or mul is not None or resid is not None):
                rotated = pl.program_id(1) * tn < rope_cols

                @pl.when(rotated)
                def _():
                    o_ref[...] = _rope(y, cos_ref[...], sin_ref[...], 1.0).astype(o_ref.dtype)

                @pl.when(jnp.logical_not(rotated))
                def _():
                    o_ref[...] = y.astype(o_ref.dtype)
                return
            if rope is not None:
                col = pl.program_id(1) * tn + lax.broadcasted_iota(jnp.int32, y.shape, 1)
                y = jnp.where(col < rope_cols, _rope(y, cos_ref[...], sin_ref[...], 1.0), y)
            if mul_ref is not None:
                y = y * (mul_ref[...].astype(F32) * mul_scale)
            if resid_ref is not None:
                y = y + resid_ref[...]
            if relu2:
                r = jnp.maximum(y, 0.0)
                o_ref[...] = r.astype(o_ref.dtype)
                o2_ref[...] = (r * r).astype(o2_ref.dtype)
            elif norm_bwd is not None:
                x = h_ref[...]
                rstd = lax.rsqrt(jnp.mean(x * x, axis=-1, keepdims=True) + NORM_EPS)
                g = y * nw_ref[...]
                o_ref[...] = dres_ref[...] + rstd * g - x * (rstd * rstd * rstd) * jnp.mean(g * x, axis=-1, keepdims=True)
                dw = jnp.sum(y * x * rstd, axis=0, keepdims=True)
                first = pl.program_id(0) == 0

                @pl.when(first)
                def _():
                    o2_ref[...] = dw

                @pl.when(jnp.logical_not(first))
                def _():
                    o2_ref[...] += dw
            else:
                o_ref[...] = y.astype(o_ref.dtype)
                if norm_out is not None:
                    rstd = lax.rsqrt(jnp.mean(y * y, axis=-1, keepdims=True) + NORM_EPS)
                    o2_ref[...] = (y * rstd * nw_ref[...]).astype(BF16)

    a_spec = pl.BlockSpec((tk, tm), lambda i, j, k: (k, i)) if ta else pl.BlockSpec((tm, tk), lambda i, j, k: (i, k))
    b_spec = pl.BlockSpec((tn, tk), lambda i, j, k: (j, k)) if tb else pl.BlockSpec((tk, tn), lambda i, j, k: (k, j))
    mn_spec = pl.BlockSpec((tm, tn), lambda i, j, k: (i, j))
    in_specs, args = [a_spec, b_spec], [a, b]
    if bias is not None:
        in_specs.append(pl.BlockSpec((1, tn), lambda i, j, k: (0, j)))
        args.append(bias)
    if resid is not None:
        in_specs.append(mn_spec)
        args.append(resid)
    if mul is not None:
        in_specs.append(mn_spec)
        args.append(mul)
    if rope is not None:
        in_specs += [pl.BlockSpec((tm, LANES), lambda i, j, k: (i, 0))] * 2
        args += [rope[0], rope[1]]
    vec_spec = pl.BlockSpec((1, tn), lambda i, j, k: (0, j))
    if rows:
        in_specs.append(vec_spec)
        args.append((norm_out if norm_out is not None else norm_bwd[1]).reshape(1, N))
    if norm_bwd is not None:
        in_specs += [mn_spec, mn_spec]
        args += [norm_bwd[0], norm_bwd[2]]
    out_shape = SDS((M, N), out_dtype)
    out_specs = mn_spec
    if relu2:
        out_shape, out_specs = (out_shape, out_shape), (mn_spec, mn_spec)
    elif norm_out is not None:
        out_shape, out_specs = (out_shape, SDS((M, N), BF16)), (mn_spec, mn_spec)
    elif norm_bwd is not None:
        out_shape, out_specs = (out_shape, SDS((1, N), F32)), (mn_spec, vec_spec)
    sem = ("arbitrary",) * 3 if norm_bwd is not None else ("parallel", "parallel", "arbitrary")
    return _hosted(plan, name, lambda comm: _pc(
        body, args, out_shape=out_shape, grid=(M // tm, N // tn, nk), in_specs=in_specs, out_specs=out_specs,
        scratch_shapes=[pltpu.VMEM((tm, tn), F32)] if nk > 1 else [], name=name, sem=sem, comm=comm))


def _colsum(x, name):
    T, N = x.shape
    tm = _pick(T, 1024, 8)

    def body(x_ref, o_ref):
        s = jnp.sum(x_ref[...].astype(F32), axis=0, keepdims=True)

        @pl.when(pl.program_id(0) == 0)
        def _():
            o_ref[...] = s

        @pl.when(pl.program_id(0) > 0)
        def _():
            o_ref[...] += s

    return pl.pallas_call(
        body, out_shape=SDS((1, N), F32), grid=(T // tm,),
        in_specs=[pl.BlockSpec((tm, N), lambda i: (i, 0))], out_specs=pl.BlockSpec((1, N), lambda i: (0, 0)),
        name=name, compiler_params=_params(("arbitrary",)),
    )(x)


def _rmsnorm_fwd(h, w, name):
    T, D = h.shape
    tm = _pick(T, 512, 8)

    def body(h_ref, w_ref, o_ref):
        x = h_ref[...]
        rstd = lax.rsqrt(jnp.mean(x * x, axis=-1, keepdims=True) + NORM_EPS)
        o_ref[...] = (x * rstd * w_ref[...]).astype(BF16)

    return pl.pallas_call(
        body, out_shape=SDS((T, D), BF16), grid=(T // tm,),
        in_specs=[pl.BlockSpec((tm, D), lambda i: (i, 0)), pl.BlockSpec((1, D), lambda i: (0, 0))],
        out_specs=pl.BlockSpec((tm, D), lambda i: (i, 0)), name=name, compiler_params=_params(("parallel",)),
    )(h, w.reshape(1, D))


def _final_loss(h, target, w):
    T, D = h.shape
    tm = _pick(T, 512, 8)

    def body(h_ref, t_ref, w_ref, dh_ref, dw_ref, loss_ref):
        x = h_ref[...]
        rstd = lax.rsqrt(jnp.mean(x * x, axis=-1, keepdims=True) + NORM_EPS)
        xn = x * rstd
        err = xn * w_ref[...] - t_ref[...]
        part = 0.5 * jnp.sum(jnp.mean(err * err, axis=-1, keepdims=True), axis=0, keepdims=True)
        dy = err * (1.0 / D)
        g = dy * w_ref[...]
        dh_ref[...] = rstd * g - x * (rstd * rstd * rstd) * jnp.mean(g * x, axis=-1, keepdims=True)
        dw = jnp.sum(dy * xn, axis=0, keepdims=True)
        lp = jnp.broadcast_to(part, (1, LANES))

        @pl.when(pl.program_id(0) == 0)
        def _():
            dw_ref[...] = dw
            loss_ref[...] = lp

        @pl.when(pl.program_id(0) > 0)
        def _():
            dw_ref[...] += dw
            loss_ref[...] += lp

    row = pl.BlockSpec((tm, D), lambda i: (i, 0))
    vec = pl.BlockSpec((1, D), lambda i: (0, 0))
    return pl.pallas_call(
        body, out_shape=(SDS((T, D), F32), SDS((1, D), F32), SDS((1, LANES), F32)), grid=(T // tm,),
        in_specs=[row, row, vec], out_specs=(row, vec, pl.BlockSpec((1, LANES), lambda i: (0, 0))),
        name="final_loss", compiler_params=_params(("arbitrary",)),
    )(h, target, w.reshape(1, D))


def _band_mask(i_blk, max_dist, first_ok):
    qi = lax.broadcasted_iota(jnp.int32, (BLOCK, 2 * BLOCK), 0)
    kj = lax.broadcasted_iota(jnp.int32, (BLOCK, 2 * BLOCK), 1)
    dist = qi + BLOCK - kj
    ok = (dist >= 0) & (dist <= max_dist)
    return ok & ((kj >= BLOCK) | first_ok)


def _pair(t, i):
    return t[:, LANES * i:LANES * (i + 1)]


def _low_half(shape):
    return lax.broadcasted_iota(jnp.int32, shape, len(shape) - 1) < HEAD_DIM


def _stack_heads(t):
    lo = _low_half(t.shape)
    z = jnp.zeros_like(t)
    return jnp.concatenate([jnp.where(lo, t, z), jnp.where(lo, z, t)], axis=0)


def _swap_halves(t):
    return jnp.concatenate([t[:, HEAD_DIM:], t[:, :HEAD_DIM]], axis=1)


def _kv_operand(kv, kv_swapped, h0, n_kv, n_heads):
    R = n_heads // n_kv
    if R == 1:
        return _pair(kv, h0 // 2)
    assert kv.shape[1] == LANES and R % 2 == 0, "grouped queries: one 128-lane tile of kv heads, both heads of a pair in one group"
    g = h0 // R
    t, ts = _pair(kv, g // 2), _pair(kv_swapped, g // 2)
    lo = _low_half(t.shape)
    return jnp.where(lo, t, ts) if g % 2 == 0 else jnp.where(lo, ts, t)


def _lane_place(cols):
    m = cols[0].shape[0]
    lane = lax.broadcasted_iota(jnp.int32, (m, LANES), 1)
    out = jnp.zeros((m, LANES), F32)
    for h, c in enumerate(cols):
        out = jnp.where(lane == h, c, out)
    return out


def _attn_specs(B, S, d, C, n_heads, n_kv, q_col, k_col, v_col):
    kvw = n_kv * HEAD_DIM
    qw = n_heads * HEAD_DIM
    cq, ck = (C // qw if d > 1 else 0), (C // kvw if d > 1 else 0)
    q_spec = pl.BlockSpec((1, BLOCK, qw), lambda b, r, i: (b, i, r * cq + q_col // qw))
    kc = pl.BlockSpec((1, BLOCK, kvw), lambda b, r, i: (b, i, r * ck + k_col // kvw))
    kp = pl.BlockSpec((1, BLOCK, kvw), lambda b, r, i: (b, jnp.maximum(i - 1, 0), r * ck + k_col // kvw))
    vc = pl.BlockSpec((1, BLOCK, kvw), lambda b, r, i: (b, i, r * ck + v_col // kvw))
    vp = pl.BlockSpec((1, BLOCK, kvw), lambda b, r, i: (b, jnp.maximum(i - 1, 0), r * ck + v_col // kvw))
    return q_spec, kp, kc, vp, vc


def _attn_fwd(qkv, B, S, d, *, n_heads, n_kv, q_col, k_col, v_col, max_dist, sinks, name, plan=None):
    C = qkv.shape[1]
    Ls = S // d
    nb = Ls // BLOCK
    qw = n_heads * HEAD_DIM
    R = n_heads // n_kv
    qkv3 = qkv.reshape(B, Ls, d * C)
    scale = HEAD_DIM ** -0.5

    def body(*refs):
        if sinks is not None:
            sink_ref, q_ref, kp_ref, kc_ref, vp_ref, vc_ref, o_ref, lse_ref = refs
        else:
            q_ref, kp_ref, kc_ref, vp_ref, vc_ref, o_ref, lse_ref = refs
        i = pl.program_id(2)
        mask1 = _band_mask(i, max_dist, i > 0)
        mask = jnp.concatenate([mask1, mask1], axis=0)
        q = q_ref[0]
        kk = jnp.concatenate([kp_ref[0], kc_ref[0]], axis=0)
        vv = jnp.concatenate([vp_ref[0], vc_ref[0]], axis=0)
        kks, vvs = (_swap_halves(kk), _swap_halves(vv)) if R > 1 else (None, None)
        lo = _low_half((BLOCK, LANES))
        top = lax.broadcasted_iota(jnp.int32, (2 * BLOCK, 1), 0) < BLOCK
        lses, tiles = [], []
        for t in range(n_heads // 2):
            k2 = _kv_operand(kk, kks, 2 * t, n_kv, n_heads)
            v2 = _kv_operand(vv, vvs, 2 * t, n_kv, n_heads)
            s = lax.dot_general(_stack_heads(_pair(q, t)), k2, NT, preferred_element_type=F32) * scale
            s = jnp.where(mask, s, NEG)
            m = jnp.max(s, axis=-1, keepdims=True)
            if sinks is not None:
                sk = jnp.where(top, sink_ref[2 * t], sink_ref[2 * t + 1])
                m = jnp.maximum(m, sk)
            p = jnp.exp(s - m)
            den = jnp.sum(p, axis=-1, keepdims=True)
            if sinks is not None:
                den = den + jnp.exp(sk - m)
            lse2 = m + jnp.log(den)
            o2 = jnp.dot((p / den).astype(BF16), v2, preferred_element_type=F32)
            tiles.append(jnp.where(lo, o2[:BLOCK], o2[BLOCK:]))
            lses += [lse2[:BLOCK], lse2[BLOCK:]]
        o_ref[0] = jnp.concatenate(tiles, axis=-1)
        lse_ref[0] = _lane_place(lses)

    specs = list(_attn_specs(B, S, d, C, n_heads, n_kv, q_col, k_col, v_col))
    args = [qkv3] * 5
    if sinks is not None:
        specs = [pl.BlockSpec(memory_space=pltpu.SMEM)] + specs
        args = [sinks] + args
    o3, lse3 = _hosted(plan, name, lambda comm: _pc(
        body, args, out_shape=(SDS((B, Ls, d * qw), F32), SDS((B, Ls, d * LANES), F32)), grid=(B, d, nb), in_specs=specs,
        out_specs=(pl.BlockSpec((1, BLOCK, qw), lambda b, r, i: (b, i, r)), pl.BlockSpec((1, BLOCK, LANES), lambda b, r, i: (b, i, r))),
        name=name, sem=("parallel", "parallel", "parallel"), comm=comm))
    return o3.reshape(B * S, qw), lse3.reshape(B * S, LANES)


def _attn_dq(qkv, do, lse, delta, cos, sin, B, S, d, *, n_heads, n_kv, q_col, k_col, v_col, max_dist, name, plan=None):
    C = qkv.shape[1]
    Ls = S // d
    nb = Ls // BLOCK
    qw = n_heads * HEAD_DIM
    R = n_heads // n_kv
    scale = HEAD_DIM ** -0.5

    def body(q_ref, kp_ref, kc_ref, vp_ref, vc_ref, do_ref, lse_ref, dl_ref, cos_ref, sin_ref, dq_ref):
        i = pl.program_id(2)
        mask1 = _band_mask(i, max_dist, i > 0)
        mask = jnp.concatenate([mask1, mask1], axis=0)
        q = q_ref[0]
        do_ = do_ref[0]
        kk = jnp.concatenate([kp_ref[0], kc_ref[0]], axis=0)
        vv = jnp.concatenate([vp_ref[0], vc_ref[0]], axis=0)
        kks, vvs = (_swap_halves(kk), _swap_halves(vv)) if R > 1 else (None, None)
        lo = _low_half((BLOCK, LANES))
        lse_t, dl_t = lse_ref[0], dl_ref[0]
        tiles = []
        for t in range(n_heads // 2):
            k2 = _kv_operand(kk, kks, 2 * t, n_kv, n_heads)
            v2 = _kv_operand(vv, vvs, 2 * t, n_kv, n_heads)
            lse2 = jnp.concatenate([lse_t[:, 2 * t:2 * t + 1], lse_t[:, 2 * t + 1:2 * t + 2]], axis=0)
            dl2 = jnp.concatenate([dl_t[:, 2 * t:2 * t + 1], dl_t[:, 2 * t + 1:2 * t + 2]], axis=0)
            s = lax.dot_general(_stack_heads(_pair(q, t)), k2, NT, preferred_element_type=F32) * scale
            p = jnp.where(mask, jnp.exp(s - lse2), 0.0)
            dp = lax.dot_general(_stack_heads(_pair(do_, t)), v2, NT, preferred_element_type=F32)
            ds = p * (dp - dl2)
            dq2 = jnp.dot(ds.astype(BF16), k2, preferred_element_type=F32) * scale
            tiles.append(jnp.where(lo, dq2[:BLOCK], dq2[BLOCK:]))
        dq = jnp.concatenate(tiles, axis=-1)
        dq_ref[0] = _rope(dq, cos_ref[0], sin_ref[0], -1.0).astype(BF16)

    qs, kp, kc, vp, vc = _attn_specs(B, S, d, C, n_heads, n_kv, q_col, k_col, v_col)
    row_q = pl.BlockSpec((1, BLOCK, qw), lambda b, r, i: (b, i, r))
    row_l = pl.BlockSpec((1, BLOCK, LANES), lambda b, r, i: (b, i, r))
    qkv3 = qkv.reshape(B, Ls, d * C)
    v3 = lambda t, w: t.reshape(B, Ls, d * w)
    args = (qkv3, qkv3, qkv3, qkv3, qkv3, v3(do, qw), v3(lse, LANES), v3(delta, LANES), v3(cos, LANES), v3(sin, LANES))
    dq3 = _hosted(plan, name, lambda comm: _pc(
        body, args, out_shape=SDS((B, Ls, d * qw), BF16), grid=(B, d, nb),
        in_specs=[qs, kp, kc, vp, vc, row_q, row_l, row_l, row_l, row_l], out_specs=row_q,
        name=name, sem=("parallel", "parallel", "parallel"), comm=comm))
    return dq3.reshape(B * S, qw)


def _attn_dkv(qkv, do, lse, delta, cos, sin, B, S, d, *, n_heads, n_kv, q_col, k_col, v_col, max_dist, name, plan=None):
    C = qkv.shape[1]
    Ls = S // d
    nb = Ls // BLOCK
    qw = n_heads * HEAD_DIM
    kvw = n_kv * HEAD_DIM
    R = n_heads // n_kv
    scale = HEAD_DIM ** -0.5
    cq, ck = (C // qw if d > 1 else 0), (C // kvw if d > 1 else 0)

    def body(k_ref, v_ref, q0_ref, q1_ref, do0_ref, do1_ref, lse0_ref, lse1_ref, dl0_ref, dl1_ref, cos_ref, sin_ref,
             dk_ref, dv_ref):
        j = pl.program_id(2)
        kj = lax.broadcasted_iota(jnp.int32, (BLOCK, BLOCK), 0)
        qi = lax.broadcasted_iota(jnp.int32, (BLOCK, BLOCK), 1)
        dist0 = qi - kj
        dist1 = qi + BLOCK - kj
        mask0 = (dist0 >= 0) & (dist0 <= max_dist)
        mask1 = (dist1 <= max_dist) & (j + 1 < nb)
        kb, vb = k_ref[0], v_ref[0]
        kbs, vbs = (_swap_halves(kb), _swap_halves(vb)) if R > 1 else (None, None)
        sides = ((q0_ref[0], do0_ref[0], lse0_ref[0].T, dl0_ref[0].T, mask0), (q1_ref[0], do1_ref[0], lse1_ref[0].T, dl1_ref[0].T, mask1))
        n_acc = n_kv if R > 1 else n_kv // 2
        dks = [jnp.zeros((BLOCK, LANES), F32) for _ in range(n_acc)]
        dvs = [jnp.zeros((BLOCK, LANES), F32) for _ in range(n_acc)]
        for t in range(n_heads // 2):
            k2 = _kv_operand(kb, kbs, 2 * t, n_kv, n_heads)
            v2 = _kv_operand(vb, vbs, 2 * t, n_kv, n_heads)
            a = (2 * t) // R if R > 1 else t
            for (q, do_, lse_r, dl_r, mask) in sides:
                q2, do2 = _stack_heads(_pair(q, t)), _stack_heads(_pair(do_, t))
                s = lax.dot_general(k2, q2, NT, preferred_element_type=F32) * scale
                dp = lax.dot_general(v2, do2, NT, preferred_element_type=F32)
                ps, dss = [], []
                for half in (0, 1):
                    h = 2 * t + half
                    sl = slice(BLOCK * half, BLOCK * (half + 1))
                    p = jnp.where(mask, jnp.exp(s[:, sl] - lse_r[h:h + 1, :]), 0.0)
                    ps.append(p)
                    dss.append(p * (dp[:, sl] - dl_r[h:h + 1, :]))
                dvs[a] = dvs[a] + jnp.dot(jnp.concatenate(ps, axis=1).astype(BF16), do2, preferred_element_type=F32)
                dks[a] = dks[a] + jnp.dot(jnp.concatenate(dss, axis=1).astype(BF16), q2, preferred_element_type=F32)
        if R > 1:
            lo = _low_half((BLOCK, LANES))
            fold = lambda x: x + pltpu.roll(x, HEAD_DIM, 1)
            dks = [jnp.where(lo, fold(dks[2 * t]), fold(dks[2 * t + 1])) for t in range(n_kv // 2)]
            dvs = [jnp.where(lo, fold(dvs[2 * t]), fold(dvs[2 * t + 1])) for t in range(n_kv // 2)]
        dk_t = jnp.concatenate(dks, axis=-1) * scale
        dk_ref[0] = _rope(dk_t, cos_ref[0], sin_ref[0], -1.0).astype(BF16)
        dv_ref[0] = jnp.concatenate(dvs, axis=-1).astype(BF16)

    nxt = lambda j: jnp.minimum(j + 1, nb - 1)
    k_spec = pl.BlockSpec((1, BLOCK, kvw), lambda b, r, j: (b, j, r * ck + k_col // kvw))
    v_spec = pl.BlockSpec((1, BLOCK, kvw), lambda b, r, j: (b, j, r * ck + v_col // kvw))
    q0 = pl.BlockSpec((1, BLOCK, qw), lambda b, r, j: (b, j, r * cq + q_col // qw))
    q1 = pl.BlockSpec((1, BLOCK, qw), lambda b, r, j: (b, nxt(j), r * cq + q_col // qw))
    w0 = lambda w: pl.BlockSpec((1, BLOCK, w), lambda b, r, j: (b, j, r))
    w1 = lambda w: pl.BlockSpec((1, BLOCK, w), lambda b, r, j: (b, nxt(j), r))
    qkv3 = qkv.reshape(B, Ls, d * C)
    v3 = lambda t, w: t.reshape(B, Ls, d * w)
    do3, lse3, dl3 = v3(do, qw), v3(lse, LANES), v3(delta, LANES)
    args = (qkv3, qkv3, qkv3, qkv3, do3, do3, lse3, lse3, dl3, dl3, v3(cos, LANES), v3(sin, LANES))
    dk3, dv3 = _hosted(plan, name, lambda comm: _pc(
        body, args, out_shape=(SDS((B, Ls, d * kvw), BF16), SDS((B, Ls, d * kvw), BF16)), grid=(B, d, nb),
        in_specs=[k_spec, v_spec, q0, q1, w0(qw), w1(qw), w0(LANES), w1(LANES), w0(LANES), w1(LANES), w0(LANES), w0(LANES)],
        out_specs=(w0(kvw), w0(kvw)), name=name, sem=("parallel", "parallel", "parallel"), comm=comm))
    return dk3.reshape(B * S, kvw), dv3.reshape(B * S, kvw)


def _head_expand():
    r = lax.broadcasted_iota(jnp.int32, (LANES, C_HEADS * HEAD_DIM), 0)
    c = lax.broadcasted_iota(jnp.int32, (LANES, C_HEADS * HEAD_DIM), 1)
    return jnp.where(c // HEAD_DIM == r, 1.0, 0.0).astype(F32)


def _delta(do, o, lse=None, sinks_row=None, name="delta"):
    T, W = do.shape
    tm = _pick(T, 512, 8)
    with_sink = sinks_row is not None

    def body(*refs):
        if with_sink:
            do_ref, o_ref, lse_ref, sk_ref, dl_ref, dob_ref, ds_ref = refs
        else:
            do_ref, o_ref, dl_ref, dob_ref = refs
        do_ = do_ref[...]
        dl = lax.dot_general(do_ * o_ref[...], _head_expand(), NT, preferred_element_type=F32, precision=HI)
        dl_ref[...] = dl
        dob_ref[...] = do_.astype(BF16)
        if with_sink:
            lane = lax.broadcasted_iota(jnp.int32, dl.shape, 1)
            contrib = jnp.where(lane < A_N_HEADS, -jnp.exp(sk_ref[...] - lse_ref[...]) * dl, 0.0)
            part = jnp.sum(contrib, axis=0, keepdims=True)

            @pl.when(pl.program_id(0) == 0)
            def _():
                ds_ref[...] = part

            @pl.when(pl.program_id(0) > 0)
            def _():
                ds_ref[...] += part

    row_w = pl.BlockSpec((tm, W), lambda i: (i, 0))
    row_l = pl.BlockSpec((tm, LANES), lambda i: (i, 0))
    vec_l = pl.BlockSpec((1, LANES), lambda i: (0, 0))
    if with_sink:
        return pl.pallas_call(
            body, out_shape=(SDS((T, LANES), F32), SDS((T, W), BF16), SDS((1, LANES), F32)), grid=(T // tm,),
            in_specs=[row_w, row_w, row_l, vec_l], out_specs=(row_l, row_w, vec_l), name=name,
            compiler_params=_params(("arbitrary",)),
        )(do, o, lse, sinks_row)
    return pl.pallas_call(
        body, out_shape=(SDS((T, LANES), F32), SDS((T, W), BF16)), grid=(T // tm,),
        in_specs=[row_w, row_w], out_specs=(row_l, row_w), name=name, compiler_params=_params(("parallel",)),
    )(do, o)


def _merge(os_, lses):
    T, W = os_[0].shape
    tm = _pick(T, 512, 8)

    def body(o0, o1, o2, l0, l1, l2, o_ref, lse_ref):
        ls = [l0[...], l1[...], l2[...]]
        m = jnp.maximum(jnp.maximum(ls[0], ls[1]), ls[2])
        ws = [jnp.exp(l - m) for l in ls]
        tot = ws[0] + ws[1] + ws[2]
        lse_ref[...] = m + jnp.log(tot)
        e = _head_expand()
        acc = jnp.zeros((tm, W), F32)
        for w, o in zip(ws, (o0, o1, o2)):
            acc = acc + jnp.dot(w / tot, e, preferred_element_type=F32, precision=HI) * o[...]
        o_ref[...] = acc

    row_w = pl.BlockSpec((tm, W), lambda i: (i, 0))
    row_l = pl.BlockSpec((tm, LANES), lambda i: (i, 0))
    return pl.pallas_call(
        body, out_shape=(SDS((T, W), F32), SDS((T, LANES), F32)), grid=(T // tm,),
        in_specs=[row_w] * 3 + [row_l] * 3, out_specs=(row_w, row_l), name="c_merge", compiler_params=_params(("parallel",)),
    )(*os_, *lses)


CONV_TC = 256


def _conv_pre(x, w, bias):
    row = lax.broadcasted_iota(jnp.int32, x.shape, 0)
    acc = x * w[SSM_CONV - 1:SSM_CONV, :] + bias
    for k in range(1, SSM_CONV):
        acc = acc + jnp.where(row >= k, pltpu.roll(x, k, 0), 0.0) * w[SSM_CONV - 1 - k:SSM_CONV - k, :]
    return acc


def _conv_fwd(zx3, w, bias):
    B, S, _ = zx3.shape
    off = SSM_D_INNER // CONV_TC

    def body(x_ref, w_ref, b_ref, o_ref):
        v = _conv_pre(x_ref[0], w_ref[...], b_ref[...])
        o_ref[0] = v * jax.nn.sigmoid(v)

    return pl.pallas_call(
        body, out_shape=SDS((B, S, SSM_CONV_DIM), F32), grid=(B, SSM_CONV_DIM // CONV_TC),
        in_specs=[pl.BlockSpec((1, S, CONV_TC), lambda b, j: (b, 0, j + off)),
                  pl.BlockSpec((SSM_CONV, CONV_TC), lambda b, j: (0, j)), pl.BlockSpec((1, CONV_TC), lambda b, j: (0, j))],
        out_specs=pl.BlockSpec((1, S, CONV_TC), lambda b, j: (b, 0, j)), name="b_conv_fwd",
        compiler_params=_params(("parallel", "parallel")),
    )(zx3, w, bias)


def _conv_bwd(zx3, dxc, w, bias, col0, name):
    B, S, n = dxc.shape
    tc = _pick(n, CONV_TC)
    off_x = (SSM_D_INNER + col0) // tc
    off_w = col0 // tc

    def body(x_ref, d_ref, w_ref, b_ref, dx_ref, dw_ref, db_ref):
        x = x_ref[0]
        wv = w_ref[...]
        v = _conv_pre(x, wv, b_ref[...])
        sg = jax.nn.sigmoid(v)
        dc = d_ref[0] * (sg * (1.0 + v * (1.0 - sg)))
        row = lax.broadcasted_iota(jnp.int32, x.shape, 0)
        dx = dc * wv[SSM_CONV - 1:SSM_CONV, :]
        dws = [jnp.sum(dc * x, axis=0, keepdims=True)]
        for k in range(1, SSM_CONV):
            dx = dx + jnp.where(row < S - k, pltpu.roll(dc, S - k, 0), 0.0) * wv[SSM_CONV - 1 - k:SSM_CONV - k, :]
            dws.append(jnp.sum(dc * jnp.where(row >= k, pltpu.roll(x, k, 0), 0.0), axis=0, keepdims=True))
        dx_ref[0] = dx.astype(BF16)
        ridx = lax.broadcasted_iota(jnp.int32, (SSM_CONV, tc), 0)
        dw = jnp.zeros((SSM_CONV, tc), F32)
        for k in range(SSM_CONV):
            dw = jnp.where(ridx == SSM_CONV - 1 - k, dws[k], dw)
        db = jnp.sum(dc, axis=0, keepdims=True)

        @pl.when(pl.program_id(1) == 0)
        def _():
            dw_ref[...] = dw
            db_ref[...] = db

        @pl.when(pl.program_id(1) > 0)
        def _():
            dw_ref[...] += dw
            db_ref[...] += db

    return pl.pallas_call(
        body, out_shape=(SDS((B, S, n), BF16), SDS((SSM_CONV, n), F32), SDS((1, n), F32)), grid=(n // tc, B),
        in_specs=[pl.BlockSpec((1, S, tc), lambda j, b: (b, 0, j + off_x)), pl.BlockSpec((1, S, tc), lambda j, b: (b, 0, j)),
                  pl.BlockSpec((SSM_CONV, tc), lambda j, b: (0, j + off_w)), pl.BlockSpec((1, tc), lambda j, b: (0, j + off_w))],
        out_specs=(pl.BlockSpec((1, S, tc), lambda j, b: (b, 0, j)), pl.BlockSpec((SSM_CONV, tc), lambda j, b: (0, j)),
                   pl.BlockSpec((1, tc), lambda j, b: (0, j))),
        name=name, compiler_params=_params(("parallel", "arbitrary")),
    )(zx3, dxc, w, bias)


def _ssd_common(x, Bm, Cm, dtc_raw, dtr_raw, pr, pc):
    Q = SSM_CHUNK
    zc = dtc_raw + pr[0:1, :]
    dt_c = jax.nn.softplus(zc)
    dt_r = jax.nn.softplus(dtr_raw + pc[:, 0:1])
    A_r = -jnp.exp(pr[1:2, :])
    A_c = -jnp.exp(pc[:, 1:2])
    row = lax.broadcasted_iota(jnp.int32, (Q, Q), 0)
    col = lax.broadcasted_iota(jnp.int32, (Q, Q), 1)
    tril = jnp.where(row >= col, 1.0, 0.0).astype(F32)
    cs_c = jnp.dot(tril, dt_c * A_r, preferred_element_type=F32, precision=HI)
    cs_r = lax.dot_general(dt_r * A_c, tril, NT, preferred_element_type=F32, precision=HI)
    return zc, dt_c, A_r, cs_c, cs_r, row, col, tril


def _ssd_fwd(xc3, dtc, dtr, prow, pcol, plan=None):
    B, S, _ = xc3.shape
    Q, G, HG, P, N = SSM_CHUNK, SSM_N_GROUPS, SSM_HG, HEAD_DIM, SSM_D_STATE
    nc = S // Q
    xw = HG * P

    def body(x_ref, b_ref, c_ref, dtc_ref, dtr_ref, pr_ref, pc_ref, y_ref, st_ref, state):
        c = pl.program_id(1)

        @pl.when(c == 0)
        def _():
            state[...] = jnp.zeros_like(state)

        pr = pr_ref[0]
        for bb in range(B):
            x, Bm, Cm = x_ref[bb], b_ref[bb], c_ref[bb]
            _, dt_c, _, cs_c, cs_r, row, col, _ = _ssd_common(x, Bm, Cm, dtc_ref[bb, 0], dtr_ref[bb, 0], pr, pc_ref[0])
            Bb, Cb = Bm.astype(BF16), Cm.astype(BF16)
            CB = lax.dot_general(Cb, Bb, NT, preferred_element_type=F32)
            ys = []
            for hg in range(HG):
                xh = x[:, P * hg:P * (hg + 1)]
                xt = xh * dt_c[:, hg:hg + 1]
                csc, csr = cs_c[:, hg:hg + 1], cs_r[hg:hg + 1, :]
                L = jnp.where(row >= col, jnp.exp(jnp.minimum(csc - csr, 0.0)), 0.0)
                ydiag = jnp.dot((CB * L).astype(BF16), xt.astype(BF16), preferred_element_type=F32)
                Sh = state[bb, hg]
                yoff = lax.dot_general(Cb, Sh.astype(BF16), NT, preferred_element_type=F32) * jnp.exp(csc)
                ys.append(ydiag + yoff + pr[2:3, hg:hg + 1] * xh)
                st_ref[bb, 0, 0, P * hg:P * (hg + 1), :] = Sh
                csq = csc[Q - 1:Q, :]
                upd = lax.dot_general((xt * jnp.exp(csq - csc)).astype(BF16), Bb, TN, preferred_element_type=F32)
                state[bb, hg] = Sh * jnp.exp(csq) + upd
            y_ref[bb] = jnp.concatenate([jnp.concatenate(ys[0:2], axis=-1), jnp.concatenate(ys[2:4], axis=-1)], axis=-1)

    bo, co = SSM_D_INNER // N, (SSM_D_INNER + SSM_BC_DIM) // N
    return _hosted(plan, "b_ssd_fwd", lambda comm: _pc(
        body, (xc3, xc3, xc3, dtc, dtr, prow, pcol),
        out_shape=(SDS((B, S, SSM_D_INNER), F32), SDS((B, G, nc, xw, N), F32)), grid=(G, nc),
        in_specs=[pl.BlockSpec((B, Q, xw), lambda g, c: (0, c, g)), pl.BlockSpec((B, Q, N), lambda g, c: (0, c, bo + g)),
                  pl.BlockSpec((B, Q, N), lambda g, c: (0, c, co + g)), pl.BlockSpec((B, 1, Q, HG), lambda g, c: (0, g, c, 0)),
                  pl.BlockSpec((B, 1, HG, Q), lambda g, c: (0, g, 0, c)), pl.BlockSpec((1, 3, HG), lambda g, c: (g, 0, 0)),
                  pl.BlockSpec((1, HG, 3), lambda g, c: (g, 0, 0))],
        out_specs=(pl.BlockSpec((B, Q, xw), lambda g, c: (0, c, g)), pl.BlockSpec((B, 1, 1, xw, N), lambda g, c: (0, g, c, 0, 0))),
        scratch_shapes=[pltpu.VMEM((B, HG, P, N), F32)], name="b_ssd_fwd", sem=("parallel", "arbitrary"), comm=comm))


def _ssd_bwd(xc3, dtc, dtr, prow, pcol, states, dy3, plan=None):
    B, S, _ = xc3.shape
    Q, G, HG, P, N = SSM_CHUNK, SSM_N_GROUPS, SSM_HG, HEAD_DIM, SSM_D_STATE
    nc = S // Q
    xw = HG * P

    def body(x_ref, b_ref, c_ref, dtc_ref, dtr_ref, pr_ref, pc_ref, st_ref, dy_ref,
             dx_ref, db_ref, dc_ref, ddt_ref, dpar_ref, dstate):
        ci = pl.program_id(1)

        @pl.when(ci == 0)
        def _():
            dstate[...] = jnp.zeros_like(dstate)

        pr = pr_ref[0]
        dpar = one_sequence(0, pr, x_ref, b_ref, c_ref, dtc_ref, dtr_ref, pc_ref, st_ref, dy_ref, dx_ref, db_ref, dc_ref,
                            ddt_ref, dstate)
        for bb in range(1, B):
            dpar = dpar + one_sequence(bb, pr, x_ref, b_ref, c_ref, dtc_ref, dtr_ref, pc_ref, st_ref, dy_ref, dx_ref, db_ref,
                                       dc_ref, ddt_ref, dstate)
        first = ci == 0

        @pl.when(first)
        def _():
            dpar_ref[0] = dpar

        @pl.when(jnp.logical_not(first))
        def _():
            dpar_ref[0] += dpar

    def one_sequence(bb, pr, x_ref, b_ref, c_ref, dtc_ref, dtr_ref, pc_ref, st_ref, dy_ref, dx_ref, db_ref, dc_ref, ddt_ref,
                     dstate):
        x, Bm, Cm, dy = x_ref[bb], b_ref[bb], c_ref[bb], dy_ref[bb]
        zc, dt_c, A_r, cs_c, cs_r, row, col, tril = _ssd_common(x, Bm, Cm, dtc_ref[bb, 0], dtr_ref[bb, 0], pr, pc_ref[0])
        Bb, Cb = Bm.astype(BF16), Cm.astype(BF16)
        CB = lax.dot_general(Cb, Bb, NT, preferred_element_type=F32)
        CBt = lax.dot_general(Bb, Cb, NT, preferred_element_type=F32)
        lane4 = lax.broadcasted_iota(jnp.int32, (Q, HG), 1)
        lane4r = lax.broadcasted_iota(jnp.int32, (1, HG), 1)
        rowq = lax.broadcasted_iota(jnp.int32, (Q, 1), 0)
        dB = jnp.zeros((Q, N), F32)
        dC = jnp.zeros((Q, N), F32)
        dcs4 = jnp.zeros((Q, HG), F32)
        dtx4 = jnp.zeros((Q, HG), F32)
        dD4 = jnp.zeros((1, HG), F32)
        dxts, xhs, dyhs = [], [], []
        for hg in range(HG):
            xh = x[:, P * hg:P * (hg + 1)]
            dyh = dy[:, P * hg:P * (hg + 1)]
            xt = xh * dt_c[:, hg:hg + 1]
            xtb, dyb = xt.astype(BF16), dyh.astype(BF16)
            csc, csr = cs_c[:, hg:hg + 1], cs_r[hg:hg + 1, :]
            L = jnp.where(row >= col, jnp.exp(jnp.minimum(csc - csr, 0.0)), 0.0)
            Lt = jnp.where(col >= row, jnp.exp(jnp.minimum(csr - csc, 0.0)), 0.0)
            M, Mt = CB * L, CBt * Lt
            Sh = st_ref[bb, 0, 0, P * hg:P * (hg + 1), :]
            dSh = dstate[bb, hg]
            Shb, dShb = Sh.astype(BF16), dSh.astype(BF16)
            ecs = jnp.exp(csc)
            csq = csc[Q - 1:Q, :]
            dec = jnp.exp(csq - csc)
            dxt = jnp.dot(Mt.astype(BF16), dyb, preferred_element_type=F32)
            dxt = dxt + lax.dot_general(Bb, dShb, NT, preferred_element_type=F32) * dec
            Gm = lax.dot_general(dyb, xtb, NT, preferred_element_type=F32)
            Gt = lax.dot_general(xtb, dyb, NT, preferred_element_type=F32)
            dC = dC + jnp.dot((Gm * L).astype(BF16), Bb, preferred_element_type=F32)
            dB = dB + jnp.dot((Gt * Lt).astype(BF16), Cb, preferred_element_type=F32)
            dC = dC + jnp.dot(dyb, Shb, preferred_element_type=F32) * ecs
            dBst = jnp.dot(xtb, dShb, preferred_element_type=F32) * dec
            dB = dB + dBst
            dcs = jnp.sum(Gm * M, axis=1, keepdims=True) - jnp.sum(Gt * Mt, axis=1, keepdims=True)
            yoff = lax.dot_general(Cb, Shb, NT, preferred_element_type=F32) * ecs
            dcs = dcs + jnp.sum(yoff * dyh, axis=1, keepdims=True)
            r = jnp.sum(dBst * Bm, axis=1, keepdims=True)
            dcs = dcs - r
            extra = jnp.sum(r, axis=0, keepdims=True) + jnp.exp(csq) * jnp.sum(
                jnp.sum(dSh * Sh, axis=1, keepdims=True), axis=0, keepdims=True)
            dcs = dcs + jnp.where(rowq == Q - 1, extra, 0.0)
            dcs4 = jnp.where(lane4 == hg, dcs, dcs4)
            dtx4 = jnp.where(lane4 == hg, jnp.sum(dxt * xh, axis=1, keepdims=True), dtx4)
            dD4 = jnp.where(lane4r == hg, jnp.sum(jnp.sum(dyh * xh, axis=1, keepdims=True), axis=0, keepdims=True), dD4)
            dstate[bb, hg] = dSh * jnp.exp(csq) + lax.dot_general((dyh * ecs).astype(BF16), Cb, TN, preferred_element_type=F32)
            dxts.append(dxt)
            xhs.append(xh)
            dyhs.append(dyh)
        da4 = lax.dot_general(tril, dcs4, TN, preferred_element_type=F32, precision=HI)
        ddt4 = da4 * A_r + dtx4
        ddtraw = ddt4 * jax.nn.sigmoid(zc)
        ddt_ref[bb, 0] = ddtraw
        dxs = [dxts[hg] * dt_c[:, hg:hg + 1] + pr[2:3, hg:hg + 1] * dyhs[hg] for hg in range(HG)]
        dx_ref[bb] = jnp.concatenate([jnp.concatenate(dxs[0:2], axis=-1), jnp.concatenate(dxs[2:4], axis=-1)], axis=-1)
        db_ref[bb] = dB
        dc_ref[bb] = dC
        d_bias = jnp.sum(ddtraw, axis=0, keepdims=True)
        d_alog = jnp.sum(da4 * dt_c, axis=0, keepdims=True) * A_r
        r3 = lax.broadcasted_iota(jnp.int32, (3, HG), 0)
        return jnp.where(r3 == 0, d_bias, jnp.where(r3 == 1, d_alog, dD4))

    rc = lambda c: nc - 1 - c
    bo, co = SSM_D_INNER // N, (SSM_D_INNER + SSM_BC_DIM) // N
    return _hosted(plan, "b_ssd_bwd", lambda comm: _pc(
        body, (xc3, xc3, xc3, dtc, dtr, prow, pcol, states, dy3),
        out_shape=(SDS((B, S, SSM_D_INNER), F32), SDS((B, S, SSM_BC_DIM), F32), SDS((B, S, SSM_BC_DIM), F32),
                   SDS((B, G, S, HG), F32), SDS((G, 3, HG), F32)),
        grid=(G, nc),
        in_specs=[pl.BlockSpec((B, Q, xw), lambda g, c: (0, rc(c), g)), pl.BlockSpec((B, Q, N), lambda g, c: (0, rc(c), bo + g)),
                  pl.BlockSpec((B, Q, N), lambda g, c: (0, rc(c), co + g)), pl.BlockSpec((B, 1, Q, HG), lambda g, c: (0, g, rc(c), 0)),
                  pl.BlockSpec((B, 1, HG, Q), lambda g, c: (0, g, 0, rc(c))), pl.BlockSpec((1, 3, HG), lambda g, c: (g, 0, 0)),
                  pl.BlockSpec((1, HG, 3), lambda g, c: (g, 0, 0)),
                  pl.BlockSpec((B, 1, 1, xw, N), lambda g, c: (0, g, rc(c), 0, 0)), pl.BlockSpec((B, Q, xw), lambda g, c: (0, rc(c), g))],
        out_specs=(pl.BlockSpec((B, Q, xw), lambda g, c: (0, rc(c), g)), pl.BlockSpec((B, Q, N), lambda g, c: (0, rc(c), g)),
                   pl.BlockSpec((B, Q, N), lambda g, c: (0, rc(c), g)), pl.BlockSpec((B, 1, Q, HG), lambda g, c: (0, g, rc(c), 0)),
                   pl.BlockSpec((1, 3, HG), lambda g, c: (g, 0, 0))),
        scratch_shapes=[pltpu.VMEM((B, HG, P, N), F32)], name="b_ssd_bwd", sem=("parallel", "arbitrary"), comm=comm))


GN_W = SSM_D_INNER // SSM_N_GROUPS


def _gate_fwd(y, zx, nw):
    T = y.shape[0]
    tm = _pick(T, 256, 8)

    def body(y_ref, z_ref, w_ref, o_ref):
        z = z_ref[...]
        gt = y_ref[...] * (z * jax.nn.sigmoid(z))
        outs = []
        for k in range(SSM_N_GROUPS):
            gk = gt[:, GN_W * k:GN_W * (k + 1)]
            outs.append(gk * lax.rsqrt(jnp.mean(gk * gk, axis=-1, keepdims=True) + NORM_EPS))
        o_ref[...] = (jnp.concatenate(outs, axis=-1) * w_ref[...]).astype(BF16)

    row = pl.BlockSpec((tm, SSM_D_INNER), lambda i: (i, 0))
    return pl.pallas_call(
        body, out_shape=SDS((T, SSM_D_INNER), BF16), grid=(T // tm,),
        in_specs=[row, row, pl.BlockSpec((1, SSM_D_INNER), lambda i: (0, 0))], out_specs=row, name="b_gate_fwd",
        compiler_params=_params(("parallel",)),
    )(y, zx, nw)


def _gate_bwd(dgn, y, zx, nw):
    T = y.shape[0]
    tm = _pick(T, 256, 8)

    def body(d_ref, y_ref, z_ref, w_ref, dy_ref, dz_ref, dw_ref):
        z, yv, w = z_ref[...], y_ref[...], w_ref[...]
        sg = jax.nn.sigmoid(z)
        sz = z * sg
        gt = yv * sz
        gw = d_ref[...] * w
        dgts, dws = [], []
        for k in range(SSM_N_GROUPS):
            sl = slice(GN_W * k, GN_W * (k + 1))
            gk, gwk = gt[:, sl], gw[:, sl]
            rstd = lax.rsqrt(jnp.mean(gk * gk, axis=-1, keepdims=True) + NORM_EPS)
            dgts.append(rstd * gwk - gk * (rstd * rstd * rstd) * jnp.mean(gwk * gk, axis=-1, keepdims=True))
            dws.append(jnp.sum(d_ref[:, sl] * gk * rstd, axis=0, keepdims=True))
        dgt = jnp.concatenate(dgts, axis=-1)
        dy_ref[...] = dgt * sz
        dz_ref[...] = (dgt * yv * (sg * (1.0 + z * (1.0 - sg)))).astype(BF16)
        dw = jnp.concatenate(dws, axis=-1)

        @pl.when(pl.program_id(0) == 0)
        def _():
            dw_ref[...] = dw

        @pl.when(pl.program_id(0) > 0)
        def _():
            dw_ref[...] += dw

    row = pl.BlockSpec((tm, SSM_D_INNER), lambda i: (i, 0))
    vec = pl.BlockSpec((1, SSM_D_INNER), lambda i: (0, 0))
    return pl.pallas_call(
        body, out_shape=(SDS((T, SSM_D_INNER), F32), SDS((T, SSM_D_INNER), BF16), SDS((1, SSM_D_INNER), F32)), grid=(T // tm,),
        in_specs=[row, row, row, vec], out_specs=(row, row, vec), name="b_gate_bwd", compiler_params=_params(("arbitrary",)),
    )(dgn, y, zx, nw)


N_CHIPS = 4


def _dev_block(ref, kind, j, size):
    if kind == "slot":
        return ref.at[j]
    start = pl.multiple_of(j * size, size)
    nd = len(ref.shape)
    if kind == "col":
        return ref.at[(slice(None),) * (nd - 1) + (pl.ds(start, size),)]
    return ref.at[(slice(None),) * (nd - 2) + (pl.ds(start, size), slice(None))]


def _dma_sems(n, k):
    return [pltpu.SemaphoreType.DMA((n, k)), pltpu.SemaphoreType.DMA((n, k)), pltpu.SemaphoreType.DMA((n, k))]


def _place(shard, layer, kind, full_shape, dev, name):
    k, n = shard.shape[1:]
    tr = _pick(k, 512, 16)
    nb = k // tr

    def body(dev_ref, s_ref, o_ref):
        if kind == "slot":
            o_ref[0] = s_ref[0].astype(BF16)
        else:
            o_ref[...] = s_ref[0].astype(BF16)

    out_spec = {"slot": pl.BlockSpec((1, tr, n), lambda i, d: (d[0], i, 0)),
                "row": pl.BlockSpec((tr, n), lambda i, d: (d[0] * nb + i, 0)),
                "col": pl.BlockSpec((tr, n), lambda i, d: (i, d[0]))}[kind]
    return pl.pallas_call(
        body, out_shape=SDS(full_shape, BF16),
        grid_spec=pltpu.PrefetchScalarGridSpec(
            num_scalar_prefetch=1, grid=(nb,), in_specs=[pl.BlockSpec((1, tr, n), lambda i, d: (layer, i, 0))], out_specs=out_spec),
        name=name, compiler_params=_params(("arbitrary",)),
    )(dev, shard)


def _run_comm(comm, name):
    c_in = len(comm.inputs)

    def body(*refs):
        cins, couts, sems = refs[:c_in], refs[c_in:c_in + len(comm.out_shapes)], refs[c_in + len(comm.out_shapes):]
        for _, fn in comm.phases:
            fn(cins, couts, sems)

    return pl.pallas_call(
        body, out_shape=list(comm.out_shapes), in_specs=[ANY] * c_in, out_specs=[ANY] * len(comm.out_shapes),
        input_output_aliases=dict(comm.aliases), scratch_shapes=list(comm.sems), name=name,
    )(*comm.inputs)


def _gather_comm(items, mid=0.7):
    n = len(items)

    def tools(srcs, dsts, sems):
        send_sems, recv_sems, local_sems = sems
        px, py, pc = lax.axis_index("x"), lax.axis_index("y"), lax.axis_index("c")
        me, sibling = (px, py, pc), (px, py, 1 - pc)
        chips = [(1 - px, py), (px, 1 - py), (1 - px, 1 - py)]

        def blk(a, dev):
            return _dev_block(dsts[a], items[a][1], 4 * dev[0] + 2 * dev[1] + dev[2], items[a][2])

        def copy(a, k, block, to, src=None):
            return pltpu.make_async_remote_copy(
                src_ref=blk(a, block) if src is None else src, dst_ref=blk(a, block),
                send_sem=send_sems.at[a, k], recv_sem=recv_sems.at[a, k], device_id=to, device_id_type=MESH)

        def mine():
            return [pltpu.make_async_copy(srcs[a], blk(a, me), local_sems.at[a, 0]) for a in range(n) if not items[a][4]]

        def first():
            out = []
            for a in range(n):
                src = blk(a, me) if items[a][4] else srcs[a]
                out.append(copy(a, 0, me, sibling, src=src))
                out += [copy(a, 1 + j, me, (*chip, pc), src=src) for j, chip in enumerate(chips)]
            return out

        def passed():
            return [copy(a, 4 + j, (*chip, pc), sibling) for j, chip in enumerate(chips) for a in range(n)]

        return me, sibling, chips, pc, copy, mine, first, passed

    def start(srcs, dsts, sems):
        *_, mine, first, _ = tools(srcs, dsts, sems)
        for cp in mine() + first():
            cp.start()

    def forward(srcs, dsts, sems):
        me, _, chips, pc, copy, _, _, passed = tools(srcs, dsts, sems)
        fwd = passed()
        for j, chip in enumerate(chips):
            for a in range(n):
                copy(a, 1 + j, (*chip, pc), me).wait_recv()
                fwd[j * n + a].start()

    def finish(srcs, dsts, sems):
        me, sibling, chips, pc, copy, mine, first, passed = tools(srcs, dsts, sems)
        for a in range(n):
            copy(a, 0, sibling, me).wait_recv()
            for j, chip in enumerate(chips):
                copy(a, 4 + j, (*chip, 1 - pc), me).wait_recv()
        for cp in first() + passed():
            cp.wait_send()
        for cp in mine():
            cp.wait()

    return _Comm([it[0] for it in items], [SDS(it[3], it[0].dtype) for it in items],
                 {a: a for a in range(n) if items[a][4]}, _dma_sems(n, 7), [(0.0, start), (mid, forward), (1.0, finish)])


def _gather(items, name):
    return _run_comm(_gather_comm(items), name)


def _reduce_d2d_comm(items):
    n = len(items)

    def copies(gs, gots, sems):
        send_sems, recv_sems, _ = sems
        px, py, pc = lax.axis_index("x"), lax.axis_index("y"), lax.axis_index("c")
        out = []
        for a in range(n):
            _, kind, size, _ = items[a]
            for q in range(N_CHIPS):
                out.append(pltpu.make_async_remote_copy(
                    src_ref=_dev_block(gs[a], kind, 2 * q + 1 - pc, size), dst_ref=gots[a].at[q], send_sem=send_sems.at[a, q],
                    recv_sem=recv_sems.at[a, q], device_id=(px, py, 1 - pc), device_id_type=MESH))
        return out

    def start(gs, gots, sems):
        for cp in copies(gs, gots, sems):
            cp.start()

    def finish(gs, gots, sems):
        for cp in copies(gs, gots, sems):
            cp.wait()

    return _Comm([it[0] for it in items], [SDS((N_CHIPS,) + tuple(it[3]), F32) for it in items], {},
                 _dma_sems(n, N_CHIPS), [(0.0, start), (1.0, finish)])


def _pair_sum(g, got, kind, core, name):
    _, k, n = got.shape
    tr = _pick(k, max(16, STREAM_VMEM // (2 * n * 10)), 16)
    nb = k // tr

    def body(c_ref, g_ref, s_ref, o_ref):
        mine = g_ref[0] if kind == "slot" else g_ref[...]
        o_ref[0] = (mine + s_ref[0]).astype(BF16)

    g_spec = {"slot": pl.BlockSpec((1, tr, n), lambda q, i, c: (2 * q + c[0], i, 0)),
              "row": pl.BlockSpec((tr, n), lambda q, i, c: ((2 * q + c[0]) * nb + i, 0)),
              "col": pl.BlockSpec((tr, n), lambda q, i, c: (i, 2 * q + c[0]))}[kind]
    part = pl.BlockSpec((1, tr, n), lambda q, i, c: (q, i, 0))
    return pl.pallas_call(
        body, out_shape=SDS((N_CHIPS, k, n), BF16),
        grid_spec=pltpu.PrefetchScalarGridSpec(num_scalar_prefetch=1, grid=(N_CHIPS, nb), in_specs=[g_spec, part], out_specs=part),
        name=name, compiler_params=_params(("arbitrary", "arbitrary")),
    )(core, g, got)


def _reduce_ici_comm(parts):
    n = len(parts)

    def copies(ps, rs, sems, arriving):
        send_sems, recv_sems, _ = sems
        px, py, pc = lax.axis_index("x"), lax.axis_index("y"), lax.axis_index("c")
        my_chip = 2 * px + py
        out = []
        for a in range(n):
            for k in range(1, N_CHIPS):
                qx, qy = px ^ (k >> 1), py ^ (k & 1)
                q = 2 * qx + qy
                out.append(pltpu.make_async_remote_copy(
                    src_ref=ps[a].at[q], dst_ref=rs[a].at[q] if arriving else rs[a].at[my_chip], send_sem=send_sems.at[a, k - 1],
                    recv_sem=recv_sems.at[a, k - 1], device_id=(qx, qy, pc), device_id_type=MESH))
        return out

    def start(ps, rs, sems):
        for cp in copies(ps, rs, sems, False):
            cp.start()

    def finish(ps, rs, sems):
        for cp in copies(ps, rs, sems, True):
            cp.wait_recv()
        for cp in copies(ps, rs, sems, False):
            cp.wait_send()

    return _Comm(list(parts), [SDS(p.shape, p.dtype) for p in parts], {}, _dma_sems(n, N_CHIPS - 1),
                 [(0.0, start), (1.0, finish)])


def _adam_update(g, w, m, v):
    c1 = 1.0 - ADAM_B1 ** ADAM_STEP
    c2 = 1.0 - ADAM_B2 ** ADAM_STEP
    nm = ADAM_B1 * m + (1.0 - ADAM_B1) * g
    nv = ADAM_B2 * v + (1.0 - ADAM_B2) * (g * g)
    delta = -ADAM_LR * ((nm / c1) / (jnp.sqrt(nv / c2) + ADAM_EPS) + ADAM_WD * w)
    return delta, nm, nv


def _adamw(parts, recv, w, m, v, layer, prev, chip, name):
    _, R, C = w.shape
    row_bytes = 2 * C * (N_CHIPS * 2 + 7 * 4)
    tr = _pick(R, max(16, STREAM_VMEM // row_bytes), 16)
    n_prev = 0 if prev is None else 4

    def body(ch_ref, own_ref, r1_ref, r2_ref, r3_ref, w_ref, m_ref, v_ref, *rest):
        g_ref, d_ref, nm_ref, nv_ref = rest[n_prev:]
        g = own_ref[0].astype(F32)
        for r_ref in (r1_ref, r2_ref, r3_ref):
            g = g + r_ref[0].astype(F32)
        g_ref[0] = g
        d_ref[0], nm_ref[0], nv_ref[0] = _adam_update(g, w_ref[0], m_ref[0], v_ref[0])

    lay = pl.BlockSpec((1, tr, C), lambda i, ch: (layer, i, 0))
    other = lambda k: pl.BlockSpec((1, tr, C), lambda i, ch: (ch[0] ^ k, i, 0))
    out = SDS(w.shape, F32)
    return pl.pallas_call(
        body, out_shape=(out, out, out, out),
        grid_spec=pltpu.PrefetchScalarGridSpec(
            num_scalar_prefetch=1, grid=(R // tr,),
            in_specs=[pl.BlockSpec((1, tr, C), lambda i, ch: (ch[0], i, 0)), other(2), other(1), other(3), lay, lay, lay]
            + [ANY] * n_prev,
            out_specs=(lay, lay, lay, lay)),
        input_output_aliases={8 + k: k for k in range(n_prev)},
        name=name, compiler_params=_params(("arbitrary",)),
    )(chip, parts, recv, recv, recv, w, m, v, *(prev or ()))


def _small_adamw(gathered, ws, ms, vs):
    n = len(ws)

    def body(*refs):
        g_in, w_in, m_in, v_in = refs[:n], refs[n:2 * n], refs[2 * n:3 * n], refs[3 * n:4 * n]
        outs = refs[4 * n:]
        for i in range(n):
            g = g_in[i][0]
            for dev in range(1, N_DEV):
                g = g + g_in[i][dev]
            d, nm, nv = _adam_update(g, w_in[i][...], m_in[i][...], v_in[i][...])
            outs[i][...] = g
            outs[n + i][...] = d
            outs[2 * n + i][...] = nm
            outs[3 * n + i][...] = nv

    shapes = [SDS(w.shape, F32) for w in ws]
    outs = pl.pallas_call(body, out_shape=shapes * 4, name="small_adamw")(*gathered, *ws, *ms, *vs)
    return outs[:n], outs[n:2 * n], outs[2 * n:3 * n], outs[3 * n:]


W_NAMES = ("norm_mix_w", "norm_mlp_w", "a_w_qkv", "a_b_qkv", "a_sinks", "a_w_o", "a_b_o", "b_in_w", "b_conv_w", "b_conv_b",
           "b_dt_bias", "b_a_log", "b_d", "b_norm_w", "b_out_w", "c_w_qkv", "c_w_o", "mlp_w_up", "mlp_w_down", "final_norm_w")
BIG_KIND = {"a_w_qkv": "slot", "a_w_o": "row", "b_in_w": "slot", "b_out_w": "row", "c_w_qkv": "col", "c_w_o": "row",
            "mlp_w_up": "col", "mlp_w_down": "row"}
SMALL_SHARDED = {"a_b_qkv": 1, "a_b_o": 1, "b_conv_w": 2}
SMALL_REPLICATED = ("norm_mix_w", "norm_mlp_w", "a_sinks", "b_conv_b", "b_dt_bias", "b_a_log", "b_d", "b_norm_w", "final_norm_w")


def _layer_big(i):
    kind, j = i % 3, i // 3
    mix = {0: [("a_w_qkv", j), ("a_w_o", j)], 1: [("b_in_w", 0), ("b_out_w", 0)], 2: [("c_w_qkv", 0), ("c_w_o", 0)]}[kind]
    return mix + [("mlp_w_up", i), ("mlp_w_down", i)]


def _block_size(kind, shard2d):
    return {"slot": None, "row": shard2d[0], "col": shard2d[1]}[kind]


def _full2d(kind, shard2d):
    k, n = shard2d
    return {"slot": (N_DEV, k, n), "row": (N_DEV * k, n), "col": (k, N_DEV * n)}[kind]


def _from_slots(t, ax):
    s = t.shape[1:]
    return jnp.moveaxis(t, 0, ax).reshape(s[:ax] + (N_DEV * s[ax],) + s[ax + 1:])


def _to_slots(g, ax):
    s = g.shape
    return jnp.moveaxis(g.reshape(s[:ax] + (N_DEV, s[ax] // N_DEV) + s[ax + 1:]), ax, 0)


def _rope_tables(positions):
    half = HEAD_DIM // 2
    inv = ROPE_THETA ** (-(jnp.arange(LANES, dtype=jnp.int32) % half).astype(F32) / half)
    ang = positions.astype(F32).reshape(-1, 1) * inv
    return jnp.cos(ang), jnp.sin(ang)


def _swa_fwd(u, h, p, j, B, S, cos, sin, tag, nw, plan=None):
    qkv = _matmul(u, p["a_w_qkv"][j], out_dtype=BF16, bias=p["a_b_qkv"][j][None], rope=(cos, sin),
                  rope_cols=A_Q_DIM + A_KV_DIM, name=f"{tag}_qkv", plan=plan)
    o, lse = _attn_fwd(qkv, B, S, 1, n_heads=A_N_HEADS, n_kv=A_N_KV, q_col=0, k_col=A_Q_DIM, v_col=A_Q_DIM + A_KV_DIM,
                       max_dist=A_WINDOW - 1, sinks=p["a_sinks"][j], name=f"{tag}_attn", plan=plan)
    h1, u2 = _matmul(o, p["a_w_o"][j], bias=p["a_b_o"][j][None], resid=h, norm_out=nw, name=f"{tag}_o")
    return h1, u2, (qkv, o, lse)


def _swa_bwd(dh1, u, saved, p, j, B, S, cos, sin, tag, norm, plan=None):
    qkv, o, lse = saved
    kw = dict(n_heads=A_N_HEADS, n_kv=A_N_KV, q_col=0, k_col=A_Q_DIM, v_col=A_Q_DIM + A_KV_DIM, max_dist=A_WINDOW - 1)
    g = {}
    do = _matmul(dh1, p["a_w_o"][j], tb=True, name=f"{tag}_do")
    g["a_w_o"] = _matmul(o, dh1, ta=True, name=f"{tag}_dwo")
    g["a_b_o"] = _colsum(dh1, f"{tag}_dbo")[0]
    sk = jnp.pad(p["a_sinks"][j], (0, LANES - A_N_HEADS))[None]
    delta, dob, dsink = _delta(do, o, lse, sk, name=f"{tag}_delta")
    g["a_sinks"] = dsink[0, :A_N_HEADS]
    dq = _attn_dq(qkv, dob, lse, delta, cos, sin, B, S, 1, name=f"{tag}_dq", plan=plan, **kw)
    dk, dv = _attn_dkv(qkv, dob, lse, delta, cos, sin, B, S, 1, name=f"{tag}_dkv", plan=plan, **kw)
    dqkv = jnp.concatenate([dq, dk, dv], axis=1)
    g["a_w_qkv"] = _matmul(u, dqkv, ta=True, name=f"{tag}_dwqkv")
    g["a_b_qkv"] = _colsum(dqkv, f"{tag}_dbqkv")[0]
    if plan is not None:
        plan.grads(3 * j, "mix", {"a_w_qkv": g["a_w_qkv"], "a_w_o": g["a_w_o"]})
    dh, dnw = _matmul(dqkv, p["a_w_qkv"][j], tb=True, norm_bwd=(norm[0], norm[1], dh1), name=f"{tag}_du", plan=plan)
    return dh, dnw, g


def _group_cols(gi, qkv):
    W = C_HEADS * HEAD_DIM
    if C_PATTERNS[gi][1] == 1:
        return qkv, (gi * W, (3 + gi) * W, (6 + gi) * W)
    part = jnp.concatenate([qkv[:, (3 * j + gi) * W:(3 * j + gi + 1) * W] for j in range(3)], axis=1)
    return part, (0, W, 2 * W)


def _dil_fwd(u, h, p, B, S, cos, sin, nw, plan=None):
    W = C_HEADS * HEAD_DIM
    qkv = _matmul(u, p["c_w_qkv"][0], out_dtype=BF16, rope=(cos, sin), rope_cols=6 * W, name="c_qkv", plan=plan)
    os_, lses, parts = [], [], []
    for gi, (window, dil) in enumerate(C_PATTERNS):
        part, (qc, kc, vc) = _group_cols(gi, qkv)
        o, lse = _attn_fwd(part, B, S, dil, n_heads=C_HEADS, n_kv=C_HEADS, q_col=qc, k_col=kc, v_col=vc,
                           max_dist=window // dil, sinks=None, name=f"c_attn{gi}", plan=plan)
        os_.append(o)
        lses.append(lse)
        parts.append((part, (qc, kc, vc)))
    o, lse = _merge(os_, lses)
    h1, u2 = _matmul(o, p["c_w_o"][0], resid=h, norm_out=nw, name="c_o")
    return h1, u2, (parts, o, lse)


def _dil_bwd(dh1, u, saved, p, B, S, cos, sin, norm, plan=None):
    parts, o, lse = saved
    g = {}
    do = _matmul(dh1, p["c_w_o"][0], tb=True, name="c_do")
    g["c_w_o"] = _matmul(o, dh1, ta=True, name="c_dwo")[None]
    delta, dob = _delta(do, o, name="c_delta")
    dqs, dks, dvs = [], [], []
    for gi, (window, dil) in enumerate(C_PATTERNS):
        part, (qc, kc, vc) = parts[gi]
        kw = dict(n_heads=C_HEADS, n_kv=C_HEADS, q_col=qc, k_col=kc, v_col=vc, max_dist=window // dil)
        dqs.append(_attn_dq(part, dob, lse, delta, cos, sin, B, S, dil, name=f"c_dq{gi}", **kw))
        dk, dv = _attn_dkv(part, dob, lse, delta, cos, sin, B, S, dil, name=f"c_dkv{gi}", **kw)
        dks.append(dk)
        dvs.append(dv)
    dqkv = jnp.concatenate(dqs + dks + dvs, axis=1)
    g["c_w_qkv"] = _matmul(u, dqkv, ta=True, name="c_dwqkv", plan=plan)[None]
    if plan is not None:
        plan.grads(2, "mix", {"c_w_qkv": g["c_w_qkv"][0], "c_w_o": g["c_w_o"][0]})
    dh, dnw = _matmul(dqkv, p["c_w_qkv"][0], tb=True, norm_bwd=(norm[0], norm[1], dh1), name="c_du", plan=plan)
    return dh, dnw, g


def _ssm_params(p):
    par = jnp.stack([p["b_dt_bias"][0], p["b_a_log"][0], p["b_d"][0]], axis=0)
    prow = par.reshape(3, SSM_N_GROUPS, SSM_HG).transpose(1, 0, 2)
    return prow, prow.transpose(0, 2, 1)


def _mamba_fwd(u, h, p, B, S, nw, plan=None):
    T = B * S
    G, HG = SSM_N_GROUPS, SSM_HG
    w_in = p["b_in_w"][0]
    nzx = SSM_D_INNER + SSM_CONV_DIM
    w_dt = jnp.pad(w_in[:, nzx:], ((0, 0), (0, LANES - SSM_N_HEADS)))
    zx = _matmul(u, w_in[:, :nzx], name="b_zx", plan=plan)
    dtraw = _matmul(u, w_dt, name="b_dt")[:, :SSM_N_HEADS]
    dtc = dtraw.reshape(B, S, G, HG).transpose(0, 2, 1, 3)
    dtr = dtraw.reshape(B, S, G, HG).transpose(0, 2, 3, 1)
    prow, pcol = _ssm_params(p)
    zx3 = zx.reshape(B, S, nzx)
    xc3 = _conv_fwd(zx3, p["b_conv_w"][0], p["b_conv_b"])
    y3, states = _ssd_fwd(xc3, dtc, dtr, prow, pcol, plan=plan)
    y = y3.reshape(T, SSM_D_INNER)
    gn = _gate_fwd(y, zx, p["b_norm_w"])
    h1, u2 = _matmul(gn, p["b_out_w"][0], resid=h, norm_out=nw, name="b_out", plan=plan)
    return h1, u2, (zx, dtc, dtr, xc3, y, states, gn, w_dt)


def _mamba_bwd(dh1, u, saved, p, B, S, norm, plan=None):
    T = B * S
    zx, dtc, dtr, xc3, y, states, gn, w_dt = saved
    nzx = SSM_D_INNER + SSM_CONV_DIM
    w_in = p["b_in_w"][0]
    prow, pcol = _ssm_params(p)
    g = {}
    dgn = _matmul(dh1, p["b_out_w"][0], tb=True, name="b_dgn")
    g["b_out_w"] = _matmul(gn, dh1, ta=True, name="b_dwout")[None]
    dy, dz, dnw = _gate_bwd(dgn, y, zx, p["b_norm_w"])
    g["b_norm_w"] = dnw
    dx3, dB3, dC3, ddt, dpar = _ssd_bwd(xc3, dtc, dtr, prow, pcol, states, dy.reshape(B, S, SSM_D_INNER), plan=plan)
    dpar = dpar.transpose(1, 0, 2).reshape(3, SSM_N_HEADS)
    g["b_dt_bias"], g["b_a_log"], g["b_d"] = dpar[0:1], dpar[1:2], dpar[2:3]
    zx3 = zx.reshape(B, S, nzx)
    cw, cb = p["b_conv_w"][0], p["b_conv_b"]
    parts, dws, dbs = [], [], []
    for col0, dpart, nm in ((0, dx3, "b_conv_bwd_x"), (SSM_D_INNER, dB3, "b_conv_bwd_b"),
                            (SSM_D_INNER + SSM_BC_DIM, dC3, "b_conv_bwd_c")):
        dxp, dw, db = _conv_bwd(zx3, dpart, cw, cb, col0, nm)
        parts.append(dxp.reshape(T, -1))
        dws.append(dw)
        dbs.append(db)
    g["b_conv_w"] = jnp.concatenate(dws, axis=1)[None]
    g["b_conv_b"] = jnp.concatenate(dbs, axis=1)
    dzx = jnp.concatenate([dz] + parts, axis=1)
    ddtraw = ddt.transpose(0, 2, 1, 3).reshape(T, SSM_N_HEADS)
    ddtp = jnp.pad(ddtraw, ((0, 0), (0, LANES - SSM_N_HEADS)))
    dw_zx = _matmul(u, dzx, ta=True, name="b_dwzx")
    dw_dt = _matmul(u, ddtp, ta=True, name="b_dwdt")[:, :SSM_N_HEADS]
    g["b_in_w"] = jnp.concatenate([dw_zx, dw_dt], axis=1)[None]
    if plan is not None:
        plan.grads(1, "mix", {"b_in_w": g["b_in_w"][0], "b_out_w": g["b_out_w"][0]})
    du = _matmul(dzx, w_in[:, :nzx], tb=True, name="b_du_zx", plan=plan)
    dh, dnw = _matmul(ddtp, w_dt, tb=True, resid=du, norm_bwd=(norm[0], norm[1], dh1), name="b_du_dt")
    return dh, dnw, g


def _local_step(x, positions, p, target, plan=None):
    B, S, D = x.shape
    T = B * S
    cos, sin = _rope_tables(positions)
    h = x.reshape(T, D)
    tape = []
    u = _rmsnorm_fwd(h, p["norm_mix_w"][0], "l0_norm_mix")
    for i in range(DEPTH):
        kind, j = i % 3, i // 3
        nw = p["norm_mlp_w"][i]
        if kind == 0:
            h1, u2, saved = _swa_fwd(u, h, p, j, B, S, cos, sin, f"a{j}", nw, plan)
        elif kind == 1:
            h1, u2, saved = _mamba_fwd(u, h, p, B, S, nw, plan)
        else:
            h1, u2, saved = _dil_fwd(u, h, p, B, S, cos, sin, nw, plan)
        r, s = _matmul(u2, p["mlp_w_up"][i], out_dtype=BF16, relu2=True, name=f"l{i}_up", plan=plan)
        if i + 1 < DEPTH:
            h2, u_next = _matmul(s, p["mlp_w_down"][i], resid=h1, norm_out=p["norm_mix_w"][i + 1], name=f"l{i}_down", plan=plan)
        else:
            h2, u_next = _matmul(s, p["mlp_w_down"][i], resid=h1, name=f"l{i}_down", plan=plan), None
        tape.append((h, u, saved, h1, u2, r, s))
        h, u = h2, u_next
    dh, dwf, loss = _final_loss(h, target.reshape(T, D), p["final_norm_w"])
    grads = {"final_norm_w": dwf[0]}
    per_layer = {n: [None] * DEPTH for n in ("norm_mix_w", "norm_mlp_w", "mlp_w_up", "mlp_w_down")}
    a_grads = [None, None]
    for i in reversed(range(DEPTH)):
        kind, j = i % 3, i // 3
        h0, u, saved, h1, u2, r, s = tape[i]
        da = _matmul(dh, p["mlp_w_down"][i], tb=True, out_dtype=BF16, mul=r, mul_scale=2.0, name=f"l{i}_da", plan=plan)
        per_layer["mlp_w_down"][i] = _matmul(s, dh, ta=True, name=f"l{i}_dwdown")
        per_layer["mlp_w_up"][i] = _matmul(u2, da, ta=True, name=f"l{i}_dwup")
        if plan is not None:
            plan.grads(i, "mlp", {"mlp_w_up": per_layer["mlp_w_up"][i], "mlp_w_down": per_layer["mlp_w_down"][i]})
        dh1, dnw = _matmul(da, p["mlp_w_up"][i], tb=True, norm_bwd=(h1, p["norm_mlp_w"][i], dh), name=f"l{i}_du2", plan=plan)
        per_layer["norm_mlp_w"][i] = dnw[0]
        norm = (h0, p["norm_mix_w"][i])
        if kind == 0:
            dh, dnw, g = _swa_bwd(dh1, u, saved, p, j, B, S, cos, sin, f"a{j}", norm, plan)
            a_grads[j] = g
        elif kind == 1:
            dh, dnw, g = _mamba_bwd(dh1, u, saved, p, B, S, norm, plan)
            grads.update(g)
        else:
            dh, dnw, g = _dil_bwd(dh1, u, saved, p, B, S, cos, sin, norm, plan)
            grads.update(g)
        per_layer["norm_mix_w"][i] = dnw[0]
    for n in ("norm_mix_w", "norm_mlp_w"):
        grads[n] = jnp.stack(per_layer[n], axis=0)
    for n in ("mlp_w_up", "mlp_w_down"):
        grads[n] = per_layer[n]
    for n in ("a_b_qkv", "a_sinks", "a_b_o"):
        grads[n] = jnp.stack([a_grads[0][n], a_grads[1][n]], axis=0)
    for n in ("a_w_qkv", "a_w_o"):
        grads[n] = [a_grads[0][n], a_grads[1][n]]
    for n in ("b_in_w", "b_out_w", "c_w_qkv", "c_w_o"):
        grads[n] = [grads[n][0]]
    return loss, dh.reshape(B, S, D), grads


MIX = {0: ("a_w_qkv", "a_w_o"), 1: ("b_in_w", "b_out_w"), 2: ("c_w_qkv", "c_w_o")}
MLP = ("mlp_w_up", "mlp_w_down")
GATHER_FIRST = (0, MIX[0])
GATHER_HOSTS = {"a0_qkv": ((0, ("mlp_w_up",)),), "a0_attn": ((0, ("mlp_w_down",)),), "l0_up": ((1, ("b_in_w",)),),
                "l0_down": ((1, ("b_out_w",)),), "b_zx": ((1, ("mlp_w_up",)),),
                "b_ssd_fwd": ((1, ("mlp_w_down",)), (2, ("c_w_qkv",))), "b_out": ((2, ("c_w_o",)),),
                "l1_up": ((2, ("mlp_w_up",)),), "l1_down": ((2, ("mlp_w_down",)),),
                "c_qkv": ((3, MLP),), "c_attn1": ((3, MIX[0]),)}
GATHER_LONG_HOSTS = ("b_ssd_fwd",)
REDUCE_HOSTS = {(3, "mlp"): ("l3_du2", "a1_dkv"), (3, "mix"): ("a1_du", "l2_da"),
                (2, "mlp"): ("l2_du2", "c_dwqkv"), (2, "mix"): ("c_du", "b_ssd_bwd"),
                (1, "mlp"): ("l1_du2", "b_ssd_bwd"), (1, "mix"): ("b_du_zx", {"b_in_w": "a0_dq", "b_out_w": "l0_da"}),
                (0, "mlp"): ("l0_du2", "a0_dkv"), (0, "mix"): (None, None)}


class _Plan:
    def __init__(self, w, m, v, p, dev, chip, core):
        self.w, self.m, self.v, self.p, self.dev, self.chip, self.core = w, m, v, p, dev, chip, core
        self.pending = {}
        self.res = {n: None for n in BIG_KIND}
        self._install(*GATHER_FIRST)(_gather(self._gather_items(*GATHER_FIRST), "gather_first"))
        for host, groups in GATHER_HOSTS.items():
            for i, only in groups:
                mid = 0.7 if host in GATHER_LONG_HOSTS else 0.9
                self._wait_for(host, _gather_comm(self._gather_items(i, only), mid), self._install(i, only))

    def _wait_for(self, host, comm, done):
        self.pending.setdefault(host, []).append((comm, done))

    def _names(self, i, only):
        return [(n, l) for n, l in _layer_big(i) if only is None or n in only]

    def _gather_items(self, i, only):
        items = []
        for n, l in self._names(i, only):
            kind, s2 = BIG_KIND[n], self.w[n].shape[1:]
            placed = _place(self.w[n], l, kind, _full2d(kind, s2), self.dev, f"place_l{i}_{n}")
            items.append((placed, kind, _block_size(kind, s2), _full2d(kind, s2), True))
        return items

    def _install(self, i, only):
        def done(fulls):
            for (n, l), t in zip(self._names(i, only), fulls):
                self.p[n][l] = _from_slots(t, 1) if BIG_KIND[n] == "slot" else t
        return done

    def take(self, host):
        return _Comm.merge([c for c, _ in self.pending[host]]) if host in self.pending else None

    def give(self, host, results):
        for comm, done in self.pending.pop(host):
            done(results[:len(comm.out_shapes)])
            results = results[len(comm.out_shapes):]

    def grads(self, i, group, grads):
        names = self._names(i, MLP if group == "mlp" else MIX[i % 3])
        items = []
        for n, _ in names:
            kind, s2 = BIG_KIND[n], self.w[n].shape[1:]
            items.append((_to_slots(grads[n], 1) if kind == "slot" else grads[n], kind, _block_size(kind, s2), s2))
        d2d_host, ici_host = REDUCE_HOSTS[(i, group)]
        tag = f"l{i}_{group}"

        def update(sel, parts):
            def done(recv):
                for (n, l), pt, r in zip(sel, parts, recv):
                    self.res[n] = _adamw(pt, r, self.w[n], self.m[n], self.v[n], l, self.res[n], self.chip, f"adamw_l{i}_{n}")
            return done

        def second(sib):
            parts = [_pair_sum(it[0], s, it[1], self.core, f"pair_sum_l{i}_{n}") for (n, _), it, s in zip(names, items, sib)]
            hosts = ici_host if isinstance(ici_host, dict) else {n: ici_host for n, _ in names}
            for h in dict.fromkeys(hosts[n] for n, _ in names):
                ks = [k for k, (n, _) in enumerate(names) if hosts[n] == h]
                sel, pts = [names[k] for k in ks], [parts[k] for k in ks]
                self._send(h, _reduce_ici_comm(pts), update(sel, pts), f"reduce_ici_{tag}_{sel[0][0]}")

        self._send(d2d_host, _reduce_d2d_comm(items), second, f"reduce_d2d_{tag}")

    def _send(self, host, comm, done, name):
        if host is None:
            done(_run_comm(comm, name))
        else:
            self._wait_for(host, comm, done)

    def flush(self):
        late = 0
        while self.pending:
            host = next(iter(self.pending))
            for comm, done in self.pending.pop(host):
                done(_run_comm(comm, f"late_{late}_{host}"))
                late += 1


def kernel(x, positions, norm_mix_w, norm_mlp_w, a_w_qkv, a_b_qkv, a_sinks, a_w_o, a_b_o, b_in_w, b_conv_w, b_conv_b, b_dt_bias, b_a_log, b_d, b_norm_w, b_out_w, c_w_qkv, c_w_o, mlp_w_up, mlp_w_down, final_norm_w, loss_target, m_norm_mix_w, m_norm_mlp_w, m_a_w_qkv, m_a_b_qkv, m_a_sinks, m_a_w_o, m_a_b_o, m_b_in_w, m_b_conv_w, m_b_conv_b, m_b_dt_bias, m_b_a_log, m_b_d, m_b_norm_w, m_b_out_w, m_c_w_qkv, m_c_w_o, m_mlp_w_up, m_mlp_w_down, m_final_norm_w, v_norm_mix_w, v_norm_mlp_w, v_a_w_qkv, v_a_b_qkv, v_a_sinks, v_a_w_o, v_a_b_o, v_b_in_w, v_b_conv_w, v_b_conv_b, v_b_dt_bias, v_b_a_log, v_b_d, v_b_norm_w, v_b_out_w, v_c_w_qkv, v_c_w_o, v_mlp_w_up, v_mlp_w_down, v_final_norm_w):
    w = dict(zip(W_NAMES, (norm_mix_w, norm_mlp_w, a_w_qkv, a_b_qkv, a_sinks, a_w_o, a_b_o, b_in_w, b_conv_w, b_conv_b,
                           b_dt_bias, b_a_log, b_d, b_norm_w, b_out_w, c_w_qkv, c_w_o, mlp_w_up, mlp_w_down, final_norm_w)))
    m = dict(zip(W_NAMES, (m_norm_mix_w, m_norm_mlp_w, m_a_w_qkv, m_a_b_qkv, m_a_sinks, m_a_w_o, m_a_b_o, m_b_in_w,
                           m_b_conv_w, m_b_conv_b, m_b_dt_bias, m_b_a_log, m_b_d, m_b_norm_w, m_b_out_w, m_c_w_qkv, m_c_w_o,
                           m_mlp_w_up, m_mlp_w_down, m_final_norm_w)))
    v = dict(zip(W_NAMES, (v_norm_mix_w, v_norm_mlp_w, v_a_w_qkv, v_a_b_qkv, v_a_sinks, v_a_w_o, v_a_b_o, v_b_in_w,
                           v_b_conv_w, v_b_conv_b, v_b_dt_bias, v_b_a_log, v_b_d, v_b_norm_w, v_b_out_w, v_c_w_qkv, v_c_w_o,
                           v_mlp_w_up, v_mlp_w_down, v_final_norm_w)))
    px, py, pc = lax.axis_index("x"), lax.axis_index("y"), lax.axis_index("c")
    me = 4 * px + 2 * py + pc
    dev, chip, core = (t.astype(jnp.int32).reshape(1) for t in (me, 2 * px + py, pc))

    trio = tuple(SMALL_SHARDED)
    got = _gather([(d[n], "slot", None, (N_DEV,) + d[n].shape, False) for n in trio for d in (w, m, v)], "gather_small")
    slots = {n: got[3 * i:3 * i + 3] for i, n in enumerate(trio)}
    p = {n: w[n] for n in SMALL_REPLICATED}
    for n in trio:
        p[n] = _from_slots(slots[n][0], SMALL_SHARDED[n])
    for n in BIG_KIND:
        p[n] = [None] * w[n].shape[0]
    plan = _Plan(w, m, v, p, dev, chip, core)
    loss_part, dx, grads = _local_step(x, positions, p, loss_target, plan)
    loss = lax.psum(loss_part[0, 0], AXES)
    plan.flush()
    out = {n: list(plan.res[n]) for n in BIG_KIND}

    small = SMALL_REPLICATED + trio
    as2d = lambda t: t.reshape(1, -1) if t.ndim == 1 else t
    g_sm = [as2d(grads[n]) for n in SMALL_REPLICATED] + [_to_slots(grads[n].reshape(p[n].shape), SMALL_SHARDED[n]) for n in trio]
    gathered = _gather([(g, "slot", None, (N_DEV,) + g.shape, False) for g in g_sm], "gather_small_grads")
    ws = [as2d(w[n]) for n in SMALL_REPLICATED] + [slots[n][0] for n in trio]
    ms = [as2d(m[n]) for n in SMALL_REPLICATED] + [slots[n][1] for n in trio]
    vs = [as2d(v[n]) for n in SMALL_REPLICATED] + [slots[n][2] for n in trio]
    sm_out = _small_adamw(gathered, ws, ms, vs)
    for i, n in enumerate(small):
        if n in SMALL_SHARDED:
            out[n] = [lax.dynamic_index_in_dim(sm_out[k][i], me, 0, keepdims=False) for k in range(4)]
        else:
            out[n] = [sm_out[k][i].reshape(w[n].shape) for k in range(4)]
    return (loss, dx, *[out[n][0] for n in W_NAMES], *[out[n][1] for n in W_NAMES], *[out[n][2] for n in W_NAMES],
            *[out[n][3] for n in W_NAMES])
```

```python
import math

import jax
import jax.numpy as jnp
from jax import lax
from jax.experimental import pallas as pl
from jax.experimental.pallas import tpu as pltpu

F32 = jnp.float32
BF16 = jnp.bfloat16
SDS = jax.ShapeDtypeStruct

D_MODEL = 1024
DEPTH = 4
BLOCK = 128
ROPE_THETA = 10000.0
NORM_EPS = 1e-5
HEAD_DIM = 64
A_N_HEADS = 16
A_N_KV = 2
A_WINDOW = 128
A_Q_DIM = 1024
A_KV_DIM = 128
SSM_D_INNER = 2048
SSM_N_HEADS = 32
SSM_N_GROUPS = 8
SSM_HG = 4
SSM_D_STATE = 128
SSM_CONV = 4
SSM_CHUNK = 128
SSM_BC_DIM = 1024
SSM_CONV_DIM = 4096
C_PATTERNS = ((128, 1), (512, 4), (2048, 16))
C_HEADS = 16
ADAM_LR, ADAM_B1, ADAM_B2, ADAM_EPS, ADAM_WD, ADAM_STEP = 0.001, 0.9, 0.999, 1e-08, 0.01, 10

N_DEV = 8
AXES = ("x", "y", "c")
LANES = 128
VMEM_LIMIT = 56 * 1024 * 1024
STREAM_VMEM = 16 * 1024 * 1024
NEG = -1e30

NN = (((1,), (0,)), ((), ()))
NT = (((1,), (1,)), ((), ()))
TN = (((0,), (0,)), ((), ()))
HI = lax.Precision.HIGHEST


def _pick(n, cap, mult=LANES):
    best = None
    for t in range(mult, min(n, cap) + 1, mult):
        if n % t == 0:
            best = t
    return best if best is not None else n


def _params(sem):
    return pltpu.CompilerParams(dimension_semantics=sem, vmem_limit_bytes=VMEM_LIMIT)


def _bf(x):
    return x if x.dtype == BF16 else x.astype(BF16)


def _rot_half(y):
    n = y.shape[-1]
    lane = lax.broadcasted_iota(jnp.int32, y.shape, y.ndim - 1)
    return jnp.where((lane % HEAD_DIM) < HEAD_DIM // 2, -pltpu.roll(y, n - 32, y.ndim - 1), pltpu.roll(y, 32, y.ndim - 1))


def _rope(y, cos, sin, sign):
    reps = y.shape[-1] // LANES
    c = jnp.tile(cos, (1, reps)) if reps > 1 else cos
    s = jnp.tile(sin, (1, reps)) if reps > 1 else sin
    return y * c + sign * (_rot_half(y) * s)


MESH = pl.DeviceIdType.MESH
ANY = pl.BlockSpec(memory_space=pl.ANY)


class _Comm:
    def __init__(self, inputs, out_shapes, aliases, sems, phases):
        self.inputs, self.out_shapes, self.aliases, self.sems, self.phases = inputs, out_shapes, aliases, sems, phases

    @staticmethod
    def merge(comms):
        if len(comms) == 1:
            return comms[0]
        ins, outs, aliases, sems, spans = [], [], {}, [], []
        for c in comms:
            aliases.update({len(ins) + i: len(outs) + j for i, j in c.aliases.items()})
            spans.append((len(ins), len(ins) + len(c.inputs), len(outs), len(outs) + len(c.out_shapes), len(sems),
                          len(sems) + len(c.sems)))
            ins, outs, sems = ins + list(c.inputs), outs + list(c.out_shapes), sems + list(c.sems)
        phases = []
        for f in sorted({f for c in comms for f, _ in c.phases}):
            todo = [(fn, sp) for c, sp in zip(comms, spans) for g, fn in c.phases if g == f]

            def run(cins, couts, csems, todo=todo):
                for fn, (i0, i1, o0, o1, s0, s1) in todo:
                    fn(cins[i0:i1], couts[o0:o1], csems[s0:s1])
            phases.append((f, run))
        return _Comm(ins, outs, aliases, sems, phases)


def _pc(body, args, *, out_shape, grid, in_specs, out_specs, name, sem, scratch_shapes=(), comm=None):
    single = not isinstance(out_shape, (tuple, list))
    outs, ospecs = ([out_shape], [out_specs]) if single else (list(out_shape), list(out_specs))
    unpack = (lambda r: r[0]) if single else (lambda r: tuple(r))
    if comm is None:
        res = pl.pallas_call(body, out_shape=outs, grid=grid, in_specs=list(in_specs), out_specs=ospecs,
                             scratch_shapes=list(scratch_shapes), name=name, compiler_params=_params(sem))(*args)
        return unpack(res)
    n_in, n_out, n_scr = len(in_specs), len(outs), len(scratch_shapes)
    c_in, c_out = len(comm.inputs), len(comm.out_shapes)
    total = math.prod(grid)
    steps = [min(total - 1, int(f * total)) for f, _ in comm.phases[:-1]]

    def wrapped(*refs):
        ins, cins = refs[:n_in], refs[n_in:n_in + c_in]
        o = refs[n_in + c_in:n_in + c_in + n_out]
        couts = refs[n_in + c_in + n_out:n_in + c_in + n_out + c_out]
        rest = refs[n_in + c_in + n_out + c_out:]
        scr, csems = rest[:n_scr], rest[n_scr:]
        step = pl.program_id(0)
        for ax in range(1, len(grid)):
            step = step * grid[ax] + pl.program_id(ax)
        for (_, fn), st in zip(comm.phases[:-1], steps):
            @pl.when(step == st)
            def _(fn=fn):
                fn(cins, couts, csems)
        body(*ins, *o, *scr)

        @pl.when(step == total - 1)
        def _():
            comm.phases[-1][1](cins, couts, csems)

    res = pl.pallas_call(
        wrapped, out_shape=outs + list(comm.out_shapes), grid=grid, in_specs=list(in_specs) + [ANY] * c_in,
        out_specs=ospecs + [ANY] * c_out, scratch_shapes=list(scratch_shapes) + list(comm.sems),
        input_output_aliases={n_in + i: n_out + j for i, j in comm.aliases.items()}, name=name,
        compiler_params=_params(("arbitrary",) * len(grid)),
    )(*args, *comm.inputs)
    return unpack(res[:n_out]), list(res[n_out:])


def _hosted(plan, name, run):
    comm = plan.take(name) if plan is not None else None
    if comm is None:
        return run(None)
    res, extra = run(comm)
    plan.give(name, extra)
    return res


MM_VMEM = 40 * 1024 * 1024
HBM_BYTES_PER_US = 2.5e6
STEP_US = 0.35


def _divisors(n, cands):
    return [c for c in cands if c <= n and n % c == 0] or [n]


def _mm_tiles(M, N, K, sa, sb, out_bytes, extra_bytes, full_rows=False):
    best = None
    for tm in _divisors(M, (2048, 1024, 512, 256)):
        for tn in ([N] if full_rows else _divisors(N, (1024, 640, 512, 256, 128))):
            for tk in _divisors(K, (K, K // 2, K // 3, K // 4, 2048, 1024, 640, 512)):
                if tk != K and tk % LANES:
                    continue
                nk = K // tk
                b_bufs = 1 if (nk == 1 and N == tn) else 2
                vmem = 2 * tm * tk * sa + b_bufs * tk * tn * sb + tm * tn * (2 * (out_bytes + extra_bytes) + 8 + (4 if nk > 1 else 0))
                if vmem > MM_VMEM:
                    continue
                a_traffic = M * K * sa * (1 if nk == 1 else N // tn)
                b_traffic = K * N * sb * (1 if (nk == 1 and N == tn) else M // tm)
                steps = (M // tm) * (N // tn) * nk
                cost = (a_traffic + b_traffic + M * N * (out_bytes + extra_bytes)) / HBM_BYTES_PER_US + steps * STEP_US
                cost += (M // tm) * (N // tn) * (nk - 1) * tm * tn * 8 / (4 * HBM_BYTES_PER_US)
                if best is None or cost < best[0]:
                    best = (cost, tm, tn, tk)
    assert best is not None, (M, N, K)
    return best[1:]


def _matmul(a, b, *, ta=False, tb=False, out_dtype=F32, bias=None, resid=None, mul=None, mul_scale=1.0,
            relu2=False, rope=None, rope_cols=0, norm_out=None, norm_bwd=None, name="mm", plan=None):
    M = a.shape[1] if ta else a.shape[0]
    K = a.shape[0] if ta else a.shape[1]
    N = b.shape[0] if tb else b.shape[1]
    assert (b.shape[1] if tb else b.shape[0]) == K
    two_out = relu2 or norm_out is not None
    out_bytes = jnp.dtype(out_dtype).itemsize * (2 if relu2 else 1) + (2 if norm_out is not None else 0)
    extra_bytes = (4 if resid is not None else 0) + (mul.dtype.itemsize if mul is not None else 0) + (8 if norm_bwd else 0)
    rows = norm_out is not None or norm_bwd is not None
    tm, tn, tk = _mm_tiles(M, N, K, a.dtype.itemsize, b.dtype.itemsize, out_bytes, extra_bytes, full_rows=rows)
    nk = K // tk
    dims = (((0 if ta else 1,), (1 if tb else 0,)), ((), ()))

    def body(*refs):
        it = iter(refs)
        a_ref, b_ref = next(it), next(it)
        bias_ref = next(it) if bias is not None else None
        resid_ref = next(it) if resid is not None else None
        mul_ref = next(it) if mul is not None else None
        cos_ref, sin_ref = (next(it), next(it)) if rope is not None else (None, None)
        nw_ref = next(it) if rows else None
        h_ref, dres_ref = (next(it), next(it)) if norm_bwd is not None else (None, None)
        o_ref = next(it)
        o2_ref = next(it) if two_out or norm_bwd is not None else None
        acc_ref = next(it) if nk > 1 else None
        k = pl.program_id(2)
        part = lax.dot_general(_bf(a_ref[...]), _bf(b_ref[...]), dims, preferred_element_type=F32)
        if nk > 1:
            @pl.when(k == 0)
            def _():
                acc_ref[...] = part

            @pl.when(k > 0)
            def _():
                acc_ref[...] += part

        @pl.when(k == nk - 1)
        def _():
            y = acc_ref[...] if nk > 1 else part
            if bias_ref is not None:
                y = y + bias_ref[...]
            if rope is not None and rope_cols % tn == 0 and not (two_out or rows or mul is not None or resid is not None):
                rotated = pl.program_id(1) * tn < rope_cols

                @pl.when(rotated)
                def _():
                    o_ref[...] = _rope(y, cos_ref[...], sin_ref[...], 1.0).astype(o_ref.dtype)

                @pl.when(jnp.logical_not(rotated))
                def _():
                    o_ref[...] = y.astype(o_ref.dtype)
                return
            if rope is not None:
                col = pl.program_id(1) * tn + lax.broadcasted_iota(jnp.int32, y.shape, 1)
                y = jnp.where(col < rope_cols, _rope(y, cos_ref[...], sin_ref[...], 1.0), y)
            if mul_ref is not None:
                y = y * (mul_ref[...].astype(F32) * mul_scale)
            if resid_ref is not None:
                y = y + resid_ref[...]
            if relu2:
                r = jnp.maximum(y, 0.0)
                o_ref[...] = r.astype(o_ref.dtype)
                o2_ref[...] = (r * r).astype(o2_ref.dtype)
            elif norm_bwd is not None:
                x = h_ref[...]
                rstd = lax.rsqrt(jnp.mean(x * x, axis=-1, keepdims=True) + NORM_EPS)
                g = y * nw_ref[...]
                o_ref[...] = dres_ref[...] + rstd * g - x * (rstd * rstd * rstd) * jnp.mean(g * x, axis=-1, keepdims=True)
                dw = jnp.sum(y * x * rstd, axis=0, keepdims=True)
                first = pl.program_id(0) == 0

                @pl.when(first)
                def _():
                    o2_ref[...] = dw

                @pl.when(jnp.logical_not(first))
                def _():
                    o2_ref[...] += dw
            else:
                o_ref[...] = y.astype(o_ref.dtype)
                if norm_out is not None:
                    rstd = lax.rsqrt(jnp.mean(y * y, axis=-1, keepdims=True) + NORM_EPS)
                    o2_ref[...] = (y * rstd * nw_ref[...]).astype(BF16)

    a_spec = pl.BlockSpec((tk, tm), lambda i, j, k: (k, i)) if ta else pl.BlockSpec((tm, tk), lambda i, j, k: (i, k))
    mode = pl.Buffered(1) if (nk == 1 and N == tn) else None
    b_spec = (pl.BlockSpec((tn, tk), lambda i, j, k: (j, k), pipeline_mode=mode) if tb
              else pl.BlockSpec((tk, tn), lambda i, j, k: (k, j), pipeline_mode=mode))
    mn_spec = pl.BlockSpec((tm, tn), lambda i, j, k: (i, j))
    in_specs, args = [a_spec, b_spec], [a, b]
    if bias is not None:
        in_specs.append(pl.BlockSpec((1, tn), lambda i, j, k: (0, j)))
        args.append(bias)
    if resid is not None:
        in_specs.append(mn_spec)
        args.append(resid)
    if mul is not None:
        in_specs.append(mn_spec)
        args.append(mul)
    if rope is not None:
        in_specs += [pl.BlockSpec((tm, LANES), lambda i, j, k: (i, 0))] * 2
        args += [rope[0], rope[1]]
    vec_spec = pl.BlockSpec((1, tn), lambda i, j, k: (0, j))
    if rows:
        in_specs.append(vec_spec)
        args.append((norm_out if norm_out is not None else norm_bwd[1]).reshape(1, N))
    if norm_bwd is not None:
        in_specs += [mn_spec, mn_spec]
        args += [norm_bwd[0], norm_bwd[2]]
    out_shape = SDS((M, N), out_dtype)
    out_specs = mn_spec
    if relu2:
        out_shape, out_specs = (out_shape, out_shape), (mn_spec, mn_spec)
    elif norm_out is not None:
        out_shape, out_specs = (out_shape, SDS((M, N), BF16)), (mn_spec, mn_spec)
    elif norm_bwd is not None:
        out_shape, out_specs = (out_shape, SDS((1, N), F32)), (mn_spec, vec_spec)
    sem = ("arbitrary",) * 3 if norm_bwd is not None else ("parallel", "parallel", "arbitrary")
    return _hosted(plan, name, lambda comm: _pc(
        body, args, out_shape=out_shape, grid=(M // tm, N // tn, nk), in_specs=in_specs, out_specs=out_specs,
        scratch_shapes=[pltpu.VMEM((tm, tn), F32)] if nk > 1 else [], name=name, sem=sem, comm=comm))


def _colsum(x, name):
    T, N = x.shape
    tm = _pick(T, 1024, 8)

    def body(x_ref, o_ref):
        s = jnp.sum(x_ref[...].astype(F32), axis=0, keepdims=True)

        @pl.when(pl.program_id(0) == 0)
        def _():
            o_ref[...] = s

        @pl.when(pl.program_id(0) > 0)
        def _():
            o_ref[...] += s

    return pl.pallas_call(
        body, out_shape=SDS((1, N), F32), grid=(T // tm,),
        in_specs=[pl.BlockSpec((tm, N), lambda i: (i, 0))], out_specs=pl.BlockSpec((1, N), lambda i: (0, 0)),
        name=name, compiler_params=_params(("arbitrary",)),
    )(x)


def _rmsnorm_fwd(h, w, name):
    T, D = h.shape
    tm = _pick(T, 512, 8)

    def body(h_ref, w_ref, o_ref):
        x = h_ref[...]
        rstd = lax.rsqrt(jnp.mean(x * x, axis=-1, keepdims=True) + NORM_EPS)
        o_ref[...] = (x * rstd * w_ref[...]).astype(BF16)

    return pl.pallas_call(
        body, out_shape=SDS((T, D), BF16), grid=(T // tm,),
        in_specs=[pl.BlockSpec((tm, D), lambda i: (i, 0)), pl.BlockSpec((1, D), lambda i: (0, 0))],
        out_specs=pl.BlockSpec((tm, D), lambda i: (i, 0)), name=name, compiler_params=_params(("parallel",)),
    )(h, w.reshape(1, D))


def _final_loss(h, target, w):
    T, D = h.shape
    tm = _pick(T, 512, 8)

    def body(h_ref, t_ref, w_ref, dh_ref, dw_ref, loss_ref):
        x = h_ref[...]
        rstd = lax.rsqrt(jnp.mean(x * x, axis=-1, keepdims=True) + NORM_EPS)
        xn = x * rstd
        err = xn * w_ref[...] - t_ref[...]
        part = 0.5 * jnp.sum(jnp.mean(err * err, axis=-1, keepdims=True), axis=0, keepdims=True)
        dy = err * (1.0 / D)
        g = dy * w_ref[...]
        dh_ref[...] = rstd * g - x * (rstd * rstd * rstd) * jnp.mean(g * x, axis=-1, keepdims=True)
        dw = jnp.sum(dy * xn, axis=0, keepdims=True)
        lp = jnp.broadcast_to(part, (1, LANES))

        @pl.when(pl.program_id(0) == 0)
        def _():
            dw_ref[...] = dw
            loss_ref[...] = lp

        @pl.when(pl.program_id(0) > 0)
        def _():
            dw_ref[...] += dw
            loss_ref[...] += lp

    row = pl.BlockSpec((tm, D), lambda i: (i, 0))
    vec = pl.BlockSpec((1, D), lambda i: (0, 0))
    return pl.pallas_call(
        body, out_shape=(SDS((T, D), F32), SDS((1, D), F32), SDS((1, LANES), F32)), grid=(T // tm,),
        in_specs=[row, row, vec], out_specs=(row, vec, pl.BlockSpec((1, LANES), lambda i: (0, 0))),
        name="final_loss", compiler_params=_params(("arbitrary",)),
    )(h, target, w.reshape(1, D))


def _band_mask(i_blk, max_dist, first_ok):
    qi = lax.broadcasted_iota(jnp.int32, (BLOCK, 2 * BLOCK), 0)
    kj = lax.broadcasted_iota(jnp.int32, (BLOCK, 2 * BLOCK), 1)
    dist = qi + BLOCK - kj
    ok = (dist >= 0) & (dist <= max_dist)
    return ok & ((kj >= BLOCK) | first_ok)


def _pair(t, i):
    return t[:, LANES * i:LANES * (i + 1)]


def _low_half(shape):
    return lax.broadcasted_iota(jnp.int32, shape, len(shape) - 1) < HEAD_DIM


def _stack_heads(t):
    lo = _low_half(t.shape)
    z = jnp.zeros_like(t)
    return jnp.concatenate([jnp.where(lo, t, z), jnp.where(lo, z, t)], axis=0)


def _swap_halves(t):
    return jnp.concatenate([t[:, HEAD_DIM:], t[:, :HEAD_DIM]], axis=1)


def _kv_operand(kv, kv_swapped, h0, n_kv, n_heads):
    R = n_heads // n_kv
    if R == 1:
        return _pair(kv, h0 // 2)
    assert kv.shape[1] == LANES and R % 2 == 0, "grouped queries: one 128-lane tile of kv heads, both heads of a pair in one group"
    g = h0 // R
    t, ts = _pair(kv, g // 2), _pair(kv_swapped, g // 2)
    lo = _low_half(t.shape)
    return jnp.where(lo, t, ts) if g % 2 == 0 else jnp.where(lo, ts, t)


def _lane_place(cols):
    m = cols[0].shape[0]
    lane = lax.broadcasted_iota(jnp.int32, (m, LANES), 1)
    out = jnp.zeros((m, LANES), F32)
    for h, c in enumerate(cols):
        out = jnp.where(lane == h, c, out)
    return out


def _attn_specs(B, S, d, C, n_heads, n_kv, q_col, k_col, v_col):
    kvw = n_kv * HEAD_DIM
    qw = n_heads * HEAD_DIM
    cq, ck = (C // qw if d > 1 else 0), (C // kvw if d > 1 else 0)
    q_spec = pl.BlockSpec((1, BLOCK, qw), lambda b, r, i: (b, i, r * cq + q_col // qw))
    kc = pl.BlockSpec((1, BLOCK, kvw), lambda b, r, i: (b, i, r * ck + k_col // kvw))
    kp = pl.BlockSpec((1, BLOCK, kvw), lambda b, r, i: (b, jnp.maximum(i - 1, 0), r * ck + k_col // kvw))
    vc = pl.BlockSpec((1, BLOCK, kvw), lambda b, r, i: (b, i, r * ck + v_col // kvw))
    vp = pl.BlockSpec((1, BLOCK, kvw), lambda b, r, i: (b, jnp.maximum(i - 1, 0), r * ck + v_col // kvw))
    return q_spec, kp, kc, vp, vc


def _attn_fwd(qkv, B, S, d, *, n_heads, n_kv, q_col, k_col, v_col, max_dist, sinks, name, plan=None):
    C = qkv.shape[1]
    Ls = S // d
    nb = Ls // BLOCK
    qw = n_heads * HEAD_DIM
    R = n_heads // n_kv
    qkv3 = qkv.reshape(B, Ls, d * C)
    scale = HEAD_DIM ** -0.5

    def body(*refs):
        if sinks is not None:
            sink_ref, q_ref, kp_ref, kc_ref, vp_ref, vc_ref, o_ref, lse_ref = refs
        else:
            q_ref, kp_ref, kc_ref, vp_ref, vc_ref, o_ref, lse_ref = refs
        i = pl.program_id(2)
        mask1 = _band_mask(i, max_dist, i > 0)
        mask = jnp.concatenate([mask1, mask1], axis=0)
        q = q_ref[0]
        kk = jnp.concatenate([kp_ref[0], kc_ref[0]], axis=0)
        vv = jnp.concatenate([vp_ref[0], vc_ref[0]], axis=0)
        kks, vvs = (_swap_halves(kk), _swap_halves(vv)) if R > 1 else (None, None)
        lo = _low_half((BLOCK, LANES))
        top = lax.broadcasted_iota(jnp.int32, (2 * BLOCK, 1), 0) < BLOCK
        lses, tiles = [], []
        for t in range(n_heads // 2):
            k2 = _kv_operand(kk, kks, 2 * t, n_kv, n_heads)
            v2 = _kv_operand(vv, vvs, 2 * t, n_kv, n_heads)
            s = lax.dot_general(_stack_heads(_pair(q, t)), k2, NT, preferred_element_type=F32) * scale
            s = jnp.where(mask, s, NEG)
            m = jnp.max(s, axis=-1, keepdims=True)
            if sinks is not None:
                sk = jnp.where(top, sink_ref[2 * t], sink_ref[2 * t + 1])
                m = jnp.maximum(m, sk)
            p = jnp.exp(s - m)
            den = jnp.sum(p, axis=-1, keepdims=True)
            if sinks is not None:
                den = den + jnp.exp(sk - m)
            lse2 = m + jnp.log(den)
            o2 = jnp.dot((p / den).astype(BF16), v2, preferred_element_type=F32)
            tiles.append(jnp.where(lo, o2[:BLOCK], o2[BLOCK:]))
            lses += [lse2[:BLOCK], lse2[BLOCK:]]
        o_ref[0] = jnp.concatenate(tiles, axis=-1)
        lse_ref[0] = _lane_place(lses)

    specs = list(_attn_specs(B, S, d, C, n_heads, n_kv, q_col, k_col, v_col))
    args = [qkv3] * 5
    if sinks is not None:
        specs = [pl.BlockSpec(memory_space=pltpu.SMEM)] + specs
        args = [sinks] + args
    o3, lse3 = _hosted(plan, name, lambda comm: _pc(
        body, args, out_shape=(SDS((B, Ls, d * qw), F32), SDS((B, Ls, d * LANES), F32)), grid=(B, d, nb), in_specs=specs,
        out_specs=(pl.BlockSpec((1, BLOCK, qw), lambda b, r, i: (b, i, r)), pl.BlockSpec((1, BLOCK, LANES), lambda b, r, i: (b, i, r))),
        name=name, sem=("parallel", "parallel", "parallel"), comm=comm))
    return o3.reshape(B * S, qw), lse3.reshape(B * S, LANES)


def _attn_dq(qkv, do, lse, delta, cos, sin, B, S, d, *, n_heads, n_kv, q_col, k_col, v_col, max_dist, name, plan=None):
    C = qkv.shape[1]
    Ls = S // d
    nb = Ls // BLOCK
    qw = n_heads * HEAD_DIM
    R = n_heads // n_kv
    scale = HEAD_DIM ** -0.5

    def body(q_ref, kp_ref, kc_ref, vp_ref, vc_ref, do_ref, lse_ref, dl_ref, cos_ref, sin_ref, dq_ref):
        i = pl.program_id(2)
        mask1 = _band_mask(i, max_dist, i > 0)
        mask = jnp.concatenate([mask1, mask1], axis=0)
        q = q_ref[0]
        do_ = do_ref[0]
        kk = jnp.concatenate([kp_ref[0], kc_ref[0]], axis=0)
        vv = jnp.concatenate([vp_ref[0], vc_ref[0]], axis=0)
        kks, vvs = (_swap_halves(kk), _swap_halves(vv)) if R > 1 else (None, None)
        lo = _low_half((BLOCK, LANES))
        lse_t, dl_t = lse_ref[0], dl_ref[0]
        tiles = []
        for t in range(n_heads // 2):
            k2 = _kv_operand(kk, kks, 2 * t, n_kv, n_heads)
            v2 = _kv_operand(vv, vvs, 2 * t, n_kv, n_heads)
            lse2 = jnp.concatenate([lse_t[:, 2 * t:2 * t + 1], lse_t[:, 2 * t + 1:2 * t + 2]], axis=0)
            dl2 = jnp.concatenate([dl_t[:, 2 * t:2 * t + 1], dl_t[:, 2 * t + 1:2 * t + 2]], axis=0)
            s = lax.dot_general(_stack_heads(_pair(q, t)), k2, NT, preferred_element_type=F32) * scale
            p = jnp.where(mask, jnp.exp(s - lse2), 0.0)
            dp = lax.dot_general(_stack_heads(_pair(do_, t)), v2, NT, preferred_element_type=F32)
            ds = p * (dp - dl2)
            dq2 = jnp.dot(ds.astype(BF16), k2, preferred_element_type=F32) * scale
            tiles.append(jnp.where(lo, dq2[:BLOCK], dq2[BLOCK:]))
        dq = jnp.concatenate(tiles, axis=-1)
        dq_ref[0] = _rope(dq, cos_ref[0], sin_ref[0], -1.0).astype(BF16)

    qs, kp, kc, vp, vc = _attn_specs(B, S, d, C, n_heads, n_kv, q_col, k_col, v_col)
    row_q = pl.BlockSpec((1, BLOCK, qw), lambda b, r, i: (b, i, r))
    row_l = pl.BlockSpec((1, BLOCK, LANES), lambda b, r, i: (b, i, r))
    qkv3 = qkv.reshape(B, Ls, d * C)
    v3 = lambda t, w: t.reshape(B, Ls, d * w)
    args = (qkv3, qkv3, qkv3, qkv3, qkv3, v3(do, qw), v3(lse, LANES), v3(delta, LANES), v3(cos, LANES), v3(sin, LANES))
    dq3 = _hosted(plan, name, lambda comm: _pc(
        body, args, out_shape=SDS((B, Ls, d * qw), BF16), grid=(B, d, nb),
        in_specs=[qs, kp, kc, vp, vc, row_q, row_l, row_l, row_l, row_l], out_specs=row_q,
        name=name, sem=("parallel", "parallel", "parallel"), comm=comm))
    return dq3.reshape(B * S, qw)


def _attn_dkv(qkv, do, lse, delta, cos, sin, B, S, d, *, n_heads, n_kv, q_col, k_col, v_col, max_dist, name, plan=None):
    C = qkv.shape[1]
    Ls = S // d
    nb = Ls // BLOCK
    qw = n_heads * HEAD_DIM
    kvw = n_kv * HEAD_DIM
    R = n_heads // n_kv
    scale = HEAD_DIM ** -0.5
    cq, ck = (C // qw if d > 1 else 0), (C // kvw if d > 1 else 0)

    def body(k_ref, v_ref, q0_ref, q1_ref, do0_ref, do1_ref, lse0_ref, lse1_ref, dl0_ref, dl1_ref, cos_ref, sin_ref,
             dk_ref, dv_ref):
        j = pl.program_id(2)
        kj = lax.broadcasted_iota(jnp.int32, (BLOCK, BLOCK), 0)
        qi = lax.broadcasted_iota(jnp.int32, (BLOCK, BLOCK), 1)
        dist0 = qi - kj
        dist1 = qi + BLOCK - kj
        mask0 = (dist0 >= 0) & (dist0 <= max_dist)
        mask1 = (dist1 <= max_dist) & (j + 1 < nb)
        kb, vb = k_ref[0], v_ref[0]
        kbs, vbs = (_swap_halves(kb), _swap_halves(vb)) if R > 1 else (None, None)
        sides = ((q0_ref[0], do0_ref[0], lse0_ref[0].T, dl0_ref[0].T, mask0), (q1_ref[0], do1_ref[0], lse1_ref[0].T, dl1_ref[0].T, mask1))
        n_acc = n_kv if R > 1 else n_kv // 2
        dks = [jnp.zeros((BLOCK, LANES), F32) for _ in range(n_acc)]
        dvs = [jnp.zeros((BLOCK, LANES), F32) for _ in range(n_acc)]
        for t in range(n_heads // 2):
            k2 = _kv_operand(kb, kbs, 2 * t, n_kv, n_heads)
            v2 = _kv_operand(vb, vbs, 2 * t, n_kv, n_heads)
            a = (2 * t) // R if R > 1 else t
            for (q, do_, lse_r, dl_r, mask) in sides:
                q2, do2 = _stack_heads(_pair(q, t)), _stack_heads(_pair(do_, t))
                s = lax.dot_general(k2, q2, NT, preferred_element_type=F32) * scale
                dp = lax.dot_general(v2, do2, NT, preferred_element_type=F32)
                ps, dss = [], []
                for half in (0, 1):
                    h = 2 * t + half
                    sl = slice(BLOCK * half, BLOCK * (half + 1))
                    p = jnp.where(mask, jnp.exp(s[:, sl] - lse_r[h:h + 1, :]), 0.0)
                    ps.append(p)
                    dss.append(p * (dp[:, sl] - dl_r[h:h + 1, :]))
                dvs[a] = dvs[a] + jnp.dot(jnp.concatenate(ps, axis=1).astype(BF16), do2, preferred_element_type=F32)
                dks[a] = dks[a] + jnp.dot(jnp.concatenate(dss, axis=1).astype(BF16), q2, preferred_element_type=F32)
        if R > 1:
            lo = _low_half((BLOCK, LANES))
            fold = lambda x: x + pltpu.roll(x, HEAD_DIM, 1)
            dks = [jnp.where(lo, fold(dks[2 * t]), fold(dks[2 * t + 1])) for t in range(n_kv // 2)]
            dvs = [jnp.where(lo, fold(dvs[2 * t]), fold(dvs[2 * t + 1])) for t in range(n_kv // 2)]
        dk_t = jnp.concatenate(dks, axis=-1) * scale
        dk_ref[0] = _rope(dk_t, cos_ref[0], sin_ref[0], -1.0).astype(BF16)
        dv_ref[0] = jnp.concatenate(dvs, axis=-1).astype(BF16)

    nxt = lambda j: jnp.minimum(j + 1, nb - 1)
    k_spec = pl.BlockSpec((1, BLOCK, kvw), lambda b, r, j: (b, j, r * ck + k_col // kvw))
    v_spec = pl.BlockSpec((1, BLOCK, kvw), lambda b, r, j: (b, j, r * ck + v_col // kvw))
    q0 = pl.BlockSpec((1, BLOCK, qw), lambda b, r, j: (b, j, r * cq + q_col // qw))
    q1 = pl.BlockSpec((1, BLOCK, qw), lambda b, r, j: (b, nxt(j), r * cq + q_col // qw))
    w0 = lambda w: pl.BlockSpec((1, BLOCK, w), lambda b, r, j: (b, j, r))
    w1 = lambda w: pl.BlockSpec((1, BLOCK, w), lambda b, r, j: (b, nxt(j), r))
    qkv3 = qkv.reshape(B, Ls, d * C)
    v3 = lambda t, w: t.reshape(B, Ls, d * w)
    do3, lse3, dl3 = v3(do, qw), v3(lse, LANES), v3(delta, LANES)
    args = (qkv3, qkv3, qkv3, qkv3, do3, do3, lse3, lse3, dl3, dl3, v3(cos, LANES), v3(sin, LANES))
    dk3, dv3 = _hosted(plan, name, lambda comm: _pc(
        body, args, out_shape=(SDS((B, Ls, d * kvw), BF16), SDS((B, Ls, d * kvw), BF16)), grid=(B, d, nb),
        in_specs=[k_spec, v_spec, q0, q1, w0(qw), w1(qw), w0(LANES), w1(LANES), w0(LANES), w1(LANES), w0(LANES), w0(LANES)],
        out_specs=(w0(kvw), w0(kvw)), name=name, sem=("parallel", "parallel", "parallel"), comm=comm))
    return dk3.reshape(B * S, kvw), dv3.reshape(B * S, kvw)


def _head_expand():
    r = lax.broadcasted_iota(jnp.int32, (LANES, C_HEADS * HEAD_DIM), 0)
    c = lax.broadcasted_iota(jnp.int32, (LANES, C_HEADS * HEAD_DIM), 1)
    return jnp.where(c // HEAD_DIM == r, 1.0, 0.0).astype(F32)


def _delta(do, o, lse=None, sinks_row=None, name="delta"):
    T, W = do.shape
    tm = _pick(T, 512, 8)
    with_sink = sinks_row is not None

    def body(*refs):
        if with_sink:
            do_ref, o_ref, lse_ref, sk_ref, dl_ref, dob_ref, ds_ref = refs
        else:
            do_ref, o_ref, dl_ref, dob_ref = refs
        do_ = do_ref[...]
        dl = lax.dot_general(do_ * o_ref[...], _head_expand(), NT, preferred_element_type=F32, precision=HI)
        dl_ref[...] = dl
        dob_ref[...] = do_.astype(BF16)
        if with_sink:
            lane = lax.broadcasted_iota(jnp.int32, dl.shape, 1)
            contrib = jnp.where(lane < A_N_HEADS, -jnp.exp(sk_ref[...] - lse_ref[...]) * dl, 0.0)
            part = jnp.sum(contrib, axis=0, keepdims=True)

            @pl.when(pl.program_id(0) == 0)
            def _():
                ds_ref[...] = part

            @pl.when(pl.program_id(0) > 0)
            def _():
                ds_ref[...] += part

    row_w = pl.BlockSpec((tm, W), lambda i: (i, 0))
    row_l = pl.BlockSpec((tm, LANES), lambda i: (i, 0))
    vec_l = pl.BlockSpec((1, LANES), lambda i: (0, 0))
    if with_sink:
        return pl.pallas_call(
            body, out_shape=(SDS((T, LANES), F32), SDS((T, W), BF16), SDS((1, LANES), F32)), grid=(T // tm,),
            in_specs=[row_w, row_w, row_l, vec_l], out_specs=(row_l, row_w, vec_l), name=name,
            compiler_params=_params(("arbitrary",)),
        )(do, o, lse, sinks_row)
    return pl.pallas_call(
        body, out_shape=(SDS((T, LANES), F32), SDS((T, W), BF16)), grid=(T // tm,),
        in_specs=[row_w, row_w], out_specs=(row_l, row_w), name=name, compiler_params=_params(("parallel",)),
    )(do, o)


def _merge(os_, lses):
    T, W = os_[0].shape
    tm = _pick(T, 512, 8)

    def body(o0, o1, o2, l0, l1, l2, o_ref, lse_ref):
        ls = [l0[...], l1[...], l2[...]]
        m = jnp.maximum(jnp.maximum(ls[0], ls[1]), ls[2])
        ws = [jnp.exp(l - m) for l in ls]
        tot = ws[0] + ws[1] + ws[2]
        lse_ref[...] = m + jnp.log(tot)
        e = _head_expand()
        acc = jnp.zeros((tm, W), F32)
        for w, o in zip(ws, (o0, o1, o2)):
            acc = acc + jnp.dot(w / tot, e, preferred_element_type=F32, precision=HI) * o[...]
        o_ref[...] = acc

    row_w = pl.BlockSpec((tm, W), lambda i: (i, 0))
    row_l = pl.BlockSpec((tm, LANES), lambda i: (i, 0))
    return pl.pallas_call(
        body, out_shape=(SDS((T, W), F32), SDS((T, LANES), F32)), grid=(T // tm,),
        in_specs=[row_w] * 3 + [row_l] * 3, out_specs=(row_w, row_l), name="c_merge", compiler_params=_params(("parallel",)),
    )(*os_, *lses)


CONV_TC = 256


def _conv_pre(x, w, bias):
    row = lax.broadcasted_iota(jnp.int32, x.shape, 0)
    acc = x * w[SSM_CONV - 1:SSM_CONV, :] + bias
    for k in range(1, SSM_CONV):
        acc = acc + jnp.where(row >= k, pltpu.roll(x, k, 0), 0.0) * w[SSM_CONV - 1 - k:SSM_CONV - k, :]
    return acc


def _conv_fwd(zx3, w, bias):
    B, S, _ = zx3.shape
    off = SSM_D_INNER // CONV_TC

    def body(x_ref, w_ref, b_ref, o_ref):
        v = _conv_pre(x_ref[0], w_ref[...], b_ref[...])
        o_ref[0] = v * jax.nn.sigmoid(v)

    return pl.pallas_call(
        body, out_shape=SDS((B, S, SSM_CONV_DIM), F32), grid=(B, SSM_CONV_DIM // CONV_TC),
        in_specs=[pl.BlockSpec((1, S, CONV_TC), lambda b, j: (b, 0, j + off)),
                  pl.BlockSpec((SSM_CONV, CONV_TC), lambda b, j: (0, j)), pl.BlockSpec((1, CONV_TC), lambda b, j: (0, j))],
        out_specs=pl.BlockSpec((1, S, CONV_TC), lambda b, j: (b, 0, j)), name="b_conv_fwd",
        compiler_params=_params(("parallel", "parallel")),
    )(zx3, w, bias)


def _conv_bwd(zx3, dxc, w, bias, col0, name):
    B, S, n = dxc.shape
    tc = _pick(n, CONV_TC)
    off_x = (SSM_D_INNER + col0) // tc
    off_w = col0 // tc

    def body(x_ref, d_ref, w_ref, b_ref, dx_ref, dw_ref, db_ref):
        x = x_ref[0]
        wv = w_ref[...]
        v = _conv_pre(x, wv, b_ref[...])
        sg = jax.nn.sigmoid(v)
        dc = d_ref[0] * (sg * (1.0 + v * (1.0 - sg)))
        row = lax.broadcasted_iota(jnp.int32, x.shape, 0)
        dx = dc * wv[SSM_CONV - 1:SSM_CONV, :]
        dws = [jnp.sum(dc * x, axis=0, keepdims=True)]
        for k in range(1, SSM_CONV):
            dx = dx + jnp.where(row < S - k, pltpu.roll(dc, S - k, 0), 0.0) * wv[SSM_CONV - 1 - k:SSM_CONV - k, :]
            dws.append(jnp.sum(dc * jnp.where(row >= k, pltpu.roll(x, k, 0), 0.0), axis=0, keepdims=True))
        dx_ref[0] = dx.astype(BF16)
        ridx = lax.broadcasted_iota(jnp.int32, (SSM_CONV, tc), 0)
        dw = jnp.zeros((SSM_CONV, tc), F32)
        for k in range(SSM_CONV):
            dw = jnp.where(ridx == SSM_CONV - 1 - k, dws[k], dw)
        db = jnp.sum(dc, axis=0, keepdims=True)

        @pl.when(pl.program_id(1) == 0)
        def _():
            dw_ref[...] = dw
            db_ref[...] = db

        @pl.when(pl.program_id(1) > 0)
        def _():
            dw_ref[...] += dw
            db_ref[...] += db

    return pl.pallas_call(
        body, out_shape=(SDS((B, S, n), BF16), SDS((SSM_CONV, n), F32), SDS((1, n), F32)), grid=(n // tc, B),
        in_specs=[pl.BlockSpec((1, S, tc), lambda j, b: (b, 0, j + off_x)), pl.BlockSpec((1, S, tc), lambda j, b: (b, 0, j)),
                  pl.BlockSpec((SSM_CONV, tc), lambda j, b: (0, j + off_w)), pl.BlockSpec((1, tc), lambda j, b: (0, j + off_w))],
        out_specs=(pl.BlockSpec((1, S, tc), lambda j, b: (b, 0, j)), pl.BlockSpec((SSM_CONV, tc), lambda j, b: (0, j)),
                   pl.BlockSpec((1, tc), lambda j, b: (0, j))),
        name=name, compiler_params=_params(("parallel", "arbitrary")),
    )(zx3, dxc, w, bias)


def _ssd_common(x, Bm, Cm, dtc_raw, dtr_raw, pr, pc):
    Q = SSM_CHUNK
    zc = dtc_raw + pr[0:1, :]
    dt_c = jax.nn.softplus(zc)
    dt_r = jax.nn.softplus(dtr_raw + pc[:, 0:1])
    A_r = -jnp.exp(pr[1:2, :])
    A_c = -jnp.exp(pc[:, 1:2])
    row = lax.broadcasted_iota(jnp.int32, (Q, Q), 0)
    col = lax.broadcasted_iota(jnp.int32, (Q, Q), 1)
    tril = jnp.where(row >= col, 1.0, 0.0).astype(F32)
    cs_c = jnp.dot(tril, dt_c * A_r, preferred_element_type=F32, precision=HI)
    cs_r = lax.dot_general(dt_r * A_c, tril, NT, preferred_element_type=F32, precision=HI)
    return zc, dt_c, A_r, cs_c, cs_r, row, col, tril


def _ssd_fwd(xc3, dtc, dtr, prow, pcol, plan=None):
    B, S, _ = xc3.shape
    Q, G, HG, P, N = SSM_CHUNK, SSM_N_GROUPS, SSM_HG, HEAD_DIM, SSM_D_STATE
    nc = S // Q
    xw = HG * P

    def body(x_ref, b_ref, c_ref, dtc_ref, dtr_ref, pr_ref, pc_ref, y_ref, st_ref, state):
        c = pl.program_id(1)

        @pl.when(c == 0)
        def _():
            state[...] = jnp.zeros_like(state)

        pr = pr_ref[0]
        for bb in range(B):
            x, Bm, Cm = x_ref[bb], b_ref[bb], c_ref[bb]
            _, dt_c, _, cs_c, cs_r, row, col, _ = _ssd_common(x, Bm, Cm, dtc_ref[bb, 0], dtr_ref[bb, 0], pr, pc_ref[0])
            Bb, Cb = Bm.astype(BF16), Cm.astype(BF16)
            CB = lax.dot_general(Cb, Bb, NT, preferred_element_type=F32)
            ys = []
            for hg in range(HG):
                xh = x[:, P * hg:P * (hg + 1)]
                xt = xh * dt_c[:, hg:hg + 1]
                csc, csr = cs_c[:, hg:hg + 1], cs_r[hg:hg + 1, :]
                L = jnp.where(row >= col, jnp.exp(jnp.minimum(csc - csr, 0.0)), 0.0)
                ydiag = jnp.dot((CB * L).astype(BF16), xt.astype(BF16), preferred_element_type=F32)
                Sh = state[bb, hg]
                yoff = lax.dot_general(Cb, Sh.astype(BF16), NT, preferred_element_type=F32) * jnp.exp(csc)
                ys.append(ydiag + yoff + pr[2:3, hg:hg + 1] * xh)
                st_ref[bb, 0, 0, P * hg:P * (hg + 1), :] = Sh
                csq = csc[Q - 1:Q, :]
                upd = lax.dot_general((xt * jnp.exp(csq - csc)).astype(BF16), Bb, TN, preferred_element_type=F32)
                state[bb, hg] = Sh * jnp.exp(csq) + upd
            y_ref[bb] = jnp.concatenate([jnp.concatenate(ys[0:2], axis=-1), jnp.concatenate(ys[2:4], axis=-1)], axis=-1)

    bo, co = SSM_D_INNER // N, (SSM_D_INNER + SSM_BC_DIM) // N
    return _hosted(plan, "b_ssd_fwd", lambda comm: _pc(
        body, (xc3, xc3, xc3, dtc, dtr, prow, pcol),
        out_shape=(SDS((B, S, SSM_D_INNER), F32), SDS((B, G, nc, xw, N), F32)), grid=(G, nc),
        in_specs=[pl.BlockSpec((B, Q, xw), lambda g, c: (0, c, g)), pl.BlockSpec((B, Q, N), lambda g, c: (0, c, bo + g)),
                  pl.BlockSpec((B, Q, N), lambda g, c: (0, c, co + g)), pl.BlockSpec((B, 1, Q, HG), lambda g, c: (0, g, c, 0)),
                  pl.BlockSpec((B, 1, HG, Q), lambda g, c: (0, g, 0, c)), pl.BlockSpec((1, 3, HG), lambda g, c: (g, 0, 0)),
                  pl.BlockSpec((1, HG, 3), lambda g, c: (g, 0, 0))],
        out_specs=(pl.BlockSpec((B, Q, xw), lambda g, c: (0, c, g)), pl.BlockSpec((B, 1, 1, xw, N), lambda g, c: (0, g, c, 0, 0))),
        scratch_shapes=[pltpu.VMEM((B, HG, P, N), F32)], name="b_ssd_fwd", sem=("parallel", "arbitrary"), comm=comm))


def _ssd_bwd(xc3, dtc, dtr, prow, pcol, states, dy3, plan=None):
    B, S, _ = xc3.shape
    Q, G, HG, P, N = SSM_CHUNK, SSM_N_GROUPS, SSM_HG, HEAD_DIM, SSM_D_STATE
    nc = S // Q
    xw = HG * P

    def body(x_ref, b_ref, c_ref, dtc_ref, dtr_ref, pr_ref, pc_ref, st_ref, dy_ref,
             dx_ref, db_ref, dc_ref, ddt_ref, dpar_ref, dstate):
        ci = pl.program_id(1)

        @pl.when(ci == 0)
        def _():
            dstate[...] = jnp.zeros_like(dstate)

        pr = pr_ref[0]
        dpar = one_sequence(0, pr, x_ref, b_ref, c_ref, dtc_ref, dtr_ref, pc_ref, st_ref, dy_ref, dx_ref, db_ref, dc_ref,
                            ddt_ref, dstate)
        for bb in range(1, B):
            dpar = dpar + one_sequence(bb, pr, x_ref, b_ref, c_ref, dtc_ref, dtr_ref, pc_ref, st_ref, dy_ref, dx_ref, db_ref,
                                       dc_ref, ddt_ref, dstate)
        first = ci == 0

        @pl.when(first)
        def _():
            dpar_ref[0] = dpar

        @pl.when(jnp.logical_not(first))
        def _():
            dpar_ref[0] += dpar

    def one_sequence(bb, pr, x_ref, b_ref, c_ref, dtc_ref, dtr_ref, pc_ref, st_ref, dy_ref, dx_ref, db_ref, dc_ref, ddt_ref,
                     dstate):
        x, Bm, Cm, dy = x_ref[bb], b_ref[bb], c_ref[bb], dy_ref[bb]
        zc, dt_c, A_r, cs_c, cs_r, row, col, tril = _ssd_common(x, Bm, Cm, dtc_ref[bb, 0], dtr_ref[bb, 0], pr, pc_ref[0])
        Bb, Cb = Bm.astype(BF16), Cm.astype(BF16)
        CB = lax.dot_general(Cb, Bb, NT, preferred_element_type=F32)
        CBt = lax.dot_general(Bb, Cb, NT, preferred_element_type=F32)
        lane4 = lax.broadcasted_iota(jnp.int32, (Q, HG), 1)
        lane4r = lax.broadcasted_iota(jnp.int32, (1, HG), 1)
        rowq = lax.broadcasted_iota(jnp.int32, (Q, 1), 0)
        dB = jnp.zeros((Q, N), F32)
        dC = jnp.zeros((Q, N), F32)
        dcs4 = jnp.zeros((Q, HG), F32)
        dtx4 = jnp.zeros((Q, HG), F32)
        dD4 = jnp.zeros((1, HG), F32)
        dxts, xhs, dyhs = [], [], []
        for hg in range(HG):
            xh = x[:, P * hg:P * (hg + 1)]
            dyh = dy[:, P * hg:P * (hg + 1)]
            xt = xh * dt_c[:, hg:hg + 1]
            xtb, dyb = xt.astype(BF16), dyh.astype(BF16)
            csc, csr = cs_c[:, hg:hg + 1], cs_r[hg:hg + 1, :]
            L = jnp.where(row >= col, jnp.exp(jnp.minimum(csc - csr, 0.0)), 0.0)
            Lt = jnp.where(col >= row, jnp.exp(jnp.minimum(csr - csc, 0.0)), 0.0)
            M, Mt = CB * L, CBt * Lt
            Sh = st_ref[bb, 0, 0, P * hg:P * (hg + 1), :]
            dSh = dstate[bb, hg]
            Shb, dShb = Sh.astype(BF16), dSh.astype(BF16)
            ecs = jnp.exp(csc)
            csq = csc[Q - 1:Q, :]
            dec = jnp.exp(csq - csc)
            dxt = jnp.dot(Mt.astype(BF16), dyb, preferred_element_type=F32)
            dxt = dxt + lax.dot_general(Bb, dShb, NT, preferred_element_type=F32) * dec
            Gm = lax.dot_general(dyb, xtb, NT, preferred_element_type=F32)
            Gt = lax.dot_general(xtb, dyb, NT, preferred_element_type=F32)
            dC = dC + jnp.dot((Gm * L).astype(BF16), Bb, preferred_element_type=F32)
            dB = dB + jnp.dot((Gt * Lt).astype(BF16), Cb, preferred_element_type=F32)
            dC = dC + jnp.dot(dyb, Shb, preferred_element_type=F32) * ecs
            dBst = jnp.dot(xtb, dShb, preferred_element_type=F32) * dec
            dB = dB + dBst
            dcs = jnp.sum(Gm * M, axis=1, keepdims=True) - jnp.sum(Gt * Mt, axis=1, keepdims=True)
            yoff = lax.dot_general(Cb, Shb, NT, preferred_element_type=F32) * ecs
            dcs = dcs + jnp.sum(yoff * dyh, axis=1, keepdims=True)
            r = jnp.sum(dBst * Bm, axis=1, keepdims=True)
            dcs = dcs - r
            extra = jnp.sum(r, axis=0, keepdims=True) + jnp.exp(csq) * jnp.sum(
                jnp.sum(dSh * Sh, axis=1, keepdims=True), axis=0, keepdims=True)
            dcs = dcs + jnp.where(rowq == Q - 1, extra, 0.0)
            dcs4 = jnp.where(lane4 == hg, dcs, dcs4)
            dtx4 = jnp.where(lane4 == hg, jnp.sum(dxt * xh, axis=1, keepdims=True), dtx4)
            dD4 = jnp.where(lane4r == hg, jnp.sum(jnp.sum(dyh * xh, axis=1, keepdims=True), axis=0, keepdims=True), dD4)
            dstate[bb, hg] = dSh * jnp.exp(csq) + lax.dot_general((dyh * ecs).astype(BF16), Cb, TN, preferred_element_type=F32)
            dxts.append(dxt)
            xhs.append(xh)
            dyhs.append(dyh)
        da4 = lax.dot_general(tril, dcs4, TN, preferred_element_type=F32, precision=HI)
        ddt4 = da4 * A_r + dtx4
        ddtraw = ddt4 * jax.nn.sigmoid(zc)
        ddt_ref[bb, 0] = ddtraw
        dxs = [dxts[hg] * dt_c[:, hg:hg + 1] + pr[2:3, hg:hg + 1] * dyhs[hg] for hg in range(HG)]
        dx_ref[bb] = jnp.concatenate([jnp.concatenate(dxs[0:2], axis=-1), jnp.concatenate(dxs[2:4], axis=-1)], axis=-1)
        db_ref[bb] = dB
        dc_ref[bb] = dC
        d_bias = jnp.sum(ddtraw, axis=0, keepdims=True)
        d_alog = jnp.sum(da4 * dt_c, axis=0, keepdims=True) * A_r
        r3 = lax.broadcasted_iota(jnp.int32, (3, HG), 0)
        return jnp.where(r3 == 0, d_bias, jnp.where(r3 == 1, d_alog, dD4))

    rc = lambda c: nc - 1 - c
    bo, co = SSM_D_INNER // N, (SSM_D_INNER + SSM_BC_DIM) // N
    return _hosted(plan, "b_ssd_bwd", lambda comm: _pc(
        body, (xc3, xc3, xc3, dtc, dtr, prow, pcol, states, dy3),
        out_shape=(SDS((B, S, SSM_D_INNER), F32), SDS((B, S, SSM_BC_DIM), F32), SDS((B, S, SSM_BC_DIM), F32),
                   SDS((B, G, S, HG), F32), SDS((G, 3, HG), F32)),
        grid=(G, nc),
        in_specs=[pl.BlockSpec((B, Q, xw), lambda g, c: (0, rc(c), g)), pl.BlockSpec((B, Q, N), lambda g, c: (0, rc(c), bo + g)),
                  pl.BlockSpec((B, Q, N), lambda g, c: (0, rc(c), co + g)), pl.BlockSpec((B, 1, Q, HG), lambda g, c: (0, g, rc(c), 0)),
                  pl.BlockSpec((B, 1, HG, Q), lambda g, c: (0, g, 0, rc(c))), pl.BlockSpec((1, 3, HG), lambda g, c: (g, 0, 0)),
                  pl.BlockSpec((1, HG, 3), lambda g, c: (g, 0, 0)),
                  pl.BlockSpec((B, 1, 1, xw, N), lambda g, c: (0, g, rc(c), 0, 0)), pl.BlockSpec((B, Q, xw), lambda g, c: (0, rc(c), g))],
        out_specs=(pl.BlockSpec((B, Q, xw), lambda g, c: (0, rc(c), g)), pl.BlockSpec((B, Q, N), lambda g, c: (0, rc(c), g)),
                   pl.BlockSpec((B, Q, N), lambda g, c: (0, rc(c), g)), pl.BlockSpec((B, 1, Q, HG), lambda g, c: (0, g, rc(c), 0)),
                   pl.BlockSpec((1, 3, HG), lambda g, c: (g, 0, 0))),
        scratch_shapes=[pltpu.VMEM((B, HG, P, N), F32)], name="b_ssd_bwd", sem=("parallel", "arbitrary"), comm=comm))


GN_W = SSM_D_INNER // SSM_N_GROUPS


def _gate_fwd(y, zx, nw):
    T = y.shape[0]
    tm = _pick(T, 256, 8)

    def body(y_ref, z_ref, w_ref, o_ref):
        z = z_ref[...]
        gt = y_ref[...] * (z * jax.nn.sigmoid(z))
        outs = []
        for k in range(SSM_N_GROUPS):
            gk = gt[:, GN_W * k:GN_W * (k + 1)]
            outs.append(gk * lax.rsqrt(jnp.mean(gk * gk, axis=-1, keepdims=True) + NORM_EPS))
        o_ref[...] = (jnp.concatenate(outs, axis=-1) * w_ref[...]).astype(BF16)

    row = pl.BlockSpec((tm, SSM_D_INNER), lambda i: (i, 0))
    return pl.pallas_call(
        body, out_shape=SDS((T, SSM_D_INNER), BF16), grid=(T // tm,),
        in_specs=[row, row, pl.BlockSpec((1, SSM_D_INNER), lambda i: (0, 0))], out_specs=row, name="b_gate_fwd",
        compiler_params=_params(("parallel",)),
    )(y, zx, nw)


def _gate_bwd(dgn, y, zx, nw):
    T = y.shape[0]
    tm = _pick(T, 256, 8)

    def body(d_ref, y_ref, z_ref, w_ref, dy_ref, dz_ref, dw_ref):
        z, yv, w = z_ref[...], y_ref[...], w_ref[...]
        sg = jax.nn.sigmoid(z)
        sz = z * sg
        gt = yv * sz
        gw = d_ref[...] * w
        dgts, dws = [], []
        for k in range(SSM_N_GROUPS):
            sl = slice(GN_W * k, GN_W * (k + 1))
            gk, gwk = gt[:, sl], gw[:, sl]
            rstd = lax.rsqrt(jnp.mean(gk * gk, axis=-1, keepdims=True) + NORM_EPS)
            dgts.append(rstd * gwk - gk * (rstd * rstd * rstd) * jnp.mean(gwk * gk, axis=-1, keepdims=True))
            dws.append(jnp.sum(d_ref[:, sl] * gk * rstd, axis=0, keepdims=True))
        dgt = jnp.concatenate(dgts, axis=-1)
        dy_ref[...] = dgt * sz
        dz_ref[...] = (dgt * yv * (sg * (1.0 + z * (1.0 - sg)))).astype(BF16)
        dw = jnp.concatenate(dws, axis=-1)

        @pl.when(pl.program_id(0) == 0)
        def _():
            dw_ref[...] = dw

        @pl.when(pl.program_id(0) > 0)
        def _():
            dw_ref[...] += dw

    row = pl.BlockSpec((tm, SSM_D_INNER), lambda i: (i, 0))
    vec = pl.BlockSpec((1, SSM_D_INNER), lambda i: (0, 0))
    return pl.pallas_call(
        body, out_shape=(SDS((T, SSM_D_INNER), F32), SDS((T, SSM_D_INNER), BF16), SDS((1, SSM_D_INNER), F32)), grid=(T // tm,),
        in_specs=[row, row, row, vec], out_specs=(row, row, vec), name="b_gate_bwd", compiler_params=_params(("arbitrary",)),
    )(dgn, y, zx, nw)


N_CHIPS = 4


def _dev_block(ref, kind, j, size):
    if kind == "slot":
        return ref.at[j]
    start = pl.multiple_of(j * size, size)
    nd = len(ref.shape)
    if kind == "col":
        return ref.at[(slice(None),) * (nd - 1) + (pl.ds(start, size),)]
    return ref.at[(slice(None),) * (nd - 2) + (pl.ds(start, size), slice(None))]


def _dma_sems(n, k):
    return [pltpu.SemaphoreType.DMA((n, k)), pltpu.SemaphoreType.DMA((n, k)), pltpu.SemaphoreType.DMA((n, k))]


def _place(shard, layer, kind, full_shape, dev, name):
    k, n = shard.shape[1:]
    tr = _pick(k, 512, 16)
    nb = k // tr

    def body(dev_ref, s_ref, o_ref):
        if kind == "slot":
            o_ref[0] = s_ref[0].astype(BF16)
        else:
            o_ref[...] = s_ref[0].astype(BF16)

    out_spec = {"slot": pl.BlockSpec((1, tr, n), lambda i, d: (d[0], i, 0)),
                "row": pl.BlockSpec((tr, n), lambda i, d: (d[0] * nb + i, 0)),
                "col": pl.BlockSpec((tr, n), lambda i, d: (i, d[0]))}[kind]
    return pl.pallas_call(
        body, out_shape=SDS(full_shape, BF16),
        grid_spec=pltpu.PrefetchScalarGridSpec(
            num_scalar_prefetch=1, grid=(nb,), in_specs=[pl.BlockSpec((1, tr, n), lambda i, d: (layer, i, 0))], out_specs=out_spec),
        name=name, compiler_params=_params(("arbitrary",)),
    )(dev, shard)


def _run_comm(comm, name):
    c_in = len(comm.inputs)

    def body(*refs):
        cins, couts, sems = refs[:c_in], refs[c_in:c_in + len(comm.out_shapes)], refs[c_in + len(comm.out_shapes):]
        for _, fn in comm.phases:
            fn(cins, couts, sems)

    return pl.pallas_call(
        body, out_shape=list(comm.out_shapes), in_specs=[ANY] * c_in, out_specs=[ANY] * len(comm.out_shapes),
        input_output_aliases=dict(comm.aliases), scratch_shapes=list(comm.sems), name=name,
    )(*comm.inputs)


def _gather_comm(items, mid=0.7):
    n = len(items)

    def tools(srcs, dsts, sems):
        send_sems, recv_sems, local_sems = sems
        px, py, pc = lax.axis_index("x"), lax.axis_index("y"), lax.axis_index("c")
        me, sibling = (px, py, pc), (px, py, 1 - pc)
        chips = [(1 - px, py), (px, 1 - py), (1 - px, 1 - py)]

        def blk(a, dev):
            return _dev_block(dsts[a], items[a][1], 4 * dev[0] + 2 * dev[1] + dev[2], items[a][2])

        def copy(a, k, block, to, src=None):
            return pltpu.make_async_remote_copy(
                src_ref=blk(a, block) if src is None else src, dst_ref=blk(a, block),
                send_sem=send_sems.at[a, k], recv_sem=recv_sems.at[a, k], device_id=to, device_id_type=MESH)

        def mine():
            return [pltpu.make_async_copy(srcs[a], blk(a, me), local_sems.at[a, 0]) for a in range(n) if not items[a][4]]

        def first():
            out = []
            for a in range(n):
                src = blk(a, me) if items[a][4] else srcs[a]
                out.append(copy(a, 0, me, sibling, src=src))
                out += [copy(a, 1 + j, me, (*chip, pc), src=src) for j, chip in enumerate(chips)]
            return out

        def passed():
            return [copy(a, 4 + j, (*chip, pc), sibling) for j, chip in enumerate(chips) for a in range(n)]

        return me, sibling, chips, pc, copy, mine, first, passed

    def start(srcs, dsts, sems):
        *_, mine, first, _ = tools(srcs, dsts, sems)
        for cp in mine() + first():
            cp.start()

    def forward(srcs, dsts, sems):
        me, _, chips, pc, copy, _, _, passed = tools(srcs, dsts, sems)
        fwd = passed()
        for j, chip in enumerate(chips):
            for a in range(n):
                copy(a, 1 + j, (*chip, pc), me).wait_recv()
                fwd[j * n + a].start()

    def finish(srcs, dsts, sems):
        me, sibling, chips, pc, copy, mine, first, passed = tools(srcs, dsts, sems)
        for a in range(n):
            copy(a, 0, sibling, me).wait_recv()
            for j, chip in enumerate(chips):
                copy(a, 4 + j, (*chip, 1 - pc), me).wait_recv()
        for cp in first() + passed():
            cp.wait_send()
        for cp in mine():
            cp.wait()

    return _Comm([it[0] for it in items], [SDS(it[3], it[0].dtype) for it in items],
                 {a: a for a in range(n) if items[a][4]}, _dma_sems(n, 7), [(0.0, start), (mid, forward), (1.0, finish)])


def _gather(items, name):
    return _run_comm(_gather_comm(items), name)


def _reduce_d2d_comm(items):
    n = len(items)

    def copies(gs, gots, sems):
        send_sems, recv_sems, _ = sems
        px, py, pc = lax.axis_index("x"), lax.axis_index("y"), lax.axis_index("c")
        out = []
        for a in range(n):
            _, kind, size, _ = items[a]
            for q in range(N_CHIPS):
                out.append(pltpu.make_async_remote_copy(
                    src_ref=_dev_block(gs[a], kind, 2 * q + 1 - pc, size), dst_ref=gots[a].at[q], send_sem=send_sems.at[a, q],
                    recv_sem=recv_sems.at[a, q], device_id=(px, py, 1 - pc), device_id_type=MESH))
        return out

    def start(gs, gots, sems):
        for cp in copies(gs, gots, sems):
            cp.start()

    def finish(gs, gots, sems):
        for cp in copies(gs, gots, sems):
            cp.wait()

    return _Comm([it[0] for it in items], [SDS((N_CHIPS,) + tuple(it[3]), F32) for it in items], {},
                 _dma_sems(n, N_CHIPS), [(0.0, start), (1.0, finish)])


def _pair_sum(g, got, kind, core, name):
    _, k, n = got.shape
    tr = _pick(k, max(16, STREAM_VMEM // (2 * n * 10)), 16)
    nb = k // tr

    def body(c_ref, g_ref, s_ref, o_ref):
        mine = g_ref[0] if kind == "slot" else g_ref[...]
        o_ref[0] = (mine + s_ref[0]).astype(BF16)

    g_spec = {"slot": pl.BlockSpec((1, tr, n), lambda q, i, c: (2 * q + c[0], i, 0)),
              "row": pl.BlockSpec((tr, n), lambda q, i, c: ((2 * q + c[0]) * nb + i, 0)),
              "col": pl.BlockSpec((tr, n), lambda q, i, c: (i, 2 * q + c[0]))}[kind]
    part = pl.BlockSpec((1, tr, n), lambda q, i, c: (q, i, 0))
    return pl.pallas_call(
        body, out_shape=SDS((N_CHIPS, k, n), BF16),
        grid_spec=pltpu.PrefetchScalarGridSpec(num_scalar_prefetch=1, grid=(N_CHIPS, nb), in_specs=[g_spec, part], out_specs=part),
        name=name, compiler_params=_params(("arbitrary", "arbitrary")),
    )(core, g, got)


def _reduce_ici_comm(parts):
    n = len(parts)

    def copies(ps, rs, sems, arriving):
        send_sems, recv_sems, _ = sems
        px, py, pc = lax.axis_index("x"), lax.axis_index("y"), lax.axis_index("c")
        my_chip = 2 * px + py
        out = []
        for a in range(n):
            for k in range(1, N_CHIPS):
                qx, qy = px ^ (k >> 1), py ^ (k & 1)
                q = 2 * qx + qy
                out.append(pltpu.make_async_remote_copy(
                    src_ref=ps[a].at[q], dst_ref=rs[a].at[q] if arriving else rs[a].at[my_chip], send_sem=send_sems.at[a, k - 1],
                    recv_sem=recv_sems.at[a, k - 1], device_id=(qx, qy, pc), device_id_type=MESH))
        return out

    def start(ps, rs, sems):
        for cp in copies(ps, rs, sems, False):
            cp.start()

    def finish(ps, rs, sems):
        for cp in copies(ps, rs, sems, True):
            cp.wait_recv()
        for cp in copies(ps, rs, sems, False):
            cp.wait_send()

    return _Comm(list(parts), [SDS(p.shape, p.dtype) for p in parts], {}, _dma_sems(n, N_CHIPS - 1),
                 [(0.0, start), (1.0, finish)])


def _adam_update(g, w, m, v):
    c1 = 1.0 - ADAM_B1 ** ADAM_STEP
    c2 = 1.0 - ADAM_B2 ** ADAM_STEP
    nm = ADAM_B1 * m + (1.0 - ADAM_B1) * g
    nv = ADAM_B2 * v + (1.0 - ADAM_B2) * (g * g)
    delta = -ADAM_LR * ((nm / c1) / (jnp.sqrt(nv / c2) + ADAM_EPS) + ADAM_WD * w)
    return delta, nm, nv


def _adamw(parts, recv, w, m, v, layer, prev, chip, name):
    _, R, C = w.shape
    row_bytes = 2 * C * (N_CHIPS * 2 + 7 * 4)
    tr = _pick(R, max(16, STREAM_VMEM // row_bytes), 16)
    n_prev = 0 if prev is None else 4

    def body(ch_ref, own_ref, r1_ref, r2_ref, r3_ref, w_ref, m_ref, v_ref, *rest):
        g_ref, d_ref, nm_ref, nv_ref = rest[n_prev:]
        g = own_ref[0].astype(F32)
        for r_ref in (r1_ref, r2_ref, r3_ref):
            g = g + r_ref[0].astype(F32)
        g_ref[0] = g
        d_ref[0], nm_ref[0], nv_ref[0] = _adam_update(g, w_ref[0], m_ref[0], v_ref[0])

    lay = pl.BlockSpec((1, tr, C), lambda i, ch: (layer, i, 0))
    other = lambda k: pl.BlockSpec((1, tr, C), lambda i, ch: (ch[0] ^ k, i, 0))
    out = SDS(w.shape, F32)
    return pl.pallas_call(
        body, out_shape=(out, out, out, out),
        grid_spec=pltpu.PrefetchScalarGridSpec(
            num_scalar_prefetch=1, grid=(R // tr,),
            in_specs=[pl.BlockSpec((1, tr, C), lambda i, ch: (ch[0], i, 0)), other(2), other(1), other(3), lay, lay, lay]
            + [ANY] * n_prev,
            out_specs=(lay, lay, lay, lay)),
        input_output_aliases={8 + k: k for k in range(n_prev)},
        name=name, compiler_params=_params(("arbitrary",)),
    )(chip, parts, recv, recv, recv, w, m, v, *(prev or ()))


def _small_adamw(gathered, ws, ms, vs):
    n = len(ws)

    def body(*refs):
        g_in, w_in, m_in, v_in = refs[:n], refs[n:2 * n], refs[2 * n:3 * n], refs[3 * n:4 * n]
        outs = refs[4 * n:]
        for i in range(n):
            g = g_in[i][0]
            for dev in range(1, N_DEV):
                g = g + g_in[i][dev]
            d, nm, nv = _adam_update(g, w_in[i][...], m_in[i][...], v_in[i][...])
            outs[i][...] = g
            outs[n + i][...] = d
            outs[2 * n + i][...] = nm
            outs[3 * n + i][...] = nv

    shapes = [SDS(w.shape, F32) for w in ws]
    outs = pl.pallas_call(body, out_shape=shapes * 4, name="small_adamw")(*gathered, *ws, *ms, *vs)
    return outs[:n], outs[n:2 * n], outs[2 * n:3 * n], outs[3 * n:]


W_NAMES = ("norm_mix_w", "norm_mlp_w", "a_w_qkv", "a_b_qkv", "a_sinks", "a_w_o", "a_b_o", "b_in_w", "b_conv_w", "b_conv_b",
           "b_dt_bias", "b_a_log", "b_d", "b_norm_w", "b_out_w", "c_w_qkv", "c_w_o", "mlp_w_up", "mlp_w_down", "final_norm_w")
BIG_KIND = {"a_w_qkv": "slot", "a_w_o": "row", "b_in_w": "slot", "b_out_w": "row", "c_w_qkv": "col", "c_w_o": "row",
            "mlp_w_up": "col", "mlp_w_down": "row"}
SMALL_SHARDED = {"a_b_qkv": 1, "a_b_o": 1, "b_conv_w": 2}
SMALL_REPLICATED = ("norm_mix_w", "norm_mlp_w", "a_sinks", "b_conv_b", "b_dt_bias", "b_a_log", "b_d", "b_norm_w", "final_norm_w")


def _layer_big(i):
    kind, j = i % 3, i // 3
    mix = {0: [("a_w_qkv", j), ("a_w_o", j)], 1: [("b_in_w", 0), ("b_out_w", 0)], 2: [("c_w_qkv", 0), ("c_w_o", 0)]}[kind]
    return mix + [("mlp_w_up", i), ("mlp_w_down", i)]


def _block_size(kind, shard2d):
    return {"slot": None, "row": shard2d[0], "col": shard2d[1]}[kind]


def _full2d(kind, shard2d):
    k, n = shard2d
    return {"slot": (N_DEV, k, n), "row": (N_DEV * k, n), "col": (k, N_DEV * n)}[kind]


def _from_slots(t, ax):
    s = t.shape[1:]
    return jnp.moveaxis(t, 0, ax).reshape(s[:ax] + (N_DEV * s[ax],) + s[ax + 1:])


def _to_slots(g, ax):
    s = g.shape
    return jnp.moveaxis(g.reshape(s[:ax] + (N_DEV, s[ax] // N_DEV) + s[ax + 1:]), ax, 0)


def _rope_tables(positions):
    half = HEAD_DIM // 2
    inv = ROPE_THETA ** (-(jnp.arange(LANES, dtype=jnp.int32) % half).astype(F32) / half)
    ang = positions.astype(F32).reshape(-1, 1) * inv
    return jnp.cos(ang), jnp.sin(ang)


def _swa_fwd(u, h, p, j, B, S, cos, sin, tag, nw, plan=None):
    qkv = _matmul(u, p["a_w_qkv"][j], out_dtype=BF16, bias=p["a_b_qkv"][j][None], rope=(cos, sin),
                  rope_cols=A_Q_DIM + A_KV_DIM, name=f"{tag}_qkv", plan=plan)
    o, lse = _attn_fwd(qkv, B, S, 1, n_heads=A_N_HEADS, n_kv=A_N_KV, q_col=0, k_col=A_Q_DIM, v_col=A_Q_DIM + A_KV_DIM,
                       max_dist=A_WINDOW - 1, sinks=p["a_sinks"][j], name=f"{tag}_attn", plan=plan)
    h1, u2 = _matmul(o, p["a_w_o"][j], bias=p["a_b_o"][j][None], resid=h, norm_out=nw, name=f"{tag}_o")
    return h1, u2, (qkv, o, lse)


def _swa_bwd(dh1, u, saved, p, j, B, S, cos, sin, tag, norm, plan=None):
    qkv, o, lse = saved
    kw = dict(n_heads=A_N_HEADS, n_kv=A_N_KV, q_col=0, k_col=A_Q_DIM, v_col=A_Q_DIM + A_KV_DIM, max_dist=A_WINDOW - 1)
    g = {}
    do = _matmul(dh1, p["a_w_o"][j], tb=True, name=f"{tag}_do")
    g["a_w_o"] = _matmul(o, dh1, ta=True, name=f"{tag}_dwo")
    g["a_b_o"] = _colsum(dh1, f"{tag}_dbo")[0]
    sk = jnp.pad(p["a_sinks"][j], (0, LANES - A_N_HEADS))[None]
    delta, dob, dsink = _delta(do, o, lse, sk, name=f"{tag}_delta")
    g["a_sinks"] = dsink[0, :A_N_HEADS]
    dq = _attn_dq(qkv, dob, lse, delta, cos, sin, B, S, 1, name=f"{tag}_dq", plan=plan, **kw)
    dk, dv = _attn_dkv(qkv, dob, lse, delta, cos, sin, B, S, 1, name=f"{tag}_dkv", plan=plan, **kw)
    dqkv = jnp.concatenate([dq, dk, dv], axis=1)
    g["a_w_qkv"] = _matmul(u, dqkv, ta=True, name=f"{tag}_dwqkv")
    g["a_b_qkv"] = _colsum(dqkv, f"{tag}_dbqkv")[0]
    if plan is not None:
        plan.grads(3 * j, "mix", {"a_w_qkv": g["a_w_qkv"], "a_w_o": g["a_w_o"]})
    dh, dnw = _matmul(dqkv, p["a_w_qkv"][j], tb=True, norm_bwd=(norm[0], norm[1], dh1), name=f"{tag}_du", plan=plan)
    return dh, dnw, g


def _group_cols(gi, qkv):
    W = C_HEADS * HEAD_DIM
    if C_PATTERNS[gi][1] == 1:
        return qkv, (gi * W, (3 + gi) * W, (6 + gi) * W)
    part = jnp.concatenate([qkv[:, (3 * j + gi) * W:(3 * j + gi + 1) * W] for j in range(3)], axis=1)
    return part, (0, W, 2 * W)


def _dil_fwd(u, h, p, B, S, cos, sin, nw, plan=None):
    W = C_HEADS * HEAD_DIM
    qkv = _matmul(u, p["c_w_qkv"][0], out_dtype=BF16, rope=(cos, sin), rope_cols=6 * W, name="c_qkv", plan=plan)
    os_, lses, parts = [], [], []
    for gi, (window, dil) in enumerate(C_PATTERNS):
        part, (qc, kc, vc) = _group_cols(gi, qkv)
        o, lse = _attn_fwd(part, B, S, dil, n_heads=C_HEADS, n_kv=C_HEADS, q_col=qc, k_col=kc, v_col=vc,
                           max_dist=window // dil, sinks=None, name=f"c_attn{gi}", plan=plan)
        os_.append(o)
        lses.append(lse)
        parts.append((part, (qc, kc, vc)))
    o, lse = _merge(os_, lses)
    h1, u2 = _matmul(o, p["c_w_o"][0], resid=h, norm_out=nw, name="c_o")
    return h1, u2, (parts, o, lse)


def _dil_bwd(dh1, u, saved, p, B, S, cos, sin, norm, plan=None):
    parts, o, lse = saved
    g = {}
    do = _matmul(dh1, p["c_w_o"][0], tb=True, name="c_do")
    g["c_w_o"] = _matmul(o, dh1, ta=True, name="c_dwo")[None]
    delta, dob = _delta(do, o, name="c_delta")
    dqs, dks, dvs = [], [], []
    for gi, (window, dil) in enumerate(C_PATTERNS):
        part, (qc, kc, vc) = parts[gi]
        kw = dict(n_heads=C_HEADS, n_kv=C_HEADS, q_col=qc, k_col=kc, v_col=vc, max_dist=window // dil)
        dqs.append(_attn_dq(part, dob, lse, delta, cos, sin, B, S, dil, name=f"c_dq{gi}", **kw))
        dk, dv = _attn_dkv(part, dob, lse, delta, cos, sin, B, S, dil, name=f"c_dkv{gi}", **kw)
        dks.append(dk)
        dvs.append(dv)
    dqkv = jnp.concatenate(dqs + dks + dvs, axis=1)
    g["c_w_qkv"] = _matmul(u, dqkv, ta=True, name="c_dwqkv", plan=plan)[None]
    if plan is not None:
        plan.grads(2, "mix", {"c_w_qkv": g["c_w_qkv"][0], "c_w_o": g["c_w_o"][0]})
    dh, dnw = _matmul(dqkv, p["c_w_qkv"][0], tb=True, norm_bwd=(norm[0], norm[1], dh1), name="c_du", plan=plan)
    return dh, dnw, g


def _ssm_params(p):
    par = jnp.stack([p["b_dt_bias"][0], p["b_a_log"][0], p["b_d"][0]], axis=0)
    prow = par.reshape(3, SSM_N_GROUPS, SSM_HG).transpose(1, 0, 2)
    return prow, prow.transpose(0, 2, 1)


def _mamba_fwd(u, h, p, B, S, nw, plan=None):
    T = B * S
    G, HG = SSM_N_GROUPS, SSM_HG
    w_in = p["b_in_w"][0]
    nzx = SSM_D_INNER + SSM_CONV_DIM
    w_dt = jnp.pad(w_in[:, nzx:], ((0, 0), (0, LANES - SSM_N_HEADS)))
    zx = _matmul(u, w_in[:, :nzx], name="b_zx", plan=plan)
    dtraw = _matmul(u, w_dt, name="b_dt")[:, :SSM_N_HEADS]
    dtc = dtraw.reshape(B, S, G, HG).transpose(0, 2, 1, 3)
    dtr = dtraw.reshape(B, S, G, HG).transpose(0, 2, 3, 1)
    prow, pcol = _ssm_params(p)
    zx3 = zx.reshape(B, S, nzx)
    xc3 = _conv_fwd(zx3, p["b_conv_w"][0], p["b_conv_b"])
    y3, states = _ssd_fwd(xc3, dtc, dtr, prow, pcol, plan=plan)
    y = y3.reshape(T, SSM_D_INNER)
    gn = _gate_fwd(y, zx, p["b_norm_w"])
    h1, u2 = _matmul(gn, p["b_out_w"][0], resid=h, norm_out=nw, name="b_out", plan=plan)
    return h1, u2, (zx, dtc, dtr, xc3, y, states, gn, w_dt)


def _mamba_bwd(dh1, u, saved, p, B, S, norm, plan=None):
    T = B * S
    zx, dtc, dtr, xc3, y, states, gn, w_dt = saved
    nzx = SSM_D_INNER + SSM_CONV_DIM
    w_in = p["b_in_w"][0]
    prow, pcol = _ssm_params(p)
    g = {}
    dgn = _matmul(dh1, p["b_out_w"][0], tb=True, name="b_dgn")
    g["b_out_w"] = _matmul(gn, dh1, ta=True, name="b_dwout")[None]
    dy, dz, dnw = _gate_bwd(dgn, y, zx, p["b_norm_w"])
    g["b_norm_w"] = dnw
    dx3, dB3, dC3, ddt, dpar = _ssd_bwd(xc3, dtc, dtr, prow, pcol, states, dy.reshape(B, S, SSM_D_INNER), plan=plan)
    dpar = dpar.transpose(1, 0, 2).reshape(3, SSM_N_HEADS)
    g["b_dt_bias"], g["b_a_log"], g["b_d"] = dpar[0:1], dpar[1:2], dpar[2:3]
    zx3 = zx.reshape(B, S, nzx)
    cw, cb = p["b_conv_w"][0], p["b_conv_b"]
    parts, dws, dbs = [], [], []
    for col0, dpart, nm in ((0, dx3, "b_conv_bwd_x"), (SSM_D_INNER, dB3, "b_conv_bwd_b"),
                            (SSM_D_INNER + SSM_BC_DIM, dC3, "b_conv_bwd_c")):
        dxp, dw, db = _conv_bwd(zx3, dpart, cw, cb, col0, nm)
        parts.append(dxp.reshape(T, -1))
        dws.append(dw)
        dbs.append(db)
    g["b_conv_w"] = jnp.concatenate(dws, axis=1)[None]
    g["b_conv_b"] = jnp.concatenate(dbs, axis=1)
    dzx = jnp.concatenate([dz] + parts, axis=1)
    ddtraw = ddt.transpose(0, 2, 1, 3).reshape(T, SSM_N_HEADS)
    ddtp = jnp.pad(ddtraw, ((0, 0), (0, LANES - SSM_N_HEADS)))
    dw_zx = _matmul(u, dzx, ta=True, name="b_dwzx")
    dw_dt = _matmul(u, ddtp, ta=True, name="b_dwdt")[:, :SSM_N_HEADS]
    g["b_in_w"] = jnp.concatenate([dw_zx, dw_dt], axis=1)[None]
    if plan is not None:
        plan.grads(1, "mix", {"b_in_w": g["b_in_w"][0], "b_out_w": g["b_out_w"][0]})
    du = _matmul(dzx, w_in[:, :nzx], tb=True, name="b_du_zx", plan=plan)
    dh, dnw = _matmul(ddtp, w_dt, tb=True, resid=du, norm_bwd=(norm[0], norm[1], dh1), name="b_du_dt")
    return dh, dnw, g


def _local_step(x, positions, p, target, plan=None):
    B, S, D = x.shape
    T = B * S
    cos, sin = _rope_tables(positions)
    h = x.reshape(T, D)
    tape = []
    u = _rmsnorm_fwd(h, p["norm_mix_w"][0], "l0_norm_mix")
    for i in range(DEPTH):
        kind, j = i % 3, i // 3
        nw = p["norm_mlp_w"][i]
        if kind == 0:
            h1, u2, saved = _swa_fwd(u, h, p, j, B, S, cos, sin, f"a{j}", nw, plan)
        elif kind == 1:
            h1, u2, saved = _mamba_fwd(u, h, p, B, S, nw, plan)
        else:
            h1, u2, saved = _dil_fwd(u, h, p, B, S, cos, sin, nw, plan)
        r, s = _matmul(u2, p["mlp_w_up"][i], out_dtype=BF16, relu2=True, name=f"l{i}_up", plan=plan)
        if i + 1 < DEPTH:
            h2, u_next = _matmul(s, p["mlp_w_down"][i], resid=h1, norm_out=p["norm_mix_w"][i + 1], name=f"l{i}_down", plan=plan)
        else:
            h2, u_next = _matmul(s, p["mlp_w_down"][i], resid=h1, name=f"l{i}_down", plan=plan), None
        tape.append((h, u, saved, h1, u2, r, s))
        h, u = h2, u_next
    dh, dwf, loss = _final_loss(h, target.reshape(T, D), p["final_norm_w"])
    grads = {"final_norm_w": dwf[0]}
    per_layer = {n: [None] * DEPTH for n in ("norm_mix_w", "norm_mlp_w", "mlp_w_up", "mlp_w_down")}
    a_grads = [None, None]
    for i in reversed(range(DEPTH)):
        kind, j = i % 3, i // 3
        h0, u, saved, h1, u2, r, s = tape[i]
        da = _matmul(dh, p["mlp_w_down"][i], tb=True, out_dtype=BF16, mul=r, mul_scale=2.0, name=f"l{i}_da", plan=plan)
        per_layer["mlp_w_down"][i] = _matmul(s, dh, ta=True, name=f"l{i}_dwdown")
        per_layer["mlp_w_up"][i] = _matmul(u2, da, ta=True, name=f"l{i}_dwup")
        if plan is not None:
            plan.grads(i, "mlp", {"mlp_w_up": per_layer["mlp_w_up"][i], "mlp_w_down": per_layer["mlp_w_down"][i]})
        dh1, dnw = _matmul(da, p["mlp_w_up"][i], tb=True, norm_bwd=(h1, p["norm_mlp_w"][i], dh), name=f"l{i}_du2", plan=plan)
        per_layer["norm_mlp_w"][i] = dnw[0]
        norm = (h0, p["norm_mix_w"][i])
        if kind == 0:
            dh, dnw, g = _swa_bwd(dh1, u, saved, p, j, B, S, cos, sin, f"a{j}", norm, plan)
            a_grads[j] = g
        elif kind == 1:
            dh, dnw, g = _mamba_bwd(dh1, u, saved, p, B, S, norm, plan)
            grads.update(g)
        else:
            dh, dnw, g = _dil_bwd(dh1, u, saved, p, B, S, cos, sin, norm, plan)
            grads.update(g)
        per_layer["norm_mix_w"][i] = dnw[0]
    for n in ("norm_mix_w", "norm_mlp_w"):
        grads[n] = jnp.stack(per_layer[n], axis=0)
    for n in ("mlp_w_up", "mlp_w_down"):
        grads[n] = per_layer[n]
    for n in ("a_b_qkv", "a_sinks", "a_b_o"):
        grads[n] = jnp.stack([a_grads[0][n], a_grads[1][n]], axis=0)
    for n in ("a_w_qkv", "a_w_o"):
        grads[n] = [a_grads[0][n], a_grads[1][n]]
    for n in ("b_in_w", "b_out_w", "c_w_qkv", "c_w_o"):
        grads[n] = [grads[n][0]]
    return loss, dh.reshape(B, S, D), grads


MIX = {0: ("a_w_qkv", "a_w_o"), 1: ("b_in_w", "b_out_w"), 2: ("c_w_qkv", "c_w_o")}
MLP = ("mlp_w_up", "mlp_w_down")
GATHER_FIRST = (0, MIX[0])
GATHER_HOSTS = {"a0_qkv": ((0, ("mlp_w_up",)),), "a0_attn": ((0, ("mlp_w_down",)),), "l0_up": ((1, ("b_in_w",)),),
                "l0_down": ((1, ("b_out_w",)),), "b_zx": ((1, ("mlp_w_up",)),),
                "b_ssd_fwd": ((1, ("mlp_w_down",)), (2, ("c_w_qkv",))), "b_out": ((2, ("c_w_o",)),),
                "l1_up": ((2, ("mlp_w_up",)),), "l1_down": ((2, ("mlp_w_down",)),),
                "c_qkv": ((3, MLP),), "c_attn1": ((3, MIX[0]),)}
GATHER_LONG_HOSTS = ("b_ssd_fwd",)
REDUCE_HOSTS = {(3, "mlp"): ("l3_du2", "a1_dkv"), (3, "mix"): ("a1_du", "l2_da"),
                (2, "mlp"): ("l2_du2", "c_dwqkv"), (2, "mix"): ("c_du", "b_ssd_bwd"),
                (1, "mlp"): ("l1_du2", "b_ssd_bwd"), (1, "mix"): ("b_du_zx", {"b_in_w": "a0_dq", "b_out_w": "l0_da"}),
                (0, "mlp"): ("l0_du2", "a0_dkv"), (0, "mix"): (None, None)}


class _Plan:
    def __init__(self, w, m, v, p, dev, chip, core):
        self.w, self.m, self.v, self.p, self.dev, self.chip, self.core = w, m, v, p, dev, chip, core
        self.pending = {}
        self.res = {n: None for n in BIG_KIND}
        self._install(*GATHER_FIRST)(_gather(self._gather_items(*GATHER_FIRST), "gather_first"))
        for host, groups in GATHER_HOSTS.items():
            for i, only in groups:
                mid = 0.7 if host in GATHER_LONG_HOSTS else 0.9
                self._wait_for(host, _gather_comm(self._gather_items(i, only), mid), self._install(i, only))

    def _wait_for(self, host, comm, done):
        self.pending.setdefault(host, []).append((comm, done))

    def _names(self, i, only):
        return [(n, l) for n, l in _layer_big(i) if only is None or n in only]

    def _gather_items(self, i, only):
        items = []
        for n, l in self._names(i, only):
            kind, s2 = BIG_KIND[n], self.w[n].shape[1:]
            placed = _place(self.w[n], l, kind, _full2d(kind, s2), self.dev, f"place_l{i}_{n}")
            items.append((placed, kind, _block_size(kind, s2), _full2d(kind, s2), True))
        return items

    def _install(self, i, only):
        def done(fulls):
            for (n, l), t in zip(self._names(i, only), fulls):
                self.p[n][l] = _from_slots(t, 1) if BIG_KIND[n] == "slot" else t
        return done

    def take(self, host):
        return _Comm.merge([c for c, _ in self.pending[host]]) if host in self.pending else None

    def give(self, host, results):
        for comm, done in self.pending.pop(host):
            done(results[:len(comm.out_shapes)])
            results = results[len(comm.out_shapes):]

    def grads(self, i, group, grads):
        names = self._names(i, MLP if group == "mlp" else MIX[i % 3])
        items = []
        for n, _ in names:
            kind, s2 = BIG_KIND[n], self.w[n].shape[1:]
            items.append((_to_slots(grads[n], 1) if kind == "slot" else grads[n], kind, _block_size(kind, s2), s2))
        d2d_host, ici_host = REDUCE_HOSTS[(i, group)]
        tag = f"l{i}_{group}"

        def update(sel, parts):
            def done(recv):
                for (n, l), pt, r in zip(sel, parts, recv):
                    self.res[n] = _adamw(pt, r, self.w[n], self.m[n], self.v[n], l, self.res[n], self.chip, f"adamw_l{i}_{n}")
            return done

        def second(sib):
            parts = [_pair_sum(it[0], s, it[1], self.core, f"pair_sum_l{i}_{n}") for (n, _), it, s in zip(names, items, sib)]
            hosts = ici_host if isinstance(ici_host, dict) else {n: ici_host for n, _ in names}
            for h in dict.fromkeys(hosts[n] for n, _ in names):
                ks = [k for k, (n, _) in enumerate(names) if hosts[n] == h]
                sel, pts = [names[k] for k in ks], [parts[k] for k in ks]
                self._send(h, _reduce_ici_comm(pts), update(sel, pts), f"reduce_ici_{tag}_{sel[0][0]}")

        self._send(d2d_host, _reduce_d2d_comm(items), second, f"reduce_d2d_{tag}")

    def _send(self, host, comm, done, name):
        if host is None:
            done(_run_comm(comm, name))
        else:
            self._wait_for(host, comm, done)

    def flush(self):
        late = 0
        while self.pending:
            host = next(iter(self.pending))
            for comm, done in self.pending.pop(host):
                done(_run_comm(comm, f"late_{late}_{host}"))
                late += 1


def kernel(x, positions, norm_mix_w, norm_mlp_w, a_w_qkv, a_b_qkv, a_sinks, a_w_o, a_b_o, b_in_w, b_conv_w, b_conv_b, b_dt_bias, b_a_log, b_d, b_norm_w, b_out_w, c_w_qkv, c_w_o, mlp_w_up, mlp_w_down, final_norm_w, loss_target, m_norm_mix_w, m_norm_mlp_w, m_a_w_qkv, m_a_b_qkv, m_a_sinks, m_a_w_o, m_a_b_o, m_b_in_w, m_b_conv_w, m_b_conv_b, m_b_dt_bias, m_b_a_log, m_b_d, m_b_norm_w, m_b_out_w, m_c_w_qkv, m_c_w_o, m_mlp_w_up, m_mlp_w_down, m_final_norm_w, v_norm_mix_w, v_norm_mlp_w, v_a_w_qkv, v_a_b_qkv, v_a_sinks, v_a_w_o, v_a_b_o, v_b_in_w, v_b_conv_w, v_b_conv_b, v_b_dt_bias, v_b_a_log, v_b_d, v_b_norm_w, v_b_out_w, v_c_w_qkv, v_c_w_o, v_mlp_w_up, v_mlp_w_down, v_final_norm_w):
    w = dict(zip(W_NAMES, (norm_mix_w, norm_mlp_w, a_w_qkv, a_b_qkv, a_sinks, a_w_o, a_b_o, b_in_w, b_conv_w, b_conv_b,
                           b_dt_bias, b_a_log, b_d, b_norm_w, b_out_w, c_w_qkv, c_w_o, mlp_w_up, mlp_w_down, final_norm_w)))
    m = dict(zip(W_NAMES, (m_norm_mix_w, m_norm_mlp_w, m_a_w_qkv, m_a_b_qkv, m_a_sinks, m_a_w_o, m_a_b_o, m_b_in_w,
                           m_b_conv_w, m_b_conv_b, m_b_dt_bias, m_b_a_log, m_b_d, m_b_norm_w, m_b_out_w, m_c_w_qkv, m_c_w_o,
                           m_mlp_w_up, m_mlp_w_down, m_final_norm_w)))
    v = dict(zip(W_NAMES, (v_norm_mix_w, v_norm_mlp_w, v_a_w_qkv, v_a_b_qkv, v_a_sinks, v_a_w_o, v_a_b_o, v_b_in_w,
                           v_b_conv_w, v_b_conv_b, v_b_dt_bias, v_b_a_log, v_b_d, v_b_norm_w, v_b_out_w, v_c_w_qkv, v_c_w_o,
                           v_mlp_w_up, v_mlp_w_down, v_final_norm_w)))
    px, py, pc = lax.axis_index("x"), lax.axis_index("y"), lax.axis_index("c")
    me = 4 * px + 2 * py + pc
    dev, chip, core = (t.astype(jnp.int32).reshape(1) for t in (me, 2 * px + py, pc))

    trio = tuple(SMALL_SHARDED)
    got = _gather([(d[n], "slot", None, (N_DEV,) + d[n].shape, False) for n in trio for d in (w, m, v)], "gather_small")
    slots = {n: got[3 * i:3 * i + 3] for i, n in enumerate(trio)}
    p = {n: w[n] for n in SMALL_REPLICATED}
    for n in trio:
        p[n] = _from_slots(slots[n][0], SMALL_SHARDED[n])
    for n in BIG_KIND:
        p[n] = [None] * w[n].shape[0]
    plan = _Plan(w, m, v, p, dev, chip, core)
    loss_part, dx, grads = _local_step(x, positions, p, loss_target, plan)
    loss = lax.psum(loss_part[0, 0], AXES)
    plan.flush()
    out = {n: list(plan.res[n]) for n in BIG_KIND}

    small = SMALL_REPLICATED + trio
    as2d = lambda t: t.reshape(1, -1) if t.ndim == 1 else t
    g_sm = [as2d(grads[n]) for n in SMALL_REPLICATED] + [_to_slots(grads[n].reshape(p[n].shape), SMALL_SHARDED[n]) for n in trio]
    gathered = _gather([(g, "slot", None, (N_DEV,) + g.shape, False) for g in g_sm], "gather_small_grads")
    ws = [as2d(w[n]) for n in SMALL_REPLICATED] + [slots[n][0] for n in trio]
    ms = [as2d(m[n]) for n in SMALL_REPLICATED] + [slots[n][1] for n in trio]
    vs = [as2d(v[n]) for n in SMALL_REPLICATED] + [slots[n][2] for n in trio]
    sm_out = _small_adamw(gathered, ws, ms, vs)
    for i, n in enumerate(small):
        if n in SMALL_SHARDED:
            out[n] = [lax.dynamic_index_in_dim(sm_out[k][i], me, 0, keepdims=False) for k in range(4)]
        else:
            out[n] = [sm_out[k][i].reshape(w[n].shape) for k in range(4)]
    return (loss, dx, *[out[n][0] for n in W_NAMES], *[out[n][1] for n in W_NAMES], *[out[n][2] for n in W_NAMES],
            *[out[n][3] for n in W_NAMES])
```
